```python
import math
import jax, jax.numpy as jnp
from jax import lax
import numpy as np

D_MODEL = 1024
BATCH = 8
SEQ = 2048
DEPTH = 1
DEC_BATCH = 128
DEC_SEQ = 4
PAST_LEN = 16384
PAGE_SIZE = 128

DN_HEADS = 8
DN_HEAD_DIM = 128
DN_WIDTH = DN_HEADS * DN_HEAD_DIM
QKV_WIDTH = 3 * DN_WIDTH
CONV_WIDTH = 4
DN_CHUNK = 64
POOL_WINDOWS = (2, 4, 8, 16)
POOL_GROUPS = len(POOL_WINDOWS)
POOL_GROUP_DIM = 128
POOL_WIDTH = POOL_GROUPS * POOL_GROUP_DIM
POOL_OUT_GROUP = D_MODEL // POOL_GROUPS
POOL_BUF = max(POOL_WINDOWS) - 1
IN_WIDTH = QKV_WIDTH + 2 * DN_HEADS + DN_WIDTH + POOL_WIDTH + 2 * D_MODEL
PEER_HEADS = 8
PEER_NKEYS = 128
PEER_EXPERTS = PEER_NKEYS * PEER_NKEYS
PEER_TOPK = 16
PEER_KEY_HALF = 128
PEER_TOKEN_BLOCK = 512
EPS = 1e-6

kernel_name = 'hybrid_deltanet_pool_peer_adaln_step'


def rmsnorm(x, g):
    xf = x.astype(jnp.float32)
    y = xf * lax.rsqrt(jnp.mean(xf * xf, axis=-1, keepdims=True) + EPS)
    return (y * g.astype(jnp.float32)).astype(x.dtype)


def l2norm(x):
    return x * lax.rsqrt(jnp.sum(x * x, axis=-1, keepdims=True) + EPS)


def short_conv(x, buf, w):
    L = x.shape[1]
    xp = jnp.concatenate([buf.astype(x.dtype), x], axis=1)
    y = sum(xp[:, k:k + L] * w[k] for k in range(CONV_WIDTH))
    return jax.nn.silu(y), xp[:, L:]


def gated_delta_rule(q, k, v, g, beta, s0):
    f32 = jnp.float32
    B, L, H, Dk = q.shape
    Dv = v.shape[-1]
    C = L if L <= DN_CHUNK else math.gcd(L, DN_CHUNK)
    N = L // C

    def to_chunks(t):
        t = jnp.moveaxis(t.astype(f32), 2, 1)
        return t.reshape((B, H, N, C) + t.shape[3:])

    q, k, v, g, beta = [to_chunks(t) for t in (q, k, v, g, beta)]
    q = q * (Dk ** -0.5)
    g = jnp.cumsum(g, axis=-1)
    k_beta = k * beta[..., None]
    v_beta = v * beta[..., None]
    idx = jnp.arange(C)
    causal = idx[:, None] >= idx[None, :]
    strict = idx[:, None] > idx[None, :]
    decay = jnp.exp(jnp.where(causal, g[..., :, None] - g[..., None, :], -jnp.inf))
    lower = jnp.einsum('bhnid,bhnjd->bhnij', k_beta, k) * decay * strict
    a_mat = lower + jnp.eye(C, dtype=f32)
    rhs = jnp.concatenate([v_beta, k_beta * jnp.exp(g)[..., None]], axis=-1)
    sol = lax.linalg.triangular_solve(a_mat, rhs, left_side=True, lower=True, unit_diagonal=True)
    u, w = sol[..., :Dv], sol[..., Dv:]
    qk = jnp.einsum('bhnid,bhnjd->bhnij', q, k) * decay
    g_last = g[..., -1]
    k_tail = k * jnp.exp(g_last[..., None] - g)[..., None]
    xs = tuple(jnp.moveaxis(t, 2, 0) for t in (q, qk, u, w, g, g_last, k_tail))

    def step(S, inp):
        q_c, qk_c, u_c, w_c, g_c, gl_c, kt_c = inp
        v_new = u_c - jnp.einsum('bhck,bhkv->bhcv', w_c, S)
        o = (jnp.einsum('bhck,bhkv->bhcv', q_c * jnp.exp(g_c)[..., None], S)
             + jnp.einsum('bhij,bhjv->bhiv', qk_c, v_new))
        S = S * jnp.exp(gl_c)[..., None, None] + jnp.einsum('bhck,bhcv->bhkv', kt_c, v_new)
        return S, o

    S, o = lax.scan(step, s0.astype(f32), xs)
    o = jnp.moveaxis(o, 0, 2).reshape(B, H, L, Dv)
    return jnp.moveaxis(o, 1, 2), S


def deltanet_branch(qkv_raw, b_raw, a_raw, z, conv_buf, s0, conv_w, a_log, dt_bias, dn_norm_g):
    f32 = jnp.float32
    B, L, _ = qkv_raw.shape
    qkv, new_conv = short_conv(qkv_raw, conv_buf, conv_w)
    qkv = qkv.astype(f32).reshape(B, L, 3, DN_HEADS, DN_HEAD_DIM)
    q = l2norm(qkv[:, :, 0])
    k = l2norm(qkv[:, :, 1])
    v = qkv[:, :, 2]
    beta = jax.nn.sigmoid(b_raw.astype(f32))
    g = -jnp.exp(a_log.astype(f32)) * jax.nn.softplus(a_raw.astype(f32) + dt_bias.astype(f32))
    o, s_new = gated_delta_rule(q, k, v, g, beta, s0)
    zf = z.astype(f32).reshape(B, L, DN_HEADS, DN_HEAD_DIM)
    o = (o * lax.rsqrt(jnp.mean(o * o, axis=-1, keepdims=True) + EPS)
         * dn_norm_g.astype(f32) * jax.nn.silu(zf))
    return o.reshape(B, L, DN_WIDTH).astype(qkv_raw.dtype), new_conv, s_new.astype(s0.dtype)


def pool_branch(p_in, buf, start, w_pool, pool_scale):
    f32 = jnp.float32
    B, L, _ = p_in.shape
    xf = p_in.astype(f32)
    seq = jnp.concatenate([buf.astype(f32), xf], axis=1)
    cs = jnp.concatenate([jnp.zeros((B, 1, POOL_WIDTH), f32), jnp.cumsum(seq, axis=1)], axis=1)
    top = cs[:, POOL_BUF + 1:POOL_BUF + 1 + L]
    pos = start + jnp.arange(L)
    means = []
    for gi, win in enumerate(POOL_WINDOWS):
        lo, hi = gi * POOL_GROUP_DIM, (gi + 1) * POOL_GROUP_DIM
        s = top[..., lo:hi] - cs[:, POOL_BUF + 1 - win:POOL_BUF + 1 - win + L, lo:hi]
        cnt = jnp.minimum(pos + 1, win).astype(f32)
        means.append(s / cnt[None, :, None])
    pooled = (jnp.concatenate(means, axis=-1) - xf).reshape(B, L, POOL_GROUPS, POOL_GROUP_DIM)
    y = jnp.einsum('blgc,gco->blgo', pooled, w_pool.astype(f32)).reshape(B, L, D_MODEL)
    y = y * pool_scale.astype(f32)
    return y.astype(p_in.dtype), seq[:, -POOL_BUF:].astype(buf.dtype)


def peer(h, w_query, sub_keys, expert_u, expert_v):
    f32 = jnp.float32
    B, L, D = h.shape
    T = B * L
    x = h.reshape(T, D)
    q = (x @ w_query).astype(f32).reshape(T, PEER_HEADS, 2, PEER_KEY_HALF)
    scores = jnp.einsum('thpd,hpnd->thpn', q, sub_keys.astype(f32))
    s_top, i_top = lax.top_k(scores, PEER_TOPK)
    cand = s_top[:, :, 0, :, None] + s_top[:, :, 1, None, :]
    cand_idx = i_top[:, :, 0, :, None] * PEER_NKEYS + i_top[:, :, 1, None, :]
    best, sel = lax.top_k(cand.reshape(T, PEER_HEADS, PEER_TOPK * PEER_TOPK), PEER_TOPK)
    idx = jnp.take_along_axis(cand_idx.reshape(T, PEER_HEADS, PEER_TOPK * PEER_TOPK), sel, axis=-1)
    gates = jax.nn.softmax(best, axis=-1)
    HK = PEER_HEADS * PEER_TOPK
    nb = T if T <= PEER_TOKEN_BLOCK else math.gcd(T, PEER_TOKEN_BLOCK)

    def token_block(args):
        xb, ib, gb = args
        act = jax.nn.gelu(jnp.einsum('td,tkd->tk', xb, expert_u[ib]).astype(f32), approximate=False)
        coef = (gb * act).astype(xb.dtype)
        return jnp.einsum('tk,tkd->td', coef, expert_v[ib])

    out = lax.map(token_block, (x.reshape(T // nb, nb, D),
                                idx.reshape(T // nb, nb, HK),
                                gates.reshape(T // nb, nb, HK)))
    return out.reshape(B, L, D).astype(h.dtype)


def trunk_layer(x, c, conv_buf, s0, pool_buf, start,
                w_ada, b_ada, norm1_g, w_in, conv_w, a_log, dt_bias, dn_norm_g,
                w_pool, pool_scale, w_out, norm2_g, w_query, sub_keys, expert_u, expert_v):
    B = x.shape[0]
    mod = (jax.nn.silu(c) @ w_ada + b_ada).reshape(B, 6, D_MODEL)[:, :, None, :]
    sh1, sc1, g1, sh2, sc2, g2 = [mod[:, i] for i in range(6)]
    h = rmsnorm(x, norm1_g) * (1 + sc1) + sh1
    proj = h @ w_in
    widths = (QKV_WIDTH, DN_HEADS, DN_HEADS, DN_WIDTH, POOL_WIDTH, D_MODEL, D_MODEL)
    offs = [int(o) for o in np.cumsum(widths)[:-1]]
    qkv_raw, b_raw, a_raw, z, pool_in, gate_a, gate_b = jnp.split(proj, offs, axis=-1)
    o_dn, new_conv, new_s = deltanet_branch(qkv_raw, b_raw, a_raw, z, conv_buf, s0,
                                            conv_w, a_log, dt_bias, dn_norm_g)
    o_pool, new_pool = pool_branch(pool_in, pool_buf, start, w_pool, pool_scale)
    mixed = jax.nn.sigmoid(gate_a) * o_dn + jax.nn.sigmoid(gate_b) * o_pool
    x = x + g1 * (mixed @ w_out)
    h2 = rmsnorm(x, norm2_g) * (1 + sc2) + sh2
    x = x + g2 * peer(h2, w_query, sub_keys, expert_u, expert_v)
    return x, new_conv, new_s, new_pool


def setup_inputs(seed: int = 0) -> dict:
    key = jax.random.key(seed)
    ks = jax.random.split(key, 24)
    f32 = jnp.float32
    D = D_MODEL

    def nrm(k, shape, s):
        return jax.random.normal(k, shape, f32) * s

    return {
        'x_prompt': nrm(ks[0], (BATCH, SEQ, D), 1.0),
        'x_sample': nrm(ks[1], (DEC_BATCH, DEC_SEQ, D), 1.0),
        'c_prompt': nrm(ks[2], (BATCH, D), 1.0),
        'c_sample': nrm(ks[3], (DEC_BATCH, D), 1.0),
        'state_conv': nrm(ks[4], (DEPTH, DEC_BATCH, CONV_WIDTH - 1, QKV_WIDTH), 1.0),
        'state_delta': nrm(ks[5], (DEPTH, DEC_BATCH, DN_HEADS, DN_HEAD_DIM, DN_HEAD_DIM), 0.05),
        'state_pool': nrm(ks[6], (DEPTH, DEC_BATCH, POOL_BUF, POOL_WIDTH), 1.0),
        'w_ada': nrm(ks[7], (DEPTH, D, 6 * D), 0.5 * D ** -0.5),
        'b_ada': nrm(ks[8], (DEPTH, 6 * D), 0.02),
        'norm1_g': 1.0 + nrm(ks[9], (DEPTH, D), 0.02),
        'w_in': nrm(ks[10], (DEPTH, D, IN_WIDTH), D ** -0.5),
        'conv_w': nrm(ks[11], (DEPTH, CONV_WIDTH, QKV_WIDTH), CONV_WIDTH ** -0.5),
        'a_log': jnp.log(jax.random.uniform(ks[12], (DEPTH, DN_HEADS), f32, 1.0, 8.0)),
        'dt_bias': nrm(ks[13], (DEPTH, DN_HEADS), 0.1),
        'dn_norm_g': 1.0 + nrm(ks[14], (DEPTH, DN_HEAD_DIM), 0.02),
        'w_pool': nrm(ks[15], (DEPTH, POOL_GROUPS, POOL_GROUP_DIM, POOL_OUT_GROUP), POOL_GROUP_DIM ** -0.5),
        'pool_scale': 1.0 + nrm(ks[16], (DEPTH, D), 0.1),
        'w_out': nrm(ks[17], (DEPTH, D, D), D ** -0.5),
        'norm2_g': 1.0 + nrm(ks[18], (DEPTH, D), 0.02),
        'w_query': nrm(ks[19], (DEPTH, D, PEER_HEADS * 2 * PEER_KEY_HALF), D ** -0.5),
        'sub_keys': nrm(ks[20], (DEPTH, PEER_HEADS, 2, PEER_NKEYS, PEER_KEY_HALF), PEER_KEY_HALF ** -0.5),
        'expert_u': nrm(ks[21], (DEPTH, PEER_EXPERTS, D), D ** -0.5),
        'expert_v': nrm(ks[22], (DEPTH, PEER_EXPERTS, D), 0.5),
        'final_norm_g': 1.0 + nrm(ks[23], (D,), 0.02),
    }


def reference(x_prompt, x_sample, c_prompt, c_sample, state_conv, state_delta, state_pool,
              w_ada, b_ada, norm1_g, w_in, conv_w, a_log, dt_bias, dn_norm_g,
              w_pool, pool_scale, w_out, norm2_g, w_query, sub_keys, expert_u, expert_v,
              final_norm_g):
    bp = x_prompt.shape[0]
    dt = x_prompt.dtype
    zero_conv = jnp.zeros((bp, CONV_WIDTH - 1, QKV_WIDTH), dt)
    zero_delta = jnp.zeros((bp, DN_HEADS, DN_HEAD_DIM, DN_HEAD_DIM), state_delta.dtype)
    zero_pool = jnp.zeros((bp, POOL_BUF, POOL_WIDTH), dt)
    yp, ys = x_prompt, x_sample
    conv_p, delta_p, pool_p, conv_s, delta_s, pool_s = [], [], [], [], [], []
    for layer in range(DEPTH):
        wl = (w_ada[layer], b_ada[layer], norm1_g[layer], w_in[layer], conv_w[layer],
              a_log[layer], dt_bias[layer], dn_norm_g[layer], w_pool[layer], pool_scale[layer],
              w_out[layer], norm2_g[layer], w_query[layer], sub_keys[layer],
              expert_u[layer], expert_v[layer])
        yp, cp, sp, pp = trunk_layer(yp, c_prompt, zero_conv, zero_delta, zero_pool, 0, *wl)
        ys, cs, ss, ps = trunk_layer(ys, c_sample, state_conv[layer], state_delta[layer],
                                     state_pool[layer], PAST_LEN, *wl)
        conv_p.append(cp)
        delta_p.append(sp)
        pool_p.append(pp)
        conv_s.append(cs)
        delta_s.append(ss)
        pool_s.append(ps)
    y_prompt = rmsnorm(yp, final_norm_g)
    y_sample = rmsnorm(ys, final_norm_g)
    return (y_prompt, y_sample,
            jnp.stack(conv_p), jnp.stack(delta_p), jnp.stack(pool_p),
            jnp.stack(conv_s), jnp.stack(delta_s), jnp.stack(pool_s))
```

```python
import functools

import jax
import jax.numpy as jnp
from jax import lax
from jax.experimental import pallas as pl
from jax.experimental.pallas import tpu as pltpu

F32 = jnp.float32
BF16 = jnp.bfloat16
I32 = jnp.int32

D_MODEL = 1024
DEPTH = 1
PAST_LEN = 16384
DN_HEADS = 8
DN_HEAD_DIM = 128
DN_WIDTH = DN_HEADS * DN_HEAD_DIM
QKV_WIDTH = 3 * DN_WIDTH
CONV_WIDTH = 4
DN_CHUNK = 64
POOL_WINDOWS = (2, 4, 8, 16)
POOL_GROUP_DIM = 128
POOL_WIDTH = len(POOL_WINDOWS) * POOL_GROUP_DIM
POOL_OUT_GROUP = D_MODEL // len(POOL_WINDOWS)
POOL_BUF = max(POOL_WINDOWS) - 1
PEER_HEADS = 8
PEER_NKEYS = 128
PEER_TOPK = 16
PEER_KEY_HALF = 128
PEER_HK = PEER_HEADS * PEER_TOPK
EPS = 1e-6

LANES = 128
SUBLANES = 8
CONV_PAD = SUBLANES
POOL_PAD = 16
VMEM_LIMIT = 56 * 1024 * 1024

NT_DIMS = (((1,), (1,)), ((), ()))
TN_DIMS = (((0,), (0,)), ((), ()))


def _dot(a, b):
    return jnp.dot(a.astype(BF16), b.astype(BF16), preferred_element_type=F32)


def _dot_nt(a, b):
    return lax.dot_general(a.astype(BF16), b.astype(BF16), NT_DIMS, preferred_element_type=F32)


def _split3(x):
    hi = x.astype(BF16)
    r1 = x - hi.astype(F32)
    mid = r1.astype(BF16)
    lo = (r1 - mid.astype(F32)).astype(BF16)
    return hi, mid, lo


def _silu(x):
    return x * jax.nn.sigmoid(x)


def _gelu(x):
    return 0.5 * x * (1.0 + lax.erf(x * (0.5 ** 0.5)))


def _softplus(x):
    return jnp.maximum(x, 0.0) + jnp.log(1.0 + jnp.exp(-jnp.abs(x)))


def _mod_rows(ref):
    m = ref[...]
    return m.reshape(m.shape[-2], m.shape[-1])


def _mod_spec(k, rows_per_batch, tm):
    if rows_per_batch >= tm:
        tiles = rows_per_batch // tm
        return pl.BlockSpec((1, 1, 1, D_MODEL), lambda i, *_: (k, i // tiles, 0, 0))
    return pl.BlockSpec((1, tm, D_MODEL), lambda i, *_: (k, i, 0))


def _const_spec(shape):
    nd = len(shape)
    return pl.BlockSpec(shape, lambda *_: (0,) * nd)


def _ada_body(c_ref, w_ref, b_ref, o_ref):
    o_ref[...] = _dot(_silu(c_ref[...]), w_ref[...]) + b_ref[...]


def _ada(c, w_ada, b_ada):
    n = c.shape[0]
    return pl.pallas_call(
        _ada_body,
        grid=(6,),
        in_specs=[pl.BlockSpec((n, D_MODEL), lambda j: (0, 0)),
                  pl.BlockSpec((D_MODEL, D_MODEL), lambda j: (0, j)),
                  pl.BlockSpec((1, D_MODEL), lambda j: (0, j))],
        out_specs=pl.BlockSpec((n, D_MODEL), lambda j: (0, j)),
        out_shape=jax.ShapeDtypeStruct((n, 6 * D_MODEL), F32),
        name="ada",
    )(c, w_ada, b_ada.reshape(1, -1))


_IN_BLOCKS = (("qkv", QKV_WIDTH), ("z", DN_WIDTH), ("pool", POOL_WIDTH),
              ("ga", D_MODEL), ("gb", D_MODEL), ("ba", LANES))
_IN_TOTAL = sum(w for _, w in _IN_BLOCKS)
_IN_COL_CHUNK = 512


def _inproj_body(x_ref, sc_ref, sh_ref, g_ref, w_ref, *out_refs):
    x = x_ref[...]
    y = x * lax.rsqrt(jnp.mean(x * x, axis=-1, keepdims=True) + EPS) * g_ref[...]
    h = (y * (1.0 + _mod_rows(sc_ref)) + _mod_rows(sh_ref)).astype(BF16)
    off = 0
    for (_, width), o_ref in zip(_IN_BLOCKS, out_refs):
        for c0 in range(0, width, _IN_COL_CHUNK):
            cw = min(_IN_COL_CHUNK, width - c0)
            o_ref[:, c0:c0 + cw] = jnp.dot(h, w_ref[:, off + c0:off + c0 + cw],
                                           preferred_element_type=F32)
        off += width


def _inproj(x2d, mod, rows_per_batch, norm_g, w_cat, tm):
    t = x2d.shape[0]
    row = lambda w: pl.BlockSpec((tm, w), lambda i: (i, 0))
    return pl.pallas_call(
        _inproj_body,
        grid=(t // tm,),
        in_specs=[row(D_MODEL), _mod_spec(1, rows_per_batch, tm), _mod_spec(0, rows_per_batch, tm),
                  _const_spec((1, D_MODEL)),
                  pl.BlockSpec((D_MODEL, _IN_TOTAL), lambda i: (0, 0), pipeline_mode=pl.Buffered(1))],
        out_specs=[row(w) for _, w in _IN_BLOCKS],
        out_shape=[jax.ShapeDtypeStruct((t, w), F32) for _, w in _IN_BLOCKS],
        compiler_params=pltpu.CompilerParams(vmem_limit_bytes=VMEM_LIMIT),
        name="inproj",
    )(x2d, mod, mod, norm_g.reshape(1, -1), w_cat)


def _mixer_body(C, Lv, start,
                qkv_ref, ba_ref, z_ref, pin_ref, ga_ref, gb_ref, cbuf_ref, s0_ref, pbuf_ref,
                convw_ref, alog_ref, dtb_ref, dng_ref, wpool_ref, pscale_ref,
                mixed_ref, nconv_ref, ns_ref, npool_ref,
                xp_scr, act_scr, s_scr, pp_scr, odn_scr):
    n = pl.program_id(1)
    last = pl.num_programs(1) - 1

    @pl.when(n == 0)
    def _load_state():
        xp_scr[0:CONV_PAD, :] = cbuf_ref[0]
        pp_scr[0:POOL_PAD, :] = pbuf_ref[0]
        s_scr[...] = s0_ref[0]

    xp_scr[CONV_PAD:CONV_PAD + C, :] = qkv_ref[0]
    for c0 in range(0, QKV_WIDTH, 512):
        cs = slice(c0, c0 + 512)
        y = xp_scr[CONV_PAD:CONV_PAD + C, cs] * convw_ref[CONV_WIDTH - 1:CONV_WIDTH, cs]
        for k in range(CONV_WIDTH - 1):
            r0 = CONV_PAD - (CONV_WIDTH - 1) + k
            y = y + xp_scr[r0:r0 + C, cs] * convw_ref[k:k + 1, cs]
        act_scr[:, cs] = _silu(y)

    ba = ba_ref[0]
    lane = lax.broadcasted_iota(I32, (C, LANES), 1)
    beta_all = jax.nn.sigmoid(ba)
    g_all = -jnp.exp(alog_ref[...]) * _softplus(ba + dtb_ref[...])
    if Lv < C:
        valid = lax.broadcasted_iota(I32, (C, LANES), 0) < Lv
        beta_all = jnp.where(valid, beta_all, 0.0)
        g_all = jnp.where(valid, g_all, 0.0)
    ii = lax.broadcasted_iota(I32, (C, C), 0)
    jj = lax.broadcasted_iota(I32, (C, C), 1)
    causal = ii >= jj
    strict = ii > jj
    tril = jnp.where(causal, 1.0, 0.0).astype(BF16)
    eye = jnp.where(ii == jj, 1.0, 0.0)
    gc_all = sum(jnp.dot(tril, part, preferred_element_type=F32) for part in _split3(g_all))
    if C < LANES:
        gc_sq = jnp.concatenate([gc_all, jnp.zeros((LANES - C, LANES), F32)], axis=0)
    else:
        gc_sq = gc_all
    gc_t = gc_sq.T

    for h in range(DN_HEADS):
        hs = slice(h * DN_HEAD_DIM, (h + 1) * DN_HEAD_DIM)
        beta = jnp.sum(jnp.where(lane == h, beta_all, 0.0), axis=1, keepdims=True)
        gcol = jnp.sum(jnp.where(lane == DN_HEADS + h, gc_all, 0.0), axis=1, keepdims=True)
        grow = gc_t[DN_HEADS + h:DN_HEADS + h + 1, 0:C]
        glast = gcol[C - 1:C, :]

        q = act_scr[:, hs]
        k = act_scr[:, DN_WIDTH + h * DN_HEAD_DIM:DN_WIDTH + (h + 1) * DN_HEAD_DIM]
        v = act_scr[:, 2 * DN_WIDTH + h * DN_HEAD_DIM:2 * DN_WIDTH + (h + 1) * DN_HEAD_DIM]
        q = q * lax.rsqrt(jnp.sum(q * q, axis=-1, keepdims=True) + EPS) * (DN_HEAD_DIM ** -0.5)
        k = k * lax.rsqrt(jnp.sum(k * k, axis=-1, keepdims=True) + EPS)
        kb = k * beta
        vb = v * beta

        decay = jnp.where(causal, jnp.exp(jnp.where(causal, gcol - grow, 0.0)), 0.0)
        lower = jnp.where(strict, _dot_nt(kb, k) * decay, 0.0)
        ainv = eye - lower
        pw = lower
        p = 1
        while 2 * p < C:
            pw = _dot(pw, pw)
            ainv = ainv + _dot(ainv, pw)
            p *= 2
        sol = _dot(ainv, jnp.concatenate([vb, kb * jnp.exp(gcol)], axis=1))
        u = sol[:, :DN_HEAD_DIM]
        w = sol[:, DN_HEAD_DIM:]
        qk = _dot_nt(q, k) * decay
        k_tail = k * jnp.exp(glast - gcol)

        S = s_scr[h]
        v_new = u - _dot(w, S)
        o = _dot(q * jnp.exp(gcol), S) + _dot(qk, v_new)
        s_scr[h] = S * jnp.exp(glast) + lax.dot_general(
            k_tail.astype(BF16), v_new.astype(BF16), TN_DIMS, preferred_element_type=F32)

        zf = z_ref[0, :, hs]
        o = o * lax.rsqrt(jnp.mean(o * o, axis=-1, keepdims=True) + EPS) * dng_ref[...] * _silu(zf)
        odn_scr[:, hs] = o

    pp_scr[POOL_PAD:POOL_PAD + C, :] = pin_ref[0]
    pos = start + n * C + lax.broadcasted_iota(I32, (C, 1), 0)
    for gi, win in enumerate(POOL_WINDOWS):
        gs = slice(gi * POOL_GROUP_DIM, (gi + 1) * POOL_GROUP_DIM)
        xg = pp_scr[POOL_PAD:POOL_PAD + C, gs]
        ssum = xg
        for sft in range(1, win):
            ssum = ssum + pp_scr[POOL_PAD - sft:POOL_PAD - sft + C, gs]
        cnt = jnp.minimum(pos + 1, win).astype(F32)
        pooled = ssum / cnt - xg
        os_ = slice(gi * POOL_OUT_GROUP, (gi + 1) * POOL_OUT_GROUP)
        yp = _dot(pooled, wpool_ref[gi]) * pscale_ref[:, os_]
        mixed_ref[0, :, os_] = (jax.nn.sigmoid(ga_ref[0, :, os_]) * odn_scr[:, os_]
                                + jax.nn.sigmoid(gb_ref[0, :, os_]) * yp)

    @pl.when(n == last)
    def _store_state():
        nconv_ref[0] = xp_scr[Lv + CONV_PAD - (CONV_WIDTH - 1):Lv + CONV_PAD, :]
        npool_ref[0] = pp_scr[Lv + POOL_PAD - POOL_BUF:Lv + POOL_PAD, :]
        ns_ref[0] = s_scr[...]

    xp_scr[0:CONV_PAD, :] = xp_scr[C:C + CONV_PAD, :]
    pp_scr[0:POOL_PAD, :] = pp_scr[C:C + POOL_PAD, :]


def _mixer(proj, conv_buf, s0, pool_buf, start, seq_len, C,
           conv_w, a_log, dt_bias, dn_norm_g, w_pool, pool_scale):
    b, lp, _ = proj["qkv"].shape
    nchunks = lp // C
    lv = seq_len - (nchunks - 1) * C
    cbuf = jnp.pad(conv_buf, ((0, 0), (CONV_PAD - (CONV_WIDTH - 1), 0), (0, 0)))
    pbuf = jnp.pad(pool_buf, ((0, 0), (POOL_PAD - POOL_BUF, 0), (0, 0)))
    lane_pad = lambda a: jnp.pad(a.reshape(1, -1), ((0, 0), (DN_HEADS, LANES - 2 * DN_HEADS)))
    chunk = lambda w: pl.BlockSpec((1, C, w), lambda i, j: (i, j, 0))
    state = lambda *s: pl.BlockSpec((1,) + s, lambda i, j: (i,) + (0,) * len(s))
    return pl.pallas_call(
        functools.partial(_mixer_body, C, lv, start),
        grid=(b, nchunks),
        in_specs=[chunk(QKV_WIDTH), chunk(LANES), chunk(DN_WIDTH), chunk(POOL_WIDTH),
                  chunk(D_MODEL), chunk(D_MODEL),
                  state(CONV_PAD, QKV_WIDTH), state(DN_HEADS, DN_HEAD_DIM, DN_HEAD_DIM),
                  state(POOL_PAD, POOL_WIDTH),
                  _const_spec((CONV_WIDTH, QKV_WIDTH)), _const_spec((1, LANES)), _const_spec((1, LANES)),
                  _const_spec((1, DN_HEAD_DIM)),
                  _const_spec((len(POOL_WINDOWS), POOL_GROUP_DIM, POOL_OUT_GROUP)),
                  _const_spec((1, D_MODEL))],
        out_specs=[chunk(D_MODEL), state(CONV_WIDTH - 1, QKV_WIDTH),
                   state(DN_HEADS, DN_HEAD_DIM, DN_HEAD_DIM), state(POOL_BUF, POOL_WIDTH)],
        out_shape=[jax.ShapeDtypeStruct((b, lp, D_MODEL), F32),
                   jax.ShapeDtypeStruct((b, CONV_WIDTH - 1, QKV_WIDTH), F32),
                   jax.ShapeDtypeStruct((b, DN_HEADS, DN_HEAD_DIM, DN_HEAD_DIM), F32),
                   jax.ShapeDtypeStruct((b, POOL_BUF, POOL_WIDTH), F32)],
        scratch_shapes=[pltpu.VMEM((CONV_PAD + C + CONV_PAD, QKV_WIDTH), F32),
                        pltpu.VMEM((C, QKV_WIDTH), F32),
                        pltpu.VMEM((DN_HEADS, DN_HEAD_DIM, DN_HEAD_DIM), F32),
                        pltpu.VMEM((POOL_PAD + C + POOL_PAD, POOL_WIDTH), F32),
                        pltpu.VMEM((C, DN_WIDTH), F32)],
        compiler_params=pltpu.CompilerParams(dimension_semantics=("arbitrary", "arbitrary"),
                                             vmem_limit_bytes=VMEM_LIMIT),
        name="mixer",
    )(proj["qkv"], proj["ba"], proj["z"], proj["pool"], proj["ga"], proj["gb"], cbuf, s0, pbuf,
      conv_w, lane_pad(a_log), lane_pad(dt_bias), dn_norm_g.reshape(1, -1), w_pool,
      pool_scale.reshape(1, -1))


def _top16(s, ids, payload=None):
    big = float(2 ** 24)
    vals, sel, pays = [], [], []
    for _ in range(PEER_TOPK):
        m = jnp.max(s, axis=0, keepdims=True)
        am = jnp.min(jnp.where(s == m, ids, big), axis=0, keepdims=True)
        hit = ids == am
        if payload is not None:
            pays.append(jnp.max(jnp.where(hit, payload, -1.0), axis=0, keepdims=True))
        s = jnp.where(hit, -jnp.inf, s)
        vals.append(m)
        sel.append(am)
    out = (jnp.concatenate(vals, axis=0), jnp.concatenate(sel, axis=0))
    if payload is not None:
        out += (jnp.concatenate(pays, axis=0),)
    return out


_CAND_EDGE = 4


def _post_body(mixed_ref, x_ref, g1_ref, sc2_ref, sh2_ref, n2g_ref, wout_ref, wq_ref, keys_ref,
               x1_ref, h2_ref, idx_ref, gate_ref):
    tm = x_ref.shape[0]
    x1 = x_ref[...] + _mod_rows(g1_ref) * _dot(mixed_ref[...], wout_ref[...])
    x1_ref[...] = x1
    y = x1 * lax.rsqrt(jnp.mean(x1 * x1, axis=-1, keepdims=True) + EPS) * n2g_ref[...]
    h2 = y * (1.0 + _mod_rows(sc2_ref)) + _mod_rows(sh2_ref)
    h2_ref[...] = h2
    q = _dot(h2, wq_ref[...])

    K = PEER_TOPK
    key_id = lax.broadcasted_iota(I32, (PEER_NKEYS, 1), 0).astype(F32)
    r16 = lax.broadcasted_iota(I32, (K, 1), 0)
    cand_id = jnp.concatenate([(a * K + r16) for a in range(_CAND_EDGE)]
                              + [(r16 * K + b) for b in range(_CAND_EDGE)], axis=0).astype(F32)
    dup = r16 < _CAND_EDGE
    idx_rows, gate_rows = [], []
    for h in range(PEER_HEADS):
        half = []
        for p in range(2):
            c0 = (h * 2 + p) * PEER_KEY_HALF
            st = _dot_nt(keys_ref[h * 2 + p], q[:, c0:c0 + PEER_KEY_HALF])
            half.append(_top16(st, key_id))
        (s1, i1), (s2, i2) = half
        cand = jnp.concatenate(
            [s1[a:a + 1] + s2 for a in range(_CAND_EDGE)]
            + [jnp.where(dup, -jnp.inf, s1 + s2[b:b + 1]) for b in range(_CAND_EDGE)], axis=0)
        cidx = jnp.concatenate(
            [i1[a:a + 1] * PEER_NKEYS + i2 for a in range(_CAND_EDGE)]
            + [i1 * PEER_NKEYS + i2[b:b + 1] for b in range(_CAND_EDGE)], axis=0)
        best, _, eidx = _top16(cand, cand_id, cidx)
        e = jnp.exp(best - best[0:1])
        gate_rows.append(e / jnp.sum(e, axis=0, keepdims=True))
        idx_rows.append(eidx)
    idx_ref[...] = jnp.concatenate(idx_rows, axis=0).T.astype(I32)
    gate_ref[...] = jnp.concatenate(gate_rows, axis=0).T


def _post(mixed2d, x2d, mod, rows_per_batch, norm2_g, w_out, w_query, keys, tm):
    t = x2d.shape[0]
    row = lambda w: pl.BlockSpec((tm, w), lambda i: (i, 0))
    return pl.pallas_call(
        _post_body,
        grid=(t // tm,),
        in_specs=[row(D_MODEL), row(D_MODEL),
                  _mod_spec(2, rows_per_batch, tm), _mod_spec(4, rows_per_batch, tm),
                  _mod_spec(3, rows_per_batch, tm), _const_spec((1, D_MODEL)),
                  _const_spec((D_MODEL, D_MODEL)), _const_spec((D_MODEL, 2 * PEER_HEADS * PEER_KEY_HALF)),
                  _const_spec((2 * PEER_HEADS, PEER_NKEYS, PEER_KEY_HALF))],
        out_specs=[row(D_MODEL), row(D_MODEL), row(PEER_HK), row(PEER_HK)],
        out_shape=[jax.ShapeDtypeStruct((t, D_MODEL), F32), jax.ShapeDtypeStruct((t, D_MODEL), F32),
                   jax.ShapeDtypeStruct((t, PEER_HK), I32), jax.ShapeDtypeStruct((t, PEER_HK), F32)],
        compiler_params=pltpu.CompilerParams(vmem_limit_bytes=VMEM_LIMIT),
        name="post",
    )(mixed2d, x2d, mod, mod, mod, norm2_g.reshape(1, -1), w_out, w_query, keys)


_EXP_TOKENS = 32


def _expert_body(idx_ref, h2_ref, gate_ref, x1_ref, g2_ref, fng_ref, u_hbm, v_hbm,
                 y_ref, ubuf, vbuf, sem):
    tb = h2_ref.shape[0]

    def row_copy(table, buf, slot, t, r):
        return pltpu.make_async_copy(table.at[pl.ds(idx_ref[t, r], 1)],
                                     buf.at[slot, pl.ds(r, 1)], sem.at[slot])

    def start(t, slot):
        def issue(r, c):
            row_copy(u_hbm, ubuf, slot, t, r).start()
            row_copy(v_hbm, vbuf, slot, t, r).start()
            return c
        lax.fori_loop(0, PEER_HK, issue, 0)

    def wait(t, slot):
        def drain(r, c):
            row_copy(u_hbm, ubuf, slot, t, r).wait()
            row_copy(v_hbm, vbuf, slot, t, r).wait()
            return c
        lax.fori_loop(0, PEER_HK, drain, 0)

    start(0, 0)
    per_token_mod = len(g2_ref.shape) == 3
    eye = (lax.broadcasted_iota(I32, (PEER_HK, PEER_HK), 0)
           == lax.broadcasted_iota(I32, (PEER_HK, PEER_HK), 1))

    def token(t, c):
        slot = t % 2

        @pl.when(t + 1 < tb)
        def _prefetch():
            start(t + 1, 1 - slot)

        wait(t, slot)
        h2 = h2_ref[pl.ds(t, 1), :]
        ub = ubuf[slot].astype(BF16).astype(F32)
        act = jnp.sum(ub * h2.astype(BF16).astype(F32), axis=1, keepdims=True)
        act = _gelu(act)
        act_row = jnp.sum(jnp.where(eye, act, 0.0), axis=0, keepdims=True)
        coef = gate_ref[pl.ds(t, 1), :] * act_row
        out = _dot(coef, vbuf[slot])
        g2t = g2_ref[0, pl.ds(t, 1), :] if per_token_mod else g2_ref[0, 0]
        x2 = x1_ref[pl.ds(t, 1), :] + g2t * out
        y_ref[pl.ds(t, 1), :] = (x2 * lax.rsqrt(jnp.mean(x2 * x2, axis=-1, keepdims=True) + EPS)
                                 * fng_ref[...])
        return c

    lax.fori_loop(0, tb, token, 0)


def _expert(idx, h2, gates, x1, mod, rows_per_batch, final_g, expert_u, expert_v):
    t = h2.shape[0]
    tb = _EXP_TOKENS
    row = lambda w: pl.BlockSpec((tb, w), lambda i: (i, 0))
    return pl.pallas_call(
        _expert_body,
        grid=(t // tb,),
        in_specs=[pl.BlockSpec((tb, PEER_HK), lambda i: (i, 0), memory_space=pltpu.SMEM),
                  row(D_MODEL), row(PEER_HK), row(D_MODEL), _mod_spec(5, rows_per_batch, tb),
                  _const_spec((1, D_MODEL)),
                  pl.BlockSpec(memory_space=pl.ANY), pl.BlockSpec(memory_space=pl.ANY)],
        out_specs=row(D_MODEL),
        out_shape=jax.ShapeDtypeStruct((t, D_MODEL), F32),
        scratch_shapes=[pltpu.VMEM((2, PEER_HK, D_MODEL), F32), pltpu.VMEM((2, PEER_HK, D_MODEL), F32),
                        pltpu.SemaphoreType.DMA((2,))],
        compiler_params=pltpu.CompilerParams(vmem_limit_bytes=VMEM_LIMIT),
        name="expert",
    )(idx, h2, gates, x1, mod, final_g.reshape(1, -1), expert_u, expert_v)


def _group(x, mod, conv_buf, s0, pool_buf, start, chunk, tm, wts):
    b, l, _ = x.shape
    t = b * l
    x2d = x.reshape(t, D_MODEL)
    if l >= tm:
        modx = mod.reshape(b, 6, 1, D_MODEL).transpose(1, 0, 2, 3)
    else:
        modx = jnp.repeat(mod.reshape(b, 6, D_MODEL), l, axis=0).transpose(1, 0, 2)
    outs = _inproj(x2d, modx, l, wts["norm1_g"], wts["w_cat"], tm)
    lp = -(-l // chunk) * chunk
    proj = {}
    for (name, w), a in zip(_IN_BLOCKS, outs):
        a = a.reshape(b, l, w)
        proj[name] = a if lp == l else jnp.pad(a, ((0, 0), (0, lp - l), (0, 0)))
    mixed, nconv, ns, npool = _mixer(proj, conv_buf, s0, pool_buf, start, l, chunk,
                                     wts["conv_w"], wts["a_log"], wts["dt_bias"], wts["dn_norm_g"],
                                     wts["w_pool"], wts["pool_scale"])
    mixed2d = mixed[:, :l].reshape(t, D_MODEL)
    x1, h2, idx, gates = _post(mixed2d, x2d, modx, l, wts["norm2_g"], wts["w_out"], wts["w_query"],
                               wts["keys"], tm)
    assert (l >= _EXP_TOKENS) == (l >= tm)
    y = _expert(idx, h2, gates, x1, modx, l, wts["final_norm_g"], wts["expert_u"], wts["expert_v"])
    return y.reshape(b, l, D_MODEL), nconv, ns, npool


def kernel(x_prompt, x_sample, c_prompt, c_sample, state_conv, state_delta, state_pool, w_ada, b_ada, norm1_g, w_in, conv_w, a_log, dt_bias, dn_norm_g, w_pool, pool_scale, w_out, norm2_g, w_query, sub_keys, expert_u, expert_v, final_norm_g):
    bp = x_prompt.shape[0]
    bs = x_sample.shape[0]
    yp, ys = x_prompt, x_sample
    conv_p, delta_p, pool_p, conv_s, delta_s, pool_s = [], [], [], [], [], []
    zero_conv = jnp.zeros((bp, CONV_WIDTH - 1, QKV_WIDTH), F32)
    zero_delta = jnp.zeros((bp, DN_HEADS, DN_HEAD_DIM, DN_HEAD_DIM), F32)
    zero_pool = jnp.zeros((bp, POOL_BUF, POOL_WIDTH), F32)
    c_all = jnp.concatenate([c_prompt, c_sample], axis=0)
    for layer in range(DEPTH):
        wi = w_in[layer]
        o_b = QKV_WIDTH
        o_z = o_b + 2 * DN_HEADS
        w_ba = jnp.pad(wi[:, o_b:o_z], ((0, 0), (0, LANES - 2 * DN_HEADS)))
        w_cat = jnp.concatenate([wi[:, :o_b], wi[:, o_z:], w_ba], axis=1).astype(BF16)
        last = layer == DEPTH - 1
        wts = dict(
            norm1_g=norm1_g[layer], w_cat=w_cat, conv_w=conv_w[layer], a_log=a_log[layer],
            dt_bias=dt_bias[layer], dn_norm_g=dn_norm_g[layer], w_pool=w_pool[layer],
            pool_scale=pool_scale[layer], w_out=w_out[layer].astype(BF16), norm2_g=norm2_g[layer],
            w_query=w_query[layer].astype(BF16),
            keys=sub_keys[layer].reshape(2 * PEER_HEADS, PEER_NKEYS, PEER_KEY_HALF).astype(BF16),
            expert_u=expert_u[layer], expert_v=expert_v[layer],
            final_norm_g=final_norm_g if last else jnp.ones_like(final_norm_g))
        mod = _ada(c_all, w_ada[layer], b_ada[layer])
        assert last, "final norm is fused into the expert stage"
        yp, cp, sp, pp = _group(yp, mod[:bp], zero_conv, zero_delta, zero_pool, 0, DN_CHUNK, 256, wts)
        ys, cs, ss, ps = _group(ys, mod[bp:], state_conv[layer], state_delta[layer], state_pool[layer],
                                PAST_LEN, SUBLANES, 256, wts)
        conv_p.append(cp)
        delta_p.append(sp)
        pool_p.append(pp)
        conv_s.append(cs)
        delta_s.append(ss)
        pool_s.append(ps)
    return (yp, ys, jnp.stack(conv_p), jnp.stack(delta_p), jnp.stack(pool_p),
            jnp.stack(conv_s), jnp.stack(delta_s), jnp.stack(pool_s))
```

```python
import functools

import jax
import jax.numpy as jnp
from jax import lax
from jax.experimental import pallas as pl
from jax.experimental.pallas import tpu as pltpu
from jax.experimental.pallas import tpu_sc as plsc

F32 = jnp.float32
BF16 = jnp.bfloat16
I32 = jnp.int32

D_MODEL = 1024
DEPTH = 1
PAST_LEN = 16384
DN_HEADS = 8
DN_HEAD_DIM = 128
DN_WIDTH = DN_HEADS * DN_HEAD_DIM
QKV_WIDTH = 3 * DN_WIDTH
CONV_WIDTH = 4
DN_CHUNK = 64
POOL_WINDOWS = (2, 4, 8, 16)
POOL_GROUP_DIM = 128
POOL_WIDTH = len(POOL_WINDOWS) * POOL_GROUP_DIM
POOL_OUT_GROUP = D_MODEL // len(POOL_WINDOWS)
POOL_BUF = max(POOL_WINDOWS) - 1
PEER_HEADS = 8
PEER_NKEYS = 128
PEER_TOPK = 16
PEER_KEY_HALF = 128
PEER_HK = PEER_HEADS * PEER_TOPK
EPS = 1e-6

LANES = 128
SUBLANES = 8
CONV_PAD = SUBLANES
POOL_PAD = 16
VMEM_LIMIT = 56 * 1024 * 1024

NT_DIMS = (((1,), (1,)), ((), ()))
TN_DIMS = (((0,), (0,)), ((), ()))


def _dot(a, b):
    return jnp.dot(a.astype(BF16), b.astype(BF16), preferred_element_type=F32)


def _dot_nt(a, b):
    return lax.dot_general(a.astype(BF16), b.astype(BF16), NT_DIMS, preferred_element_type=F32)


def _split3(x):
    hi = x.astype(BF16)
    r1 = x - hi.astype(F32)
    mid = r1.astype(BF16)
    lo = (r1 - mid.astype(F32)).astype(BF16)
    return hi, mid, lo


def _silu(x):
    return x * jax.nn.sigmoid(x)


def _gelu(x):
    return 0.5 * x * (1.0 + lax.erf(x * (0.5 ** 0.5)))


def _softplus(x):
    return jnp.maximum(x, 0.0) + jnp.log(1.0 + jnp.exp(-jnp.abs(x)))


def _mod_rows(ref):
    m = ref[...]
    return m.reshape(m.shape[-2], m.shape[-1])


def _mod_spec(k, rows_per_batch, tm):
    if rows_per_batch >= tm:
        tiles = rows_per_batch // tm
        return pl.BlockSpec((1, 1, 1, D_MODEL), lambda i, *_: (k, i // tiles, 0, 0))
    return pl.BlockSpec((1, tm, D_MODEL), lambda i, *_: (k, i, 0))


def _const_spec(shape):
    nd = len(shape)
    return pl.BlockSpec(shape, lambda *_: (0,) * nd)


def _ada_body(c_ref, w_ref, b_ref, o_ref):
    o_ref[...] = _dot(_silu(c_ref[...]), w_ref[...]) + b_ref[...]


def _ada(c, w_ada, b_ada):
    n = c.shape[0]
    return pl.pallas_call(
        _ada_body,
        grid=(6,),
        in_specs=[pl.BlockSpec((n, D_MODEL), lambda j: (0, 0)),
                  pl.BlockSpec((D_MODEL, D_MODEL), lambda j: (0, j)),
                  pl.BlockSpec((1, D_MODEL), lambda j: (0, j))],
        out_specs=pl.BlockSpec((n, D_MODEL), lambda j: (0, j)),
        out_shape=jax.ShapeDtypeStruct((n, 6 * D_MODEL), F32),
        name="ada",
    )(c, w_ada, b_ada.reshape(1, -1))


_IN_BLOCKS = (("qkv", QKV_WIDTH), ("z", DN_WIDTH), ("pool", POOL_WIDTH),
              ("ga", D_MODEL), ("gb", D_MODEL), ("ba", LANES))
_IN_TOTAL = sum(w for _, w in _IN_BLOCKS)
_IN_COL_CHUNK = 512


def _inproj_body(x_ref, sc_ref, sh_ref, g_ref, w_ref, *out_refs):
    x = x_ref[...]
    y = x * lax.rsqrt(jnp.mean(x * x, axis=-1, keepdims=True) + EPS) * g_ref[...]
    h = (y * (1.0 + _mod_rows(sc_ref)) + _mod_rows(sh_ref)).astype(BF16)
    off = 0
    for (_, width), o_ref in zip(_IN_BLOCKS, out_refs):
        for c0 in range(0, width, _IN_COL_CHUNK):
            cw = min(_IN_COL_CHUNK, width - c0)
            o_ref[:, c0:c0 + cw] = jnp.dot(h, w_ref[:, off + c0:off + c0 + cw],
                                           preferred_element_type=F32)
        off += width


def _inproj(x2d, mod, rows_per_batch, norm_g, w_cat, tm):
    t = x2d.shape[0]
    row = lambda w: pl.BlockSpec((tm, w), lambda i: (i, 0))
    return pl.pallas_call(
        _inproj_body,
        grid=(t // tm,),
        in_specs=[row(D_MODEL), _mod_spec(1, rows_per_batch, tm), _mod_spec(0, rows_per_batch, tm),
                  _const_spec((1, D_MODEL)),
                  pl.BlockSpec((D_MODEL, _IN_TOTAL), lambda i: (0, 0), pipeline_mode=pl.Buffered(1))],
        out_specs=[row(w) for _, w in _IN_BLOCKS],
        out_shape=[jax.ShapeDtypeStruct((t, w), F32) for _, w in _IN_BLOCKS],
        compiler_params=pltpu.CompilerParams(vmem_limit_bytes=VMEM_LIMIT),
        name="inproj",
    )(x2d, mod, mod, norm_g.reshape(1, -1), w_cat)


def _mixer_body(C, Lv, start,
                qkv_ref, ba_ref, z_ref, pin_ref, ga_ref, gb_ref, cbuf_ref, s0_ref, pbuf_ref,
                convw_ref, alog_ref, dtb_ref, dng_ref, wpool_ref, pscale_ref,
                mixed_ref, nconv_ref, ns_ref, npool_ref,
                xp_scr, act_scr, s_scr, pp_scr, odn_scr):
    n = pl.program_id(1)
    last = pl.num_programs(1) - 1

    @pl.when(n == 0)
    def _load_state():
        xp_scr[0:CONV_PAD, :] = cbuf_ref[0]
        pp_scr[0:POOL_PAD, :] = pbuf_ref[0]
        s_scr[...] = s0_ref[0]

    xp_scr[CONV_PAD:CONV_PAD + C, :] = qkv_ref[0]
    for c0 in range(0, QKV_WIDTH, 512):
        cs = slice(c0, c0 + 512)
        y = xp_scr[CONV_PAD:CONV_PAD + C, cs] * convw_ref[CONV_WIDTH - 1:CONV_WIDTH, cs]
        for k in range(CONV_WIDTH - 1):
            r0 = CONV_PAD - (CONV_WIDTH - 1) + k
            y = y + xp_scr[r0:r0 + C, cs] * convw_ref[k:k + 1, cs]
        act_scr[:, cs] = _silu(y)

    ba = ba_ref[0]
    lane = lax.broadcasted_iota(I32, (C, LANES), 1)
    beta_all = jax.nn.sigmoid(ba)
    g_all = -jnp.exp(alog_ref[...]) * _softplus(ba + dtb_ref[...])
    if Lv < C:
        valid = lax.broadcasted_iota(I32, (C, LANES), 0) < Lv
        beta_all = jnp.where(valid, beta_all, 0.0)
        g_all = jnp.where(valid, g_all, 0.0)
    ii = lax.broadcasted_iota(I32, (C, C), 0)
    jj = lax.broadcasted_iota(I32, (C, C), 1)
    causal = ii >= jj
    strict = ii > jj
    tril = jnp.where(causal, 1.0, 0.0).astype(BF16)
    eye = jnp.where(ii == jj, 1.0, 0.0)
    gc_all = sum(jnp.dot(tril, part, preferred_element_type=F32) for part in _split3(g_all))
    if C < LANES:
        gc_sq = jnp.concatenate([gc_all, jnp.zeros((LANES - C, LANES), F32)], axis=0)
    else:
        gc_sq = gc_all
    gc_t = gc_sq.T

    for h in range(DN_HEADS):
        hs = slice(h * DN_HEAD_DIM, (h + 1) * DN_HEAD_DIM)
        beta = jnp.sum(jnp.where(lane == h, beta_all, 0.0), axis=1, keepdims=True)
        gcol = jnp.sum(jnp.where(lane == DN_HEADS + h, gc_all, 0.0), axis=1, keepdims=True)
        grow = gc_t[DN_HEADS + h:DN_HEADS + h + 1, 0:C]
        glast = gcol[C - 1:C, :]

        q = act_scr[:, hs]
        k = act_scr[:, DN_WIDTH + h * DN_HEAD_DIM:DN_WIDTH + (h + 1) * DN_HEAD_DIM]
        v = act_scr[:, 2 * DN_WIDTH + h * DN_HEAD_DIM:2 * DN_WIDTH + (h + 1) * DN_HEAD_DIM]
        q = q * lax.rsqrt(jnp.sum(q * q, axis=-1, keepdims=True) + EPS) * (DN_HEAD_DIM ** -0.5)
        k = k * lax.rsqrt(jnp.sum(k * k, axis=-1, keepdims=True) + EPS)
        kb = k * beta
        vb = v * beta

        decay = jnp.where(causal, jnp.exp(jnp.where(causal, gcol - grow, 0.0)), 0.0)
        lower = jnp.where(strict, _dot_nt(kb, k) * decay, 0.0)
        ainv = eye - lower
        pw = lower
        p = 1
        while 2 * p < C:
            pw = _dot(pw, pw)
            ainv = ainv + _dot(ainv, pw)
            p *= 2
        sol = _dot(ainv, jnp.concatenate([vb, kb * jnp.exp(gcol)], axis=1))
        u = sol[:, :DN_HEAD_DIM]
        w = sol[:, DN_HEAD_DIM:]
        qk = _dot_nt(q, k) * decay
        k_tail = k * jnp.exp(glast - gcol)

        S = s_scr[h]
        v_new = u - _dot(w, S)
        o = _dot(q * jnp.exp(gcol), S) + _dot(qk, v_new)
        s_scr[h] = S * jnp.exp(glast) + lax.dot_general(
            k_tail.astype(BF16), v_new.astype(BF16), TN_DIMS, preferred_element_type=F32)

        zf = z_ref[0, :, hs]
        o = o * lax.rsqrt(jnp.mean(o * o, axis=-1, keepdims=True) + EPS) * dng_ref[...] * _silu(zf)
        odn_scr[:, hs] = o

    pp_scr[POOL_PAD:POOL_PAD + C, :] = pin_ref[0]
    pos = start + n * C + lax.broadcasted_iota(I32, (C, 1), 0)
    for gi, win in enumerate(POOL_WINDOWS):
        gs = slice(gi * POOL_GROUP_DIM, (gi + 1) * POOL_GROUP_DIM)
        xg = pp_scr[POOL_PAD:POOL_PAD + C, gs]
        ssum = xg
        for sft in range(1, win):
            ssum = ssum + pp_scr[POOL_PAD - sft:POOL_PAD - sft + C, gs]
        cnt = jnp.minimum(pos + 1, win).astype(F32)
        pooled = ssum / cnt - xg
        os_ = slice(gi * POOL_OUT_GROUP, (gi + 1) * POOL_OUT_GROUP)
        yp = _dot(pooled, wpool_ref[gi]) * pscale_ref[:, os_]
        mixed_ref[0, :, os_] = (jax.nn.sigmoid(ga_ref[0, :, os_]) * odn_scr[:, os_]
                                + jax.nn.sigmoid(gb_ref[0, :, os_]) * yp)

    @pl.when(n == last)
    def _store_state():
        nconv_ref[0] = xp_scr[Lv + CONV_PAD - (CONV_WIDTH - 1):Lv + CONV_PAD, :]
        npool_ref[0] = pp_scr[Lv + POOL_PAD - POOL_BUF:Lv + POOL_PAD, :]
        ns_ref[0] = s_scr[...]

    xp_scr[0:CONV_PAD, :] = xp_scr[C:C + CONV_PAD, :]
    pp_scr[0:POOL_PAD, :] = pp_scr[C:C + POOL_PAD, :]


def _mixer(proj, conv_buf, s0, pool_buf, start, seq_len, C,
           conv_w, a_log, dt_bias, dn_norm_g, w_pool, pool_scale):
    b, lp, _ = proj["qkv"].shape
    nchunks = lp // C
    lv = seq_len - (nchunks - 1) * C
    cbuf = jnp.pad(conv_buf, ((0, 0), (CONV_PAD - (CONV_WIDTH - 1), 0), (0, 0)))
    pbuf = jnp.pad(pool_buf, ((0, 0), (POOL_PAD - POOL_BUF, 0), (0, 0)))
    lane_pad = lambda a: jnp.pad(a.reshape(1, -1), ((0, 0), (DN_HEADS, LANES - 2 * DN_HEADS)))
    chunk = lambda w: pl.BlockSpec((1, C, w), lambda i, j: (i, j, 0))
    state = lambda *s: pl.BlockSpec((1,) + s, lambda i, j: (i,) + (0,) * len(s))
    return pl.pallas_call(
        functools.partial(_mixer_body, C, lv, start),
        grid=(b, nchunks),
        in_specs=[chunk(QKV_WIDTH), chunk(LANES), chunk(DN_WIDTH), chunk(POOL_WIDTH),
                  chunk(D_MODEL), chunk(D_MODEL),
                  state(CONV_PAD, QKV_WIDTH), state(DN_HEADS, DN_HEAD_DIM, DN_HEAD_DIM),
                  state(POOL_PAD, POOL_WIDTH),
                  _const_spec((CONV_WIDTH, QKV_WIDTH)), _const_spec((1, LANES)), _const_spec((1, LANES)),
                  _const_spec((1, DN_HEAD_DIM)),
                  _const_spec((len(POOL_WINDOWS), POOL_GROUP_DIM, POOL_OUT_GROUP)),
                  _const_spec((1, D_MODEL))],
        out_specs=[chunk(D_MODEL), state(CONV_WIDTH - 1, QKV_WIDTH),
                   state(DN_HEADS, DN_HEAD_DIM, DN_HEAD_DIM), state(POOL_BUF, POOL_WIDTH)],
        out_shape=[jax.ShapeDtypeStruct((b, lp, D_MODEL), F32),
                   jax.ShapeDtypeStruct((b, CONV_WIDTH - 1, QKV_WIDTH), F32),
                   jax.ShapeDtypeStruct((b, DN_HEADS, DN_HEAD_DIM, DN_HEAD_DIM), F32),
                   jax.ShapeDtypeStruct((b, POOL_BUF, POOL_WIDTH), F32)],
        scratch_shapes=[pltpu.VMEM((CONV_PAD + C + CONV_PAD, QKV_WIDTH), F32),
                        pltpu.VMEM((C, QKV_WIDTH), F32),
                        pltpu.VMEM((DN_HEADS, DN_HEAD_DIM, DN_HEAD_DIM), F32),
                        pltpu.VMEM((POOL_PAD + C + POOL_PAD, POOL_WIDTH), F32),
                        pltpu.VMEM((C, DN_WIDTH), F32)],
        compiler_params=pltpu.CompilerParams(dimension_semantics=("arbitrary", "arbitrary"),
                                             vmem_limit_bytes=VMEM_LIMIT),
        name="mixer",
    )(proj["qkv"], proj["ba"], proj["z"], proj["pool"], proj["ga"], proj["gb"], cbuf, s0, pbuf,
      conv_w, lane_pad(a_log), lane_pad(dt_bias), dn_norm_g.reshape(1, -1), w_pool,
      pool_scale.reshape(1, -1))


def _top16(s, ids, payload=None):
    big = float(2 ** 24)
    vals, sel, pays = [], [], []
    for _ in range(PEER_TOPK):
        m = jnp.max(s, axis=0, keepdims=True)
        am = jnp.min(jnp.where(s == m, ids, big), axis=0, keepdims=True)
        hit = ids == am
        if payload is not None:
            pays.append(jnp.max(jnp.where(hit, payload, -1.0), axis=0, keepdims=True))
        s = jnp.where(hit, -jnp.inf, s)
        vals.append(m)
        sel.append(am)
    out = (jnp.concatenate(vals, axis=0), jnp.concatenate(sel, axis=0))
    if payload is not None:
        out += (jnp.concatenate(pays, axis=0),)
    return out


_CAND_EDGE = 4


def _post_body(mixed_ref, x_ref, g1_ref, sc2_ref, sh2_ref, n2g_ref, wout_ref, wq_ref, keys_ref,
               x1_ref, h2_ref, idx_ref, gate_ref):
    tm = x_ref.shape[0]
    x1 = x_ref[...] + _mod_rows(g1_ref) * _dot(mixed_ref[...], wout_ref[...])
    x1_ref[...] = x1
    y = x1 * lax.rsqrt(jnp.mean(x1 * x1, axis=-1, keepdims=True) + EPS) * n2g_ref[...]
    h2 = y * (1.0 + _mod_rows(sc2_ref)) + _mod_rows(sh2_ref)
    h2_ref[...] = h2
    q = _dot(h2, wq_ref[...])

    K = PEER_TOPK
    key_id = lax.broadcasted_iota(I32, (PEER_NKEYS, 1), 0).astype(F32)
    r16 = lax.broadcasted_iota(I32, (K, 1), 0)
    cand_id = jnp.concatenate([(a * K + r16) for a in range(_CAND_EDGE)]
                              + [(r16 * K + b) for b in range(_CAND_EDGE)], axis=0).astype(F32)
    dup = r16 < _CAND_EDGE
    idx_rows, gate_rows = [], []
    for h in range(PEER_HEADS):
        half = []
        for p in range(2):
            c0 = (h * 2 + p) * PEER_KEY_HALF
            st = _dot_nt(keys_ref[h * 2 + p], q[:, c0:c0 + PEER_KEY_HALF])
            half.append(_top16(st, key_id))
        (s1, i1), (s2, i2) = half
        cand = jnp.concatenate(
            [s1[a:a + 1] + s2 for a in range(_CAND_EDGE)]
            + [jnp.where(dup, -jnp.inf, s1 + s2[b:b + 1]) for b in range(_CAND_EDGE)], axis=0)
        cidx = jnp.concatenate(
            [i1[a:a + 1] * PEER_NKEYS + i2 for a in range(_CAND_EDGE)]
            + [i1 * PEER_NKEYS + i2[b:b + 1] for b in range(_CAND_EDGE)], axis=0)
        best, _, eidx = _top16(cand, cand_id, cidx)
        e = jnp.exp(best - best[0:1])
        gate_rows.append(e / jnp.sum(e, axis=0, keepdims=True))
        idx_rows.append(eidx)
    idx_ref[...] = jnp.concatenate(idx_rows, axis=0).T.astype(I32)
    gate_ref[...] = jnp.concatenate(gate_rows, axis=0).T


def _post(mixed2d, x2d, mod, rows_per_batch, norm2_g, w_out, w_query, keys, tm):
    t = x2d.shape[0]
    row = lambda w: pl.BlockSpec((tm, w), lambda i: (i, 0))
    return pl.pallas_call(
        _post_body,
        grid=(t // tm,),
        in_specs=[row(D_MODEL), row(D_MODEL),
                  _mod_spec(2, rows_per_batch, tm), _mod_spec(4, rows_per_batch, tm),
                  _mod_spec(3, rows_per_batch, tm), _const_spec((1, D_MODEL)),
                  _const_spec((D_MODEL, D_MODEL)), _const_spec((D_MODEL, 2 * PEER_HEADS * PEER_KEY_HALF)),
                  _const_spec((2 * PEER_HEADS, PEER_NKEYS, PEER_KEY_HALF))],
        out_specs=[row(D_MODEL), row(D_MODEL), row(PEER_HK), row(PEER_HK)],
        out_shape=[jax.ShapeDtypeStruct((t, D_MODEL), F32), jax.ShapeDtypeStruct((t, D_MODEL), F32),
                   jax.ShapeDtypeStruct((t, PEER_HK), I32), jax.ShapeDtypeStruct((t, PEER_HK), F32)],
        compiler_params=pltpu.CompilerParams(vmem_limit_bytes=VMEM_LIMIT),
        name="post",
    )(mixed2d, x2d, mod, mod, mod, norm2_g.reshape(1, -1), w_out, w_query, keys)


SC_CORES = 2
SC_SUBCORES = 16
SC_LANES = 16
SC_WORKERS = SC_CORES * SC_SUBCORES
SC_TOKENS = 16
SC_SLOTS = 4
SC_CHUNKS = D_MODEL // SC_LANES


def _sc_mesh():
    return plsc.VectorSubcoreMesh(core_axis_name="c", subcore_axis_name="s")


def _sc_worker():
    return lax.axis_index("s") * SC_CORES + lax.axis_index("c")


def _sc_jobs(table_hbm, idx_v, buf, sem, compute):
    njobs = SC_TOKENS * PEER_HEADS

    def copy(j, slot):
        tt = j // PEER_HEADS
        h = j % PEER_HEADS
        rows = idx_v[tt, pl.ds(h * PEER_TOPK, PEER_TOPK)]
        return pltpu.make_async_copy(table_hbm.at[rows], buf.at[slot], sem.at[slot])

    for s in range(SC_SLOTS):
        copy(s, s).start()

    def group(g, c):
        for s in range(SC_SLOTS):
            j = g * SC_SLOTS + s
            copy(j, s).wait()
            compute(j // PEER_HEADS, j % PEER_HEADS, s)

            @pl.when(j + SC_SLOTS < njobs)
            def _next():
                copy(j + SC_SLOTS, s).start()
        return c

    lax.fori_loop(0, njobs // SC_SLOTS, group, 0)


def _peer_u_body(n_tok, idx_hbm, h2_hbm, u_hbm, pre_hbm, idx_v, h2_v, pre_v, ubuf, acc_v, sem):
    base = _sc_worker() * n_tok
    lane = lax.iota(I32, SC_LANES)

    def compute(tt, h, slot):
        def chunk(c, accs):
            xv = h2_v[tt, pl.ds(c * SC_LANES, SC_LANES)]
            return tuple(a + ubuf[slot, k, pl.ds(c * SC_LANES, SC_LANES)] * xv
                         for k, a in enumerate(accs))
        zero = jnp.zeros((SC_LANES,), F32)
        accs = lax.fori_loop(0, SC_CHUNKS, chunk, (zero,) * PEER_TOPK)
        for k, a in enumerate(accs):
            acc_v[k, :] = a
        tot = zero
        for j in range(SC_LANES):
            tot = tot + plsc.load_gather(acc_v, [lane, jnp.full((SC_LANES,), j, I32)])
        pre_v[tt, pl.ds(h * PEER_TOPK, PEER_TOPK)] = tot

    def block(bi, c):
        t0 = base + bi * SC_TOKENS
        pltpu.sync_copy(idx_hbm.at[pl.ds(t0, SC_TOKENS)], idx_v)
        pltpu.sync_copy(h2_hbm.at[pl.ds(t0, SC_TOKENS)], h2_v)
        _sc_jobs(u_hbm, idx_v, ubuf, sem, compute)
        pltpu.sync_copy(pre_v, pre_hbm.at[pl.ds(t0, SC_TOKENS)])
        return c

    lax.fori_loop(0, n_tok // SC_TOKENS, block, 0)


def _peer_v_body(n_tok, idx_hbm, coef_hbm, v_hbm, out_hbm, idx_v, coef_v, out_v, vbuf, sem):
    base = _sc_worker() * n_tok
    zero = jnp.zeros((SC_LANES,), F32)

    def compute(tt, h, slot):
        row = jnp.full((SC_LANES,), tt, I32)
        cb = [plsc.load_gather(coef_v, [row, jnp.full((SC_LANES,), h * PEER_TOPK + k, I32)])
              for k in range(PEER_TOPK)]

        def chunk(c, carry):
            cs = pl.ds(c * SC_LANES, SC_LANES)
            o = cb[0] * vbuf[slot, 0, cs]
            for k in range(1, PEER_TOPK):
                o = o + cb[k] * vbuf[slot, k, cs]
            plsc.addupdate(out_v.at[tt, cs], o)
            return carry
        lax.fori_loop(0, SC_CHUNKS, chunk, 0)

    def block(bi, c):
        t0 = base + bi * SC_TOKENS
        pltpu.sync_copy(idx_hbm.at[pl.ds(t0, SC_TOKENS)], idx_v)
        pltpu.sync_copy(coef_hbm.at[pl.ds(t0, SC_TOKENS)], coef_v)

        def clear(i, cc):
            out_v[i // SC_CHUNKS, pl.ds((i % SC_CHUNKS) * SC_LANES, SC_LANES)] = zero
            return cc
        lax.fori_loop(0, SC_TOKENS * SC_CHUNKS, clear, 0)
        _sc_jobs(v_hbm, idx_v, vbuf, sem, compute)
        pltpu.sync_copy(out_v, out_hbm.at[pl.ds(t0, SC_TOKENS)])
        return c

    lax.fori_loop(0, n_tok // SC_TOKENS, block, 0)


def _peer_sc(body, idx, rows, table, out_width, name):
    t = idx.shape[0]
    assert t % (SC_WORKERS * SC_TOKENS) == 0
    n_tok = t // SC_WORKERS
    return pl.kernel(
        functools.partial(body, n_tok),
        out_type=jax.ShapeDtypeStruct((t, out_width), F32),
        mesh=_sc_mesh(),
        scratch_types=[pltpu.VMEM((SC_TOKENS, PEER_HK), I32),
                       pltpu.VMEM((SC_TOKENS, rows.shape[1]), F32),
                       pltpu.VMEM((SC_TOKENS, out_width), F32),
                       pltpu.VMEM((SC_SLOTS, PEER_TOPK, D_MODEL), F32)]
                      + ([pltpu.VMEM((PEER_TOPK, SC_LANES), F32)] if body is _peer_u_body else [])
                      + [pltpu.SemaphoreType.DMA((SC_SLOTS,))],
        compiler_params=pltpu.CompilerParams(needs_layout_passes=False),
        name=name,
    )(idx, rows, table)


def _coef_body(pre_ref, gate_ref, coef_ref):
    coef_ref[...] = gate_ref[...] * _gelu(pre_ref[...])


def _coef(pre, gates, tm):
    t = pre.shape[0]
    row = pl.BlockSpec((tm, PEER_HK), lambda i: (i, 0))
    return pl.pallas_call(_coef_body, grid=(t // tm,), in_specs=[row, row], out_specs=row,
                          out_shape=jax.ShapeDtypeStruct((t, PEER_HK), F32), name="coef")(pre, gates)


def _final_body(x1_ref, peer_ref, g2_ref, fng_ref, y_ref):
    x2 = x1_ref[...] + _mod_rows(g2_ref) * peer_ref[...]
    y_ref[...] = x2 * lax.rsqrt(jnp.mean(x2 * x2, axis=-1, keepdims=True) + EPS) * fng_ref[...]


def _final(x1, peer_out, mod, rows_per_batch, final_g, tm):
    t = x1.shape[0]
    row = pl.BlockSpec((tm, D_MODEL), lambda i: (i, 0))
    return pl.pallas_call(
        _final_body, grid=(t // tm,),
        in_specs=[row, row, _mod_spec(5, rows_per_batch, tm), _const_spec((1, D_MODEL))],
        out_specs=row, out_shape=jax.ShapeDtypeStruct((t, D_MODEL), F32), name="final",
    )(x1, peer_out, mod, final_g.reshape(1, -1))


def _expert(idx, h2, gates, x1, mod, rows_per_batch, final_g, expert_u, expert_v, tm):
    pre = _peer_sc(_peer_u_body, idx, h2, expert_u, PEER_HK, "peer_u")
    coef = _coef(pre, gates, tm)
    peer_out = _peer_sc(_peer_v_body, idx, coef, expert_v, D_MODEL, "peer_v")
    return _final(x1, peer_out, mod, rows_per_batch, final_g, tm)


def _group(x, mod, conv_buf, s0, pool_buf, start, chunk, tm, wts):
    b, l, _ = x.shape
    t = b * l
    x2d = x.reshape(t, D_MODEL)
    if l >= tm:
        modx = mod.reshape(b, 6, 1, D_MODEL).transpose(1, 0, 2, 3)
    else:
        modx = jnp.repeat(mod.reshape(b, 6, D_MODEL), l, axis=0).transpose(1, 0, 2)
    outs = _inproj(x2d, modx, l, wts["norm1_g"], wts["w_cat"], tm)
    lp = -(-l // chunk) * chunk
    proj = {}
    for (name, w), a in zip(_IN_BLOCKS, outs):
        a = a.reshape(b, l, w)
        proj[name] = a if lp == l else jnp.pad(a, ((0, 0), (0, lp - l), (0, 0)))
    mixed, nconv, ns, npool = _mixer(proj, conv_buf, s0, pool_buf, start, l, chunk,
                                     wts["conv_w"], wts["a_log"], wts["dt_bias"], wts["dn_norm_g"],
                                     wts["w_pool"], wts["pool_scale"])
    mixed2d = mixed[:, :l].reshape(t, D_MODEL)
    x1, h2, idx, gates = _post(mixed2d, x2d, modx, l, wts["norm2_g"], wts["w_out"], wts["w_query"],
                               wts["keys"], tm)
    y = _expert(idx, h2, gates, x1, modx, l, wts["final_norm_g"], wts["expert_u"], wts["expert_v"], tm)
    return y.reshape(b, l, D_MODEL), nconv, ns, npool


def kernel(x_prompt, x_sample, c_prompt, c_sample, state_conv, state_delta, state_pool, w_ada, b_ada, norm1_g, w_in, conv_w, a_log, dt_bias, dn_norm_g, w_pool, pool_scale, w_out, norm2_g, w_query, sub_keys, expert_u, expert_v, final_norm_g):
    bp = x_prompt.shape[0]
    bs = x_sample.shape[0]
    yp, ys = x_prompt, x_sample
    conv_p, delta_p, pool_p, conv_s, delta_s, pool_s = [], [], [], [], [], []
    zero_conv = jnp.zeros((bp, CONV_WIDTH - 1, QKV_WIDTH), F32)
    zero_delta = jnp.zeros((bp, DN_HEADS, DN_HEAD_DIM, DN_HEAD_DIM), F32)
    zero_pool = jnp.zeros((bp, POOL_BUF, POOL_WIDTH), F32)
    c_all = jnp.concatenate([c_prompt, c_sample], axis=0)
    for layer in range(DEPTH):
        wi = w_in[layer]
        o_b = QKV_WIDTH
        o_z = o_b + 2 * DN_HEADS
        w_ba = jnp.pad(wi[:, o_b:o_z], ((0, 0), (0, LANES - 2 * DN_HEADS)))
        w_cat = jnp.concatenate([wi[:, :o_b], wi[:, o_z:], w_ba], axis=1).astype(BF16)
        last = layer == DEPTH - 1
        wts = dict(
            norm1_g=norm1_g[layer], w_cat=w_cat, conv_w=conv_w[layer], a_log=a_log[layer],
            dt_bias=dt_bias[layer], dn_norm_g=dn_norm_g[layer], w_pool=w_pool[layer],
            pool_scale=pool_scale[layer], w_out=w_out[layer].astype(BF16), norm2_g=norm2_g[layer],
            w_query=w_query[layer].astype(BF16),
            keys=sub_keys[layer].reshape(2 * PEER_HEADS, PEER_NKEYS, PEER_KEY_HALF).astype(BF16),
            expert_u=expert_u[layer], expert_v=expert_v[layer],
            final_norm_g=final_norm_g if last else jnp.ones_like(final_norm_g))
        mod = _ada(c_all, w_ada[layer], b_ada[layer])
        assert last, "final norm is fused into the expert stage"
        yp, cp, sp, pp = _group(yp, mod[:bp], zero_conv, zero_delta, zero_pool, 0, DN_CHUNK, 256, wts)
        ys, cs, ss, ps = _group(ys, mod[bp:], state_conv[layer], state_delta[layer], state_pool[layer],
                                PAST_LEN, SUBLANES, 256, wts)
        conv_p.append(cp)
        delta_p.append(sp)
        pool_p.append(pp)
        conv_s.append(cs)
        delta_s.append(ss)
        pool_s.append(ps)
    return (yp, ys, jnp.stack(conv_p), jnp.stack(delta_p), jnp.stack(pool_p),
            jnp.stack(conv_s), jnp.stack(delta_s), jnp.stack(pool_s))
```

```python
import functools

import jax
import jax.numpy as jnp
from jax import lax
from jax.experimental import pallas as pl
from jax.experimental.pallas import tpu as pltpu
from jax.experimental.pallas import tpu_sc as plsc

F32 = jnp.float32
BF16 = jnp.bfloat16
I32 = jnp.int32

D_MODEL = 1024
DEPTH = 1
PAST_LEN = 16384
DN_HEADS = 8
DN_HEAD_DIM = 128
DN_WIDTH = DN_HEADS * DN_HEAD_DIM
QKV_WIDTH = 3 * DN_WIDTH
CONV_WIDTH = 4
DN_CHUNK = 64
POOL_WINDOWS = (2, 4, 8, 16)
POOL_GROUP_DIM = 128
POOL_WIDTH = len(POOL_WINDOWS) * POOL_GROUP_DIM
POOL_OUT_GROUP = D_MODEL // len(POOL_WINDOWS)
POOL_BUF = max(POOL_WINDOWS) - 1
PEER_HEADS = 8
PEER_NKEYS = 128
PEER_TOPK = 16
PEER_KEY_HALF = 128
PEER_HK = PEER_HEADS * PEER_TOPK
EPS = 1e-6

LANES = 128
SUBLANES = 8
CONV_PAD = SUBLANES
POOL_PAD = 16
VMEM_LIMIT = 56 * 1024 * 1024

NT_DIMS = (((1,), (1,)), ((), ()))
TN_DIMS = (((0,), (0,)), ((), ()))


def _dot(a, b):
    return jnp.dot(a.astype(BF16), b.astype(BF16), preferred_element_type=F32)


def _dot_nt(a, b):
    return lax.dot_general(a.astype(BF16), b.astype(BF16), NT_DIMS, preferred_element_type=F32)


def _split3(x):
    hi = x.astype(BF16)
    r1 = x - hi.astype(F32)
    mid = r1.astype(BF16)
    lo = (r1 - mid.astype(F32)).astype(BF16)
    return hi, mid, lo


def _silu(x):
    return x * jax.nn.sigmoid(x)


def _gelu(x):
    return 0.5 * x * (1.0 + lax.erf(x * (0.5 ** 0.5)))


def _softplus(x):
    return jnp.maximum(x, 0.0) + jnp.log(1.0 + jnp.exp(-jnp.abs(x)))


def _mod_rows(ref):
    m = ref[...]
    return m.reshape(m.shape[-2], m.shape[-1])


def _mod_spec(k, rows_per_batch, tm):
    if rows_per_batch >= tm:
        tiles = rows_per_batch // tm
        return pl.BlockSpec((1, 1, 1, D_MODEL), lambda i, *_: (k, i // tiles, 0, 0))
    return pl.BlockSpec((1, tm, D_MODEL), lambda i, *_: (k, i, 0))


def _const_spec(shape):
    nd = len(shape)
    return pl.BlockSpec(shape, lambda *_: (0,) * nd)


def _ada_body(c_ref, w_ref, b_ref, o_ref):
    o_ref[...] = _dot(_silu(c_ref[...]), w_ref[...]) + b_ref[...]


def _ada(c, w_ada, b_ada):
    n = c.shape[0]
    return pl.pallas_call(
        _ada_body,
        grid=(6,),
        in_specs=[pl.BlockSpec((n, D_MODEL), lambda j: (0, 0)),
                  pl.BlockSpec((D_MODEL, D_MODEL), lambda j: (0, j)),
                  pl.BlockSpec((1, D_MODEL), lambda j: (0, j))],
        out_specs=pl.BlockSpec((n, D_MODEL), lambda j: (0, j)),
        out_shape=jax.ShapeDtypeStruct((n, 6 * D_MODEL), F32),
        name="ada",
    )(c, w_ada, b_ada.reshape(1, -1))


_IN_BLOCKS = (("qkv", QKV_WIDTH), ("z", DN_WIDTH), ("pool", POOL_WIDTH),
              ("ga", D_MODEL), ("gb", D_MODEL), ("ba", LANES))
_IN_TOTAL = sum(w for _, w in _IN_BLOCKS)
_IN_COL_CHUNK = 512


def _inproj_body(x_ref, sc_ref, sh_ref, g_ref, w_ref, *out_refs):
    x = x_ref[...]
    y = x * lax.rsqrt(jnp.mean(x * x, axis=-1, keepdims=True) + EPS) * g_ref[...]
    h = (y * (1.0 + _mod_rows(sc_ref)) + _mod_rows(sh_ref)).astype(BF16)
    off = 0
    for (_, width), o_ref in zip(_IN_BLOCKS, out_refs):
        for c0 in range(0, width, _IN_COL_CHUNK):
            cw = min(_IN_COL_CHUNK, width - c0)
            o_ref[:, c0:c0 + cw] = jnp.dot(h, w_ref[:, off + c0:off + c0 + cw],
                                           preferred_element_type=F32)
        off += width


def _inproj(x2d, mod, rows_per_batch, norm_g, w_cat, tm):
    t = x2d.shape[0]
    row = lambda w: pl.BlockSpec((tm, w), lambda i: (i, 0))
    return pl.pallas_call(
        _inproj_body,
        grid=(t // tm,),
        in_specs=[row(D_MODEL), _mod_spec(1, rows_per_batch, tm), _mod_spec(0, rows_per_batch, tm),
                  _const_spec((1, D_MODEL)),
                  pl.BlockSpec((D_MODEL, _IN_TOTAL), lambda i: (0, 0), pipeline_mode=pl.Buffered(1))],
        out_specs=[row(w) for _, w in _IN_BLOCKS],
        out_shape=[jax.ShapeDtypeStruct((t, w), F32) for _, w in _IN_BLOCKS],
        compiler_params=pltpu.CompilerParams(vmem_limit_bytes=VMEM_LIMIT),
        name="inproj",
    )(x2d, mod, mod, norm_g.reshape(1, -1), w_cat)


def _mixer_body(C, Lv, start,
                qkv_ref, ba_ref, z_ref, pin_ref, ga_ref, gb_ref, cbuf_ref, s0_ref, pbuf_ref,
                convw_ref, alog_ref, dtb_ref, dng_ref, wpool_ref, pscale_ref,
                mixed_ref, nconv_ref, ns_ref, npool_ref,
                xp_scr, act_scr, s_scr, pp_scr, odn_scr):
    n = pl.program_id(1)
    last = pl.num_programs(1) - 1

    @pl.when(n == 0)
    def _load_state():
        xp_scr[0:CONV_PAD, :] = cbuf_ref[0]
        pp_scr[0:POOL_PAD, :] = pbuf_ref[0]
        s_scr[...] = s0_ref[0]

    xp_scr[CONV_PAD:CONV_PAD + C, :] = qkv_ref[0]
    for c0 in range(0, QKV_WIDTH, 512):
        cs = slice(c0, c0 + 512)
        y = xp_scr[CONV_PAD:CONV_PAD + C, cs] * convw_ref[CONV_WIDTH - 1:CONV_WIDTH, cs]
        for k in range(CONV_WIDTH - 1):
            r0 = CONV_PAD - (CONV_WIDTH - 1) + k
            y = y + xp_scr[r0:r0 + C, cs] * convw_ref[k:k + 1, cs]
        act_scr[:, cs] = _silu(y)

    ba = ba_ref[0]
    lane = lax.broadcasted_iota(I32, (C, LANES), 1)
    beta_all = jax.nn.sigmoid(ba)
    g_all = -jnp.exp(alog_ref[...]) * _softplus(ba + dtb_ref[...])
    if Lv < C:
        valid = lax.broadcasted_iota(I32, (C, LANES), 0) < Lv
        beta_all = jnp.where(valid, beta_all, 0.0)
        g_all = jnp.where(valid, g_all, 0.0)
    ii = lax.broadcasted_iota(I32, (C, C), 0)
    jj = lax.broadcasted_iota(I32, (C, C), 1)
    causal = ii >= jj
    strict = ii > jj
    tril = jnp.where(causal, 1.0, 0.0).astype(BF16)
    eye = jnp.where(ii == jj, 1.0, 0.0)
    gc_all = sum(jnp.dot(tril, part, preferred_element_type=F32) for part in _split3(g_all))
    if C < LANES:
        gc_sq = jnp.concatenate([gc_all, jnp.zeros((LANES - C, LANES), F32)], axis=0)
    else:
        gc_sq = gc_all
    gc_t = gc_sq.T

    for h in range(DN_HEADS):
        hs = slice(h * DN_HEAD_DIM, (h + 1) * DN_HEAD_DIM)
        beta = jnp.sum(jnp.where(lane == h, beta_all, 0.0), axis=1, keepdims=True)
        gcol = jnp.sum(jnp.where(lane == DN_HEADS + h, gc_all, 0.0), axis=1, keepdims=True)
        grow = gc_t[DN_HEADS + h:DN_HEADS + h + 1, 0:C]
        glast = gcol[C - 1:C, :]

        q = act_scr[:, hs]
        k = act_scr[:, DN_WIDTH + h * DN_HEAD_DIM:DN_WIDTH + (h + 1) * DN_HEAD_DIM]
        v = act_scr[:, 2 * DN_WIDTH + h * DN_HEAD_DIM:2 * DN_WIDTH + (h + 1) * DN_HEAD_DIM]
        q = q * lax.rsqrt(jnp.sum(q * q, axis=-1, keepdims=True) + EPS) * (DN_HEAD_DIM ** -0.5)
        k = k * lax.rsqrt(jnp.sum(k * k, axis=-1, keepdims=True) + EPS)
        kb = k * beta
        vb = v * beta

        decay = jnp.where(causal, jnp.exp(jnp.where(causal, gcol - grow, 0.0)), 0.0)
        lower = jnp.where(strict, _dot_nt(kb, k) * decay, 0.0)
        ainv = eye - lower
        pw = lower
        p = 1
        while 2 * p < C:
            pw = _dot(pw, pw)
            ainv = ainv + _dot(ainv, pw)
            p *= 2
        sol = _dot(ainv, jnp.concatenate([vb, kb * jnp.exp(gcol)], axis=1))
        u = sol[:, :DN_HEAD_DIM]
        w = sol[:, DN_HEAD_DIM:]
        qk = _dot_nt(q, k) * decay
        k_tail = k * jnp.exp(glast - gcol)

        S = s_scr[h]
        v_new = u - _dot(w, S)
        o = _dot(q * jnp.exp(gcol), S) + _dot(qk, v_new)
        s_scr[h] = S * jnp.exp(glast) + lax.dot_general(
            k_tail.astype(BF16), v_new.astype(BF16), TN_DIMS, preferred_element_type=F32)

        zf = z_ref[0, :, hs]
        o = o * lax.rsqrt(jnp.mean(o * o, axis=-1, keepdims=True) + EPS) * dng_ref[...] * _silu(zf)
        odn_scr[:, hs] = o

    pp_scr[POOL_PAD:POOL_PAD + C, :] = pin_ref[0]
    pos = start + n * C + lax.broadcasted_iota(I32, (C, 1), 0)
    for gi, win in enumerate(POOL_WINDOWS):
        gs = slice(gi * POOL_GROUP_DIM, (gi + 1) * POOL_GROUP_DIM)
        xg = pp_scr[POOL_PAD:POOL_PAD + C, gs]
        ssum = xg
        for sft in range(1, win):
            ssum = ssum + pp_scr[POOL_PAD - sft:POOL_PAD - sft + C, gs]
        cnt = jnp.minimum(pos + 1, win).astype(F32)
        pooled = ssum / cnt - xg
        os_ = slice(gi * POOL_OUT_GROUP, (gi + 1) * POOL_OUT_GROUP)
        yp = _dot(pooled, wpool_ref[gi]) * pscale_ref[:, os_]
        mixed_ref[0, :, os_] = (jax.nn.sigmoid(ga_ref[0, :, os_]) * odn_scr[:, os_]
                                + jax.nn.sigmoid(gb_ref[0, :, os_]) * yp)

    @pl.when(n == last)
    def _store_state():
        nconv_ref[0] = xp_scr[Lv + CONV_PAD - (CONV_WIDTH - 1):Lv + CONV_PAD, :]
        npool_ref[0] = pp_scr[Lv + POOL_PAD - POOL_BUF:Lv + POOL_PAD, :]
        ns_ref[0] = s_scr[...]

    xp_scr[0:CONV_PAD, :] = xp_scr[C:C + CONV_PAD, :]
    pp_scr[0:POOL_PAD, :] = pp_scr[C:C + POOL_PAD, :]


def _mixer(proj, conv_buf, s0, pool_buf, start, seq_len, C,
           conv_w, a_log, dt_bias, dn_norm_g, w_pool, pool_scale):
    b, lp, _ = proj["qkv"].shape
    nchunks = lp // C
    lv = seq_len - (nchunks - 1) * C
    cbuf = jnp.pad(conv_buf, ((0, 0), (CONV_PAD - (CONV_WIDTH - 1), 0), (0, 0)))
    pbuf = jnp.pad(pool_buf, ((0, 0), (POOL_PAD - POOL_BUF, 0), (0, 0)))
    lane_pad = lambda a: jnp.pad(a.reshape(1, -1), ((0, 0), (DN_HEADS, LANES - 2 * DN_HEADS)))
    chunk = lambda w: pl.BlockSpec((1, C, w), lambda i, j: (i, j, 0))
    state = lambda *s: pl.BlockSpec((1,) + s, lambda i, j: (i,) + (0,) * len(s))
    return pl.pallas_call(
        functools.partial(_mixer_body, C, lv, start),
        grid=(b, nchunks),
        in_specs=[chunk(QKV_WIDTH), chunk(LANES), chunk(DN_WIDTH), chunk(POOL_WIDTH),
                  chunk(D_MODEL), chunk(D_MODEL),
                  state(CONV_PAD, QKV_WIDTH), state(DN_HEADS, DN_HEAD_DIM, DN_HEAD_DIM),
                  state(POOL_PAD, POOL_WIDTH),
                  _const_spec((CONV_WIDTH, QKV_WIDTH)), _const_spec((1, LANES)), _const_spec((1, LANES)),
                  _const_spec((1, DN_HEAD_DIM)),
                  _const_spec((len(POOL_WINDOWS), POOL_GROUP_DIM, POOL_OUT_GROUP)),
                  _const_spec((1, D_MODEL))],
        out_specs=[chunk(D_MODEL), state(CONV_WIDTH - 1, QKV_WIDTH),
                   state(DN_HEADS, DN_HEAD_DIM, DN_HEAD_DIM), state(POOL_BUF, POOL_WIDTH)],
        out_shape=[jax.ShapeDtypeStruct((b, lp, D_MODEL), F32),
                   jax.ShapeDtypeStruct((b, CONV_WIDTH - 1, QKV_WIDTH), F32),
                   jax.ShapeDtypeStruct((b, DN_HEADS, DN_HEAD_DIM, DN_HEAD_DIM), F32),
                   jax.ShapeDtypeStruct((b, POOL_BUF, POOL_WIDTH), F32)],
        scratch_shapes=[pltpu.VMEM((CONV_PAD + C + CONV_PAD, QKV_WIDTH), F32),
                        pltpu.VMEM((C, QKV_WIDTH), F32),
                        pltpu.VMEM((DN_HEADS, DN_HEAD_DIM, DN_HEAD_DIM), F32),
                        pltpu.VMEM((POOL_PAD + C + POOL_PAD, POOL_WIDTH), F32),
                        pltpu.VMEM((C, DN_WIDTH), F32)],
        compiler_params=pltpu.CompilerParams(dimension_semantics=("arbitrary", "arbitrary"),
                                             vmem_limit_bytes=VMEM_LIMIT),
        name="mixer",
    )(proj["qkv"], proj["ba"], proj["z"], proj["pool"], proj["ga"], proj["gb"], cbuf, s0, pbuf,
      conv_w, lane_pad(a_log), lane_pad(dt_bias), dn_norm_g.reshape(1, -1), w_pool,
      pool_scale.reshape(1, -1))


def _top16(s, ids, payload=None):
    big = float(2 ** 24)
    vals, sel, pays = [], [], []
    for _ in range(PEER_TOPK):
        m = jnp.max(s, axis=0, keepdims=True)
        am = jnp.min(jnp.where(s == m, ids, big), axis=0, keepdims=True)
        hit = ids == am
        if payload is not None:
            pays.append(jnp.max(jnp.where(hit, payload, -1.0), axis=0, keepdims=True))
        s = jnp.where(hit, -jnp.inf, s)
        vals.append(m)
        sel.append(am)
    out = (jnp.concatenate(vals, axis=0), jnp.concatenate(sel, axis=0))
    if payload is not None:
        out += (jnp.concatenate(pays, axis=0),)
    return out


_CAND_EDGE = 4


def _post_body(mixed_ref, x_ref, g1_ref, sc2_ref, sh2_ref, n2g_ref, wout_ref, wq_ref, keys_ref,
               x1_ref, h2_ref, idx_ref, gate_ref):
    tm = x_ref.shape[0]
    x1 = x_ref[...] + _mod_rows(g1_ref) * _dot(mixed_ref[...], wout_ref[...])
    x1_ref[...] = x1
    y = x1 * lax.rsqrt(jnp.mean(x1 * x1, axis=-1, keepdims=True) + EPS) * n2g_ref[...]
    h2 = y * (1.0 + _mod_rows(sc2_ref)) + _mod_rows(sh2_ref)
    h2_ref[...] = h2
    q = _dot(h2, wq_ref[...])

    K = PEER_TOPK
    key_id = lax.broadcasted_iota(I32, (PEER_NKEYS, 1), 0).astype(F32)
    r16 = lax.broadcasted_iota(I32, (K, 1), 0)
    cand_id = jnp.concatenate([(a * K + r16) for a in range(_CAND_EDGE)]
                              + [(r16 * K + b) for b in range(_CAND_EDGE)], axis=0).astype(F32)
    dup = r16 < _CAND_EDGE
    idx_rows, gate_rows = [], []
    for h in range(PEER_HEADS):
        half = []
        for p in range(2):
            c0 = (h * 2 + p) * PEER_KEY_HALF
            st = _dot_nt(keys_ref[h * 2 + p], q[:, c0:c0 + PEER_KEY_HALF])
            half.append(_top16(st, key_id))
        (s1, i1), (s2, i2) = half
        cand = jnp.concatenate(
            [s1[a:a + 1] + s2 for a in range(_CAND_EDGE)]
            + [jnp.where(dup, -jnp.inf, s1 + s2[b:b + 1]) for b in range(_CAND_EDGE)], axis=0)
        cidx = jnp.concatenate(
            [i1[a:a + 1] * PEER_NKEYS + i2 for a in range(_CAND_EDGE)]
            + [i1 * PEER_NKEYS + i2[b:b + 1] for b in range(_CAND_EDGE)], axis=0)
        best, _, eidx = _top16(cand, cand_id, cidx)
        e = jnp.exp(best - best[0:1])
        gate_rows.append(e / jnp.sum(e, axis=0, keepdims=True))
        idx_rows.append(eidx)
    idx_ref[...] = jnp.concatenate(idx_rows, axis=0).T.astype(I32)
    gate_ref[...] = jnp.concatenate(gate_rows, axis=0).T


def _post(mixed2d, x2d, mod, rows_per_batch, norm2_g, w_out, w_query, keys, tm):
    t = x2d.shape[0]
    row = lambda w: pl.BlockSpec((tm, w), lambda i: (i, 0))
    return pl.pallas_call(
        _post_body,
        grid=(t // tm,),
        in_specs=[row(D_MODEL), row(D_MODEL),
                  _mod_spec(2, rows_per_batch, tm), _mod_spec(4, rows_per_batch, tm),
                  _mod_spec(3, rows_per_batch, tm), _const_spec((1, D_MODEL)),
                  _const_spec((D_MODEL, D_MODEL)), _const_spec((D_MODEL, 2 * PEER_HEADS * PEER_KEY_HALF)),
                  _const_spec((2 * PEER_HEADS, PEER_NKEYS, PEER_KEY_HALF))],
        out_specs=[row(D_MODEL), row(D_MODEL), row(PEER_HK), row(PEER_HK)],
        out_shape=[jax.ShapeDtypeStruct((t, D_MODEL), F32), jax.ShapeDtypeStruct((t, D_MODEL), F32),
                   jax.ShapeDtypeStruct((t, PEER_HK), I32), jax.ShapeDtypeStruct((t, PEER_HK), F32)],
        compiler_params=pltpu.CompilerParams(vmem_limit_bytes=VMEM_LIMIT),
        name="post",
    )(mixed2d, x2d, mod, mod, mod, norm2_g.reshape(1, -1), w_out, w_query, keys)


SC_CORES = 2
SC_SUBCORES = 16
SC_LANES = 16
SC_WORKERS = SC_CORES * SC_SUBCORES
SC_TOKENS = 16
SC_SLOTS = 4
SC_CHUNKS = D_MODEL // SC_LANES


def _sc_mesh():
    return plsc.VectorSubcoreMesh(core_axis_name="c", subcore_axis_name="s")


def _sc_worker():
    return lax.axis_index("s") * SC_CORES + lax.axis_index("c")


def _sc_jobs(table_hbm, idx_v, buf, sem, compute):
    njobs = SC_TOKENS * PEER_HEADS

    def copy(j, slot):
        tt = j // PEER_HEADS
        h = j % PEER_HEADS
        rows = idx_v[tt, pl.ds(h * PEER_TOPK, PEER_TOPK)]
        return pltpu.make_async_copy(table_hbm.at[rows], buf.at[slot], sem.at[slot])

    for s in range(SC_SLOTS):
        copy(s, s).start()

    def group(g, c):
        for s in range(SC_SLOTS):
            j = g * SC_SLOTS + s
            copy(j, s).wait()
            compute(j // PEER_HEADS, j % PEER_HEADS, s)

            @pl.when(j + SC_SLOTS < njobs)
            def _next():
                copy(j + SC_SLOTS, s).start()
        return c

    lax.fori_loop(0, njobs // SC_SLOTS, group, 0)


def _peer_u_body(n_tok, idx_hbm, h2_hbm, u_hbm, pre_hbm, idx_v, h2_v, pre_v, ubuf, acc_v, sem):
    base = _sc_worker() * n_tok
    lane = lax.iota(I32, SC_LANES)

    def compute(tt, h, slot):
        def chunk(c, accs):
            xv = h2_v[tt, pl.ds(c * SC_LANES, SC_LANES)]
            return tuple(a + ubuf[slot, k, pl.ds(c * SC_LANES, SC_LANES)] * xv
                         for k, a in enumerate(accs))
        zero = jnp.zeros((SC_LANES,), F32)
        accs = lax.fori_loop(0, SC_CHUNKS, chunk, (zero,) * PEER_TOPK)
        for k, a in enumerate(accs):
            acc_v[k, :] = a
        tot = zero
        for j in range(SC_LANES):
            tot = tot + plsc.load_gather(acc_v, [lane, jnp.full((SC_LANES,), j, I32)])
        pre_v[tt, pl.ds(h * PEER_TOPK, PEER_TOPK)] = tot

    def block(bi, c):
        t0 = base + bi * SC_TOKENS
        pltpu.sync_copy(idx_hbm.at[pl.ds(t0, SC_TOKENS)], idx_v)
        pltpu.sync_copy(h2_hbm.at[pl.ds(t0, SC_TOKENS)], h2_v)
        _sc_jobs(u_hbm, idx_v, ubuf, sem, compute)
        pltpu.sync_copy(pre_v, pre_hbm.at[pl.ds(t0, SC_TOKENS)])
        return c

    lax.fori_loop(0, n_tok // SC_TOKENS, block, 0)


def _peer_v_body(n_tok, idx_hbm, coef_hbm, v_hbm, out_hbm, idx_v, coef_v, out_v, vbuf, sem):
    base = _sc_worker() * n_tok
    zero = jnp.zeros((SC_LANES,), F32)

    def compute(tt, h, slot):
        row = jnp.full((SC_LANES,), tt, I32)
        cb = [plsc.load_gather(coef_v, [row, jnp.full((SC_LANES,), h * PEER_TOPK + k, I32)])
              for k in range(PEER_TOPK)]

        @plsc.parallel_loop(0, SC_CHUNKS, unroll=2)
        def _chunk(c):
            cs = pl.ds(c * SC_LANES, SC_LANES)
            terms = [cb[k] * vbuf[slot, k, cs] for k in range(PEER_TOPK)]
            while len(terms) > 1:
                terms = [a + b for a, b in zip(terms[0::2], terms[1::2])]
            plsc.addupdate(out_v.at[tt, cs], terms[0])

    def block(bi, c):
        t0 = base + bi * SC_TOKENS
        pltpu.sync_copy(idx_hbm.at[pl.ds(t0, SC_TOKENS)], idx_v)
        pltpu.sync_copy(coef_hbm.at[pl.ds(t0, SC_TOKENS)], coef_v)

        def clear(i, cc):
            out_v[i // SC_CHUNKS, pl.ds((i % SC_CHUNKS) * SC_LANES, SC_LANES)] = zero
            return cc
        lax.fori_loop(0, SC_TOKENS * SC_CHUNKS, clear, 0)
        _sc_jobs(v_hbm, idx_v, vbuf, sem, compute)
        pltpu.sync_copy(out_v, out_hbm.at[pl.ds(t0, SC_TOKENS)])
        return c

    lax.fori_loop(0, n_tok // SC_TOKENS, block, 0)


def _peer_sc(body, idx, rows, table, out_width, name):
    t = idx.shape[0]
    assert t % (SC_WORKERS * SC_TOKENS) == 0
    n_tok = t // SC_WORKERS
    return pl.kernel(
        functools.partial(body, n_tok),
        out_type=jax.ShapeDtypeStruct((t, out_width), F32),
        mesh=_sc_mesh(),
        scratch_types=[pltpu.VMEM((SC_TOKENS, PEER_HK), I32),
                       pltpu.VMEM((SC_TOKENS, rows.shape[1]), F32),
                       pltpu.VMEM((SC_TOKENS, out_width), F32),
                       pltpu.VMEM((SC_SLOTS, PEER_TOPK, D_MODEL), F32)]
                      + ([pltpu.VMEM((PEER_TOPK, SC_LANES), F32)] if body is _peer_u_body else [])
                      + [pltpu.SemaphoreType.DMA((SC_SLOTS,))],
        compiler_params=pltpu.CompilerParams(needs_layout_passes=False),
        name=name,
    )(idx, rows, table)


def _coef_body(pre_ref, gate_ref, coef_ref):
    coef_ref[...] = gate_ref[...] * _gelu(pre_ref[...])


def _coef(pre, gates, tm):
    t = pre.shape[0]
    row = pl.BlockSpec((tm, PEER_HK), lambda i: (i, 0))
    return pl.pallas_call(_coef_body, grid=(t // tm,), in_specs=[row, row], out_specs=row,
                          out_shape=jax.ShapeDtypeStruct((t, PEER_HK), F32), name="coef")(pre, gates)


def _final_body(x1_ref, peer_ref, g2_ref, fng_ref, y_ref):
    x2 = x1_ref[...] + _mod_rows(g2_ref) * peer_ref[...]
    y_ref[...] = x2 * lax.rsqrt(jnp.mean(x2 * x2, axis=-1, keepdims=True) + EPS) * fng_ref[...]


def _final(x1, peer_out, mod, rows_per_batch, final_g, tm):
    t = x1.shape[0]
    row = pl.BlockSpec((tm, D_MODEL), lambda i: (i, 0))
    return pl.pallas_call(
        _final_body, grid=(t // tm,),
        in_specs=[row, row, _mod_spec(5, rows_per_batch, tm), _const_spec((1, D_MODEL))],
        out_specs=row, out_shape=jax.ShapeDtypeStruct((t, D_MODEL), F32), name="final",
    )(x1, peer_out, mod, final_g.reshape(1, -1))


def _expert(idx, h2, gates, x1, mod, rows_per_batch, final_g, expert_u, expert_v, tm):
    pre = _peer_sc(_peer_u_body, idx, h2, expert_u, PEER_HK, "peer_u")
    coef = _coef(pre, gates, tm)
    peer_out = _peer_sc(_peer_v_body, idx, coef, expert_v, D_MODEL, "peer_v")
    return _final(x1, peer_out, mod, rows_per_batch, final_g, tm)


def _group(x, mod, conv_buf, s0, pool_buf, start, chunk, tm, wts):
    b, l, _ = x.shape
    t = b * l
    x2d = x.reshape(t, D_MODEL)
    if l >= tm:
        modx = mod.reshape(b, 6, 1, D_MODEL).transpose(1, 0, 2, 3)
    else:
        modx = jnp.repeat(mod.reshape(b, 6, D_MODEL), l, axis=0).transpose(1, 0, 2)
    outs = _inproj(x2d, modx, l, wts["norm1_g"], wts["w_cat"], tm)
    lp = -(-l // chunk) * chunk
    proj = {}
    for (name, w), a in zip(_IN_BLOCKS, outs):
        a = a.reshape(b, l, w)
        proj[name] = a if lp == l else jnp.pad(a, ((0, 0), (0, lp - l), (0, 0)))
    mixed, nconv, ns, npool = _mixer(proj, conv_buf, s0, pool_buf, start, l, chunk,
                                     wts["conv_w"], wts["a_log"], wts["dt_bias"], wts["dn_norm_g"],
                                     wts["w_pool"], wts["pool_scale"])
    mixed2d = mixed[:, :l].reshape(t, D_MODEL)
    x1, h2, idx, gates = _post(mixed2d, x2d, modx, l, wts["norm2_g"], wts["w_out"], wts["w_query"],
                               wts["keys"], tm)
    y = _expert(idx, h2, gates, x1, modx, l, wts["final_norm_g"], wts["expert_u"], wts["expert_v"], tm)
    return y.reshape(b, l, D_MODEL), nconv, ns, npool


def kernel(x_prompt, x_sample, c_prompt, c_sample, state_conv, state_delta, state_pool, w_ada, b_ada, norm1_g, w_in, conv_w, a_log, dt_bias, dn_norm_g, w_pool, pool_scale, w_out, norm2_g, w_query, sub_keys, expert_u, expert_v, final_norm_g):
    bp = x_prompt.shape[0]
    bs = x_sample.shape[0]
    yp, ys = x_prompt, x_sample
    conv_p, delta_p, pool_p, conv_s, delta_s, pool_s = [], [], [], [], [], []
    zero_conv = jnp.zeros((bp, CONV_WIDTH - 1, QKV_WIDTH), F32)
    zero_delta = jnp.zeros((bp, DN_HEADS, DN_HEAD_DIM, DN_HEAD_DIM), F32)
    zero_pool = jnp.zeros((bp, POOL_BUF, POOL_WIDTH), F32)
    c_all = jnp.concatenate([c_prompt, c_sample], axis=0)
    for layer in range(DEPTH):
        wi = w_in[layer]
        o_b = QKV_WIDTH
        o_z = o_b + 2 * DN_HEADS
        w_ba = jnp.pad(wi[:, o_b:o_z], ((0, 0), (0, LANES - 2 * DN_HEADS)))
        w_cat = jnp.concatenate([wi[:, :o_b], wi[:, o_z:], w_ba], axis=1).astype(BF16)
        last = layer == DEPTH - 1
        wts = dict(
            norm1_g=norm1_g[layer], w_cat=w_cat, conv_w=conv_w[layer], a_log=a_log[layer],
            dt_bias=dt_bias[layer], dn_norm_g=dn_norm_g[layer], w_pool=w_pool[layer],
            pool_scale=pool_scale[layer], w_out=w_out[layer].astype(BF16), norm2_g=norm2_g[layer],
            w_query=w_query[layer].astype(BF16),
            keys=sub_keys[layer].reshape(2 * PEER_HEADS, PEER_NKEYS, PEER_KEY_HALF).astype(BF16),
            expert_u=expert_u[layer], expert_v=expert_v[layer],
            final_norm_g=final_norm_g if last else jnp.ones_like(final_norm_g))
        mod = _ada(c_all, w_ada[layer], b_ada[layer])
        assert last, "final norm is fused into the expert stage"
        yp, cp, sp, pp = _group(yp, mod[:bp], zero_conv, zero_delta, zero_pool, 0, DN_CHUNK, 256, wts)
        ys, cs, ss, ps = _group(ys, mod[bp:], state_conv[layer], state_delta[layer], state_pool[layer],
                                PAST_LEN, SUBLANES, 256, wts)
        conv_p.append(cp)
        delta_p.append(sp)
        pool_p.append(pp)
        conv_s.append(cs)
        delta_s.append(ss)
        pool_s.append(ps)
    return (yp, ys, jnp.stack(conv_p), jnp.stack(delta_p), jnp.stack(pool_p),
            jnp.stack(conv_s), jnp.stack(delta_s), jnp.stack(pool_s))
```

```python
import functools

import jax
import jax.numpy as jnp
from jax import lax
from jax.experimental import pallas as pl
from jax.experimental.pallas import tpu as pltpu
from jax.experimental.pallas import tpu_sc as plsc

F32 = jnp.float32
BF16 = jnp.bfloat16
I32 = jnp.int32

D_MODEL = 1024
DEPTH = 1
PAST_LEN = 16384
DN_HEADS = 8
DN_HEAD_DIM = 128
DN_WIDTH = DN_HEADS * DN_HEAD_DIM
QKV_WIDTH = 3 * DN_WIDTH
CONV_WIDTH = 4
DN_CHUNK = 64
POOL_WINDOWS = (2, 4, 8, 16)
POOL_GROUP_DIM = 128
POOL_WIDTH = len(POOL_WINDOWS) * POOL_GROUP_DIM
POOL_OUT_GROUP = D_MODEL // len(POOL_WINDOWS)
POOL_BUF = max(POOL_WINDOWS) - 1
PEER_HEADS = 8
PEER_NKEYS = 128
PEER_TOPK = 16
PEER_KEY_HALF = 128
PEER_HK = PEER_HEADS * PEER_TOPK
EPS = 1e-6

LANES = 128
SUBLANES = 8
CONV_PAD = SUBLANES
POOL_PAD = 16
VMEM_LIMIT = 56 * 1024 * 1024

NT_DIMS = (((1,), (1,)), ((), ()))
TN_DIMS = (((0,), (0,)), ((), ()))


def _dot(a, b):
    return jnp.dot(a.astype(BF16), b.astype(BF16), preferred_element_type=F32)


def _dot_nt(a, b):
    return lax.dot_general(a.astype(BF16), b.astype(BF16), NT_DIMS, preferred_element_type=F32)


def _split3(x):
    hi = x.astype(BF16)
    r1 = x - hi.astype(F32)
    mid = r1.astype(BF16)
    lo = (r1 - mid.astype(F32)).astype(BF16)
    return hi, mid, lo


def _silu(x):
    return x * jax.nn.sigmoid(x)


def _gelu(x):
    return 0.5 * x * (1.0 + lax.erf(x * (0.5 ** 0.5)))


def _softplus(x):
    return jnp.maximum(x, 0.0) + jnp.log(1.0 + jnp.exp(-jnp.abs(x)))


def _mod_rows(ref):
    m = ref[...]
    return m.reshape(m.shape[-2], m.shape[-1])


def _mod_spec(k, rows_per_batch, tm):
    if rows_per_batch >= tm:
        tiles = rows_per_batch // tm
        return pl.BlockSpec((1, 1, 1, D_MODEL), lambda i, *_: (k, i // tiles, 0, 0))
    return pl.BlockSpec((1, tm, D_MODEL), lambda i, *_: (k, i, 0))


def _const_spec(shape):
    nd = len(shape)
    return pl.BlockSpec(shape, lambda *_: (0,) * nd)


def _ada_body(c_ref, w_ref, b_ref, o_ref):
    o_ref[...] = _dot(_silu(c_ref[...]), w_ref[...]) + b_ref[...]


def _ada(c, w_ada, b_ada):
    n = c.shape[0]
    return pl.pallas_call(
        _ada_body,
        grid=(6,),
        in_specs=[pl.BlockSpec((n, D_MODEL), lambda j: (0, 0)),
                  pl.BlockSpec((D_MODEL, D_MODEL), lambda j: (0, j)),
                  pl.BlockSpec((1, D_MODEL), lambda j: (0, j))],
        out_specs=pl.BlockSpec((n, D_MODEL), lambda j: (0, j)),
        out_shape=jax.ShapeDtypeStruct((n, 6 * D_MODEL), F32),
        name="ada",
    )(c, w_ada, b_ada.reshape(1, -1))


_IN_BLOCKS = (("qkv", QKV_WIDTH), ("z", DN_WIDTH), ("pool", POOL_WIDTH),
              ("ga", D_MODEL), ("gb", D_MODEL), ("ba", LANES))
_IN_TOTAL = sum(w for _, w in _IN_BLOCKS)
_IN_COL_CHUNK = 512


def _inproj_body(x_ref, sc_ref, sh_ref, g_ref, w_ref, *out_refs):
    x = x_ref[...]
    y = x * lax.rsqrt(jnp.mean(x * x, axis=-1, keepdims=True) + EPS) * g_ref[...]
    h = (y * (1.0 + _mod_rows(sc_ref)) + _mod_rows(sh_ref)).astype(BF16)
    off = 0
    for (_, width), o_ref in zip(_IN_BLOCKS, out_refs):
        for c0 in range(0, width, _IN_COL_CHUNK):
            cw = min(_IN_COL_CHUNK, width - c0)
            o_ref[:, c0:c0 + cw] = jnp.dot(h, w_ref[:, off + c0:off + c0 + cw],
                                           preferred_element_type=F32)
        off += width


def _inproj(x2d, mod, rows_per_batch, norm_g, w_cat, tm):
    t = x2d.shape[0]
    row = lambda w: pl.BlockSpec((tm, w), lambda i: (i, 0))
    return pl.pallas_call(
        _inproj_body,
        grid=(t // tm,),
        in_specs=[row(D_MODEL), _mod_spec(1, rows_per_batch, tm), _mod_spec(0, rows_per_batch, tm),
                  _const_spec((1, D_MODEL)),
                  pl.BlockSpec((D_MODEL, _IN_TOTAL), lambda i: (0, 0), pipeline_mode=pl.Buffered(1))],
        out_specs=[row(w) for _, w in _IN_BLOCKS],
        out_shape=[jax.ShapeDtypeStruct((t, w), F32) for _, w in _IN_BLOCKS],
        compiler_params=pltpu.CompilerParams(vmem_limit_bytes=VMEM_LIMIT),
        name="inproj",
    )(x2d, mod, mod, norm_g.reshape(1, -1), w_cat)


def _mixer_body(C, Lv, start,
                qkv_ref, ba_ref, z_ref, pin_ref, ga_ref, gb_ref, cbuf_ref, s0_ref, pbuf_ref,
                convw_ref, alog_ref, dtb_ref, dng_ref, wpool_ref, pscale_ref,
                mixed_ref, nconv_ref, ns_ref, npool_ref,
                xp_scr, act_scr, s_scr, pp_scr, odn_scr):
    n = pl.program_id(1)
    last = pl.num_programs(1) - 1

    @pl.when(n == 0)
    def _load_state():
        xp_scr[0:CONV_PAD, :] = cbuf_ref[0]
        pp_scr[0:POOL_PAD, :] = pbuf_ref[0]
        s_scr[...] = s0_ref[0]

    xp_scr[CONV_PAD:CONV_PAD + C, :] = qkv_ref[0]
    for c0 in range(0, QKV_WIDTH, 512):
        cs = slice(c0, c0 + 512)
        y = xp_scr[CONV_PAD:CONV_PAD + C, cs] * convw_ref[CONV_WIDTH - 1:CONV_WIDTH, cs]
        for k in range(CONV_WIDTH - 1):
            r0 = CONV_PAD - (CONV_WIDTH - 1) + k
            y = y + xp_scr[r0:r0 + C, cs] * convw_ref[k:k + 1, cs]
        act_scr[:, cs] = _silu(y)

    ba = ba_ref[0]
    lane = lax.broadcasted_iota(I32, (C, LANES), 1)
    beta_all = jax.nn.sigmoid(ba)
    g_all = -jnp.exp(alog_ref[...]) * _softplus(ba + dtb_ref[...])
    if Lv < C:
        valid = lax.broadcasted_iota(I32, (C, LANES), 0) < Lv
        beta_all = jnp.where(valid, beta_all, 0.0)
        g_all = jnp.where(valid, g_all, 0.0)
    ii = lax.broadcasted_iota(I32, (C, C), 0)
    jj = lax.broadcasted_iota(I32, (C, C), 1)
    causal = ii >= jj
    strict = ii > jj
    tril = jnp.where(causal, 1.0, 0.0).astype(BF16)
    eye = jnp.where(ii == jj, 1.0, 0.0)
    gc_all = sum(jnp.dot(tril, part, preferred_element_type=F32) for part in _split3(g_all))
    if C < LANES:
        gc_sq = jnp.concatenate([gc_all, jnp.zeros((LANES - C, LANES), F32)], axis=0)
    else:
        gc_sq = gc_all
    gc_t = gc_sq.T

    for h in range(DN_HEADS):
        hs = slice(h * DN_HEAD_DIM, (h + 1) * DN_HEAD_DIM)
        beta = jnp.sum(jnp.where(lane == h, beta_all, 0.0), axis=1, keepdims=True)
        gcol = jnp.sum(jnp.where(lane == DN_HEADS + h, gc_all, 0.0), axis=1, keepdims=True)
        grow = gc_t[DN_HEADS + h:DN_HEADS + h + 1, 0:C]
        glast = gcol[C - 1:C, :]

        q = act_scr[:, hs]
        k = act_scr[:, DN_WIDTH + h * DN_HEAD_DIM:DN_WIDTH + (h + 1) * DN_HEAD_DIM]
        v = act_scr[:, 2 * DN_WIDTH + h * DN_HEAD_DIM:2 * DN_WIDTH + (h + 1) * DN_HEAD_DIM]
        q = q * lax.rsqrt(jnp.sum(q * q, axis=-1, keepdims=True) + EPS) * (DN_HEAD_DIM ** -0.5)
        k = k * lax.rsqrt(jnp.sum(k * k, axis=-1, keepdims=True) + EPS)
        kb = k * beta
        vb = v * beta

        decay = jnp.where(causal, jnp.exp(jnp.where(causal, gcol - grow, 0.0)), 0.0)
        lower = jnp.where(strict, _dot_nt(kb, k) * decay, 0.0)
        ainv = eye - lower
        pw = lower
        p = 1
        while 2 * p < C:
            pw = _dot(pw, pw)
            ainv = ainv + _dot(ainv, pw)
            p *= 2
        sol = _dot(ainv, jnp.concatenate([vb, kb * jnp.exp(gcol)], axis=1))
        u = sol[:, :DN_HEAD_DIM]
        w = sol[:, DN_HEAD_DIM:]
        qk = _dot_nt(q, k) * decay
        k_tail = k * jnp.exp(glast - gcol)

        S = s_scr[h]
        v_new = u - _dot(w, S)
        o = _dot(q * jnp.exp(gcol), S) + _dot(qk, v_new)
        s_scr[h] = S * jnp.exp(glast) + lax.dot_general(
            k_tail.astype(BF16), v_new.astype(BF16), TN_DIMS, preferred_element_type=F32)

        zf = z_ref[0, :, hs]
        o = o * lax.rsqrt(jnp.mean(o * o, axis=-1, keepdims=True) + EPS) * dng_ref[...] * _silu(zf)
        odn_scr[:, hs] = o

    pp_scr[POOL_PAD:POOL_PAD + C, :] = pin_ref[0]
    pos = start + n * C + lax.broadcasted_iota(I32, (C, 1), 0)
    for gi, win in enumerate(POOL_WINDOWS):
        gs = slice(gi * POOL_GROUP_DIM, (gi + 1) * POOL_GROUP_DIM)
        xg = pp_scr[POOL_PAD:POOL_PAD + C, gs]
        ssum = xg
        for sft in range(1, win):
            ssum = ssum + pp_scr[POOL_PAD - sft:POOL_PAD - sft + C, gs]
        cnt = jnp.minimum(pos + 1, win).astype(F32)
        pooled = ssum / cnt - xg
        os_ = slice(gi * POOL_OUT_GROUP, (gi + 1) * POOL_OUT_GROUP)
        yp = _dot(pooled, wpool_ref[gi]) * pscale_ref[:, os_]
        mixed_ref[0, :, os_] = (jax.nn.sigmoid(ga_ref[0, :, os_]) * odn_scr[:, os_]
                                + jax.nn.sigmoid(gb_ref[0, :, os_]) * yp)

    @pl.when(n == last)
    def _store_state():
        nconv_ref[0] = xp_scr[Lv + CONV_PAD - (CONV_WIDTH - 1):Lv + CONV_PAD, :]
        npool_ref[0] = pp_scr[Lv + POOL_PAD - POOL_BUF:Lv + POOL_PAD, :]
        ns_ref[0] = s_scr[...]

    xp_scr[0:CONV_PAD, :] = xp_scr[C:C + CONV_PAD, :]
    pp_scr[0:POOL_PAD, :] = pp_scr[C:C + POOL_PAD, :]


def _mixer(proj, conv_buf, s0, pool_buf, start, seq_len, C,
           conv_w, a_log, dt_bias, dn_norm_g, w_pool, pool_scale):
    b, lp, _ = proj["qkv"].shape
    nchunks = lp // C
    lv = seq_len - (nchunks - 1) * C
    cbuf = jnp.pad(conv_buf, ((0, 0), (CONV_PAD - (CONV_WIDTH - 1), 0), (0, 0)))
    pbuf = jnp.pad(pool_buf, ((0, 0), (POOL_PAD - POOL_BUF, 0), (0, 0)))
    lane_pad = lambda a: jnp.pad(a.reshape(1, -1), ((0, 0), (DN_HEADS, LANES - 2 * DN_HEADS)))
    chunk = lambda w: pl.BlockSpec((1, C, w), lambda i, j: (i, j, 0))
    state = lambda *s: pl.BlockSpec((1,) + s, lambda i, j: (i,) + (0,) * len(s))
    return pl.pallas_call(
        functools.partial(_mixer_body, C, lv, start),
        grid=(b, nchunks),
        in_specs=[chunk(QKV_WIDTH), chunk(LANES), chunk(DN_WIDTH), chunk(POOL_WIDTH),
                  chunk(D_MODEL), chunk(D_MODEL),
                  state(CONV_PAD, QKV_WIDTH), state(DN_HEADS, DN_HEAD_DIM, DN_HEAD_DIM),
                  state(POOL_PAD, POOL_WIDTH),
                  _const_spec((CONV_WIDTH, QKV_WIDTH)), _const_spec((1, LANES)), _const_spec((1, LANES)),
                  _const_spec((1, DN_HEAD_DIM)),
                  _const_spec((len(POOL_WINDOWS), POOL_GROUP_DIM, POOL_OUT_GROUP)),
                  _const_spec((1, D_MODEL))],
        out_specs=[chunk(D_MODEL), state(CONV_WIDTH - 1, QKV_WIDTH),
                   state(DN_HEADS, DN_HEAD_DIM, DN_HEAD_DIM), state(POOL_BUF, POOL_WIDTH)],
        out_shape=[jax.ShapeDtypeStruct((b, lp, D_MODEL), F32),
                   jax.ShapeDtypeStruct((b, CONV_WIDTH - 1, QKV_WIDTH), F32),
                   jax.ShapeDtypeStruct((b, DN_HEADS, DN_HEAD_DIM, DN_HEAD_DIM), F32),
                   jax.ShapeDtypeStruct((b, POOL_BUF, POOL_WIDTH), F32)],
        scratch_shapes=[pltpu.VMEM((CONV_PAD + C + CONV_PAD, QKV_WIDTH), F32),
                        pltpu.VMEM((C, QKV_WIDTH), F32),
                        pltpu.VMEM((DN_HEADS, DN_HEAD_DIM, DN_HEAD_DIM), F32),
                        pltpu.VMEM((POOL_PAD + C + POOL_PAD, POOL_WIDTH), F32),
                        pltpu.VMEM((C, DN_WIDTH), F32)],
        compiler_params=pltpu.CompilerParams(dimension_semantics=("arbitrary", "arbitrary"),
                                             vmem_limit_bytes=VMEM_LIMIT),
        name="mixer",
    )(proj["qkv"], proj["ba"], proj["z"], proj["pool"], proj["ga"], proj["gb"], cbuf, s0, pbuf,
      conv_w, lane_pad(a_log), lane_pad(dt_bias), dn_norm_g.reshape(1, -1), w_pool,
      pool_scale.reshape(1, -1))


def _top16(s, ids, payload=None):
    big = float(2 ** 24)
    vals, sel, pays = [], [], []
    for _ in range(PEER_TOPK):
        m = jnp.max(s, axis=0, keepdims=True)
        am = jnp.min(jnp.where(s == m, ids, big), axis=0, keepdims=True)
        hit = ids == am
        if payload is not None:
            pays.append(jnp.max(jnp.where(hit, payload, -1.0), axis=0, keepdims=True))
        s = jnp.where(hit, -jnp.inf, s)
        vals.append(m)
        sel.append(am)
    out = (jnp.concatenate(vals, axis=0), jnp.concatenate(sel, axis=0))
    if payload is not None:
        out += (jnp.concatenate(pays, axis=0),)
    return out


_CAND_EDGE = 4


def _post_body(mixed_ref, x_ref, g1_ref, sc2_ref, sh2_ref, n2g_ref, wout_ref, wq_ref, keys_ref,
               x1_ref, h2_ref, idx_ref, gate_ref):
    tm = x_ref.shape[0]
    x1 = x_ref[...] + _mod_rows(g1_ref) * _dot(mixed_ref[...], wout_ref[...])
    x1_ref[...] = x1
    y = x1 * lax.rsqrt(jnp.mean(x1 * x1, axis=-1, keepdims=True) + EPS) * n2g_ref[...]
    h2 = y * (1.0 + _mod_rows(sc2_ref)) + _mod_rows(sh2_ref)
    h2_ref[...] = h2
    q = _dot(h2, wq_ref[...])

    K = PEER_TOPK
    key_id = lax.broadcasted_iota(I32, (PEER_NKEYS, 1), 0).astype(F32)
    r16 = lax.broadcasted_iota(I32, (K, 1), 0)
    cand_id = jnp.concatenate([(a * K + r16) for a in range(_CAND_EDGE)]
                              + [(r16 * K + b) for b in range(_CAND_EDGE)], axis=0).astype(F32)
    dup = r16 < _CAND_EDGE
    idx_rows, gate_rows = [], []
    for h in range(PEER_HEADS):
        half = []
        for p in range(2):
            c0 = (h * 2 + p) * PEER_KEY_HALF
            st = _dot_nt(keys_ref[h * 2 + p], q[:, c0:c0 + PEER_KEY_HALF])
            half.append(_top16(st, key_id))
        (s1, i1), (s2, i2) = half
        cand = jnp.concatenate(
            [s1[a:a + 1] + s2 for a in range(_CAND_EDGE)]
            + [jnp.where(dup, -jnp.inf, s1 + s2[b:b + 1]) for b in range(_CAND_EDGE)], axis=0)
        cidx = jnp.concatenate(
            [i1[a:a + 1] * PEER_NKEYS + i2 for a in range(_CAND_EDGE)]
            + [i1 * PEER_NKEYS + i2[b:b + 1] for b in range(_CAND_EDGE)], axis=0)
        best, _, eidx = _top16(cand, cand_id, cidx)
        e = jnp.exp(best - best[0:1])
        gate_rows.append(e / jnp.sum(e, axis=0, keepdims=True))
        idx_rows.append(eidx)
    idx_ref[...] = jnp.concatenate(idx_rows, axis=0).T.astype(I32)
    gate_ref[...] = jnp.concatenate(gate_rows, axis=0).T


def _post(mixed2d, x2d, mod, rows_per_batch, norm2_g, w_out, w_query, keys, tm):
    t = x2d.shape[0]
    row = lambda w: pl.BlockSpec((tm, w), lambda i: (i, 0))
    return pl.pallas_call(
        _post_body,
        grid=(t // tm,),
        in_specs=[row(D_MODEL), row(D_MODEL),
                  _mod_spec(2, rows_per_batch, tm), _mod_spec(4, rows_per_batch, tm),
                  _mod_spec(3, rows_per_batch, tm), _const_spec((1, D_MODEL)),
                  _const_spec((D_MODEL, D_MODEL)), _const_spec((D_MODEL, 2 * PEER_HEADS * PEER_KEY_HALF)),
                  _const_spec((2 * PEER_HEADS, PEER_NKEYS, PEER_KEY_HALF))],
        out_specs=[row(D_MODEL), row(D_MODEL), row(PEER_HK), row(PEER_HK)],
        out_shape=[jax.ShapeDtypeStruct((t, D_MODEL), F32), jax.ShapeDtypeStruct((t, D_MODEL), F32),
                   jax.ShapeDtypeStruct((t, PEER_HK), I32), jax.ShapeDtypeStruct((t, PEER_HK), F32)],
        compiler_params=pltpu.CompilerParams(vmem_limit_bytes=VMEM_LIMIT),
        name="post",
    )(mixed2d, x2d, mod, mod, mod, norm2_g.reshape(1, -1), w_out, w_query, keys)


SC_CORES = 2
SC_SUBCORES = 16
SC_LANES = 16
SC_WORKERS = SC_CORES * SC_SUBCORES
SC_TOKENS = 16
SC_SLOTS = 4
SC_CHUNKS = D_MODEL // SC_LANES
PROMPT_PARTS = 2


def _sc_mesh():
    return plsc.VectorSubcoreMesh(core_axis_name="c", subcore_axis_name="s")


def _sc_worker():
    return lax.axis_index("s") * SC_CORES + lax.axis_index("c")


def _sc_jobs(table_hbm, idx_v, buf, sem, compute):
    njobs = SC_TOKENS * PEER_HEADS

    def copy(j, slot):
        tt = j // PEER_HEADS
        h = j % PEER_HEADS
        rows = idx_v[tt, pl.ds(h * PEER_TOPK, PEER_TOPK)]
        return pltpu.make_async_copy(table_hbm.at[rows], buf.at[slot], sem.at[slot])

    for s in range(SC_SLOTS):
        copy(s, s).start()

    def group(g, c):
        for s in range(SC_SLOTS):
            j = g * SC_SLOTS + s
            copy(j, s).wait()
            compute(j // PEER_HEADS, j % PEER_HEADS, s)

            @pl.when(j + SC_SLOTS < njobs)
            def _next():
                copy(j + SC_SLOTS, s).start()
        return c

    lax.fori_loop(0, njobs // SC_SLOTS, group, 0)


def _peer_u_body(n_tok, idx_hbm, h2_hbm, u_hbm, pre_hbm, idx_v, h2_v, pre_v, ubuf, acc_v, sem):
    base = _sc_worker() * n_tok
    lane = lax.iota(I32, SC_LANES)

    def compute(tt, h, slot):
        def chunk(c, accs):
            xv = h2_v[tt, pl.ds(c * SC_LANES, SC_LANES)]
            return tuple(a + ubuf[slot, k, pl.ds(c * SC_LANES, SC_LANES)] * xv
                         for k, a in enumerate(accs))
        zero = jnp.zeros((SC_LANES,), F32)
        accs = lax.fori_loop(0, SC_CHUNKS, chunk, (zero,) * PEER_TOPK)
        for k, a in enumerate(accs):
            acc_v[k, :] = a
        tot = zero
        for j in range(SC_LANES):
            tot = tot + plsc.load_gather(acc_v, [lane, jnp.full((SC_LANES,), j, I32)])
        pre_v[tt, pl.ds(h * PEER_TOPK, PEER_TOPK)] = tot

    def block(bi, c):
        t0 = base + bi * SC_TOKENS
        pltpu.sync_copy(idx_hbm.at[pl.ds(t0, SC_TOKENS)], idx_v)
        pltpu.sync_copy(h2_hbm.at[pl.ds(t0, SC_TOKENS)], h2_v)
        _sc_jobs(u_hbm, idx_v, ubuf, sem, compute)
        pltpu.sync_copy(pre_v, pre_hbm.at[pl.ds(t0, SC_TOKENS)])
        return c

    lax.fori_loop(0, n_tok // SC_TOKENS, block, 0)


def _peer_v_body(n_tok, idx_hbm, coef_hbm, v_hbm, out_hbm, idx_v, coef_v, out_v, vbuf, sem):
    base = _sc_worker() * n_tok
    zero = jnp.zeros((SC_LANES,), F32)

    def compute(tt, h, slot):
        row = jnp.full((SC_LANES,), tt, I32)
        cb = [plsc.load_gather(coef_v, [row, jnp.full((SC_LANES,), h * PEER_TOPK + k, I32)])
              for k in range(PEER_TOPK)]

        @plsc.parallel_loop(0, SC_CHUNKS, unroll=2)
        def _chunk(c):
            cs = pl.ds(c * SC_LANES, SC_LANES)
            terms = [cb[k] * vbuf[slot, k, cs] for k in range(PEER_TOPK)]
            while len(terms) > 1:
                terms = [a + b for a, b in zip(terms[0::2], terms[1::2])]
            plsc.addupdate(out_v.at[tt, cs], terms[0])

    def block(bi, c):
        t0 = base + bi * SC_TOKENS
        pltpu.sync_copy(idx_hbm.at[pl.ds(t0, SC_TOKENS)], idx_v)
        pltpu.sync_copy(coef_hbm.at[pl.ds(t0, SC_TOKENS)], coef_v)

        def clear(i, cc):
            out_v[i // SC_CHUNKS, pl.ds((i % SC_CHUNKS) * SC_LANES, SC_LANES)] = zero
            return cc
        lax.fori_loop(0, SC_TOKENS * SC_CHUNKS, clear, 0)
        _sc_jobs(v_hbm, idx_v, vbuf, sem, compute)
        pltpu.sync_copy(out_v, out_hbm.at[pl.ds(t0, SC_TOKENS)])
        return c

    lax.fori_loop(0, n_tok // SC_TOKENS, block, 0)


def _peer_sc(body, idx, rows, table, out_width, name):
    t = idx.shape[0]
    assert t % (SC_WORKERS * SC_TOKENS) == 0
    n_tok = t // SC_WORKERS
    return pl.kernel(
        functools.partial(body, n_tok),
        out_type=jax.ShapeDtypeStruct((t, out_width), F32),
        mesh=_sc_mesh(),
        scratch_types=[pltpu.VMEM((SC_TOKENS, PEER_HK), I32),
                       pltpu.VMEM((SC_TOKENS, rows.shape[1]), F32),
                       pltpu.VMEM((SC_TOKENS, out_width), F32),
                       pltpu.VMEM((SC_SLOTS, PEER_TOPK, D_MODEL), F32)]
                      + ([pltpu.VMEM((PEER_TOPK, SC_LANES), F32)] if body is _peer_u_body else [])
                      + [pltpu.SemaphoreType.DMA((SC_SLOTS,))],
        compiler_params=pltpu.CompilerParams(needs_layout_passes=False),
        name=name,
    )(idx, rows, table)


def _coef_body(pre_ref, gate_ref, coef_ref):
    coef_ref[...] = gate_ref[...] * _gelu(pre_ref[...])


def _coef(pre, gates, tm):
    t = pre.shape[0]
    row = pl.BlockSpec((tm, PEER_HK), lambda i: (i, 0))
    return pl.pallas_call(_coef_body, grid=(t // tm,), in_specs=[row, row], out_specs=row,
                          out_shape=jax.ShapeDtypeStruct((t, PEER_HK), F32), name="coef")(pre, gates)


def _final_body(x1_ref, peer_ref, g2_ref, fng_ref, y_ref):
    x2 = x1_ref[...] + _mod_rows(g2_ref) * peer_ref[...]
    y_ref[...] = x2 * lax.rsqrt(jnp.mean(x2 * x2, axis=-1, keepdims=True) + EPS) * fng_ref[...]


def _final(x1, peer_out, mod, rows_per_batch, final_g, tm):
    t = x1.shape[0]
    row = pl.BlockSpec((tm, D_MODEL), lambda i: (i, 0))
    return pl.pallas_call(
        _final_body, grid=(t // tm,),
        in_specs=[row, row, _mod_spec(5, rows_per_batch, tm), _const_spec((1, D_MODEL))],
        out_specs=row, out_shape=jax.ShapeDtypeStruct((t, D_MODEL), F32), name="final",
    )(x1, peer_out, mod, final_g.reshape(1, -1))


def _expert(idx, h2, gates, x1, mod, rows_per_batch, final_g, expert_u, expert_v, tm):
    pre = _peer_sc(_peer_u_body, idx, h2, expert_u, PEER_HK, "peer_u")
    coef = _coef(pre, gates, tm)
    peer_out = _peer_sc(_peer_v_body, idx, coef, expert_v, D_MODEL, "peer_v")
    return _final(x1, peer_out, mod, rows_per_batch, final_g, tm)


def _group(x, mod, conv_buf, s0, pool_buf, start, chunk, tm, wts):
    b, l, _ = x.shape
    t = b * l
    x2d = x.reshape(t, D_MODEL)
    if l >= tm:
        modx = mod.reshape(b, 6, 1, D_MODEL).transpose(1, 0, 2, 3)
    else:
        modx = jnp.repeat(mod.reshape(b, 6, D_MODEL), l, axis=0).transpose(1, 0, 2)
    outs = _inproj(x2d, modx, l, wts["norm1_g"], wts["w_cat"], tm)
    lp = -(-l // chunk) * chunk
    proj = {}
    for (name, w), a in zip(_IN_BLOCKS, outs):
        a = a.reshape(b, l, w)
        proj[name] = a if lp == l else jnp.pad(a, ((0, 0), (0, lp - l), (0, 0)))
    mixed, nconv, ns, npool = _mixer(proj, conv_buf, s0, pool_buf, start, l, chunk,
                                     wts["conv_w"], wts["a_log"], wts["dt_bias"], wts["dn_norm_g"],
                                     wts["w_pool"], wts["pool_scale"])
    mixed2d = mixed[:, :l].reshape(t, D_MODEL)
    x1, h2, idx, gates = _post(mixed2d, x2d, modx, l, wts["norm2_g"], wts["w_out"], wts["w_query"],
                               wts["keys"], tm)
    y = _expert(idx, h2, gates, x1, modx, l, wts["final_norm_g"], wts["expert_u"], wts["expert_v"], tm)
    return y.reshape(b, l, D_MODEL), nconv, ns, npool


def kernel(x_prompt, x_sample, c_prompt, c_sample, state_conv, state_delta, state_pool, w_ada, b_ada, norm1_g, w_in, conv_w, a_log, dt_bias, dn_norm_g, w_pool, pool_scale, w_out, norm2_g, w_query, sub_keys, expert_u, expert_v, final_norm_g):
    bp = x_prompt.shape[0]
    bs = x_sample.shape[0]
    yp, ys = x_prompt, x_sample
    conv_p, delta_p, pool_p, conv_s, delta_s, pool_s = [], [], [], [], [], []
    zero_conv = jnp.zeros((bp, CONV_WIDTH - 1, QKV_WIDTH), F32)
    zero_delta = jnp.zeros((bp, DN_HEADS, DN_HEAD_DIM, DN_HEAD_DIM), F32)
    zero_pool = jnp.zeros((bp, POOL_BUF, POOL_WIDTH), F32)
    c_all = jnp.concatenate([c_prompt, c_sample], axis=0)
    for layer in range(DEPTH):
        wi = w_in[layer]
        o_b = QKV_WIDTH
        o_z = o_b + 2 * DN_HEADS
        w_ba = jnp.pad(wi[:, o_b:o_z], ((0, 0), (0, LANES - 2 * DN_HEADS)))
        w_cat = jnp.concatenate([wi[:, :o_b], wi[:, o_z:], w_ba], axis=1).astype(BF16)
        last = layer == DEPTH - 1
        wts = dict(
            norm1_g=norm1_g[layer], w_cat=w_cat, conv_w=conv_w[layer], a_log=a_log[layer],
            dt_bias=dt_bias[layer], dn_norm_g=dn_norm_g[layer], w_pool=w_pool[layer],
            pool_scale=pool_scale[layer], w_out=w_out[layer].astype(BF16), norm2_g=norm2_g[layer],
            w_query=w_query[layer].astype(BF16),
            keys=sub_keys[layer].reshape(2 * PEER_HEADS, PEER_NKEYS, PEER_KEY_HALF).astype(BF16),
            expert_u=expert_u[layer], expert_v=expert_v[layer],
            final_norm_g=final_norm_g if last else jnp.ones_like(final_norm_g))
        mod = _ada(c_all, w_ada[layer], b_ada[layer])
        assert last, "final norm is fused into the expert stage"
        parts = []
        for b0 in range(0, bp, bp // PROMPT_PARTS):
            bsl = slice(b0, b0 + bp // PROMPT_PARTS)
            parts.append(_group(yp[bsl], mod[bsl], zero_conv[bsl], zero_delta[bsl], zero_pool[bsl],
                                0, DN_CHUNK, 256, wts))
        yp, cp, sp, pp = (jnp.concatenate(a, axis=0) for a in zip(*parts))
        ys, cs, ss, ps = _group(ys, mod[bp:], state_conv[layer], state_delta[layer], state_pool[layer],
                                PAST_LEN, SUBLANES, 256, wts)
        conv_p.append(cp)
        delta_p.append(sp)
        pool_p.append(pp)
        conv_s.append(cs)
        delta_s.append(ss)
        pool_s.append(ps)
    return (yp, ys, jnp.stack(conv_p), jnp.stack(delta_p), jnp.stack(pool_p),
            jnp.stack(conv_s), jnp.stack(delta_s), jnp.stack(pool_s))
```

```python
import functools

import jax
import jax.numpy as jnp
from jax import lax
from jax.experimental import pallas as pl
from jax.experimental.pallas import tpu as pltpu
from jax.experimental.pallas import tpu_sc as plsc

F32 = jnp.float32
BF16 = jnp.bfloat16
I32 = jnp.int32

D_MODEL = 1024
DEPTH = 1
PAST_LEN = 16384
DN_HEADS = 8
DN_HEAD_DIM = 128
DN_WIDTH = DN_HEADS * DN_HEAD_DIM
QKV_WIDTH = 3 * DN_WIDTH
CONV_WIDTH = 4
DN_CHUNK = 64
POOL_WINDOWS = (2, 4, 8, 16)
POOL_GROUP_DIM = 128
POOL_WIDTH = len(POOL_WINDOWS) * POOL_GROUP_DIM
POOL_OUT_GROUP = D_MODEL // len(POOL_WINDOWS)
POOL_BUF = max(POOL_WINDOWS) - 1
PEER_HEADS = 8
PEER_NKEYS = 128
PEER_TOPK = 16
PEER_KEY_HALF = 128
PEER_HK = PEER_HEADS * PEER_TOPK
EPS = 1e-6

LANES = 128
SUBLANES = 8
CONV_PAD = SUBLANES
POOL_PAD = 16
VMEM_LIMIT = 56 * 1024 * 1024

NT_DIMS = (((1,), (1,)), ((), ()))
TN_DIMS = (((0,), (0,)), ((), ()))


def _dot(a, b):
    return jnp.dot(a.astype(BF16), b.astype(BF16), preferred_element_type=F32)


def _dot_nt(a, b):
    return lax.dot_general(a.astype(BF16), b.astype(BF16), NT_DIMS, preferred_element_type=F32)


def _split3(x):
    hi = x.astype(BF16)
    r1 = x - hi.astype(F32)
    mid = r1.astype(BF16)
    lo = (r1 - mid.astype(F32)).astype(BF16)
    return hi, mid, lo


def _silu(x):
    return x * jax.nn.sigmoid(x)


def _gelu(x):
    return 0.5 * x * (1.0 + lax.erf(x * (0.5 ** 0.5)))


def _softplus(x):
    return jnp.maximum(x, 0.0) + jnp.log(1.0 + jnp.exp(-jnp.abs(x)))


def _mod_rows(ref):
    m = ref[...]
    return m.reshape(m.shape[-2], m.shape[-1])


def _mod_spec(k, rows_per_batch, tm):
    if rows_per_batch >= tm:
        tiles = rows_per_batch // tm
        return pl.BlockSpec((1, 1, 1, D_MODEL), lambda i, *_: (k, i // tiles, 0, 0))
    return pl.BlockSpec((1, tm, D_MODEL), lambda i, *_: (k, i, 0))


def _const_spec(shape):
    nd = len(shape)
    return pl.BlockSpec(shape, lambda *_: (0,) * nd)


def _ada_body(c_ref, w_ref, b_ref, o_ref):
    o_ref[...] = _dot(_silu(c_ref[...]), w_ref[...]) + b_ref[...]


def _ada(c, w_ada, b_ada):
    n = c.shape[0]
    return pl.pallas_call(
        _ada_body,
        grid=(6,),
        in_specs=[pl.BlockSpec((n, D_MODEL), lambda j: (0, 0)),
                  pl.BlockSpec((D_MODEL, D_MODEL), lambda j: (0, j)),
                  pl.BlockSpec((1, D_MODEL), lambda j: (0, j))],
        out_specs=pl.BlockSpec((n, D_MODEL), lambda j: (0, j)),
        out_shape=jax.ShapeDtypeStruct((n, 6 * D_MODEL), F32),
        name="ada",
    )(c, w_ada, b_ada.reshape(1, -1))


_IN_BLOCKS = (("qkv", QKV_WIDTH), ("z", DN_WIDTH), ("pool", POOL_WIDTH),
              ("ga", D_MODEL), ("gb", D_MODEL), ("ba", LANES))
_IN_TOTAL = sum(w for _, w in _IN_BLOCKS)
_IN_COL_CHUNK = 512


def _inproj_body(x_ref, sc_ref, sh_ref, g_ref, w_ref, *out_refs):
    x = x_ref[...]
    y = x * lax.rsqrt(jnp.mean(x * x, axis=-1, keepdims=True) + EPS) * g_ref[...]
    h = (y * (1.0 + _mod_rows(sc_ref)) + _mod_rows(sh_ref)).astype(BF16)
    off = 0
    for (_, width), o_ref in zip(_IN_BLOCKS, out_refs):
        for c0 in range(0, width, _IN_COL_CHUNK):
            cw = min(_IN_COL_CHUNK, width - c0)
            o_ref[:, c0:c0 + cw] = jnp.dot(h, w_ref[:, off + c0:off + c0 + cw],
                                           preferred_element_type=F32)
        off += width


def _inproj(x2d, mod, rows_per_batch, norm_g, w_cat, tm):
    t = x2d.shape[0]
    row = lambda w: pl.BlockSpec((tm, w), lambda i: (i, 0))
    return pl.pallas_call(
        _inproj_body,
        grid=(t // tm,),
        in_specs=[row(D_MODEL), _mod_spec(1, rows_per_batch, tm), _mod_spec(0, rows_per_batch, tm),
                  _const_spec((1, D_MODEL)),
                  pl.BlockSpec((D_MODEL, _IN_TOTAL), lambda i: (0, 0), pipeline_mode=pl.Buffered(1))],
        out_specs=[row(w) for _, w in _IN_BLOCKS],
        out_shape=[jax.ShapeDtypeStruct((t, w), F32) for _, w in _IN_BLOCKS],
        compiler_params=pltpu.CompilerParams(vmem_limit_bytes=VMEM_LIMIT),
        name="inproj",
    )(x2d, mod, mod, norm_g.reshape(1, -1), w_cat)


def _mixer_body(C, Lv, start,
                qkv_ref, ba_ref, z_ref, pin_ref, ga_ref, gb_ref, cbuf_ref, s0_ref, pbuf_ref,
                convw_ref, alog_ref, dtb_ref, dng_ref, wpool_ref, pscale_ref,
                mixed_ref, nconv_ref, ns_ref, npool_ref,
                xp_scr, act_scr, s_scr, pp_scr, odn_scr):
    n = pl.program_id(1)
    last = pl.num_programs(1) - 1

    @pl.when(n == 0)
    def _load_state():
        xp_scr[0:CONV_PAD, :] = cbuf_ref[0]
        pp_scr[0:POOL_PAD, :] = pbuf_ref[0]
        s_scr[...] = s0_ref[0]

    xp_scr[CONV_PAD:CONV_PAD + C, :] = qkv_ref[0]
    for c0 in range(0, QKV_WIDTH, 512):
        cs = slice(c0, c0 + 512)
        y = xp_scr[CONV_PAD:CONV_PAD + C, cs] * convw_ref[CONV_WIDTH - 1:CONV_WIDTH, cs]
        for k in range(CONV_WIDTH - 1):
            r0 = CONV_PAD - (CONV_WIDTH - 1) + k
            y = y + xp_scr[r0:r0 + C, cs] * convw_ref[k:k + 1, cs]
        act_scr[:, cs] = _silu(y)

    ba = ba_ref[0]
    lane = lax.broadcasted_iota(I32, (C, LANES), 1)
    beta_all = jax.nn.sigmoid(ba)
    g_all = -jnp.exp(alog_ref[...]) * _softplus(ba + dtb_ref[...])
    if Lv < C:
        valid = lax.broadcasted_iota(I32, (C, LANES), 0) < Lv
        beta_all = jnp.where(valid, beta_all, 0.0)
        g_all = jnp.where(valid, g_all, 0.0)
    ii = lax.broadcasted_iota(I32, (C, C), 0)
    jj = lax.broadcasted_iota(I32, (C, C), 1)
    causal = ii >= jj
    strict = ii > jj
    tril = jnp.where(causal, 1.0, 0.0).astype(BF16)
    eye = jnp.where(ii == jj, 1.0, 0.0)
    gc_all = sum(jnp.dot(tril, part, preferred_element_type=F32) for part in _split3(g_all))
    if C < LANES:
        gc_sq = jnp.concatenate([gc_all, jnp.zeros((LANES - C, LANES), F32)], axis=0)
    else:
        gc_sq = gc_all
    gc_t = gc_sq.T

    for h in range(DN_HEADS):
        hs = slice(h * DN_HEAD_DIM, (h + 1) * DN_HEAD_DIM)
        beta = jnp.sum(jnp.where(lane == h, beta_all, 0.0), axis=1, keepdims=True)
        gcol = jnp.sum(jnp.where(lane == DN_HEADS + h, gc_all, 0.0), axis=1, keepdims=True)
        grow = gc_t[DN_HEADS + h:DN_HEADS + h + 1, 0:C]
        glast = gcol[C - 1:C, :]

        q = act_scr[:, hs]
        k = act_scr[:, DN_WIDTH + h * DN_HEAD_DIM:DN_WIDTH + (h + 1) * DN_HEAD_DIM]
        v = act_scr[:, 2 * DN_WIDTH + h * DN_HEAD_DIM:2 * DN_WIDTH + (h + 1) * DN_HEAD_DIM]
        q = q * lax.rsqrt(jnp.sum(q * q, axis=-1, keepdims=True) + EPS) * (DN_HEAD_DIM ** -0.5)
        k = k * lax.rsqrt(jnp.sum(k * k, axis=-1, keepdims=True) + EPS)
        kb = k * beta
        vb = v * beta

        decay = jnp.where(causal, jnp.exp(jnp.where(causal, gcol - grow, 0.0)), 0.0)
        lower = jnp.where(strict, _dot_nt(kb, k) * decay, 0.0)
        ainv = eye - lower
        pw = lower
        p = 1
        while 2 * p < C:
            pw = _dot(pw, pw)
            ainv = ainv + _dot(ainv, pw)
            p *= 2
        sol = _dot(ainv, jnp.concatenate([vb, kb * jnp.exp(gcol)], axis=1))
        u = sol[:, :DN_HEAD_DIM]
        w = sol[:, DN_HEAD_DIM:]
        qk = _dot_nt(q, k) * decay
        k_tail = k * jnp.exp(glast - gcol)

        S = s_scr[h]
        v_new = u - _dot(w, S)
        o = _dot(q * jnp.exp(gcol), S) + _dot(qk, v_new)
        s_scr[h] = S * jnp.exp(glast) + lax.dot_general(
            k_tail.astype(BF16), v_new.astype(BF16), TN_DIMS, preferred_element_type=F32)

        zf = z_ref[0, :, hs]
        o = o * lax.rsqrt(jnp.mean(o * o, axis=-1, keepdims=True) + EPS) * dng_ref[...] * _silu(zf)
        odn_scr[:, hs] = o

    pp_scr[POOL_PAD:POOL_PAD + C, :] = pin_ref[0]
    pos = start + n * C + lax.broadcasted_iota(I32, (C, 1), 0)
    for gi, win in enumerate(POOL_WINDOWS):
        gs = slice(gi * POOL_GROUP_DIM, (gi + 1) * POOL_GROUP_DIM)
        xg = pp_scr[POOL_PAD:POOL_PAD + C, gs]
        ssum = xg
        for sft in range(1, win):
            ssum = ssum + pp_scr[POOL_PAD - sft:POOL_PAD - sft + C, gs]
        cnt = jnp.minimum(pos + 1, win).astype(F32)
        pooled = ssum / cnt - xg
        os_ = slice(gi * POOL_OUT_GROUP, (gi + 1) * POOL_OUT_GROUP)
        yp = _dot(pooled, wpool_ref[gi]) * pscale_ref[:, os_]
        mixed_ref[0, :, os_] = (jax.nn.sigmoid(ga_ref[0, :, os_]) * odn_scr[:, os_]
                                + jax.nn.sigmoid(gb_ref[0, :, os_]) * yp)

    @pl.when(n == last)
    def _store_state():
        nconv_ref[0] = xp_scr[Lv + CONV_PAD - (CONV_WIDTH - 1):Lv + CONV_PAD, :]
        npool_ref[0] = pp_scr[Lv + POOL_PAD - POOL_BUF:Lv + POOL_PAD, :]
        ns_ref[0] = s_scr[...]

    xp_scr[0:CONV_PAD, :] = xp_scr[C:C + CONV_PAD, :]
    pp_scr[0:POOL_PAD, :] = pp_scr[C:C + POOL_PAD, :]


def _mixer(proj, conv_buf, s0, pool_buf, start, seq_len, C,
           conv_w, a_log, dt_bias, dn_norm_g, w_pool, pool_scale):
    b, lp, _ = proj["qkv"].shape
    nchunks = lp // C
    lv = seq_len - (nchunks - 1) * C
    cbuf = jnp.pad(conv_buf, ((0, 0), (CONV_PAD - (CONV_WIDTH - 1), 0), (0, 0)))
    pbuf = jnp.pad(pool_buf, ((0, 0), (POOL_PAD - POOL_BUF, 0), (0, 0)))
    lane_pad = lambda a: jnp.pad(a.reshape(1, -1), ((0, 0), (DN_HEADS, LANES - 2 * DN_HEADS)))
    chunk = lambda w: pl.BlockSpec((1, C, w), lambda i, j: (i, j, 0))
    state = lambda *s: pl.BlockSpec((1,) + s, lambda i, j: (i,) + (0,) * len(s))
    return pl.pallas_call(
        functools.partial(_mixer_body, C, lv, start),
        grid=(b, nchunks),
        in_specs=[chunk(QKV_WIDTH), chunk(LANES), chunk(DN_WIDTH), chunk(POOL_WIDTH),
                  chunk(D_MODEL), chunk(D_MODEL),
                  state(CONV_PAD, QKV_WIDTH), state(DN_HEADS, DN_HEAD_DIM, DN_HEAD_DIM),
                  state(POOL_PAD, POOL_WIDTH),
                  _const_spec((CONV_WIDTH, QKV_WIDTH)), _const_spec((1, LANES)), _const_spec((1, LANES)),
                  _const_spec((1, DN_HEAD_DIM)),
                  _const_spec((len(POOL_WINDOWS), POOL_GROUP_DIM, POOL_OUT_GROUP)),
                  _const_spec((1, D_MODEL))],
        out_specs=[chunk(D_MODEL), state(CONV_WIDTH - 1, QKV_WIDTH),
                   state(DN_HEADS, DN_HEAD_DIM, DN_HEAD_DIM), state(POOL_BUF, POOL_WIDTH)],
        out_shape=[jax.ShapeDtypeStruct((b, lp, D_MODEL), F32),
                   jax.ShapeDtypeStruct((b, CONV_WIDTH - 1, QKV_WIDTH), F32),
                   jax.ShapeDtypeStruct((b, DN_HEADS, DN_HEAD_DIM, DN_HEAD_DIM), F32),
                   jax.ShapeDtypeStruct((b, POOL_BUF, POOL_WIDTH), F32)],
        scratch_shapes=[pltpu.VMEM((CONV_PAD + C + CONV_PAD, QKV_WIDTH), F32),
                        pltpu.VMEM((C, QKV_WIDTH), F32),
                        pltpu.VMEM((DN_HEADS, DN_HEAD_DIM, DN_HEAD_DIM), F32),
                        pltpu.VMEM((POOL_PAD + C + POOL_PAD, POOL_WIDTH), F32),
                        pltpu.VMEM((C, DN_WIDTH), F32)],
        compiler_params=pltpu.CompilerParams(dimension_semantics=("arbitrary", "arbitrary"),
                                             vmem_limit_bytes=VMEM_LIMIT),
        name="mixer",
    )(proj["qkv"], proj["ba"], proj["z"], proj["pool"], proj["ga"], proj["gb"], cbuf, s0, pbuf,
      conv_w, lane_pad(a_log), lane_pad(dt_bias), dn_norm_g.reshape(1, -1), w_pool,
      pool_scale.reshape(1, -1))


def _top16(s, ids, payload=None):
    big = float(2 ** 24)
    vals, sel, pays = [], [], []
    for _ in range(PEER_TOPK):
        m = jnp.max(s, axis=0, keepdims=True)
        am = jnp.min(jnp.where(s == m, ids, big), axis=0, keepdims=True)
        hit = ids == am
        if payload is not None:
            pays.append(jnp.max(jnp.where(hit, payload, -1.0), axis=0, keepdims=True))
        s = jnp.where(hit, -jnp.inf, s)
        vals.append(m)
        sel.append(am)
    out = (jnp.concatenate(vals, axis=0), jnp.concatenate(sel, axis=0))
    if payload is not None:
        out += (jnp.concatenate(pays, axis=0),)
    return out


_CAND_EDGE = 4


def _post_body(mixed_ref, x_ref, g1_ref, sc2_ref, sh2_ref, n2g_ref, wout_ref, wq_ref, keys_ref,
               x1_ref, h2_ref, idx_ref, gate_ref):
    tm = x_ref.shape[0]
    x1 = x_ref[...] + _mod_rows(g1_ref) * _dot(mixed_ref[...], wout_ref[...])
    x1_ref[...] = x1
    y = x1 * lax.rsqrt(jnp.mean(x1 * x1, axis=-1, keepdims=True) + EPS) * n2g_ref[...]
    h2 = y * (1.0 + _mod_rows(sc2_ref)) + _mod_rows(sh2_ref)
    h2_ref[...] = h2
    q = _dot(h2, wq_ref[...])

    K = PEER_TOPK
    key_id = lax.broadcasted_iota(I32, (PEER_NKEYS, 1), 0).astype(F32)
    r16 = lax.broadcasted_iota(I32, (K, 1), 0)
    cand_id = jnp.concatenate([(a * K + r16) for a in range(_CAND_EDGE)]
                              + [(r16 * K + b) for b in range(_CAND_EDGE)], axis=0).astype(F32)
    dup = r16 < _CAND_EDGE
    idx_rows, gate_rows = [], []
    for h in range(PEER_HEADS):
        half = []
        for p in range(2):
            c0 = (h * 2 + p) * PEER_KEY_HALF
            st = _dot_nt(keys_ref[h * 2 + p], q[:, c0:c0 + PEER_KEY_HALF])
            half.append(_top16(st, key_id))
        (s1, i1), (s2, i2) = half
        cand = jnp.concatenate(
            [s1[a:a + 1] + s2 for a in range(_CAND_EDGE)]
            + [jnp.where(dup, -jnp.inf, s1 + s2[b:b + 1]) for b in range(_CAND_EDGE)], axis=0)
        cidx = jnp.concatenate(
            [i1[a:a + 1] * PEER_NKEYS + i2 for a in range(_CAND_EDGE)]
            + [i1 * PEER_NKEYS + i2[b:b + 1] for b in range(_CAND_EDGE)], axis=0)
        best, _, eidx = _top16(cand, cand_id, cidx)
        e = jnp.exp(best - best[0:1])
        gate_rows.append(e / jnp.sum(e, axis=0, keepdims=True))
        idx_rows.append(eidx)
    idx_ref[...] = jnp.concatenate(idx_rows, axis=0).T.astype(I32)
    gate_ref[...] = jnp.concatenate(gate_rows, axis=0).T


def _post(mixed2d, x2d, mod, rows_per_batch, norm2_g, w_out, w_query, keys, tm):
    t = x2d.shape[0]
    row = lambda w: pl.BlockSpec((tm, w), lambda i: (i, 0))
    return pl.pallas_call(
        _post_body,
        grid=(t // tm,),
        in_specs=[row(D_MODEL), row(D_MODEL),
                  _mod_spec(2, rows_per_batch, tm), _mod_spec(4, rows_per_batch, tm),
                  _mod_spec(3, rows_per_batch, tm), _const_spec((1, D_MODEL)),
                  _const_spec((D_MODEL, D_MODEL)), _const_spec((D_MODEL, 2 * PEER_HEADS * PEER_KEY_HALF)),
                  _const_spec((2 * PEER_HEADS, PEER_NKEYS, PEER_KEY_HALF))],
        out_specs=[row(D_MODEL), row(D_MODEL), row(PEER_HK), row(PEER_HK)],
        out_shape=[jax.ShapeDtypeStruct((t, D_MODEL), F32), jax.ShapeDtypeStruct((t, D_MODEL), F32),
                   jax.ShapeDtypeStruct((t, PEER_HK), I32), jax.ShapeDtypeStruct((t, PEER_HK), F32)],
        compiler_params=pltpu.CompilerParams(vmem_limit_bytes=VMEM_LIMIT),
        name="post",
    )(mixed2d, x2d, mod, mod, mod, norm2_g.reshape(1, -1), w_out, w_query, keys)


SC_CORES = 2
SC_SUBCORES = 16
SC_LANES = 16
SC_WORKERS = SC_CORES * SC_SUBCORES
SC_TOKENS = 16
SC_SLOTS = 4
SC_CHUNKS = D_MODEL // SC_LANES
PROMPT_PARTS = 8


def _sc_mesh():
    return plsc.VectorSubcoreMesh(core_axis_name="c", subcore_axis_name="s")


def _sc_worker():
    return lax.axis_index("s") * SC_CORES + lax.axis_index("c")


def _sc_jobs(table_hbm, idx_v, buf, sem, compute):
    njobs = SC_TOKENS * PEER_HEADS

    def copy(j, slot):
        tt = j // PEER_HEADS
        h = j % PEER_HEADS
        rows = idx_v[tt, pl.ds(h * PEER_TOPK, PEER_TOPK)]
        return pltpu.make_async_copy(table_hbm.at[rows], buf.at[slot], sem.at[slot])

    for s in range(SC_SLOTS):
        copy(s, s).start()

    def group(g, c):
        for s in range(SC_SLOTS):
            j = g * SC_SLOTS + s
            copy(j, s).wait()
            compute(j // PEER_HEADS, j % PEER_HEADS, s)

            @pl.when(j + SC_SLOTS < njobs)
            def _next():
                copy(j + SC_SLOTS, s).start()
        return c

    lax.fori_loop(0, njobs // SC_SLOTS, group, 0)


def _peer_u_body(n_tok, idx_hbm, h2_hbm, u_hbm, pre_hbm, idx_v, h2_v, pre_v, ubuf, acc_v, sem):
    base = _sc_worker() * n_tok
    lane = lax.iota(I32, SC_LANES)

    def compute(tt, h, slot):
        def chunk(c, accs):
            xv = h2_v[tt, pl.ds(c * SC_LANES, SC_LANES)]
            return tuple(a + ubuf[slot, k, pl.ds(c * SC_LANES, SC_LANES)] * xv
                         for k, a in enumerate(accs))
        zero = jnp.zeros((SC_LANES,), F32)
        accs = lax.fori_loop(0, SC_CHUNKS, chunk, (zero,) * PEER_TOPK)
        for k, a in enumerate(accs):
            acc_v[k, :] = a
        tot = zero
        for j in range(SC_LANES):
            tot = tot + plsc.load_gather(acc_v, [lane, jnp.full((SC_LANES,), j, I32)])
        pre_v[tt, pl.ds(h * PEER_TOPK, PEER_TOPK)] = tot

    def block(bi, c):
        t0 = base + bi * SC_TOKENS
        pltpu.sync_copy(idx_hbm.at[pl.ds(t0, SC_TOKENS)], idx_v)
        pltpu.sync_copy(h2_hbm.at[pl.ds(t0, SC_TOKENS)], h2_v)
        _sc_jobs(u_hbm, idx_v, ubuf, sem, compute)
        pltpu.sync_copy(pre_v, pre_hbm.at[pl.ds(t0, SC_TOKENS)])
        return c

    lax.fori_loop(0, n_tok // SC_TOKENS, block, 0)


def _peer_v_body(n_tok, idx_hbm, coef_hbm, v_hbm, out_hbm, idx_v, coef_v, out_v, vbuf, sem):
    base = _sc_worker() * n_tok
    zero = jnp.zeros((SC_LANES,), F32)

    def compute(tt, h, slot):
        row = jnp.full((SC_LANES,), tt, I32)
        cb = [plsc.load_gather(coef_v, [row, jnp.full((SC_LANES,), h * PEER_TOPK + k, I32)])
              for k in range(PEER_TOPK)]

        @plsc.parallel_loop(0, SC_CHUNKS, unroll=2)
        def _chunk(c):
            cs = pl.ds(c * SC_LANES, SC_LANES)
            terms = [cb[k] * vbuf[slot, k, cs] for k in range(PEER_TOPK)]
            while len(terms) > 1:
                terms = [a + b for a, b in zip(terms[0::2], terms[1::2])]
            plsc.addupdate(out_v.at[tt, cs], terms[0])

    def block(bi, c):
        t0 = base + bi * SC_TOKENS
        pltpu.sync_copy(idx_hbm.at[pl.ds(t0, SC_TOKENS)], idx_v)
        pltpu.sync_copy(coef_hbm.at[pl.ds(t0, SC_TOKENS)], coef_v)

        def clear(i, cc):
            out_v[i // SC_CHUNKS, pl.ds((i % SC_CHUNKS) * SC_LANES, SC_LANES)] = zero
            return cc
        lax.fori_loop(0, SC_TOKENS * SC_CHUNKS, clear, 0)
        _sc_jobs(v_hbm, idx_v, vbuf, sem, compute)
        pltpu.sync_copy(out_v, out_hbm.at[pl.ds(t0, SC_TOKENS)])
        return c

    lax.fori_loop(0, n_tok // SC_TOKENS, block, 0)


def _peer_sc(body, idx, rows, table, out_width, name):
    t = idx.shape[0]
    assert t % (SC_WORKERS * SC_TOKENS) == 0
    n_tok = t // SC_WORKERS
    return pl.kernel(
        functools.partial(body, n_tok),
        out_type=jax.ShapeDtypeStruct((t, out_width), F32),
        mesh=_sc_mesh(),
        scratch_types=[pltpu.VMEM((SC_TOKENS, PEER_HK), I32),
                       pltpu.VMEM((SC_TOKENS, rows.shape[1]), F32),
                       pltpu.VMEM((SC_TOKENS, out_width), F32),
                       pltpu.VMEM((SC_SLOTS, PEER_TOPK, D_MODEL), F32)]
                      + ([pltpu.VMEM((PEER_TOPK, SC_LANES), F32)] if body is _peer_u_body else [])
                      + [pltpu.SemaphoreType.DMA((SC_SLOTS,))],
        compiler_params=pltpu.CompilerParams(needs_layout_passes=False),
        name=name,
    )(idx, rows, table)


def _coef_body(pre_ref, gate_ref, coef_ref):
    coef_ref[...] = gate_ref[...] * _gelu(pre_ref[...])


def _coef(pre, gates, tm):
    t = pre.shape[0]
    row = pl.BlockSpec((tm, PEER_HK), lambda i: (i, 0))
    return pl.pallas_call(_coef_body, grid=(t // tm,), in_specs=[row, row], out_specs=row,
                          out_shape=jax.ShapeDtypeStruct((t, PEER_HK), F32), name="coef")(pre, gates)


def _final_body(x1_ref, peer_ref, g2_ref, fng_ref, y_ref):
    x2 = x1_ref[...] + _mod_rows(g2_ref) * peer_ref[...]
    y_ref[...] = x2 * lax.rsqrt(jnp.mean(x2 * x2, axis=-1, keepdims=True) + EPS) * fng_ref[...]


def _final(x1, peer_out, mod, rows_per_batch, final_g, tm):
    t = x1.shape[0]
    row = pl.BlockSpec((tm, D_MODEL), lambda i: (i, 0))
    return pl.pallas_call(
        _final_body, grid=(t // tm,),
        in_specs=[row, row, _mod_spec(5, rows_per_batch, tm), _const_spec((1, D_MODEL))],
        out_specs=row, out_shape=jax.ShapeDtypeStruct((t, D_MODEL), F32), name="final",
    )(x1, peer_out, mod, final_g.reshape(1, -1))


def _expert(idx, h2, gates, x1, mod, rows_per_batch, final_g, expert_u, expert_v, tm):
    pre = _peer_sc(_peer_u_body, idx, h2, expert_u, PEER_HK, "peer_u")
    coef = _coef(pre, gates, tm)
    peer_out = _peer_sc(_peer_v_body, idx, coef, expert_v, D_MODEL, "peer_v")
    return _final(x1, peer_out, mod, rows_per_batch, final_g, tm)


def _group(x, mod, conv_buf, s0, pool_buf, start, chunk, tm, wts):
    b, l, _ = x.shape
    t = b * l
    x2d = x.reshape(t, D_MODEL)
    if l >= tm:
        modx = mod.reshape(b, 6, 1, D_MODEL).transpose(1, 0, 2, 3)
    else:
        modx = jnp.repeat(mod.reshape(b, 6, D_MODEL), l, axis=0).transpose(1, 0, 2)
    outs = _inproj(x2d, modx, l, wts["norm1_g"], wts["w_cat"], tm)
    lp = -(-l // chunk) * chunk
    proj = {}
    for (name, w), a in zip(_IN_BLOCKS, outs):
        a = a.reshape(b, l, w)
        proj[name] = a if lp == l else jnp.pad(a, ((0, 0), (0, lp - l), (0, 0)))
    mixed, nconv, ns, npool = _mixer(proj, conv_buf, s0, pool_buf, start, l, chunk,
                                     wts["conv_w"], wts["a_log"], wts["dt_bias"], wts["dn_norm_g"],
                                     wts["w_pool"], wts["pool_scale"])
    mixed2d = mixed[:, :l].reshape(t, D_MODEL)
    x1, h2, idx, gates = _post(mixed2d, x2d, modx, l, wts["norm2_g"], wts["w_out"], wts["w_query"],
                               wts["keys"], tm)
    y = _expert(idx, h2, gates, x1, modx, l, wts["final_norm_g"], wts["expert_u"], wts["expert_v"], tm)
    return y.reshape(b, l, D_MODEL), nconv, ns, npool, idx


def kernel(x_prompt, x_sample, c_prompt, c_sample, state_conv, state_delta, state_pool, w_ada, b_ada, norm1_g, w_in, conv_w, a_log, dt_bias, dn_norm_g, w_pool, pool_scale, w_out, norm2_g, w_query, sub_keys, expert_u, expert_v, final_norm_g):
    bp = x_prompt.shape[0]
    bs = x_sample.shape[0]
    yp, ys = x_prompt, x_sample
    conv_p, delta_p, pool_p, conv_s, delta_s, pool_s = [], [], [], [], [], []
    zero_conv = jnp.zeros((bp, CONV_WIDTH - 1, QKV_WIDTH), F32)
    zero_delta = jnp.zeros((bp, DN_HEADS, DN_HEAD_DIM, DN_HEAD_DIM), F32)
    zero_pool = jnp.zeros((bp, POOL_BUF, POOL_WIDTH), F32)
    c_all = jnp.concatenate([c_prompt, c_sample], axis=0)
    for layer in range(DEPTH):
        wi = w_in[layer]
        o_b = QKV_WIDTH
        o_z = o_b + 2 * DN_HEADS
        w_ba = jnp.pad(wi[:, o_b:o_z], ((0, 0), (0, LANES - 2 * DN_HEADS)))
        w_cat = jnp.concatenate([wi[:, :o_b], wi[:, o_z:], w_ba], axis=1).astype(BF16)
        last = layer == DEPTH - 1
        wts = dict(
            norm1_g=norm1_g[layer], w_cat=w_cat, conv_w=conv_w[layer], a_log=a_log[layer],
            dt_bias=dt_bias[layer], dn_norm_g=dn_norm_g[layer], w_pool=w_pool[layer],
            pool_scale=pool_scale[layer], w_out=w_out[layer].astype(BF16), norm2_g=norm2_g[layer],
            w_query=w_query[layer].astype(BF16),
            keys=sub_keys[layer].reshape(2 * PEER_HEADS, PEER_NKEYS, PEER_KEY_HALF).astype(BF16),
            expert_u=expert_u[layer], expert_v=expert_v[layer],
            final_norm_g=final_norm_g if last else jnp.ones_like(final_norm_g))
        mod = _ada(c_all, w_ada[layer], b_ada[layer])
        assert last, "final norm is fused into the expert stage"
        parts = []
        order = None
        for b0 in range(0, bp, bp // PROMPT_PARTS):
            bsl = slice(b0, b0 + bp // PROMPT_PARTS)
            xin = yp[bsl]
            if order is not None:
                xin, order = lax.optimization_barrier((xin, order))
            *outs, order = _group(xin, mod[bsl], zero_conv[bsl], zero_delta[bsl], zero_pool[bsl],
                                  0, DN_CHUNK, 256, wts)
            parts.append(outs)
        yp, cp, sp, pp = (jnp.concatenate(a, axis=0) for a in zip(*parts))
        ys, order = lax.optimization_barrier((ys, order))
        ys, cs, ss, ps, _ = _group(ys, mod[bp:], state_conv[layer], state_delta[layer],
                                   state_pool[layer], PAST_LEN, SUBLANES, 256, wts)
        conv_p.append(cp)
        delta_p.append(sp)
        pool_p.append(pp)
        conv_s.append(cs)
        delta_s.append(ss)
        pool_s.append(ps)
    return (yp, ys, jnp.stack(conv_p), jnp.stack(delta_p), jnp.stack(pool_p),
            jnp.stack(conv_s), jnp.stack(delta_s), jnp.stack(pool_s))
```

```python
import functools

import jax
import jax.numpy as jnp
from jax import lax
from jax.experimental import pallas as pl
from jax.experimental.pallas import tpu as pltpu
from jax.experimental.pallas import tpu_sc as plsc

F32 = jnp.float32
BF16 = jnp.bfloat16
I32 = jnp.int32

D_MODEL = 1024
DEPTH = 1
PAST_LEN = 16384
DN_HEADS = 8
DN_HEAD_DIM = 128
DN_WIDTH = DN_HEADS * DN_HEAD_DIM
QKV_WIDTH = 3 * DN_WIDTH
CONV_WIDTH = 4
DN_CHUNK = 64
POOL_WINDOWS = (2, 4, 8, 16)
POOL_GROUP_DIM = 128
POOL_WIDTH = len(POOL_WINDOWS) * POOL_GROUP_DIM
POOL_OUT_GROUP = D_MODEL // len(POOL_WINDOWS)
POOL_BUF = max(POOL_WINDOWS) - 1
PEER_HEADS = 8
PEER_NKEYS = 128
PEER_TOPK = 16
PEER_KEY_HALF = 128
PEER_HK = PEER_HEADS * PEER_TOPK
EPS = 1e-6

LANES = 128
SUBLANES = 8
CONV_PAD = SUBLANES
POOL_PAD = 16
VMEM_LIMIT = 56 * 1024 * 1024

NT_DIMS = (((1,), (1,)), ((), ()))
TN_DIMS = (((0,), (0,)), ((), ()))


def _dot(a, b):
    return jnp.dot(a.astype(BF16), b.astype(BF16), preferred_element_type=F32)


def _dot_nt(a, b):
    return lax.dot_general(a.astype(BF16), b.astype(BF16), NT_DIMS, preferred_element_type=F32)


def _split3(x):
    hi = x.astype(BF16)
    r1 = x - hi.astype(F32)
    mid = r1.astype(BF16)
    lo = (r1 - mid.astype(F32)).astype(BF16)
    return hi, mid, lo


def _silu(x):
    return x * jax.nn.sigmoid(x)


def _gelu(x):
    return 0.5 * x * (1.0 + lax.erf(x * (0.5 ** 0.5)))


def _softplus(x):
    return jnp.maximum(x, 0.0) + jnp.log(1.0 + jnp.exp(-jnp.abs(x)))


def _mod_rows(ref):
    m = ref[...]
    return m.reshape(m.shape[-2], m.shape[-1])


def _mod_spec(k, rows_per_batch, tm):
    if rows_per_batch >= tm:
        tiles = rows_per_batch // tm
        return pl.BlockSpec((1, 1, 1, D_MODEL), lambda i, *_: (k, i // tiles, 0, 0))
    return pl.BlockSpec((1, tm, D_MODEL), lambda i, *_: (k, i, 0))


def _const_spec(shape):
    nd = len(shape)
    return pl.BlockSpec(shape, lambda *_: (0,) * nd)


def _ada_body(c_ref, w_ref, b_ref, o_ref):
    o_ref[...] = _dot(_silu(c_ref[...]), w_ref[...]) + b_ref[...]


def _ada(c, w_ada, b_ada):
    n = c.shape[0]
    return pl.pallas_call(
        _ada_body,
        grid=(6,),
        in_specs=[pl.BlockSpec((n, D_MODEL), lambda j: (0, 0)),
                  pl.BlockSpec((D_MODEL, D_MODEL), lambda j: (0, j)),
                  pl.BlockSpec((1, D_MODEL), lambda j: (0, j))],
        out_specs=pl.BlockSpec((n, D_MODEL), lambda j: (0, j)),
        out_shape=jax.ShapeDtypeStruct((n, 6 * D_MODEL), F32),
        name="ada",
    )(c, w_ada, b_ada.reshape(1, -1))


_IN_BLOCKS = (("qkv", QKV_WIDTH), ("z", DN_WIDTH), ("pool", POOL_WIDTH),
              ("ga", D_MODEL), ("gb", D_MODEL), ("ba", LANES))
_IN_TOTAL = sum(w for _, w in _IN_BLOCKS)
_IN_COL_CHUNK = 512


def _inproj_body(x_ref, sc_ref, sh_ref, g_ref, w_ref, *out_refs):
    x = x_ref[...]
    y = x * lax.rsqrt(jnp.mean(x * x, axis=-1, keepdims=True) + EPS) * g_ref[...]
    h = (y * (1.0 + _mod_rows(sc_ref)) + _mod_rows(sh_ref)).astype(BF16)
    off = 0
    for (_, width), o_ref in zip(_IN_BLOCKS, out_refs):
        for c0 in range(0, width, _IN_COL_CHUNK):
            cw = min(_IN_COL_CHUNK, width - c0)
            o_ref[:, c0:c0 + cw] = jnp.dot(h, w_ref[:, off + c0:off + c0 + cw],
                                           preferred_element_type=F32)
        off += width


def _inproj(x2d, mod, rows_per_batch, norm_g, w_cat, tm):
    t = x2d.shape[0]
    row = lambda w: pl.BlockSpec((tm, w), lambda i: (i, 0))
    return pl.pallas_call(
        _inproj_body,
        grid=(t // tm,),
        in_specs=[row(D_MODEL), _mod_spec(1, rows_per_batch, tm), _mod_spec(0, rows_per_batch, tm),
                  _const_spec((1, D_MODEL)),
                  pl.BlockSpec((D_MODEL, _IN_TOTAL), lambda i: (0, 0), pipeline_mode=pl.Buffered(1))],
        out_specs=[row(w) for _, w in _IN_BLOCKS],
        out_shape=[jax.ShapeDtypeStruct((t, w), F32) for _, w in _IN_BLOCKS],
        compiler_params=pltpu.CompilerParams(vmem_limit_bytes=VMEM_LIMIT),
        name="inproj",
    )(x2d, mod, mod, norm_g.reshape(1, -1), w_cat)


def _mixer_body(C, Lv, start,
                qkv_ref, ba_ref, z_ref, pin_ref, ga_ref, gb_ref, cbuf_ref, s0_ref, pbuf_ref,
                convw_ref, alog_ref, dtb_ref, dng_ref, wpool_ref, pscale_ref,
                mixed_ref, nconv_ref, ns_ref, npool_ref,
                xp_scr, act_scr, s_scr, pp_scr, odn_scr):
    n = pl.program_id(1)
    last = pl.num_programs(1) - 1

    @pl.when(n == 0)
    def _load_state():
        xp_scr[0:CONV_PAD, :] = cbuf_ref[0]
        pp_scr[0:POOL_PAD, :] = pbuf_ref[0]
        s_scr[...] = s0_ref[0]

    xp_scr[CONV_PAD:CONV_PAD + C, :] = qkv_ref[0]
    for c0 in range(0, QKV_WIDTH, 512):
        cs = slice(c0, c0 + 512)
        y = xp_scr[CONV_PAD:CONV_PAD + C, cs] * convw_ref[CONV_WIDTH - 1:CONV_WIDTH, cs]
        for k in range(CONV_WIDTH - 1):
            r0 = CONV_PAD - (CONV_WIDTH - 1) + k
            y = y + xp_scr[r0:r0 + C, cs] * convw_ref[k:k + 1, cs]
        act_scr[:, cs] = _silu(y)

    ba = ba_ref[0]
    lane = lax.broadcasted_iota(I32, (C, LANES), 1)
    beta_all = jax.nn.sigmoid(ba)
    g_all = -jnp.exp(alog_ref[...]) * _softplus(ba + dtb_ref[...])
    if Lv < C:
        valid = lax.broadcasted_iota(I32, (C, LANES), 0) < Lv
        beta_all = jnp.where(valid, beta_all, 0.0)
        g_all = jnp.where(valid, g_all, 0.0)
    ii = lax.broadcasted_iota(I32, (C, C), 0)
    jj = lax.broadcasted_iota(I32, (C, C), 1)
    causal = ii >= jj
    strict = ii > jj
    tril = jnp.where(causal, 1.0, 0.0).astype(BF16)
    eye = jnp.where(ii == jj, 1.0, 0.0)
    gc_all = sum(jnp.dot(tril, part, preferred_element_type=F32) for part in _split3(g_all))
    if C < LANES:
        gc_sq = jnp.concatenate([gc_all, jnp.zeros((LANES - C, LANES), F32)], axis=0)
    else:
        gc_sq = gc_all
    gc_t = gc_sq.T

    for h in range(DN_HEADS):
        hs = slice(h * DN_HEAD_DIM, (h + 1) * DN_HEAD_DIM)
        beta = jnp.sum(jnp.where(lane == h, beta_all, 0.0), axis=1, keepdims=True)
        gcol = jnp.sum(jnp.where(lane == DN_HEADS + h, gc_all, 0.0), axis=1, keepdims=True)
        grow = gc_t[DN_HEADS + h:DN_HEADS + h + 1, 0:C]
        glast = gcol[C - 1:C, :]

        q = act_scr[:, hs]
        k = act_scr[:, DN_WIDTH + h * DN_HEAD_DIM:DN_WIDTH + (h + 1) * DN_HEAD_DIM]
        v = act_scr[:, 2 * DN_WIDTH + h * DN_HEAD_DIM:2 * DN_WIDTH + (h + 1) * DN_HEAD_DIM]
        q = q * lax.rsqrt(jnp.sum(q * q, axis=-1, keepdims=True) + EPS) * (DN_HEAD_DIM ** -0.5)
        k = k * lax.rsqrt(jnp.sum(k * k, axis=-1, keepdims=True) + EPS)
        kb = k * beta
        vb = v * beta

        decay = jnp.where(causal, jnp.exp(jnp.where(causal, gcol - grow, 0.0)), 0.0)
        lower = jnp.where(strict, _dot_nt(kb, k) * decay, 0.0)
        ainv = eye - lower
        pw = lower
        p = 1
        while 2 * p < C:
            pw = _dot(pw, pw)
            ainv = ainv + _dot(ainv, pw)
            p *= 2
        sol = _dot(ainv, jnp.concatenate([vb, kb * jnp.exp(gcol)], axis=1))
        u = sol[:, :DN_HEAD_DIM]
        w = sol[:, DN_HEAD_DIM:]
        qk = _dot_nt(q, k) * decay
        k_tail = k * jnp.exp(glast - gcol)

        S = s_scr[h]
        v_new = u - _dot(w, S)
        o = _dot(q * jnp.exp(gcol), S) + _dot(qk, v_new)
        s_scr[h] = S * jnp.exp(glast) + lax.dot_general(
            k_tail.astype(BF16), v_new.astype(BF16), TN_DIMS, preferred_element_type=F32)

        zf = z_ref[0, :, hs]
        o = o * lax.rsqrt(jnp.mean(o * o, axis=-1, keepdims=True) + EPS) * dng_ref[...] * _silu(zf)
        odn_scr[:, hs] = o

    pp_scr[POOL_PAD:POOL_PAD + C, :] = pin_ref[0]
    pos = start + n * C + lax.broadcasted_iota(I32, (C, 1), 0)
    for gi, win in enumerate(POOL_WINDOWS):
        gs = slice(gi * POOL_GROUP_DIM, (gi + 1) * POOL_GROUP_DIM)
        xg = pp_scr[POOL_PAD:POOL_PAD + C, gs]
        ssum = xg
        for sft in range(1, win):
            ssum = ssum + pp_scr[POOL_PAD - sft:POOL_PAD - sft + C, gs]
        cnt = jnp.minimum(pos + 1, win).astype(F32)
        pooled = ssum / cnt - xg
        os_ = slice(gi * POOL_OUT_GROUP, (gi + 1) * POOL_OUT_GROUP)
        yp = _dot(pooled, wpool_ref[gi]) * pscale_ref[:, os_]
        mixed_ref[0, :, os_] = (jax.nn.sigmoid(ga_ref[0, :, os_]) * odn_scr[:, os_]
                                + jax.nn.sigmoid(gb_ref[0, :, os_]) * yp)

    @pl.when(n == last)
    def _store_state():
        nconv_ref[0] = xp_scr[Lv + CONV_PAD - (CONV_WIDTH - 1):Lv + CONV_PAD, :]
        npool_ref[0] = pp_scr[Lv + POOL_PAD - POOL_BUF:Lv + POOL_PAD, :]
        ns_ref[0] = s_scr[...]

    xp_scr[0:CONV_PAD, :] = xp_scr[C:C + CONV_PAD, :]
    pp_scr[0:POOL_PAD, :] = pp_scr[C:C + POOL_PAD, :]


def _mixer(proj, conv_buf, s0, pool_buf, start, seq_len, C,
           conv_w, a_log, dt_bias, dn_norm_g, w_pool, pool_scale):
    b, lp, _ = proj["qkv"].shape
    nchunks = lp // C
    lv = seq_len - (nchunks - 1) * C
    cbuf = jnp.pad(conv_buf, ((0, 0), (CONV_PAD - (CONV_WIDTH - 1), 0), (0, 0)))
    pbuf = jnp.pad(pool_buf, ((0, 0), (POOL_PAD - POOL_BUF, 0), (0, 0)))
    lane_pad = lambda a: jnp.pad(a.reshape(1, -1), ((0, 0), (DN_HEADS, LANES - 2 * DN_HEADS)))
    chunk = lambda w: pl.BlockSpec((1, C, w), lambda i, j: (i, j, 0))
    state = lambda *s: pl.BlockSpec((1,) + s, lambda i, j: (i,) + (0,) * len(s))
    return pl.pallas_call(
        functools.partial(_mixer_body, C, lv, start),
        grid=(b, nchunks),
        in_specs=[chunk(QKV_WIDTH), chunk(LANES), chunk(DN_WIDTH), chunk(POOL_WIDTH),
                  chunk(D_MODEL), chunk(D_MODEL),
                  state(CONV_PAD, QKV_WIDTH), state(DN_HEADS, DN_HEAD_DIM, DN_HEAD_DIM),
                  state(POOL_PAD, POOL_WIDTH),
                  _const_spec((CONV_WIDTH, QKV_WIDTH)), _const_spec((1, LANES)), _const_spec((1, LANES)),
                  _const_spec((1, DN_HEAD_DIM)),
                  _const_spec((len(POOL_WINDOWS), POOL_GROUP_DIM, POOL_OUT_GROUP)),
                  _const_spec((1, D_MODEL))],
        out_specs=[chunk(D_MODEL), state(CONV_WIDTH - 1, QKV_WIDTH),
                   state(DN_HEADS, DN_HEAD_DIM, DN_HEAD_DIM), state(POOL_BUF, POOL_WIDTH)],
        out_shape=[jax.ShapeDtypeStruct((b, lp, D_MODEL), F32),
                   jax.ShapeDtypeStruct((b, CONV_WIDTH - 1, QKV_WIDTH), F32),
                   jax.ShapeDtypeStruct((b, DN_HEADS, DN_HEAD_DIM, DN_HEAD_DIM), F32),
                   jax.ShapeDtypeStruct((b, POOL_BUF, POOL_WIDTH), F32)],
        scratch_shapes=[pltpu.VMEM((CONV_PAD + C + CONV_PAD, QKV_WIDTH), F32),
                        pltpu.VMEM((C, QKV_WIDTH), F32),
                        pltpu.VMEM((DN_HEADS, DN_HEAD_DIM, DN_HEAD_DIM), F32),
                        pltpu.VMEM((POOL_PAD + C + POOL_PAD, POOL_WIDTH), F32),
                        pltpu.VMEM((C, DN_WIDTH), F32)],
        compiler_params=pltpu.CompilerParams(dimension_semantics=("arbitrary", "arbitrary"),
                                             vmem_limit_bytes=VMEM_LIMIT),
        name="mixer",
    )(proj["qkv"], proj["ba"], proj["z"], proj["pool"], proj["ga"], proj["gb"], cbuf, s0, pbuf,
      conv_w, lane_pad(a_log), lane_pad(dt_bias), dn_norm_g.reshape(1, -1), w_pool,
      pool_scale.reshape(1, -1))


def _top16(s, ids, payload=None):
    big = float(2 ** 24)
    vals, sel, pays = [], [], []
    for _ in range(PEER_TOPK):
        m = jnp.max(s, axis=0, keepdims=True)
        am = jnp.min(jnp.where(s == m, ids, big), axis=0, keepdims=True)
        hit = ids == am
        if payload is not None:
            pays.append(jnp.max(jnp.where(hit, payload, -1.0), axis=0, keepdims=True))
        s = jnp.where(hit, -jnp.inf, s)
        vals.append(m)
        sel.append(am)
    out = (jnp.concatenate(vals, axis=0), jnp.concatenate(sel, axis=0))
    if payload is not None:
        out += (jnp.concatenate(pays, axis=0),)
    return out


_CAND_EDGE = 4


def _post_body(mixed_ref, x_ref, g1_ref, sc2_ref, sh2_ref, n2g_ref, wout_ref, wq_ref, keys_ref,
               x1_ref, h2_ref, idx_ref, gate_ref):
    tm = x_ref.shape[0]
    x1 = x_ref[...] + _mod_rows(g1_ref) * _dot(mixed_ref[...], wout_ref[...])
    x1_ref[...] = x1
    y = x1 * lax.rsqrt(jnp.mean(x1 * x1, axis=-1, keepdims=True) + EPS) * n2g_ref[...]
    h2 = y * (1.0 + _mod_rows(sc2_ref)) + _mod_rows(sh2_ref)
    h2_ref[...] = h2
    q = _dot(h2, wq_ref[...])

    K = PEER_TOPK
    key_id = lax.broadcasted_iota(I32, (PEER_NKEYS, 1), 0).astype(F32)
    r16 = lax.broadcasted_iota(I32, (K, 1), 0)
    cand_id = jnp.concatenate([(a * K + r16) for a in range(_CAND_EDGE)]
                              + [(r16 * K + b) for b in range(_CAND_EDGE)], axis=0).astype(F32)
    dup = r16 < _CAND_EDGE
    idx_rows, gate_rows = [], []
    for h in range(PEER_HEADS):
        half = []
        for p in range(2):
            c0 = (h * 2 + p) * PEER_KEY_HALF
            st = _dot_nt(keys_ref[h * 2 + p], q[:, c0:c0 + PEER_KEY_HALF])
            half.append(_top16(st, key_id))
        (s1, i1), (s2, i2) = half
        cand = jnp.concatenate(
            [s1[a:a + 1] + s2 for a in range(_CAND_EDGE)]
            + [jnp.where(dup, -jnp.inf, s1 + s2[b:b + 1]) for b in range(_CAND_EDGE)], axis=0)
        cidx = jnp.concatenate(
            [i1[a:a + 1] * PEER_NKEYS + i2 for a in range(_CAND_EDGE)]
            + [i1 * PEER_NKEYS + i2[b:b + 1] for b in range(_CAND_EDGE)], axis=0)
        best, _, eidx = _top16(cand, cand_id, cidx)
        e = jnp.exp(best - best[0:1])
        gate_rows.append(e / jnp.sum(e, axis=0, keepdims=True))
        idx_rows.append(eidx)
    idx_ref[...] = jnp.concatenate(idx_rows, axis=0).T.astype(I32)
    gate_ref[...] = jnp.concatenate(gate_rows, axis=0).T


def _post(mixed2d, x2d, mod, rows_per_batch, norm2_g, w_out, w_query, keys, tm):
    t = x2d.shape[0]
    row = lambda w: pl.BlockSpec((tm, w), lambda i: (i, 0))
    return pl.pallas_call(
        _post_body,
        grid=(t // tm,),
        in_specs=[row(D_MODEL), row(D_MODEL),
                  _mod_spec(2, rows_per_batch, tm), _mod_spec(4, rows_per_batch, tm),
                  _mod_spec(3, rows_per_batch, tm), _const_spec((1, D_MODEL)),
                  _const_spec((D_MODEL, D_MODEL)), _const_spec((D_MODEL, 2 * PEER_HEADS * PEER_KEY_HALF)),
                  _const_spec((2 * PEER_HEADS, PEER_NKEYS, PEER_KEY_HALF))],
        out_specs=[row(D_MODEL), row(D_MODEL), row(PEER_HK), row(PEER_HK)],
        out_shape=[jax.ShapeDtypeStruct((t, D_MODEL), F32), jax.ShapeDtypeStruct((t, D_MODEL), F32),
                   jax.ShapeDtypeStruct((t, PEER_HK), I32), jax.ShapeDtypeStruct((t, PEER_HK), F32)],
        compiler_params=pltpu.CompilerParams(vmem_limit_bytes=VMEM_LIMIT),
        name="post",
    )(mixed2d, x2d, mod, mod, mod, norm2_g.reshape(1, -1), w_out, w_query, keys)


SC_CORES = 2
SC_SUBCORES = 16
SC_LANES = 16
SC_WORKERS = SC_CORES * SC_SUBCORES
SC_TOKENS = 16
SC_SLOTS = 4
SC_CHUNKS = D_MODEL // SC_LANES
PROMPT_PARTS = 8


def _sc_mesh():
    return plsc.VectorSubcoreMesh(core_axis_name="c", subcore_axis_name="s")


def _sc_worker():
    return lax.axis_index("s") * SC_CORES + lax.axis_index("c")


def _sc_jobs(table_hbm, idx_v, buf, sem, compute):
    njobs = SC_TOKENS * PEER_HEADS

    def copy(j, slot):
        tt = j // PEER_HEADS
        h = j % PEER_HEADS
        rows = idx_v[tt, pl.ds(h * PEER_TOPK, PEER_TOPK)]
        return pltpu.make_async_copy(table_hbm.at[rows], buf.at[slot], sem.at[slot])

    for s in range(SC_SLOTS):
        copy(s, s).start()

    def group(g, c):
        for s in range(SC_SLOTS):
            j = g * SC_SLOTS + s
            copy(j, s).wait()
            compute(j // PEER_HEADS, j % PEER_HEADS, s)

            @pl.when(j + SC_SLOTS < njobs)
            def _next():
                copy(j + SC_SLOTS, s).start()
        return c

    lax.fori_loop(0, njobs // SC_SLOTS, group, 0)


def _peer_u_body(n_tok, idx_hbm, h2_hbm, u_hbm, pre_hbm, idx_v, h2_v, pre_v, ubuf, acc_v, sem):
    base = _sc_worker() * n_tok
    lane = lax.iota(I32, SC_LANES)

    def compute(tt, h, slot):
        def chunk(c, accs):
            xv = h2_v[tt, pl.ds(c * SC_LANES, SC_LANES)]
            return tuple(a + ubuf[slot, k, pl.ds(c * SC_LANES, SC_LANES)] * xv
                         for k, a in enumerate(accs))
        zero = jnp.zeros((SC_LANES,), F32)
        accs = lax.fori_loop(0, SC_CHUNKS, chunk, (zero,) * PEER_TOPK)
        for k, a in enumerate(accs):
            acc_v[k, :] = a
        tot = zero
        for j in range(SC_LANES):
            tot = tot + plsc.load_gather(acc_v, [lane, jnp.full((SC_LANES,), j, I32)])
        pre_v[tt, pl.ds(h * PEER_TOPK, PEER_TOPK)] = tot

    def block(bi, c):
        t0 = base + bi * SC_TOKENS
        pltpu.sync_copy(idx_hbm.at[pl.ds(t0, SC_TOKENS)], idx_v)
        pltpu.sync_copy(h2_hbm.at[pl.ds(t0, SC_TOKENS)], h2_v)
        _sc_jobs(u_hbm, idx_v, ubuf, sem, compute)
        pltpu.sync_copy(pre_v, pre_hbm.at[pl.ds(t0, SC_TOKENS)])
        return c

    lax.fori_loop(0, n_tok // SC_TOKENS, block, 0)


def _peer_v_body(n_tok, idx_hbm, coef_hbm, v_hbm, out_hbm, idx_v, coef_v, out_v, vbuf, sem):
    base = _sc_worker() * n_tok
    zero = jnp.zeros((SC_LANES,), F32)

    def compute(tt, h, slot):
        row = jnp.full((SC_LANES,), tt, I32)
        cb = [plsc.load_gather(coef_v, [row, jnp.full((SC_LANES,), h * PEER_TOPK + k, I32)])
              for k in range(PEER_TOPK)]

        @plsc.parallel_loop(0, SC_CHUNKS, unroll=2)
        def _chunk(c):
            cs = pl.ds(c * SC_LANES, SC_LANES)
            terms = [cb[k] * vbuf[slot, k, cs] for k in range(PEER_TOPK)]
            while len(terms) > 1:
                terms = [a + b for a, b in zip(terms[0::2], terms[1::2])]
            plsc.addupdate(out_v.at[tt, cs], terms[0])

    def block(bi, c):
        t0 = base + bi * SC_TOKENS
        pltpu.sync_copy(idx_hbm.at[pl.ds(t0, SC_TOKENS)], idx_v)
        pltpu.sync_copy(coef_hbm.at[pl.ds(t0, SC_TOKENS)], coef_v)

        def clear(i, cc):
            out_v[i // SC_CHUNKS, pl.ds((i % SC_CHUNKS) * SC_LANES, SC_LANES)] = zero
            return cc
        lax.fori_loop(0, SC_TOKENS * SC_CHUNKS, clear, 0)
        _sc_jobs(v_hbm, idx_v, vbuf, sem, compute)
        pltpu.sync_copy(out_v, out_hbm.at[pl.ds(t0, SC_TOKENS)])
        return c

    lax.fori_loop(0, n_tok // SC_TOKENS, block, 0)


def _peer_sc(body, idx, rows, table, out_width, name):
    t = idx.shape[0]
    assert t % (SC_WORKERS * SC_TOKENS) == 0
    n_tok = t // SC_WORKERS
    return pl.kernel(
        functools.partial(body, n_tok),
        out_type=jax.ShapeDtypeStruct((t, out_width), F32),
        mesh=_sc_mesh(),
        scratch_types=[pltpu.VMEM((SC_TOKENS, PEER_HK), I32),
                       pltpu.VMEM((SC_TOKENS, rows.shape[1]), F32),
                       pltpu.VMEM((SC_TOKENS, out_width), F32),
                       pltpu.VMEM((SC_SLOTS, PEER_TOPK, D_MODEL), F32)]
                      + ([pltpu.VMEM((PEER_TOPK, SC_LANES), F32)] if body is _peer_u_body else [])
                      + [pltpu.SemaphoreType.DMA((SC_SLOTS,))],
        compiler_params=pltpu.CompilerParams(needs_layout_passes=False),
        name=name,
    )(idx, rows, table)


def _coef_body(pre_ref, gate_ref, coef_ref):
    coef_ref[...] = gate_ref[...] * _gelu(pre_ref[...])


def _coef(pre, gates, tm):
    t = pre.shape[0]
    row = pl.BlockSpec((tm, PEER_HK), lambda i: (i, 0))
    return pl.pallas_call(_coef_body, grid=(t // tm,), in_specs=[row, row], out_specs=row,
                          out_shape=jax.ShapeDtypeStruct((t, PEER_HK), F32), name="coef")(pre, gates)


def _final_body(x1_ref, peer_ref, g2_ref, fng_ref, y_ref):
    x2 = x1_ref[...] + _mod_rows(g2_ref) * peer_ref[...]
    y_ref[...] = x2 * lax.rsqrt(jnp.mean(x2 * x2, axis=-1, keepdims=True) + EPS) * fng_ref[...]


def _final(x1, peer_out, mod, rows_per_batch, final_g, tm):
    t = x1.shape[0]
    row = pl.BlockSpec((tm, D_MODEL), lambda i: (i, 0))
    return pl.pallas_call(
        _final_body, grid=(t // tm,),
        in_specs=[row, row, _mod_spec(5, rows_per_batch, tm), _const_spec((1, D_MODEL))],
        out_specs=row, out_shape=jax.ShapeDtypeStruct((t, D_MODEL), F32), name="final",
    )(x1, peer_out, mod, final_g.reshape(1, -1))


def _expert(idx, h2, gates, x1, mod, rows_per_batch, final_g, expert_u, expert_v, tm):
    pre = _peer_sc(_peer_u_body, idx, h2, expert_u, PEER_HK, "peer_u")
    coef = _coef(pre, gates, tm)
    peer_out = _peer_sc(_peer_v_body, idx, coef, expert_v, D_MODEL, "peer_v")
    return _final(x1, peer_out, mod, rows_per_batch, final_g, tm)


def _group(x, mod, conv_buf, s0, pool_buf, start, chunk, tm, wts):
    b, l, _ = x.shape
    t = b * l
    x2d = x.reshape(t, D_MODEL)
    if l >= tm:
        modx = mod.reshape(b, 6, 1, D_MODEL).transpose(1, 0, 2, 3)
    else:
        modx = jnp.repeat(mod.reshape(b, 6, D_MODEL), l, axis=0).transpose(1, 0, 2)
    outs = _inproj(x2d, modx, l, wts["norm1_g"], wts["w_cat"], tm)
    lp = -(-l // chunk) * chunk
    proj = {}
    for (name, w), a in zip(_IN_BLOCKS, outs):
        a = a.reshape(b, l, w)
        proj[name] = a if lp == l else jnp.pad(a, ((0, 0), (0, lp - l), (0, 0)))
    mixed, nconv, ns, npool = _mixer(proj, conv_buf, s0, pool_buf, start, l, chunk,
                                     wts["conv_w"], wts["a_log"], wts["dt_bias"], wts["dn_norm_g"],
                                     wts["w_pool"], wts["pool_scale"])
    mixed2d = mixed[:, :l].reshape(t, D_MODEL)
    x1, h2, idx, gates = _post(mixed2d, x2d, modx, l, wts["norm2_g"], wts["w_out"], wts["w_query"],
                               wts["keys"], tm)
    y = _expert(idx, h2, gates, x1, modx, l, wts["final_norm_g"], wts["expert_u"], wts["expert_v"], tm)
    return y.reshape(b, l, D_MODEL), nconv, ns, npool


def kernel(x_prompt, x_sample, c_prompt, c_sample, state_conv, state_delta, state_pool, w_ada, b_ada, norm1_g, w_in, conv_w, a_log, dt_bias, dn_norm_g, w_pool, pool_scale, w_out, norm2_g, w_query, sub_keys, expert_u, expert_v, final_norm_g):
    bp = x_prompt.shape[0]
    bs = x_sample.shape[0]
    yp, ys = x_prompt, x_sample
    conv_p, delta_p, pool_p, conv_s, delta_s, pool_s = [], [], [], [], [], []
    zero_conv = jnp.zeros((bp, CONV_WIDTH - 1, QKV_WIDTH), F32)
    zero_delta = jnp.zeros((bp, DN_HEADS, DN_HEAD_DIM, DN_HEAD_DIM), F32)
    zero_pool = jnp.zeros((bp, POOL_BUF, POOL_WIDTH), F32)
    c_all = jnp.concatenate([c_prompt, c_sample], axis=0)
    for layer in range(DEPTH):
        wi = w_in[layer]
        o_b = QKV_WIDTH
        o_z = o_b + 2 * DN_HEADS
        w_ba = jnp.pad(wi[:, o_b:o_z], ((0, 0), (0, LANES - 2 * DN_HEADS)))
        w_cat = jnp.concatenate([wi[:, :o_b], wi[:, o_z:], w_ba], axis=1).astype(BF16)
        last = layer == DEPTH - 1
        wts = dict(
            norm1_g=norm1_g[layer], w_cat=w_cat, conv_w=conv_w[layer], a_log=a_log[layer],
            dt_bias=dt_bias[layer], dn_norm_g=dn_norm_g[layer], w_pool=w_pool[layer],
            pool_scale=pool_scale[layer], w_out=w_out[layer].astype(BF16), norm2_g=norm2_g[layer],
            w_query=w_query[layer].astype(BF16),
            keys=sub_keys[layer].reshape(2 * PEER_HEADS, PEER_NKEYS, PEER_KEY_HALF).astype(BF16),
            expert_u=expert_u[layer], expert_v=expert_v[layer],
            final_norm_g=final_norm_g if last else jnp.ones_like(final_norm_g))
        mod = _ada(c_all, w_ada[layer], b_ada[layer])
        assert last, "final norm is fused into the expert stage"
        parts = []
        for b0 in range(0, bp, bp // PROMPT_PARTS):
            bsl = slice(b0, b0 + bp // PROMPT_PARTS)
            parts.append(_group(yp[bsl], mod[bsl], zero_conv[bsl], zero_delta[bsl], zero_pool[bsl],
                                0, DN_CHUNK, 256, wts))
        yp, cp, sp, pp = (jnp.concatenate(a, axis=0) for a in zip(*parts))
        ys, cs, ss, ps = _group(ys, mod[bp:], state_conv[layer], state_delta[layer],
                                state_pool[layer], PAST_LEN, SUBLANES, 256, wts)
        conv_p.append(cp)
        delta_p.append(sp)
        pool_p.append(pp)
        conv_s.append(cs)
        delta_s.append(ss)
        pool_s.append(ps)
    return (yp, ys, jnp.stack(conv_p), jnp.stack(delta_p), jnp.stack(pool_p),
            jnp.stack(conv_s), jnp.stack(delta_s), jnp.stack(pool_s))
```

```python
import functools

import jax
import jax.numpy as jnp
from jax import lax
from jax.experimental import pallas as pl
from jax.experimental.pallas import tpu as pltpu
from jax.experimental.pallas import tpu_sc as plsc

F32 = jnp.float32
BF16 = jnp.bfloat16
I32 = jnp.int32

D_MODEL = 1024
DEPTH = 1
PAST_LEN = 16384
DN_HEADS = 8
DN_HEAD_DIM = 128
DN_WIDTH = DN_HEADS * DN_HEAD_DIM
QKV_WIDTH = 3 * DN_WIDTH
CONV_WIDTH = 4
DN_CHUNK = 64
POOL_WINDOWS = (2, 4, 8, 16)
POOL_GROUP_DIM = 128
POOL_WIDTH = len(POOL_WINDOWS) * POOL_GROUP_DIM
POOL_OUT_GROUP = D_MODEL // len(POOL_WINDOWS)
POOL_BUF = max(POOL_WINDOWS) - 1
PEER_HEADS = 8
PEER_NKEYS = 128
PEER_TOPK = 16
PEER_KEY_HALF = 128
PEER_HK = PEER_HEADS * PEER_TOPK
EPS = 1e-6

LANES = 128
SUBLANES = 8
CONV_PAD = SUBLANES
POOL_PAD = 16
VMEM_LIMIT = 56 * 1024 * 1024

NT_DIMS = (((1,), (1,)), ((), ()))
TN_DIMS = (((0,), (0,)), ((), ()))


def _dot(a, b):
    return jnp.dot(a.astype(BF16), b.astype(BF16), preferred_element_type=F32)


def _dot_nt(a, b):
    return lax.dot_general(a.astype(BF16), b.astype(BF16), NT_DIMS, preferred_element_type=F32)


def _split3(x):
    hi = x.astype(BF16)
    r1 = x - hi.astype(F32)
    mid = r1.astype(BF16)
    lo = (r1 - mid.astype(F32)).astype(BF16)
    return hi, mid, lo


def _silu(x):
    return x * jax.nn.sigmoid(x)


def _gelu(x):
    return 0.5 * x * (1.0 + lax.erf(x * (0.5 ** 0.5)))


def _softplus(x):
    return jnp.maximum(x, 0.0) + jnp.log(1.0 + jnp.exp(-jnp.abs(x)))


def _mod_rows(ref):
    m = ref[...]
    return m.reshape(m.shape[-2], m.shape[-1])


def _mod_spec(k, rows_per_batch, tm):
    if rows_per_batch >= tm:
        tiles = rows_per_batch // tm
        return pl.BlockSpec((1, 1, 1, D_MODEL), lambda i, *_: (k, i // tiles, 0, 0))
    return pl.BlockSpec((1, tm, D_MODEL), lambda i, *_: (k, i, 0))


def _const_spec(shape):
    nd = len(shape)
    return pl.BlockSpec(shape, lambda *_: (0,) * nd)


def _ada_body(c_ref, w_ref, b_ref, o_ref):
    o_ref[...] = _dot(_silu(c_ref[...]), w_ref[...]) + b_ref[...]


def _ada(c, w_ada, b_ada):
    n = c.shape[0]
    return pl.pallas_call(
        _ada_body,
        grid=(6,),
        in_specs=[pl.BlockSpec((n, D_MODEL), lambda j: (0, 0)),
                  pl.BlockSpec((D_MODEL, D_MODEL), lambda j: (0, j)),
                  pl.BlockSpec((1, D_MODEL), lambda j: (0, j))],
        out_specs=pl.BlockSpec((n, D_MODEL), lambda j: (0, j)),
        out_shape=jax.ShapeDtypeStruct((n, 6 * D_MODEL), F32),
        name="ada",
    )(c, w_ada, b_ada.reshape(1, -1))


_IN_BLOCKS = (("qkv", QKV_WIDTH), ("z", DN_WIDTH), ("pool", POOL_WIDTH),
              ("ga", D_MODEL), ("gb", D_MODEL), ("ba", LANES))
_IN_TOTAL = sum(w for _, w in _IN_BLOCKS)
_IN_COL_CHUNK = 512


def _inproj_body(x_ref, sc_ref, sh_ref, g_ref, w_ref, *out_refs):
    x = x_ref[...]
    y = x * lax.rsqrt(jnp.mean(x * x, axis=-1, keepdims=True) + EPS) * g_ref[...]
    h = (y * (1.0 + _mod_rows(sc_ref)) + _mod_rows(sh_ref)).astype(BF16)
    off = 0
    for (_, width), o_ref in zip(_IN_BLOCKS, out_refs):
        for c0 in range(0, width, _IN_COL_CHUNK):
            cw = min(_IN_COL_CHUNK, width - c0)
            o_ref[:, c0:c0 + cw] = jnp.dot(h, w_ref[:, off + c0:off + c0 + cw],
                                           preferred_element_type=F32)
        off += width


def _inproj(x2d, mod, rows_per_batch, norm_g, w_cat, tm):
    t = x2d.shape[0]
    row = lambda w: pl.BlockSpec((tm, w), lambda i: (i, 0))
    return pl.pallas_call(
        _inproj_body,
        grid=(t // tm,),
        in_specs=[row(D_MODEL), _mod_spec(1, rows_per_batch, tm), _mod_spec(0, rows_per_batch, tm),
                  _const_spec((1, D_MODEL)),
                  pl.BlockSpec((D_MODEL, _IN_TOTAL), lambda i: (0, 0), pipeline_mode=pl.Buffered(1))],
        out_specs=[row(w) for _, w in _IN_BLOCKS],
        out_shape=[jax.ShapeDtypeStruct((t, w), F32) for _, w in _IN_BLOCKS],
        compiler_params=pltpu.CompilerParams(vmem_limit_bytes=VMEM_LIMIT),
        name="inproj",
    )(x2d, mod, mod, norm_g.reshape(1, -1), w_cat)


def _mixer_body(C, Lv, start,
                qkv_ref, ba_ref, z_ref, pin_ref, ga_ref, gb_ref, cbuf_ref, s0_ref, pbuf_ref,
                convw_ref, alog_ref, dtb_ref, dng_ref, wpool_ref, pscale_ref,
                mixed_ref, nconv_ref, ns_ref, npool_ref,
                xp_scr, act_scr, s_scr, pp_scr, odn_scr):
    n = pl.program_id(1)
    last = pl.num_programs(1) - 1

    @pl.when(n == 0)
    def _load_state():
        xp_scr[0:CONV_PAD, :] = cbuf_ref[0]
        pp_scr[0:POOL_PAD, :] = pbuf_ref[0]
        s_scr[...] = s0_ref[0]

    xp_scr[CONV_PAD:CONV_PAD + C, :] = qkv_ref[0]
    for c0 in range(0, QKV_WIDTH, 512):
        cs = slice(c0, c0 + 512)
        y = xp_scr[CONV_PAD:CONV_PAD + C, cs] * convw_ref[CONV_WIDTH - 1:CONV_WIDTH, cs]
        for k in range(CONV_WIDTH - 1):
            r0 = CONV_PAD - (CONV_WIDTH - 1) + k
            y = y + xp_scr[r0:r0 + C, cs] * convw_ref[k:k + 1, cs]
        act_scr[:, cs] = _silu(y)

    ba = ba_ref[0]
    lane = lax.broadcasted_iota(I32, (C, LANES), 1)
    beta_all = jax.nn.sigmoid(ba)
    g_all = -jnp.exp(alog_ref[...]) * _softplus(ba + dtb_ref[...])
    if Lv < C:
        valid = lax.broadcasted_iota(I32, (C, LANES), 0) < Lv
        beta_all = jnp.where(valid, beta_all, 0.0)
        g_all = jnp.where(valid, g_all, 0.0)
    ii = lax.broadcasted_iota(I32, (C, C), 0)
    jj = lax.broadcasted_iota(I32, (C, C), 1)
    causal = ii >= jj
    strict = ii > jj
    tril = jnp.where(causal, 1.0, 0.0).astype(BF16)
    eye = jnp.where(ii == jj, 1.0, 0.0)
    gc_all = sum(jnp.dot(tril, part, preferred_element_type=F32) for part in _split3(g_all))
    if C < LANES:
        gc_sq = jnp.concatenate([gc_all, jnp.zeros((LANES - C, LANES), F32)], axis=0)
    else:
        gc_sq = gc_all
    gc_t = gc_sq.T

    for h in range(DN_HEADS):
        hs = slice(h * DN_HEAD_DIM, (h + 1) * DN_HEAD_DIM)
        beta = jnp.sum(jnp.where(lane == h, beta_all, 0.0), axis=1, keepdims=True)
        gcol = jnp.sum(jnp.where(lane == DN_HEADS + h, gc_all, 0.0), axis=1, keepdims=True)
        grow = gc_t[DN_HEADS + h:DN_HEADS + h + 1, 0:C]
        glast = gcol[C - 1:C, :]

        q = act_scr[:, hs]
        k = act_scr[:, DN_WIDTH + h * DN_HEAD_DIM:DN_WIDTH + (h + 1) * DN_HEAD_DIM]
        v = act_scr[:, 2 * DN_WIDTH + h * DN_HEAD_DIM:2 * DN_WIDTH + (h + 1) * DN_HEAD_DIM]
        q = q * lax.rsqrt(jnp.sum(q * q, axis=-1, keepdims=True) + EPS) * (DN_HEAD_DIM ** -0.5)
        k = k * lax.rsqrt(jnp.sum(k * k, axis=-1, keepdims=True) + EPS)
        kb = k * beta
        vb = v * beta

        decay = jnp.where(causal, jnp.exp(jnp.where(causal, gcol - grow, 0.0)), 0.0)
        lower = jnp.where(strict, _dot_nt(kb, k) * decay, 0.0)
        ainv = eye - lower
        pw = lower
        p = 1
        while 2 * p < C:
            pw = _dot(pw, pw)
            ainv = ainv + _dot(ainv, pw)
            p *= 2
        sol = _dot(ainv, jnp.concatenate([vb, kb * jnp.exp(gcol)], axis=1))
        u = sol[:, :DN_HEAD_DIM]
        w = sol[:, DN_HEAD_DIM:]
        qk = _dot_nt(q, k) * decay
        k_tail = k * jnp.exp(glast - gcol)

        S = s_scr[h]
        v_new = u - _dot(w, S)
        o = _dot(q * jnp.exp(gcol), S) + _dot(qk, v_new)
        s_scr[h] = S * jnp.exp(glast) + lax.dot_general(
            k_tail.astype(BF16), v_new.astype(BF16), TN_DIMS, preferred_element_type=F32)

        zf = z_ref[0, :, hs]
        o = o * lax.rsqrt(jnp.mean(o * o, axis=-1, keepdims=True) + EPS) * dng_ref[...] * _silu(zf)
        odn_scr[:, hs] = o

    pp_scr[POOL_PAD:POOL_PAD + C, :] = pin_ref[0]
    pos = start + n * C + lax.broadcasted_iota(I32, (C, 1), 0)
    for gi, win in enumerate(POOL_WINDOWS):
        gs = slice(gi * POOL_GROUP_DIM, (gi + 1) * POOL_GROUP_DIM)
        xg = pp_scr[POOL_PAD:POOL_PAD + C, gs]
        ssum = xg
        for sft in range(1, win):
            ssum = ssum + pp_scr[POOL_PAD - sft:POOL_PAD - sft + C, gs]
        cnt = jnp.minimum(pos + 1, win).astype(F32)
        pooled = ssum / cnt - xg
        os_ = slice(gi * POOL_OUT_GROUP, (gi + 1) * POOL_OUT_GROUP)
        yp = _dot(pooled, wpool_ref[gi]) * pscale_ref[:, os_]
        mixed_ref[0, :, os_] = (jax.nn.sigmoid(ga_ref[0, :, os_]) * odn_scr[:, os_]
                                + jax.nn.sigmoid(gb_ref[0, :, os_]) * yp)

    @pl.when(n == last)
    def _store_state():
        nconv_ref[0] = xp_scr[Lv + CONV_PAD - (CONV_WIDTH - 1):Lv + CONV_PAD, :]
        npool_ref[0] = pp_scr[Lv + POOL_PAD - POOL_BUF:Lv + POOL_PAD, :]
        ns_ref[0] = s_scr[...]

    xp_scr[0:CONV_PAD, :] = xp_scr[C:C + CONV_PAD, :]
    pp_scr[0:POOL_PAD, :] = pp_scr[C:C + POOL_PAD, :]


def _mixer(proj, conv_buf, s0, pool_buf, start, seq_len, C,
           conv_w, a_log, dt_bias, dn_norm_g, w_pool, pool_scale):
    b, lp, _ = proj["qkv"].shape
    nchunks = lp // C
    lv = seq_len - (nchunks - 1) * C
    cbuf = jnp.pad(conv_buf, ((0, 0), (CONV_PAD - (CONV_WIDTH - 1), 0), (0, 0)))
    pbuf = jnp.pad(pool_buf, ((0, 0), (POOL_PAD - POOL_BUF, 0), (0, 0)))
    lane_pad = lambda a: jnp.pad(a.reshape(1, -1), ((0, 0), (DN_HEADS, LANES - 2 * DN_HEADS)))
    chunk = lambda w: pl.BlockSpec((1, C, w), lambda i, j: (i, j, 0))
    state = lambda *s: pl.BlockSpec((1,) + s, lambda i, j: (i,) + (0,) * len(s))
    return pl.pallas_call(
        functools.partial(_mixer_body, C, lv, start),
        grid=(b, nchunks),
        in_specs=[chunk(QKV_WIDTH), chunk(LANES), chunk(DN_WIDTH), chunk(POOL_WIDTH),
                  chunk(D_MODEL), chunk(D_MODEL),
                  state(CONV_PAD, QKV_WIDTH), state(DN_HEADS, DN_HEAD_DIM, DN_HEAD_DIM),
                  state(POOL_PAD, POOL_WIDTH),
                  _const_spec((CONV_WIDTH, QKV_WIDTH)), _const_spec((1, LANES)), _const_spec((1, LANES)),
                  _const_spec((1, DN_HEAD_DIM)),
                  _const_spec((len(POOL_WINDOWS), POOL_GROUP_DIM, POOL_OUT_GROUP)),
                  _const_spec((1, D_MODEL))],
        out_specs=[chunk(D_MODEL), state(CONV_WIDTH - 1, QKV_WIDTH),
                   state(DN_HEADS, DN_HEAD_DIM, DN_HEAD_DIM), state(POOL_BUF, POOL_WIDTH)],
        out_shape=[jax.ShapeDtypeStruct((b, lp, D_MODEL), F32),
                   jax.ShapeDtypeStruct((b, CONV_WIDTH - 1, QKV_WIDTH), F32),
                   jax.ShapeDtypeStruct((b, DN_HEADS, DN_HEAD_DIM, DN_HEAD_DIM), F32),
                   jax.ShapeDtypeStruct((b, POOL_BUF, POOL_WIDTH), F32)],
        scratch_shapes=[pltpu.VMEM((CONV_PAD + C + CONV_PAD, QKV_WIDTH), F32),
                        pltpu.VMEM((C, QKV_WIDTH), F32),
                        pltpu.VMEM((DN_HEADS, DN_HEAD_DIM, DN_HEAD_DIM), F32),
                        pltpu.VMEM((POOL_PAD + C + POOL_PAD, POOL_WIDTH), F32),
                        pltpu.VMEM((C, DN_WIDTH), F32)],
        compiler_params=pltpu.CompilerParams(dimension_semantics=("arbitrary", "arbitrary"),
                                             vmem_limit_bytes=VMEM_LIMIT),
        name="mixer",
    )(proj["qkv"], proj["ba"], proj["z"], proj["pool"], proj["ga"], proj["gb"], cbuf, s0, pbuf,
      conv_w, lane_pad(a_log), lane_pad(dt_bias), dn_norm_g.reshape(1, -1), w_pool,
      pool_scale.reshape(1, -1))


def _top16(s, ids, payload=None):
    big = float(2 ** 24)
    vals, sel, pays = [], [], []
    for _ in range(PEER_TOPK):
        m = jnp.max(s, axis=0, keepdims=True)
        am = jnp.min(jnp.where(s == m, ids, big), axis=0, keepdims=True)
        hit = ids == am
        if payload is not None:
            pays.append(jnp.max(jnp.where(hit, payload, -1.0), axis=0, keepdims=True))
        s = jnp.where(hit, -jnp.inf, s)
        vals.append(m)
        sel.append(am)
    out = (jnp.concatenate(vals, axis=0), jnp.concatenate(sel, axis=0))
    if payload is not None:
        out += (jnp.concatenate(pays, axis=0),)
    return out


_CAND_EDGE = 4


def _post_body(has_prev, mixed_ref, x_ref, g1_ref, sc2_ref, sh2_ref, n2g_ref, wout_ref, wq_ref,
               keys_ref, *refs):
    if has_prev:
        pre_ref, pgate_ref, x1_ref, h2_ref, idx_ref, gate_ref, coef_ref = refs
        coef_ref[...] = pgate_ref[...] * _gelu(pre_ref[...])
    else:
        x1_ref, h2_ref, idx_ref, gate_ref = refs
    tm = x_ref.shape[0]
    x1 = x_ref[...] + _mod_rows(g1_ref) * _dot(mixed_ref[...], wout_ref[...])
    x1_ref[...] = x1
    y = x1 * lax.rsqrt(jnp.mean(x1 * x1, axis=-1, keepdims=True) + EPS) * n2g_ref[...]
    h2 = y * (1.0 + _mod_rows(sc2_ref)) + _mod_rows(sh2_ref)
    h2_ref[...] = h2
    q = _dot(h2, wq_ref[...])

    K = PEER_TOPK
    key_id = lax.broadcasted_iota(I32, (PEER_NKEYS, 1), 0).astype(F32)
    r16 = lax.broadcasted_iota(I32, (K, 1), 0)
    cand_id = jnp.concatenate([(a * K + r16) for a in range(_CAND_EDGE)]
                              + [(r16 * K + b) for b in range(_CAND_EDGE)], axis=0).astype(F32)
    dup = r16 < _CAND_EDGE
    idx_rows, gate_rows = [], []
    for h in range(PEER_HEADS):
        half = []
        for p in range(2):
            c0 = (h * 2 + p) * PEER_KEY_HALF
            st = _dot_nt(keys_ref[h * 2 + p], q[:, c0:c0 + PEER_KEY_HALF])
            half.append(_top16(st, key_id))
        (s1, i1), (s2, i2) = half
        cand = jnp.concatenate(
            [s1[a:a + 1] + s2 for a in range(_CAND_EDGE)]
            + [jnp.where(dup, -jnp.inf, s1 + s2[b:b + 1]) for b in range(_CAND_EDGE)], axis=0)
        cidx = jnp.concatenate(
            [i1[a:a + 1] * PEER_NKEYS + i2 for a in range(_CAND_EDGE)]
            + [i1 * PEER_NKEYS + i2[b:b + 1] for b in range(_CAND_EDGE)], axis=0)
        best, _, eidx = _top16(cand, cand_id, cidx)
        e = jnp.exp(best - best[0:1])
        gate_rows.append(e / jnp.sum(e, axis=0, keepdims=True))
        idx_rows.append(eidx)
    idx_ref[...] = jnp.concatenate(idx_rows, axis=0).T.astype(I32)
    gate_ref[...] = jnp.concatenate(gate_rows, axis=0).T


def _post(mixed2d, x2d, mod, rows_per_batch, norm2_g, w_out, w_query, keys, tm, prev=None):
    t = x2d.shape[0]
    steps = t // tm
    row = lambda w: pl.BlockSpec((tm, w), lambda i: (i, 0))
    in_specs = [row(D_MODEL), row(D_MODEL),
                _mod_spec(2, rows_per_batch, tm), _mod_spec(4, rows_per_batch, tm),
                _mod_spec(3, rows_per_batch, tm), _const_spec((1, D_MODEL)),
                _const_spec((D_MODEL, D_MODEL)), _const_spec((D_MODEL, 2 * PEER_HEADS * PEER_KEY_HALF)),
                _const_spec((2 * PEER_HEADS, PEER_NKEYS, PEER_KEY_HALF))]
    out_specs = [row(D_MODEL), row(D_MODEL), row(PEER_HK), row(PEER_HK)]
    out_shape = [jax.ShapeDtypeStruct((t, D_MODEL), F32), jax.ShapeDtypeStruct((t, D_MODEL), F32),
                 jax.ShapeDtypeStruct((t, PEER_HK), I32), jax.ShapeDtypeStruct((t, PEER_HK), F32)]
    args = [mixed2d, x2d, mod, mod, mod, norm2_g.reshape(1, -1), w_out, w_query, keys]
    if prev is not None:
        tp = prev[0].shape[0]
        prow = pl.BlockSpec((tp // steps, PEER_HK), lambda i: (i, 0))
        in_specs += [prow, prow]
        out_specs += [prow]
        out_shape += [jax.ShapeDtypeStruct((tp, PEER_HK), F32)]
        args += list(prev)
    return pl.pallas_call(
        functools.partial(_post_body, prev is not None),
        grid=(steps,),
        in_specs=in_specs, out_specs=out_specs, out_shape=out_shape,
        compiler_params=pltpu.CompilerParams(vmem_limit_bytes=VMEM_LIMIT),
        name="post",
    )(*args)


SC_CORES = 2
SC_SUBCORES = 16
SC_LANES = 16
SC_WORKERS = SC_CORES * SC_SUBCORES
SC_TOKENS = 16
SC_SLOTS = 4
SC_CHUNKS = D_MODEL // SC_LANES
PROMPT_PARTS = 8
ROW_TILE = 256


def _sc_mesh():
    return plsc.VectorSubcoreMesh(core_axis_name="c", subcore_axis_name="s")


def _sc_worker():
    return lax.axis_index("s") * SC_CORES + lax.axis_index("c")


def _sc_jobs(table_hbm, idx_v, buf, sem, compute):
    njobs = SC_TOKENS * PEER_HEADS

    def copy(j, slot):
        tt = j // PEER_HEADS
        h = j % PEER_HEADS
        rows = idx_v[tt, pl.ds(h * PEER_TOPK, PEER_TOPK)]
        return pltpu.make_async_copy(table_hbm.at[rows], buf.at[slot], sem.at[slot])

    for s in range(SC_SLOTS):
        copy(s, s).start()

    def group(g, c):
        for s in range(SC_SLOTS):
            j = g * SC_SLOTS + s
            copy(j, s).wait()
            compute(j // PEER_HEADS, j % PEER_HEADS, s)

            @pl.when(j + SC_SLOTS < njobs)
            def _next():
                copy(j + SC_SLOTS, s).start()
        return c

    lax.fori_loop(0, njobs // SC_SLOTS, group, 0)


def _peer_u_body(n_tok, idx_hbm, h2_hbm, u_hbm, pre_hbm, idx_v, h2_v, pre_v, ubuf, acc_v, sem):
    base = _sc_worker() * n_tok
    lane = lax.iota(I32, SC_LANES)

    def compute(tt, h, slot):
        def chunk(c, accs):
            xv = h2_v[tt, pl.ds(c * SC_LANES, SC_LANES)]
            return tuple(a + ubuf[slot, k, pl.ds(c * SC_LANES, SC_LANES)] * xv
                         for k, a in enumerate(accs))
        zero = jnp.zeros((SC_LANES,), F32)
        accs = lax.fori_loop(0, SC_CHUNKS, chunk, (zero,) * PEER_TOPK)
        for k, a in enumerate(accs):
            acc_v[k, :] = a
        tot = zero
        for j in range(SC_LANES):
            tot = tot + plsc.load_gather(acc_v, [lane, jnp.full((SC_LANES,), j, I32)])
        pre_v[tt, pl.ds(h * PEER_TOPK, PEER_TOPK)] = tot

    def block(bi, c):
        t0 = base + bi * SC_TOKENS
        pltpu.sync_copy(idx_hbm.at[pl.ds(t0, SC_TOKENS)], idx_v)
        pltpu.sync_copy(h2_hbm.at[pl.ds(t0, SC_TOKENS)], h2_v)
        _sc_jobs(u_hbm, idx_v, ubuf, sem, compute)
        pltpu.sync_copy(pre_v, pre_hbm.at[pl.ds(t0, SC_TOKENS)])
        return c

    lax.fori_loop(0, n_tok // SC_TOKENS, block, 0)


def _peer_v_body(n_tok, idx_hbm, coef_hbm, v_hbm, out_hbm, idx_v, coef_v, out_v, vbuf, sem):
    base = _sc_worker() * n_tok
    zero = jnp.zeros((SC_LANES,), F32)

    def compute(tt, h, slot):
        row = jnp.full((SC_LANES,), tt, I32)
        cb = [plsc.load_gather(coef_v, [row, jnp.full((SC_LANES,), h * PEER_TOPK + k, I32)])
              for k in range(PEER_TOPK)]

        @plsc.parallel_loop(0, SC_CHUNKS, unroll=2)
        def _chunk(c):
            cs = pl.ds(c * SC_LANES, SC_LANES)
            terms = [cb[k] * vbuf[slot, k, cs] for k in range(PEER_TOPK)]
            while len(terms) > 1:
                terms = [a + b for a, b in zip(terms[0::2], terms[1::2])]
            plsc.addupdate(out_v.at[tt, cs], terms[0])

    def block(bi, c):
        t0 = base + bi * SC_TOKENS
        pltpu.sync_copy(idx_hbm.at[pl.ds(t0, SC_TOKENS)], idx_v)
        pltpu.sync_copy(coef_hbm.at[pl.ds(t0, SC_TOKENS)], coef_v)

        def clear(i, cc):
            out_v[i // SC_CHUNKS, pl.ds((i % SC_CHUNKS) * SC_LANES, SC_LANES)] = zero
            return cc
        lax.fori_loop(0, SC_TOKENS * SC_CHUNKS, clear, 0)
        _sc_jobs(v_hbm, idx_v, vbuf, sem, compute)
        pltpu.sync_copy(out_v, out_hbm.at[pl.ds(t0, SC_TOKENS)])
        return c

    lax.fori_loop(0, n_tok // SC_TOKENS, block, 0)


def _peer_sc(body, idx, rows, table, out_width, name):
    t = idx.shape[0]
    assert t % (SC_WORKERS * SC_TOKENS) == 0
    n_tok = t // SC_WORKERS
    return pl.kernel(
        functools.partial(body, n_tok),
        out_type=jax.ShapeDtypeStruct((t, out_width), F32),
        mesh=_sc_mesh(),
        scratch_types=[pltpu.VMEM((SC_TOKENS, PEER_HK), I32),
                       pltpu.VMEM((SC_TOKENS, rows.shape[1]), F32),
                       pltpu.VMEM((SC_TOKENS, out_width), F32),
                       pltpu.VMEM((SC_SLOTS, PEER_TOPK, D_MODEL), F32)]
                      + ([pltpu.VMEM((PEER_TOPK, SC_LANES), F32)] if body is _peer_u_body else [])
                      + [pltpu.SemaphoreType.DMA((SC_SLOTS,))],
        compiler_params=pltpu.CompilerParams(needs_layout_passes=False),
        name=name,
    )(idx, rows, table)


def _coef_body(pre_ref, gate_ref, coef_ref):
    coef_ref[...] = gate_ref[...] * _gelu(pre_ref[...])


def _coef(pre, gates, tm):
    t = pre.shape[0]
    row = pl.BlockSpec((tm, PEER_HK), lambda i: (i, 0))
    return pl.pallas_call(_coef_body, grid=(t // tm,), in_specs=[row, row], out_specs=row,
                          out_shape=jax.ShapeDtypeStruct((t, PEER_HK), F32), name="coef")(pre, gates)


def _final_body(x1_ref, peer_ref, g2_ref, fng_ref, y_ref):
    x2 = x1_ref[...] + _mod_rows(g2_ref) * peer_ref[...]
    y_ref[...] = x2 * lax.rsqrt(jnp.mean(x2 * x2, axis=-1, keepdims=True) + EPS) * fng_ref[...]


def _final(x1, peer_out, mod, rows_per_batch, final_g, tm):
    t = x1.shape[0]
    row = pl.BlockSpec((tm, D_MODEL), lambda i: (i, 0))
    return pl.pallas_call(
        _final_body, grid=(t // tm,),
        in_specs=[row, row, _mod_spec(5, rows_per_batch, tm), _const_spec((1, D_MODEL))],
        out_specs=row, out_shape=jax.ShapeDtypeStruct((t, D_MODEL), F32), name="final",
    )(x1, peer_out, mod, final_g.reshape(1, -1))


def _expert_finish(g, coef, final_g, expert_v):
    peer_out = _peer_sc(_peer_v_body, g["idx"], coef, expert_v, D_MODEL, "peer_v")
    y = _final(g["x1"], peer_out, g["mod"], g["l"], final_g, g["tm"])
    return y.reshape(g["b"], g["l"], D_MODEL)


def _front(x, mod, conv_buf, s0, pool_buf, start, chunk, tm, wts, prev):
    b, l, _ = x.shape
    t = b * l
    x2d = x.reshape(t, D_MODEL)
    if l >= tm:
        modx = mod.reshape(b, 6, 1, D_MODEL).transpose(1, 0, 2, 3)
    else:
        modx = jnp.repeat(mod.reshape(b, 6, D_MODEL), l, axis=0).transpose(1, 0, 2)
    outs = _inproj(x2d, modx, l, wts["norm1_g"], wts["w_cat"], tm)
    lp = -(-l // chunk) * chunk
    proj = {}
    for (name, w), a in zip(_IN_BLOCKS, outs):
        a = a.reshape(b, l, w)
        proj[name] = a if lp == l else jnp.pad(a, ((0, 0), (0, lp - l), (0, 0)))
    mixed, nconv, ns, npool = _mixer(proj, conv_buf, s0, pool_buf, start, l, chunk,
                                     wts["conv_w"], wts["a_log"], wts["dt_bias"], wts["dn_norm_g"],
                                     wts["w_pool"], wts["pool_scale"])
    mixed2d = mixed[:, :l].reshape(t, D_MODEL)
    res = _post(mixed2d, x2d, modx, l, wts["norm2_g"], wts["w_out"], wts["w_query"], wts["keys"], tm,
                prev=None if prev is None else (prev["pre"], prev["gates"]))
    x1, h2, idx, gates = res[:4]
    pre = _peer_sc(_peer_u_body, idx, h2, wts["expert_u"], PEER_HK, "peer_u")
    g = dict(x1=x1, idx=idx, gates=gates, pre=pre, mod=modx, b=b, l=l, tm=tm,
             states=(nconv, ns, npool))
    return g, (res[4] if prev is not None else None)


def kernel(x_prompt, x_sample, c_prompt, c_sample, state_conv, state_delta, state_pool, w_ada, b_ada, norm1_g, w_in, conv_w, a_log, dt_bias, dn_norm_g, w_pool, pool_scale, w_out, norm2_g, w_query, sub_keys, expert_u, expert_v, final_norm_g):
    bp = x_prompt.shape[0]
    bs = x_sample.shape[0]
    yp, ys = x_prompt, x_sample
    conv_p, delta_p, pool_p, conv_s, delta_s, pool_s = [], [], [], [], [], []
    zero_conv = jnp.zeros((bp, CONV_WIDTH - 1, QKV_WIDTH), F32)
    zero_delta = jnp.zeros((bp, DN_HEADS, DN_HEAD_DIM, DN_HEAD_DIM), F32)
    zero_pool = jnp.zeros((bp, POOL_BUF, POOL_WIDTH), F32)
    c_all = jnp.concatenate([c_prompt, c_sample], axis=0)
    for layer in range(DEPTH):
        wi = w_in[layer]
        o_b = QKV_WIDTH
        o_z = o_b + 2 * DN_HEADS
        w_ba = jnp.pad(wi[:, o_b:o_z], ((0, 0), (0, LANES - 2 * DN_HEADS)))
        w_cat = jnp.concatenate([wi[:, :o_b], wi[:, o_z:], w_ba], axis=1).astype(BF16)
        last = layer == DEPTH - 1
        wts = dict(
            norm1_g=norm1_g[layer], w_cat=w_cat, conv_w=conv_w[layer], a_log=a_log[layer],
            dt_bias=dt_bias[layer], dn_norm_g=dn_norm_g[layer], w_pool=w_pool[layer],
            pool_scale=pool_scale[layer], w_out=w_out[layer].astype(BF16), norm2_g=norm2_g[layer],
            w_query=w_query[layer].astype(BF16),
            keys=sub_keys[layer].reshape(2 * PEER_HEADS, PEER_NKEYS, PEER_KEY_HALF).astype(BF16),
            expert_u=expert_u[layer], expert_v=expert_v[layer],
            final_norm_g=final_norm_g if last else jnp.ones_like(final_norm_g))
        mod = _ada(c_all, w_ada[layer], b_ada[layer])
        assert last, "final norm is fused into the expert stage"
        step = bp // PROMPT_PARTS
        jobs = [(yp[b0:b0 + step], mod[b0:b0 + step], zero_conv[b0:b0 + step], zero_delta[b0:b0 + step],
                 zero_pool[b0:b0 + step], 0, DN_CHUNK) for b0 in range(0, bp, step)]
        jobs.append((ys, mod[bp:], state_conv[layer], state_delta[layer], state_pool[layer],
                     PAST_LEN, SUBLANES))
        done, prev = [], None
        for xg, mg, cg, sg, pg, start, chunk in jobs:
            g, coef_prev = _front(xg, mg, cg, sg, pg, start, chunk, ROW_TILE, wts, prev)
            if prev is not None:
                done.append((_expert_finish(prev, coef_prev, wts["final_norm_g"], wts["expert_v"]),)
                            + prev["states"])
            prev = g
        coef_last = _coef(prev["pre"], prev["gates"], ROW_TILE)
        done.append((_expert_finish(prev, coef_last, wts["final_norm_g"], wts["expert_v"]),)
                    + prev["states"])
        yp, cp, sp, pp = (jnp.concatenate(a, axis=0) for a in zip(*done[:-1]))
        ys, cs, ss, ps = done[-1]
        conv_p.append(cp)
        delta_p.append(sp)
        pool_p.append(pp)
        conv_s.append(cs)
        delta_s.append(ss)
        pool_s.append(ps)
    return (yp, ys, jnp.stack(conv_p), jnp.stack(delta_p), jnp.stack(pool_p),
            jnp.stack(conv_s), jnp.stack(delta_s), jnp.stack(pool_s))
```

```python
import functools

import jax
import jax.numpy as jnp
from jax import lax
from jax.experimental import pallas as pl
from jax.experimental.pallas import tpu as pltpu
from jax.experimental.pallas import tpu_sc as plsc

F32 = jnp.float32
BF16 = jnp.bfloat16
I32 = jnp.int32

D_MODEL = 1024
DEPTH = 1
PAST_LEN = 16384
DN_HEADS = 8
DN_HEAD_DIM = 128
DN_WIDTH = DN_HEADS * DN_HEAD_DIM
QKV_WIDTH = 3 * DN_WIDTH
CONV_WIDTH = 4
DN_CHUNK = 64
POOL_WINDOWS = (2, 4, 8, 16)
POOL_GROUP_DIM = 128
POOL_WIDTH = len(POOL_WINDOWS) * POOL_GROUP_DIM
POOL_OUT_GROUP = D_MODEL // len(POOL_WINDOWS)
POOL_BUF = max(POOL_WINDOWS) - 1
PEER_HEADS = 8
PEER_NKEYS = 128
PEER_TOPK = 16
PEER_KEY_HALF = 128
PEER_HK = PEER_HEADS * PEER_TOPK
EPS = 1e-6

LANES = 128
SUBLANES = 8
CONV_PAD = SUBLANES
POOL_PAD = 16
VMEM_LIMIT = 56 * 1024 * 1024

NT_DIMS = (((1,), (1,)), ((), ()))
TN_DIMS = (((0,), (0,)), ((), ()))


def _dot(a, b):
    return jnp.dot(a.astype(BF16), b.astype(BF16), preferred_element_type=F32)


def _dot_nt(a, b):
    return lax.dot_general(a.astype(BF16), b.astype(BF16), NT_DIMS, preferred_element_type=F32)


def _split3(x):
    hi = x.astype(BF16)
    r1 = x - hi.astype(F32)
    mid = r1.astype(BF16)
    lo = (r1 - mid.astype(F32)).astype(BF16)
    return hi, mid, lo


def _silu(x):
    return x * jax.nn.sigmoid(x)


def _gelu(x):
    return 0.5 * x * (1.0 + lax.erf(x * (0.5 ** 0.5)))


def _softplus(x):
    return jnp.maximum(x, 0.0) + jnp.log(1.0 + jnp.exp(-jnp.abs(x)))


def _mod_rows(ref):
    m = ref[...]
    return m.reshape(m.shape[-2], m.shape[-1])


def _mod_spec(k, rows_per_batch, tm):
    if rows_per_batch >= tm:
        tiles = rows_per_batch // tm
        return pl.BlockSpec((1, 1, 1, D_MODEL), lambda i, *_: (k, i // tiles, 0, 0))
    return pl.BlockSpec((1, tm, D_MODEL), lambda i, *_: (k, i, 0))


def _const_spec(shape):
    nd = len(shape)
    return pl.BlockSpec(shape, lambda *_: (0,) * nd)


def _ada_body(c_ref, w_ref, b_ref, o_ref):
    o_ref[...] = _dot(_silu(c_ref[...]), w_ref[...]) + b_ref[...]


def _ada(c, w_ada, b_ada):
    n = c.shape[0]
    return pl.pallas_call(
        _ada_body,
        grid=(6,),
        in_specs=[pl.BlockSpec((n, D_MODEL), lambda j: (0, 0)),
                  pl.BlockSpec((D_MODEL, D_MODEL), lambda j: (0, j)),
                  pl.BlockSpec((1, D_MODEL), lambda j: (0, j))],
        out_specs=pl.BlockSpec((n, D_MODEL), lambda j: (0, j)),
        out_shape=jax.ShapeDtypeStruct((n, 6 * D_MODEL), F32),
        name="ada",
    )(c, w_ada, b_ada.reshape(1, -1))


_IN_BLOCKS = (("qkv", QKV_WIDTH), ("z", DN_WIDTH), ("pool", POOL_WIDTH),
              ("ga", D_MODEL), ("gb", D_MODEL), ("ba", LANES))
_IN_TOTAL = sum(w for _, w in _IN_BLOCKS)
_IN_COL_CHUNK = 512


def _inproj_body(x_ref, sc_ref, sh_ref, g_ref, w_ref, *out_refs):
    x = x_ref[...]
    y = x * lax.rsqrt(jnp.mean(x * x, axis=-1, keepdims=True) + EPS) * g_ref[...]
    h = (y * (1.0 + _mod_rows(sc_ref)) + _mod_rows(sh_ref)).astype(BF16)
    off = 0
    for (_, width), o_ref in zip(_IN_BLOCKS, out_refs):
        for c0 in range(0, width, _IN_COL_CHUNK):
            cw = min(_IN_COL_CHUNK, width - c0)
            o_ref[:, c0:c0 + cw] = jnp.dot(h, w_ref[:, off + c0:off + c0 + cw],
                                           preferred_element_type=F32)
        off += width


def _inproj(x2d, mod, rows_per_batch, norm_g, w_cat, tm):
    t = x2d.shape[0]
    row = lambda w: pl.BlockSpec((tm, w), lambda i: (i, 0))
    return pl.pallas_call(
        _inproj_body,
        grid=(t // tm,),
        in_specs=[row(D_MODEL), _mod_spec(1, rows_per_batch, tm), _mod_spec(0, rows_per_batch, tm),
                  _const_spec((1, D_MODEL)),
                  pl.BlockSpec((D_MODEL, _IN_TOTAL), lambda i: (0, 0), pipeline_mode=pl.Buffered(1))],
        out_specs=[row(w) for _, w in _IN_BLOCKS],
        out_shape=[jax.ShapeDtypeStruct((t, w), F32) for _, w in _IN_BLOCKS],
        compiler_params=pltpu.CompilerParams(vmem_limit_bytes=VMEM_LIMIT),
        name="inproj",
    )(x2d, mod, mod, norm_g.reshape(1, -1), w_cat)


def _mixer_body(C, Lv, start,
                qkv_ref, ba_ref, z_ref, pin_ref, ga_ref, gb_ref, cbuf_ref, s0_ref, pbuf_ref,
                convw_ref, alog_ref, dtb_ref, dng_ref, wpool_ref, pscale_ref,
                mixed_ref, nconv_ref, ns_ref, npool_ref,
                xp_scr, act_scr, s_scr, pp_scr, odn_scr):
    n = pl.program_id(1)
    last = pl.num_programs(1) - 1

    @pl.when(n == 0)
    def _load_state():
        xp_scr[0:CONV_PAD, :] = cbuf_ref[0]
        pp_scr[0:POOL_PAD, :] = pbuf_ref[0]
        s_scr[...] = s0_ref[0]

    xp_scr[CONV_PAD:CONV_PAD + C, :] = qkv_ref[0]
    for c0 in range(0, QKV_WIDTH, 512):
        cs = slice(c0, c0 + 512)
        y = xp_scr[CONV_PAD:CONV_PAD + C, cs] * convw_ref[CONV_WIDTH - 1:CONV_WIDTH, cs]
        for k in range(CONV_WIDTH - 1):
            r0 = CONV_PAD - (CONV_WIDTH - 1) + k
            y = y + xp_scr[r0:r0 + C, cs] * convw_ref[k:k + 1, cs]
        act_scr[:, cs] = _silu(y)

    ba = ba_ref[0]
    lane = lax.broadcasted_iota(I32, (C, LANES), 1)
    beta_all = jax.nn.sigmoid(ba)
    g_all = -jnp.exp(alog_ref[...]) * _softplus(ba + dtb_ref[...])
    if Lv < C:
        valid = lax.broadcasted_iota(I32, (C, LANES), 0) < Lv
        beta_all = jnp.where(valid, beta_all, 0.0)
        g_all = jnp.where(valid, g_all, 0.0)
    ii = lax.broadcasted_iota(I32, (C, C), 0)
    jj = lax.broadcasted_iota(I32, (C, C), 1)
    causal = ii >= jj
    strict = ii > jj
    tril = jnp.where(causal, 1.0, 0.0).astype(BF16)
    eye = jnp.where(ii == jj, 1.0, 0.0)
    gc_all = sum(jnp.dot(tril, part, preferred_element_type=F32) for part in _split3(g_all))
    if C < LANES:
        gc_sq = jnp.concatenate([gc_all, jnp.zeros((LANES - C, LANES), F32)], axis=0)
    else:
        gc_sq = gc_all
    gc_t = gc_sq.T

    for h in range(DN_HEADS):
        hs = slice(h * DN_HEAD_DIM, (h + 1) * DN_HEAD_DIM)
        beta = jnp.sum(jnp.where(lane == h, beta_all, 0.0), axis=1, keepdims=True)
        gcol = jnp.sum(jnp.where(lane == DN_HEADS + h, gc_all, 0.0), axis=1, keepdims=True)
        grow = gc_t[DN_HEADS + h:DN_HEADS + h + 1, 0:C]
        glast = gcol[C - 1:C, :]

        q = act_scr[:, hs]
        k = act_scr[:, DN_WIDTH + h * DN_HEAD_DIM:DN_WIDTH + (h + 1) * DN_HEAD_DIM]
        v = act_scr[:, 2 * DN_WIDTH + h * DN_HEAD_DIM:2 * DN_WIDTH + (h + 1) * DN_HEAD_DIM]
        q = q * lax.rsqrt(jnp.sum(q * q, axis=-1, keepdims=True) + EPS) * (DN_HEAD_DIM ** -0.5)
        k = k * lax.rsqrt(jnp.sum(k * k, axis=-1, keepdims=True) + EPS)
        kb = k * beta
        vb = v * beta

        decay = jnp.where(causal, jnp.exp(jnp.where(causal, gcol - grow, 0.0)), 0.0)
        lower = jnp.where(strict, _dot_nt(kb, k) * decay, 0.0)
        ainv = eye - lower
        pw = lower
        p = 1
        while 2 * p < C:
            pw = _dot(pw, pw)
            ainv = ainv + _dot(ainv, pw)
            p *= 2
        sol = _dot(ainv, jnp.concatenate([vb, kb * jnp.exp(gcol)], axis=1))
        u = sol[:, :DN_HEAD_DIM]
        w = sol[:, DN_HEAD_DIM:]
        qk = _dot_nt(q, k) * decay
        k_tail = k * jnp.exp(glast - gcol)

        S = s_scr[h]
        v_new = u - _dot(w, S)
        o = _dot(q * jnp.exp(gcol), S) + _dot(qk, v_new)
        s_scr[h] = S * jnp.exp(glast) + lax.dot_general(
            k_tail.astype(BF16), v_new.astype(BF16), TN_DIMS, preferred_element_type=F32)

        zf = z_ref[0, :, hs]
        o = o * lax.rsqrt(jnp.mean(o * o, axis=-1, keepdims=True) + EPS) * dng_ref[...] * _silu(zf)
        odn_scr[:, hs] = o

    pp_scr[POOL_PAD:POOL_PAD + C, :] = pin_ref[0]
    pos = start + n * C + lax.broadcasted_iota(I32, (C, 1), 0)
    for gi, win in enumerate(POOL_WINDOWS):
        gs = slice(gi * POOL_GROUP_DIM, (gi + 1) * POOL_GROUP_DIM)
        xg = pp_scr[POOL_PAD:POOL_PAD + C, gs]
        ssum = xg
        for sft in range(1, win):
            ssum = ssum + pp_scr[POOL_PAD - sft:POOL_PAD - sft + C, gs]
        cnt = jnp.minimum(pos + 1, win).astype(F32)
        pooled = ssum / cnt - xg
        os_ = slice(gi * POOL_OUT_GROUP, (gi + 1) * POOL_OUT_GROUP)
        yp = _dot(pooled, wpool_ref[gi]) * pscale_ref[:, os_]
        mixed_ref[0, :, os_] = (jax.nn.sigmoid(ga_ref[0, :, os_]) * odn_scr[:, os_]
                                + jax.nn.sigmoid(gb_ref[0, :, os_]) * yp)

    @pl.when(n == last)
    def _store_state():
        nconv_ref[0] = xp_scr[Lv + CONV_PAD - (CONV_WIDTH - 1):Lv + CONV_PAD, :]
        npool_ref[0] = pp_scr[Lv + POOL_PAD - POOL_BUF:Lv + POOL_PAD, :]
        ns_ref[0] = s_scr[...]

    xp_scr[0:CONV_PAD, :] = xp_scr[C:C + CONV_PAD, :]
    pp_scr[0:POOL_PAD, :] = pp_scr[C:C + POOL_PAD, :]


def _mixer(proj, conv_buf, s0, pool_buf, start, seq_len, C,
           conv_w, a_log, dt_bias, dn_norm_g, w_pool, pool_scale):
    b, lp, _ = proj["qkv"].shape
    nchunks = lp // C
    lv = seq_len - (nchunks - 1) * C
    cbuf = jnp.pad(conv_buf, ((0, 0), (CONV_PAD - (CONV_WIDTH - 1), 0), (0, 0)))
    pbuf = jnp.pad(pool_buf, ((0, 0), (POOL_PAD - POOL_BUF, 0), (0, 0)))
    lane_pad = lambda a: jnp.pad(a.reshape(1, -1), ((0, 0), (DN_HEADS, LANES - 2 * DN_HEADS)))
    chunk = lambda w: pl.BlockSpec((1, C, w), lambda i, j: (i, j, 0))
    state = lambda *s: pl.BlockSpec((1,) + s, lambda i, j: (i,) + (0,) * len(s))
    return pl.pallas_call(
        functools.partial(_mixer_body, C, lv, start),
        grid=(b, nchunks),
        in_specs=[chunk(QKV_WIDTH), chunk(LANES), chunk(DN_WIDTH), chunk(POOL_WIDTH),
                  chunk(D_MODEL), chunk(D_MODEL),
                  state(CONV_PAD, QKV_WIDTH), state(DN_HEADS, DN_HEAD_DIM, DN_HEAD_DIM),
                  state(POOL_PAD, POOL_WIDTH),
                  _const_spec((CONV_WIDTH, QKV_WIDTH)), _const_spec((1, LANES)), _const_spec((1, LANES)),
                  _const_spec((1, DN_HEAD_DIM)),
                  _const_spec((len(POOL_WINDOWS), POOL_GROUP_DIM, POOL_OUT_GROUP)),
                  _const_spec((1, D_MODEL))],
        out_specs=[chunk(D_MODEL), state(CONV_WIDTH - 1, QKV_WIDTH),
                   state(DN_HEADS, DN_HEAD_DIM, DN_HEAD_DIM), state(POOL_BUF, POOL_WIDTH)],
        out_shape=[jax.ShapeDtypeStruct((b, lp, D_MODEL), F32),
                   jax.ShapeDtypeStruct((b, CONV_WIDTH - 1, QKV_WIDTH), F32),
                   jax.ShapeDtypeStruct((b, DN_HEADS, DN_HEAD_DIM, DN_HEAD_DIM), F32),
                   jax.ShapeDtypeStruct((b, POOL_BUF, POOL_WIDTH), F32)],
        scratch_shapes=[pltpu.VMEM((CONV_PAD + C + CONV_PAD, QKV_WIDTH), F32),
                        pltpu.VMEM((C, QKV_WIDTH), F32),
                        pltpu.VMEM((DN_HEADS, DN_HEAD_DIM, DN_HEAD_DIM), F32),
                        pltpu.VMEM((POOL_PAD + C + POOL_PAD, POOL_WIDTH), F32),
                        pltpu.VMEM((C, DN_WIDTH), F32)],
        compiler_params=pltpu.CompilerParams(dimension_semantics=("arbitrary", "arbitrary"),
                                             vmem_limit_bytes=VMEM_LIMIT),
        name="mixer",
    )(proj["qkv"], proj["ba"], proj["z"], proj["pool"], proj["ga"], proj["gb"], cbuf, s0, pbuf,
      conv_w, lane_pad(a_log), lane_pad(dt_bias), dn_norm_g.reshape(1, -1), w_pool,
      pool_scale.reshape(1, -1))


def _top16(s, ids, payload=None):
    big = float(2 ** 24)
    vals, sel, pays = [], [], []
    for _ in range(PEER_TOPK):
        m = jnp.max(s, axis=0, keepdims=True)
        am = jnp.min(jnp.where(s == m, ids, big), axis=0, keepdims=True)
        hit = ids == am
        if payload is not None:
            pays.append(jnp.max(jnp.where(hit, payload, -1.0), axis=0, keepdims=True))
        s = jnp.where(hit, -jnp.inf, s)
        vals.append(m)
        sel.append(am)
    out = (jnp.concatenate(vals, axis=0), jnp.concatenate(sel, axis=0))
    if payload is not None:
        out += (jnp.concatenate(pays, axis=0),)
    return out


_CAND_EDGE = 4


def _post_body(has_prev, has_fin, mixed_ref, x_ref, g1_ref, sc2_ref, sh2_ref, n2g_ref, wout_ref,
               wq_ref, keys_ref, *refs):
    refs = list(refs)
    prev_in = [refs.pop(0) for _ in range(2 if has_prev else 0)]
    fin_in = [refs.pop(0) for _ in range(4 if has_fin else 0)]
    x1_ref, h2_ref, idx_ref, gate_ref = refs[:4]
    extra_out = refs[4:]
    if has_prev:
        pre_ref, pgate_ref = prev_in
        extra_out.pop(0)[...] = pgate_ref[...] * _gelu(pre_ref[...])
    if has_fin:
        _final_body(*fin_in, extra_out.pop(0))
    tm = x_ref.shape[0]
    x1 = x_ref[...] + _mod_rows(g1_ref) * _dot(mixed_ref[...], wout_ref[...])
    x1_ref[...] = x1
    y = x1 * lax.rsqrt(jnp.mean(x1 * x1, axis=-1, keepdims=True) + EPS) * n2g_ref[...]
    h2 = y * (1.0 + _mod_rows(sc2_ref)) + _mod_rows(sh2_ref)
    h2_ref[...] = h2
    q = _dot(h2, wq_ref[...])

    K = PEER_TOPK
    key_id = lax.broadcasted_iota(I32, (PEER_NKEYS, 1), 0).astype(F32)
    r16 = lax.broadcasted_iota(I32, (K, 1), 0)
    cand_id = jnp.concatenate([(a * K + r16) for a in range(_CAND_EDGE)]
                              + [(r16 * K + b) for b in range(_CAND_EDGE)], axis=0).astype(F32)
    dup = r16 < _CAND_EDGE
    idx_rows, gate_rows = [], []
    for h in range(PEER_HEADS):
        half = []
        for p in range(2):
            c0 = (h * 2 + p) * PEER_KEY_HALF
            st = _dot_nt(keys_ref[h * 2 + p], q[:, c0:c0 + PEER_KEY_HALF])
            half.append(_top16(st, key_id))
        (s1, i1), (s2, i2) = half
        cand = jnp.concatenate(
            [s1[a:a + 1] + s2 for a in range(_CAND_EDGE)]
            + [jnp.where(dup, -jnp.inf, s1 + s2[b:b + 1]) for b in range(_CAND_EDGE)], axis=0)
        cidx = jnp.concatenate(
            [i1[a:a + 1] * PEER_NKEYS + i2 for a in range(_CAND_EDGE)]
            + [i1 * PEER_NKEYS + i2[b:b + 1] for b in range(_CAND_EDGE)], axis=0)
        best, _, eidx = _top16(cand, cand_id, cidx)
        e = jnp.exp(best - best[0:1])
        gate_rows.append(e / jnp.sum(e, axis=0, keepdims=True))
        idx_rows.append(eidx)
    idx_ref[...] = jnp.concatenate(idx_rows, axis=0).T.astype(I32)
    gate_ref[...] = jnp.concatenate(gate_rows, axis=0).T


def _post(mixed2d, x2d, mod, rows_per_batch, norm2_g, w_out, w_query, keys, tm, prev=None, fin=None):
    t = x2d.shape[0]
    steps = t // tm
    row = lambda w: pl.BlockSpec((tm, w), lambda i: (i, 0))
    in_specs = [row(D_MODEL), row(D_MODEL),
                _mod_spec(2, rows_per_batch, tm), _mod_spec(4, rows_per_batch, tm),
                _mod_spec(3, rows_per_batch, tm), _const_spec((1, D_MODEL)),
                _const_spec((D_MODEL, D_MODEL)), _const_spec((D_MODEL, 2 * PEER_HEADS * PEER_KEY_HALF)),
                _const_spec((2 * PEER_HEADS, PEER_NKEYS, PEER_KEY_HALF))]
    out_specs = [row(D_MODEL), row(D_MODEL), row(PEER_HK), row(PEER_HK)]
    out_shape = [jax.ShapeDtypeStruct((t, D_MODEL), F32), jax.ShapeDtypeStruct((t, D_MODEL), F32),
                 jax.ShapeDtypeStruct((t, PEER_HK), I32), jax.ShapeDtypeStruct((t, PEER_HK), F32)]
    args = [mixed2d, x2d, mod, mod, mod, norm2_g.reshape(1, -1), w_out, w_query, keys]
    if prev is not None:
        tp = prev[0].shape[0]
        prow = pl.BlockSpec((tp // steps, PEER_HK), lambda i: (i, 0))
        in_specs += [prow, prow]
        out_specs += [prow]
        out_shape += [jax.ShapeDtypeStruct((tp, PEER_HK), F32)]
        args += list(prev)
    if fin is not None:
        x1_f, peer_f, mod_f, rows_f, final_g = fin
        tf = x1_f.shape[0]
        frow = pl.BlockSpec((tf // steps, D_MODEL), lambda i: (i, 0))
        in_specs += [frow, frow, _mod_spec(5, rows_f, tf // steps), _const_spec((1, D_MODEL))]
        out_specs += [frow]
        out_shape += [jax.ShapeDtypeStruct((tf, D_MODEL), F32)]
        args += [x1_f, peer_f, mod_f, final_g.reshape(1, -1)]
    return pl.pallas_call(
        functools.partial(_post_body, prev is not None, fin is not None),
        grid=(steps,),
        in_specs=in_specs, out_specs=out_specs, out_shape=out_shape,
        compiler_params=pltpu.CompilerParams(vmem_limit_bytes=VMEM_LIMIT),
        name="post",
    )(*args)


SC_CORES = 2
SC_SUBCORES = 16
SC_LANES = 16
SC_WORKERS = SC_CORES * SC_SUBCORES
SC_TOKENS = 16
SC_SLOTS = 4
SC_CHUNKS = D_MODEL // SC_LANES
PROMPT_PARTS = 8
ROW_TILE = 256


def _sc_mesh():
    return plsc.VectorSubcoreMesh(core_axis_name="c", subcore_axis_name="s")


def _sc_worker():
    return lax.axis_index("s") * SC_CORES + lax.axis_index("c")


def _sc_jobs(table_hbm, idx_v, buf, sem, compute):
    njobs = SC_TOKENS * PEER_HEADS

    def copy(j, slot):
        tt = j // PEER_HEADS
        h = j % PEER_HEADS
        rows = idx_v[tt, pl.ds(h * PEER_TOPK, PEER_TOPK)]
        return pltpu.make_async_copy(table_hbm.at[rows], buf.at[slot], sem.at[slot])

    for s in range(SC_SLOTS):
        copy(s, s).start()

    def group(g, c):
        for s in range(SC_SLOTS):
            j = g * SC_SLOTS + s
            copy(j, s).wait()
            compute(j // PEER_HEADS, j % PEER_HEADS, s)

            @pl.when(j + SC_SLOTS < njobs)
            def _next():
                copy(j + SC_SLOTS, s).start()
        return c

    lax.fori_loop(0, njobs // SC_SLOTS, group, 0)


def _peer_u_body(n_tok, idx_hbm, h2_hbm, u_hbm, pre_hbm, idx_v, h2_v, pre_v, ubuf, acc_v, sem):
    base = _sc_worker() * n_tok
    lane = lax.iota(I32, SC_LANES)

    def compute(tt, h, slot):
        def chunk(c, accs):
            xv = h2_v[tt, pl.ds(c * SC_LANES, SC_LANES)]
            return tuple(a + ubuf[slot, k, pl.ds(c * SC_LANES, SC_LANES)] * xv
                         for k, a in enumerate(accs))
        zero = jnp.zeros((SC_LANES,), F32)
        accs = lax.fori_loop(0, SC_CHUNKS, chunk, (zero,) * PEER_TOPK)
        for k, a in enumerate(accs):
            acc_v[k, :] = a
        tot = zero
        for j in range(SC_LANES):
            tot = tot + plsc.load_gather(acc_v, [lane, jnp.full((SC_LANES,), j, I32)])
        pre_v[tt, pl.ds(h * PEER_TOPK, PEER_TOPK)] = tot

    def block(bi, c):
        t0 = base + bi * SC_TOKENS
        pltpu.sync_copy(idx_hbm.at[pl.ds(t0, SC_TOKENS)], idx_v)
        pltpu.sync_copy(h2_hbm.at[pl.ds(t0, SC_TOKENS)], h2_v)
        _sc_jobs(u_hbm, idx_v, ubuf, sem, compute)
        pltpu.sync_copy(pre_v, pre_hbm.at[pl.ds(t0, SC_TOKENS)])
        return c

    lax.fori_loop(0, n_tok // SC_TOKENS, block, 0)


def _peer_v_body(n_tok, idx_hbm, coef_hbm, v_hbm, out_hbm, idx_v, coef_v, out_v, vbuf, sem):
    base = _sc_worker() * n_tok
    zero = jnp.zeros((SC_LANES,), F32)

    def compute(tt, h, slot):
        row = jnp.full((SC_LANES,), tt, I32)
        cb = [plsc.load_gather(coef_v, [row, jnp.full((SC_LANES,), h * PEER_TOPK + k, I32)])
              for k in range(PEER_TOPK)]

        @plsc.parallel_loop(0, SC_CHUNKS, unroll=2)
        def _chunk(c):
            cs = pl.ds(c * SC_LANES, SC_LANES)
            terms = [cb[k] * vbuf[slot, k, cs] for k in range(PEER_TOPK)]
            while len(terms) > 1:
                terms = [a + b for a, b in zip(terms[0::2], terms[1::2])]
            plsc.addupdate(out_v.at[tt, cs], terms[0])

    def block(bi, c):
        t0 = base + bi * SC_TOKENS
        pltpu.sync_copy(idx_hbm.at[pl.ds(t0, SC_TOKENS)], idx_v)
        pltpu.sync_copy(coef_hbm.at[pl.ds(t0, SC_TOKENS)], coef_v)

        def clear(i, cc):
            out_v[i // SC_CHUNKS, pl.ds((i % SC_CHUNKS) * SC_LANES, SC_LANES)] = zero
            return cc
        lax.fori_loop(0, SC_TOKENS * SC_CHUNKS, clear, 0)
        _sc_jobs(v_hbm, idx_v, vbuf, sem, compute)
        pltpu.sync_copy(out_v, out_hbm.at[pl.ds(t0, SC_TOKENS)])
        return c

    lax.fori_loop(0, n_tok // SC_TOKENS, block, 0)


def _peer_sc(body, idx, rows, table, out_width, name):
    t = idx.shape[0]
    assert t % (SC_WORKERS * SC_TOKENS) == 0
    n_tok = t // SC_WORKERS
    return pl.kernel(
        functools.partial(body, n_tok),
        out_type=jax.ShapeDtypeStruct((t, out_width), F32),
        mesh=_sc_mesh(),
        scratch_types=[pltpu.VMEM((SC_TOKENS, PEER_HK), I32),
                       pltpu.VMEM((SC_TOKENS, rows.shape[1]), F32),
                       pltpu.VMEM((SC_TOKENS, out_width), F32),
                       pltpu.VMEM((SC_SLOTS, PEER_TOPK, D_MODEL), F32)]
                      + ([pltpu.VMEM((PEER_TOPK, SC_LANES), F32)] if body is _peer_u_body else [])
                      + [pltpu.SemaphoreType.DMA((SC_SLOTS,))],
        compiler_params=pltpu.CompilerParams(needs_layout_passes=False),
        name=name,
    )(idx, rows, table)


def _coef_body(pre_ref, gate_ref, coef_ref):
    coef_ref[...] = gate_ref[...] * _gelu(pre_ref[...])


def _coef(pre, gates, tm):
    t = pre.shape[0]
    row = pl.BlockSpec((tm, PEER_HK), lambda i: (i, 0))
    return pl.pallas_call(_coef_body, grid=(t // tm,), in_specs=[row, row], out_specs=row,
                          out_shape=jax.ShapeDtypeStruct((t, PEER_HK), F32), name="coef")(pre, gates)


def _final_body(x1_ref, peer_ref, g2_ref, fng_ref, y_ref):
    x2 = x1_ref[...] + _mod_rows(g2_ref) * peer_ref[...]
    y_ref[...] = x2 * lax.rsqrt(jnp.mean(x2 * x2, axis=-1, keepdims=True) + EPS) * fng_ref[...]


def _final(x1, peer_out, mod, rows_per_batch, final_g, tm):
    t = x1.shape[0]
    row = pl.BlockSpec((tm, D_MODEL), lambda i: (i, 0))
    return pl.pallas_call(
        _final_body, grid=(t // tm,),
        in_specs=[row, row, _mod_spec(5, rows_per_batch, tm), _const_spec((1, D_MODEL))],
        out_specs=row, out_shape=jax.ShapeDtypeStruct((t, D_MODEL), F32), name="final",
    )(x1, peer_out, mod, final_g.reshape(1, -1))


def _expert_gather_v(g, coef, expert_v):
    g["peer_out"] = _peer_sc(_peer_v_body, g["idx"], coef, expert_v, D_MODEL, "peer_v")


def _front(x, mod, conv_buf, s0, pool_buf, start, chunk, tm, wts, prev, fin):
    b, l, _ = x.shape
    t = b * l
    x2d = x.reshape(t, D_MODEL)
    if l >= tm:
        modx = mod.reshape(b, 6, 1, D_MODEL).transpose(1, 0, 2, 3)
    else:
        modx = jnp.repeat(mod.reshape(b, 6, D_MODEL), l, axis=0).transpose(1, 0, 2)
    outs = _inproj(x2d, modx, l, wts["norm1_g"], wts["w_cat"], tm)
    lp = -(-l // chunk) * chunk
    proj = {}
    for (name, w), a in zip(_IN_BLOCKS, outs):
        a = a.reshape(b, l, w)
        proj[name] = a if lp == l else jnp.pad(a, ((0, 0), (0, lp - l), (0, 0)))
    mixed, nconv, ns, npool = _mixer(proj, conv_buf, s0, pool_buf, start, l, chunk,
                                     wts["conv_w"], wts["a_log"], wts["dt_bias"], wts["dn_norm_g"],
                                     wts["w_pool"], wts["pool_scale"])
    mixed2d = mixed[:, :l].reshape(t, D_MODEL)
    res = _post(mixed2d, x2d, modx, l, wts["norm2_g"], wts["w_out"], wts["w_query"], wts["keys"], tm,
                prev=None if prev is None else (prev["pre"], prev["gates"]),
                fin=None if fin is None else (fin["x1"], fin["peer_out"], fin["mod"], fin["l"],
                                              wts["final_norm_g"]))
    x1, h2, idx, gates = res[:4]
    extra = list(res[4:])
    coef_prev = extra.pop(0) if prev is not None else None
    y_fin = extra.pop(0).reshape(fin["b"], fin["l"], D_MODEL) if fin is not None else None
    pre = _peer_sc(_peer_u_body, idx, h2, wts["expert_u"], PEER_HK, "peer_u")
    g = dict(x1=x1, idx=idx, gates=gates, pre=pre, mod=modx, b=b, l=l, tm=tm,
             states=(nconv, ns, npool))
    return g, coef_prev, y_fin


def kernel(x_prompt, x_sample, c_prompt, c_sample, state_conv, state_delta, state_pool, w_ada, b_ada, norm1_g, w_in, conv_w, a_log, dt_bias, dn_norm_g, w_pool, pool_scale, w_out, norm2_g, w_query, sub_keys, expert_u, expert_v, final_norm_g):
    bp = x_prompt.shape[0]
    bs = x_sample.shape[0]
    yp, ys = x_prompt, x_sample
    conv_p, delta_p, pool_p, conv_s, delta_s, pool_s = [], [], [], [], [], []
    zero_conv = jnp.zeros((bp, CONV_WIDTH - 1, QKV_WIDTH), F32)
    zero_delta = jnp.zeros((bp, DN_HEADS, DN_HEAD_DIM, DN_HEAD_DIM), F32)
    zero_pool = jnp.zeros((bp, POOL_BUF, POOL_WIDTH), F32)
    c_all = jnp.concatenate([c_prompt, c_sample], axis=0)
    for layer in range(DEPTH):
        wi = w_in[layer]
        o_b = QKV_WIDTH
        o_z = o_b + 2 * DN_HEADS
        w_ba = jnp.pad(wi[:, o_b:o_z], ((0, 0), (0, LANES - 2 * DN_HEADS)))
        w_cat = jnp.concatenate([wi[:, :o_b], wi[:, o_z:], w_ba], axis=1).astype(BF16)
        last = layer == DEPTH - 1
        wts = dict(
            norm1_g=norm1_g[layer], w_cat=w_cat, conv_w=conv_w[layer], a_log=a_log[layer],
            dt_bias=dt_bias[layer], dn_norm_g=dn_norm_g[layer], w_pool=w_pool[layer],
            pool_scale=pool_scale[layer], w_out=w_out[layer].astype(BF16), norm2_g=norm2_g[layer],
            w_query=w_query[layer].astype(BF16),
            keys=sub_keys[layer].reshape(2 * PEER_HEADS, PEER_NKEYS, PEER_KEY_HALF).astype(BF16),
            expert_u=expert_u[layer], expert_v=expert_v[layer],
            final_norm_g=final_norm_g if last else jnp.ones_like(final_norm_g))
        mod = _ada(c_all, w_ada[layer], b_ada[layer])
        assert last, "final norm is fused into the expert stage"
        step = bp // PROMPT_PARTS
        jobs = [(yp[b0:b0 + step], mod[b0:b0 + step], zero_conv[b0:b0 + step], zero_delta[b0:b0 + step],
                 zero_pool[b0:b0 + step], 0, DN_CHUNK) for b0 in range(0, bp, step)]
        jobs.append((ys, mod[bp:], state_conv[layer], state_delta[layer], state_pool[layer],
                     PAST_LEN, SUBLANES))
        groups = []
        for j, (xg, mg, cg, sg, pg, start, chunk) in enumerate(jobs):
            prev = groups[j - 1] if j >= 1 else None
            fin = groups[j - 2] if j >= 2 and groups[j - 2]["x1"].shape[0] == xg.shape[0] * xg.shape[1] else None
            g, coef_prev, y_fin = _front(xg, mg, cg, sg, pg, start, chunk, ROW_TILE, wts, prev, fin)
            if prev is not None:
                _expert_gather_v(prev, coef_prev, wts["expert_v"])
            if fin is not None:
                fin["y"] = y_fin
            groups.append(g)
        _expert_gather_v(groups[-1], _coef(groups[-1]["pre"], groups[-1]["gates"], ROW_TILE),
                         wts["expert_v"])
        done = []
        for g in groups:
            if "y" not in g:
                g["y"] = _final(g["x1"], g["peer_out"], g["mod"], g["l"], wts["final_norm_g"],
                                g["tm"]).reshape(g["b"], g["l"], D_MODEL)
            done.append((g["y"],) + g["states"])
        yp, cp, sp, pp = (jnp.concatenate(a, axis=0) for a in zip(*done[:-1]))
        ys, cs, ss, ps = done[-1]
        conv_p.append(cp)
        delta_p.append(sp)
        pool_p.append(pp)
        conv_s.append(cs)
        delta_s.append(ss)
        pool_s.append(ps)
    return (yp, ys, jnp.stack(conv_p), jnp.stack(delta_p), jnp.stack(pool_p),
            jnp.stack(conv_s), jnp.stack(delta_s), jnp.stack(pool_s))
```

```python
import functools

import jax
import jax.numpy as jnp
from jax import lax
from jax.experimental import pallas as pl
from jax.experimental.pallas import tpu as pltpu
from jax.experimental.pallas import tpu_sc as plsc

F32 = jnp.float32
BF16 = jnp.bfloat16
I32 = jnp.int32

D_MODEL = 1024
DEPTH = 1
PAST_LEN = 16384
DN_HEADS = 8
DN_HEAD_DIM = 128
DN_WIDTH = DN_HEADS * DN_HEAD_DIM
QKV_WIDTH = 3 * DN_WIDTH
CONV_WIDTH = 4
DN_CHUNK = 64
POOL_WINDOWS = (2, 4, 8, 16)
POOL_GROUP_DIM = 128
POOL_WIDTH = len(POOL_WINDOWS) * POOL_GROUP_DIM
POOL_OUT_GROUP = D_MODEL // len(POOL_WINDOWS)
POOL_BUF = max(POOL_WINDOWS) - 1
PEER_HEADS = 8
PEER_NKEYS = 128
PEER_TOPK = 16
PEER_KEY_HALF = 128
PEER_HK = PEER_HEADS * PEER_TOPK
EPS = 1e-6

LANES = 128
SUBLANES = 8
CONV_PAD = SUBLANES
POOL_PAD = 16
VMEM_LIMIT = 56 * 1024 * 1024

NT_DIMS = (((1,), (1,)), ((), ()))
TN_DIMS = (((0,), (0,)), ((), ()))


def _dot(a, b):
    return jnp.dot(a.astype(BF16), b.astype(BF16), preferred_element_type=F32)


def _dot_nt(a, b):
    return lax.dot_general(a.astype(BF16), b.astype(BF16), NT_DIMS, preferred_element_type=F32)


def _split3(x):
    hi = x.astype(BF16)
    r1 = x - hi.astype(F32)
    mid = r1.astype(BF16)
    lo = (r1 - mid.astype(F32)).astype(BF16)
    return hi, mid, lo


def _silu(x):
    return x * jax.nn.sigmoid(x)


def _gelu(x):
    return 0.5 * x * (1.0 + lax.erf(x * (0.5 ** 0.5)))


def _softplus(x):
    return jnp.maximum(x, 0.0) + jnp.log(1.0 + jnp.exp(-jnp.abs(x)))


def _mod_rows(ref):
    m = ref[...]
    return m.reshape(m.shape[-2], m.shape[-1])


def _mod_spec(k, rows_per_batch, tm):
    if rows_per_batch >= tm:
        tiles = rows_per_batch // tm
        return pl.BlockSpec((1, 1, 1, D_MODEL), lambda i, *_: (k, i // tiles, 0, 0))
    return pl.BlockSpec((1, tm, D_MODEL), lambda i, *_: (k, i, 0))


def _const_spec(shape):
    nd = len(shape)
    return pl.BlockSpec(shape, lambda *_: (0,) * nd)


def _ada_body(c_ref, w_ref, b_ref, o_ref):
    o_ref[...] = _dot(_silu(c_ref[...]), w_ref[...]) + b_ref[...]


def _ada(c, w_ada, b_ada):
    n = c.shape[0]
    return pl.pallas_call(
        _ada_body,
        grid=(6,),
        in_specs=[pl.BlockSpec((n, D_MODEL), lambda j: (0, 0)),
                  pl.BlockSpec((D_MODEL, D_MODEL), lambda j: (0, j)),
                  pl.BlockSpec((1, D_MODEL), lambda j: (0, j))],
        out_specs=pl.BlockSpec((n, D_MODEL), lambda j: (0, j)),
        out_shape=jax.ShapeDtypeStruct((n, 6 * D_MODEL), F32),
        name="ada",
    )(c, w_ada, b_ada.reshape(1, -1))


_IN_BLOCKS = (("qkv", QKV_WIDTH), ("z", DN_WIDTH), ("pool", POOL_WIDTH),
              ("ga", D_MODEL), ("gb", D_MODEL), ("ba", LANES))
_IN_TOTAL = sum(w for _, w in _IN_BLOCKS)
_IN_COL_CHUNK = 512


def _inproj_body(x_ref, sc_ref, sh_ref, g_ref, w_ref, *out_refs):
    x = x_ref[...]
    y = x * lax.rsqrt(jnp.mean(x * x, axis=-1, keepdims=True) + EPS) * g_ref[...]
    h = (y * (1.0 + _mod_rows(sc_ref)) + _mod_rows(sh_ref)).astype(BF16)
    off = 0
    for (_, width), o_ref in zip(_IN_BLOCKS, out_refs):
        for c0 in range(0, width, _IN_COL_CHUNK):
            cw = min(_IN_COL_CHUNK, width - c0)
            o_ref[:, c0:c0 + cw] = jnp.dot(h, w_ref[:, off + c0:off + c0 + cw],
                                           preferred_element_type=F32)
        off += width


def _inproj(x2d, mod, rows_per_batch, norm_g, w_cat, tm):
    t = x2d.shape[0]
    row = lambda w: pl.BlockSpec((tm, w), lambda i: (i, 0))
    return pl.pallas_call(
        _inproj_body,
        grid=(t // tm,),
        in_specs=[row(D_MODEL), _mod_spec(1, rows_per_batch, tm), _mod_spec(0, rows_per_batch, tm),
                  _const_spec((1, D_MODEL)),
                  pl.BlockSpec((D_MODEL, _IN_TOTAL), lambda i: (0, 0), pipeline_mode=pl.Buffered(1))],
        out_specs=[row(w) for _, w in _IN_BLOCKS],
        out_shape=[jax.ShapeDtypeStruct((t, w), F32) for _, w in _IN_BLOCKS],
        compiler_params=pltpu.CompilerParams(vmem_limit_bytes=VMEM_LIMIT),
        name="inproj",
    )(x2d, mod, mod, norm_g.reshape(1, -1), w_cat)


def _mixer_body(C, Lv, start,
                qkv_ref, ba_ref, z_ref, pin_ref, ga_ref, gb_ref, cbuf_ref, s0_ref, pbuf_ref,
                convw_ref, alog_ref, dtb_ref, dng_ref, wpool_ref, pscale_ref,
                mixed_ref, nconv_ref, ns_ref, npool_ref,
                xp_scr, act_scr, s_scr, pp_scr, odn_scr):
    n = pl.program_id(1)
    last = pl.num_programs(1) - 1

    @pl.when(n == 0)
    def _load_state():
        xp_scr[0:CONV_PAD, :] = cbuf_ref[0]
        pp_scr[0:POOL_PAD, :] = pbuf_ref[0]
        s_scr[...] = s0_ref[0]

    xp_scr[CONV_PAD:CONV_PAD + C, :] = qkv_ref[0]
    for c0 in range(0, QKV_WIDTH, 512):
        cs = slice(c0, c0 + 512)
        y = xp_scr[CONV_PAD:CONV_PAD + C, cs] * convw_ref[CONV_WIDTH - 1:CONV_WIDTH, cs]
        for k in range(CONV_WIDTH - 1):
            r0 = CONV_PAD - (CONV_WIDTH - 1) + k
            y = y + xp_scr[r0:r0 + C, cs] * convw_ref[k:k + 1, cs]
        act_scr[:, cs] = _silu(y)

    ba = ba_ref[0]
    lane = lax.broadcasted_iota(I32, (C, LANES), 1)
    beta_all = jax.nn.sigmoid(ba)
    g_all = -jnp.exp(alog_ref[...]) * _softplus(ba + dtb_ref[...])
    if Lv < C:
        valid = lax.broadcasted_iota(I32, (C, LANES), 0) < Lv
        beta_all = jnp.where(valid, beta_all, 0.0)
        g_all = jnp.where(valid, g_all, 0.0)
    ii = lax.broadcasted_iota(I32, (C, C), 0)
    jj = lax.broadcasted_iota(I32, (C, C), 1)
    causal = ii >= jj
    strict = ii > jj
    tril = jnp.where(causal, 1.0, 0.0).astype(BF16)
    eye = jnp.where(ii == jj, 1.0, 0.0)
    gc_all = sum(jnp.dot(tril, part, preferred_element_type=F32) for part in _split3(g_all))
    if C < LANES:
        gc_sq = jnp.concatenate([gc_all, jnp.zeros((LANES - C, LANES), F32)], axis=0)
    else:
        gc_sq = gc_all
    gc_t = gc_sq.T

    for h in range(DN_HEADS):
        hs = slice(h * DN_HEAD_DIM, (h + 1) * DN_HEAD_DIM)
        beta = jnp.sum(jnp.where(lane == h, beta_all, 0.0), axis=1, keepdims=True)
        gcol = jnp.sum(jnp.where(lane == DN_HEADS + h, gc_all, 0.0), axis=1, keepdims=True)
        grow = gc_t[DN_HEADS + h:DN_HEADS + h + 1, 0:C]
        glast = gcol[C - 1:C, :]

        q = act_scr[:, hs]
        k = act_scr[:, DN_WIDTH + h * DN_HEAD_DIM:DN_WIDTH + (h + 1) * DN_HEAD_DIM]
        v = act_scr[:, 2 * DN_WIDTH + h * DN_HEAD_DIM:2 * DN_WIDTH + (h + 1) * DN_HEAD_DIM]
        q = q * lax.rsqrt(jnp.sum(q * q, axis=-1, keepdims=True) + EPS) * (DN_HEAD_DIM ** -0.5)
        k = k * lax.rsqrt(jnp.sum(k * k, axis=-1, keepdims=True) + EPS)
        kb = k * beta
        vb = v * beta

        decay = jnp.where(causal, jnp.exp(jnp.where(causal, gcol - grow, 0.0)), 0.0)
        lower = jnp.where(strict, _dot_nt(kb, k) * decay, 0.0)
        ainv = eye - lower
        pw = lower
        p = 1
        while 2 * p < C:
            pw = _dot(pw, pw)
            ainv = ainv + _dot(ainv, pw)
            p *= 2
        sol = _dot(ainv, jnp.concatenate([vb, kb * jnp.exp(gcol)], axis=1))
        u = sol[:, :DN_HEAD_DIM]
        w = sol[:, DN_HEAD_DIM:]
        qk = _dot_nt(q, k) * decay
        k_tail = k * jnp.exp(glast - gcol)

        S = s_scr[h]
        v_new = u - _dot(w, S)
        o = _dot(q * jnp.exp(gcol), S) + _dot(qk, v_new)
        s_scr[h] = S * jnp.exp(glast) + lax.dot_general(
            k_tail.astype(BF16), v_new.astype(BF16), TN_DIMS, preferred_element_type=F32)

        zf = z_ref[0, :, hs]
        o = o * lax.rsqrt(jnp.mean(o * o, axis=-1, keepdims=True) + EPS) * dng_ref[...] * _silu(zf)
        odn_scr[:, hs] = o

    pp_scr[POOL_PAD:POOL_PAD + C, :] = pin_ref[0]
    pos = start + n * C + lax.broadcasted_iota(I32, (C, 1), 0)
    for gi, win in enumerate(POOL_WINDOWS):
        gs = slice(gi * POOL_GROUP_DIM, (gi + 1) * POOL_GROUP_DIM)
        xg = pp_scr[POOL_PAD:POOL_PAD + C, gs]
        ssum = xg
        for sft in range(1, win):
            ssum = ssum + pp_scr[POOL_PAD - sft:POOL_PAD - sft + C, gs]
        cnt = jnp.minimum(pos + 1, win).astype(F32)
        pooled = ssum / cnt - xg
        os_ = slice(gi * POOL_OUT_GROUP, (gi + 1) * POOL_OUT_GROUP)
        yp = _dot(pooled, wpool_ref[gi]) * pscale_ref[:, os_]
        mixed_ref[0, :, os_] = (jax.nn.sigmoid(ga_ref[0, :, os_]) * odn_scr[:, os_]
                                + jax.nn.sigmoid(gb_ref[0, :, os_]) * yp)

    @pl.when(n == last)
    def _store_state():
        nconv_ref[0] = xp_scr[Lv + CONV_PAD - (CONV_WIDTH - 1):Lv + CONV_PAD, :]
        npool_ref[0] = pp_scr[Lv + POOL_PAD - POOL_BUF:Lv + POOL_PAD, :]
        ns_ref[0] = s_scr[...]

    xp_scr[0:CONV_PAD, :] = xp_scr[C:C + CONV_PAD, :]
    pp_scr[0:POOL_PAD, :] = pp_scr[C:C + POOL_PAD, :]


def _mixer(proj, conv_buf, s0, pool_buf, start, seq_len, C,
           conv_w, a_log, dt_bias, dn_norm_g, w_pool, pool_scale):
    b, lp, _ = proj["qkv"].shape
    nchunks = lp // C
    lv = seq_len - (nchunks - 1) * C
    cbuf = jnp.pad(conv_buf, ((0, 0), (CONV_PAD - (CONV_WIDTH - 1), 0), (0, 0)))
    pbuf = jnp.pad(pool_buf, ((0, 0), (POOL_PAD - POOL_BUF, 0), (0, 0)))
    lane_pad = lambda a: jnp.pad(a.reshape(1, -1), ((0, 0), (DN_HEADS, LANES - 2 * DN_HEADS)))
    chunk = lambda w: pl.BlockSpec((1, C, w), lambda i, j: (i, j, 0))
    state = lambda *s: pl.BlockSpec((1,) + s, lambda i, j: (i,) + (0,) * len(s))
    return pl.pallas_call(
        functools.partial(_mixer_body, C, lv, start),
        grid=(b, nchunks),
        in_specs=[chunk(QKV_WIDTH), chunk(LANES), chunk(DN_WIDTH), chunk(POOL_WIDTH),
                  chunk(D_MODEL), chunk(D_MODEL),
                  state(CONV_PAD, QKV_WIDTH), state(DN_HEADS, DN_HEAD_DIM, DN_HEAD_DIM),
                  state(POOL_PAD, POOL_WIDTH),
                  _const_spec((CONV_WIDTH, QKV_WIDTH)), _const_spec((1, LANES)), _const_spec((1, LANES)),
                  _const_spec((1, DN_HEAD_DIM)),
                  _const_spec((len(POOL_WINDOWS), POOL_GROUP_DIM, POOL_OUT_GROUP)),
                  _const_spec((1, D_MODEL))],
        out_specs=[chunk(D_MODEL), state(CONV_WIDTH - 1, QKV_WIDTH),
                   state(DN_HEADS, DN_HEAD_DIM, DN_HEAD_DIM), state(POOL_BUF, POOL_WIDTH)],
        out_shape=[jax.ShapeDtypeStruct((b, lp, D_MODEL), F32),
                   jax.ShapeDtypeStruct((b, CONV_WIDTH - 1, QKV_WIDTH), F32),
                   jax.ShapeDtypeStruct((b, DN_HEADS, DN_HEAD_DIM, DN_HEAD_DIM), F32),
                   jax.ShapeDtypeStruct((b, POOL_BUF, POOL_WIDTH), F32)],
        scratch_shapes=[pltpu.VMEM((CONV_PAD + C + CONV_PAD, QKV_WIDTH), F32),
                        pltpu.VMEM((C, QKV_WIDTH), F32),
                        pltpu.VMEM((DN_HEADS, DN_HEAD_DIM, DN_HEAD_DIM), F32),
                        pltpu.VMEM((POOL_PAD + C + POOL_PAD, POOL_WIDTH), F32),
                        pltpu.VMEM((C, DN_WIDTH), F32)],
        compiler_params=pltpu.CompilerParams(dimension_semantics=("arbitrary", "arbitrary"),
                                             vmem_limit_bytes=VMEM_LIMIT),
        name="mixer",
    )(proj["qkv"], proj["ba"], proj["z"], proj["pool"], proj["ga"], proj["gb"], cbuf, s0, pbuf,
      conv_w, lane_pad(a_log), lane_pad(dt_bias), dn_norm_g.reshape(1, -1), w_pool,
      pool_scale.reshape(1, -1))


def _top16(s, ids, payload=None):
    big = float(2 ** 24)
    vals, sel, pays = [], [], []
    for _ in range(PEER_TOPK):
        m = jnp.max(s, axis=0, keepdims=True)
        am = jnp.min(jnp.where(s == m, ids, big), axis=0, keepdims=True)
        hit = ids == am
        if payload is not None:
            pays.append(jnp.max(jnp.where(hit, payload, -1.0), axis=0, keepdims=True))
        s = jnp.where(hit, -jnp.inf, s)
        vals.append(m)
        sel.append(am)
    out = (jnp.concatenate(vals, axis=0), jnp.concatenate(sel, axis=0))
    if payload is not None:
        out += (jnp.concatenate(pays, axis=0),)
    return out


_CAND_EDGE = 4


def _post_body(has_prev, has_fin, mixed_ref, x_ref, g1_ref, sc2_ref, sh2_ref, n2g_ref, wout_ref,
               wq_ref, keys_ref, *refs):
    refs = list(refs)
    prev_in = [refs.pop(0) for _ in range(2 if has_prev else 0)]
    fin_in = [refs.pop(0) for _ in range(4 if has_fin else 0)]
    x1_ref, h2_ref, idx_ref, gate_ref = refs[:4]
    extra_out = refs[4:]
    if has_prev:
        pre_ref, pgate_ref = prev_in
        extra_out.pop(0)[...] = pgate_ref[...] * _gelu(pre_ref[...])
    if has_fin:
        _final_body(*fin_in, extra_out.pop(0))
    tm = x_ref.shape[0]
    x1 = x_ref[...] + _mod_rows(g1_ref) * _dot(mixed_ref[...], wout_ref[...])
    x1_ref[...] = x1
    y = x1 * lax.rsqrt(jnp.mean(x1 * x1, axis=-1, keepdims=True) + EPS) * n2g_ref[...]
    h2 = y * (1.0 + _mod_rows(sc2_ref)) + _mod_rows(sh2_ref)
    h2_ref[...] = h2
    q = _dot(h2, wq_ref[...])

    K = PEER_TOPK
    key_id = lax.broadcasted_iota(I32, (PEER_NKEYS, 1), 0).astype(F32)
    r16 = lax.broadcasted_iota(I32, (K, 1), 0)
    cand_id = jnp.concatenate([(a * K + r16) for a in range(_CAND_EDGE)]
                              + [(r16 * K + b) for b in range(_CAND_EDGE)], axis=0).astype(F32)
    dup = r16 < _CAND_EDGE
    idx_rows, gate_rows = [], []
    for h in range(PEER_HEADS):
        half = []
        for p in range(2):
            c0 = (h * 2 + p) * PEER_KEY_HALF
            st = _dot_nt(keys_ref[h * 2 + p], q[:, c0:c0 + PEER_KEY_HALF])
            half.append(_top16(st, key_id))
        (s1, i1), (s2, i2) = half
        cand = jnp.concatenate(
            [s1[a:a + 1] + s2 for a in range(_CAND_EDGE)]
            + [jnp.where(dup, -jnp.inf, s1 + s2[b:b + 1]) for b in range(_CAND_EDGE)], axis=0)
        cidx = jnp.concatenate(
            [i1[a:a + 1] * PEER_NKEYS + i2 for a in range(_CAND_EDGE)]
            + [i1 * PEER_NKEYS + i2[b:b + 1] for b in range(_CAND_EDGE)], axis=0)
        best, _, eidx = _top16(cand, cand_id, cidx)
        e = jnp.exp(best - best[0:1])
        gate_rows.append(e / jnp.sum(e, axis=0, keepdims=True))
        idx_rows.append(eidx)
    idx_ref[...] = jnp.concatenate(idx_rows, axis=0).T.astype(I32)
    gate_ref[...] = jnp.concatenate(gate_rows, axis=0).T


def _post(mixed2d, x2d, mod, rows_per_batch, norm2_g, w_out, w_query, keys, tm, prev=None, fin=None):
    t = x2d.shape[0]
    steps = t // tm
    row = lambda w: pl.BlockSpec((tm, w), lambda i: (i, 0))
    in_specs = [row(D_MODEL), row(D_MODEL),
                _mod_spec(2, rows_per_batch, tm), _mod_spec(4, rows_per_batch, tm),
                _mod_spec(3, rows_per_batch, tm), _const_spec((1, D_MODEL)),
                _const_spec((D_MODEL, D_MODEL)), _const_spec((D_MODEL, 2 * PEER_HEADS * PEER_KEY_HALF)),
                _const_spec((2 * PEER_HEADS, PEER_NKEYS, PEER_KEY_HALF))]
    out_specs = [row(D_MODEL), row(D_MODEL), row(PEER_HK), row(PEER_HK)]
    out_shape = [jax.ShapeDtypeStruct((t, D_MODEL), F32), jax.ShapeDtypeStruct((t, D_MODEL), F32),
                 jax.ShapeDtypeStruct((t, PEER_HK), I32), jax.ShapeDtypeStruct((t, PEER_HK), F32)]
    args = [mixed2d, x2d, mod, mod, mod, norm2_g.reshape(1, -1), w_out, w_query, keys]
    if prev is not None:
        tp = prev[0].shape[0]
        prow = pl.BlockSpec((tp // steps, PEER_HK), lambda i: (i, 0))
        in_specs += [prow, prow]
        out_specs += [prow]
        out_shape += [jax.ShapeDtypeStruct((tp, PEER_HK), F32)]
        args += list(prev)
    if fin is not None:
        x1_f, peer_f, mod_f, rows_f, final_g = fin
        tf = x1_f.shape[0]
        frow = pl.BlockSpec((tf // steps, D_MODEL), lambda i: (i, 0))
        in_specs += [frow, frow, _mod_spec(5, rows_f, tf // steps), _const_spec((1, D_MODEL))]
        out_specs += [frow]
        out_shape += [jax.ShapeDtypeStruct((tf, D_MODEL), F32)]
        args += [x1_f, peer_f, mod_f, final_g.reshape(1, -1)]
    return pl.pallas_call(
        functools.partial(_post_body, prev is not None, fin is not None),
        grid=(steps,),
        in_specs=in_specs, out_specs=out_specs, out_shape=out_shape,
        compiler_params=pltpu.CompilerParams(vmem_limit_bytes=VMEM_LIMIT),
        name="post",
    )(*args)


SC_CORES = 2
SC_SUBCORES = 16
SC_LANES = 16
SC_WORKERS = SC_CORES * SC_SUBCORES
SC_TOKENS = 16
SC_SLOTS = 4
SC_CHUNKS = D_MODEL // SC_LANES
PROMPT_PARTS = 8
FIN_LAG = 3
ROW_TILE = 256


def _sc_mesh():
    return plsc.VectorSubcoreMesh(core_axis_name="c", subcore_axis_name="s")


def _sc_worker():
    return lax.axis_index("s") * SC_CORES + lax.axis_index("c")


def _sc_jobs(table_hbm, idx_v, buf, sem, compute):
    njobs = SC_TOKENS * PEER_HEADS

    def copy(j, slot):
        tt = j // PEER_HEADS
        h = j % PEER_HEADS
        rows = idx_v[tt, pl.ds(h * PEER_TOPK, PEER_TOPK)]
        return pltpu.make_async_copy(table_hbm.at[rows], buf.at[slot], sem.at[slot])

    for s in range(SC_SLOTS):
        copy(s, s).start()

    def group(g, c):
        for s in range(SC_SLOTS):
            j = g * SC_SLOTS + s
            copy(j, s).wait()
            compute(j // PEER_HEADS, j % PEER_HEADS, s)

            @pl.when(j + SC_SLOTS < njobs)
            def _next():
                copy(j + SC_SLOTS, s).start()
        return c

    lax.fori_loop(0, njobs // SC_SLOTS, group, 0)


def _peer_u_body(n_tok, idx_hbm, h2_hbm, u_hbm, pre_hbm, idx_v, h2_v, pre_v, ubuf, acc_v, sem):
    base = _sc_worker() * n_tok
    lane = lax.iota(I32, SC_LANES)

    def compute(tt, h, slot):
        def chunk(c, accs):
            xv = h2_v[tt, pl.ds(c * SC_LANES, SC_LANES)]
            return tuple(a + ubuf[slot, k, pl.ds(c * SC_LANES, SC_LANES)] * xv
                         for k, a in enumerate(accs))
        zero = jnp.zeros((SC_LANES,), F32)
        accs = lax.fori_loop(0, SC_CHUNKS, chunk, (zero,) * PEER_TOPK)
        for k, a in enumerate(accs):
            acc_v[k, :] = a
        tot = zero
        for j in range(SC_LANES):
            tot = tot + plsc.load_gather(acc_v, [lane, jnp.full((SC_LANES,), j, I32)])
        pre_v[tt, pl.ds(h * PEER_TOPK, PEER_TOPK)] = tot

    def block(bi, c):
        t0 = base + bi * SC_TOKENS
        pltpu.sync_copy(idx_hbm.at[pl.ds(t0, SC_TOKENS)], idx_v)
        pltpu.sync_copy(h2_hbm.at[pl.ds(t0, SC_TOKENS)], h2_v)
        _sc_jobs(u_hbm, idx_v, ubuf, sem, compute)
        pltpu.sync_copy(pre_v, pre_hbm.at[pl.ds(t0, SC_TOKENS)])
        return c

    lax.fori_loop(0, n_tok // SC_TOKENS, block, 0)


def _peer_v_body(n_tok, idx_hbm, coef_hbm, v_hbm, out_hbm, idx_v, coef_v, out_v, vbuf, sem):
    base = _sc_worker() * n_tok
    zero = jnp.zeros((SC_LANES,), F32)

    def compute(tt, h, slot):
        row = jnp.full((SC_LANES,), tt, I32)
        cb = [plsc.load_gather(coef_v, [row, jnp.full((SC_LANES,), h * PEER_TOPK + k, I32)])
              for k in range(PEER_TOPK)]

        @plsc.parallel_loop(0, SC_CHUNKS, unroll=2)
        def _chunk(c):
            cs = pl.ds(c * SC_LANES, SC_LANES)
            terms = [cb[k] * vbuf[slot, k, cs] for k in range(PEER_TOPK)]
            while len(terms) > 1:
                terms = [a + b for a, b in zip(terms[0::2], terms[1::2])]
            plsc.addupdate(out_v.at[tt, cs], terms[0])

    def block(bi, c):
        t0 = base + bi * SC_TOKENS
        pltpu.sync_copy(idx_hbm.at[pl.ds(t0, SC_TOKENS)], idx_v)
        pltpu.sync_copy(coef_hbm.at[pl.ds(t0, SC_TOKENS)], coef_v)

        def clear(i, cc):
            out_v[i // SC_CHUNKS, pl.ds((i % SC_CHUNKS) * SC_LANES, SC_LANES)] = zero
            return cc
        lax.fori_loop(0, SC_TOKENS * SC_CHUNKS, clear, 0)
        _sc_jobs(v_hbm, idx_v, vbuf, sem, compute)
        pltpu.sync_copy(out_v, out_hbm.at[pl.ds(t0, SC_TOKENS)])
        return c

    lax.fori_loop(0, n_tok // SC_TOKENS, block, 0)


def _peer_sc(body, idx, rows, table, out_width, name):
    t = idx.shape[0]
    assert t % (SC_WORKERS * SC_TOKENS) == 0
    n_tok = t // SC_WORKERS
    return pl.kernel(
        functools.partial(body, n_tok),
        out_type=jax.ShapeDtypeStruct((t, out_width), F32),
        mesh=_sc_mesh(),
        scratch_types=[pltpu.VMEM((SC_TOKENS, PEER_HK), I32),
                       pltpu.VMEM((SC_TOKENS, rows.shape[1]), F32),
                       pltpu.VMEM((SC_TOKENS, out_width), F32),
                       pltpu.VMEM((SC_SLOTS, PEER_TOPK, D_MODEL), F32)]
                      + ([pltpu.VMEM((PEER_TOPK, SC_LANES), F32)] if body is _peer_u_body else [])
                      + [pltpu.SemaphoreType.DMA((SC_SLOTS,))],
        compiler_params=pltpu.CompilerParams(needs_layout_passes=False),
        name=name,
    )(idx, rows, table)


def _coef_body(pre_ref, gate_ref, coef_ref):
    coef_ref[...] = gate_ref[...] * _gelu(pre_ref[...])


def _coef(pre, gates, tm):
    t = pre.shape[0]
    row = pl.BlockSpec((tm, PEER_HK), lambda i: (i, 0))
    return pl.pallas_call(_coef_body, grid=(t // tm,), in_specs=[row, row], out_specs=row,
                          out_shape=jax.ShapeDtypeStruct((t, PEER_HK), F32), name="coef")(pre, gates)


def _final_body(x1_ref, peer_ref, g2_ref, fng_ref, y_ref):
    x2 = x1_ref[...] + _mod_rows(g2_ref) * peer_ref[...]
    y_ref[...] = x2 * lax.rsqrt(jnp.mean(x2 * x2, axis=-1, keepdims=True) + EPS) * fng_ref[...]


def _final(x1, peer_out, mod, rows_per_batch, final_g, tm):
    t = x1.shape[0]
    row = pl.BlockSpec((tm, D_MODEL), lambda i: (i, 0))
    return pl.pallas_call(
        _final_body, grid=(t // tm,),
        in_specs=[row, row, _mod_spec(5, rows_per_batch, tm), _const_spec((1, D_MODEL))],
        out_specs=row, out_shape=jax.ShapeDtypeStruct((t, D_MODEL), F32), name="final",
    )(x1, peer_out, mod, final_g.reshape(1, -1))


def _expert_gather_v(g, coef, expert_v):
    g["peer_out"] = _peer_sc(_peer_v_body, g["idx"], coef, expert_v, D_MODEL, "peer_v")


def _front(x, mod, conv_buf, s0, pool_buf, start, chunk, tm, wts, prev, fin):
    b, l, _ = x.shape
    t = b * l
    x2d = x.reshape(t, D_MODEL)
    if l >= tm:
        modx = mod.reshape(b, 6, 1, D_MODEL).transpose(1, 0, 2, 3)
    else:
        modx = jnp.repeat(mod.reshape(b, 6, D_MODEL), l, axis=0).transpose(1, 0, 2)
    outs = _inproj(x2d, modx, l, wts["norm1_g"], wts["w_cat"], tm)
    lp = -(-l // chunk) * chunk
    proj = {}
    for (name, w), a in zip(_IN_BLOCKS, outs):
        a = a.reshape(b, l, w)
        proj[name] = a if lp == l else jnp.pad(a, ((0, 0), (0, lp - l), (0, 0)))
    mixed, nconv, ns, npool = _mixer(proj, conv_buf, s0, pool_buf, start, l, chunk,
                                     wts["conv_w"], wts["a_log"], wts["dt_bias"], wts["dn_norm_g"],
                                     wts["w_pool"], wts["pool_scale"])
    mixed2d = mixed[:, :l].reshape(t, D_MODEL)
    res = _post(mixed2d, x2d, modx, l, wts["norm2_g"], wts["w_out"], wts["w_query"], wts["keys"], tm,
                prev=None if prev is None else (prev["pre"], prev["gates"]),
                fin=None if fin is None else (fin["x1"], fin["peer_out"], fin["mod"], fin["l"],
                                              wts["final_norm_g"]))
    x1, h2, idx, gates = res[:4]
    extra = list(res[4:])
    coef_prev = extra.pop(0) if prev is not None else None
    y_fin = extra.pop(0).reshape(fin["b"], fin["l"], D_MODEL) if fin is not None else None
    pre = _peer_sc(_peer_u_body, idx, h2, wts["expert_u"], PEER_HK, "peer_u")
    g = dict(x1=x1, idx=idx, gates=gates, pre=pre, mod=modx, b=b, l=l, tm=tm,
             states=(nconv, ns, npool))
    return g, coef_prev, y_fin


def kernel(x_prompt, x_sample, c_prompt, c_sample, state_conv, state_delta, state_pool, w_ada, b_ada, norm1_g, w_in, conv_w, a_log, dt_bias, dn_norm_g, w_pool, pool_scale, w_out, norm2_g, w_query, sub_keys, expert_u, expert_v, final_norm_g):
    bp = x_prompt.shape[0]
    bs = x_sample.shape[0]
    yp, ys = x_prompt, x_sample
    conv_p, delta_p, pool_p, conv_s, delta_s, pool_s = [], [], [], [], [], []
    zero_conv = jnp.zeros((bp, CONV_WIDTH - 1, QKV_WIDTH), F32)
    zero_delta = jnp.zeros((bp, DN_HEADS, DN_HEAD_DIM, DN_HEAD_DIM), F32)
    zero_pool = jnp.zeros((bp, POOL_BUF, POOL_WIDTH), F32)
    c_all = jnp.concatenate([c_prompt, c_sample], axis=0)
    for layer in range(DEPTH):
        wi = w_in[layer]
        o_b = QKV_WIDTH
        o_z = o_b + 2 * DN_HEADS
        w_ba = jnp.pad(wi[:, o_b:o_z], ((0, 0), (0, LANES - 2 * DN_HEADS)))
        w_cat = jnp.concatenate([wi[:, :o_b], wi[:, o_z:], w_ba], axis=1).astype(BF16)
        last = layer == DEPTH - 1
        wts = dict(
            norm1_g=norm1_g[layer], w_cat=w_cat, conv_w=conv_w[layer], a_log=a_log[layer],
            dt_bias=dt_bias[layer], dn_norm_g=dn_norm_g[layer], w_pool=w_pool[layer],
            pool_scale=pool_scale[layer], w_out=w_out[layer].astype(BF16), norm2_g=norm2_g[layer],
            w_query=w_query[layer].astype(BF16),
            keys=sub_keys[layer].reshape(2 * PEER_HEADS, PEER_NKEYS, PEER_KEY_HALF).astype(BF16),
            expert_u=expert_u[layer], expert_v=expert_v[layer],
            final_norm_g=final_norm_g if last else jnp.ones_like(final_norm_g))
        mod = _ada(c_all, w_ada[layer], b_ada[layer])
        assert last, "final norm is fused into the expert stage"
        step = bp // PROMPT_PARTS
        jobs = [(yp[b0:b0 + step], mod[b0:b0 + step], zero_conv[b0:b0 + step], zero_delta[b0:b0 + step],
                 zero_pool[b0:b0 + step], 0, DN_CHUNK) for b0 in range(0, bp, step)]
        jobs.append((ys, mod[bp:], state_conv[layer], state_delta[layer], state_pool[layer],
                     PAST_LEN, SUBLANES))
        groups = []
        for j, (xg, mg, cg, sg, pg, start, chunk) in enumerate(jobs):
            prev = groups[j - 1] if j >= 1 else None
            fin = groups[j - FIN_LAG] if j >= FIN_LAG else None
            if fin is not None and fin["x1"].shape[0] != xg.shape[0] * xg.shape[1]:
                fin = None
            g, coef_prev, y_fin = _front(xg, mg, cg, sg, pg, start, chunk, ROW_TILE, wts, prev, fin)
            if prev is not None:
                _expert_gather_v(prev, coef_prev, wts["expert_v"])
            if fin is not None:
                fin["y"] = y_fin
            groups.append(g)
        _expert_gather_v(groups[-1], _coef(groups[-1]["pre"], groups[-1]["gates"], ROW_TILE),
                         wts["expert_v"])
        done = []
        for g in groups:
            if "y" not in g:
                g["y"] = _final(g["x1"], g["peer_out"], g["mod"], g["l"], wts["final_norm_g"],
                                g["tm"]).reshape(g["b"], g["l"], D_MODEL)
            done.append((g["y"],) + g["states"])
        yp, cp, sp, pp = (jnp.concatenate(a, axis=0) for a in zip(*done[:-1]))
        ys, cs, ss, ps = done[-1]
        conv_p.append(cp)
        delta_p.append(sp)
        pool_p.append(pp)
        conv_s.append(cs)
        delta_s.append(ss)
        pool_s.append(ps)
    return (yp, ys, jnp.stack(conv_p), jnp.stack(delta_p), jnp.stack(pool_p),
            jnp.stack(conv_s), jnp.stack(delta_s), jnp.stack(pool_s))
```

```python
import functools

import jax
import jax.numpy as jnp
from jax import lax
from jax.experimental import pallas as pl
from jax.experimental.pallas import tpu as pltpu
from jax.experimental.pallas import tpu_sc as plsc

F32 = jnp.float32
BF16 = jnp.bfloat16
I32 = jnp.int32

D_MODEL = 1024
DEPTH = 1
PAST_LEN = 16384
DN_HEADS = 8
DN_HEAD_DIM = 128
DN_WIDTH = DN_HEADS * DN_HEAD_DIM
QKV_WIDTH = 3 * DN_WIDTH
CONV_WIDTH = 4
DN_CHUNK = 64
POOL_WINDOWS = (2, 4, 8, 16)
POOL_GROUP_DIM = 128
POOL_WIDTH = len(POOL_WINDOWS) * POOL_GROUP_DIM
POOL_OUT_GROUP = D_MODEL // len(POOL_WINDOWS)
POOL_BUF = max(POOL_WINDOWS) - 1
PEER_HEADS = 8
PEER_NKEYS = 128
PEER_TOPK = 16
PEER_KEY_HALF = 128
PEER_HK = PEER_HEADS * PEER_TOPK
EPS = 1e-6

LANES = 128
SUBLANES = 8
CONV_PAD = SUBLANES
POOL_PAD = 16
VMEM_LIMIT = 56 * 1024 * 1024

NT_DIMS = (((1,), (1,)), ((), ()))
TN_DIMS = (((0,), (0,)), ((), ()))


def _dot(a, b):
    return jnp.dot(a.astype(BF16), b.astype(BF16), preferred_element_type=F32)


def _dot_nt(a, b):
    return lax.dot_general(a.astype(BF16), b.astype(BF16), NT_DIMS, preferred_element_type=F32)


def _split3(x):
    hi = x.astype(BF16)
    r1 = x - hi.astype(F32)
    mid = r1.astype(BF16)
    lo = (r1 - mid.astype(F32)).astype(BF16)
    return hi, mid, lo


def _silu(x):
    return x * jax.nn.sigmoid(x)


def _gelu(x):
    return 0.5 * x * (1.0 + lax.erf(x * (0.5 ** 0.5)))


def _softplus(x):
    return jnp.maximum(x, 0.0) + jnp.log(1.0 + jnp.exp(-jnp.abs(x)))


def _mod_rows(ref):
    m = ref[...]
    return m.reshape(m.shape[-2], m.shape[-1])


def _mod_spec(k, rows_per_batch, tm):
    if rows_per_batch >= tm:
        tiles = rows_per_batch // tm
        return pl.BlockSpec((1, 1, 1, D_MODEL), lambda i, *_: (k, i // tiles, 0, 0))
    return pl.BlockSpec((1, tm, D_MODEL), lambda i, *_: (k, i, 0))


def _const_spec(shape):
    nd = len(shape)
    return pl.BlockSpec(shape, lambda *_: (0,) * nd)


def _ada_body(c_ref, w_ref, b_ref, o_ref):
    o_ref[...] = _dot(_silu(c_ref[...]), w_ref[...]) + b_ref[...]


def _ada(c, w_ada, b_ada):
    n = c.shape[0]
    return pl.pallas_call(
        _ada_body,
        grid=(6,),
        in_specs=[pl.BlockSpec((n, D_MODEL), lambda j: (0, 0)),
                  pl.BlockSpec((D_MODEL, D_MODEL), lambda j: (0, j)),
                  pl.BlockSpec((1, D_MODEL), lambda j: (0, j))],
        out_specs=pl.BlockSpec((n, D_MODEL), lambda j: (0, j)),
        out_shape=jax.ShapeDtypeStruct((n, 6 * D_MODEL), F32),
        name="ada",
    )(c, w_ada, b_ada.reshape(1, -1))


_IN_BLOCKS = (("qkv", QKV_WIDTH), ("z", DN_WIDTH), ("pool", POOL_WIDTH),
              ("ga", D_MODEL), ("gb", D_MODEL), ("ba", LANES))
_IN_TOTAL = sum(w for _, w in _IN_BLOCKS)
_IN_COL_CHUNK = 512


def _inproj_body(x_ref, sc_ref, sh_ref, g_ref, w_ref, *out_refs):
    x = x_ref[...]
    y = x * lax.rsqrt(jnp.mean(x * x, axis=-1, keepdims=True) + EPS) * g_ref[...]
    h = (y * (1.0 + _mod_rows(sc_ref)) + _mod_rows(sh_ref)).astype(BF16)
    off = 0
    for (_, width), o_ref in zip(_IN_BLOCKS, out_refs):
        for c0 in range(0, width, _IN_COL_CHUNK):
            cw = min(_IN_COL_CHUNK, width - c0)
            o_ref[:, c0:c0 + cw] = jnp.dot(h, w_ref[:, off + c0:off + c0 + cw],
                                           preferred_element_type=F32)
        off += width


def _inproj(x2d, mod, rows_per_batch, norm_g, w_cat, tm):
    t = x2d.shape[0]
    row = lambda w: pl.BlockSpec((tm, w), lambda i: (i, 0))
    return pl.pallas_call(
        _inproj_body,
        grid=(t // tm,),
        in_specs=[row(D_MODEL), _mod_spec(1, rows_per_batch, tm), _mod_spec(0, rows_per_batch, tm),
                  _const_spec((1, D_MODEL)),
                  pl.BlockSpec((D_MODEL, _IN_TOTAL), lambda i: (0, 0), pipeline_mode=pl.Buffered(1))],
        out_specs=[row(w) for _, w in _IN_BLOCKS],
        out_shape=[jax.ShapeDtypeStruct((t, w), F32) for _, w in _IN_BLOCKS],
        compiler_params=pltpu.CompilerParams(vmem_limit_bytes=VMEM_LIMIT),
        name="inproj",
    )(x2d, mod, mod, norm_g.reshape(1, -1), w_cat)


def _mixer_body(C, Lv, start,
                qkv_ref, ba_ref, z_ref, pin_ref, ga_ref, gb_ref, cbuf_ref, s0_ref, pbuf_ref,
                convw_ref, alog_ref, dtb_ref, dng_ref, wpool_ref, pscale_ref,
                mixed_ref, nconv_ref, ns_ref, npool_ref,
                xp_scr, act_scr, s_scr, pp_scr, odn_scr):
    n = pl.program_id(1)
    last = pl.num_programs(1) - 1

    @pl.when(n == 0)
    def _load_state():
        xp_scr[0:CONV_PAD, :] = cbuf_ref[0]
        pp_scr[0:POOL_PAD, :] = pbuf_ref[0]
        s_scr[...] = s0_ref[0]

    xp_scr[CONV_PAD:CONV_PAD + C, :] = qkv_ref[0]
    for c0 in range(0, QKV_WIDTH, 512):
        cs = slice(c0, c0 + 512)
        y = xp_scr[CONV_PAD:CONV_PAD + C, cs] * convw_ref[CONV_WIDTH - 1:CONV_WIDTH, cs]
        for k in range(CONV_WIDTH - 1):
            r0 = CONV_PAD - (CONV_WIDTH - 1) + k
            y = y + xp_scr[r0:r0 + C, cs] * convw_ref[k:k + 1, cs]
        act_scr[:, cs] = _silu(y)

    ba = ba_ref[0]
    lane = lax.broadcasted_iota(I32, (C, LANES), 1)
    beta_all = jax.nn.sigmoid(ba)
    g_all = -jnp.exp(alog_ref[...]) * _softplus(ba + dtb_ref[...])
    if Lv < C:
        valid = lax.broadcasted_iota(I32, (C, LANES), 0) < Lv
        beta_all = jnp.where(valid, beta_all, 0.0)
        g_all = jnp.where(valid, g_all, 0.0)
    ii = lax.broadcasted_iota(I32, (C, C), 0)
    jj = lax.broadcasted_iota(I32, (C, C), 1)
    causal = ii >= jj
    strict = ii > jj
    tril = jnp.where(causal, 1.0, 0.0).astype(BF16)
    eye = jnp.where(ii == jj, 1.0, 0.0)
    gc_all = sum(jnp.dot(tril, part, preferred_element_type=F32) for part in _split3(g_all))
    if C < LANES:
        gc_sq = jnp.concatenate([gc_all, jnp.zeros((LANES - C, LANES), F32)], axis=0)
    else:
        gc_sq = gc_all
    gc_t = gc_sq.T

    for h in range(DN_HEADS):
        hs = slice(h * DN_HEAD_DIM, (h + 1) * DN_HEAD_DIM)
        beta = jnp.sum(jnp.where(lane == h, beta_all, 0.0), axis=1, keepdims=True)
        gcol = jnp.sum(jnp.where(lane == DN_HEADS + h, gc_all, 0.0), axis=1, keepdims=True)
        grow = gc_t[DN_HEADS + h:DN_HEADS + h + 1, 0:C]
        glast = gcol[C - 1:C, :]

        q = act_scr[:, hs]
        k = act_scr[:, DN_WIDTH + h * DN_HEAD_DIM:DN_WIDTH + (h + 1) * DN_HEAD_DIM]
        v = act_scr[:, 2 * DN_WIDTH + h * DN_HEAD_DIM:2 * DN_WIDTH + (h + 1) * DN_HEAD_DIM]
        q = q * lax.rsqrt(jnp.sum(q * q, axis=-1, keepdims=True) + EPS) * (DN_HEAD_DIM ** -0.5)
        k = k * lax.rsqrt(jnp.sum(k * k, axis=-1, keepdims=True) + EPS)
        kb = k * beta
        vb = v * beta

        decay = jnp.where(causal, jnp.exp(jnp.where(causal, gcol - grow, 0.0)), 0.0)
        lower = jnp.where(strict, _dot_nt(kb, k) * decay, 0.0)
        ainv = eye - lower
        pw = lower
        p = 1
        while 2 * p < C:
            pw = _dot(pw, pw)
            ainv = ainv + _dot(ainv, pw)
            p *= 2
        sol = _dot(ainv, jnp.concatenate([vb, kb * jnp.exp(gcol)], axis=1))
        u = sol[:, :DN_HEAD_DIM]
        w = sol[:, DN_HEAD_DIM:]
        qk = _dot_nt(q, k) * decay
        k_tail = k * jnp.exp(glast - gcol)

        S = s_scr[h]
        v_new = u - _dot(w, S)
        o = _dot(q * jnp.exp(gcol), S) + _dot(qk, v_new)
        s_scr[h] = S * jnp.exp(glast) + lax.dot_general(
            k_tail.astype(BF16), v_new.astype(BF16), TN_DIMS, preferred_element_type=F32)

        zf = z_ref[0, :, hs]
        o = o * lax.rsqrt(jnp.mean(o * o, axis=-1, keepdims=True) + EPS) * dng_ref[...] * _silu(zf)
        odn_scr[:, hs] = o

    pp_scr[POOL_PAD:POOL_PAD + C, :] = pin_ref[0]
    pos = start + n * C + lax.broadcasted_iota(I32, (C, 1), 0)
    for gi, win in enumerate(POOL_WINDOWS):
        gs = slice(gi * POOL_GROUP_DIM, (gi + 1) * POOL_GROUP_DIM)
        xg = pp_scr[POOL_PAD:POOL_PAD + C, gs]
        ssum = xg
        for sft in range(1, win):
            ssum = ssum + pp_scr[POOL_PAD - sft:POOL_PAD - sft + C, gs]
        cnt = jnp.minimum(pos + 1, win).astype(F32)
        pooled = ssum / cnt - xg
        os_ = slice(gi * POOL_OUT_GROUP, (gi + 1) * POOL_OUT_GROUP)
        yp = _dot(pooled, wpool_ref[gi]) * pscale_ref[:, os_]
        mixed_ref[0, :, os_] = (jax.nn.sigmoid(ga_ref[0, :, os_]) * odn_scr[:, os_]
                                + jax.nn.sigmoid(gb_ref[0, :, os_]) * yp)

    @pl.when(n == last)
    def _store_state():
        nconv_ref[0] = xp_scr[Lv + CONV_PAD - (CONV_WIDTH - 1):Lv + CONV_PAD, :]
        npool_ref[0] = pp_scr[Lv + POOL_PAD - POOL_BUF:Lv + POOL_PAD, :]
        ns_ref[0] = s_scr[...]

    xp_scr[0:CONV_PAD, :] = xp_scr[C:C + CONV_PAD, :]
    pp_scr[0:POOL_PAD, :] = pp_scr[C:C + POOL_PAD, :]


def _mixer(proj, conv_buf, s0, pool_buf, start, seq_len, C,
           conv_w, a_log, dt_bias, dn_norm_g, w_pool, pool_scale):
    b, lp, _ = proj["qkv"].shape
    nchunks = lp // C
    lv = seq_len - (nchunks - 1) * C
    cbuf = jnp.pad(conv_buf, ((0, 0), (CONV_PAD - (CONV_WIDTH - 1), 0), (0, 0)))
    pbuf = jnp.pad(pool_buf, ((0, 0), (POOL_PAD - POOL_BUF, 0), (0, 0)))
    lane_pad = lambda a: jnp.pad(a.reshape(1, -1), ((0, 0), (DN_HEADS, LANES - 2 * DN_HEADS)))
    chunk = lambda w: pl.BlockSpec((1, C, w), lambda i, j: (i, j, 0))
    state = lambda *s: pl.BlockSpec((1,) + s, lambda i, j: (i,) + (0,) * len(s))
    return pl.pallas_call(
        functools.partial(_mixer_body, C, lv, start),
        grid=(b, nchunks),
        in_specs=[chunk(QKV_WIDTH), chunk(LANES), chunk(DN_WIDTH), chunk(POOL_WIDTH),
                  chunk(D_MODEL), chunk(D_MODEL),
                  state(CONV_PAD, QKV_WIDTH), state(DN_HEADS, DN_HEAD_DIM, DN_HEAD_DIM),
                  state(POOL_PAD, POOL_WIDTH),
                  _const_spec((CONV_WIDTH, QKV_WIDTH)), _const_spec((1, LANES)), _const_spec((1, LANES)),
                  _const_spec((1, DN_HEAD_DIM)),
                  _const_spec((len(POOL_WINDOWS), POOL_GROUP_DIM, POOL_OUT_GROUP)),
                  _const_spec((1, D_MODEL))],
        out_specs=[chunk(D_MODEL), state(CONV_WIDTH - 1, QKV_WIDTH),
                   state(DN_HEADS, DN_HEAD_DIM, DN_HEAD_DIM), state(POOL_BUF, POOL_WIDTH)],
        out_shape=[jax.ShapeDtypeStruct((b, lp, D_MODEL), F32),
                   jax.ShapeDtypeStruct((b, CONV_WIDTH - 1, QKV_WIDTH), F32),
                   jax.ShapeDtypeStruct((b, DN_HEADS, DN_HEAD_DIM, DN_HEAD_DIM), F32),
                   jax.ShapeDtypeStruct((b, POOL_BUF, POOL_WIDTH), F32)],
        scratch_shapes=[pltpu.VMEM((CONV_PAD + C + CONV_PAD, QKV_WIDTH), F32),
                        pltpu.VMEM((C, QKV_WIDTH), F32),
                        pltpu.VMEM((DN_HEADS, DN_HEAD_DIM, DN_HEAD_DIM), F32),
                        pltpu.VMEM((POOL_PAD + C + POOL_PAD, POOL_WIDTH), F32),
                        pltpu.VMEM((C, DN_WIDTH), F32)],
        compiler_params=pltpu.CompilerParams(dimension_semantics=("arbitrary", "arbitrary"),
                                             vmem_limit_bytes=VMEM_LIMIT),
        name="mixer",
    )(proj["qkv"], proj["ba"], proj["z"], proj["pool"], proj["ga"], proj["gb"], cbuf, s0, pbuf,
      conv_w, lane_pad(a_log), lane_pad(dt_bias), dn_norm_g.reshape(1, -1), w_pool,
      pool_scale.reshape(1, -1))


def _top16(s, ids, payload=None):
    big = float(2 ** 24)
    vals, sel, pays = [], [], []
    for _ in range(PEER_TOPK):
        m = jnp.max(s, axis=0, keepdims=True)
        am = jnp.min(jnp.where(s == m, ids, big), axis=0, keepdims=True)
        hit = ids == am
        if payload is not None:
            pays.append(jnp.max(jnp.where(hit, payload, -1.0), axis=0, keepdims=True))
        s = jnp.where(hit, -jnp.inf, s)
        vals.append(m)
        sel.append(am)
    out = (jnp.concatenate(vals, axis=0), jnp.concatenate(sel, axis=0))
    if payload is not None:
        out += (jnp.concatenate(pays, axis=0),)
    return out


_CAND_EDGE = 4


def _post_body(has_prev, has_fin, mixed_ref, x_ref, g1_ref, sc2_ref, sh2_ref, n2g_ref, wout_ref,
               wq_ref, keys_ref, *refs):
    refs = list(refs)
    prev_in = [refs.pop(0) for _ in range(2 if has_prev else 0)]
    fin_in = [refs.pop(0) for _ in range(4 if has_fin else 0)]
    x1_ref, h2_ref, idx_ref, gate_ref = refs[:4]
    extra_out = refs[4:]
    if has_prev:
        pre_ref, pgate_ref = prev_in
        extra_out.pop(0)[...] = pgate_ref[...] * _gelu(pre_ref[...])
    if has_fin:
        _final_body(*fin_in, extra_out.pop(0))
    tm = x_ref.shape[0]
    x1 = x_ref[...] + _mod_rows(g1_ref) * _dot(mixed_ref[...], wout_ref[...])
    x1_ref[...] = x1
    y = x1 * lax.rsqrt(jnp.mean(x1 * x1, axis=-1, keepdims=True) + EPS) * n2g_ref[...]
    h2 = y * (1.0 + _mod_rows(sc2_ref)) + _mod_rows(sh2_ref)
    h2_ref[...] = h2
    q = _dot(h2, wq_ref[...])

    K = PEER_TOPK
    key_id = lax.broadcasted_iota(I32, (PEER_NKEYS, 1), 0).astype(F32)
    r16 = lax.broadcasted_iota(I32, (K, 1), 0)
    cand_id = jnp.concatenate([(a * K + r16) for a in range(_CAND_EDGE)]
                              + [(r16 * K + b) for b in range(_CAND_EDGE)], axis=0).astype(F32)
    dup = r16 < _CAND_EDGE
    idx_rows, gate_rows = [], []
    for h in range(PEER_HEADS):
        half = []
        for p in range(2):
            c0 = (h * 2 + p) * PEER_KEY_HALF
            st = _dot_nt(keys_ref[h * 2 + p], q[:, c0:c0 + PEER_KEY_HALF])
            half.append(_top16(st, key_id))
        (s1, i1), (s2, i2) = half
        cand = jnp.concatenate(
            [s1[a:a + 1] + s2 for a in range(_CAND_EDGE)]
            + [jnp.where(dup, -jnp.inf, s1 + s2[b:b + 1]) for b in range(_CAND_EDGE)], axis=0)
        cidx = jnp.concatenate(
            [i1[a:a + 1] * PEER_NKEYS + i2 for a in range(_CAND_EDGE)]
            + [i1 * PEER_NKEYS + i2[b:b + 1] for b in range(_CAND_EDGE)], axis=0)
        best, _, eidx = _top16(cand, cand_id, cidx)
        e = jnp.exp(best - best[0:1])
        gate_rows.append(e / jnp.sum(e, axis=0, keepdims=True))
        idx_rows.append(eidx)
    idx_ref[...] = jnp.concatenate(idx_rows, axis=0).T.astype(I32)
    gate_ref[...] = jnp.concatenate(gate_rows, axis=0).T


def _post(mixed2d, x2d, mod, rows_per_batch, norm2_g, w_out, w_query, keys, tm, prev=None, fin=None):
    t = x2d.shape[0]
    steps = t // tm
    row = lambda w: pl.BlockSpec((tm, w), lambda i: (i, 0))
    in_specs = [row(D_MODEL), row(D_MODEL),
                _mod_spec(2, rows_per_batch, tm), _mod_spec(4, rows_per_batch, tm),
                _mod_spec(3, rows_per_batch, tm), _const_spec((1, D_MODEL)),
                _const_spec((D_MODEL, D_MODEL)), _const_spec((D_MODEL, 2 * PEER_HEADS * PEER_KEY_HALF)),
                _const_spec((2 * PEER_HEADS, PEER_NKEYS, PEER_KEY_HALF))]
    out_specs = [row(D_MODEL), row(D_MODEL), row(PEER_HK), row(PEER_HK)]
    out_shape = [jax.ShapeDtypeStruct((t, D_MODEL), F32), jax.ShapeDtypeStruct((t, D_MODEL), F32),
                 jax.ShapeDtypeStruct((t, PEER_HK), I32), jax.ShapeDtypeStruct((t, PEER_HK), F32)]
    args = [mixed2d, x2d, mod, mod, mod, norm2_g.reshape(1, -1), w_out, w_query, keys]
    if prev is not None:
        tp = prev[0].shape[0]
        prow = pl.BlockSpec((tp // steps, PEER_HK), lambda i: (i, 0))
        in_specs += [prow, prow]
        out_specs += [prow]
        out_shape += [jax.ShapeDtypeStruct((tp, PEER_HK), F32)]
        args += list(prev)
    if fin is not None:
        x1_f, peer_f, mod_f, rows_f, final_g = fin
        tf = x1_f.shape[0]
        frow = pl.BlockSpec((tf // steps, D_MODEL), lambda i: (i, 0))
        in_specs += [frow, frow, _mod_spec(5, rows_f, tf // steps), _const_spec((1, D_MODEL))]
        out_specs += [frow]
        out_shape += [jax.ShapeDtypeStruct((tf, D_MODEL), F32)]
        args += [x1_f, peer_f, mod_f, final_g.reshape(1, -1)]
    return pl.pallas_call(
        functools.partial(_post_body, prev is not None, fin is not None),
        grid=(steps,),
        in_specs=in_specs, out_specs=out_specs, out_shape=out_shape,
        compiler_params=pltpu.CompilerParams(vmem_limit_bytes=VMEM_LIMIT),
        name="post",
    )(*args)


SC_CORES = 2
SC_SUBCORES = 16
SC_LANES = 16
SC_WORKERS = SC_CORES * SC_SUBCORES
SC_TOKENS = 16
SC_SLOTS = 4
PACK_HALF = D_MODEL // 2
SC_CHUNKS = PACK_HALF // SC_LANES
PROMPT_PARTS = 8
FIN_LAG = 3
ROW_TILE = 256


def _pack_body(x_ref, o_ref):
    bits = lambda v: lax.bitcast_convert_type(v.astype(BF16).astype(F32), jnp.uint32)
    lo = bits(x_ref[:, :PACK_HALF]) >> 16
    hi = bits(x_ref[:, PACK_HALF:])
    o_ref[...] = lax.bitcast_convert_type(lo | hi, I32)


def _pack_table(tbl, rows=512):
    e = tbl.shape[0]
    return pl.pallas_call(
        _pack_body, grid=(e // rows,),
        in_specs=[pl.BlockSpec((rows, D_MODEL), lambda i: (i, 0))],
        out_specs=pl.BlockSpec((rows, PACK_HALF), lambda i: (i, 0)),
        out_shape=jax.ShapeDtypeStruct((e, PACK_HALF), I32), name="pack_table")(tbl)


def _unpack_pair(w):
    lo = plsc.bitcast(lax.shift_left(w, jnp.full(w.shape, 16, I32)), F32)
    hi = plsc.bitcast(w & jnp.full(w.shape, -65536, I32), F32)
    return lo, hi


def _sc_mesh():
    return plsc.VectorSubcoreMesh(core_axis_name="c", subcore_axis_name="s")


def _sc_worker():
    return lax.axis_index("s") * SC_CORES + lax.axis_index("c")


def _sc_jobs(table_hbm, idx_v, buf, sem, compute):
    njobs = SC_TOKENS * PEER_HEADS

    def copy(j, slot):
        tt = j // PEER_HEADS
        h = j % PEER_HEADS
        rows = idx_v[tt, pl.ds(h * PEER_TOPK, PEER_TOPK)]
        return pltpu.make_async_copy(table_hbm.at[rows], buf.at[slot], sem.at[slot])

    for s in range(SC_SLOTS):
        copy(s, s).start()

    def group(g, c):
        for s in range(SC_SLOTS):
            j = g * SC_SLOTS + s
            copy(j, s).wait()
            compute(j // PEER_HEADS, j % PEER_HEADS, s)

            @pl.when(j + SC_SLOTS < njobs)
            def _next():
                copy(j + SC_SLOTS, s).start()
        return c

    lax.fori_loop(0, njobs // SC_SLOTS, group, 0)


def _peer_u_body(n_tok, idx_hbm, h2_hbm, u_hbm, pre_hbm, idx_v, h2_v, pre_v, ubuf, acc_v, sem):
    base = _sc_worker() * n_tok
    lane = lax.iota(I32, SC_LANES)

    def compute(tt, h, slot):
        def chunk(c, accs):
            cs = pl.ds(c * SC_LANES, SC_LANES)
            x_lo = h2_v[tt, cs]
            x_hi = h2_v[tt, pl.ds(PACK_HALF + c * SC_LANES, SC_LANES)]
            out = []
            for k, a in enumerate(accs):
                lo, hi = _unpack_pair(ubuf[slot, k, cs])
                out.append(a + (lo * x_lo + hi * x_hi))
            return tuple(out)
        zero = jnp.zeros((SC_LANES,), F32)
        accs = lax.fori_loop(0, SC_CHUNKS, chunk, (zero,) * PEER_TOPK)
        for k, a in enumerate(accs):
            acc_v[k, :] = a
        tot = zero
        for j in range(SC_LANES):
            tot = tot + plsc.load_gather(acc_v, [lane, jnp.full((SC_LANES,), j, I32)])
        pre_v[tt, pl.ds(h * PEER_TOPK, PEER_TOPK)] = tot

    def block(bi, c):
        t0 = base + bi * SC_TOKENS
        pltpu.sync_copy(idx_hbm.at[pl.ds(t0, SC_TOKENS)], idx_v)
        pltpu.sync_copy(h2_hbm.at[pl.ds(t0, SC_TOKENS)], h2_v)
        _sc_jobs(u_hbm, idx_v, ubuf, sem, compute)
        pltpu.sync_copy(pre_v, pre_hbm.at[pl.ds(t0, SC_TOKENS)])
        return c

    lax.fori_loop(0, n_tok // SC_TOKENS, block, 0)


def _peer_v_body(n_tok, idx_hbm, coef_hbm, v_hbm, out_hbm, idx_v, coef_v, out_v, vbuf, sem):
    base = _sc_worker() * n_tok
    zero = jnp.zeros((SC_LANES,), F32)

    def compute(tt, h, slot):
        row = jnp.full((SC_LANES,), tt, I32)
        cb = [plsc.load_gather(coef_v, [row, jnp.full((SC_LANES,), h * PEER_TOPK + k, I32)])
              for k in range(PEER_TOPK)]

        @plsc.parallel_loop(0, SC_CHUNKS, unroll=2)
        def _chunk(c):
            cs = pl.ds(c * SC_LANES, SC_LANES)
            pairs = [_unpack_pair(vbuf[slot, k, cs]) for k in range(PEER_TOPK)]
            for half, off in ((0, 0), (1, PACK_HALF)):
                terms = [cb[k] * pairs[k][half] for k in range(PEER_TOPK)]
                while len(terms) > 1:
                    terms = [a + b for a, b in zip(terms[0::2], terms[1::2])]
                plsc.addupdate(out_v.at[tt, pl.ds(off + c * SC_LANES, SC_LANES)], terms[0])

    def block(bi, c):
        t0 = base + bi * SC_TOKENS
        pltpu.sync_copy(idx_hbm.at[pl.ds(t0, SC_TOKENS)], idx_v)
        pltpu.sync_copy(coef_hbm.at[pl.ds(t0, SC_TOKENS)], coef_v)

        def clear(i, cc):
            per_row = D_MODEL // SC_LANES
            out_v[i // per_row, pl.ds((i % per_row) * SC_LANES, SC_LANES)] = zero
            return cc
        lax.fori_loop(0, SC_TOKENS * (D_MODEL // SC_LANES), clear, 0)
        _sc_jobs(v_hbm, idx_v, vbuf, sem, compute)
        pltpu.sync_copy(out_v, out_hbm.at[pl.ds(t0, SC_TOKENS)])
        return c

    lax.fori_loop(0, n_tok // SC_TOKENS, block, 0)


def _peer_sc(body, idx, rows, table, out_width, name):
    t = idx.shape[0]
    assert t % (SC_WORKERS * SC_TOKENS) == 0
    n_tok = t // SC_WORKERS
    return pl.kernel(
        functools.partial(body, n_tok),
        out_type=jax.ShapeDtypeStruct((t, out_width), F32),
        mesh=_sc_mesh(),
        scratch_types=[pltpu.VMEM((SC_TOKENS, PEER_HK), I32),
                       pltpu.VMEM((SC_TOKENS, rows.shape[1]), F32),
                       pltpu.VMEM((SC_TOKENS, out_width), F32),
                       pltpu.VMEM((SC_SLOTS, PEER_TOPK, PACK_HALF), I32)]
                      + ([pltpu.VMEM((PEER_TOPK, SC_LANES), F32)] if body is _peer_u_body else [])
                      + [pltpu.SemaphoreType.DMA((SC_SLOTS,))],
        compiler_params=pltpu.CompilerParams(needs_layout_passes=False),
        name=name,
    )(idx, rows, table)


def _coef_body(pre_ref, gate_ref, coef_ref):
    coef_ref[...] = gate_ref[...] * _gelu(pre_ref[...])


def _coef(pre, gates, tm):
    t = pre.shape[0]
    row = pl.BlockSpec((tm, PEER_HK), lambda i: (i, 0))
    return pl.pallas_call(_coef_body, grid=(t // tm,), in_specs=[row, row], out_specs=row,
                          out_shape=jax.ShapeDtypeStruct((t, PEER_HK), F32), name="coef")(pre, gates)


def _final_body(x1_ref, peer_ref, g2_ref, fng_ref, y_ref):
    x2 = x1_ref[...] + _mod_rows(g2_ref) * peer_ref[...]
    y_ref[...] = x2 * lax.rsqrt(jnp.mean(x2 * x2, axis=-1, keepdims=True) + EPS) * fng_ref[...]


def _final(x1, peer_out, mod, rows_per_batch, final_g, tm):
    t = x1.shape[0]
    row = pl.BlockSpec((tm, D_MODEL), lambda i: (i, 0))
    return pl.pallas_call(
        _final_body, grid=(t // tm,),
        in_specs=[row, row, _mod_spec(5, rows_per_batch, tm), _const_spec((1, D_MODEL))],
        out_specs=row, out_shape=jax.ShapeDtypeStruct((t, D_MODEL), F32), name="final",
    )(x1, peer_out, mod, final_g.reshape(1, -1))


def _expert_gather_v(g, coef, expert_v):
    g["peer_out"] = _peer_sc(_peer_v_body, g["idx"], coef, expert_v, D_MODEL, "peer_v")


def _front(x, mod, conv_buf, s0, pool_buf, start, chunk, tm, wts, prev, fin):
    b, l, _ = x.shape
    t = b * l
    x2d = x.reshape(t, D_MODEL)
    if l >= tm:
        modx = mod.reshape(b, 6, 1, D_MODEL).transpose(1, 0, 2, 3)
    else:
        modx = jnp.repeat(mod.reshape(b, 6, D_MODEL), l, axis=0).transpose(1, 0, 2)
    outs = _inproj(x2d, modx, l, wts["norm1_g"], wts["w_cat"], tm)
    lp = -(-l // chunk) * chunk
    proj = {}
    for (name, w), a in zip(_IN_BLOCKS, outs):
        a = a.reshape(b, l, w)
        proj[name] = a if lp == l else jnp.pad(a, ((0, 0), (0, lp - l), (0, 0)))
    mixed, nconv, ns, npool = _mixer(proj, conv_buf, s0, pool_buf, start, l, chunk,
                                     wts["conv_w"], wts["a_log"], wts["dt_bias"], wts["dn_norm_g"],
                                     wts["w_pool"], wts["pool_scale"])
    mixed2d = mixed[:, :l].reshape(t, D_MODEL)
    res = _post(mixed2d, x2d, modx, l, wts["norm2_g"], wts["w_out"], wts["w_query"], wts["keys"], tm,
                prev=None if prev is None else (prev["pre"], prev["gates"]),
                fin=None if fin is None else (fin["x1"], fin["peer_out"], fin["mod"], fin["l"],
                                              wts["final_norm_g"]))
    x1, h2, idx, gates = res[:4]
    extra = list(res[4:])
    coef_prev = extra.pop(0) if prev is not None else None
    y_fin = extra.pop(0).reshape(fin["b"], fin["l"], D_MODEL) if fin is not None else None
    pre = _peer_sc(_peer_u_body, idx, h2, wts["expert_u"], PEER_HK, "peer_u")
    g = dict(x1=x1, idx=idx, gates=gates, pre=pre, mod=modx, b=b, l=l, tm=tm,
             states=(nconv, ns, npool))
    return g, coef_prev, y_fin


def kernel(x_prompt, x_sample, c_prompt, c_sample, state_conv, state_delta, state_pool, w_ada, b_ada, norm1_g, w_in, conv_w, a_log, dt_bias, dn_norm_g, w_pool, pool_scale, w_out, norm2_g, w_query, sub_keys, expert_u, expert_v, final_norm_g):
    bp = x_prompt.shape[0]
    bs = x_sample.shape[0]
    yp, ys = x_prompt, x_sample
    conv_p, delta_p, pool_p, conv_s, delta_s, pool_s = [], [], [], [], [], []
    zero_conv = jnp.zeros((bp, CONV_WIDTH - 1, QKV_WIDTH), F32)
    zero_delta = jnp.zeros((bp, DN_HEADS, DN_HEAD_DIM, DN_HEAD_DIM), F32)
    zero_pool = jnp.zeros((bp, POOL_BUF, POOL_WIDTH), F32)
    c_all = jnp.concatenate([c_prompt, c_sample], axis=0)
    for layer in range(DEPTH):
        wi = w_in[layer]
        o_b = QKV_WIDTH
        o_z = o_b + 2 * DN_HEADS
        w_ba = jnp.pad(wi[:, o_b:o_z], ((0, 0), (0, LANES - 2 * DN_HEADS)))
        w_cat = jnp.concatenate([wi[:, :o_b], wi[:, o_z:], w_ba], axis=1).astype(BF16)
        last = layer == DEPTH - 1
        wts = dict(
            norm1_g=norm1_g[layer], w_cat=w_cat, conv_w=conv_w[layer], a_log=a_log[layer],
            dt_bias=dt_bias[layer], dn_norm_g=dn_norm_g[layer], w_pool=w_pool[layer],
            pool_scale=pool_scale[layer], w_out=w_out[layer].astype(BF16), norm2_g=norm2_g[layer],
            w_query=w_query[layer].astype(BF16),
            keys=sub_keys[layer].reshape(2 * PEER_HEADS, PEER_NKEYS, PEER_KEY_HALF).astype(BF16),
            expert_u=_pack_table(expert_u[layer]), expert_v=_pack_table(expert_v[layer]),
            final_norm_g=final_norm_g if last else jnp.ones_like(final_norm_g))
        mod = _ada(c_all, w_ada[layer], b_ada[layer])
        assert last, "final norm is fused into the expert stage"
        step = bp // PROMPT_PARTS
        jobs = [(yp[b0:b0 + step], mod[b0:b0 + step], zero_conv[b0:b0 + step], zero_delta[b0:b0 + step],
                 zero_pool[b0:b0 + step], 0, DN_CHUNK) for b0 in range(0, bp, step)]
        jobs.append((ys, mod[bp:], state_conv[layer], state_delta[layer], state_pool[layer],
                     PAST_LEN, SUBLANES))
        groups = []
        for j, (xg, mg, cg, sg, pg, start, chunk) in enumerate(jobs):
            prev = groups[j - 1] if j >= 1 else None
            fin = groups[j - FIN_LAG] if j >= FIN_LAG else None
            if fin is not None and fin["x1"].shape[0] != xg.shape[0] * xg.shape[1]:
                fin = None
            g, coef_prev, y_fin = _front(xg, mg, cg, sg, pg, start, chunk, ROW_TILE, wts, prev, fin)
            if prev is not None:
                _expert_gather_v(prev, coef_prev, wts["expert_v"])
            if fin is not None:
                fin["y"] = y_fin
            groups.append(g)
        _expert_gather_v(groups[-1], _coef(groups[-1]["pre"], groups[-1]["gates"], ROW_TILE),
                         wts["expert_v"])
        done = []
        for g in groups:
            if "y" not in g:
                g["y"] = _final(g["x1"], g["peer_out"], g["mod"], g["l"], wts["final_norm_g"],
                                g["tm"]).reshape(g["b"], g["l"], D_MODEL)
            done.append((g["y"],) + g["states"])
        yp, cp, sp, pp = (jnp.concatenate(a, axis=0) for a in zip(*done[:-1]))
        ys, cs, ss, ps = done[-1]
        conv_p.append(cp)
        delta_p.append(sp)
        pool_p.append(pp)
        conv_s.append(cs)
        delta_s.append(ss)
        pool_s.append(ps)
    return (yp, ys, jnp.stack(conv_p), jnp.stack(delta_p), jnp.stack(pool_p),
            jnp.stack(conv_s), jnp.stack(delta_s), jnp.stack(pool_s))
```

```python
import functools

import jax
import jax.numpy as jnp
from jax import lax
from jax.experimental import pallas as pl
from jax.experimental.pallas import tpu as pltpu
from jax.experimental.pallas import tpu_sc as plsc

F32 = jnp.float32
BF16 = jnp.bfloat16
I32 = jnp.int32

D_MODEL = 1024
DEPTH = 1
PAST_LEN = 16384
DN_HEADS = 8
DN_HEAD_DIM = 128
DN_WIDTH = DN_HEADS * DN_HEAD_DIM
QKV_WIDTH = 3 * DN_WIDTH
CONV_WIDTH = 4
DN_CHUNK = 64
POOL_WINDOWS = (2, 4, 8, 16)
POOL_GROUP_DIM = 128
POOL_WIDTH = len(POOL_WINDOWS) * POOL_GROUP_DIM
POOL_OUT_GROUP = D_MODEL // len(POOL_WINDOWS)
POOL_BUF = max(POOL_WINDOWS) - 1
PEER_HEADS = 8
PEER_NKEYS = 128
PEER_TOPK = 16
PEER_KEY_HALF = 128
PEER_HK = PEER_HEADS * PEER_TOPK
EPS = 1e-6

LANES = 128
SUBLANES = 8
CONV_PAD = SUBLANES
POOL_PAD = 16
VMEM_LIMIT = 56 * 1024 * 1024

NT_DIMS = (((1,), (1,)), ((), ()))
TN_DIMS = (((0,), (0,)), ((), ()))


def _dot(a, b):
    return jnp.dot(a.astype(BF16), b.astype(BF16), preferred_element_type=F32)


def _dot_nt(a, b):
    return lax.dot_general(a.astype(BF16), b.astype(BF16), NT_DIMS, preferred_element_type=F32)


def _split3(x):
    hi = x.astype(BF16)
    r1 = x - hi.astype(F32)
    mid = r1.astype(BF16)
    lo = (r1 - mid.astype(F32)).astype(BF16)
    return hi, mid, lo


def _silu(x):
    return x * jax.nn.sigmoid(x)


def _gelu(x):
    return 0.5 * x * (1.0 + lax.erf(x * (0.5 ** 0.5)))


def _softplus(x):
    return jnp.maximum(x, 0.0) + jnp.log(1.0 + jnp.exp(-jnp.abs(x)))


def _mod_rows(ref):
    m = ref[...]
    return m.reshape(m.shape[-2], m.shape[-1])


def _mod_spec(k, rows_per_batch, tm):
    if rows_per_batch >= tm:
        tiles = rows_per_batch // tm
        return pl.BlockSpec((1, 1, 1, D_MODEL), lambda i, *_: (k, i // tiles, 0, 0))
    return pl.BlockSpec((1, tm, D_MODEL), lambda i, *_: (k, i, 0))


def _const_spec(shape):
    nd = len(shape)
    return pl.BlockSpec(shape, lambda *_: (0,) * nd)


def _ada_body(c_ref, w_ref, b_ref, o_ref):
    o_ref[...] = _dot(_silu(c_ref[...]), w_ref[...]) + b_ref[...]


def _ada(c, w_ada, b_ada):
    n = c.shape[0]
    return pl.pallas_call(
        _ada_body,
        grid=(6,),
        in_specs=[pl.BlockSpec((n, D_MODEL), lambda j: (0, 0)),
                  pl.BlockSpec((D_MODEL, D_MODEL), lambda j: (0, j)),
                  pl.BlockSpec((1, D_MODEL), lambda j: (0, j))],
        out_specs=pl.BlockSpec((n, D_MODEL), lambda j: (0, j)),
        out_shape=jax.ShapeDtypeStruct((n, 6 * D_MODEL), F32),
        name="ada",
    )(c, w_ada, b_ada.reshape(1, -1))


_IN_BLOCKS = (("qkv", QKV_WIDTH), ("z", DN_WIDTH), ("pool", POOL_WIDTH),
              ("ga", D_MODEL), ("gb", D_MODEL), ("ba", LANES))
_IN_TOTAL = sum(w for _, w in _IN_BLOCKS)
_IN_COL_CHUNK = 512


def _inproj_body(x_ref, sc_ref, sh_ref, g_ref, w_ref, *out_refs):
    x = x_ref[...]
    y = x * lax.rsqrt(jnp.mean(x * x, axis=-1, keepdims=True) + EPS) * g_ref[...]
    h = (y * (1.0 + _mod_rows(sc_ref)) + _mod_rows(sh_ref)).astype(BF16)
    off = 0
    for (_, width), o_ref in zip(_IN_BLOCKS, out_refs):
        for c0 in range(0, width, _IN_COL_CHUNK):
            cw = min(_IN_COL_CHUNK, width - c0)
            o_ref[:, c0:c0 + cw] = jnp.dot(h, w_ref[:, off + c0:off + c0 + cw],
                                           preferred_element_type=F32)
        off += width


def _inproj(x2d, mod, rows_per_batch, norm_g, w_cat, tm):
    t = x2d.shape[0]
    row = lambda w: pl.BlockSpec((tm, w), lambda i: (i, 0))
    return pl.pallas_call(
        _inproj_body,
        grid=(t // tm,),
        in_specs=[row(D_MODEL), _mod_spec(1, rows_per_batch, tm), _mod_spec(0, rows_per_batch, tm),
                  _const_spec((1, D_MODEL)),
                  pl.BlockSpec((D_MODEL, _IN_TOTAL), lambda i: (0, 0), pipeline_mode=pl.Buffered(1))],
        out_specs=[row(w) for _, w in _IN_BLOCKS],
        out_shape=[jax.ShapeDtypeStruct((t, w), F32) for _, w in _IN_BLOCKS],
        compiler_params=pltpu.CompilerParams(vmem_limit_bytes=VMEM_LIMIT),
        name="inproj",
    )(x2d, mod, mod, norm_g.reshape(1, -1), w_cat)


def _mixer_body(C, Lv, start,
                qkv_ref, ba_ref, z_ref, pin_ref, ga_ref, gb_ref, cbuf_ref, s0_ref, pbuf_ref,
                convw_ref, alog_ref, dtb_ref, dng_ref, wpool_ref, pscale_ref,
                mixed_ref, nconv_ref, ns_ref, npool_ref,
                xp_scr, act_scr, s_scr, pp_scr, odn_scr):
    n = pl.program_id(1)
    last = pl.num_programs(1) - 1

    @pl.when(n == 0)
    def _load_state():
        xp_scr[0:CONV_PAD, :] = cbuf_ref[0]
        pp_scr[0:POOL_PAD, :] = pbuf_ref[0]
        s_scr[...] = s0_ref[0]

    xp_scr[CONV_PAD:CONV_PAD + C, :] = qkv_ref[0]
    for c0 in range(0, QKV_WIDTH, 512):
        cs = slice(c0, c0 + 512)
        y = xp_scr[CONV_PAD:CONV_PAD + C, cs] * convw_ref[CONV_WIDTH - 1:CONV_WIDTH, cs]
        for k in range(CONV_WIDTH - 1):
            r0 = CONV_PAD - (CONV_WIDTH - 1) + k
            y = y + xp_scr[r0:r0 + C, cs] * convw_ref[k:k + 1, cs]
        act_scr[:, cs] = _silu(y)

    ba = ba_ref[0]
    lane = lax.broadcasted_iota(I32, (C, LANES), 1)
    beta_all = jax.nn.sigmoid(ba)
    g_all = -jnp.exp(alog_ref[...]) * _softplus(ba + dtb_ref[...])
    if Lv < C:
        valid = lax.broadcasted_iota(I32, (C, LANES), 0) < Lv
        beta_all = jnp.where(valid, beta_all, 0.0)
        g_all = jnp.where(valid, g_all, 0.0)
    ii = lax.broadcasted_iota(I32, (C, C), 0)
    jj = lax.broadcasted_iota(I32, (C, C), 1)
    causal = ii >= jj
    strict = ii > jj
    tril = jnp.where(causal, 1.0, 0.0).astype(BF16)
    eye = jnp.where(ii == jj, 1.0, 0.0)
    gc_all = sum(jnp.dot(tril, part, preferred_element_type=F32) for part in _split3(g_all))
    if C < LANES:
        gc_sq = jnp.concatenate([gc_all, jnp.zeros((LANES - C, LANES), F32)], axis=0)
    else:
        gc_sq = gc_all
    gc_t = gc_sq.T

    for h in range(DN_HEADS):
        hs = slice(h * DN_HEAD_DIM, (h + 1) * DN_HEAD_DIM)
        beta = jnp.sum(jnp.where(lane == h, beta_all, 0.0), axis=1, keepdims=True)
        gcol = jnp.sum(jnp.where(lane == DN_HEADS + h, gc_all, 0.0), axis=1, keepdims=True)
        grow = gc_t[DN_HEADS + h:DN_HEADS + h + 1, 0:C]
        glast = gcol[C - 1:C, :]

        q = act_scr[:, hs]
        k = act_scr[:, DN_WIDTH + h * DN_HEAD_DIM:DN_WIDTH + (h + 1) * DN_HEAD_DIM]
        v = act_scr[:, 2 * DN_WIDTH + h * DN_HEAD_DIM:2 * DN_WIDTH + (h + 1) * DN_HEAD_DIM]
        q = q * lax.rsqrt(jnp.sum(q * q, axis=-1, keepdims=True) + EPS) * (DN_HEAD_DIM ** -0.5)
        k = k * lax.rsqrt(jnp.sum(k * k, axis=-1, keepdims=True) + EPS)
        kb = k * beta
        vb = v * beta

        decay = jnp.where(causal, jnp.exp(jnp.where(causal, gcol - grow, 0.0)), 0.0)
        lower = jnp.where(strict, _dot_nt(kb, k) * decay, 0.0)
        ainv = eye - lower
        pw = lower
        p = 1
        while 2 * p < C:
            pw = _dot(pw, pw)
            ainv = ainv + _dot(ainv, pw)
            p *= 2
        sol = _dot(ainv, jnp.concatenate([vb, kb * jnp.exp(gcol)], axis=1))
        u = sol[:, :DN_HEAD_DIM]
        w = sol[:, DN_HEAD_DIM:]
        qk = _dot_nt(q, k) * decay
        k_tail = k * jnp.exp(glast - gcol)

        S = s_scr[h]
        v_new = u - _dot(w, S)
        o = _dot(q * jnp.exp(gcol), S) + _dot(qk, v_new)
        s_scr[h] = S * jnp.exp(glast) + lax.dot_general(
            k_tail.astype(BF16), v_new.astype(BF16), TN_DIMS, preferred_element_type=F32)

        zf = z_ref[0, :, hs]
        o = o * lax.rsqrt(jnp.mean(o * o, axis=-1, keepdims=True) + EPS) * dng_ref[...] * _silu(zf)
        odn_scr[:, hs] = o

    pp_scr[POOL_PAD:POOL_PAD + C, :] = pin_ref[0]
    pos = start + n * C + lax.broadcasted_iota(I32, (C, 1), 0)
    for gi, win in enumerate(POOL_WINDOWS):
        gs = slice(gi * POOL_GROUP_DIM, (gi + 1) * POOL_GROUP_DIM)
        xg = pp_scr[POOL_PAD:POOL_PAD + C, gs]
        ssum = xg
        for sft in range(1, win):
            ssum = ssum + pp_scr[POOL_PAD - sft:POOL_PAD - sft + C, gs]
        cnt = jnp.minimum(pos + 1, win).astype(F32)
        pooled = ssum / cnt - xg
        os_ = slice(gi * POOL_OUT_GROUP, (gi + 1) * POOL_OUT_GROUP)
        yp = _dot(pooled, wpool_ref[gi]) * pscale_ref[:, os_]
        mixed_ref[0, :, os_] = (jax.nn.sigmoid(ga_ref[0, :, os_]) * odn_scr[:, os_]
                                + jax.nn.sigmoid(gb_ref[0, :, os_]) * yp)

    @pl.when(n == last)
    def _store_state():
        nconv_ref[0] = xp_scr[Lv + CONV_PAD - (CONV_WIDTH - 1):Lv + CONV_PAD, :]
        npool_ref[0] = pp_scr[Lv + POOL_PAD - POOL_BUF:Lv + POOL_PAD, :]
        ns_ref[0] = s_scr[...]

    xp_scr[0:CONV_PAD, :] = xp_scr[C:C + CONV_PAD, :]
    pp_scr[0:POOL_PAD, :] = pp_scr[C:C + POOL_PAD, :]


def _mixer(proj, conv_buf, s0, pool_buf, start, seq_len, C,
           conv_w, a_log, dt_bias, dn_norm_g, w_pool, pool_scale):
    b, lp, _ = proj["qkv"].shape
    nchunks = lp // C
    lv = seq_len - (nchunks - 1) * C
    cbuf = jnp.pad(conv_buf, ((0, 0), (CONV_PAD - (CONV_WIDTH - 1), 0), (0, 0)))
    pbuf = jnp.pad(pool_buf, ((0, 0), (POOL_PAD - POOL_BUF, 0), (0, 0)))
    lane_pad = lambda a: jnp.pad(a.reshape(1, -1), ((0, 0), (DN_HEADS, LANES - 2 * DN_HEADS)))
    chunk = lambda w: pl.BlockSpec((1, C, w), lambda i, j: (i, j, 0))
    state = lambda *s: pl.BlockSpec((1,) + s, lambda i, j: (i,) + (0,) * len(s))
    return pl.pallas_call(
        functools.partial(_mixer_body, C, lv, start),
        grid=(b, nchunks),
        in_specs=[chunk(QKV_WIDTH), chunk(LANES), chunk(DN_WIDTH), chunk(POOL_WIDTH),
                  chunk(D_MODEL), chunk(D_MODEL),
                  state(CONV_PAD, QKV_WIDTH), state(DN_HEADS, DN_HEAD_DIM, DN_HEAD_DIM),
                  state(POOL_PAD, POOL_WIDTH),
                  _const_spec((CONV_WIDTH, QKV_WIDTH)), _const_spec((1, LANES)), _const_spec((1, LANES)),
                  _const_spec((1, DN_HEAD_DIM)),
                  _const_spec((len(POOL_WINDOWS), POOL_GROUP_DIM, POOL_OUT_GROUP)),
                  _const_spec((1, D_MODEL))],
        out_specs=[chunk(D_MODEL), state(CONV_WIDTH - 1, QKV_WIDTH),
                   state(DN_HEADS, DN_HEAD_DIM, DN_HEAD_DIM), state(POOL_BUF, POOL_WIDTH)],
        out_shape=[jax.ShapeDtypeStruct((b, lp, D_MODEL), F32),
                   jax.ShapeDtypeStruct((b, CONV_WIDTH - 1, QKV_WIDTH), F32),
                   jax.ShapeDtypeStruct((b, DN_HEADS, DN_HEAD_DIM, DN_HEAD_DIM), F32),
                   jax.ShapeDtypeStruct((b, POOL_BUF, POOL_WIDTH), F32)],
        scratch_shapes=[pltpu.VMEM((CONV_PAD + C + CONV_PAD, QKV_WIDTH), F32),
                        pltpu.VMEM((C, QKV_WIDTH), F32),
                        pltpu.VMEM((DN_HEADS, DN_HEAD_DIM, DN_HEAD_DIM), F32),
                        pltpu.VMEM((POOL_PAD + C + POOL_PAD, POOL_WIDTH), F32),
                        pltpu.VMEM((C, DN_WIDTH), F32)],
        compiler_params=pltpu.CompilerParams(dimension_semantics=("arbitrary", "arbitrary"),
                                             vmem_limit_bytes=VMEM_LIMIT),
        name="mixer",
    )(proj["qkv"], proj["ba"], proj["z"], proj["pool"], proj["ga"], proj["gb"], cbuf, s0, pbuf,
      conv_w, lane_pad(a_log), lane_pad(dt_bias), dn_norm_g.reshape(1, -1), w_pool,
      pool_scale.reshape(1, -1))


def _top16(s, ids, payload=None):
    big = float(2 ** 24)
    vals, sel, pays = [], [], []
    for _ in range(PEER_TOPK):
        m = jnp.max(s, axis=0, keepdims=True)
        am = jnp.min(jnp.where(s == m, ids, big), axis=0, keepdims=True)
        hit = ids == am
        if payload is not None:
            pays.append(jnp.max(jnp.where(hit, payload, -1.0), axis=0, keepdims=True))
        s = jnp.where(hit, -jnp.inf, s)
        vals.append(m)
        sel.append(am)
    out = (jnp.concatenate(vals, axis=0), jnp.concatenate(sel, axis=0))
    if payload is not None:
        out += (jnp.concatenate(pays, axis=0),)
    return out


_CAND_EDGE = 4


def _post_body(has_prev, has_fin, mixed_ref, x_ref, g1_ref, sc2_ref, sh2_ref, n2g_ref, wout_ref,
               wq_ref, keys_ref, *refs):
    refs = list(refs)
    prev_in = [refs.pop(0) for _ in range(2 if has_prev else 0)]
    fin_in = [refs.pop(0) for _ in range(4 if has_fin else 0)]
    x1_ref, h2_ref, idx_ref, gate_ref = refs[:4]
    extra_out = refs[4:]
    if has_prev:
        pre_ref, pgate_ref = prev_in
        extra_out.pop(0)[...] = pgate_ref[...] * _gelu(pre_ref[...])
    if has_fin:
        _final_body(*fin_in, extra_out.pop(0))
    tm = x_ref.shape[0]
    x1 = x_ref[...] + _mod_rows(g1_ref) * _dot(mixed_ref[...], wout_ref[...])
    x1_ref[...] = x1
    y = x1 * lax.rsqrt(jnp.mean(x1 * x1, axis=-1, keepdims=True) + EPS) * n2g_ref[...]
    h2 = y * (1.0 + _mod_rows(sc2_ref)) + _mod_rows(sh2_ref)
    h2_ref[...] = h2
    q = _dot(h2, wq_ref[...])

    K = PEER_TOPK
    key_id = lax.broadcasted_iota(I32, (PEER_NKEYS, 1), 0).astype(F32)
    r16 = lax.broadcasted_iota(I32, (K, 1), 0)
    cand_id = jnp.concatenate([(a * K + r16) for a in range(_CAND_EDGE)]
                              + [(r16 * K + b) for b in range(_CAND_EDGE)], axis=0).astype(F32)
    dup = r16 < _CAND_EDGE
    idx_rows, gate_rows = [], []
    for h in range(PEER_HEADS):
        half = []
        for p in range(2):
            c0 = (h * 2 + p) * PEER_KEY_HALF
            st = _dot_nt(keys_ref[h * 2 + p], q[:, c0:c0 + PEER_KEY_HALF])
            half.append(_top16(st, key_id))
        (s1, i1), (s2, i2) = half
        cand = jnp.concatenate(
            [s1[a:a + 1] + s2 for a in range(_CAND_EDGE)]
            + [jnp.where(dup, -jnp.inf, s1 + s2[b:b + 1]) for b in range(_CAND_EDGE)], axis=0)
        cidx = jnp.concatenate(
            [i1[a:a + 1] * PEER_NKEYS + i2 for a in range(_CAND_EDGE)]
            + [i1 * PEER_NKEYS + i2[b:b + 1] for b in range(_CAND_EDGE)], axis=0)
        best, _, eidx = _top16(cand, cand_id, cidx)
        e = jnp.exp(best - best[0:1])
        gate_rows.append(e / jnp.sum(e, axis=0, keepdims=True))
        idx_rows.append(eidx)
    idx_ref[...] = jnp.concatenate(idx_rows, axis=0).T.astype(I32)
    gate_ref[...] = jnp.concatenate(gate_rows, axis=0).T


def _post(mixed2d, x2d, mod, rows_per_batch, norm2_g, w_out, w_query, keys, tm, prev=None, fin=None):
    t = x2d.shape[0]
    steps = t // tm
    row = lambda w: pl.BlockSpec((tm, w), lambda i: (i, 0))
    in_specs = [row(D_MODEL), row(D_MODEL),
                _mod_spec(2, rows_per_batch, tm), _mod_spec(4, rows_per_batch, tm),
                _mod_spec(3, rows_per_batch, tm), _const_spec((1, D_MODEL)),
                _const_spec((D_MODEL, D_MODEL)), _const_spec((D_MODEL, 2 * PEER_HEADS * PEER_KEY_HALF)),
                _const_spec((2 * PEER_HEADS, PEER_NKEYS, PEER_KEY_HALF))]
    out_specs = [row(D_MODEL), row(D_MODEL), row(PEER_HK), row(PEER_HK)]
    out_shape = [jax.ShapeDtypeStruct((t, D_MODEL), F32), jax.ShapeDtypeStruct((t, D_MODEL), F32),
                 jax.ShapeDtypeStruct((t, PEER_HK), I32), jax.ShapeDtypeStruct((t, PEER_HK), F32)]
    args = [mixed2d, x2d, mod, mod, mod, norm2_g.reshape(1, -1), w_out, w_query, keys]
    if prev is not None:
        tp = prev[0].shape[0]
        prow = pl.BlockSpec((tp // steps, PEER_HK), lambda i: (i, 0))
        in_specs += [prow, prow]
        out_specs += [prow]
        out_shape += [jax.ShapeDtypeStruct((tp, PEER_HK), F32)]
        args += list(prev)
    if fin is not None:
        x1_f, peer_f, mod_f, rows_f, final_g = fin
        tf = x1_f.shape[0]
        frow = pl.BlockSpec((tf // steps, D_MODEL), lambda i: (i, 0))
        in_specs += [frow, frow, _mod_spec(5, rows_f, tf // steps), _const_spec((1, D_MODEL))]
        out_specs += [frow]
        out_shape += [jax.ShapeDtypeStruct((tf, D_MODEL), F32)]
        args += [x1_f, peer_f, mod_f, final_g.reshape(1, -1)]
    return pl.pallas_call(
        functools.partial(_post_body, prev is not None, fin is not None),
        grid=(steps,),
        in_specs=in_specs, out_specs=out_specs, out_shape=out_shape,
        compiler_params=pltpu.CompilerParams(vmem_limit_bytes=VMEM_LIMIT),
        name="post",
    )(*args)


SC_CORES = 2
SC_SUBCORES = 16
SC_LANES = 16
SC_WORKERS = SC_CORES * SC_SUBCORES
SC_TOKENS = 32
SC_SLOTS = 8
PACK_HALF = D_MODEL // 2
SC_CHUNKS = PACK_HALF // SC_LANES
PROMPT_PARTS = 8
FIN_LAG = 3
ROW_TILE = 256


def _pack_body(x_ref, o_ref):
    bits = lambda v: lax.bitcast_convert_type(v.astype(BF16).astype(F32), jnp.uint32)
    lo = bits(x_ref[:, :PACK_HALF]) >> 16
    hi = bits(x_ref[:, PACK_HALF:])
    o_ref[...] = lax.bitcast_convert_type(lo | hi, I32)


def _pack_table(tbl, rows=512):
    e = tbl.shape[0]
    return pl.pallas_call(
        _pack_body, grid=(e // rows,),
        in_specs=[pl.BlockSpec((rows, D_MODEL), lambda i: (i, 0))],
        out_specs=pl.BlockSpec((rows, PACK_HALF), lambda i: (i, 0)),
        out_shape=jax.ShapeDtypeStruct((e, PACK_HALF), I32), name="pack_table")(tbl)


def _unpack_pair(w):
    lo = plsc.bitcast(lax.shift_left(w, jnp.full(w.shape, 16, I32)), F32)
    hi = plsc.bitcast(w & jnp.full(w.shape, -65536, I32), F32)
    return lo, hi


def _sc_mesh():
    return plsc.VectorSubcoreMesh(core_axis_name="c", subcore_axis_name="s")


def _sc_worker():
    return lax.axis_index("s") * SC_CORES + lax.axis_index("c")


def _sc_jobs(table_hbm, idx_v, buf, sem, compute):
    njobs = idx_v.shape[0] * PEER_HEADS

    def copy(j, slot):
        tt = j // PEER_HEADS
        h = j % PEER_HEADS
        rows = idx_v[tt, pl.ds(h * PEER_TOPK, PEER_TOPK)]
        return pltpu.make_async_copy(table_hbm.at[rows], buf.at[slot], sem.at[slot])

    for s in range(SC_SLOTS):
        copy(s, s).start()

    def group(g, c):
        for s in range(SC_SLOTS):
            j = g * SC_SLOTS + s
            copy(j, s).wait()
            compute(j // PEER_HEADS, j % PEER_HEADS, s)

            @pl.when(j + SC_SLOTS < njobs)
            def _next():
                copy(j + SC_SLOTS, s).start()
        return c

    lax.fori_loop(0, njobs // SC_SLOTS, group, 0)


def _peer_u_body(n_tok, idx_hbm, h2_hbm, u_hbm, pre_hbm, idx_v, h2_v, pre_v, ubuf, acc_v, sem):
    base = _sc_worker() * n_tok
    lane = lax.iota(I32, SC_LANES)

    def compute(tt, h, slot):
        def chunk(c, accs):
            cs = pl.ds(c * SC_LANES, SC_LANES)
            x_lo = h2_v[tt, cs]
            x_hi = h2_v[tt, pl.ds(PACK_HALF + c * SC_LANES, SC_LANES)]
            out = []
            for k, a in enumerate(accs):
                lo, hi = _unpack_pair(ubuf[slot, k, cs])
                out.append(a + (lo * x_lo + hi * x_hi))
            return tuple(out)
        zero = jnp.zeros((SC_LANES,), F32)
        accs = lax.fori_loop(0, SC_CHUNKS, chunk, (zero,) * PEER_TOPK)
        for k, a in enumerate(accs):
            acc_v[k, :] = a
        tot = zero
        for j in range(SC_LANES):
            tot = tot + plsc.load_gather(acc_v, [lane, jnp.full((SC_LANES,), j, I32)])
        pre_v[tt, pl.ds(h * PEER_TOPK, PEER_TOPK)] = tot

    tb = idx_v.shape[0]

    def block(bi, c):
        t0 = base + bi * tb
        pltpu.sync_copy(idx_hbm.at[pl.ds(t0, tb)], idx_v)
        pltpu.sync_copy(h2_hbm.at[pl.ds(t0, tb)], h2_v)
        _sc_jobs(u_hbm, idx_v, ubuf, sem, compute)
        pltpu.sync_copy(pre_v, pre_hbm.at[pl.ds(t0, tb)])
        return c

    lax.fori_loop(0, n_tok // tb, block, 0)


def _peer_v_body(n_tok, idx_hbm, coef_hbm, v_hbm, out_hbm, idx_v, coef_v, out_v, vbuf, sem):
    base = _sc_worker() * n_tok
    zero = jnp.zeros((SC_LANES,), F32)

    def compute(tt, h, slot):
        row = jnp.full((SC_LANES,), tt, I32)
        cb = [plsc.load_gather(coef_v, [row, jnp.full((SC_LANES,), h * PEER_TOPK + k, I32)])
              for k in range(PEER_TOPK)]

        @plsc.parallel_loop(0, SC_CHUNKS, unroll=2)
        def _chunk(c):
            cs = pl.ds(c * SC_LANES, SC_LANES)
            pairs = [_unpack_pair(vbuf[slot, k, cs]) for k in range(PEER_TOPK)]
            for half, off in ((0, 0), (1, PACK_HALF)):
                terms = [cb[k] * pairs[k][half] for k in range(PEER_TOPK)]
                while len(terms) > 1:
                    terms = [a + b for a, b in zip(terms[0::2], terms[1::2])]
                plsc.addupdate(out_v.at[tt, pl.ds(off + c * SC_LANES, SC_LANES)], terms[0])

    tb = idx_v.shape[0]

    def block(bi, c):
        t0 = base + bi * tb
        pltpu.sync_copy(idx_hbm.at[pl.ds(t0, tb)], idx_v)
        pltpu.sync_copy(coef_hbm.at[pl.ds(t0, tb)], coef_v)

        def clear(i, cc):
            per_row = D_MODEL // SC_LANES
            out_v[i // per_row, pl.ds((i % per_row) * SC_LANES, SC_LANES)] = zero
            return cc
        lax.fori_loop(0, tb * (D_MODEL // SC_LANES), clear, 0)
        _sc_jobs(v_hbm, idx_v, vbuf, sem, compute)
        pltpu.sync_copy(out_v, out_hbm.at[pl.ds(t0, tb)])
        return c

    lax.fori_loop(0, n_tok // tb, block, 0)


def _peer_sc(body, idx, rows, table, out_width, name):
    t = idx.shape[0]
    assert t % SC_WORKERS == 0
    n_tok = t // SC_WORKERS
    tb = min(SC_TOKENS, n_tok)
    assert n_tok % tb == 0 and (tb * PEER_HEADS) % SC_SLOTS == 0
    return pl.kernel(
        functools.partial(body, n_tok),
        out_type=jax.ShapeDtypeStruct((t, out_width), F32),
        mesh=_sc_mesh(),
        scratch_types=[pltpu.VMEM((tb, PEER_HK), I32),
                       pltpu.VMEM((tb, rows.shape[1]), F32),
                       pltpu.VMEM((tb, out_width), F32),
                       pltpu.VMEM((SC_SLOTS, PEER_TOPK, PACK_HALF), I32)]
                      + ([pltpu.VMEM((PEER_TOPK, SC_LANES), F32)] if body is _peer_u_body else [])
                      + [pltpu.SemaphoreType.DMA((SC_SLOTS,))],
        compiler_params=pltpu.CompilerParams(needs_layout_passes=False),
        name=name,
    )(idx, rows, table)


def _coef_body(pre_ref, gate_ref, coef_ref):
    coef_ref[...] = gate_ref[...] * _gelu(pre_ref[...])


def _coef(pre, gates, tm):
    t = pre.shape[0]
    row = pl.BlockSpec((tm, PEER_HK), lambda i: (i, 0))
    return pl.pallas_call(_coef_body, grid=(t // tm,), in_specs=[row, row], out_specs=row,
                          out_shape=jax.ShapeDtypeStruct((t, PEER_HK), F32), name="coef")(pre, gates)


def _final_body(x1_ref, peer_ref, g2_ref, fng_ref, y_ref):
    x2 = x1_ref[...] + _mod_rows(g2_ref) * peer_ref[...]
    y_ref[...] = x2 * lax.rsqrt(jnp.mean(x2 * x2, axis=-1, keepdims=True) + EPS) * fng_ref[...]


def _final(x1, peer_out, mod, rows_per_batch, final_g, tm):
    t = x1.shape[0]
    row = pl.BlockSpec((tm, D_MODEL), lambda i: (i, 0))
    return pl.pallas_call(
        _final_body, grid=(t // tm,),
        in_specs=[row, row, _mod_spec(5, rows_per_batch, tm), _const_spec((1, D_MODEL))],
        out_specs=row, out_shape=jax.ShapeDtypeStruct((t, D_MODEL), F32), name="final",
    )(x1, peer_out, mod, final_g.reshape(1, -1))


def _expert_gather_v(g, coef, expert_v):
    g["peer_out"] = _peer_sc(_peer_v_body, g["idx"], coef, expert_v, D_MODEL, "peer_v")


def _front(x, mod, conv_buf, s0, pool_buf, start, chunk, tm, wts, prev, fin):
    b, l, _ = x.shape
    t = b * l
    x2d = x.reshape(t, D_MODEL)
    if l >= tm:
        modx = mod.reshape(b, 6, 1, D_MODEL).transpose(1, 0, 2, 3)
    else:
        modx = jnp.repeat(mod.reshape(b, 6, D_MODEL), l, axis=0).transpose(1, 0, 2)
    outs = _inproj(x2d, modx, l, wts["norm1_g"], wts["w_cat"], tm)
    lp = -(-l // chunk) * chunk
    proj = {}
    for (name, w), a in zip(_IN_BLOCKS, outs):
        a = a.reshape(b, l, w)
        proj[name] = a if lp == l else jnp.pad(a, ((0, 0), (0, lp - l), (0, 0)))
    mixed, nconv, ns, npool = _mixer(proj, conv_buf, s0, pool_buf, start, l, chunk,
                                     wts["conv_w"], wts["a_log"], wts["dt_bias"], wts["dn_norm_g"],
                                     wts["w_pool"], wts["pool_scale"])
    mixed2d = mixed[:, :l].reshape(t, D_MODEL)
    res = _post(mixed2d, x2d, modx, l, wts["norm2_g"], wts["w_out"], wts["w_query"], wts["keys"], tm,
                prev=None if prev is None else (prev["pre"], prev["gates"]),
                fin=None if fin is None else (fin["x1"], fin["peer_out"], fin["mod"], fin["l"],
                                              wts["final_norm_g"]))
    x1, h2, idx, gates = res[:4]
    extra = list(res[4:])
    coef_prev = extra.pop(0) if prev is not None else None
    y_fin = extra.pop(0).reshape(fin["b"], fin["l"], D_MODEL) if fin is not None else None
    pre = _peer_sc(_peer_u_body, idx, h2, wts["expert_u"], PEER_HK, "peer_u")
    g = dict(x1=x1, idx=idx, gates=gates, pre=pre, mod=modx, b=b, l=l, tm=tm,
             states=(nconv, ns, npool))
    return g, coef_prev, y_fin


def kernel(x_prompt, x_sample, c_prompt, c_sample, state_conv, state_delta, state_pool, w_ada, b_ada, norm1_g, w_in, conv_w, a_log, dt_bias, dn_norm_g, w_pool, pool_scale, w_out, norm2_g, w_query, sub_keys, expert_u, expert_v, final_norm_g):
    bp = x_prompt.shape[0]
    bs = x_sample.shape[0]
    yp, ys = x_prompt, x_sample
    conv_p, delta_p, pool_p, conv_s, delta_s, pool_s = [], [], [], [], [], []
    zero_conv = jnp.zeros((bp, CONV_WIDTH - 1, QKV_WIDTH), F32)
    zero_delta = jnp.zeros((bp, DN_HEADS, DN_HEAD_DIM, DN_HEAD_DIM), F32)
    zero_pool = jnp.zeros((bp, POOL_BUF, POOL_WIDTH), F32)
    c_all = jnp.concatenate([c_prompt, c_sample], axis=0)
    for layer in range(DEPTH):
        wi = w_in[layer]
        o_b = QKV_WIDTH
        o_z = o_b + 2 * DN_HEADS
        w_ba = jnp.pad(wi[:, o_b:o_z], ((0, 0), (0, LANES - 2 * DN_HEADS)))
        w_cat = jnp.concatenate([wi[:, :o_b], wi[:, o_z:], w_ba], axis=1).astype(BF16)
        last = layer == DEPTH - 1
        wts = dict(
            norm1_g=norm1_g[layer], w_cat=w_cat, conv_w=conv_w[layer], a_log=a_log[layer],
            dt_bias=dt_bias[layer], dn_norm_g=dn_norm_g[layer], w_pool=w_pool[layer],
            pool_scale=pool_scale[layer], w_out=w_out[layer].astype(BF16), norm2_g=norm2_g[layer],
            w_query=w_query[layer].astype(BF16),
            keys=sub_keys[layer].reshape(2 * PEER_HEADS, PEER_NKEYS, PEER_KEY_HALF).astype(BF16),
            expert_u=_pack_table(expert_u[layer]), expert_v=_pack_table(expert_v[layer]),
            final_norm_g=final_norm_g if last else jnp.ones_like(final_norm_g))
        mod = _ada(c_all, w_ada[layer], b_ada[layer])
        assert last, "final norm is fused into the expert stage"
        step = bp // PROMPT_PARTS
        jobs = [(yp[b0:b0 + step], mod[b0:b0 + step], zero_conv[b0:b0 + step], zero_delta[b0:b0 + step],
                 zero_pool[b0:b0 + step], 0, DN_CHUNK) for b0 in range(0, bp, step)]
        jobs.append((ys, mod[bp:], state_conv[layer], state_delta[layer], state_pool[layer],
                     PAST_LEN, SUBLANES))
        groups = []
        for j, (xg, mg, cg, sg, pg, start, chunk) in enumerate(jobs):
            prev = groups[j - 1] if j >= 1 else None
            fin = groups[j - FIN_LAG] if j >= FIN_LAG else None
            if fin is not None and fin["x1"].shape[0] != xg.shape[0] * xg.shape[1]:
                fin = None
            g, coef_prev, y_fin = _front(xg, mg, cg, sg, pg, start, chunk, ROW_TILE, wts, prev, fin)
            if prev is not None:
                _expert_gather_v(prev, coef_prev, wts["expert_v"])
            if fin is not None:
                fin["y"] = y_fin
            groups.append(g)
        _expert_gather_v(groups[-1], _coef(groups[-1]["pre"], groups[-1]["gates"], ROW_TILE),
                         wts["expert_v"])
        done = []
        for g in groups:
            if "y" not in g:
                g["y"] = _final(g["x1"], g["peer_out"], g["mod"], g["l"], wts["final_norm_g"],
                                g["tm"]).reshape(g["b"], g["l"], D_MODEL)
            done.append((g["y"],) + g["states"])
        yp, cp, sp, pp = (jnp.concatenate(a, axis=0) for a in zip(*done[:-1]))
        ys, cs, ss, ps = done[-1]
        conv_p.append(cp)
        delta_p.append(sp)
        pool_p.append(pp)
        conv_s.append(cs)
        delta_s.append(ss)
        pool_s.append(ps)
    return (yp, ys, jnp.stack(conv_p), jnp.stack(delta_p), jnp.stack(pool_p),
            jnp.stack(conv_s), jnp.stack(delta_s), jnp.stack(pool_s))
```

```python
import functools

import jax
import jax.numpy as jnp
from jax import lax
from jax.experimental import pallas as pl
from jax.experimental.pallas import tpu as pltpu
from jax.experimental.pallas import tpu_sc as plsc

F32 = jnp.float32
BF16 = jnp.bfloat16
I32 = jnp.int32

D_MODEL = 1024
DEPTH = 1
PAST_LEN = 16384
DN_HEADS = 8
DN_HEAD_DIM = 128
DN_WIDTH = DN_HEADS * DN_HEAD_DIM
QKV_WIDTH = 3 * DN_WIDTH
CONV_WIDTH = 4
DN_CHUNK = 64
POOL_WINDOWS = (2, 4, 8, 16)
POOL_GROUP_DIM = 128
POOL_WIDTH = len(POOL_WINDOWS) * POOL_GROUP_DIM
POOL_OUT_GROUP = D_MODEL // len(POOL_WINDOWS)
POOL_BUF = max(POOL_WINDOWS) - 1
PEER_HEADS = 8
PEER_NKEYS = 128
PEER_TOPK = 16
PEER_KEY_HALF = 128
PEER_HK = PEER_HEADS * PEER_TOPK
EPS = 1e-6

LANES = 128
SUBLANES = 8
CONV_PAD = SUBLANES
POOL_PAD = 16
VMEM_LIMIT = 56 * 1024 * 1024

NT_DIMS = (((1,), (1,)), ((), ()))
TN_DIMS = (((0,), (0,)), ((), ()))


def _dot(a, b):
    return jnp.dot(a.astype(BF16), b.astype(BF16), preferred_element_type=F32)


def _dot_nt(a, b):
    return lax.dot_general(a.astype(BF16), b.astype(BF16), NT_DIMS, preferred_element_type=F32)


def _split3(x):
    hi = x.astype(BF16)
    r1 = x - hi.astype(F32)
    mid = r1.astype(BF16)
    lo = (r1 - mid.astype(F32)).astype(BF16)
    return hi, mid, lo


def _silu(x):
    return x * jax.nn.sigmoid(x)


def _gelu(x):
    return 0.5 * x * (1.0 + lax.erf(x * (0.5 ** 0.5)))


def _softplus(x):
    return jnp.maximum(x, 0.0) + jnp.log(1.0 + jnp.exp(-jnp.abs(x)))


def _mod_rows(ref):
    m = ref[...]
    return m.reshape(m.shape[-2], m.shape[-1])


def _mod_spec(k, rows_per_batch, tm):
    if rows_per_batch >= tm:
        tiles = rows_per_batch // tm
        return pl.BlockSpec((1, 1, 1, D_MODEL), lambda i, *_: (k, i // tiles, 0, 0))
    return pl.BlockSpec((1, tm, D_MODEL), lambda i, *_: (k, i, 0))


def _const_spec(shape):
    nd = len(shape)
    return pl.BlockSpec(shape, lambda *_: (0,) * nd)


def _ada_body(c_ref, w_ref, b_ref, o_ref):
    o_ref[...] = _dot(_silu(c_ref[...]), w_ref[...]) + b_ref[...]


def _ada(c, w_ada, b_ada):
    n = c.shape[0]
    return pl.pallas_call(
        _ada_body,
        grid=(6,),
        in_specs=[pl.BlockSpec((n, D_MODEL), lambda j: (0, 0)),
                  pl.BlockSpec((D_MODEL, D_MODEL), lambda j: (0, j)),
                  pl.BlockSpec((1, D_MODEL), lambda j: (0, j))],
        out_specs=pl.BlockSpec((n, D_MODEL), lambda j: (0, j)),
        out_shape=jax.ShapeDtypeStruct((n, 6 * D_MODEL), F32),
        name="ada",
    )(c, w_ada, b_ada.reshape(1, -1))


_IN_BLOCKS = (("qkv", QKV_WIDTH), ("z", DN_WIDTH), ("pool", POOL_WIDTH),
              ("ga", D_MODEL), ("gb", D_MODEL), ("ba", LANES))
_IN_TOTAL = sum(w for _, w in _IN_BLOCKS)
_IN_COL_CHUNK = 512


def _inproj_body(x_ref, sc_ref, sh_ref, g_ref, w_ref, *out_refs):
    x = x_ref[...]
    y = x * lax.rsqrt(jnp.mean(x * x, axis=-1, keepdims=True) + EPS) * g_ref[...]
    h = (y * (1.0 + _mod_rows(sc_ref)) + _mod_rows(sh_ref)).astype(BF16)
    off = 0
    for (_, width), o_ref in zip(_IN_BLOCKS, out_refs):
        for c0 in range(0, width, _IN_COL_CHUNK):
            cw = min(_IN_COL_CHUNK, width - c0)
            o_ref[:, c0:c0 + cw] = jnp.dot(h, w_ref[:, off + c0:off + c0 + cw],
                                           preferred_element_type=F32)
        off += width


def _inproj(x2d, mod, rows_per_batch, norm_g, w_cat, tm):
    t = x2d.shape[0]
    row = lambda w: pl.BlockSpec((tm, w), lambda i: (i, 0))
    return pl.pallas_call(
        _inproj_body,
        grid=(t // tm,),
        in_specs=[row(D_MODEL), _mod_spec(1, rows_per_batch, tm), _mod_spec(0, rows_per_batch, tm),
                  _const_spec((1, D_MODEL)),
                  pl.BlockSpec((D_MODEL, _IN_TOTAL), lambda i: (0, 0), pipeline_mode=pl.Buffered(1))],
        out_specs=[row(w) for _, w in _IN_BLOCKS],
        out_shape=[jax.ShapeDtypeStruct((t, w), F32) for _, w in _IN_BLOCKS],
        compiler_params=pltpu.CompilerParams(vmem_limit_bytes=VMEM_LIMIT),
        name="inproj",
    )(x2d, mod, mod, norm_g.reshape(1, -1), w_cat)


def _mixer_body(C, Lv, start,
                qkv_ref, ba_ref, z_ref, pin_ref, ga_ref, gb_ref, cbuf_ref, s0_ref, pbuf_ref,
                convw_ref, alog_ref, dtb_ref, dng_ref, wpool_ref, pscale_ref,
                mixed_ref, nconv_ref, ns_ref, npool_ref,
                xp_scr, act_scr, s_scr, pp_scr, odn_scr):
    n = pl.program_id(1)
    last = pl.num_programs(1) - 1

    @pl.when(n == 0)
    def _load_state():
        xp_scr[0:CONV_PAD, :] = cbuf_ref[0]
        pp_scr[0:POOL_PAD, :] = pbuf_ref[0]
        s_scr[...] = s0_ref[0]

    xp_scr[CONV_PAD:CONV_PAD + C, :] = qkv_ref[0]
    for c0 in range(0, QKV_WIDTH, 512):
        cs = slice(c0, c0 + 512)
        y = xp_scr[CONV_PAD:CONV_PAD + C, cs] * convw_ref[CONV_WIDTH - 1:CONV_WIDTH, cs]
        for k in range(CONV_WIDTH - 1):
            r0 = CONV_PAD - (CONV_WIDTH - 1) + k
            y = y + xp_scr[r0:r0 + C, cs] * convw_ref[k:k + 1, cs]
        act_scr[:, cs] = _silu(y)

    ba = ba_ref[0]
    lane = lax.broadcasted_iota(I32, (C, LANES), 1)
    beta_all = jax.nn.sigmoid(ba)
    g_all = -jnp.exp(alog_ref[...]) * _softplus(ba + dtb_ref[...])
    if Lv < C:
        valid = lax.broadcasted_iota(I32, (C, LANES), 0) < Lv
        beta_all = jnp.where(valid, beta_all, 0.0)
        g_all = jnp.where(valid, g_all, 0.0)
    ii = lax.broadcasted_iota(I32, (C, C), 0)
    jj = lax.broadcasted_iota(I32, (C, C), 1)
    causal = ii >= jj
    strict = ii > jj
    tril = jnp.where(causal, 1.0, 0.0).astype(BF16)
    eye = jnp.where(ii == jj, 1.0, 0.0)
    gc_all = sum(jnp.dot(tril, part, preferred_element_type=F32) for part in _split3(g_all))
    if C < LANES:
        gc_sq = jnp.concatenate([gc_all, jnp.zeros((LANES - C, LANES), F32)], axis=0)
    else:
        gc_sq = gc_all
    gc_t = gc_sq.T

    for h in range(DN_HEADS):
        hs = slice(h * DN_HEAD_DIM, (h + 1) * DN_HEAD_DIM)
        beta = jnp.sum(jnp.where(lane == h, beta_all, 0.0), axis=1, keepdims=True)
        gcol = jnp.sum(jnp.where(lane == DN_HEADS + h, gc_all, 0.0), axis=1, keepdims=True)
        grow = gc_t[DN_HEADS + h:DN_HEADS + h + 1, 0:C]
        glast = gcol[C - 1:C, :]

        q = act_scr[:, hs]
        k = act_scr[:, DN_WIDTH + h * DN_HEAD_DIM:DN_WIDTH + (h + 1) * DN_HEAD_DIM]
        v = act_scr[:, 2 * DN_WIDTH + h * DN_HEAD_DIM:2 * DN_WIDTH + (h + 1) * DN_HEAD_DIM]
        q = q * lax.rsqrt(jnp.sum(q * q, axis=-1, keepdims=True) + EPS) * (DN_HEAD_DIM ** -0.5)
        k = k * lax.rsqrt(jnp.sum(k * k, axis=-1, keepdims=True) + EPS)
        kb = k * beta
        vb = v * beta

        decay = jnp.where(causal, jnp.exp(jnp.where(causal, gcol - grow, 0.0)), 0.0)
        lower = jnp.where(strict, _dot_nt(kb, k) * decay, 0.0)
        ainv = eye - lower
        pw = lower
        p = 1
        while 2 * p < C:
            pw = _dot(pw, pw)
            ainv = ainv + _dot(ainv, pw)
            p *= 2
        sol = _dot(ainv, jnp.concatenate([vb, kb * jnp.exp(gcol)], axis=1))
        u = sol[:, :DN_HEAD_DIM]
        w = sol[:, DN_HEAD_DIM:]
        qk = _dot_nt(q, k) * decay
        k_tail = k * jnp.exp(glast - gcol)

        S = s_scr[h]
        v_new = u - _dot(w, S)
        o = _dot(q * jnp.exp(gcol), S) + _dot(qk, v_new)
        s_scr[h] = S * jnp.exp(glast) + lax.dot_general(
            k_tail.astype(BF16), v_new.astype(BF16), TN_DIMS, preferred_element_type=F32)

        zf = z_ref[0, :, hs]
        o = o * lax.rsqrt(jnp.mean(o * o, axis=-1, keepdims=True) + EPS) * dng_ref[...] * _silu(zf)
        odn_scr[:, hs] = o

    pp_scr[POOL_PAD:POOL_PAD + C, :] = pin_ref[0]
    pos = start + n * C + lax.broadcasted_iota(I32, (C, 1), 0)
    for gi, win in enumerate(POOL_WINDOWS):
        gs = slice(gi * POOL_GROUP_DIM, (gi + 1) * POOL_GROUP_DIM)
        xg = pp_scr[POOL_PAD:POOL_PAD + C, gs]
        ssum = xg
        for sft in range(1, win):
            ssum = ssum + pp_scr[POOL_PAD - sft:POOL_PAD - sft + C, gs]
        cnt = jnp.minimum(pos + 1, win).astype(F32)
        pooled = ssum / cnt - xg
        os_ = slice(gi * POOL_OUT_GROUP, (gi + 1) * POOL_OUT_GROUP)
        yp = _dot(pooled, wpool_ref[gi]) * pscale_ref[:, os_]
        mixed_ref[0, :, os_] = (jax.nn.sigmoid(ga_ref[0, :, os_]) * odn_scr[:, os_]
                                + jax.nn.sigmoid(gb_ref[0, :, os_]) * yp)

    @pl.when(n == last)
    def _store_state():
        nconv_ref[0] = xp_scr[Lv + CONV_PAD - (CONV_WIDTH - 1):Lv + CONV_PAD, :]
        npool_ref[0] = pp_scr[Lv + POOL_PAD - POOL_BUF:Lv + POOL_PAD, :]
        ns_ref[0] = s_scr[...]

    xp_scr[0:CONV_PAD, :] = xp_scr[C:C + CONV_PAD, :]
    pp_scr[0:POOL_PAD, :] = pp_scr[C:C + POOL_PAD, :]


def _mixer(proj, conv_buf, s0, pool_buf, start, seq_len, C,
           conv_w, a_log, dt_bias, dn_norm_g, w_pool, pool_scale):
    b, lp, _ = proj["qkv"].shape
    nchunks = lp // C
    lv = seq_len - (nchunks - 1) * C
    cbuf = jnp.pad(conv_buf, ((0, 0), (CONV_PAD - (CONV_WIDTH - 1), 0), (0, 0)))
    pbuf = jnp.pad(pool_buf, ((0, 0), (POOL_PAD - POOL_BUF, 0), (0, 0)))
    lane_pad = lambda a: jnp.pad(a.reshape(1, -1), ((0, 0), (DN_HEADS, LANES - 2 * DN_HEADS)))
    chunk = lambda w: pl.BlockSpec((1, C, w), lambda i, j: (i, j, 0))
    state = lambda *s: pl.BlockSpec((1,) + s, lambda i, j: (i,) + (0,) * len(s))
    return pl.pallas_call(
        functools.partial(_mixer_body, C, lv, start),
        grid=(b, nchunks),
        in_specs=[chunk(QKV_WIDTH), chunk(LANES), chunk(DN_WIDTH), chunk(POOL_WIDTH),
                  chunk(D_MODEL), chunk(D_MODEL),
                  state(CONV_PAD, QKV_WIDTH), state(DN_HEADS, DN_HEAD_DIM, DN_HEAD_DIM),
                  state(POOL_PAD, POOL_WIDTH),
                  _const_spec((CONV_WIDTH, QKV_WIDTH)), _const_spec((1, LANES)), _const_spec((1, LANES)),
                  _const_spec((1, DN_HEAD_DIM)),
                  _const_spec((len(POOL_WINDOWS), POOL_GROUP_DIM, POOL_OUT_GROUP)),
                  _const_spec((1, D_MODEL))],
        out_specs=[chunk(D_MODEL), state(CONV_WIDTH - 1, QKV_WIDTH),
                   state(DN_HEADS, DN_HEAD_DIM, DN_HEAD_DIM), state(POOL_BUF, POOL_WIDTH)],
        out_shape=[jax.ShapeDtypeStruct((b, lp, D_MODEL), F32),
                   jax.ShapeDtypeStruct((b, CONV_WIDTH - 1, QKV_WIDTH), F32),
                   jax.ShapeDtypeStruct((b, DN_HEADS, DN_HEAD_DIM, DN_HEAD_DIM), F32),
                   jax.ShapeDtypeStruct((b, POOL_BUF, POOL_WIDTH), F32)],
        scratch_shapes=[pltpu.VMEM((CONV_PAD + C + CONV_PAD, QKV_WIDTH), F32),
                        pltpu.VMEM((C, QKV_WIDTH), F32),
                        pltpu.VMEM((DN_HEADS, DN_HEAD_DIM, DN_HEAD_DIM), F32),
                        pltpu.VMEM((POOL_PAD + C + POOL_PAD, POOL_WIDTH), F32),
                        pltpu.VMEM((C, DN_WIDTH), F32)],
        compiler_params=pltpu.CompilerParams(dimension_semantics=("arbitrary", "arbitrary"),
                                             vmem_limit_bytes=VMEM_LIMIT),
        name="mixer",
    )(proj["qkv"], proj["ba"], proj["z"], proj["pool"], proj["ga"], proj["gb"], cbuf, s0, pbuf,
      conv_w, lane_pad(a_log), lane_pad(dt_bias), dn_norm_g.reshape(1, -1), w_pool,
      pool_scale.reshape(1, -1))


def _top16(s, ids, payload=None):
    big = float(2 ** 24)
    vals, sel, pays = [], [], []
    for _ in range(PEER_TOPK):
        m = jnp.max(s, axis=0, keepdims=True)
        am = jnp.min(jnp.where(s == m, ids, big), axis=0, keepdims=True)
        hit = ids == am
        if payload is not None:
            pays.append(jnp.max(jnp.where(hit, payload, -1.0), axis=0, keepdims=True))
        s = jnp.where(hit, -jnp.inf, s)
        vals.append(m)
        sel.append(am)
    out = (jnp.concatenate(vals, axis=0), jnp.concatenate(sel, axis=0))
    if payload is not None:
        out += (jnp.concatenate(pays, axis=0),)
    return out


_CAND_EDGE = 4


def _post_body(has_prev, has_fin, mixed_ref, x_ref, g1_ref, sc2_ref, sh2_ref, n2g_ref, wout_ref,
               wq_ref, keys_ref, *refs):
    refs = list(refs)
    prev_in = [refs.pop(0) for _ in range(2 if has_prev else 0)]
    fin_in = [refs.pop(0) for _ in range(4 if has_fin else 0)]
    x1_ref, h2_ref, idx_ref, gate_ref = refs[:4]
    extra_out = refs[4:]
    if has_prev:
        pre_ref, pgate_ref = prev_in
        extra_out.pop(0)[...] = _coef_words(pre_ref[...], pgate_ref[...])
    if has_fin:
        _final_body(*fin_in, extra_out.pop(0))
    tm = x_ref.shape[0]
    x1 = x_ref[...] + _mod_rows(g1_ref) * _dot(mixed_ref[...], wout_ref[...])
    x1_ref[...] = x1
    y = x1 * lax.rsqrt(jnp.mean(x1 * x1, axis=-1, keepdims=True) + EPS) * n2g_ref[...]
    h2 = y * (1.0 + _mod_rows(sc2_ref)) + _mod_rows(sh2_ref)
    h2_ref[...] = _pack_words(h2[:, :PACK_HALF], h2[:, PACK_HALF:])
    q = _dot(h2, wq_ref[...])

    K = PEER_TOPK
    key_id = lax.broadcasted_iota(I32, (PEER_NKEYS, 1), 0).astype(F32)
    r16 = lax.broadcasted_iota(I32, (K, 1), 0)
    cand_id = jnp.concatenate([(a * K + r16) for a in range(_CAND_EDGE)]
                              + [(r16 * K + b) for b in range(_CAND_EDGE)], axis=0).astype(F32)
    dup = r16 < _CAND_EDGE
    idx_rows, gate_rows = [], []
    for h in range(PEER_HEADS):
        half = []
        for p in range(2):
            c0 = (h * 2 + p) * PEER_KEY_HALF
            st = _dot_nt(keys_ref[h * 2 + p], q[:, c0:c0 + PEER_KEY_HALF])
            half.append(_top16(st, key_id))
        (s1, i1), (s2, i2) = half
        cand = jnp.concatenate(
            [s1[a:a + 1] + s2 for a in range(_CAND_EDGE)]
            + [jnp.where(dup, -jnp.inf, s1 + s2[b:b + 1]) for b in range(_CAND_EDGE)], axis=0)
        cidx = jnp.concatenate(
            [i1[a:a + 1] * PEER_NKEYS + i2 for a in range(_CAND_EDGE)]
            + [i1 * PEER_NKEYS + i2[b:b + 1] for b in range(_CAND_EDGE)], axis=0)
        best, _, eidx = _top16(cand, cand_id, cidx)
        e = jnp.exp(best - best[0:1])
        gate_rows.append(e / jnp.sum(e, axis=0, keepdims=True))
        idx_rows.append(eidx)
    idx_ref[...] = jnp.concatenate(idx_rows, axis=0).T.astype(I32)
    gate_ref[...] = jnp.concatenate(gate_rows, axis=0).T


def _post(mixed2d, x2d, mod, rows_per_batch, norm2_g, w_out, w_query, keys, tm, prev=None, fin=None):
    t = x2d.shape[0]
    steps = t // tm
    row = lambda w: pl.BlockSpec((tm, w), lambda i: (i, 0))
    in_specs = [row(D_MODEL), row(D_MODEL),
                _mod_spec(2, rows_per_batch, tm), _mod_spec(4, rows_per_batch, tm),
                _mod_spec(3, rows_per_batch, tm), _const_spec((1, D_MODEL)),
                _const_spec((D_MODEL, D_MODEL)), _const_spec((D_MODEL, 2 * PEER_HEADS * PEER_KEY_HALF)),
                _const_spec((2 * PEER_HEADS, PEER_NKEYS, PEER_KEY_HALF))]
    out_specs = [row(D_MODEL), row(PACK_HALF), row(PEER_HK), row(PEER_HK)]
    out_shape = [jax.ShapeDtypeStruct((t, D_MODEL), F32), jax.ShapeDtypeStruct((t, PACK_HALF), I32),
                 jax.ShapeDtypeStruct((t, PEER_HK), I32), jax.ShapeDtypeStruct((t, PEER_HK), F32)]
    args = [mixed2d, x2d, mod, mod, mod, norm2_g.reshape(1, -1), w_out, w_query, keys]
    if prev is not None:
        tp = prev[0].shape[0]
        prow = pl.BlockSpec((tp // steps, PEER_HK), lambda i: (i, 0))
        in_specs += [prow, prow]
        out_specs += [prow]
        out_shape += [jax.ShapeDtypeStruct((tp, PEER_HK), I32)]
        args += list(prev)
    if fin is not None:
        x1_f, peer_f, mod_f, rows_f, final_g = fin
        tf = x1_f.shape[0]
        frow = pl.BlockSpec((tf // steps, D_MODEL), lambda i: (i, 0))
        in_specs += [frow, frow, _mod_spec(5, rows_f, tf // steps), _const_spec((1, D_MODEL))]
        out_specs += [frow]
        out_shape += [jax.ShapeDtypeStruct((tf, D_MODEL), F32)]
        args += [x1_f, peer_f, mod_f, final_g.reshape(1, -1)]
    return pl.pallas_call(
        functools.partial(_post_body, prev is not None, fin is not None),
        grid=(steps,),
        in_specs=in_specs, out_specs=out_specs, out_shape=out_shape,
        compiler_params=pltpu.CompilerParams(vmem_limit_bytes=VMEM_LIMIT),
        name="post",
    )(*args)


SC_CORES = 2
SC_SUBCORES = 16
SC_LANES = 16
SC_WORKERS = SC_CORES * SC_SUBCORES
SC_TOKENS = 16
SC_SLOTS = 4
SC_BF16_GROUP = 4
PACK_HALF = D_MODEL // 2
SC_CHUNKS = PACK_HALF // SC_LANES
PROMPT_PARTS = 8
FIN_LAG = 3
ROW_TILE = 256


def _bf16_bits(v):
    return lax.bitcast_convert_type(v.astype(BF16).astype(F32), jnp.uint32)


def _pack_words(lo, hi):
    return lax.bitcast_convert_type((_bf16_bits(lo) >> 16) | _bf16_bits(hi), I32)


def _pack_body(x_ref, o_ref):
    o_ref[...] = _pack_words(x_ref[:, :PACK_HALF], x_ref[:, PACK_HALF:])


def _pack_table(tbl, rows=512):
    e = tbl.shape[0]
    return pl.pallas_call(
        _pack_body, grid=(e // rows,),
        in_specs=[pl.BlockSpec((rows, D_MODEL), lambda i: (i, 0))],
        out_specs=pl.BlockSpec((rows, PACK_HALF), lambda i: (i, 0)),
        out_shape=jax.ShapeDtypeStruct((e, PACK_HALF), I32), name="pack_table")(tbl)


def _tree_sum(terms):
    terms = list(terms)
    while len(terms) > 1:
        terms = [a + b for a, b in zip(terms[0::2], terms[1::2])] + terms[len(terms) & ~1:]
    return terms[0]


def _unpack_pair(w):
    lo = plsc.bitcast(lax.shift_left(w, jnp.full(w.shape, 16, I32)), F32)
    hi = plsc.bitcast(w & jnp.full(w.shape, -65536, I32), F32)
    return lo, hi


def _sc_mesh():
    return plsc.VectorSubcoreMesh(core_axis_name="c", subcore_axis_name="s")


def _sc_worker():
    return lax.axis_index("s") * SC_CORES + lax.axis_index("c")


def _sc_jobs(table_hbm, idx_v, buf, sem, compute):
    njobs = idx_v.shape[0] * PEER_HEADS

    def copy(j, slot):
        tt = j // PEER_HEADS
        h = j % PEER_HEADS
        rows = idx_v[tt, pl.ds(h * PEER_TOPK, PEER_TOPK)]
        return pltpu.make_async_copy(table_hbm.at[rows], buf.at[slot], sem.at[slot])

    for s in range(SC_SLOTS):
        copy(s, s).start()

    def group(g, c):
        for s in range(SC_SLOTS):
            j = g * SC_SLOTS + s
            copy(j, s).wait()
            compute(j // PEER_HEADS, j % PEER_HEADS, s)

            @pl.when(j + SC_SLOTS < njobs)
            def _next():
                copy(j + SC_SLOTS, s).start()
        return c

    lax.fori_loop(0, njobs // SC_SLOTS, group, 0)


def _peer_u_body(n_tok, idx_hbm, h2_hbm, u_hbm, pre_hbm, idx_v, h2_v, pre_v, ubuf, acc_v, sem):
    base = _sc_worker() * n_tok
    lane = lax.iota(I32, SC_LANES)

    def compute(tt, h, slot):
        def chunk(cg, accs):
            cs = [pl.ds((cg * SC_BF16_GROUP + i) * SC_LANES, SC_LANES) for i in range(SC_BF16_GROUP)]
            xs = [plsc.bitcast(h2_v[tt, c], BF16) for c in cs]
            out = []
            for k, a in enumerate(accs):
                part = _tree_sum([plsc.bitcast(ubuf[slot, k, c], BF16) * x for c, x in zip(cs, xs)])
                lo, hi = _unpack_pair(plsc.bitcast(part, I32))
                out.append(a + (lo + hi))
            return tuple(out)
        zero = jnp.zeros((SC_LANES,), F32)
        accs = lax.fori_loop(0, SC_CHUNKS // SC_BF16_GROUP, chunk, (zero,) * PEER_TOPK)
        for k, a in enumerate(accs):
            acc_v[k, :] = a
        tot = zero
        for j in range(SC_LANES):
            tot = tot + plsc.load_gather(acc_v, [lane, (lane + j) & (SC_LANES - 1)])
        pre_v[tt, pl.ds(h * PEER_TOPK, PEER_TOPK)] = tot

    tb = idx_v.shape[0]

    def block(bi, c):
        t0 = base + bi * tb
        pltpu.sync_copy(idx_hbm.at[pl.ds(t0, tb)], idx_v)
        pltpu.sync_copy(h2_hbm.at[pl.ds(t0, tb)], h2_v)
        _sc_jobs(u_hbm, idx_v, ubuf, sem, compute)
        pltpu.sync_copy(pre_v, pre_hbm.at[pl.ds(t0, tb)])
        return c

    lax.fori_loop(0, n_tok // tb, block, 0)


def _peer_v_body(n_tok, idx_hbm, coef_hbm, v_hbm, out_hbm, idx_v, coef_v, out_v, vbuf, sem):
    base = _sc_worker() * n_tok
    zero = jnp.zeros((SC_LANES,), F32)

    def compute(tt, h, slot):
        row = jnp.full((SC_LANES,), tt, I32)
        cb = [plsc.bitcast(plsc.load_gather(
                  coef_v, [row, jnp.full((SC_LANES,), h * PEER_TOPK + k, I32)]), BF16)
              for k in range(PEER_TOPK)]

        @plsc.parallel_loop(0, SC_CHUNKS, unroll=2)
        def _chunk(c):
            cs = pl.ds(c * SC_LANES, SC_LANES)
            prods = [plsc.bitcast(vbuf[slot, k, cs], BF16) * cb[k] for k in range(PEER_TOPK)]
            pairs = [_unpack_pair(plsc.bitcast(_tree_sum(prods[g:g + SC_BF16_GROUP]), I32))
                     for g in range(0, PEER_TOPK, SC_BF16_GROUP)]
            for half, off in ((0, 0), (1, PACK_HALF)):
                plsc.addupdate(out_v.at[tt, pl.ds(off + c * SC_LANES, SC_LANES)],
                               _tree_sum([p[half] for p in pairs]))

    tb = idx_v.shape[0]

    def block(bi, c):
        t0 = base + bi * tb
        pltpu.sync_copy(idx_hbm.at[pl.ds(t0, tb)], idx_v)
        pltpu.sync_copy(coef_hbm.at[pl.ds(t0, tb)], coef_v)

        def clear(i, cc):
            per_row = D_MODEL // SC_LANES
            out_v[i // per_row, pl.ds((i % per_row) * SC_LANES, SC_LANES)] = zero
            return cc
        lax.fori_loop(0, tb * (D_MODEL // SC_LANES), clear, 0)
        _sc_jobs(v_hbm, idx_v, vbuf, sem, compute)
        pltpu.sync_copy(out_v, out_hbm.at[pl.ds(t0, tb)])
        return c

    lax.fori_loop(0, n_tok // tb, block, 0)


def _peer_sc(body, idx, rows, table, out_width, name):
    t = idx.shape[0]
    assert t % SC_WORKERS == 0
    n_tok = t // SC_WORKERS
    tb = min(SC_TOKENS, n_tok)
    assert n_tok % tb == 0 and (tb * PEER_HEADS) % SC_SLOTS == 0
    return pl.kernel(
        functools.partial(body, n_tok),
        out_type=jax.ShapeDtypeStruct((t, out_width), F32),
        mesh=_sc_mesh(),
        scratch_types=[pltpu.VMEM((tb, PEER_HK), I32),
                       pltpu.VMEM((tb, rows.shape[1]), rows.dtype),
                       pltpu.VMEM((tb, out_width), F32),
                       pltpu.VMEM((SC_SLOTS, PEER_TOPK, PACK_HALF), I32)]
                      + ([pltpu.VMEM((PEER_TOPK, SC_LANES), F32)] if body is _peer_u_body else [])
                      + [pltpu.SemaphoreType.DMA((SC_SLOTS,))],
        compiler_params=pltpu.CompilerParams(needs_layout_passes=False),
        name=name,
    )(idx, rows, table)


def _coef_words(pre, gates):
    return _pack_words(*(gates * _gelu(pre),) * 2)


def _coef_body(pre_ref, gate_ref, coef_ref):
    coef_ref[...] = _coef_words(pre_ref[...], gate_ref[...])


def _coef(pre, gates, tm):
    t = pre.shape[0]
    row = pl.BlockSpec((tm, PEER_HK), lambda i: (i, 0))
    return pl.pallas_call(_coef_body, grid=(t // tm,), in_specs=[row, row], out_specs=row,
                          out_shape=jax.ShapeDtypeStruct((t, PEER_HK), I32), name="coef")(pre, gates)


def _final_body(x1_ref, peer_ref, g2_ref, fng_ref, y_ref):
    x2 = x1_ref[...] + _mod_rows(g2_ref) * peer_ref[...]
    y_ref[...] = x2 * lax.rsqrt(jnp.mean(x2 * x2, axis=-1, keepdims=True) + EPS) * fng_ref[...]


def _final(x1, peer_out, mod, rows_per_batch, final_g, tm):
    t = x1.shape[0]
    row = pl.BlockSpec((tm, D_MODEL), lambda i: (i, 0))
    return pl.pallas_call(
        _final_body, grid=(t // tm,),
        in_specs=[row, row, _mod_spec(5, rows_per_batch, tm), _const_spec((1, D_MODEL))],
        out_specs=row, out_shape=jax.ShapeDtypeStruct((t, D_MODEL), F32), name="final",
    )(x1, peer_out, mod, final_g.reshape(1, -1))


def _expert_gather_v(g, coef, expert_v):
    g["peer_out"] = _peer_sc(_peer_v_body, g["idx"], coef, expert_v, D_MODEL, "peer_v")


def _front(x, mod, conv_buf, s0, pool_buf, start, chunk, tm, wts, prev, fin):
    b, l, _ = x.shape
    t = b * l
    x2d = x.reshape(t, D_MODEL)
    if l >= tm:
        modx = mod.reshape(b, 6, 1, D_MODEL).transpose(1, 0, 2, 3)
    else:
        modx = jnp.repeat(mod.reshape(b, 6, D_MODEL), l, axis=0).transpose(1, 0, 2)
    outs = _inproj(x2d, modx, l, wts["norm1_g"], wts["w_cat"], tm)
    lp = -(-l // chunk) * chunk
    proj = {}
    for (name, w), a in zip(_IN_BLOCKS, outs):
        a = a.reshape(b, l, w)
        proj[name] = a if lp == l else jnp.pad(a, ((0, 0), (0, lp - l), (0, 0)))
    mixed, nconv, ns, npool = _mixer(proj, conv_buf, s0, pool_buf, start, l, chunk,
                                     wts["conv_w"], wts["a_log"], wts["dt_bias"], wts["dn_norm_g"],
                                     wts["w_pool"], wts["pool_scale"])
    mixed2d = mixed[:, :l].reshape(t, D_MODEL)
    res = _post(mixed2d, x2d, modx, l, wts["norm2_g"], wts["w_out"], wts["w_query"], wts["keys"], tm,
                prev=None if prev is None else (prev["pre"], prev["gates"]),
                fin=None if fin is None else (fin["x1"], fin["peer_out"], fin["mod"], fin["l"],
                                              wts["final_norm_g"]))
    x1, h2, idx, gates = res[:4]
    extra = list(res[4:])
    coef_prev = extra.pop(0) if prev is not None else None
    y_fin = extra.pop(0).reshape(fin["b"], fin["l"], D_MODEL) if fin is not None else None
    pre = _peer_sc(_peer_u_body, idx, h2, wts["expert_u"], PEER_HK, "peer_u")
    g = dict(x1=x1, idx=idx, gates=gates, pre=pre, mod=modx, b=b, l=l, tm=tm,
             states=(nconv, ns, npool))
    return g, coef_prev, y_fin


def kernel(x_prompt, x_sample, c_prompt, c_sample, state_conv, state_delta, state_pool, w_ada, b_ada, norm1_g, w_in, conv_w, a_log, dt_bias, dn_norm_g, w_pool, pool_scale, w_out, norm2_g, w_query, sub_keys, expert_u, expert_v, final_norm_g):
    bp = x_prompt.shape[0]
    bs = x_sample.shape[0]
    yp, ys = x_prompt, x_sample
    conv_p, delta_p, pool_p, conv_s, delta_s, pool_s = [], [], [], [], [], []
    zero_conv = jnp.zeros((bp, CONV_WIDTH - 1, QKV_WIDTH), F32)
    zero_delta = jnp.zeros((bp, DN_HEADS, DN_HEAD_DIM, DN_HEAD_DIM), F32)
    zero_pool = jnp.zeros((bp, POOL_BUF, POOL_WIDTH), F32)
    c_all = jnp.concatenate([c_prompt, c_sample], axis=0)
    for layer in range(DEPTH):
        wi = w_in[layer]
        o_b = QKV_WIDTH
        o_z = o_b + 2 * DN_HEADS
        w_ba = jnp.pad(wi[:, o_b:o_z], ((0, 0), (0, LANES - 2 * DN_HEADS)))
        w_cat = jnp.concatenate([wi[:, :o_b], wi[:, o_z:], w_ba], axis=1).astype(BF16)
        last = layer == DEPTH - 1
        wts = dict(
            norm1_g=norm1_g[layer], w_cat=w_cat, conv_w=conv_w[layer], a_log=a_log[layer],
            dt_bias=dt_bias[layer], dn_norm_g=dn_norm_g[layer], w_pool=w_pool[layer],
            pool_scale=pool_scale[layer], w_out=w_out[layer].astype(BF16), norm2_g=norm2_g[layer],
            w_query=w_query[layer].astype(BF16),
            keys=sub_keys[layer].reshape(2 * PEER_HEADS, PEER_NKEYS, PEER_KEY_HALF).astype(BF16),
            expert_u=_pack_table(expert_u[layer]), expert_v=_pack_table(expert_v[layer]),
            final_norm_g=final_norm_g if last else jnp.ones_like(final_norm_g))
        mod = _ada(c_all, w_ada[layer], b_ada[layer])
        assert last, "final norm is fused into the expert stage"
        step = bp // PROMPT_PARTS
        jobs = [(yp[b0:b0 + step], mod[b0:b0 + step], zero_conv[b0:b0 + step], zero_delta[b0:b0 + step],
                 zero_pool[b0:b0 + step], 0, DN_CHUNK) for b0 in range(0, bp, step)]
        jobs.append((ys, mod[bp:], state_conv[layer], state_delta[layer], state_pool[layer],
                     PAST_LEN, SUBLANES))
        groups = []
        for j, (xg, mg, cg, sg, pg, start, chunk) in enumerate(jobs):
            prev = groups[j - 1] if j >= 1 else None
            fin = groups[j - FIN_LAG] if j >= FIN_LAG else None
            if fin is not None and fin["x1"].shape[0] != xg.shape[0] * xg.shape[1]:
                fin = None
            g, coef_prev, y_fin = _front(xg, mg, cg, sg, pg, start, chunk, ROW_TILE, wts, prev, fin)
            if prev is not None:
                _expert_gather_v(prev, coef_prev, wts["expert_v"])
            if fin is not None:
                fin["y"] = y_fin
            groups.append(g)
        _expert_gather_v(groups[-1], _coef(groups[-1]["pre"], groups[-1]["gates"], ROW_TILE),
                         wts["expert_v"])
        done = []
        for g in groups:
            if "y" not in g:
                g["y"] = _final(g["x1"], g["peer_out"], g["mod"], g["l"], wts["final_norm_g"],
                                g["tm"]).reshape(g["b"], g["l"], D_MODEL)
            done.append((g["y"],) + g["states"])
        yp, cp, sp, pp = (jnp.concatenate(a, axis=0) for a in zip(*done[:-1]))
        ys, cs, ss, ps = done[-1]
        conv_p.append(cp)
        delta_p.append(sp)
        pool_p.append(pp)
        conv_s.append(cs)
        delta_s.append(ss)
        pool_s.append(ps)
    return (yp, ys, jnp.stack(conv_p), jnp.stack(delta_p), jnp.stack(pool_p),
            jnp.stack(conv_s), jnp.stack(delta_s), jnp.stack(pool_s))
```

```python
import functools

import jax
import jax.numpy as jnp
from jax import lax
from jax.experimental import pallas as pl
from jax.experimental.pallas import tpu as pltpu
from jax.experimental.pallas import tpu_sc as plsc

F32 = jnp.float32
BF16 = jnp.bfloat16
I32 = jnp.int32

D_MODEL = 1024
DEPTH = 1
PAST_LEN = 16384
DN_HEADS = 8
DN_HEAD_DIM = 128
DN_WIDTH = DN_HEADS * DN_HEAD_DIM
QKV_WIDTH = 3 * DN_WIDTH
CONV_WIDTH = 4
DN_CHUNK = 64
POOL_WINDOWS = (2, 4, 8, 16)
POOL_GROUP_DIM = 128
POOL_WIDTH = len(POOL_WINDOWS) * POOL_GROUP_DIM
POOL_OUT_GROUP = D_MODEL // len(POOL_WINDOWS)
POOL_BUF = max(POOL_WINDOWS) - 1
PEER_HEADS = 8
PEER_NKEYS = 128
PEER_TOPK = 16
PEER_KEY_HALF = 128
PEER_HK = PEER_HEADS * PEER_TOPK
EPS = 1e-6

LANES = 128
SUBLANES = 8
CONV_PAD = SUBLANES
POOL_PAD = 16
VMEM_LIMIT = 56 * 1024 * 1024

NT_DIMS = (((1,), (1,)), ((), ()))
TN_DIMS = (((0,), (0,)), ((), ()))


def _dot(a, b):
    return jnp.dot(a.astype(BF16), b.astype(BF16), preferred_element_type=F32)


def _dot_nt(a, b):
    return lax.dot_general(a.astype(BF16), b.astype(BF16), NT_DIMS, preferred_element_type=F32)


def _split3(x):
    hi = x.astype(BF16)
    r1 = x - hi.astype(F32)
    mid = r1.astype(BF16)
    lo = (r1 - mid.astype(F32)).astype(BF16)
    return hi, mid, lo


def _silu(x):
    return x * jax.nn.sigmoid(x)


def _gelu(x):
    return 0.5 * x * (1.0 + lax.erf(x * (0.5 ** 0.5)))


def _softplus(x):
    return jnp.maximum(x, 0.0) + jnp.log(1.0 + jnp.exp(-jnp.abs(x)))


def _mod_rows(ref):
    m = ref[...]
    return m.reshape(m.shape[-2], m.shape[-1])


def _mod_spec(k, rows_per_batch, tm):
    if rows_per_batch >= tm:
        tiles = rows_per_batch // tm
        return pl.BlockSpec((1, 1, 1, D_MODEL), lambda i, *_: (k, i // tiles, 0, 0))
    return pl.BlockSpec((1, tm, D_MODEL), lambda i, *_: (k, i, 0))


def _const_spec(shape):
    nd = len(shape)
    return pl.BlockSpec(shape, lambda *_: (0,) * nd)


def _ada_body(c_ref, w_ref, b_ref, o_ref):
    o_ref[...] = _dot(_silu(c_ref[...]), w_ref[...]) + b_ref[...]


def _ada(c, w_ada, b_ada):
    n = c.shape[0]
    return pl.pallas_call(
        _ada_body,
        grid=(6,),
        in_specs=[pl.BlockSpec((n, D_MODEL), lambda j: (0, 0)),
                  pl.BlockSpec((D_MODEL, D_MODEL), lambda j: (0, j)),
                  pl.BlockSpec((1, D_MODEL), lambda j: (0, j))],
        out_specs=pl.BlockSpec((n, D_MODEL), lambda j: (0, j)),
        out_shape=jax.ShapeDtypeStruct((n, 6 * D_MODEL), F32),
        name="ada",
    )(c, w_ada, b_ada.reshape(1, -1))


_IN_BLOCKS = (("qkv", QKV_WIDTH), ("z", DN_WIDTH), ("pool", POOL_WIDTH),
              ("ga", D_MODEL), ("gb", D_MODEL), ("ba", LANES))
_IN_TOTAL = sum(w for _, w in _IN_BLOCKS)
_IN_COL_CHUNK = 512


def _inproj_body(x_ref, sc_ref, sh_ref, g_ref, w_ref, *out_refs):
    x = x_ref[...]
    y = x * lax.rsqrt(jnp.mean(x * x, axis=-1, keepdims=True) + EPS) * g_ref[...]
    h = (y * (1.0 + _mod_rows(sc_ref)) + _mod_rows(sh_ref)).astype(BF16)
    off = 0
    for (_, width), o_ref in zip(_IN_BLOCKS, out_refs):
        for c0 in range(0, width, _IN_COL_CHUNK):
            cw = min(_IN_COL_CHUNK, width - c0)
            o_ref[:, c0:c0 + cw] = jnp.dot(h, w_ref[:, off + c0:off + c0 + cw],
                                           preferred_element_type=F32)
        off += width


def _inproj(x2d, mod, rows_per_batch, norm_g, w_cat, tm):
    t = x2d.shape[0]
    row = lambda w: pl.BlockSpec((tm, w), lambda i: (i, 0))
    return pl.pallas_call(
        _inproj_body,
        grid=(t // tm,),
        in_specs=[row(D_MODEL), _mod_spec(1, rows_per_batch, tm), _mod_spec(0, rows_per_batch, tm),
                  _const_spec((1, D_MODEL)),
                  pl.BlockSpec((D_MODEL, _IN_TOTAL), lambda i: (0, 0), pipeline_mode=pl.Buffered(1))],
        out_specs=[row(w) for _, w in _IN_BLOCKS],
        out_shape=[jax.ShapeDtypeStruct((t, w), F32) for _, w in _IN_BLOCKS],
        compiler_params=pltpu.CompilerParams(vmem_limit_bytes=VMEM_LIMIT),
        name="inproj",
    )(x2d, mod, mod, norm_g.reshape(1, -1), w_cat)


def _mixer_body(C, Lv, start,
                qkv_ref, ba_ref, z_ref, pin_ref, ga_ref, gb_ref, cbuf_ref, s0_ref, pbuf_ref,
                convw_ref, alog_ref, dtb_ref, dng_ref, wpool_ref, pscale_ref,
                mixed_ref, nconv_ref, ns_ref, npool_ref,
                xp_scr, act_scr, s_scr, pp_scr, odn_scr):
    n = pl.program_id(1)
    last = pl.num_programs(1) - 1

    @pl.when(n == 0)
    def _load_state():
        xp_scr[0:CONV_PAD, :] = cbuf_ref[0]
        pp_scr[0:POOL_PAD, :] = pbuf_ref[0]
        s_scr[...] = s0_ref[0]

    xp_scr[CONV_PAD:CONV_PAD + C, :] = qkv_ref[0]
    for c0 in range(0, QKV_WIDTH, 512):
        cs = slice(c0, c0 + 512)
        y = xp_scr[CONV_PAD:CONV_PAD + C, cs] * convw_ref[CONV_WIDTH - 1:CONV_WIDTH, cs]
        for k in range(CONV_WIDTH - 1):
            r0 = CONV_PAD - (CONV_WIDTH - 1) + k
            y = y + xp_scr[r0:r0 + C, cs] * convw_ref[k:k + 1, cs]
        act_scr[:, cs] = _silu(y)

    ba = ba_ref[0]
    lane = lax.broadcasted_iota(I32, (C, LANES), 1)
    beta_all = jax.nn.sigmoid(ba)
    g_all = -jnp.exp(alog_ref[...]) * _softplus(ba + dtb_ref[...])
    if Lv < C:
        valid = lax.broadcasted_iota(I32, (C, LANES), 0) < Lv
        beta_all = jnp.where(valid, beta_all, 0.0)
        g_all = jnp.where(valid, g_all, 0.0)
    ii = lax.broadcasted_iota(I32, (C, C), 0)
    jj = lax.broadcasted_iota(I32, (C, C), 1)
    causal = ii >= jj
    strict = ii > jj
    tril = jnp.where(causal, 1.0, 0.0).astype(BF16)
    eye = jnp.where(ii == jj, 1.0, 0.0)
    gc_all = sum(jnp.dot(tril, part, preferred_element_type=F32) for part in _split3(g_all))
    if C < LANES:
        gc_sq = jnp.concatenate([gc_all, jnp.zeros((LANES - C, LANES), F32)], axis=0)
    else:
        gc_sq = gc_all
    gc_t = gc_sq.T

    H = range(DN_HEADS)
    hsl = [slice(h * DN_HEAD_DIM, (h + 1) * DN_HEAD_DIM) for h in H]
    beta = [jnp.sum(jnp.where(lane == h, beta_all, 0.0), axis=1, keepdims=True) for h in H]
    gcol = [jnp.sum(jnp.where(lane == DN_HEADS + h, gc_all, 0.0), axis=1, keepdims=True) for h in H]
    grow = [gc_t[DN_HEADS + h:DN_HEADS + h + 1, 0:C] for h in H]
    glast = [g[C - 1:C, :] for g in gcol]
    q = [act_scr[:, hsl[h]] for h in H]
    k = [act_scr[:, DN_WIDTH + h * DN_HEAD_DIM:DN_WIDTH + (h + 1) * DN_HEAD_DIM] for h in H]
    v = [act_scr[:, 2 * DN_WIDTH + h * DN_HEAD_DIM:2 * DN_WIDTH + (h + 1) * DN_HEAD_DIM] for h in H]
    q = [x * lax.rsqrt(jnp.sum(x * x, axis=-1, keepdims=True) + EPS) * (DN_HEAD_DIM ** -0.5) for x in q]
    k = [x * lax.rsqrt(jnp.sum(x * x, axis=-1, keepdims=True) + EPS) for x in k]
    kb = [k[h] * beta[h] for h in H]
    vb = [v[h] * beta[h] for h in H]
    decay = [jnp.where(causal, jnp.exp(jnp.where(causal, gcol[h] - grow[h], 0.0)), 0.0) for h in H]
    lower = [jnp.where(strict, _dot_nt(kb[h], k[h]) * decay[h], 0.0) for h in H]
    ainv = [eye - x for x in lower]
    pw = lower
    p = 1
    while 2 * p < C:
        pw = [_dot(x, x) for x in pw]
        ainv = [ainv[h] + _dot(ainv[h], pw[h]) for h in H]
        p *= 2
    sol = [_dot(ainv[h], jnp.concatenate([vb[h], kb[h] * jnp.exp(gcol[h])], axis=1)) for h in H]
    qk = [_dot_nt(q[h], k[h]) * decay[h] for h in H]
    k_tail = [k[h] * jnp.exp(glast[h] - gcol[h]) for h in H]
    S = [s_scr[h] for h in H]
    v_new = [sol[h][:, :DN_HEAD_DIM] - _dot(sol[h][:, DN_HEAD_DIM:], S[h]) for h in H]
    o = [_dot(q[h] * jnp.exp(gcol[h]), S[h]) + _dot(qk[h], v_new[h]) for h in H]
    for h in H:
        s_scr[h] = S[h] * jnp.exp(glast[h]) + lax.dot_general(
            k_tail[h].astype(BF16), v_new[h].astype(BF16), TN_DIMS, preferred_element_type=F32)
    for h in H:
        zf = z_ref[0, :, hsl[h]]
        odn_scr[:, hsl[h]] = (o[h] * lax.rsqrt(jnp.mean(o[h] * o[h], axis=-1, keepdims=True) + EPS)
                              * dng_ref[...] * _silu(zf))

    pp_scr[POOL_PAD:POOL_PAD + C, :] = pin_ref[0]
    pos = start + n * C + lax.broadcasted_iota(I32, (C, 1), 0)
    for gi, win in enumerate(POOL_WINDOWS):
        gs = slice(gi * POOL_GROUP_DIM, (gi + 1) * POOL_GROUP_DIM)
        xg = pp_scr[POOL_PAD:POOL_PAD + C, gs]
        ssum = xg
        for sft in range(1, win):
            ssum = ssum + pp_scr[POOL_PAD - sft:POOL_PAD - sft + C, gs]
        cnt = jnp.minimum(pos + 1, win).astype(F32)
        pooled = ssum / cnt - xg
        os_ = slice(gi * POOL_OUT_GROUP, (gi + 1) * POOL_OUT_GROUP)
        yp = _dot(pooled, wpool_ref[gi]) * pscale_ref[:, os_]
        mixed_ref[0, :, os_] = (jax.nn.sigmoid(ga_ref[0, :, os_]) * odn_scr[:, os_]
                                + jax.nn.sigmoid(gb_ref[0, :, os_]) * yp)

    @pl.when(n == last)
    def _store_state():
        nconv_ref[0] = xp_scr[Lv + CONV_PAD - (CONV_WIDTH - 1):Lv + CONV_PAD, :]
        npool_ref[0] = pp_scr[Lv + POOL_PAD - POOL_BUF:Lv + POOL_PAD, :]
        ns_ref[0] = s_scr[...]

    xp_scr[0:CONV_PAD, :] = xp_scr[C:C + CONV_PAD, :]
    pp_scr[0:POOL_PAD, :] = pp_scr[C:C + POOL_PAD, :]


def _mixer(proj, conv_buf, s0, pool_buf, start, seq_len, C,
           conv_w, a_log, dt_bias, dn_norm_g, w_pool, pool_scale):
    b, lp, _ = proj["qkv"].shape
    nchunks = lp // C
    lv = seq_len - (nchunks - 1) * C
    cbuf = jnp.pad(conv_buf, ((0, 0), (CONV_PAD - (CONV_WIDTH - 1), 0), (0, 0)))
    pbuf = jnp.pad(pool_buf, ((0, 0), (POOL_PAD - POOL_BUF, 0), (0, 0)))
    lane_pad = lambda a: jnp.pad(a.reshape(1, -1), ((0, 0), (DN_HEADS, LANES - 2 * DN_HEADS)))
    chunk = lambda w: pl.BlockSpec((1, C, w), lambda i, j: (i, j, 0))
    state = lambda *s: pl.BlockSpec((1,) + s, lambda i, j: (i,) + (0,) * len(s))
    return pl.pallas_call(
        functools.partial(_mixer_body, C, lv, start),
        grid=(b, nchunks),
        in_specs=[chunk(QKV_WIDTH), chunk(LANES), chunk(DN_WIDTH), chunk(POOL_WIDTH),
                  chunk(D_MODEL), chunk(D_MODEL),
                  state(CONV_PAD, QKV_WIDTH), state(DN_HEADS, DN_HEAD_DIM, DN_HEAD_DIM),
                  state(POOL_PAD, POOL_WIDTH),
                  _const_spec((CONV_WIDTH, QKV_WIDTH)), _const_spec((1, LANES)), _const_spec((1, LANES)),
                  _const_spec((1, DN_HEAD_DIM)),
                  _const_spec((len(POOL_WINDOWS), POOL_GROUP_DIM, POOL_OUT_GROUP)),
                  _const_spec((1, D_MODEL))],
        out_specs=[chunk(D_MODEL), state(CONV_WIDTH - 1, QKV_WIDTH),
                   state(DN_HEADS, DN_HEAD_DIM, DN_HEAD_DIM), state(POOL_BUF, POOL_WIDTH)],
        out_shape=[jax.ShapeDtypeStruct((b, lp, D_MODEL), F32),
                   jax.ShapeDtypeStruct((b, CONV_WIDTH - 1, QKV_WIDTH), F32),
                   jax.ShapeDtypeStruct((b, DN_HEADS, DN_HEAD_DIM, DN_HEAD_DIM), F32),
                   jax.ShapeDtypeStruct((b, POOL_BUF, POOL_WIDTH), F32)],
        scratch_shapes=[pltpu.VMEM((CONV_PAD + C + CONV_PAD, QKV_WIDTH), F32),
                        pltpu.VMEM((C, QKV_WIDTH), F32),
                        pltpu.VMEM((DN_HEADS, DN_HEAD_DIM, DN_HEAD_DIM), F32),
                        pltpu.VMEM((POOL_PAD + C + POOL_PAD, POOL_WIDTH), F32),
                        pltpu.VMEM((C, DN_WIDTH), F32)],
        compiler_params=pltpu.CompilerParams(dimension_semantics=("arbitrary", "arbitrary"),
                                             vmem_limit_bytes=VMEM_LIMIT),
        name="mixer",
    )(proj["qkv"], proj["ba"], proj["z"], proj["pool"], proj["ga"], proj["gb"], cbuf, s0, pbuf,
      conv_w, lane_pad(a_log), lane_pad(dt_bias), dn_norm_g.reshape(1, -1), w_pool,
      pool_scale.reshape(1, -1))


def _top16(s, ids, payload=None):
    big = float(2 ** 24)
    vals, sel, pays = [], [], []
    for _ in range(PEER_TOPK):
        m = jnp.max(s, axis=0, keepdims=True)
        am = jnp.min(jnp.where(s == m, ids, big), axis=0, keepdims=True)
        hit = ids == am
        if payload is not None:
            pays.append(jnp.max(jnp.where(hit, payload, -1.0), axis=0, keepdims=True))
        s = jnp.where(hit, -jnp.inf, s)
        vals.append(m)
        sel.append(am)
    out = (jnp.concatenate(vals, axis=0), jnp.concatenate(sel, axis=0))
    if payload is not None:
        out += (jnp.concatenate(pays, axis=0),)
    return out


_CAND_EDGE = 4


def _post_body(has_prev, has_fin, mixed_ref, x_ref, g1_ref, sc2_ref, sh2_ref, n2g_ref, wout_ref,
               wq_ref, keys_ref, *refs):
    refs = list(refs)
    prev_in = [refs.pop(0) for _ in range(2 if has_prev else 0)]
    fin_in = [refs.pop(0) for _ in range(4 if has_fin else 0)]
    x1_ref, h2_ref, idx_ref, gate_ref = refs[:4]
    extra_out = refs[4:]
    if has_prev:
        pre_ref, pgate_ref = prev_in
        extra_out.pop(0)[...] = _coef_words(pre_ref[...], pgate_ref[...])
    if has_fin:
        _final_body(*fin_in, extra_out.pop(0))
    tm = x_ref.shape[0]
    x1 = x_ref[...] + _mod_rows(g1_ref) * _dot(mixed_ref[...], wout_ref[...])
    x1_ref[...] = x1
    y = x1 * lax.rsqrt(jnp.mean(x1 * x1, axis=-1, keepdims=True) + EPS) * n2g_ref[...]
    h2 = y * (1.0 + _mod_rows(sc2_ref)) + _mod_rows(sh2_ref)
    h2_ref[...] = _pack_words(h2[:, :PACK_HALF], h2[:, PACK_HALF:])
    q = _dot(h2, wq_ref[...])

    K = PEER_TOPK
    key_id = lax.broadcasted_iota(I32, (PEER_NKEYS, 1), 0).astype(F32)
    r16 = lax.broadcasted_iota(I32, (K, 1), 0)
    cand_id = jnp.concatenate([(a * K + r16) for a in range(_CAND_EDGE)]
                              + [(r16 * K + b) for b in range(_CAND_EDGE)], axis=0).astype(F32)
    dup = r16 < _CAND_EDGE
    idx_rows, gate_rows = [], []
    for h in range(PEER_HEADS):
        half = []
        for p in range(2):
            c0 = (h * 2 + p) * PEER_KEY_HALF
            st = _dot_nt(keys_ref[h * 2 + p], q[:, c0:c0 + PEER_KEY_HALF])
            half.append(_top16(st, key_id))
        (s1, i1), (s2, i2) = half
        cand = jnp.concatenate(
            [s1[a:a + 1] + s2 for a in range(_CAND_EDGE)]
            + [jnp.where(dup, -jnp.inf, s1 + s2[b:b + 1]) for b in range(_CAND_EDGE)], axis=0)
        cidx = jnp.concatenate(
            [i1[a:a + 1] * PEER_NKEYS + i2 for a in range(_CAND_EDGE)]
            + [i1 * PEER_NKEYS + i2[b:b + 1] for b in range(_CAND_EDGE)], axis=0)
        best, _, eidx = _top16(cand, cand_id, cidx)
        e = jnp.exp(best - best[0:1])
        gate_rows.append(e / jnp.sum(e, axis=0, keepdims=True))
        idx_rows.append(eidx)
    idx_ref[...] = jnp.concatenate(idx_rows, axis=0).T.astype(I32)
    gate_ref[...] = jnp.concatenate(gate_rows, axis=0).T


def _post(mixed2d, x2d, mod, rows_per_batch, norm2_g, w_out, w_query, keys, tm, prev=None, fin=None):
    t = x2d.shape[0]
    steps = t // tm
    row = lambda w: pl.BlockSpec((tm, w), lambda i: (i, 0))
    in_specs = [row(D_MODEL), row(D_MODEL),
                _mod_spec(2, rows_per_batch, tm), _mod_spec(4, rows_per_batch, tm),
                _mod_spec(3, rows_per_batch, tm), _const_spec((1, D_MODEL)),
                _const_spec((D_MODEL, D_MODEL)), _const_spec((D_MODEL, 2 * PEER_HEADS * PEER_KEY_HALF)),
                _const_spec((2 * PEER_HEADS, PEER_NKEYS, PEER_KEY_HALF))]
    out_specs = [row(D_MODEL), row(PACK_HALF), row(PEER_HK), row(PEER_HK)]
    out_shape = [jax.ShapeDtypeStruct((t, D_MODEL), F32), jax.ShapeDtypeStruct((t, PACK_HALF), I32),
                 jax.ShapeDtypeStruct((t, PEER_HK), I32), jax.ShapeDtypeStruct((t, PEER_HK), F32)]
    args = [mixed2d, x2d, mod, mod, mod, norm2_g.reshape(1, -1), w_out, w_query, keys]
    if prev is not None:
        tp = prev[0].shape[0]
        prow = pl.BlockSpec((tp // steps, PEER_HK), lambda i: (i, 0))
        in_specs += [prow, prow]
        out_specs += [prow]
        out_shape += [jax.ShapeDtypeStruct((tp, PEER_HK), I32)]
        args += list(prev)
    if fin is not None:
        x1_f, peer_f, mod_f, rows_f, final_g = fin
        tf = x1_f.shape[0]
        frow = pl.BlockSpec((tf // steps, D_MODEL), lambda i: (i, 0))
        in_specs += [frow, frow, _mod_spec(5, rows_f, tf // steps), _const_spec((1, D_MODEL))]
        out_specs += [frow]
        out_shape += [jax.ShapeDtypeStruct((tf, D_MODEL), F32)]
        args += [x1_f, peer_f, mod_f, final_g.reshape(1, -1)]
    return pl.pallas_call(
        functools.partial(_post_body, prev is not None, fin is not None),
        grid=(steps,),
        in_specs=in_specs, out_specs=out_specs, out_shape=out_shape,
        compiler_params=pltpu.CompilerParams(vmem_limit_bytes=VMEM_LIMIT),
        name="post",
    )(*args)


SC_CORES = 2
SC_SUBCORES = 16
SC_LANES = 16
SC_WORKERS = SC_CORES * SC_SUBCORES
SC_TOKENS = 16
SC_SLOTS = 4
SC_BF16_GROUP = 4
PACK_HALF = D_MODEL // 2
SC_CHUNKS = PACK_HALF // SC_LANES
PROMPT_PARTS = 8
FIN_LAG = 3
ROW_TILE = 256


def _bf16_bits(v):
    return lax.bitcast_convert_type(v.astype(BF16).astype(F32), jnp.uint32)


def _pack_words(lo, hi):
    return lax.bitcast_convert_type((_bf16_bits(lo) >> 16) | _bf16_bits(hi), I32)


def _pack_body(x_ref, o_ref):
    o_ref[...] = _pack_words(x_ref[:, :PACK_HALF], x_ref[:, PACK_HALF:])


def _pack_table(tbl, rows=512):
    e = tbl.shape[0]
    return pl.pallas_call(
        _pack_body, grid=(e // rows,),
        in_specs=[pl.BlockSpec((rows, D_MODEL), lambda i: (i, 0))],
        out_specs=pl.BlockSpec((rows, PACK_HALF), lambda i: (i, 0)),
        out_shape=jax.ShapeDtypeStruct((e, PACK_HALF), I32), name="pack_table")(tbl)


def _tree_sum(terms):
    terms = list(terms)
    while len(terms) > 1:
        terms = [a + b for a, b in zip(terms[0::2], terms[1::2])] + terms[len(terms) & ~1:]
    return terms[0]


def _unpack_pair(w):
    lo = plsc.bitcast(lax.shift_left(w, jnp.full(w.shape, 16, I32)), F32)
    hi = plsc.bitcast(w & jnp.full(w.shape, -65536, I32), F32)
    return lo, hi


def _sc_mesh():
    return plsc.VectorSubcoreMesh(core_axis_name="c", subcore_axis_name="s")


def _sc_worker():
    return lax.axis_index("s") * SC_CORES + lax.axis_index("c")


def _sc_jobs(table_hbm, idx_v, buf, sem, compute):
    njobs = idx_v.shape[0] * PEER_HEADS

    def copy(j, slot):
        tt = j // PEER_HEADS
        h = j % PEER_HEADS
        rows = idx_v[tt, pl.ds(h * PEER_TOPK, PEER_TOPK)]
        return pltpu.make_async_copy(table_hbm.at[rows], buf.at[slot], sem.at[slot])

    for s in range(SC_SLOTS):
        copy(s, s).start()

    def group(g, c):
        for s in range(SC_SLOTS):
            j = g * SC_SLOTS + s
            copy(j, s).wait()
            compute(j // PEER_HEADS, j % PEER_HEADS, s)

            @pl.when(j + SC_SLOTS < njobs)
            def _next():
                copy(j + SC_SLOTS, s).start()
        return c

    lax.fori_loop(0, njobs // SC_SLOTS, group, 0)


def _peer_u_body(n_tok, idx_hbm, h2_hbm, u_hbm, pre_hbm, idx_v, h2_v, pre_v, ubuf, acc_v, sem):
    base = _sc_worker() * n_tok
    lane = lax.iota(I32, SC_LANES)

    def compute(tt, h, slot):
        def chunk(cg, accs):
            cs = [pl.ds((cg * SC_BF16_GROUP + i) * SC_LANES, SC_LANES) for i in range(SC_BF16_GROUP)]
            xs = [plsc.bitcast(h2_v[tt, c], BF16) for c in cs]
            out = []
            for k, a in enumerate(accs):
                part = _tree_sum([plsc.bitcast(ubuf[slot, k, c], BF16) * x for c, x in zip(cs, xs)])
                lo, hi = _unpack_pair(plsc.bitcast(part, I32))
                out.append(a + (lo + hi))
            return tuple(out)
        zero = jnp.zeros((SC_LANES,), F32)
        accs = lax.fori_loop(0, SC_CHUNKS // SC_BF16_GROUP, chunk, (zero,) * PEER_TOPK)
        for k, a in enumerate(accs):
            acc_v[k, :] = a
        tot = zero
        for j in range(SC_LANES):
            tot = tot + plsc.load_gather(acc_v, [lane, (lane + j) & (SC_LANES - 1)])
        pre_v[tt, pl.ds(h * PEER_TOPK, PEER_TOPK)] = tot

    tb = idx_v.shape[0]

    def block(bi, c):
        t0 = base + bi * tb
        pltpu.sync_copy(idx_hbm.at[pl.ds(t0, tb)], idx_v)
        pltpu.sync_copy(h2_hbm.at[pl.ds(t0, tb)], h2_v)
        _sc_jobs(u_hbm, idx_v, ubuf, sem, compute)
        pltpu.sync_copy(pre_v, pre_hbm.at[pl.ds(t0, tb)])
        return c

    lax.fori_loop(0, n_tok // tb, block, 0)


def _peer_v_body(n_tok, idx_hbm, coef_hbm, v_hbm, out_hbm, idx_v, coef_v, out_v, vbuf, sem):
    base = _sc_worker() * n_tok
    zero = jnp.zeros((SC_LANES,), F32)

    def compute(tt, h, slot):
        row = jnp.full((SC_LANES,), tt, I32)
        cb = [plsc.bitcast(plsc.load_gather(
                  coef_v, [row, jnp.full((SC_LANES,), h * PEER_TOPK + k, I32)]), BF16)
              for k in range(PEER_TOPK)]

        @plsc.parallel_loop(0, SC_CHUNKS, unroll=2)
        def _chunk(c):
            cs = pl.ds(c * SC_LANES, SC_LANES)
            prods = [plsc.bitcast(vbuf[slot, k, cs], BF16) * cb[k] for k in range(PEER_TOPK)]
            pairs = [_unpack_pair(plsc.bitcast(_tree_sum(prods[g:g + SC_BF16_GROUP]), I32))
                     for g in range(0, PEER_TOPK, SC_BF16_GROUP)]
            for half, off in ((0, 0), (1, PACK_HALF)):
                plsc.addupdate(out_v.at[tt, pl.ds(off + c * SC_LANES, SC_LANES)],
                               _tree_sum([p[half] for p in pairs]))

    tb = idx_v.shape[0]

    def block(bi, c):
        t0 = base + bi * tb
        pltpu.sync_copy(idx_hbm.at[pl.ds(t0, tb)], idx_v)
        pltpu.sync_copy(coef_hbm.at[pl.ds(t0, tb)], coef_v)

        def clear(i, cc):
            per_row = D_MODEL // SC_LANES
            out_v[i // per_row, pl.ds((i % per_row) * SC_LANES, SC_LANES)] = zero
            return cc
        lax.fori_loop(0, tb * (D_MODEL // SC_LANES), clear, 0)
        _sc_jobs(v_hbm, idx_v, vbuf, sem, compute)
        pltpu.sync_copy(out_v, out_hbm.at[pl.ds(t0, tb)])
        return c

    lax.fori_loop(0, n_tok // tb, block, 0)


def _peer_sc(body, idx, rows, table, out_width, name):
    t = idx.shape[0]
    assert t % SC_WORKERS == 0
    n_tok = t // SC_WORKERS
    tb = min(SC_TOKENS, n_tok)
    assert n_tok % tb == 0 and (tb * PEER_HEADS) % SC_SLOTS == 0
    return pl.kernel(
        functools.partial(body, n_tok),
        out_type=jax.ShapeDtypeStruct((t, out_width), F32),
        mesh=_sc_mesh(),
        scratch_types=[pltpu.VMEM((tb, PEER_HK), I32),
                       pltpu.VMEM((tb, rows.shape[1]), rows.dtype),
                       pltpu.VMEM((tb, out_width), F32),
                       pltpu.VMEM((SC_SLOTS, PEER_TOPK, PACK_HALF), I32)]
                      + ([pltpu.VMEM((PEER_TOPK, SC_LANES), F32)] if body is _peer_u_body else [])
                      + [pltpu.SemaphoreType.DMA((SC_SLOTS,))],
        compiler_params=pltpu.CompilerParams(needs_layout_passes=False),
        name=name,
    )(idx, rows, table)


def _coef_words(pre, gates):
    return _pack_words(*(gates * _gelu(pre),) * 2)


def _coef_body(pre_ref, gate_ref, coef_ref):
    coef_ref[...] = _coef_words(pre_ref[...], gate_ref[...])


def _coef(pre, gates, tm):
    t = pre.shape[0]
    row = pl.BlockSpec((tm, PEER_HK), lambda i: (i, 0))
    return pl.pallas_call(_coef_body, grid=(t // tm,), in_specs=[row, row], out_specs=row,
                          out_shape=jax.ShapeDtypeStruct((t, PEER_HK), I32), name="coef")(pre, gates)


def _final_body(x1_ref, peer_ref, g2_ref, fng_ref, y_ref):
    x2 = x1_ref[...] + _mod_rows(g2_ref) * peer_ref[...]
    y_ref[...] = x2 * lax.rsqrt(jnp.mean(x2 * x2, axis=-1, keepdims=True) + EPS) * fng_ref[...]


def _final(x1, peer_out, mod, rows_per_batch, final_g, tm):
    t = x1.shape[0]
    row = pl.BlockSpec((tm, D_MODEL), lambda i: (i, 0))
    return pl.pallas_call(
        _final_body, grid=(t // tm,),
        in_specs=[row, row, _mod_spec(5, rows_per_batch, tm), _const_spec((1, D_MODEL))],
        out_specs=row, out_shape=jax.ShapeDtypeStruct((t, D_MODEL), F32), name="final",
    )(x1, peer_out, mod, final_g.reshape(1, -1))


def _expert_gather_v(g, coef, expert_v):
    g["peer_out"] = _peer_sc(_peer_v_body, g["idx"], coef, expert_v, D_MODEL, "peer_v")


def _front(x, mod, conv_buf, s0, pool_buf, start, chunk, tm, wts, prev, fin):
    b, l, _ = x.shape
    t = b * l
    x2d = x.reshape(t, D_MODEL)
    if l >= tm:
        modx = mod.reshape(b, 6, 1, D_MODEL).transpose(1, 0, 2, 3)
    else:
        modx = jnp.repeat(mod.reshape(b, 6, D_MODEL), l, axis=0).transpose(1, 0, 2)
    outs = _inproj(x2d, modx, l, wts["norm1_g"], wts["w_cat"], tm)
    lp = -(-l // chunk) * chunk
    proj = {}
    for (name, w), a in zip(_IN_BLOCKS, outs):
        a = a.reshape(b, l, w)
        proj[name] = a if lp == l else jnp.pad(a, ((0, 0), (0, lp - l), (0, 0)))
    mixed, nconv, ns, npool = _mixer(proj, conv_buf, s0, pool_buf, start, l, chunk,
                                     wts["conv_w"], wts["a_log"], wts["dt_bias"], wts["dn_norm_g"],
                                     wts["w_pool"], wts["pool_scale"])
    mixed2d = mixed[:, :l].reshape(t, D_MODEL)
    res = _post(mixed2d, x2d, modx, l, wts["norm2_g"], wts["w_out"], wts["w_query"], wts["keys"], tm,
                prev=None if prev is None else (prev["pre"], prev["gates"]),
                fin=None if fin is None else (fin["x1"], fin["peer_out"], fin["mod"], fin["l"],
                                              wts["final_norm_g"]))
    x1, h2, idx, gates = res[:4]
    extra = list(res[4:])
    coef_prev = extra.pop(0) if prev is not None else None
    y_fin = extra.pop(0).reshape(fin["b"], fin["l"], D_MODEL) if fin is not None else None
    pre = _peer_sc(_peer_u_body, idx, h2, wts["expert_u"], PEER_HK, "peer_u")
    g = dict(x1=x1, idx=idx, gates=gates, pre=pre, mod=modx, b=b, l=l, tm=tm,
             states=(nconv, ns, npool))
    return g, coef_prev, y_fin


def kernel(x_prompt, x_sample, c_prompt, c_sample, state_conv, state_delta, state_pool, w_ada, b_ada, norm1_g, w_in, conv_w, a_log, dt_bias, dn_norm_g, w_pool, pool_scale, w_out, norm2_g, w_query, sub_keys, expert_u, expert_v, final_norm_g):
    bp = x_prompt.shape[0]
    bs = x_sample.shape[0]
    yp, ys = x_prompt, x_sample
    conv_p, delta_p, pool_p, conv_s, delta_s, pool_s = [], [], [], [], [], []
    zero_conv = jnp.zeros((bp, CONV_WIDTH - 1, QKV_WIDTH), F32)
    zero_delta = jnp.zeros((bp, DN_HEADS, DN_HEAD_DIM, DN_HEAD_DIM), F32)
    zero_pool = jnp.zeros((bp, POOL_BUF, POOL_WIDTH), F32)
    c_all = jnp.concatenate([c_prompt, c_sample], axis=0)
    for layer in range(DEPTH):
        wi = w_in[layer]
        o_b = QKV_WIDTH
        o_z = o_b + 2 * DN_HEADS
        w_ba = jnp.pad(wi[:, o_b:o_z], ((0, 0), (0, LANES - 2 * DN_HEADS)))
        w_cat = jnp.concatenate([wi[:, :o_b], wi[:, o_z:], w_ba], axis=1).astype(BF16)
        last = layer == DEPTH - 1
        wts = dict(
            norm1_g=norm1_g[layer], w_cat=w_cat, conv_w=conv_w[layer], a_log=a_log[layer],
            dt_bias=dt_bias[layer], dn_norm_g=dn_norm_g[layer], w_pool=w_pool[layer],
            pool_scale=pool_scale[layer], w_out=w_out[layer].astype(BF16), norm2_g=norm2_g[layer],
            w_query=w_query[layer].astype(BF16),
            keys=sub_keys[layer].reshape(2 * PEER_HEADS, PEER_NKEYS, PEER_KEY_HALF).astype(BF16),
            expert_u=_pack_table(expert_u[layer]), expert_v=_pack_table(expert_v[layer]),
            final_norm_g=final_norm_g if last else jnp.ones_like(final_norm_g))
        mod = _ada(c_all, w_ada[layer], b_ada[layer])
        assert last, "final norm is fused into the expert stage"
        step = bp // PROMPT_PARTS
        jobs = [(yp[b0:b0 + step], mod[b0:b0 + step], zero_conv[b0:b0 + step], zero_delta[b0:b0 + step],
                 zero_pool[b0:b0 + step], 0, DN_CHUNK) for b0 in range(0, bp, step)]
        jobs.append((ys, mod[bp:], state_conv[layer], state_delta[layer], state_pool[layer],
                     PAST_LEN, SUBLANES))
        groups = []
        for j, (xg, mg, cg, sg, pg, start, chunk) in enumerate(jobs):
            prev = groups[j - 1] if j >= 1 else None
            fin = groups[j - FIN_LAG] if j >= FIN_LAG else None
            if fin is not None and fin["x1"].shape[0] != xg.shape[0] * xg.shape[1]:
                fin = None
            g, coef_prev, y_fin = _front(xg, mg, cg, sg, pg, start, chunk, ROW_TILE, wts, prev, fin)
            if prev is not None:
                _expert_gather_v(prev, coef_prev, wts["expert_v"])
            if fin is not None:
                fin["y"] = y_fin
            groups.append(g)
        _expert_gather_v(groups[-1], _coef(groups[-1]["pre"], groups[-1]["gates"], ROW_TILE),
                         wts["expert_v"])
        done = []
        for g in groups:
            if "y" not in g:
                g["y"] = _final(g["x1"], g["peer_out"], g["mod"], g["l"], wts["final_norm_g"],
                                g["tm"]).reshape(g["b"], g["l"], D_MODEL)
            done.append((g["y"],) + g["states"])
        yp, cp, sp, pp = (jnp.concatenate(a, axis=0) for a in zip(*done[:-1]))
        ys, cs, ss, ps = done[-1]
        conv_p.append(cp)
        delta_p.append(sp)
        pool_p.append(pp)
        conv_s.append(cs)
        delta_s.append(ss)
        pool_s.append(ps)
    return (yp, ys, jnp.stack(conv_p), jnp.stack(delta_p), jnp.stack(pool_p),
            jnp.stack(conv_s), jnp.stack(delta_s), jnp.stack(pool_s))
```

```python
import functools

import jax
import jax.numpy as jnp
from jax import lax
from jax.experimental import pallas as pl
from jax.experimental.pallas import tpu as pltpu
from jax.experimental.pallas import tpu_sc as plsc

F32 = jnp.float32
BF16 = jnp.bfloat16
I32 = jnp.int32

D_MODEL = 1024
DEPTH = 1
PAST_LEN = 16384
DN_HEADS = 8
DN_HEAD_DIM = 128
DN_WIDTH = DN_HEADS * DN_HEAD_DIM
QKV_WIDTH = 3 * DN_WIDTH
CONV_WIDTH = 4
DN_CHUNK = 64
POOL_WINDOWS = (2, 4, 8, 16)
POOL_GROUP_DIM = 128
POOL_WIDTH = len(POOL_WINDOWS) * POOL_GROUP_DIM
POOL_OUT_GROUP = D_MODEL // len(POOL_WINDOWS)
POOL_BUF = max(POOL_WINDOWS) - 1
PEER_HEADS = 8
PEER_NKEYS = 128
PEER_TOPK = 16
PEER_KEY_HALF = 128
PEER_HK = PEER_HEADS * PEER_TOPK
EPS = 1e-6

LANES = 128
SUBLANES = 8
CONV_PAD = SUBLANES
POOL_PAD = 16
VMEM_LIMIT = 56 * 1024 * 1024

NT_DIMS = (((1,), (1,)), ((), ()))
TN_DIMS = (((0,), (0,)), ((), ()))


def _dot(a, b):
    return jnp.dot(a.astype(BF16), b.astype(BF16), preferred_element_type=F32)


def _dot_nt(a, b):
    return lax.dot_general(a.astype(BF16), b.astype(BF16), NT_DIMS, preferred_element_type=F32)


def _split3(x):
    hi = x.astype(BF16)
    r1 = x - hi.astype(F32)
    mid = r1.astype(BF16)
    lo = (r1 - mid.astype(F32)).astype(BF16)
    return hi, mid, lo


def _silu(x):
    return x * jax.nn.sigmoid(x)


def _gelu(x):
    return 0.5 * x * (1.0 + lax.erf(x * (0.5 ** 0.5)))


def _softplus(x):
    return jnp.maximum(x, 0.0) + jnp.log(1.0 + jnp.exp(-jnp.abs(x)))


def _mod_rows(ref):
    m = ref[...]
    return m.reshape(m.shape[-2], m.shape[-1])


def _mod_spec(k, rows_per_batch, tm):
    if rows_per_batch >= tm:
        tiles = rows_per_batch // tm
        return pl.BlockSpec((1, 1, 1, D_MODEL), lambda i, *_: (k, i // tiles, 0, 0))
    return pl.BlockSpec((1, tm, D_MODEL), lambda i, *_: (k, i, 0))


def _const_spec(shape):
    nd = len(shape)
    return pl.BlockSpec(shape, lambda *_: (0,) * nd)


def _ada_body(c_ref, w_ref, b_ref, o_ref):
    o_ref[...] = _dot(_silu(c_ref[...]), w_ref[...]) + b_ref[...]


def _ada(c, w_ada, b_ada):
    n = c.shape[0]
    return pl.pallas_call(
        _ada_body,
        grid=(6,),
        in_specs=[pl.BlockSpec((n, D_MODEL), lambda j: (0, 0)),
                  pl.BlockSpec((D_MODEL, D_MODEL), lambda j: (0, j)),
                  pl.BlockSpec((1, D_MODEL), lambda j: (0, j))],
        out_specs=pl.BlockSpec((n, D_MODEL), lambda j: (0, j)),
        out_shape=jax.ShapeDtypeStruct((n, 6 * D_MODEL), F32),
        name="ada",
    )(c, w_ada, b_ada.reshape(1, -1))


_IN_BLOCKS = (("qkv", QKV_WIDTH), ("z", DN_WIDTH), ("pool", POOL_WIDTH),
              ("ga", D_MODEL), ("gb", D_MODEL), ("ba", LANES))
_IN_TOTAL = sum(w for _, w in _IN_BLOCKS)
_IN_COL_CHUNK = 512


def _inproj_body(x_ref, sc_ref, sh_ref, g_ref, w_ref, *out_refs):
    x = x_ref[...]
    y = x * lax.rsqrt(jnp.mean(x * x, axis=-1, keepdims=True) + EPS) * g_ref[...]
    h = (y * (1.0 + _mod_rows(sc_ref)) + _mod_rows(sh_ref)).astype(BF16)
    off = 0
    for (_, width), o_ref in zip(_IN_BLOCKS, out_refs):
        for c0 in range(0, width, _IN_COL_CHUNK):
            cw = min(_IN_COL_CHUNK, width - c0)
            o_ref[:, c0:c0 + cw] = jnp.dot(h, w_ref[:, off + c0:off + c0 + cw],
                                           preferred_element_type=F32)
        off += width


def _inproj(x2d, mod, rows_per_batch, norm_g, w_cat, tm):
    t = x2d.shape[0]
    row = lambda w: pl.BlockSpec((tm, w), lambda i: (i, 0))
    return pl.pallas_call(
        _inproj_body,
        grid=(t // tm,),
        in_specs=[row(D_MODEL), _mod_spec(1, rows_per_batch, tm), _mod_spec(0, rows_per_batch, tm),
                  _const_spec((1, D_MODEL)),
                  pl.BlockSpec((D_MODEL, _IN_TOTAL), lambda i: (0, 0), pipeline_mode=pl.Buffered(1))],
        out_specs=[row(w) for _, w in _IN_BLOCKS],
        out_shape=[jax.ShapeDtypeStruct((t, w), F32) for _, w in _IN_BLOCKS],
        compiler_params=pltpu.CompilerParams(vmem_limit_bytes=VMEM_LIMIT),
        name="inproj",
    )(x2d, mod, mod, norm_g.reshape(1, -1), w_cat)


def _mixer_body(C, Lv, start,
                qkv_ref, ba_ref, z_ref, pin_ref, ga_ref, gb_ref, cbuf_ref, s0_ref, pbuf_ref,
                convw_ref, alog_ref, dtb_ref, dng_ref, wpool_ref, pscale_ref,
                mixed_ref, nconv_ref, ns_ref, npool_ref,
                xp_scr, act_scr, s_scr, pp_scr, odn_scr):
    n = pl.program_id(1)
    last = pl.num_programs(1) - 1

    @pl.when(n == 0)
    def _load_state():
        xp_scr[0:CONV_PAD, :] = cbuf_ref[0]
        pp_scr[0:POOL_PAD, :] = pbuf_ref[0]
        s_scr[...] = s0_ref[0]

    xp_scr[CONV_PAD:CONV_PAD + C, :] = qkv_ref[0]
    for c0 in range(0, QKV_WIDTH, 512):
        cs = slice(c0, c0 + 512)
        y = xp_scr[CONV_PAD:CONV_PAD + C, cs] * convw_ref[CONV_WIDTH - 1:CONV_WIDTH, cs]
        for k in range(CONV_WIDTH - 1):
            r0 = CONV_PAD - (CONV_WIDTH - 1) + k
            y = y + xp_scr[r0:r0 + C, cs] * convw_ref[k:k + 1, cs]
        act_scr[:, cs] = _silu(y)

    ba = ba_ref[0]
    lane = lax.broadcasted_iota(I32, (C, LANES), 1)
    beta_all = jax.nn.sigmoid(ba)
    g_all = -jnp.exp(alog_ref[...]) * _softplus(ba + dtb_ref[...])
    if Lv < C:
        valid = lax.broadcasted_iota(I32, (C, LANES), 0) < Lv
        beta_all = jnp.where(valid, beta_all, 0.0)
        g_all = jnp.where(valid, g_all, 0.0)
    ii = lax.broadcasted_iota(I32, (C, C), 0)
    jj = lax.broadcasted_iota(I32, (C, C), 1)
    causal = ii >= jj
    strict = ii > jj
    tril = jnp.where(causal, 1.0, 0.0).astype(BF16)
    eye = jnp.where(ii == jj, 1.0, 0.0)
    gc_all = sum(jnp.dot(tril, part, preferred_element_type=F32) for part in _split3(g_all))
    if C < LANES:
        gc_sq = jnp.concatenate([gc_all, jnp.zeros((LANES - C, LANES), F32)], axis=0)
    else:
        gc_sq = gc_all
    gc_t = gc_sq.T

    H = range(DN_HEADS)
    hsl = [slice(h * DN_HEAD_DIM, (h + 1) * DN_HEAD_DIM) for h in H]
    beta = [jnp.sum(jnp.where(lane == h, beta_all, 0.0), axis=1, keepdims=True) for h in H]
    gcol = [jnp.sum(jnp.where(lane == DN_HEADS + h, gc_all, 0.0), axis=1, keepdims=True) for h in H]
    grow = [gc_t[DN_HEADS + h:DN_HEADS + h + 1, 0:C] for h in H]
    glast = [g[C - 1:C, :] for g in gcol]
    q = [act_scr[:, hsl[h]] for h in H]
    k = [act_scr[:, DN_WIDTH + h * DN_HEAD_DIM:DN_WIDTH + (h + 1) * DN_HEAD_DIM] for h in H]
    v = [act_scr[:, 2 * DN_WIDTH + h * DN_HEAD_DIM:2 * DN_WIDTH + (h + 1) * DN_HEAD_DIM] for h in H]
    q = [x * lax.rsqrt(jnp.sum(x * x, axis=-1, keepdims=True) + EPS) * (DN_HEAD_DIM ** -0.5) for x in q]
    k = [x * lax.rsqrt(jnp.sum(x * x, axis=-1, keepdims=True) + EPS) for x in k]
    kb = [k[h] * beta[h] for h in H]
    vb = [v[h] * beta[h] for h in H]
    decay = [jnp.where(causal, jnp.exp(jnp.where(causal, gcol[h] - grow[h], 0.0)), 0.0) for h in H]
    lower = [jnp.where(strict, _dot_nt(kb[h], k[h]) * decay[h], 0.0) for h in H]
    ainv = [eye - x for x in lower]
    pw = lower
    p = 1
    while 2 * p < C:
        pw = [_dot(x, x) for x in pw]
        ainv = [ainv[h] + _dot(ainv[h], pw[h]) for h in H]
        p *= 2
    sol = [_dot(ainv[h], jnp.concatenate([vb[h], kb[h] * jnp.exp(gcol[h])], axis=1)) for h in H]
    qk = [_dot_nt(q[h], k[h]) * decay[h] for h in H]
    k_tail = [k[h] * jnp.exp(glast[h] - gcol[h]) for h in H]
    S = [s_scr[h] for h in H]
    v_new = [sol[h][:, :DN_HEAD_DIM] - _dot(sol[h][:, DN_HEAD_DIM:], S[h]) for h in H]
    o = [_dot(q[h] * jnp.exp(gcol[h]), S[h]) + _dot(qk[h], v_new[h]) for h in H]
    for h in H:
        s_scr[h] = S[h] * jnp.exp(glast[h]) + lax.dot_general(
            k_tail[h].astype(BF16), v_new[h].astype(BF16), TN_DIMS, preferred_element_type=F32)
    for h in H:
        zf = z_ref[0, :, hsl[h]]
        odn_scr[:, hsl[h]] = (o[h] * lax.rsqrt(jnp.mean(o[h] * o[h], axis=-1, keepdims=True) + EPS)
                              * dng_ref[...] * _silu(zf))

    pp_scr[POOL_PAD:POOL_PAD + C, :] = pin_ref[0]
    pos = start + n * C + lax.broadcasted_iota(I32, (C, 1), 0)
    for gi, win in enumerate(POOL_WINDOWS):
        gs = slice(gi * POOL_GROUP_DIM, (gi + 1) * POOL_GROUP_DIM)
        xg = pp_scr[POOL_PAD:POOL_PAD + C, gs]
        ssum = xg
        for sft in range(1, win):
            ssum = ssum + pp_scr[POOL_PAD - sft:POOL_PAD - sft + C, gs]
        cnt = jnp.minimum(pos + 1, win).astype(F32)
        pooled = ssum / cnt - xg
        os_ = slice(gi * POOL_OUT_GROUP, (gi + 1) * POOL_OUT_GROUP)
        yp = _dot(pooled, wpool_ref[gi]) * pscale_ref[:, os_]
        mixed_ref[0, :, os_] = (jax.nn.sigmoid(ga_ref[0, :, os_]) * odn_scr[:, os_]
                                + jax.nn.sigmoid(gb_ref[0, :, os_]) * yp)

    @pl.when(n == last)
    def _store_state():
        nconv_ref[0] = xp_scr[Lv + CONV_PAD - (CONV_WIDTH - 1):Lv + CONV_PAD, :]
        npool_ref[0] = pp_scr[Lv + POOL_PAD - POOL_BUF:Lv + POOL_PAD, :]
        ns_ref[0] = s_scr[...]

    xp_scr[0:CONV_PAD, :] = xp_scr[C:C + CONV_PAD, :]
    pp_scr[0:POOL_PAD, :] = pp_scr[C:C + POOL_PAD, :]


def _mixer(proj, conv_buf, s0, pool_buf, start, seq_len, C,
           conv_w, a_log, dt_bias, dn_norm_g, w_pool, pool_scale):
    b, lp, _ = proj["qkv"].shape
    nchunks = lp // C
    lv = seq_len - (nchunks - 1) * C
    cbuf = jnp.pad(conv_buf, ((0, 0), (CONV_PAD - (CONV_WIDTH - 1), 0), (0, 0)))
    pbuf = jnp.pad(pool_buf, ((0, 0), (POOL_PAD - POOL_BUF, 0), (0, 0)))
    lane_pad = lambda a: jnp.pad(a.reshape(1, -1), ((0, 0), (DN_HEADS, LANES - 2 * DN_HEADS)))
    chunk = lambda w: pl.BlockSpec((1, C, w), lambda i, j: (i, j, 0))
    state = lambda *s: pl.BlockSpec((1,) + s, lambda i, j: (i,) + (0,) * len(s))
    return pl.pallas_call(
        functools.partial(_mixer_body, C, lv, start),
        grid=(b, nchunks),
        in_specs=[chunk(QKV_WIDTH), chunk(LANES), chunk(DN_WIDTH), chunk(POOL_WIDTH),
                  chunk(D_MODEL), chunk(D_MODEL),
                  state(CONV_PAD, QKV_WIDTH), state(DN_HEADS, DN_HEAD_DIM, DN_HEAD_DIM),
                  state(POOL_PAD, POOL_WIDTH),
                  _const_spec((CONV_WIDTH, QKV_WIDTH)), _const_spec((1, LANES)), _const_spec((1, LANES)),
                  _const_spec((1, DN_HEAD_DIM)),
                  _const_spec((len(POOL_WINDOWS), POOL_GROUP_DIM, POOL_OUT_GROUP)),
                  _const_spec((1, D_MODEL))],
        out_specs=[chunk(D_MODEL), state(CONV_WIDTH - 1, QKV_WIDTH),
                   state(DN_HEADS, DN_HEAD_DIM, DN_HEAD_DIM), state(POOL_BUF, POOL_WIDTH)],
        out_shape=[jax.ShapeDtypeStruct((b, lp, D_MODEL), F32),
                   jax.ShapeDtypeStruct((b, CONV_WIDTH - 1, QKV_WIDTH), F32),
                   jax.ShapeDtypeStruct((b, DN_HEADS, DN_HEAD_DIM, DN_HEAD_DIM), F32),
                   jax.ShapeDtypeStruct((b, POOL_BUF, POOL_WIDTH), F32)],
        scratch_shapes=[pltpu.VMEM((CONV_PAD + C + CONV_PAD, QKV_WIDTH), F32),
                        pltpu.VMEM((C, QKV_WIDTH), F32),
                        pltpu.VMEM((DN_HEADS, DN_HEAD_DIM, DN_HEAD_DIM), F32),
                        pltpu.VMEM((POOL_PAD + C + POOL_PAD, POOL_WIDTH), F32),
                        pltpu.VMEM((C, DN_WIDTH), F32)],
        compiler_params=pltpu.CompilerParams(dimension_semantics=("arbitrary", "arbitrary"),
                                             vmem_limit_bytes=VMEM_LIMIT),
        name="mixer",
    )(proj["qkv"], proj["ba"], proj["z"], proj["pool"], proj["ga"], proj["gb"], cbuf, s0, pbuf,
      conv_w, lane_pad(a_log), lane_pad(dt_bias), dn_norm_g.reshape(1, -1), w_pool,
      pool_scale.reshape(1, -1))


def _top16(s, ids, payload=None):
    big = float(2 ** 24)
    vals, sel, pays = [], [], []
    for _ in range(PEER_TOPK):
        m = jnp.max(s, axis=0, keepdims=True)
        am = jnp.min(jnp.where(s == m, ids, big), axis=0, keepdims=True)
        hit = ids == am
        if payload is not None:
            pays.append(jnp.max(jnp.where(hit, payload, -1.0), axis=0, keepdims=True))
        s = jnp.where(hit, -jnp.inf, s)
        vals.append(m)
        sel.append(am)
    out = (jnp.concatenate(vals, axis=0), jnp.concatenate(sel, axis=0))
    if payload is not None:
        out += (jnp.concatenate(pays, axis=0),)
    return out


_CAND_EDGE = 4


def _post_body(has_prev, has_fin, mixed_ref, x_ref, g1_ref, sc2_ref, sh2_ref, n2g_ref, wout_ref,
               wq_ref, keys_ref, *refs):
    refs = list(refs)
    prev_in = [refs.pop(0) for _ in range(2 if has_prev else 0)]
    fin_in = [refs.pop(0) for _ in range(4 if has_fin else 0)]
    x1_ref, h2_ref, idx_ref, gate_ref = refs[:4]
    extra_out = refs[4:]
    if has_prev:
        pre_ref, pgate_ref = prev_in
        extra_out.pop(0)[...] = _coef_words(pre_ref[...], pgate_ref[...])
    if has_fin:
        _final_body(*fin_in, extra_out.pop(0))
    tm = x_ref.shape[0]
    x1 = x_ref[...] + _mod_rows(g1_ref) * _dot(mixed_ref[...], wout_ref[...])
    x1_ref[...] = x1
    y = x1 * lax.rsqrt(jnp.mean(x1 * x1, axis=-1, keepdims=True) + EPS) * n2g_ref[...]
    h2 = y * (1.0 + _mod_rows(sc2_ref)) + _mod_rows(sh2_ref)
    h2_ref[...] = _pack_words(h2[:, :PACK_HALF], h2[:, PACK_HALF:])
    q = _dot(h2, wq_ref[...])

    K = PEER_TOPK
    key_id = lax.broadcasted_iota(I32, (PEER_NKEYS, 1), 0).astype(F32)
    r16 = lax.broadcasted_iota(I32, (K, 1), 0)
    cand_id = jnp.concatenate([(a * K + r16) for a in range(_CAND_EDGE)]
                              + [(r16 * K + b) for b in range(_CAND_EDGE)], axis=0).astype(F32)
    dup = r16 < _CAND_EDGE
    idx_rows, gate_rows = [], []
    for h in range(PEER_HEADS):
        half = []
        for p in range(2):
            c0 = (h * 2 + p) * PEER_KEY_HALF
            st = _dot_nt(keys_ref[h * 2 + p], q[:, c0:c0 + PEER_KEY_HALF])
            half.append(_top16(st, key_id))
        (s1, i1), (s2, i2) = half
        cand = jnp.concatenate(
            [s1[a:a + 1] + s2 for a in range(_CAND_EDGE)]
            + [jnp.where(dup, -jnp.inf, s1 + s2[b:b + 1]) for b in range(_CAND_EDGE)], axis=0)
        cidx = jnp.concatenate(
            [i1[a:a + 1] * PEER_NKEYS + i2 for a in range(_CAND_EDGE)]
            + [i1 * PEER_NKEYS + i2[b:b + 1] for b in range(_CAND_EDGE)], axis=0)
        best, _, eidx = _top16(cand, cand_id, cidx)
        e = jnp.exp(best - best[0:1])
        gate_rows.append(e / jnp.sum(e, axis=0, keepdims=True))
        idx_rows.append(eidx)
    idx_ref[...] = jnp.concatenate(idx_rows, axis=0).T.astype(I32)
    gate_ref[...] = jnp.concatenate(gate_rows, axis=0).T


def _post(mixed2d, x2d, mod, rows_per_batch, norm2_g, w_out, w_query, keys, tm, prev=None, fin=None):
    t = x2d.shape[0]
    steps = t // tm
    row = lambda w: pl.BlockSpec((tm, w), lambda i: (i, 0))
    in_specs = [row(D_MODEL), row(D_MODEL),
                _mod_spec(2, rows_per_batch, tm), _mod_spec(4, rows_per_batch, tm),
                _mod_spec(3, rows_per_batch, tm), _const_spec((1, D_MODEL)),
                _const_spec((D_MODEL, D_MODEL)), _const_spec((D_MODEL, 2 * PEER_HEADS * PEER_KEY_HALF)),
                _const_spec((2 * PEER_HEADS, PEER_NKEYS, PEER_KEY_HALF))]
    out_specs = [row(D_MODEL), row(PACK_HALF), row(PEER_HK), row(PEER_HK)]
    out_shape = [jax.ShapeDtypeStruct((t, D_MODEL), F32), jax.ShapeDtypeStruct((t, PACK_HALF), I32),
                 jax.ShapeDtypeStruct((t, PEER_HK), I32), jax.ShapeDtypeStruct((t, PEER_HK), F32)]
    args = [mixed2d, x2d, mod, mod, mod, norm2_g.reshape(1, -1), w_out, w_query, keys]
    if prev is not None:
        tp = prev[0].shape[0]
        prow = pl.BlockSpec((tp // steps, PEER_HK), lambda i: (i, 0))
        in_specs += [prow, prow]
        out_specs += [prow]
        out_shape += [jax.ShapeDtypeStruct((tp, PEER_HK), I32)]
        args += list(prev)
    if fin is not None:
        x1_f, peer_f, mod_f, rows_f, final_g = fin
        tf = x1_f.shape[0]
        frow = pl.BlockSpec((tf // steps, D_MODEL), lambda i: (i, 0))
        in_specs += [frow, frow, _mod_spec(5, rows_f, tf // steps), _const_spec((1, D_MODEL))]
        out_specs += [frow]
        out_shape += [jax.ShapeDtypeStruct((tf, D_MODEL), F32)]
        args += [x1_f, peer_f, mod_f, final_g.reshape(1, -1)]
    return pl.pallas_call(
        functools.partial(_post_body, prev is not None, fin is not None),
        grid=(steps,),
        in_specs=in_specs, out_specs=out_specs, out_shape=out_shape,
        compiler_params=pltpu.CompilerParams(vmem_limit_bytes=VMEM_LIMIT),
        name="post",
    )(*args)


SC_CORES = 2
SC_SUBCORES = 16
SC_LANES = 16
SC_WORKERS = SC_CORES * SC_SUBCORES
SC_TOKENS = 16
SC_SLOTS = 8
SC_BF16_GROUP = 4
PACK_HALF = D_MODEL // 2
SC_CHUNKS = PACK_HALF // SC_LANES
PROMPT_PARTS = 8
FIN_LAG = 3
ROW_TILE = 256


def _bf16_bits(v):
    return lax.bitcast_convert_type(v.astype(BF16).astype(F32), jnp.uint32)


def _pack_words(lo, hi):
    return lax.bitcast_convert_type((_bf16_bits(lo) >> 16) | _bf16_bits(hi), I32)


def _pack_body(x_ref, o_ref):
    o_ref[...] = _pack_words(x_ref[:, :PACK_HALF], x_ref[:, PACK_HALF:])


def _pack_table(tbl, rows=512):
    e = tbl.shape[0]
    return pl.pallas_call(
        _pack_body, grid=(e // rows,),
        in_specs=[pl.BlockSpec((rows, D_MODEL), lambda i: (i, 0))],
        out_specs=pl.BlockSpec((rows, PACK_HALF), lambda i: (i, 0)),
        out_shape=jax.ShapeDtypeStruct((e, PACK_HALF), I32), name="pack_table")(tbl)


def _tree_sum(terms):
    terms = list(terms)
    while len(terms) > 1:
        terms = [a + b for a, b in zip(terms[0::2], terms[1::2])] + terms[len(terms) & ~1:]
    return terms[0]


def _unpack_pair(w):
    lo = plsc.bitcast(lax.shift_left(w, jnp.full(w.shape, 16, I32)), F32)
    hi = plsc.bitcast(w & jnp.full(w.shape, -65536, I32), F32)
    return lo, hi


def _sc_mesh():
    return plsc.VectorSubcoreMesh(core_axis_name="c", subcore_axis_name="s")


def _sc_worker():
    return lax.axis_index("s") * SC_CORES + lax.axis_index("c")


def _sc_jobs(table_hbm, idx_v, buf, sem, compute):
    njobs = idx_v.shape[0] * PEER_HEADS

    def copy(j, slot):
        tt = j // PEER_HEADS
        h = j % PEER_HEADS
        rows = idx_v[tt, pl.ds(h * PEER_TOPK, PEER_TOPK)]
        return pltpu.make_async_copy(table_hbm.at[rows], buf.at[slot], sem.at[slot])

    for s in range(SC_SLOTS):
        copy(s, s).start()

    def group(g, c):
        for s in range(SC_SLOTS):
            j = g * SC_SLOTS + s
            copy(j, s).wait()
            compute(j // PEER_HEADS, j % PEER_HEADS, s)

            @pl.when(j + SC_SLOTS < njobs)
            def _next():
                copy(j + SC_SLOTS, s).start()
        return c

    lax.fori_loop(0, njobs // SC_SLOTS, group, 0)


def _peer_u_body(n_tok, idx_hbm, h2_hbm, u_hbm, pre_hbm, idx_v, h2_v, pre_v, ubuf, acc_v, sem):
    base = _sc_worker() * n_tok
    lane = lax.iota(I32, SC_LANES)

    def compute(tt, h, slot):
        def chunk(cg, accs):
            cs = [pl.ds((cg * SC_BF16_GROUP + i) * SC_LANES, SC_LANES) for i in range(SC_BF16_GROUP)]
            xs = [plsc.bitcast(h2_v[tt, c], BF16) for c in cs]
            out = []
            for k, a in enumerate(accs):
                part = _tree_sum([plsc.bitcast(ubuf[slot, k, c], BF16) * x for c, x in zip(cs, xs)])
                lo, hi = _unpack_pair(plsc.bitcast(part, I32))
                out.append(a + (lo + hi))
            return tuple(out)
        zero = jnp.zeros((SC_LANES,), F32)
        accs = lax.fori_loop(0, SC_CHUNKS // SC_BF16_GROUP, chunk, (zero,) * PEER_TOPK)
        for k, a in enumerate(accs):
            acc_v[k, :] = a
        tot = zero
        for j in range(SC_LANES):
            tot = tot + plsc.load_gather(acc_v, [lane, (lane + j) & (SC_LANES - 1)])
        pre_v[tt, pl.ds(h * PEER_TOPK, PEER_TOPK)] = tot

    tb = idx_v.shape[0]

    def block(bi, c):
        t0 = base + bi * tb
        pltpu.sync_copy(idx_hbm.at[pl.ds(t0, tb)], idx_v)
        pltpu.sync_copy(h2_hbm.at[pl.ds(t0, tb)], h2_v)
        _sc_jobs(u_hbm, idx_v, ubuf, sem, compute)
        pltpu.sync_copy(pre_v, pre_hbm.at[pl.ds(t0, tb)])
        return c

    lax.fori_loop(0, n_tok // tb, block, 0)


def _peer_v_body(n_tok, idx_hbm, coef_hbm, v_hbm, out_hbm, idx_v, coef_v, out_v, vbuf, sem):
    base = _sc_worker() * n_tok
    zero = jnp.zeros((SC_LANES,), F32)

    def compute(tt, h, slot):
        row = jnp.full((SC_LANES,), tt, I32)
        cb = [plsc.bitcast(plsc.load_gather(
                  coef_v, [row, jnp.full((SC_LANES,), h * PEER_TOPK + k, I32)]), BF16)
              for k in range(PEER_TOPK)]

        @plsc.parallel_loop(0, SC_CHUNKS, unroll=2)
        def _chunk(c):
            cs = pl.ds(c * SC_LANES, SC_LANES)
            prods = [plsc.bitcast(vbuf[slot, k, cs], BF16) * cb[k] for k in range(PEER_TOPK)]
            pairs = [_unpack_pair(plsc.bitcast(_tree_sum(prods[g:g + SC_BF16_GROUP]), I32))
                     for g in range(0, PEER_TOPK, SC_BF16_GROUP)]
            for half, off in ((0, 0), (1, PACK_HALF)):
                plsc.addupdate(out_v.at[tt, pl.ds(off + c * SC_LANES, SC_LANES)],
                               _tree_sum([p[half] for p in pairs]))

    tb = idx_v.shape[0]

    def block(bi, c):
        t0 = base + bi * tb
        pltpu.sync_copy(idx_hbm.at[pl.ds(t0, tb)], idx_v)
        pltpu.sync_copy(coef_hbm.at[pl.ds(t0, tb)], coef_v)

        def clear(i, cc):
            per_row = D_MODEL // SC_LANES
            out_v[i // per_row, pl.ds((i % per_row) * SC_LANES, SC_LANES)] = zero
            return cc
        lax.fori_loop(0, tb * (D_MODEL // SC_LANES), clear, 0)
        _sc_jobs(v_hbm, idx_v, vbuf, sem, compute)
        pltpu.sync_copy(out_v, out_hbm.at[pl.ds(t0, tb)])
        return c

    lax.fori_loop(0, n_tok // tb, block, 0)


def _peer_sc(body, idx, rows, table, out_width, name):
    t = idx.shape[0]
    assert t % SC_WORKERS == 0
    n_tok = t // SC_WORKERS
    tb = min(SC_TOKENS, n_tok)
    assert n_tok % tb == 0 and (tb * PEER_HEADS) % SC_SLOTS == 0
    return pl.kernel(
        functools.partial(body, n_tok),
        out_type=jax.ShapeDtypeStruct((t, out_width), F32),
        mesh=_sc_mesh(),
        scratch_types=[pltpu.VMEM((tb, PEER_HK), I32),
                       pltpu.VMEM((tb, rows.shape[1]), rows.dtype),
                       pltpu.VMEM((tb, out_width), F32),
                       pltpu.VMEM((SC_SLOTS, PEER_TOPK, PACK_HALF), I32)]
                      + ([pltpu.VMEM((PEER_TOPK, SC_LANES), F32)] if body is _peer_u_body else [])
                      + [pltpu.SemaphoreType.DMA((SC_SLOTS,))],
        compiler_params=pltpu.CompilerParams(needs_layout_passes=False),
        name=name,
    )(idx, rows, table)


def _coef_words(pre, gates):
    return _pack_words(*(gates * _gelu(pre),) * 2)


def _coef_body(pre_ref, gate_ref, coef_ref):
    coef_ref[...] = _coef_words(pre_ref[...], gate_ref[...])


def _coef(pre, gates, tm):
    t = pre.shape[0]
    row = pl.BlockSpec((tm, PEER_HK), lambda i: (i, 0))
    return pl.pallas_call(_coef_body, grid=(t // tm,), in_specs=[row, row], out_specs=row,
                          out_shape=jax.ShapeDtypeStruct((t, PEER_HK), I32), name="coef")(pre, gates)


def _final_body(x1_ref, peer_ref, g2_ref, fng_ref, y_ref):
    x2 = x1_ref[...] + _mod_rows(g2_ref) * peer_ref[...]
    y_ref[...] = x2 * lax.rsqrt(jnp.mean(x2 * x2, axis=-1, keepdims=True) + EPS) * fng_ref[...]


def _final(x1, peer_out, mod, rows_per_batch, final_g, tm):
    t = x1.shape[0]
    row = pl.BlockSpec((tm, D_MODEL), lambda i: (i, 0))
    return pl.pallas_call(
        _final_body, grid=(t // tm,),
        in_specs=[row, row, _mod_spec(5, rows_per_batch, tm), _const_spec((1, D_MODEL))],
        out_specs=row, out_shape=jax.ShapeDtypeStruct((t, D_MODEL), F32), name="final",
    )(x1, peer_out, mod, final_g.reshape(1, -1))


def _expert_gather_v(g, coef, expert_v):
    g["peer_out"] = _peer_sc(_peer_v_body, g["idx"], coef, expert_v, D_MODEL, "peer_v")


def _front(x, mod, conv_buf, s0, pool_buf, start, chunk, tm, wts, prev, fin):
    b, l, _ = x.shape
    t = b * l
    x2d = x.reshape(t, D_MODEL)
    if l >= tm:
        modx = mod.reshape(b, 6, 1, D_MODEL).transpose(1, 0, 2, 3)
    else:
        modx = jnp.repeat(mod.reshape(b, 6, D_MODEL), l, axis=0).transpose(1, 0, 2)
    outs = _inproj(x2d, modx, l, wts["norm1_g"], wts["w_cat"], tm)
    lp = -(-l // chunk) * chunk
    proj = {}
    for (name, w), a in zip(_IN_BLOCKS, outs):
        a = a.reshape(b, l, w)
        proj[name] = a if lp == l else jnp.pad(a, ((0, 0), (0, lp - l), (0, 0)))
    mixed, nconv, ns, npool = _mixer(proj, conv_buf, s0, pool_buf, start, l, chunk,
                                     wts["conv_w"], wts["a_log"], wts["dt_bias"], wts["dn_norm_g"],
                                     wts["w_pool"], wts["pool_scale"])
    mixed2d = mixed[:, :l].reshape(t, D_MODEL)
    res = _post(mixed2d, x2d, modx, l, wts["norm2_g"], wts["w_out"], wts["w_query"], wts["keys"], tm,
                prev=None if prev is None else (prev["pre"], prev["gates"]),
                fin=None if fin is None else (fin["x1"], fin["peer_out"], fin["mod"], fin["l"],
                                              wts["final_norm_g"]))
    x1, h2, idx, gates = res[:4]
    extra = list(res[4:])
    coef_prev = extra.pop(0) if prev is not None else None
    y_fin = extra.pop(0).reshape(fin["b"], fin["l"], D_MODEL) if fin is not None else None
    pre = _peer_sc(_peer_u_body, idx, h2, wts["expert_u"], PEER_HK, "peer_u")
    g = dict(x1=x1, idx=idx, gates=gates, pre=pre, mod=modx, b=b, l=l, tm=tm,
             states=(nconv, ns, npool))
    return g, coef_prev, y_fin


def kernel(x_prompt, x_sample, c_prompt, c_sample, state_conv, state_delta, state_pool, w_ada, b_ada, norm1_g, w_in, conv_w, a_log, dt_bias, dn_norm_g, w_pool, pool_scale, w_out, norm2_g, w_query, sub_keys, expert_u, expert_v, final_norm_g):
    bp = x_prompt.shape[0]
    bs = x_sample.shape[0]
    yp, ys = x_prompt, x_sample
    conv_p, delta_p, pool_p, conv_s, delta_s, pool_s = [], [], [], [], [], []
    zero_conv = jnp.zeros((bp, CONV_WIDTH - 1, QKV_WIDTH), F32)
    zero_delta = jnp.zeros((bp, DN_HEADS, DN_HEAD_DIM, DN_HEAD_DIM), F32)
    zero_pool = jnp.zeros((bp, POOL_BUF, POOL_WIDTH), F32)
    c_all = jnp.concatenate([c_prompt, c_sample], axis=0)
    for layer in range(DEPTH):
        wi = w_in[layer]
        o_b = QKV_WIDTH
        o_z = o_b + 2 * DN_HEADS
        w_ba = jnp.pad(wi[:, o_b:o_z], ((0, 0), (0, LANES - 2 * DN_HEADS)))
        w_cat = jnp.concatenate([wi[:, :o_b], wi[:, o_z:], w_ba], axis=1).astype(BF16)
        last = layer == DEPTH - 1
        wts = dict(
            norm1_g=norm1_g[layer], w_cat=w_cat, conv_w=conv_w[layer], a_log=a_log[layer],
            dt_bias=dt_bias[layer], dn_norm_g=dn_norm_g[layer], w_pool=w_pool[layer],
            pool_scale=pool_scale[layer], w_out=w_out[layer].astype(BF16), norm2_g=norm2_g[layer],
            w_query=w_query[layer].astype(BF16),
            keys=sub_keys[layer].reshape(2 * PEER_HEADS, PEER_NKEYS, PEER_KEY_HALF).astype(BF16),
            expert_u=_pack_table(expert_u[layer]), expert_v=_pack_table(expert_v[layer]),
            final_norm_g=final_norm_g if last else jnp.ones_like(final_norm_g))
        mod = _ada(c_all, w_ada[layer], b_ada[layer])
        assert last, "final norm is fused into the expert stage"
        step = bp // PROMPT_PARTS
        jobs = [(yp[b0:b0 + step], mod[b0:b0 + step], zero_conv[b0:b0 + step], zero_delta[b0:b0 + step],
                 zero_pool[b0:b0 + step], 0, DN_CHUNK) for b0 in range(0, bp, step)]
        jobs.append((ys, mod[bp:], state_conv[layer], state_delta[layer], state_pool[layer],
                     PAST_LEN, SUBLANES))
        groups = []
        for j, (xg, mg, cg, sg, pg, start, chunk) in enumerate(jobs):
            prev = groups[j - 1] if j >= 1 else None
            fin = groups[j - FIN_LAG] if j >= FIN_LAG else None
            if fin is not None and fin["x1"].shape[0] != xg.shape[0] * xg.shape[1]:
                fin = None
            g, coef_prev, y_fin = _front(xg, mg, cg, sg, pg, start, chunk, ROW_TILE, wts, prev, fin)
            if prev is not None:
                _expert_gather_v(prev, coef_prev, wts["expert_v"])
            if fin is not None:
                fin["y"] = y_fin
            groups.append(g)
        _expert_gather_v(groups[-1], _coef(groups[-1]["pre"], groups[-1]["gates"], ROW_TILE),
                         wts["expert_v"])
        done = []
        for g in groups:
            if "y" not in g:
                g["y"] = _final(g["x1"], g["peer_out"], g["mod"], g["l"], wts["final_norm_g"],
                                g["tm"]).reshape(g["b"], g["l"], D_MODEL)
            done.append((g["y"],) + g["states"])
        yp, cp, sp, pp = (jnp.concatenate(a, axis=0) for a in zip(*done[:-1]))
        ys, cs, ss, ps = done[-1]
        conv_p.append(cp)
        delta_p.append(sp)
        pool_p.append(pp)
        conv_s.append(cs)
        delta_s.append(ss)
        pool_s.append(ps)
    return (yp, ys, jnp.stack(conv_p), jnp.stack(delta_p), jnp.stack(pool_p),
            jnp.stack(conv_s), jnp.stack(delta_s), jnp.stack(pool_s))
```

```python
import functools

import jax
import jax.numpy as jnp
from jax import lax
from jax.experimental import pallas as pl
from jax.experimental.pallas import tpu as pltpu
from jax.experimental.pallas import tpu_sc as plsc

F32 = jnp.float32
BF16 = jnp.bfloat16
I32 = jnp.int32

D_MODEL = 1024
DEPTH = 1
PAST_LEN = 16384
DN_HEADS = 8
DN_HEAD_DIM = 128
DN_WIDTH = DN_HEADS * DN_HEAD_DIM
QKV_WIDTH = 3 * DN_WIDTH
CONV_WIDTH = 4
DN_CHUNK = 64
POOL_WINDOWS = (2, 4, 8, 16)
POOL_GROUP_DIM = 128
POOL_WIDTH = len(POOL_WINDOWS) * POOL_GROUP_DIM
POOL_OUT_GROUP = D_MODEL // len(POOL_WINDOWS)
POOL_BUF = max(POOL_WINDOWS) - 1
PEER_HEADS = 8
PEER_NKEYS = 128
PEER_TOPK = 16
PEER_KEY_HALF = 128
PEER_HK = PEER_HEADS * PEER_TOPK
EPS = 1e-6

LANES = 128
SUBLANES = 8
CONV_PAD = SUBLANES
POOL_PAD = 16
VMEM_LIMIT = 56 * 1024 * 1024

NT_DIMS = (((1,), (1,)), ((), ()))
TN_DIMS = (((0,), (0,)), ((), ()))


def _dot(a, b):
    return jnp.dot(a.astype(BF16), b.astype(BF16), preferred_element_type=F32)


def _dot_nt(a, b):
    return lax.dot_general(a.astype(BF16), b.astype(BF16), NT_DIMS, preferred_element_type=F32)


def _split3(x):
    hi = x.astype(BF16)
    r1 = x - hi.astype(F32)
    mid = r1.astype(BF16)
    lo = (r1 - mid.astype(F32)).astype(BF16)
    return hi, mid, lo


def _silu(x):
    return x * jax.nn.sigmoid(x)


def _gelu(x):
    return 0.5 * x * (1.0 + lax.erf(x * (0.5 ** 0.5)))


def _softplus(x):
    return jnp.maximum(x, 0.0) + jnp.log(1.0 + jnp.exp(-jnp.abs(x)))


def _mod_rows(ref):
    m = ref[...]
    return m.reshape(m.shape[-2], m.shape[-1])


def _mod_spec(k, rows_per_batch, tm):
    if rows_per_batch >= tm:
        tiles = rows_per_batch // tm
        return pl.BlockSpec((1, 1, 1, D_MODEL), lambda i, *_: (k, i // tiles, 0, 0))
    return pl.BlockSpec((1, tm, D_MODEL), lambda i, *_: (k, i, 0))


def _const_spec(shape):
    nd = len(shape)
    return pl.BlockSpec(shape, lambda *_: (0,) * nd)


def _ada_body(c_ref, w_ref, b_ref, o_ref):
    o_ref[...] = _dot(_silu(c_ref[...]), w_ref[...]) + b_ref[...]


def _ada(c, w_ada, b_ada):
    n = c.shape[0]
    return pl.pallas_call(
        _ada_body,
        grid=(6,),
        in_specs=[pl.BlockSpec((n, D_MODEL), lambda j: (0, 0)),
                  pl.BlockSpec((D_MODEL, D_MODEL), lambda j: (0, j)),
                  pl.BlockSpec((1, D_MODEL), lambda j: (0, j))],
        out_specs=pl.BlockSpec((n, D_MODEL), lambda j: (0, j)),
        out_shape=jax.ShapeDtypeStruct((n, 6 * D_MODEL), F32),
        name="ada",
    )(c, w_ada, b_ada.reshape(1, -1))


_IN_BLOCKS = (("qkv", QKV_WIDTH), ("z", DN_WIDTH), ("pool", POOL_WIDTH),
              ("ga", D_MODEL), ("gb", D_MODEL), ("ba", LANES))
_IN_TOTAL = sum(w for _, w in _IN_BLOCKS)
_IN_COL_CHUNK = 512


def _inproj_body(x_ref, sc_ref, sh_ref, g_ref, w_ref, *out_refs):
    x = x_ref[...]
    y = x * lax.rsqrt(jnp.mean(x * x, axis=-1, keepdims=True) + EPS) * g_ref[...]
    h = (y * (1.0 + _mod_rows(sc_ref)) + _mod_rows(sh_ref)).astype(BF16)
    off = 0
    for (_, width), o_ref in zip(_IN_BLOCKS, out_refs):
        for c0 in range(0, width, _IN_COL_CHUNK):
            cw = min(_IN_COL_CHUNK, width - c0)
            o_ref[:, c0:c0 + cw] = jnp.dot(h, w_ref[:, off + c0:off + c0 + cw],
                                           preferred_element_type=F32)
        off += width


def _inproj(x2d, mod, rows_per_batch, norm_g, w_cat, tm):
    t = x2d.shape[0]
    row = lambda w: pl.BlockSpec((tm, w), lambda i: (i, 0))
    return pl.pallas_call(
        _inproj_body,
        grid=(t // tm,),
        in_specs=[row(D_MODEL), _mod_spec(1, rows_per_batch, tm), _mod_spec(0, rows_per_batch, tm),
                  _const_spec((1, D_MODEL)),
                  pl.BlockSpec((D_MODEL, _IN_TOTAL), lambda i: (0, 0), pipeline_mode=pl.Buffered(1))],
        out_specs=[row(w) for _, w in _IN_BLOCKS],
        out_shape=[jax.ShapeDtypeStruct((t, w), F32) for _, w in _IN_BLOCKS],
        compiler_params=pltpu.CompilerParams(vmem_limit_bytes=VMEM_LIMIT),
        name="inproj",
    )(x2d, mod, mod, norm_g.reshape(1, -1), w_cat)


def _mixer_body(C, Lv, start,
                qkv_ref, ba_ref, z_ref, pin_ref, ga_ref, gb_ref, cbuf_ref, s0_ref, pbuf_ref,
                convw_ref, alog_ref, dtb_ref, dng_ref, wpool_ref, pscale_ref,
                mixed_ref, nconv_ref, ns_ref, npool_ref,
                xp_scr, act_scr, s_scr, pp_scr, odn_scr):
    n = pl.program_id(1)
    last = pl.num_programs(1) - 1

    @pl.when(n == 0)
    def _load_state():
        xp_scr[0:CONV_PAD, :] = cbuf_ref[0]
        pp_scr[0:POOL_PAD, :] = pbuf_ref[0]
        s_scr[...] = s0_ref[0]

    xp_scr[CONV_PAD:CONV_PAD + C, :] = qkv_ref[0]
    for c0 in range(0, QKV_WIDTH, 512):
        cs = slice(c0, c0 + 512)
        y = xp_scr[CONV_PAD:CONV_PAD + C, cs] * convw_ref[CONV_WIDTH - 1:CONV_WIDTH, cs]
        for k in range(CONV_WIDTH - 1):
            r0 = CONV_PAD - (CONV_WIDTH - 1) + k
            y = y + xp_scr[r0:r0 + C, cs] * convw_ref[k:k + 1, cs]
        act_scr[:, cs] = _silu(y)

    ba = ba_ref[0]
    lane = lax.broadcasted_iota(I32, (C, LANES), 1)
    beta_all = jax.nn.sigmoid(ba)
    g_all = -jnp.exp(alog_ref[...]) * _softplus(ba + dtb_ref[...])
    if Lv < C:
        valid = lax.broadcasted_iota(I32, (C, LANES), 0) < Lv
        beta_all = jnp.where(valid, beta_all, 0.0)
        g_all = jnp.where(valid, g_all, 0.0)
    ii = lax.broadcasted_iota(I32, (C, C), 0)
    jj = lax.broadcasted_iota(I32, (C, C), 1)
    causal = ii >= jj
    strict = ii > jj
    tril = jnp.where(causal, 1.0, 0.0).astype(BF16)
    eye = jnp.where(ii == jj, 1.0, 0.0)
    gc_all = sum(jnp.dot(tril, part, preferred_element_type=F32) for part in _split3(g_all))
    if C < LANES:
        gc_sq = jnp.concatenate([gc_all, jnp.zeros((LANES - C, LANES), F32)], axis=0)
    else:
        gc_sq = gc_all
    gc_t = gc_sq.T

    H = range(DN_HEADS)
    hsl = [slice(h * DN_HEAD_DIM, (h + 1) * DN_HEAD_DIM) for h in H]
    beta = [jnp.sum(jnp.where(lane == h, beta_all, 0.0), axis=1, keepdims=True) for h in H]
    gcol = [jnp.sum(jnp.where(lane == DN_HEADS + h, gc_all, 0.0), axis=1, keepdims=True) for h in H]
    grow = [gc_t[DN_HEADS + h:DN_HEADS + h + 1, 0:C] for h in H]
    glast = [g[C - 1:C, :] for g in gcol]
    q = [act_scr[:, hsl[h]] for h in H]
    k = [act_scr[:, DN_WIDTH + h * DN_HEAD_DIM:DN_WIDTH + (h + 1) * DN_HEAD_DIM] for h in H]
    v = [act_scr[:, 2 * DN_WIDTH + h * DN_HEAD_DIM:2 * DN_WIDTH + (h + 1) * DN_HEAD_DIM] for h in H]
    q = [x * lax.rsqrt(jnp.sum(x * x, axis=-1, keepdims=True) + EPS) * (DN_HEAD_DIM ** -0.5) for x in q]
    k = [x * lax.rsqrt(jnp.sum(x * x, axis=-1, keepdims=True) + EPS) for x in k]
    kb = [k[h] * beta[h] for h in H]
    vb = [v[h] * beta[h] for h in H]
    decay = [jnp.where(causal, jnp.exp(jnp.where(causal, gcol[h] - grow[h], 0.0)), 0.0) for h in H]
    lower = [jnp.where(strict, _dot_nt(kb[h], k[h]) * decay[h], 0.0) for h in H]
    ainv = [eye - x for x in lower]
    pw = lower
    p = 1
    while 2 * p < C:
        pw = [_dot(x, x) for x in pw]
        ainv = [ainv[h] + _dot(ainv[h], pw[h]) for h in H]
        p *= 2
    sol = [_dot(ainv[h], jnp.concatenate([vb[h], kb[h] * jnp.exp(gcol[h])], axis=1)) for h in H]
    qk = [_dot_nt(q[h], k[h]) * decay[h] for h in H]
    k_tail = [k[h] * jnp.exp(glast[h] - gcol[h]) for h in H]
    S = [s_scr[h] for h in H]
    v_new = [sol[h][:, :DN_HEAD_DIM] - _dot(sol[h][:, DN_HEAD_DIM:], S[h]) for h in H]
    o = [_dot(q[h] * jnp.exp(gcol[h]), S[h]) + _dot(qk[h], v_new[h]) for h in H]
    for h in H:
        s_scr[h] = S[h] * jnp.exp(glast[h]) + lax.dot_general(
            k_tail[h].astype(BF16), v_new[h].astype(BF16), TN_DIMS, preferred_element_type=F32)
    for h in H:
        zf = z_ref[0, :, hsl[h]]
        odn_scr[:, hsl[h]] = (o[h] * lax.rsqrt(jnp.mean(o[h] * o[h], axis=-1, keepdims=True) + EPS)
                              * dng_ref[...] * _silu(zf))

    pp_scr[POOL_PAD:POOL_PAD + C, :] = pin_ref[0]
    pos = start + n * C + lax.broadcasted_iota(I32, (C, 1), 0)
    for gi, win in enumerate(POOL_WINDOWS):
        gs = slice(gi * POOL_GROUP_DIM, (gi + 1) * POOL_GROUP_DIM)
        xg = pp_scr[POOL_PAD:POOL_PAD + C, gs]
        ssum = xg
        for sft in range(1, win):
            ssum = ssum + pp_scr[POOL_PAD - sft:POOL_PAD - sft + C, gs]
        cnt = jnp.minimum(pos + 1, win).astype(F32)
        pooled = ssum / cnt - xg
        os_ = slice(gi * POOL_OUT_GROUP, (gi + 1) * POOL_OUT_GROUP)
        yp = _dot(pooled, wpool_ref[gi]) * pscale_ref[:, os_]
        mixed_ref[0, :, os_] = (jax.nn.sigmoid(ga_ref[0, :, os_]) * odn_scr[:, os_]
                                + jax.nn.sigmoid(gb_ref[0, :, os_]) * yp)

    @pl.when(n == last)
    def _store_state():
        nconv_ref[0] = xp_scr[Lv + CONV_PAD - (CONV_WIDTH - 1):Lv + CONV_PAD, :]
        npool_ref[0] = pp_scr[Lv + POOL_PAD - POOL_BUF:Lv + POOL_PAD, :]
        ns_ref[0] = s_scr[...]

    xp_scr[0:CONV_PAD, :] = xp_scr[C:C + CONV_PAD, :]
    pp_scr[0:POOL_PAD, :] = pp_scr[C:C + POOL_PAD, :]


def _mixer(proj, conv_buf, s0, pool_buf, start, seq_len, C,
           conv_w, a_log, dt_bias, dn_norm_g, w_pool, pool_scale):
    b, lp, _ = proj["qkv"].shape
    nchunks = lp // C
    lv = seq_len - (nchunks - 1) * C
    cbuf = jnp.pad(conv_buf, ((0, 0), (CONV_PAD - (CONV_WIDTH - 1), 0), (0, 0)))
    pbuf = jnp.pad(pool_buf, ((0, 0), (POOL_PAD - POOL_BUF, 0), (0, 0)))
    lane_pad = lambda a: jnp.pad(a.reshape(1, -1), ((0, 0), (DN_HEADS, LANES - 2 * DN_HEADS)))
    chunk = lambda w: pl.BlockSpec((1, C, w), lambda i, j: (i, j, 0))
    state = lambda *s: pl.BlockSpec((1,) + s, lambda i, j: (i,) + (0,) * len(s))
    return pl.pallas_call(
        functools.partial(_mixer_body, C, lv, start),
        grid=(b, nchunks),
        in_specs=[chunk(QKV_WIDTH), chunk(LANES), chunk(DN_WIDTH), chunk(POOL_WIDTH),
                  chunk(D_MODEL), chunk(D_MODEL),
                  state(CONV_PAD, QKV_WIDTH), state(DN_HEADS, DN_HEAD_DIM, DN_HEAD_DIM),
                  state(POOL_PAD, POOL_WIDTH),
                  _const_spec((CONV_WIDTH, QKV_WIDTH)), _const_spec((1, LANES)), _const_spec((1, LANES)),
                  _const_spec((1, DN_HEAD_DIM)),
                  _const_spec((len(POOL_WINDOWS), POOL_GROUP_DIM, POOL_OUT_GROUP)),
                  _const_spec((1, D_MODEL))],
        out_specs=[chunk(D_MODEL), state(CONV_WIDTH - 1, QKV_WIDTH),
                   state(DN_HEADS, DN_HEAD_DIM, DN_HEAD_DIM), state(POOL_BUF, POOL_WIDTH)],
        out_shape=[jax.ShapeDtypeStruct((b, lp, D_MODEL), F32),
                   jax.ShapeDtypeStruct((b, CONV_WIDTH - 1, QKV_WIDTH), F32),
                   jax.ShapeDtypeStruct((b, DN_HEADS, DN_HEAD_DIM, DN_HEAD_DIM), F32),
                   jax.ShapeDtypeStruct((b, POOL_BUF, POOL_WIDTH), F32)],
        scratch_shapes=[pltpu.VMEM((CONV_PAD + C + CONV_PAD, QKV_WIDTH), F32),
                        pltpu.VMEM((C, QKV_WIDTH), F32),
                        pltpu.VMEM((DN_HEADS, DN_HEAD_DIM, DN_HEAD_DIM), F32),
                        pltpu.VMEM((POOL_PAD + C + POOL_PAD, POOL_WIDTH), F32),
                        pltpu.VMEM((C, DN_WIDTH), F32)],
        compiler_params=pltpu.CompilerParams(dimension_semantics=("arbitrary", "arbitrary"),
                                             vmem_limit_bytes=VMEM_LIMIT),
        name="mixer",
    )(proj["qkv"], proj["ba"], proj["z"], proj["pool"], proj["ga"], proj["gb"], cbuf, s0, pbuf,
      conv_w, lane_pad(a_log), lane_pad(dt_bias), dn_norm_g.reshape(1, -1), w_pool,
      pool_scale.reshape(1, -1))


def _top16(s, ids, payload=None):
    big = float(2 ** 24)
    vals, sel, pays = [], [], []
    for _ in range(PEER_TOPK):
        m = jnp.max(s, axis=0, keepdims=True)
        am = jnp.min(jnp.where(s == m, ids, big), axis=0, keepdims=True)
        hit = ids == am
        if payload is not None:
            pays.append(jnp.max(jnp.where(hit, payload, -1.0), axis=0, keepdims=True))
        s = jnp.where(hit, -jnp.inf, s)
        vals.append(m)
        sel.append(am)
    out = (jnp.concatenate(vals, axis=0), jnp.concatenate(sel, axis=0))
    if payload is not None:
        out += (jnp.concatenate(pays, axis=0),)
    return out


_CAND_EDGE = 4


def _post_body(has_prev, has_fin, mixed_ref, x_ref, g1_ref, sc2_ref, sh2_ref, n2g_ref, wout_ref,
               wq_ref, keys_ref, *refs):
    refs = list(refs)
    prev_in = [refs.pop(0) for _ in range(2 if has_prev else 0)]
    fin_in = [refs.pop(0) for _ in range(4 if has_fin else 0)]
    x1_ref, h2_ref, idx_ref, gate_ref = refs[:4]
    extra_out = refs[4:]
    if has_prev:
        pre_ref, pgate_ref = prev_in
        extra_out.pop(0)[...] = _coef_words(pre_ref[...], pgate_ref[...])
    if has_fin:
        _final_body(*fin_in, extra_out.pop(0))
    tm = x_ref.shape[0]
    x1 = x_ref[...] + _mod_rows(g1_ref) * _dot(mixed_ref[...], wout_ref[...])
    x1_ref[...] = x1
    y = x1 * lax.rsqrt(jnp.mean(x1 * x1, axis=-1, keepdims=True) + EPS) * n2g_ref[...]
    h2 = y * (1.0 + _mod_rows(sc2_ref)) + _mod_rows(sh2_ref)
    h2_ref[...] = _pack_words(h2[:, :PACK_HALF], h2[:, PACK_HALF:])
    q = _dot(h2, wq_ref[...])

    K = PEER_TOPK
    key_id = lax.broadcasted_iota(I32, (PEER_NKEYS, 1), 0).astype(F32)
    r16 = lax.broadcasted_iota(I32, (K, 1), 0)
    cand_id = jnp.concatenate([(a * K + r16) for a in range(_CAND_EDGE)]
                              + [(r16 * K + b) for b in range(_CAND_EDGE)], axis=0).astype(F32)
    dup = r16 < _CAND_EDGE
    idx_rows, gate_rows = [], []
    for h in range(PEER_HEADS):
        half = []
        for p in range(2):
            c0 = (h * 2 + p) * PEER_KEY_HALF
            st = _dot_nt(keys_ref[h * 2 + p], q[:, c0:c0 + PEER_KEY_HALF])
            half.append(_top16(st, key_id))
        (s1, i1), (s2, i2) = half
        cand = jnp.concatenate(
            [s1[a:a + 1] + s2 for a in range(_CAND_EDGE)]
            + [jnp.where(dup, -jnp.inf, s1 + s2[b:b + 1]) for b in range(_CAND_EDGE)], axis=0)
        cidx = jnp.concatenate(
            [i1[a:a + 1] * PEER_NKEYS + i2 for a in range(_CAND_EDGE)]
            + [i1 * PEER_NKEYS + i2[b:b + 1] for b in range(_CAND_EDGE)], axis=0)
        best, _, eidx = _top16(cand, cand_id, cidx)
        e = jnp.exp(best - best[0:1])
        gate_rows.append(e / jnp.sum(e, axis=0, keepdims=True))
        idx_rows.append(eidx)
    idx_ref[...] = jnp.concatenate(idx_rows, axis=0).T.astype(I32)
    gate_ref[...] = jnp.concatenate(gate_rows, axis=0).T


def _post(mixed2d, x2d, mod, rows_per_batch, norm2_g, w_out, w_query, keys, tm, prev=None, fin=None):
    t = x2d.shape[0]
    steps = t // tm
    row = lambda w: pl.BlockSpec((tm, w), lambda i: (i, 0))
    in_specs = [row(D_MODEL), row(D_MODEL),
                _mod_spec(2, rows_per_batch, tm), _mod_spec(4, rows_per_batch, tm),
                _mod_spec(3, rows_per_batch, tm), _const_spec((1, D_MODEL)),
                _const_spec((D_MODEL, D_MODEL)), _const_spec((D_MODEL, 2 * PEER_HEADS * PEER_KEY_HALF)),
                _const_spec((2 * PEER_HEADS, PEER_NKEYS, PEER_KEY_HALF))]
    out_specs = [row(D_MODEL), row(PACK_HALF), row(PEER_HK), row(PEER_HK)]
    out_shape = [jax.ShapeDtypeStruct((t, D_MODEL), F32), jax.ShapeDtypeStruct((t, PACK_HALF), I32),
                 jax.ShapeDtypeStruct((t, PEER_HK), I32), jax.ShapeDtypeStruct((t, PEER_HK), F32)]
    args = [mixed2d, x2d, mod, mod, mod, norm2_g.reshape(1, -1), w_out, w_query, keys]
    if prev is not None:
        tp = prev[0].shape[0]
        prow = pl.BlockSpec((tp // steps, PEER_HK), lambda i: (i, 0))
        in_specs += [prow, prow]
        out_specs += [prow]
        out_shape += [jax.ShapeDtypeStruct((tp, PEER_HK), I32)]
        args += list(prev)
    if fin is not None:
        x1_f, peer_f, mod_f, rows_f, final_g = fin
        tf = x1_f.shape[0]
        frow = pl.BlockSpec((tf // steps, D_MODEL), lambda i: (i, 0))
        in_specs += [frow, frow, _mod_spec(5, rows_f, tf // steps), _const_spec((1, D_MODEL))]
        out_specs += [frow]
        out_shape += [jax.ShapeDtypeStruct((tf, D_MODEL), F32)]
        args += [x1_f, peer_f, mod_f, final_g.reshape(1, -1)]
    return pl.pallas_call(
        functools.partial(_post_body, prev is not None, fin is not None),
        grid=(steps,),
        in_specs=in_specs, out_specs=out_specs, out_shape=out_shape,
        compiler_params=pltpu.CompilerParams(vmem_limit_bytes=VMEM_LIMIT),
        name="post",
    )(*args)


SC_CORES = 2
SC_SUBCORES = 16
SC_LANES = 16
SC_WORKERS = SC_CORES * SC_SUBCORES
SC_TOKENS = 16
SC_SLOTS = 4
SC_BF16_GROUP = 4
PACK_HALF = D_MODEL // 2
SC_CHUNKS = PACK_HALF // SC_LANES
PROMPT_PARTS = 8
SEQ_SPLITS = 2
FIN_LAG = 3
ROW_TILE = 256


def _bf16_bits(v):
    return lax.bitcast_convert_type(v.astype(BF16).astype(F32), jnp.uint32)


def _pack_words(lo, hi):
    return lax.bitcast_convert_type((_bf16_bits(lo) >> 16) | _bf16_bits(hi), I32)


def _pack_body(x_ref, o_ref):
    o_ref[...] = _pack_words(x_ref[:, :PACK_HALF], x_ref[:, PACK_HALF:])


def _pack_table(tbl, rows=512):
    e = tbl.shape[0]
    return pl.pallas_call(
        _pack_body, grid=(e // rows,),
        in_specs=[pl.BlockSpec((rows, D_MODEL), lambda i: (i, 0))],
        out_specs=pl.BlockSpec((rows, PACK_HALF), lambda i: (i, 0)),
        out_shape=jax.ShapeDtypeStruct((e, PACK_HALF), I32), name="pack_table")(tbl)


def _tree_sum(terms):
    terms = list(terms)
    while len(terms) > 1:
        terms = [a + b for a, b in zip(terms[0::2], terms[1::2])] + terms[len(terms) & ~1:]
    return terms[0]


def _unpack_pair(w):
    lo = plsc.bitcast(lax.shift_left(w, jnp.full(w.shape, 16, I32)), F32)
    hi = plsc.bitcast(w & jnp.full(w.shape, -65536, I32), F32)
    return lo, hi


def _sc_mesh():
    return plsc.VectorSubcoreMesh(core_axis_name="c", subcore_axis_name="s")


def _sc_worker():
    return lax.axis_index("s") * SC_CORES + lax.axis_index("c")


def _sc_jobs(table_hbm, idx_v, buf, sem, compute):
    njobs = idx_v.shape[0] * PEER_HEADS

    def copy(j, slot):
        tt = j // PEER_HEADS
        h = j % PEER_HEADS
        rows = idx_v[tt, pl.ds(h * PEER_TOPK, PEER_TOPK)]
        return pltpu.make_async_copy(table_hbm.at[rows], buf.at[slot], sem.at[slot])

    for s in range(SC_SLOTS):
        copy(s, s).start()

    def group(g, c):
        for s in range(SC_SLOTS):
            j = g * SC_SLOTS + s
            copy(j, s).wait()
            compute(j // PEER_HEADS, j % PEER_HEADS, s)

            @pl.when(j + SC_SLOTS < njobs)
            def _next():
                copy(j + SC_SLOTS, s).start()
        return c

    lax.fori_loop(0, njobs // SC_SLOTS, group, 0)


def _peer_u_body(n_tok, idx_hbm, h2_hbm, u_hbm, pre_hbm, idx_v, h2_v, pre_v, ubuf, acc_v, sem):
    base = _sc_worker() * n_tok
    lane = lax.iota(I32, SC_LANES)

    def compute(tt, h, slot):
        def chunk(cg, accs):
            cs = [pl.ds((cg * SC_BF16_GROUP + i) * SC_LANES, SC_LANES) for i in range(SC_BF16_GROUP)]
            xs = [plsc.bitcast(h2_v[tt, c], BF16) for c in cs]
            out = []
            for k, a in enumerate(accs):
                part = _tree_sum([plsc.bitcast(ubuf[slot, k, c], BF16) * x for c, x in zip(cs, xs)])
                lo, hi = _unpack_pair(plsc.bitcast(part, I32))
                out.append(a + (lo + hi))
            return tuple(out)
        zero = jnp.zeros((SC_LANES,), F32)
        accs = lax.fori_loop(0, SC_CHUNKS // SC_BF16_GROUP, chunk, (zero,) * PEER_TOPK)
        for k, a in enumerate(accs):
            acc_v[k, :] = a
        tot = zero
        for j in range(SC_LANES):
            tot = tot + plsc.load_gather(acc_v, [lane, (lane + j) & (SC_LANES - 1)])
        pre_v[tt, pl.ds(h * PEER_TOPK, PEER_TOPK)] = tot

    tb = idx_v.shape[0]

    def block(bi, c):
        t0 = base + bi * tb
        pltpu.sync_copy(idx_hbm.at[pl.ds(t0, tb)], idx_v)
        pltpu.sync_copy(h2_hbm.at[pl.ds(t0, tb)], h2_v)
        _sc_jobs(u_hbm, idx_v, ubuf, sem, compute)
        pltpu.sync_copy(pre_v, pre_hbm.at[pl.ds(t0, tb)])
        return c

    lax.fori_loop(0, n_tok // tb, block, 0)


def _peer_v_body(n_tok, idx_hbm, coef_hbm, v_hbm, out_hbm, idx_v, coef_v, out_v, vbuf, sem):
    base = _sc_worker() * n_tok
    zero = jnp.zeros((SC_LANES,), F32)

    def compute(tt, h, slot):
        row = jnp.full((SC_LANES,), tt, I32)
        cb = [plsc.bitcast(plsc.load_gather(
                  coef_v, [row, jnp.full((SC_LANES,), h * PEER_TOPK + k, I32)]), BF16)
              for k in range(PEER_TOPK)]

        @plsc.parallel_loop(0, SC_CHUNKS, unroll=2)
        def _chunk(c):
            cs = pl.ds(c * SC_LANES, SC_LANES)
            prods = [plsc.bitcast(vbuf[slot, k, cs], BF16) * cb[k] for k in range(PEER_TOPK)]
            pairs = [_unpack_pair(plsc.bitcast(_tree_sum(prods[g:g + SC_BF16_GROUP]), I32))
                     for g in range(0, PEER_TOPK, SC_BF16_GROUP)]
            for half, off in ((0, 0), (1, PACK_HALF)):
                plsc.addupdate(out_v.at[tt, pl.ds(off + c * SC_LANES, SC_LANES)],
                               _tree_sum([p[half] for p in pairs]))

    tb = idx_v.shape[0]

    def block(bi, c):
        t0 = base + bi * tb
        pltpu.sync_copy(idx_hbm.at[pl.ds(t0, tb)], idx_v)
        pltpu.sync_copy(coef_hbm.at[pl.ds(t0, tb)], coef_v)

        def clear(i, cc):
            per_row = D_MODEL // SC_LANES
            out_v[i // per_row, pl.ds((i % per_row) * SC_LANES, SC_LANES)] = zero
            return cc
        lax.fori_loop(0, tb * (D_MODEL // SC_LANES), clear, 0)
        _sc_jobs(v_hbm, idx_v, vbuf, sem, compute)
        pltpu.sync_copy(out_v, out_hbm.at[pl.ds(t0, tb)])
        return c

    lax.fori_loop(0, n_tok // tb, block, 0)


def _peer_sc(body, idx, rows, table, out_width, name):
    t = idx.shape[0]
    assert t % SC_WORKERS == 0
    n_tok = t // SC_WORKERS
    tb = min(SC_TOKENS, n_tok)
    assert n_tok % tb == 0 and (tb * PEER_HEADS) % SC_SLOTS == 0
    return pl.kernel(
        functools.partial(body, n_tok),
        out_type=jax.ShapeDtypeStruct((t, out_width), F32),
        mesh=_sc_mesh(),
        scratch_types=[pltpu.VMEM((tb, PEER_HK), I32),
                       pltpu.VMEM((tb, rows.shape[1]), rows.dtype),
                       pltpu.VMEM((tb, out_width), F32),
                       pltpu.VMEM((SC_SLOTS, PEER_TOPK, PACK_HALF), I32)]
                      + ([pltpu.VMEM((PEER_TOPK, SC_LANES), F32)] if body is _peer_u_body else [])
                      + [pltpu.SemaphoreType.DMA((SC_SLOTS,))],
        compiler_params=pltpu.CompilerParams(needs_layout_passes=False),
        name=name,
    )(idx, rows, table)


def _coef_words(pre, gates):
    return _pack_words(*(gates * _gelu(pre),) * 2)


def _coef_body(pre_ref, gate_ref, coef_ref):
    coef_ref[...] = _coef_words(pre_ref[...], gate_ref[...])


def _coef(pre, gates, tm):
    t = pre.shape[0]
    row = pl.BlockSpec((tm, PEER_HK), lambda i: (i, 0))
    return pl.pallas_call(_coef_body, grid=(t // tm,), in_specs=[row, row], out_specs=row,
                          out_shape=jax.ShapeDtypeStruct((t, PEER_HK), I32), name="coef")(pre, gates)


def _final_body(x1_ref, peer_ref, g2_ref, fng_ref, y_ref):
    x2 = x1_ref[...] + _mod_rows(g2_ref) * peer_ref[...]
    y_ref[...] = x2 * lax.rsqrt(jnp.mean(x2 * x2, axis=-1, keepdims=True) + EPS) * fng_ref[...]


def _final(x1, peer_out, mod, rows_per_batch, final_g, tm):
    t = x1.shape[0]
    row = pl.BlockSpec((tm, D_MODEL), lambda i: (i, 0))
    return pl.pallas_call(
        _final_body, grid=(t // tm,),
        in_specs=[row, row, _mod_spec(5, rows_per_batch, tm), _const_spec((1, D_MODEL))],
        out_specs=row, out_shape=jax.ShapeDtypeStruct((t, D_MODEL), F32), name="final",
    )(x1, peer_out, mod, final_g.reshape(1, -1))


def _expert_gather_v(g, coef, expert_v):
    g["peer_out"] = _peer_sc(_peer_v_body, g["idx"], coef, expert_v, D_MODEL, "peer_v")


def _front(x, mod, conv_buf, s0, pool_buf, start, chunk, tm, wts, prev, fin):
    b, l, _ = x.shape
    t = b * l
    x2d = x.reshape(t, D_MODEL)
    if l >= tm:
        modx = mod.reshape(b, 6, 1, D_MODEL).transpose(1, 0, 2, 3)
    else:
        modx = jnp.repeat(mod.reshape(b, 6, D_MODEL), l, axis=0).transpose(1, 0, 2)
    outs = _inproj(x2d, modx, l, wts["norm1_g"], wts["w_cat"], tm)
    lp = -(-l // chunk) * chunk
    proj = {}
    for (name, w), a in zip(_IN_BLOCKS, outs):
        a = a.reshape(b, l, w)
        proj[name] = a if lp == l else jnp.pad(a, ((0, 0), (0, lp - l), (0, 0)))
    mixed, nconv, ns, npool = _mixer(proj, conv_buf, s0, pool_buf, start, l, chunk,
                                     wts["conv_w"], wts["a_log"], wts["dt_bias"], wts["dn_norm_g"],
                                     wts["w_pool"], wts["pool_scale"])
    mixed2d = mixed[:, :l].reshape(t, D_MODEL)
    res = _post(mixed2d, x2d, modx, l, wts["norm2_g"], wts["w_out"], wts["w_query"], wts["keys"], tm,
                prev=None if prev is None else (prev["pre"], prev["gates"]),
                fin=None if fin is None else (fin["x1"], fin["peer_out"], fin["mod"], fin["l"],
                                              wts["final_norm_g"]))
    x1, h2, idx, gates = res[:4]
    extra = list(res[4:])
    coef_prev = extra.pop(0) if prev is not None else None
    y_fin = extra.pop(0).reshape(fin["b"], fin["l"], D_MODEL) if fin is not None else None
    pre = _peer_sc(_peer_u_body, idx, h2, wts["expert_u"], PEER_HK, "peer_u")
    g = dict(x1=x1, idx=idx, gates=gates, pre=pre, mod=modx, b=b, l=l, tm=tm,
             states=(nconv, ns, npool))
    return g, coef_prev, y_fin


def kernel(x_prompt, x_sample, c_prompt, c_sample, state_conv, state_delta, state_pool, w_ada, b_ada, norm1_g, w_in, conv_w, a_log, dt_bias, dn_norm_g, w_pool, pool_scale, w_out, norm2_g, w_query, sub_keys, expert_u, expert_v, final_norm_g):
    bp = x_prompt.shape[0]
    bs = x_sample.shape[0]
    yp, ys = x_prompt, x_sample
    conv_p, delta_p, pool_p, conv_s, delta_s, pool_s = [], [], [], [], [], []
    zero_conv = jnp.zeros((bp, CONV_WIDTH - 1, QKV_WIDTH), F32)
    zero_delta = jnp.zeros((bp, DN_HEADS, DN_HEAD_DIM, DN_HEAD_DIM), F32)
    zero_pool = jnp.zeros((bp, POOL_BUF, POOL_WIDTH), F32)
    c_all = jnp.concatenate([c_prompt, c_sample], axis=0)
    for layer in range(DEPTH):
        wi = w_in[layer]
        o_b = QKV_WIDTH
        o_z = o_b + 2 * DN_HEADS
        w_ba = jnp.pad(wi[:, o_b:o_z], ((0, 0), (0, LANES - 2 * DN_HEADS)))
        w_cat = jnp.concatenate([wi[:, :o_b], wi[:, o_z:], w_ba], axis=1).astype(BF16)
        last = layer == DEPTH - 1
        wts = dict(
            norm1_g=norm1_g[layer], w_cat=w_cat, conv_w=conv_w[layer], a_log=a_log[layer],
            dt_bias=dt_bias[layer], dn_norm_g=dn_norm_g[layer], w_pool=w_pool[layer],
            pool_scale=pool_scale[layer], w_out=w_out[layer].astype(BF16), norm2_g=norm2_g[layer],
            w_query=w_query[layer].astype(BF16),
            keys=sub_keys[layer].reshape(2 * PEER_HEADS, PEER_NKEYS, PEER_KEY_HALF).astype(BF16),
            expert_u=_pack_table(expert_u[layer]), expert_v=_pack_table(expert_v[layer]),
            final_norm_g=final_norm_g if last else jnp.ones_like(final_norm_g))
        mod = _ada(c_all, w_ada[layer], b_ada[layer])
        assert last, "final norm is fused into the expert stage"
        step = bp // PROMPT_PARTS
        ls = x_prompt.shape[1] // SEQ_SPLITS
        zeros = (zero_conv[:step], zero_delta[:step], zero_pool[:step])
        jobs = [(yp[b0:b0 + step, s0:s0 + ls], mod[b0:b0 + step], zeros if s0 == 0 else None, s0, DN_CHUNK)
                for b0 in range(0, bp, step) for s0 in range(0, SEQ_SPLITS * ls, ls)]
        jobs.append((ys, mod[bp:], (state_conv[layer], state_delta[layer], state_pool[layer]),
                     PAST_LEN, SUBLANES))
        groups = []
        for j, (xg, mg, states, start, chunk) in enumerate(jobs):
            prev = groups[j - 1] if j >= 1 else None
            fin = groups[j - FIN_LAG] if j >= FIN_LAG else None
            if fin is not None and fin["x1"].shape[0] != xg.shape[0] * xg.shape[1]:
                fin = None
            if states is None:
                states = prev["states"]
            g, coef_prev, y_fin = _front(xg, mg, *states, start, chunk, ROW_TILE, wts, prev, fin)
            if prev is not None:
                _expert_gather_v(prev, coef_prev, wts["expert_v"])
            if fin is not None:
                fin["y"] = y_fin
            groups.append(g)
        _expert_gather_v(groups[-1], _coef(groups[-1]["pre"], groups[-1]["gates"], ROW_TILE),
                         wts["expert_v"])
        for g in groups:
            if "y" not in g:
                g["y"] = _final(g["x1"], g["peer_out"], g["mod"], g["l"], wts["final_norm_g"],
                                g["tm"]).reshape(g["b"], g["l"], D_MODEL)
        rows = [groups[i:i + SEQ_SPLITS] for i in range(0, len(groups) - 1, SEQ_SPLITS)]
        yp = jnp.concatenate([jnp.concatenate([g["y"] for g in row], axis=1) for row in rows], axis=0)
        cp, sp, pp = (jnp.concatenate(a, axis=0) for a in zip(*(row[-1]["states"] for row in rows)))
        ys = groups[-1]["y"]
        cs, ss, ps = groups[-1]["states"]
        conv_p.append(cp)
        delta_p.append(sp)
        pool_p.append(pp)
        conv_s.append(cs)
        delta_s.append(ss)
        pool_s.append(ps)
    return (yp, ys, jnp.stack(conv_p), jnp.stack(delta_p), jnp.stack(pool_p),
            jnp.stack(conv_s), jnp.stack(delta_s), jnp.stack(pool_s))
```

```python
import functools

import jax
import jax.numpy as jnp
from jax import lax
from jax.experimental import pallas as pl
from jax.experimental.pallas import tpu as pltpu
from jax.experimental.pallas import tpu_sc as plsc

F32 = jnp.float32
BF16 = jnp.bfloat16
I32 = jnp.int32

D_MODEL = 1024
DEPTH = 1
PAST_LEN = 16384
DN_HEADS = 8
DN_HEAD_DIM = 128
DN_WIDTH = DN_HEADS * DN_HEAD_DIM
QKV_WIDTH = 3 * DN_WIDTH
CONV_WIDTH = 4
DN_CHUNK = 64
POOL_WINDOWS = (2, 4, 8, 16)
POOL_GROUP_DIM = 128
POOL_WIDTH = len(POOL_WINDOWS) * POOL_GROUP_DIM
POOL_OUT_GROUP = D_MODEL // len(POOL_WINDOWS)
POOL_BUF = max(POOL_WINDOWS) - 1
PEER_HEADS = 8
PEER_NKEYS = 128
PEER_TOPK = 16
PEER_KEY_HALF = 128
PEER_HK = PEER_HEADS * PEER_TOPK
EPS = 1e-6

LANES = 128
SUBLANES = 8
CONV_PAD = SUBLANES
POOL_PAD = 16
VMEM_LIMIT = 56 * 1024 * 1024

NT_DIMS = (((1,), (1,)), ((), ()))
TN_DIMS = (((0,), (0,)), ((), ()))


def _dot(a, b):
    return jnp.dot(a.astype(BF16), b.astype(BF16), preferred_element_type=F32)


def _dot_nt(a, b):
    return lax.dot_general(a.astype(BF16), b.astype(BF16), NT_DIMS, preferred_element_type=F32)


def _split3(x):
    hi = x.astype(BF16)
    r1 = x - hi.astype(F32)
    mid = r1.astype(BF16)
    lo = (r1 - mid.astype(F32)).astype(BF16)
    return hi, mid, lo


def _silu(x):
    return x * jax.nn.sigmoid(x)


def _gelu(x):
    return 0.5 * x * (1.0 + lax.erf(x * (0.5 ** 0.5)))


def _softplus(x):
    return jnp.maximum(x, 0.0) + jnp.log(1.0 + jnp.exp(-jnp.abs(x)))


def _mod_rows(ref):
    m = ref[...]
    return m.reshape(m.shape[-2], m.shape[-1])


def _mod_spec(k, rows_per_batch, tm):
    if rows_per_batch >= tm:
        tiles = rows_per_batch // tm
        return pl.BlockSpec((1, 1, 1, D_MODEL), lambda i, *_: (k, i // tiles, 0, 0))
    return pl.BlockSpec((1, tm, D_MODEL), lambda i, *_: (k, i, 0))


def _const_spec(shape):
    nd = len(shape)
    return pl.BlockSpec(shape, lambda *_: (0,) * nd)


def _ada_body(c_ref, w_ref, b_ref, o_ref):
    o_ref[...] = _dot(_silu(c_ref[...]), w_ref[...]) + b_ref[...]


def _ada(c, w_ada, b_ada):
    n = c.shape[0]
    return pl.pallas_call(
        _ada_body,
        grid=(6,),
        in_specs=[pl.BlockSpec((n, D_MODEL), lambda j: (0, 0)),
                  pl.BlockSpec((D_MODEL, D_MODEL), lambda j: (0, j)),
                  pl.BlockSpec((1, D_MODEL), lambda j: (0, j))],
        out_specs=pl.BlockSpec((n, D_MODEL), lambda j: (0, j)),
        out_shape=jax.ShapeDtypeStruct((n, 6 * D_MODEL), F32),
        name="ada",
    )(c, w_ada, b_ada.reshape(1, -1))


_IN_BLOCKS = (("qkv", QKV_WIDTH), ("z", DN_WIDTH), ("pool", POOL_WIDTH),
              ("ga", D_MODEL), ("gb", D_MODEL), ("ba", LANES))
_IN_TOTAL = sum(w for _, w in _IN_BLOCKS)
_IN_COL_CHUNK = 512


def _inproj_body(x_ref, sc_ref, sh_ref, g_ref, w_ref, *out_refs):
    x = x_ref[...]
    y = x * lax.rsqrt(jnp.mean(x * x, axis=-1, keepdims=True) + EPS) * g_ref[...]
    h = (y * (1.0 + _mod_rows(sc_ref)) + _mod_rows(sh_ref)).astype(BF16)
    off = 0
    for (_, width), o_ref in zip(_IN_BLOCKS, out_refs):
        for c0 in range(0, width, _IN_COL_CHUNK):
            cw = min(_IN_COL_CHUNK, width - c0)
            o_ref[:, c0:c0 + cw] = jnp.dot(h, w_ref[:, off + c0:off + c0 + cw],
                                           preferred_element_type=F32)
        off += width


def _inproj(x2d, mod, rows_per_batch, norm_g, w_cat, tm):
    t = x2d.shape[0]
    row = lambda w: pl.BlockSpec((tm, w), lambda i: (i, 0))
    return pl.pallas_call(
        _inproj_body,
        grid=(t // tm,),
        in_specs=[row(D_MODEL), _mod_spec(1, rows_per_batch, tm), _mod_spec(0, rows_per_batch, tm),
                  _const_spec((1, D_MODEL)),
                  pl.BlockSpec((D_MODEL, _IN_TOTAL), lambda i: (0, 0), pipeline_mode=pl.Buffered(1))],
        out_specs=[row(w) for _, w in _IN_BLOCKS],
        out_shape=[jax.ShapeDtypeStruct((t, w), F32) for _, w in _IN_BLOCKS],
        compiler_params=pltpu.CompilerParams(vmem_limit_bytes=VMEM_LIMIT),
        name="inproj",
    )(x2d, mod, mod, norm_g.reshape(1, -1), w_cat)


def _mixer_body(C, Lv, start,
                qkv_ref, ba_ref, z_ref, pin_ref, ga_ref, gb_ref, cbuf_ref, s0_ref, pbuf_ref,
                convw_ref, alog_ref, dtb_ref, dng_ref, wpool_ref, pscale_ref,
                mixed_ref, nconv_ref, ns_ref, npool_ref,
                xp_scr, act_scr, s_scr, pp_scr, odn_scr):
    n = pl.program_id(1)
    last = pl.num_programs(1) - 1

    @pl.when(n == 0)
    def _load_state():
        xp_scr[0:CONV_PAD, :] = cbuf_ref[0]
        pp_scr[0:POOL_PAD, :] = pbuf_ref[0]
        s_scr[...] = s0_ref[0]

    xp_scr[CONV_PAD:CONV_PAD + C, :] = qkv_ref[0]
    for c0 in range(0, QKV_WIDTH, 512):
        cs = slice(c0, c0 + 512)
        y = xp_scr[CONV_PAD:CONV_PAD + C, cs] * convw_ref[CONV_WIDTH - 1:CONV_WIDTH, cs]
        for k in range(CONV_WIDTH - 1):
            r0 = CONV_PAD - (CONV_WIDTH - 1) + k
            y = y + xp_scr[r0:r0 + C, cs] * convw_ref[k:k + 1, cs]
        act_scr[:, cs] = _silu(y)

    ba = ba_ref[0]
    lane = lax.broadcasted_iota(I32, (C, LANES), 1)
    beta_all = jax.nn.sigmoid(ba)
    g_all = -jnp.exp(alog_ref[...]) * _softplus(ba + dtb_ref[...])
    if Lv < C:
        valid = lax.broadcasted_iota(I32, (C, LANES), 0) < Lv
        beta_all = jnp.where(valid, beta_all, 0.0)
        g_all = jnp.where(valid, g_all, 0.0)
    ii = lax.broadcasted_iota(I32, (C, C), 0)
    jj = lax.broadcasted_iota(I32, (C, C), 1)
    causal = ii >= jj
    strict = ii > jj
    tril = jnp.where(causal, 1.0, 0.0).astype(BF16)
    eye = jnp.where(ii == jj, 1.0, 0.0)
    gc_all = sum(jnp.dot(tril, part, preferred_element_type=F32) for part in _split3(g_all))
    if C < LANES:
        gc_sq = jnp.concatenate([gc_all, jnp.zeros((LANES - C, LANES), F32)], axis=0)
    else:
        gc_sq = gc_all
    gc_t = gc_sq.T

    H = range(DN_HEADS)
    hsl = [slice(h * DN_HEAD_DIM, (h + 1) * DN_HEAD_DIM) for h in H]
    beta = [jnp.sum(jnp.where(lane == h, beta_all, 0.0), axis=1, keepdims=True) for h in H]
    gcol = [jnp.sum(jnp.where(lane == DN_HEADS + h, gc_all, 0.0), axis=1, keepdims=True) for h in H]
    grow = [gc_t[DN_HEADS + h:DN_HEADS + h + 1, 0:C] for h in H]
    glast = [g[C - 1:C, :] for g in gcol]
    q = [act_scr[:, hsl[h]] for h in H]
    k = [act_scr[:, DN_WIDTH + h * DN_HEAD_DIM:DN_WIDTH + (h + 1) * DN_HEAD_DIM] for h in H]
    v = [act_scr[:, 2 * DN_WIDTH + h * DN_HEAD_DIM:2 * DN_WIDTH + (h + 1) * DN_HEAD_DIM] for h in H]
    q = [x * lax.rsqrt(jnp.sum(x * x, axis=-1, keepdims=True) + EPS) * (DN_HEAD_DIM ** -0.5) for x in q]
    k = [x * lax.rsqrt(jnp.sum(x * x, axis=-1, keepdims=True) + EPS) for x in k]
    kb = [k[h] * beta[h] for h in H]
    vb = [v[h] * beta[h] for h in H]
    decay = [jnp.where(causal, jnp.exp(jnp.where(causal, gcol[h] - grow[h], 0.0)), 0.0) for h in H]
    lower = [jnp.where(strict, _dot_nt(kb[h], k[h]) * decay[h], 0.0) for h in H]
    ainv = [eye - x for x in lower]
    pw = lower
    p = 1
    while 2 * p < C:
        pw = [_dot(x, x) for x in pw]
        ainv = [ainv[h] + _dot(ainv[h], pw[h]) for h in H]
        p *= 2
    sol = [_dot(ainv[h], jnp.concatenate([vb[h], kb[h] * jnp.exp(gcol[h])], axis=1)) for h in H]
    qk = [_dot_nt(q[h], k[h]) * decay[h] for h in H]
    k_tail = [k[h] * jnp.exp(glast[h] - gcol[h]) for h in H]
    S = [s_scr[h] for h in H]
    v_new = [sol[h][:, :DN_HEAD_DIM] - _dot(sol[h][:, DN_HEAD_DIM:], S[h]) for h in H]
    o = [_dot(q[h] * jnp.exp(gcol[h]), S[h]) + _dot(qk[h], v_new[h]) for h in H]
    for h in H:
        s_scr[h] = S[h] * jnp.exp(glast[h]) + lax.dot_general(
            k_tail[h].astype(BF16), v_new[h].astype(BF16), TN_DIMS, preferred_element_type=F32)
    for h in H:
        zf = z_ref[0, :, hsl[h]]
        odn_scr[:, hsl[h]] = (o[h] * lax.rsqrt(jnp.mean(o[h] * o[h], axis=-1, keepdims=True) + EPS)
                              * dng_ref[...] * _silu(zf))

    pp_scr[POOL_PAD:POOL_PAD + C, :] = pin_ref[0]
    pos = start + n * C + lax.broadcasted_iota(I32, (C, 1), 0)
    for gi, win in enumerate(POOL_WINDOWS):
        gs = slice(gi * POOL_GROUP_DIM, (gi + 1) * POOL_GROUP_DIM)
        xg = pp_scr[POOL_PAD:POOL_PAD + C, gs]
        ssum = xg
        for sft in range(1, win):
            ssum = ssum + pp_scr[POOL_PAD - sft:POOL_PAD - sft + C, gs]
        cnt = jnp.minimum(pos + 1, win).astype(F32)
        pooled = ssum / cnt - xg
        os_ = slice(gi * POOL_OUT_GROUP, (gi + 1) * POOL_OUT_GROUP)
        yp = _dot(pooled, wpool_ref[gi]) * pscale_ref[:, os_]
        mixed_ref[0, :, os_] = (jax.nn.sigmoid(ga_ref[0, :, os_]) * odn_scr[:, os_]
                                + jax.nn.sigmoid(gb_ref[0, :, os_]) * yp)

    @pl.when(n == last)
    def _store_state():
        nconv_ref[0] = xp_scr[Lv + CONV_PAD - (CONV_WIDTH - 1):Lv + CONV_PAD, :]
        npool_ref[0] = pp_scr[Lv + POOL_PAD - POOL_BUF:Lv + POOL_PAD, :]
        ns_ref[0] = s_scr[...]

    xp_scr[0:CONV_PAD, :] = xp_scr[C:C + CONV_PAD, :]
    pp_scr[0:POOL_PAD, :] = pp_scr[C:C + POOL_PAD, :]


def _mixer(proj, conv_buf, s0, pool_buf, start, seq_len, C,
           conv_w, a_log, dt_bias, dn_norm_g, w_pool, pool_scale):
    b, lp, _ = proj["qkv"].shape
    nchunks = lp // C
    lv = seq_len - (nchunks - 1) * C
    cbuf = jnp.pad(conv_buf, ((0, 0), (CONV_PAD - (CONV_WIDTH - 1), 0), (0, 0)))
    pbuf = jnp.pad(pool_buf, ((0, 0), (POOL_PAD - POOL_BUF, 0), (0, 0)))
    lane_pad = lambda a: jnp.pad(a.reshape(1, -1), ((0, 0), (DN_HEADS, LANES - 2 * DN_HEADS)))
    chunk = lambda w: pl.BlockSpec((1, C, w), lambda i, j: (i, j, 0))
    state = lambda *s: pl.BlockSpec((1,) + s, lambda i, j: (i,) + (0,) * len(s))
    return pl.pallas_call(
        functools.partial(_mixer_body, C, lv, start),
        grid=(b, nchunks),
        in_specs=[chunk(QKV_WIDTH), chunk(LANES), chunk(DN_WIDTH), chunk(POOL_WIDTH),
                  chunk(D_MODEL), chunk(D_MODEL),
                  state(CONV_PAD, QKV_WIDTH), state(DN_HEADS, DN_HEAD_DIM, DN_HEAD_DIM),
                  state(POOL_PAD, POOL_WIDTH),
                  _const_spec((CONV_WIDTH, QKV_WIDTH)), _const_spec((1, LANES)), _const_spec((1, LANES)),
                  _const_spec((1, DN_HEAD_DIM)),
                  _const_spec((len(POOL_WINDOWS), POOL_GROUP_DIM, POOL_OUT_GROUP)),
                  _const_spec((1, D_MODEL))],
        out_specs=[chunk(D_MODEL), state(CONV_WIDTH - 1, QKV_WIDTH),
                   state(DN_HEADS, DN_HEAD_DIM, DN_HEAD_DIM), state(POOL_BUF, POOL_WIDTH)],
        out_shape=[jax.ShapeDtypeStruct((b, lp, D_MODEL), F32),
                   jax.ShapeDtypeStruct((b, CONV_WIDTH - 1, QKV_WIDTH), F32),
                   jax.ShapeDtypeStruct((b, DN_HEADS, DN_HEAD_DIM, DN_HEAD_DIM), F32),
                   jax.ShapeDtypeStruct((b, POOL_BUF, POOL_WIDTH), F32)],
        scratch_shapes=[pltpu.VMEM((CONV_PAD + C + CONV_PAD, QKV_WIDTH), F32),
                        pltpu.VMEM((C, QKV_WIDTH), F32),
                        pltpu.VMEM((DN_HEADS, DN_HEAD_DIM, DN_HEAD_DIM), F32),
                        pltpu.VMEM((POOL_PAD + C + POOL_PAD, POOL_WIDTH), F32),
                        pltpu.VMEM((C, DN_WIDTH), F32)],
        compiler_params=pltpu.CompilerParams(dimension_semantics=("arbitrary", "arbitrary"),
                                             vmem_limit_bytes=VMEM_LIMIT),
        name="mixer",
    )(proj["qkv"], proj["ba"], proj["z"], proj["pool"], proj["ga"], proj["gb"], cbuf, s0, pbuf,
      conv_w, lane_pad(a_log), lane_pad(dt_bias), dn_norm_g.reshape(1, -1), w_pool,
      pool_scale.reshape(1, -1))


def _top16(s, ids, payload=None):
    big = float(2 ** 24)
    vals, sel, pays = [], [], []
    for _ in range(PEER_TOPK):
        m = jnp.max(s, axis=0, keepdims=True)
        am = jnp.min(jnp.where(s == m, ids, big), axis=0, keepdims=True)
        hit = ids == am
        if payload is not None:
            pays.append(jnp.max(jnp.where(hit, payload, -1.0), axis=0, keepdims=True))
        s = jnp.where(hit, -jnp.inf, s)
        vals.append(m)
        sel.append(am)
    out = (jnp.concatenate(vals, axis=0), jnp.concatenate(sel, axis=0))
    if payload is not None:
        out += (jnp.concatenate(pays, axis=0),)
    return out


_CAND_EDGE = 4


def _post_body(has_prev, has_fin, mixed_ref, x_ref, g1_ref, sc2_ref, sh2_ref, n2g_ref, wout_ref,
               wq_ref, keys_ref, *refs):
    refs = list(refs)
    prev_in = [refs.pop(0) for _ in range(2 if has_prev else 0)]
    fin_in = [refs.pop(0) for _ in range(4 if has_fin else 0)]
    x1_ref, h2_ref, idx_ref, gate_ref = refs[:4]
    extra_out = refs[4:]
    if has_prev:
        pre_ref, pgate_ref = prev_in
        extra_out.pop(0)[...] = _coef_words(pre_ref[...], pgate_ref[...])
    if has_fin:
        _final_body(*fin_in, extra_out.pop(0))
    tm = x_ref.shape[0]
    x1 = x_ref[...] + _mod_rows(g1_ref) * _dot(mixed_ref[...], wout_ref[...])
    x1_ref[...] = x1
    y = x1 * lax.rsqrt(jnp.mean(x1 * x1, axis=-1, keepdims=True) + EPS) * n2g_ref[...]
    h2 = y * (1.0 + _mod_rows(sc2_ref)) + _mod_rows(sh2_ref)
    h2_ref[...] = _pack_words(h2[:, :PACK_HALF], h2[:, PACK_HALF:])
    q = _dot(h2, wq_ref[...])

    K = PEER_TOPK
    key_id = lax.broadcasted_iota(I32, (PEER_NKEYS, 1), 0).astype(F32)
    r16 = lax.broadcasted_iota(I32, (K, 1), 0)
    cand_id = jnp.concatenate([(a * K + r16) for a in range(_CAND_EDGE)]
                              + [(r16 * K + b) for b in range(_CAND_EDGE)], axis=0).astype(F32)
    dup = r16 < _CAND_EDGE
    idx_rows, gate_rows = [], []
    for h in range(PEER_HEADS):
        half = []
        for p in range(2):
            c0 = (h * 2 + p) * PEER_KEY_HALF
            st = _dot_nt(keys_ref[h * 2 + p], q[:, c0:c0 + PEER_KEY_HALF])
            half.append(_top16(st, key_id))
        (s1, i1), (s2, i2) = half
        cand = jnp.concatenate(
            [s1[a:a + 1] + s2 for a in range(_CAND_EDGE)]
            + [jnp.where(dup, -jnp.inf, s1 + s2[b:b + 1]) for b in range(_CAND_EDGE)], axis=0)
        cidx = jnp.concatenate(
            [i1[a:a + 1] * PEER_NKEYS + i2 for a in range(_CAND_EDGE)]
            + [i1 * PEER_NKEYS + i2[b:b + 1] for b in range(_CAND_EDGE)], axis=0)
        best, _, eidx = _top16(cand, cand_id, cidx)
        e = jnp.exp(best - best[0:1])
        gate_rows.append(e / jnp.sum(e, axis=0, keepdims=True))
        idx_rows.append(eidx)
    idx_ref[...] = jnp.concatenate(idx_rows, axis=0).T.astype(I32)
    gate_ref[...] = jnp.concatenate(gate_rows, axis=0).T


def _post(mixed2d, x2d, mod, rows_per_batch, norm2_g, w_out, w_query, keys, tm, prev=None, fin=None):
    t = x2d.shape[0]
    steps = t // tm
    row = lambda w: pl.BlockSpec((tm, w), lambda i: (i, 0))
    in_specs = [row(D_MODEL), row(D_MODEL),
                _mod_spec(2, rows_per_batch, tm), _mod_spec(4, rows_per_batch, tm),
                _mod_spec(3, rows_per_batch, tm), _const_spec((1, D_MODEL)),
                _const_spec((D_MODEL, D_MODEL)), _const_spec((D_MODEL, 2 * PEER_HEADS * PEER_KEY_HALF)),
                _const_spec((2 * PEER_HEADS, PEER_NKEYS, PEER_KEY_HALF))]
    out_specs = [row(D_MODEL), row(PACK_HALF), row(PEER_HK), row(PEER_HK)]
    out_shape = [jax.ShapeDtypeStruct((t, D_MODEL), F32), jax.ShapeDtypeStruct((t, PACK_HALF), I32),
                 jax.ShapeDtypeStruct((t, PEER_HK), I32), jax.ShapeDtypeStruct((t, PEER_HK), F32)]
    args = [mixed2d, x2d, mod, mod, mod, norm2_g.reshape(1, -1), w_out, w_query, keys]
    if prev is not None:
        tp = prev[0].shape[0]
        prow = pl.BlockSpec((tp // steps, PEER_HK), lambda i: (i, 0))
        in_specs += [prow, prow]
        out_specs += [prow]
        out_shape += [jax.ShapeDtypeStruct((tp, PEER_HK), I32)]
        args += list(prev)
    if fin is not None:
        x1_f, peer_f, mod_f, rows_f, final_g = fin
        tf = x1_f.shape[0]
        frow = pl.BlockSpec((tf // steps, D_MODEL), lambda i: (i, 0))
        in_specs += [frow, frow, _mod_spec(5, rows_f, tf // steps), _const_spec((1, D_MODEL))]
        out_specs += [frow]
        out_shape += [jax.ShapeDtypeStruct((tf, D_MODEL), F32)]
        args += [x1_f, peer_f, mod_f, final_g.reshape(1, -1)]
    return pl.pallas_call(
        functools.partial(_post_body, prev is not None, fin is not None),
        grid=(steps,),
        in_specs=in_specs, out_specs=out_specs, out_shape=out_shape,
        compiler_params=pltpu.CompilerParams(vmem_limit_bytes=VMEM_LIMIT),
        name="post",
    )(*args)


SC_CORES = 2
SC_SUBCORES = 16
SC_LANES = 16
SC_WORKERS = SC_CORES * SC_SUBCORES
SC_TOKENS = 16
SC_SLOTS = 4
SC_JOB_HEADS = 2
SC_BF16_GROUP = 4
PACK_HALF = D_MODEL // 2
SC_CHUNKS = PACK_HALF // SC_LANES
PROMPT_PARTS = 8
SEQ_SPLITS = 1
FIN_LAG = 3
ROW_TILE = 256


def _bf16_bits(v):
    return lax.bitcast_convert_type(v.astype(BF16).astype(F32), jnp.uint32)


def _pack_words(lo, hi):
    return lax.bitcast_convert_type((_bf16_bits(lo) >> 16) | _bf16_bits(hi), I32)


def _pack_body(x_ref, o_ref):
    o_ref[...] = _pack_words(x_ref[:, :PACK_HALF], x_ref[:, PACK_HALF:])


def _pack_table(tbl, rows=512):
    e = tbl.shape[0]
    return pl.pallas_call(
        _pack_body, grid=(e // rows,),
        in_specs=[pl.BlockSpec((rows, D_MODEL), lambda i: (i, 0))],
        out_specs=pl.BlockSpec((rows, PACK_HALF), lambda i: (i, 0)),
        out_shape=jax.ShapeDtypeStruct((e, PACK_HALF), I32), name="pack_table")(tbl)


def _tree_sum(terms):
    terms = list(terms)
    while len(terms) > 1:
        terms = [a + b for a, b in zip(terms[0::2], terms[1::2])] + terms[len(terms) & ~1:]
    return terms[0]


def _unpack_pair(w):
    lo = plsc.bitcast(lax.shift_left(w, jnp.full(w.shape, 16, I32)), F32)
    hi = plsc.bitcast(w & jnp.full(w.shape, -65536, I32), F32)
    return lo, hi


def _sc_mesh():
    return plsc.VectorSubcoreMesh(core_axis_name="c", subcore_axis_name="s")


def _sc_worker():
    return lax.axis_index("s") * SC_CORES + lax.axis_index("c")


def _sc_jobs(table_hbm, idx_v, buf, sem, compute):
    per_tok = PEER_HEADS // SC_JOB_HEADS
    njobs = idx_v.shape[0] * per_tok
    nrows = SC_JOB_HEADS * PEER_TOPK

    def copy(j, slot):
        rows = idx_v.at[j // per_tok, pl.ds((j % per_tok) * nrows, nrows)]
        return pltpu.make_async_copy(table_hbm.at[rows], buf.at[slot], sem.at[slot])

    for s in range(SC_SLOTS):
        copy(s, s).start()

    def group(g, c):
        for s in range(SC_SLOTS):
            j = g * SC_SLOTS + s
            copy(j, s).wait()
            for i in range(SC_JOB_HEADS):
                compute(j // per_tok, (j % per_tok) * SC_JOB_HEADS + i, s, i * PEER_TOPK)

            @pl.when(j + SC_SLOTS < njobs)
            def _next():
                copy(j + SC_SLOTS, s).start()
        return c

    lax.fori_loop(0, njobs // SC_SLOTS, group, 0)


def _peer_u_body(n_tok, idx_hbm, h2_hbm, u_hbm, pre_hbm, idx_v, h2_v, pre_v, ubuf, acc_v, sem):
    base = _sc_worker() * n_tok
    lane = lax.iota(I32, SC_LANES)

    def compute(tt, h, slot, r0):
        def chunk(cg, accs):
            cs = [pl.ds((cg * SC_BF16_GROUP + i) * SC_LANES, SC_LANES) for i in range(SC_BF16_GROUP)]
            xs = [plsc.bitcast(h2_v[tt, c], BF16) for c in cs]
            out = []
            for k, a in enumerate(accs):
                part = _tree_sum([plsc.bitcast(ubuf[slot, r0 + k, c], BF16) * x for c, x in zip(cs, xs)])
                lo, hi = _unpack_pair(plsc.bitcast(part, I32))
                out.append(a + (lo + hi))
            return tuple(out)
        zero = jnp.zeros((SC_LANES,), F32)
        accs = lax.fori_loop(0, SC_CHUNKS // SC_BF16_GROUP, chunk, (zero,) * PEER_TOPK)
        for k, a in enumerate(accs):
            acc_v[k, :] = a
        tot = zero
        for j in range(SC_LANES):
            tot = tot + plsc.load_gather(acc_v, [lane, (lane + j) & (SC_LANES - 1)])
        pre_v[tt, pl.ds(h * PEER_TOPK, PEER_TOPK)] = tot

    tb = idx_v.shape[0]

    def block(bi, c):
        t0 = base + bi * tb
        pltpu.sync_copy(idx_hbm.at[pl.ds(t0, tb)], idx_v)
        pltpu.sync_copy(h2_hbm.at[pl.ds(t0, tb)], h2_v)
        _sc_jobs(u_hbm, idx_v, ubuf, sem, compute)
        pltpu.sync_copy(pre_v, pre_hbm.at[pl.ds(t0, tb)])
        return c

    lax.fori_loop(0, n_tok // tb, block, 0)


def _peer_v_body(n_tok, idx_hbm, coef_hbm, v_hbm, out_hbm, idx_v, coef_v, out_v, vbuf, sem):
    base = _sc_worker() * n_tok
    zero = jnp.zeros((SC_LANES,), F32)

    def compute(tt, h, slot, r0):
        row = jnp.full((SC_LANES,), tt, I32)
        cb = [plsc.bitcast(plsc.load_gather(
                  coef_v, [row, jnp.full((SC_LANES,), h * PEER_TOPK + k, I32)]), BF16)
              for k in range(PEER_TOPK)]

        @plsc.parallel_loop(0, SC_CHUNKS, unroll=2)
        def _chunk(c):
            cs = pl.ds(c * SC_LANES, SC_LANES)
            prods = [plsc.bitcast(vbuf[slot, r0 + k, cs], BF16) * cb[k] for k in range(PEER_TOPK)]
            pairs = [_unpack_pair(plsc.bitcast(_tree_sum(prods[g:g + SC_BF16_GROUP]), I32))
                     for g in range(0, PEER_TOPK, SC_BF16_GROUP)]
            for half, off in ((0, 0), (1, PACK_HALF)):
                plsc.addupdate(out_v.at[tt, pl.ds(off + c * SC_LANES, SC_LANES)],
                               _tree_sum([p[half] for p in pairs]))

    tb = idx_v.shape[0]

    def block(bi, c):
        t0 = base + bi * tb
        pltpu.sync_copy(idx_hbm.at[pl.ds(t0, tb)], idx_v)
        pltpu.sync_copy(coef_hbm.at[pl.ds(t0, tb)], coef_v)

        def clear(i, cc):
            per_row = D_MODEL // SC_LANES
            out_v[i // per_row, pl.ds((i % per_row) * SC_LANES, SC_LANES)] = zero
            return cc
        lax.fori_loop(0, tb * (D_MODEL // SC_LANES), clear, 0)
        _sc_jobs(v_hbm, idx_v, vbuf, sem, compute)
        pltpu.sync_copy(out_v, out_hbm.at[pl.ds(t0, tb)])
        return c

    lax.fori_loop(0, n_tok // tb, block, 0)


def _peer_sc(body, idx, rows, table, out_width, name):
    t = idx.shape[0]
    assert t % SC_WORKERS == 0
    n_tok = t // SC_WORKERS
    tb = min(SC_TOKENS, n_tok)
    assert n_tok % tb == 0 and (tb * PEER_HEADS // SC_JOB_HEADS) % SC_SLOTS == 0
    return pl.kernel(
        functools.partial(body, n_tok),
        out_type=jax.ShapeDtypeStruct((t, out_width), F32),
        mesh=_sc_mesh(),
        scratch_types=[pltpu.VMEM((tb, PEER_HK), I32),
                       pltpu.VMEM((tb, rows.shape[1]), rows.dtype),
                       pltpu.VMEM((tb, out_width), F32),
                       pltpu.VMEM((SC_SLOTS, SC_JOB_HEADS * PEER_TOPK, PACK_HALF), I32)]
                      + ([pltpu.VMEM((PEER_TOPK, SC_LANES), F32)] if body is _peer_u_body else [])
                      + [pltpu.SemaphoreType.DMA((SC_SLOTS,))],
        compiler_params=pltpu.CompilerParams(needs_layout_passes=False),
        name=name,
    )(idx, rows, table)


def _coef_words(pre, gates):
    return _pack_words(*(gates * _gelu(pre),) * 2)


def _coef_body(pre_ref, gate_ref, coef_ref):
    coef_ref[...] = _coef_words(pre_ref[...], gate_ref[...])


def _coef(pre, gates, tm):
    t = pre.shape[0]
    row = pl.BlockSpec((tm, PEER_HK), lambda i: (i, 0))
    return pl.pallas_call(_coef_body, grid=(t // tm,), in_specs=[row, row], out_specs=row,
                          out_shape=jax.ShapeDtypeStruct((t, PEER_HK), I32), name="coef")(pre, gates)


def _final_body(x1_ref, peer_ref, g2_ref, fng_ref, y_ref):
    x2 = x1_ref[...] + _mod_rows(g2_ref) * peer_ref[...]
    y_ref[...] = x2 * lax.rsqrt(jnp.mean(x2 * x2, axis=-1, keepdims=True) + EPS) * fng_ref[...]


def _final(x1, peer_out, mod, rows_per_batch, final_g, tm):
    t = x1.shape[0]
    row = pl.BlockSpec((tm, D_MODEL), lambda i: (i, 0))
    return pl.pallas_call(
        _final_body, grid=(t // tm,),
        in_specs=[row, row, _mod_spec(5, rows_per_batch, tm), _const_spec((1, D_MODEL))],
        out_specs=row, out_shape=jax.ShapeDtypeStruct((t, D_MODEL), F32), name="final",
    )(x1, peer_out, mod, final_g.reshape(1, -1))


def _expert_gather_v(g, coef, expert_v):
    g["peer_out"] = _peer_sc(_peer_v_body, g["idx"], coef, expert_v, D_MODEL, "peer_v")


def _front(x, mod, conv_buf, s0, pool_buf, start, chunk, tm, wts, prev, fin):
    b, l, _ = x.shape
    t = b * l
    x2d = x.reshape(t, D_MODEL)
    if l >= tm:
        modx = mod.reshape(b, 6, 1, D_MODEL).transpose(1, 0, 2, 3)
    else:
        modx = jnp.repeat(mod.reshape(b, 6, D_MODEL), l, axis=0).transpose(1, 0, 2)
    outs = _inproj(x2d, modx, l, wts["norm1_g"], wts["w_cat"], tm)
    lp = -(-l // chunk) * chunk
    proj = {}
    for (name, w), a in zip(_IN_BLOCKS, outs):
        a = a.reshape(b, l, w)
        proj[name] = a if lp == l else jnp.pad(a, ((0, 0), (0, lp - l), (0, 0)))
    mixed, nconv, ns, npool = _mixer(proj, conv_buf, s0, pool_buf, start, l, chunk,
                                     wts["conv_w"], wts["a_log"], wts["dt_bias"], wts["dn_norm_g"],
                                     wts["w_pool"], wts["pool_scale"])
    mixed2d = mixed[:, :l].reshape(t, D_MODEL)
    res = _post(mixed2d, x2d, modx, l, wts["norm2_g"], wts["w_out"], wts["w_query"], wts["keys"], tm,
                prev=None if prev is None else (prev["pre"], prev["gates"]),
                fin=None if fin is None else (fin["x1"], fin["peer_out"], fin["mod"], fin["l"],
                                              wts["final_norm_g"]))
    x1, h2, idx, gates = res[:4]
    extra = list(res[4:])
    coef_prev = extra.pop(0) if prev is not None else None
    y_fin = extra.pop(0).reshape(fin["b"], fin["l"], D_MODEL) if fin is not None else None
    pre = _peer_sc(_peer_u_body, idx, h2, wts["expert_u"], PEER_HK, "peer_u")
    g = dict(x1=x1, idx=idx, gates=gates, pre=pre, mod=modx, b=b, l=l, tm=tm,
             states=(nconv, ns, npool))
    return g, coef_prev, y_fin


def kernel(x_prompt, x_sample, c_prompt, c_sample, state_conv, state_delta, state_pool, w_ada, b_ada, norm1_g, w_in, conv_w, a_log, dt_bias, dn_norm_g, w_pool, pool_scale, w_out, norm2_g, w_query, sub_keys, expert_u, expert_v, final_norm_g):
    bp = x_prompt.shape[0]
    bs = x_sample.shape[0]
    yp, ys = x_prompt, x_sample
    conv_p, delta_p, pool_p, conv_s, delta_s, pool_s = [], [], [], [], [], []
    zero_conv = jnp.zeros((bp, CONV_WIDTH - 1, QKV_WIDTH), F32)
    zero_delta = jnp.zeros((bp, DN_HEADS, DN_HEAD_DIM, DN_HEAD_DIM), F32)
    zero_pool = jnp.zeros((bp, POOL_BUF, POOL_WIDTH), F32)
    c_all = jnp.concatenate([c_prompt, c_sample], axis=0)
    for layer in range(DEPTH):
        wi = w_in[layer]
        o_b = QKV_WIDTH
        o_z = o_b + 2 * DN_HEADS
        w_ba = jnp.pad(wi[:, o_b:o_z], ((0, 0), (0, LANES - 2 * DN_HEADS)))
        w_cat = jnp.concatenate([wi[:, :o_b], wi[:, o_z:], w_ba], axis=1).astype(BF16)
        last = layer == DEPTH - 1
        wts = dict(
            norm1_g=norm1_g[layer], w_cat=w_cat, conv_w=conv_w[layer], a_log=a_log[layer],
            dt_bias=dt_bias[layer], dn_norm_g=dn_norm_g[layer], w_pool=w_pool[layer],
            pool_scale=pool_scale[layer], w_out=w_out[layer].astype(BF16), norm2_g=norm2_g[layer],
            w_query=w_query[layer].astype(BF16),
            keys=sub_keys[layer].reshape(2 * PEER_HEADS, PEER_NKEYS, PEER_KEY_HALF).astype(BF16),
            expert_u=_pack_table(expert_u[layer]), expert_v=_pack_table(expert_v[layer]),
            final_norm_g=final_norm_g if last else jnp.ones_like(final_norm_g))
        mod = _ada(c_all, w_ada[layer], b_ada[layer])
        assert last, "final norm is fused into the expert stage"
        step = bp // PROMPT_PARTS
        ls = x_prompt.shape[1] // SEQ_SPLITS
        zeros = (zero_conv[:step], zero_delta[:step], zero_pool[:step])
        jobs = [(yp[b0:b0 + step, s0:s0 + ls], mod[b0:b0 + step], zeros if s0 == 0 else None, s0, DN_CHUNK)
                for b0 in range(0, bp, step) for s0 in range(0, SEQ_SPLITS * ls, ls)]
        jobs.append((ys, mod[bp:], (state_conv[layer], state_delta[layer], state_pool[layer]),
                     PAST_LEN, SUBLANES))
        groups = []
        for j, (xg, mg, states, start, chunk) in enumerate(jobs):
            prev = groups[j - 1] if j >= 1 else None
            fin = groups[j - FIN_LAG] if j >= FIN_LAG else None
            if fin is not None and fin["x1"].shape[0] != xg.shape[0] * xg.shape[1]:
                fin = None
            if states is None:
                states = prev["states"]
            g, coef_prev, y_fin = _front(xg, mg, *states, start, chunk, ROW_TILE, wts, prev, fin)
            if prev is not None:
                _expert_gather_v(prev, coef_prev, wts["expert_v"])
            if fin is not None:
                fin["y"] = y_fin
            groups.append(g)
        _expert_gather_v(groups[-1], _coef(groups[-1]["pre"], groups[-1]["gates"], ROW_TILE),
                         wts["expert_v"])
        for g in groups:
            if "y" not in g:
                g["y"] = _final(g["x1"], g["peer_out"], g["mod"], g["l"], wts["final_norm_g"],
                                g["tm"]).reshape(g["b"], g["l"], D_MODEL)
        rows = [groups[i:i + SEQ_SPLITS] for i in range(0, len(groups) - 1, SEQ_SPLITS)]
        yp = jnp.concatenate([jnp.concatenate([g["y"] for g in row], axis=1) for row in rows], axis=0)
        cp, sp, pp = (jnp.concatenate(a, axis=0) for a in zip(*(row[-1]["states"] for row in rows)))
        ys = groups[-1]["y"]
        cs, ss, ps = groups[-1]["states"]
        conv_p.append(cp)
        delta_p.append(sp)
        pool_p.append(pp)
        conv_s.append(cs)
        delta_s.append(ss)
        pool_s.append(ps)
    return (yp, ys, jnp.stack(conv_p), jnp.stack(delta_p), jnp.stack(pool_p),
            jnp.stack(conv_s), jnp.stack(delta_s), jnp.stack(pool_s))
```

```python
import functools

import jax
import jax.numpy as jnp
from jax import lax
from jax.experimental import pallas as pl
from jax.experimental.pallas import tpu as pltpu
from jax.experimental.pallas import tpu_sc as plsc

F32 = jnp.float32
BF16 = jnp.bfloat16
I32 = jnp.int32

D_MODEL = 1024
DEPTH = 1
PAST_LEN = 16384
DN_HEADS = 8
DN_HEAD_DIM = 128
DN_WIDTH = DN_HEADS * DN_HEAD_DIM
QKV_WIDTH = 3 * DN_WIDTH
CONV_WIDTH = 4
DN_CHUNK = 64
POOL_WINDOWS = (2, 4, 8, 16)
POOL_GROUP_DIM = 128
POOL_WIDTH = len(POOL_WINDOWS) * POOL_GROUP_DIM
POOL_OUT_GROUP = D_MODEL // len(POOL_WINDOWS)
POOL_BUF = max(POOL_WINDOWS) - 1
PEER_HEADS = 8
PEER_NKEYS = 128
PEER_TOPK = 16
PEER_KEY_HALF = 128
PEER_HK = PEER_HEADS * PEER_TOPK
EPS = 1e-6

LANES = 128
SUBLANES = 8
CONV_PAD = SUBLANES
POOL_PAD = 16
VMEM_LIMIT = 56 * 1024 * 1024

NT_DIMS = (((1,), (1,)), ((), ()))
TN_DIMS = (((0,), (0,)), ((), ()))


def _dot(a, b):
    return jnp.dot(a.astype(BF16), b.astype(BF16), preferred_element_type=F32)


def _dot_nt(a, b):
    return lax.dot_general(a.astype(BF16), b.astype(BF16), NT_DIMS, preferred_element_type=F32)


def _split3(x):
    hi = x.astype(BF16)
    r1 = x - hi.astype(F32)
    mid = r1.astype(BF16)
    lo = (r1 - mid.astype(F32)).astype(BF16)
    return hi, mid, lo


def _silu(x):
    return x * jax.nn.sigmoid(x)


def _gelu(x):
    return 0.5 * x * (1.0 + lax.erf(x * (0.5 ** 0.5)))


def _softplus(x):
    return jnp.maximum(x, 0.0) + jnp.log(1.0 + jnp.exp(-jnp.abs(x)))


def _mod_rows(ref):
    m = ref[...]
    return m.reshape(m.shape[-2], m.shape[-1])


def _mod_spec(k, rows_per_batch, tm):
    if rows_per_batch >= tm:
        tiles = rows_per_batch // tm
        return pl.BlockSpec((1, 1, 1, D_MODEL), lambda i, *_: (k, i // tiles, 0, 0))
    return pl.BlockSpec((1, tm, D_MODEL), lambda i, *_: (k, i, 0))


def _const_spec(shape):
    nd = len(shape)
    return pl.BlockSpec(shape, lambda *_: (0,) * nd)


def _ada_body(c_ref, w_ref, b_ref, o_ref):
    o_ref[...] = _dot(_silu(c_ref[...]), w_ref[...]) + b_ref[...]


def _ada(c, w_ada, b_ada):
    n = c.shape[0]
    return pl.pallas_call(
        _ada_body,
        grid=(6,),
        in_specs=[pl.BlockSpec((n, D_MODEL), lambda j: (0, 0)),
                  pl.BlockSpec((D_MODEL, D_MODEL), lambda j: (0, j)),
                  pl.BlockSpec((1, D_MODEL), lambda j: (0, j))],
        out_specs=pl.BlockSpec((n, D_MODEL), lambda j: (0, j)),
        out_shape=jax.ShapeDtypeStruct((n, 6 * D_MODEL), F32),
        name="ada",
    )(c, w_ada, b_ada.reshape(1, -1))


_IN_BLOCKS = (("qkv", QKV_WIDTH), ("z", DN_WIDTH), ("pool", POOL_WIDTH),
              ("ga", D_MODEL), ("gb", D_MODEL), ("ba", LANES))
_IN_TOTAL = sum(w for _, w in _IN_BLOCKS)
_IN_COL_CHUNK = 512


def _inproj_body(x_ref, sc_ref, sh_ref, g_ref, w_ref, *out_refs):
    x = x_ref[...]
    y = x * lax.rsqrt(jnp.mean(x * x, axis=-1, keepdims=True) + EPS) * g_ref[...]
    h = (y * (1.0 + _mod_rows(sc_ref)) + _mod_rows(sh_ref)).astype(BF16)
    off = 0
    for (_, width), o_ref in zip(_IN_BLOCKS, out_refs):
        for c0 in range(0, width, _IN_COL_CHUNK):
            cw = min(_IN_COL_CHUNK, width - c0)
            o_ref[:, c0:c0 + cw] = jnp.dot(h, w_ref[:, off + c0:off + c0 + cw],
                                           preferred_element_type=F32)
        off += width


def _inproj(x2d, mod, rows_per_batch, norm_g, w_cat, tm):
    t = x2d.shape[0]
    row = lambda w: pl.BlockSpec((tm, w), lambda i: (i, 0))
    return pl.pallas_call(
        _inproj_body,
        grid=(t // tm,),
        in_specs=[row(D_MODEL), _mod_spec(1, rows_per_batch, tm), _mod_spec(0, rows_per_batch, tm),
                  _const_spec((1, D_MODEL)),
                  pl.BlockSpec((D_MODEL, _IN_TOTAL), lambda i: (0, 0), pipeline_mode=pl.Buffered(1))],
        out_specs=[row(w) for _, w in _IN_BLOCKS],
        out_shape=[jax.ShapeDtypeStruct((t, w), F32) for _, w in _IN_BLOCKS],
        compiler_params=pltpu.CompilerParams(vmem_limit_bytes=VMEM_LIMIT),
        name="inproj",
    )(x2d, mod, mod, norm_g.reshape(1, -1), w_cat)


def _mixer_body(C, Lv, start,
                qkv_ref, ba_ref, z_ref, pin_ref, ga_ref, gb_ref, cbuf_ref, s0_ref, pbuf_ref,
                convw_ref, alog_ref, dtb_ref, dng_ref, wpool_ref, pscale_ref,
                mixed_ref, nconv_ref, ns_ref, npool_ref,
                xp_scr, act_scr, s_scr, pp_scr, odn_scr):
    n = pl.program_id(1)
    last = pl.num_programs(1) - 1

    @pl.when(n == 0)
    def _load_state():
        xp_scr[0:CONV_PAD, :] = cbuf_ref[0]
        pp_scr[0:POOL_PAD, :] = pbuf_ref[0]
        s_scr[...] = s0_ref[0]

    xp_scr[CONV_PAD:CONV_PAD + C, :] = qkv_ref[0]
    for c0 in range(0, QKV_WIDTH, 512):
        cs = slice(c0, c0 + 512)
        y = xp_scr[CONV_PAD:CONV_PAD + C, cs] * convw_ref[CONV_WIDTH - 1:CONV_WIDTH, cs]
        for k in range(CONV_WIDTH - 1):
            r0 = CONV_PAD - (CONV_WIDTH - 1) + k
            y = y + xp_scr[r0:r0 + C, cs] * convw_ref[k:k + 1, cs]
        act_scr[:, cs] = _silu(y)

    ba = ba_ref[0]
    lane = lax.broadcasted_iota(I32, (C, LANES), 1)
    beta_all = jax.nn.sigmoid(ba)
    g_all = -jnp.exp(alog_ref[...]) * _softplus(ba + dtb_ref[...])
    if Lv < C:
        valid = lax.broadcasted_iota(I32, (C, LANES), 0) < Lv
        beta_all = jnp.where(valid, beta_all, 0.0)
        g_all = jnp.where(valid, g_all, 0.0)
    ii = lax.broadcasted_iota(I32, (C, C), 0)
    jj = lax.broadcasted_iota(I32, (C, C), 1)
    causal = ii >= jj
    strict = ii > jj
    tril = jnp.where(causal, 1.0, 0.0).astype(BF16)
    eye = jnp.where(ii == jj, 1.0, 0.0)
    gc_all = sum(jnp.dot(tril, part, preferred_element_type=F32) for part in _split3(g_all))
    if C < LANES:
        gc_sq = jnp.concatenate([gc_all, jnp.zeros((LANES - C, LANES), F32)], axis=0)
    else:
        gc_sq = gc_all
    gc_t = gc_sq.T

    H = range(DN_HEADS)
    hsl = [slice(h * DN_HEAD_DIM, (h + 1) * DN_HEAD_DIM) for h in H]
    beta = [jnp.sum(jnp.where(lane == h, beta_all, 0.0), axis=1, keepdims=True) for h in H]
    gcol = [jnp.sum(jnp.where(lane == DN_HEADS + h, gc_all, 0.0), axis=1, keepdims=True) for h in H]
    grow = [gc_t[DN_HEADS + h:DN_HEADS + h + 1, 0:C] for h in H]
    glast = [g[C - 1:C, :] for g in gcol]
    q = [act_scr[:, hsl[h]] for h in H]
    k = [act_scr[:, DN_WIDTH + h * DN_HEAD_DIM:DN_WIDTH + (h + 1) * DN_HEAD_DIM] for h in H]
    v = [act_scr[:, 2 * DN_WIDTH + h * DN_HEAD_DIM:2 * DN_WIDTH + (h + 1) * DN_HEAD_DIM] for h in H]
    q = [x * lax.rsqrt(jnp.sum(x * x, axis=-1, keepdims=True) + EPS) * (DN_HEAD_DIM ** -0.5) for x in q]
    k = [x * lax.rsqrt(jnp.sum(x * x, axis=-1, keepdims=True) + EPS) for x in k]
    kb = [k[h] * beta[h] for h in H]
    vb = [v[h] * beta[h] for h in H]
    decay = [jnp.where(causal, jnp.exp(jnp.where(causal, gcol[h] - grow[h], 0.0)), 0.0) for h in H]
    lower = [jnp.where(strict, _dot_nt(kb[h], k[h]) * decay[h], 0.0) for h in H]
    ainv = [eye - x for x in lower]
    pw = lower
    p = 1
    while 2 * p < C:
        pw = [_dot(x, x) for x in pw]
        ainv = [ainv[h] + _dot(ainv[h], pw[h]) for h in H]
        p *= 2
    sol = [_dot(ainv[h], jnp.concatenate([vb[h], kb[h] * jnp.exp(gcol[h])], axis=1)) for h in H]
    qk = [_dot_nt(q[h], k[h]) * decay[h] for h in H]
    k_tail = [k[h] * jnp.exp(glast[h] - gcol[h]) for h in H]
    S = [s_scr[h] for h in H]
    v_new = [sol[h][:, :DN_HEAD_DIM] - _dot(sol[h][:, DN_HEAD_DIM:], S[h]) for h in H]
    o = [_dot(q[h] * jnp.exp(gcol[h]), S[h]) + _dot(qk[h], v_new[h]) for h in H]
    for h in H:
        s_scr[h] = S[h] * jnp.exp(glast[h]) + lax.dot_general(
            k_tail[h].astype(BF16), v_new[h].astype(BF16), TN_DIMS, preferred_element_type=F32)
    for h in H:
        zf = z_ref[0, :, hsl[h]]
        odn_scr[:, hsl[h]] = (o[h] * lax.rsqrt(jnp.mean(o[h] * o[h], axis=-1, keepdims=True) + EPS)
                              * dng_ref[...] * _silu(zf))

    pp_scr[POOL_PAD:POOL_PAD + C, :] = pin_ref[0]
    pos = start + n * C + lax.broadcasted_iota(I32, (C, 1), 0)
    for gi, win in enumerate(POOL_WINDOWS):
        gs = slice(gi * POOL_GROUP_DIM, (gi + 1) * POOL_GROUP_DIM)
        xg = pp_scr[POOL_PAD:POOL_PAD + C, gs]
        ssum = xg
        for sft in range(1, win):
            ssum = ssum + pp_scr[POOL_PAD - sft:POOL_PAD - sft + C, gs]
        cnt = jnp.minimum(pos + 1, win).astype(F32)
        pooled = ssum / cnt - xg
        os_ = slice(gi * POOL_OUT_GROUP, (gi + 1) * POOL_OUT_GROUP)
        yp = _dot(pooled, wpool_ref[gi]) * pscale_ref[:, os_]
        mixed_ref[0, :, os_] = (jax.nn.sigmoid(ga_ref[0, :, os_]) * odn_scr[:, os_]
                                + jax.nn.sigmoid(gb_ref[0, :, os_]) * yp)

    @pl.when(n == last)
    def _store_state():
        nconv_ref[0] = xp_scr[Lv + CONV_PAD - (CONV_WIDTH - 1):Lv + CONV_PAD, :]
        npool_ref[0] = pp_scr[Lv + POOL_PAD - POOL_BUF:Lv + POOL_PAD, :]
        ns_ref[0] = s_scr[...]

    xp_scr[0:CONV_PAD, :] = xp_scr[C:C + CONV_PAD, :]
    pp_scr[0:POOL_PAD, :] = pp_scr[C:C + POOL_PAD, :]


def _mixer(proj, conv_buf, s0, pool_buf, start, seq_len, C,
           conv_w, a_log, dt_bias, dn_norm_g, w_pool, pool_scale):
    b, lp, _ = proj["qkv"].shape
    nchunks = lp // C
    lv = seq_len - (nchunks - 1) * C
    cbuf = jnp.pad(conv_buf, ((0, 0), (CONV_PAD - (CONV_WIDTH - 1), 0), (0, 0)))
    pbuf = jnp.pad(pool_buf, ((0, 0), (POOL_PAD - POOL_BUF, 0), (0, 0)))
    lane_pad = lambda a: jnp.pad(a.reshape(1, -1), ((0, 0), (DN_HEADS, LANES - 2 * DN_HEADS)))
    chunk = lambda w: pl.BlockSpec((1, C, w), lambda i, j: (i, j, 0))
    state = lambda *s: pl.BlockSpec((1,) + s, lambda i, j: (i,) + (0,) * len(s))
    return pl.pallas_call(
        functools.partial(_mixer_body, C, lv, start),
        grid=(b, nchunks),
        in_specs=[chunk(QKV_WIDTH), chunk(LANES), chunk(DN_WIDTH), chunk(POOL_WIDTH),
                  chunk(D_MODEL), chunk(D_MODEL),
                  state(CONV_PAD, QKV_WIDTH), state(DN_HEADS, DN_HEAD_DIM, DN_HEAD_DIM),
                  state(POOL_PAD, POOL_WIDTH),
                  _const_spec((CONV_WIDTH, QKV_WIDTH)), _const_spec((1, LANES)), _const_spec((1, LANES)),
                  _const_spec((1, DN_HEAD_DIM)),
                  _const_spec((len(POOL_WINDOWS), POOL_GROUP_DIM, POOL_OUT_GROUP)),
                  _const_spec((1, D_MODEL))],
        out_specs=[chunk(D_MODEL), state(CONV_WIDTH - 1, QKV_WIDTH),
                   state(DN_HEADS, DN_HEAD_DIM, DN_HEAD_DIM), state(POOL_BUF, POOL_WIDTH)],
        out_shape=[jax.ShapeDtypeStruct((b, lp, D_MODEL), F32),
                   jax.ShapeDtypeStruct((b, CONV_WIDTH - 1, QKV_WIDTH), F32),
                   jax.ShapeDtypeStruct((b, DN_HEADS, DN_HEAD_DIM, DN_HEAD_DIM), F32),
                   jax.ShapeDtypeStruct((b, POOL_BUF, POOL_WIDTH), F32)],
        scratch_shapes=[pltpu.VMEM((CONV_PAD + C + CONV_PAD, QKV_WIDTH), F32),
                        pltpu.VMEM((C, QKV_WIDTH), F32),
                        pltpu.VMEM((DN_HEADS, DN_HEAD_DIM, DN_HEAD_DIM), F32),
                        pltpu.VMEM((POOL_PAD + C + POOL_PAD, POOL_WIDTH), F32),
                        pltpu.VMEM((C, DN_WIDTH), F32)],
        compiler_params=pltpu.CompilerParams(dimension_semantics=("arbitrary", "arbitrary"),
                                             vmem_limit_bytes=VMEM_LIMIT),
        name="mixer",
    )(proj["qkv"], proj["ba"], proj["z"], proj["pool"], proj["ga"], proj["gb"], cbuf, s0, pbuf,
      conv_w, lane_pad(a_log), lane_pad(dt_bias), dn_norm_g.reshape(1, -1), w_pool,
      pool_scale.reshape(1, -1))


def _top16(s, ids, payload=None):
    big = float(2 ** 24)
    vals, sel, pays = [], [], []
    for _ in range(PEER_TOPK):
        m = jnp.max(s, axis=0, keepdims=True)
        am = jnp.min(jnp.where(s == m, ids, big), axis=0, keepdims=True)
        hit = ids == am
        if payload is not None:
            pays.append(jnp.max(jnp.where(hit, payload, -1.0), axis=0, keepdims=True))
        s = jnp.where(hit, -jnp.inf, s)
        vals.append(m)
        sel.append(am)
    out = (jnp.concatenate(vals, axis=0), jnp.concatenate(sel, axis=0))
    if payload is not None:
        out += (jnp.concatenate(pays, axis=0),)
    return out


_CAND_EDGE = 4


def _post_body(has_prev, has_fin, mixed_ref, x_ref, g1_ref, sc2_ref, sh2_ref, n2g_ref, wout_ref,
               wq_ref, keys_ref, *refs):
    refs = list(refs)
    prev_in = [refs.pop(0) for _ in range(2 if has_prev else 0)]
    fin_in = [refs.pop(0) for _ in range(4 if has_fin else 0)]
    x1_ref, h2_ref, idx_ref, gate_ref = refs[:4]
    extra_out = refs[4:]
    if has_prev:
        pre_ref, pgate_ref = prev_in
        extra_out.pop(0)[...] = _coef_words(pre_ref[...], pgate_ref[...])
    if has_fin:
        _final_body(*fin_in, extra_out.pop(0))
    tm = x_ref.shape[0]
    x1 = x_ref[...] + _mod_rows(g1_ref) * _dot(mixed_ref[...], wout_ref[...])
    x1_ref[...] = x1
    y = x1 * lax.rsqrt(jnp.mean(x1 * x1, axis=-1, keepdims=True) + EPS) * n2g_ref[...]
    h2 = y * (1.0 + _mod_rows(sc2_ref)) + _mod_rows(sh2_ref)
    h2_ref[...] = _pack_words(h2[:, :PACK_HALF], h2[:, PACK_HALF:])
    q = _dot(h2, wq_ref[...])

    K = PEER_TOPK
    key_id = lax.broadcasted_iota(I32, (PEER_NKEYS, 1), 0).astype(F32)
    r16 = lax.broadcasted_iota(I32, (K, 1), 0)
    cand_id = jnp.concatenate([(a * K + r16) for a in range(_CAND_EDGE)]
                              + [(r16 * K + b) for b in range(_CAND_EDGE)], axis=0).astype(F32)
    dup = r16 < _CAND_EDGE
    idx_rows, gate_rows = [], []
    for h in range(PEER_HEADS):
        half = []
        for p in range(2):
            c0 = (h * 2 + p) * PEER_KEY_HALF
            st = _dot_nt(keys_ref[h * 2 + p], q[:, c0:c0 + PEER_KEY_HALF])
            half.append(_top16(st, key_id))
        (s1, i1), (s2, i2) = half
        cand = jnp.concatenate(
            [s1[a:a + 1] + s2 for a in range(_CAND_EDGE)]
            + [jnp.where(dup, -jnp.inf, s1 + s2[b:b + 1]) for b in range(_CAND_EDGE)], axis=0)
        cidx = jnp.concatenate(
            [i1[a:a + 1] * PEER_NKEYS + i2 for a in range(_CAND_EDGE)]
            + [i1 * PEER_NKEYS + i2[b:b + 1] for b in range(_CAND_EDGE)], axis=0)
        best, _, eidx = _top16(cand, cand_id, cidx)
        e = jnp.exp(best - best[0:1])
        gate_rows.append(e / jnp.sum(e, axis=0, keepdims=True))
        idx_rows.append(eidx)
    idx_ref[...] = jnp.concatenate(idx_rows, axis=0).T.astype(I32)
    gate_ref[...] = jnp.concatenate(gate_rows, axis=0).T


def _post(mixed2d, x2d, mod, rows_per_batch, norm2_g, w_out, w_query, keys, tm, prev=None, fin=None):
    t = x2d.shape[0]
    steps = t // tm
    row = lambda w: pl.BlockSpec((tm, w), lambda i: (i, 0))
    in_specs = [row(D_MODEL), row(D_MODEL),
                _mod_spec(2, rows_per_batch, tm), _mod_spec(4, rows_per_batch, tm),
                _mod_spec(3, rows_per_batch, tm), _const_spec((1, D_MODEL)),
                _const_spec((D_MODEL, D_MODEL)), _const_spec((D_MODEL, 2 * PEER_HEADS * PEER_KEY_HALF)),
                _const_spec((2 * PEER_HEADS, PEER_NKEYS, PEER_KEY_HALF))]
    out_specs = [row(D_MODEL), row(PACK_HALF), row(PEER_HK), row(PEER_HK)]
    out_shape = [jax.ShapeDtypeStruct((t, D_MODEL), F32), jax.ShapeDtypeStruct((t, PACK_HALF), I32),
                 jax.ShapeDtypeStruct((t, PEER_HK), I32), jax.ShapeDtypeStruct((t, PEER_HK), F32)]
    args = [mixed2d, x2d, mod, mod, mod, norm2_g.reshape(1, -1), w_out, w_query, keys]
    if prev is not None:
        tp = prev[0].shape[0]
        prow = pl.BlockSpec((tp // steps, PEER_HK), lambda i: (i, 0))
        in_specs += [prow, prow]
        out_specs += [prow]
        out_shape += [jax.ShapeDtypeStruct((tp, PEER_HK), I32)]
        args += list(prev)
    if fin is not None:
        x1_f, peer_f, mod_f, rows_f, final_g = fin
        tf = x1_f.shape[0]
        frow = pl.BlockSpec((tf // steps, D_MODEL), lambda i: (i, 0))
        in_specs += [frow, frow, _mod_spec(5, rows_f, tf // steps), _const_spec((1, D_MODEL))]
        out_specs += [frow]
        out_shape += [jax.ShapeDtypeStruct((tf, D_MODEL), F32)]
        args += [x1_f, peer_f, mod_f, final_g.reshape(1, -1)]
    return pl.pallas_call(
        functools.partial(_post_body, prev is not None, fin is not None),
        grid=(steps,),
        in_specs=in_specs, out_specs=out_specs, out_shape=out_shape,
        compiler_params=pltpu.CompilerParams(vmem_limit_bytes=VMEM_LIMIT),
        name="post",
    )(*args)


SC_CORES = 2
SC_SUBCORES = 16
SC_LANES = 16
SC_WORKERS = SC_CORES * SC_SUBCORES
SC_TOKENS = 16
SC_SLOTS = 4
SC_JOB_HEADS = 2
SC_BF16_GROUP = 4
PACK_HALF = D_MODEL // 2
SC_CHUNKS = PACK_HALF // SC_LANES
PROMPT_PARTS = 8
SEQ_SPLITS = 1
FIN_LAG = 3
ROW_TILE = 256


def _bf16_bits(v):
    return lax.bitcast_convert_type(v.astype(BF16).astype(F32), jnp.uint32)


def _pack_words(lo, hi):
    return lax.bitcast_convert_type((_bf16_bits(lo) >> 16) | _bf16_bits(hi), I32)


def _pack_body(x_ref, o_ref):
    o_ref[...] = _pack_words(x_ref[:, :PACK_HALF], x_ref[:, PACK_HALF:])


def _pack_table(tbl, rows=512):
    e = tbl.shape[0]
    return pl.pallas_call(
        _pack_body, grid=(e // rows,),
        in_specs=[pl.BlockSpec((rows, D_MODEL), lambda i: (i, 0))],
        out_specs=pl.BlockSpec((rows, PACK_HALF), lambda i: (i, 0)),
        out_shape=jax.ShapeDtypeStruct((e, PACK_HALF), I32), name="pack_table")(tbl)


def _tree_sum(terms):
    terms = list(terms)
    while len(terms) > 1:
        terms = [a + b for a, b in zip(terms[0::2], terms[1::2])] + terms[len(terms) & ~1:]
    return terms[0]


def _unpack_pair(w):
    lo = plsc.bitcast(lax.shift_left(w, jnp.full(w.shape, 16, I32)), F32)
    hi = plsc.bitcast(w & jnp.full(w.shape, -65536, I32), F32)
    return lo, hi


def _sc_mesh():
    return plsc.VectorSubcoreMesh(core_axis_name="c", subcore_axis_name="s")


def _sc_worker():
    return lax.axis_index("s") * SC_CORES + lax.axis_index("c")


def _sc_jobs(table_hbm, idx_v, buf, sem, compute):
    per_tok = PEER_HEADS // SC_JOB_HEADS
    njobs = idx_v.shape[0] * per_tok
    nrows = SC_JOB_HEADS * PEER_TOPK

    def copy(j, slot):
        rows = idx_v.at[j // per_tok, pl.ds((j % per_tok) * nrows, nrows)]
        return pltpu.make_async_copy(table_hbm.at[rows], buf.at[slot], sem.at[slot])

    for s in range(SC_SLOTS):
        copy(s, s).start()

    def job(j, c):
        s = j % SC_SLOTS
        copy(j, s).wait()

        def head(i, cc):
            compute(j // per_tok, (j % per_tok) * SC_JOB_HEADS + i, s, i * PEER_TOPK)
            return cc
        lax.fori_loop(0, SC_JOB_HEADS, head, 0)

        @pl.when(j + SC_SLOTS < njobs)
        def _next():
            copy(j + SC_SLOTS, s).start()
        return c

    lax.fori_loop(0, njobs, job, 0)


def _peer_u_body(n_tok, idx_hbm, h2_hbm, u_hbm, pre_hbm, idx_v, h2_v, pre_v, ubuf, acc_v, sem):
    base = _sc_worker() * n_tok
    lane = lax.iota(I32, SC_LANES)

    def compute(tt, h, slot, r0):
        def chunk(cg, accs):
            cs = [pl.ds((cg * SC_BF16_GROUP + i) * SC_LANES, SC_LANES) for i in range(SC_BF16_GROUP)]
            xs = [plsc.bitcast(h2_v[tt, c], BF16) for c in cs]
            out = []
            for k, a in enumerate(accs):
                part = _tree_sum([plsc.bitcast(ubuf[slot, r0 + k, c], BF16) * x for c, x in zip(cs, xs)])
                lo, hi = _unpack_pair(plsc.bitcast(part, I32))
                out.append(a + (lo + hi))
            return tuple(out)
        zero = jnp.zeros((SC_LANES,), F32)
        accs = lax.fori_loop(0, SC_CHUNKS // SC_BF16_GROUP, chunk, (zero,) * PEER_TOPK)
        for k, a in enumerate(accs):
            acc_v[k, :] = a
        tot = zero
        for j in range(SC_LANES):
            tot = tot + plsc.load_gather(acc_v, [lane, (lane + j) & (SC_LANES - 1)])
        pre_v[tt, pl.ds(h * PEER_TOPK, PEER_TOPK)] = tot

    tb = idx_v.shape[0]

    def block(bi, c):
        t0 = base + bi * tb
        pltpu.sync_copy(idx_hbm.at[pl.ds(t0, tb)], idx_v)
        pltpu.sync_copy(h2_hbm.at[pl.ds(t0, tb)], h2_v)
        _sc_jobs(u_hbm, idx_v, ubuf, sem, compute)
        pltpu.sync_copy(pre_v, pre_hbm.at[pl.ds(t0, tb)])
        return c

    lax.fori_loop(0, n_tok // tb, block, 0)


def _peer_v_body(n_tok, idx_hbm, coef_hbm, v_hbm, out_hbm, idx_v, coef_v, out_v, vbuf, sem):
    base = _sc_worker() * n_tok
    zero = jnp.zeros((SC_LANES,), F32)

    def compute(tt, h, slot, r0):
        row = jnp.full((SC_LANES,), tt, I32)
        cb = [plsc.bitcast(plsc.load_gather(
                  coef_v, [row, jnp.full((SC_LANES,), h * PEER_TOPK + k, I32)]), BF16)
              for k in range(PEER_TOPK)]

        @plsc.parallel_loop(0, SC_CHUNKS, unroll=2)
        def _chunk(c):
            cs = pl.ds(c * SC_LANES, SC_LANES)
            prods = [plsc.bitcast(vbuf[slot, r0 + k, cs], BF16) * cb[k] for k in range(PEER_TOPK)]
            pairs = [_unpack_pair(plsc.bitcast(_tree_sum(prods[g:g + SC_BF16_GROUP]), I32))
                     for g in range(0, PEER_TOPK, SC_BF16_GROUP)]
            for half, off in ((0, 0), (1, PACK_HALF)):
                plsc.addupdate(out_v.at[tt, pl.ds(off + c * SC_LANES, SC_LANES)],
                               _tree_sum([p[half] for p in pairs]))

    tb = idx_v.shape[0]

    def block(bi, c):
        t0 = base + bi * tb
        pltpu.sync_copy(idx_hbm.at[pl.ds(t0, tb)], idx_v)
        pltpu.sync_copy(coef_hbm.at[pl.ds(t0, tb)], coef_v)

        def clear(i, cc):
            per_row = D_MODEL // SC_LANES
            out_v[i // per_row, pl.ds((i % per_row) * SC_LANES, SC_LANES)] = zero
            return cc
        lax.fori_loop(0, tb * (D_MODEL // SC_LANES), clear, 0)
        _sc_jobs(v_hbm, idx_v, vbuf, sem, compute)
        pltpu.sync_copy(out_v, out_hbm.at[pl.ds(t0, tb)])
        return c

    lax.fori_loop(0, n_tok // tb, block, 0)


def _peer_sc(body, idx, rows, table, out_width, name):
    t = idx.shape[0]
    assert t % SC_WORKERS == 0
    n_tok = t // SC_WORKERS
    tb = min(SC_TOKENS, n_tok)
    assert n_tok % tb == 0 and (tb * PEER_HEADS // SC_JOB_HEADS) % SC_SLOTS == 0
    return pl.kernel(
        functools.partial(body, n_tok),
        out_type=jax.ShapeDtypeStruct((t, out_width), F32),
        mesh=_sc_mesh(),
        scratch_types=[pltpu.VMEM((tb, PEER_HK), I32),
                       pltpu.VMEM((tb, rows.shape[1]), rows.dtype),
                       pltpu.VMEM((tb, out_width), F32),
                       pltpu.VMEM((SC_SLOTS, SC_JOB_HEADS * PEER_TOPK, PACK_HALF), I32)]
                      + ([pltpu.VMEM((PEER_TOPK, SC_LANES), F32)] if body is _peer_u_body else [])
                      + [pltpu.SemaphoreType.DMA((SC_SLOTS,))],
        compiler_params=pltpu.CompilerParams(needs_layout_passes=False),
        name=name,
    )(idx, rows, table)


def _coef_words(pre, gates):
    return _pack_words(*(gates * _gelu(pre),) * 2)


def _coef_body(pre_ref, gate_ref, coef_ref):
    coef_ref[...] = _coef_words(pre_ref[...], gate_ref[...])


def _coef(pre, gates, tm):
    t = pre.shape[0]
    row = pl.BlockSpec((tm, PEER_HK), lambda i: (i, 0))
    return pl.pallas_call(_coef_body, grid=(t // tm,), in_specs=[row, row], out_specs=row,
                          out_shape=jax.ShapeDtypeStruct((t, PEER_HK), I32), name="coef")(pre, gates)


def _final_body(x1_ref, peer_ref, g2_ref, fng_ref, y_ref):
    x2 = x1_ref[...] + _mod_rows(g2_ref) * peer_ref[...]
    y_ref[...] = x2 * lax.rsqrt(jnp.mean(x2 * x2, axis=-1, keepdims=True) + EPS) * fng_ref[...]


def _final(x1, peer_out, mod, rows_per_batch, final_g, tm):
    t = x1.shape[0]
    row = pl.BlockSpec((tm, D_MODEL), lambda i: (i, 0))
    return pl.pallas_call(
        _final_body, grid=(t // tm,),
        in_specs=[row, row, _mod_spec(5, rows_per_batch, tm), _const_spec((1, D_MODEL))],
        out_specs=row, out_shape=jax.ShapeDtypeStruct((t, D_MODEL), F32), name="final",
    )(x1, peer_out, mod, final_g.reshape(1, -1))


def _expert_gather_v(g, coef, expert_v):
    g["peer_out"] = _peer_sc(_peer_v_body, g["idx"], coef, expert_v, D_MODEL, "peer_v")


def _front(x, mod, conv_buf, s0, pool_buf, start, chunk, tm, wts, prev, fin):
    b, l, _ = x.shape
    t = b * l
    x2d = x.reshape(t, D_MODEL)
    if l >= tm:
        modx = mod.reshape(b, 6, 1, D_MODEL).transpose(1, 0, 2, 3)
    else:
        modx = jnp.repeat(mod.reshape(b, 6, D_MODEL), l, axis=0).transpose(1, 0, 2)
    outs = _inproj(x2d, modx, l, wts["norm1_g"], wts["w_cat"], tm)
    lp = -(-l // chunk) * chunk
    proj = {}
    for (name, w), a in zip(_IN_BLOCKS, outs):
        a = a.reshape(b, l, w)
        proj[name] = a if lp == l else jnp.pad(a, ((0, 0), (0, lp - l), (0, 0)))
    mixed, nconv, ns, npool = _mixer(proj, conv_buf, s0, pool_buf, start, l, chunk,
                                     wts["conv_w"], wts["a_log"], wts["dt_bias"], wts["dn_norm_g"],
                                     wts["w_pool"], wts["pool_scale"])
    mixed2d = mixed[:, :l].reshape(t, D_MODEL)
    res = _post(mixed2d, x2d, modx, l, wts["norm2_g"], wts["w_out"], wts["w_query"], wts["keys"], tm,
                prev=None if prev is None else (prev["pre"], prev["gates"]),
                fin=None if fin is None else (fin["x1"], fin["peer_out"], fin["mod"], fin["l"],
                                              wts["final_norm_g"]))
    x1, h2, idx, gates = res[:4]
    extra = list(res[4:])
    coef_prev = extra.pop(0) if prev is not None else None
    y_fin = extra.pop(0).reshape(fin["b"], fin["l"], D_MODEL) if fin is not None else None
    pre = _peer_sc(_peer_u_body, idx, h2, wts["expert_u"], PEER_HK, "peer_u")
    g = dict(x1=x1, idx=idx, gates=gates, pre=pre, mod=modx, b=b, l=l, tm=tm,
             states=(nconv, ns, npool))
    return g, coef_prev, y_fin


def kernel(x_prompt, x_sample, c_prompt, c_sample, state_conv, state_delta, state_pool, w_ada, b_ada, norm1_g, w_in, conv_w, a_log, dt_bias, dn_norm_g, w_pool, pool_scale, w_out, norm2_g, w_query, sub_keys, expert_u, expert_v, final_norm_g):
    bp = x_prompt.shape[0]
    bs = x_sample.shape[0]
    yp, ys = x_prompt, x_sample
    conv_p, delta_p, pool_p, conv_s, delta_s, pool_s = [], [], [], [], [], []
    zero_conv = jnp.zeros((bp, CONV_WIDTH - 1, QKV_WIDTH), F32)
    zero_delta = jnp.zeros((bp, DN_HEADS, DN_HEAD_DIM, DN_HEAD_DIM), F32)
    zero_pool = jnp.zeros((bp, POOL_BUF, POOL_WIDTH), F32)
    c_all = jnp.concatenate([c_prompt, c_sample], axis=0)
    for layer in range(DEPTH):
        wi = w_in[layer]
        o_b = QKV_WIDTH
        o_z = o_b + 2 * DN_HEADS
        w_ba = jnp.pad(wi[:, o_b:o_z], ((0, 0), (0, LANES - 2 * DN_HEADS)))
        w_cat = jnp.concatenate([wi[:, :o_b], wi[:, o_z:], w_ba], axis=1).astype(BF16)
        last = layer == DEPTH - 1
        wts = dict(
            norm1_g=norm1_g[layer], w_cat=w_cat, conv_w=conv_w[layer], a_log=a_log[layer],
            dt_bias=dt_bias[layer], dn_norm_g=dn_norm_g[layer], w_pool=w_pool[layer],
            pool_scale=pool_scale[layer], w_out=w_out[layer].astype(BF16), norm2_g=norm2_g[layer],
            w_query=w_query[layer].astype(BF16),
            keys=sub_keys[layer].reshape(2 * PEER_HEADS, PEER_NKEYS, PEER_KEY_HALF).astype(BF16),
            expert_u=_pack_table(expert_u[layer]), expert_v=_pack_table(expert_v[layer]),
            final_norm_g=final_norm_g if last else jnp.ones_like(final_norm_g))
        mod = _ada(c_all, w_ada[layer], b_ada[layer])
        assert last, "final norm is fused into the expert stage"
        step = bp // PROMPT_PARTS
        ls = x_prompt.shape[1] // SEQ_SPLITS
        zeros = (zero_conv[:step], zero_delta[:step], zero_pool[:step])
        jobs = [(yp[b0:b0 + step, s0:s0 + ls], mod[b0:b0 + step], zeros if s0 == 0 else None, s0, DN_CHUNK)
                for b0 in range(0, bp, step) for s0 in range(0, SEQ_SPLITS * ls, ls)]
        jobs.append((ys, mod[bp:], (state_conv[layer], state_delta[layer], state_pool[layer]),
                     PAST_LEN, SUBLANES))
        groups = []
        for j, (xg, mg, states, start, chunk) in enumerate(jobs):
            prev = groups[j - 1] if j >= 1 else None
            fin = groups[j - FIN_LAG] if j >= FIN_LAG else None
            if fin is not None and fin["x1"].shape[0] != xg.shape[0] * xg.shape[1]:
                fin = None
            if states is None:
                states = prev["states"]
            g, coef_prev, y_fin = _front(xg, mg, *states, start, chunk, ROW_TILE, wts, prev, fin)
            if prev is not None:
                _expert_gather_v(prev, coef_prev, wts["expert_v"])
            if fin is not None:
                fin["y"] = y_fin
            groups.append(g)
        _expert_gather_v(groups[-1], _coef(groups[-1]["pre"], groups[-1]["gates"], ROW_TILE),
                         wts["expert_v"])
        for g in groups:
            if "y" not in g:
                g["y"] = _final(g["x1"], g["peer_out"], g["mod"], g["l"], wts["final_norm_g"],
                                g["tm"]).reshape(g["b"], g["l"], D_MODEL)
        rows = [groups[i:i + SEQ_SPLITS] for i in range(0, len(groups) - 1, SEQ_SPLITS)]
        yp = jnp.concatenate([jnp.concatenate([g["y"] for g in row], axis=1) for row in rows], axis=0)
        cp, sp, pp = (jnp.concatenate(a, axis=0) for a in zip(*(row[-1]["states"] for row in rows)))
        ys = groups[-1]["y"]
        cs, ss, ps = groups[-1]["states"]
        conv_p.append(cp)
        delta_p.append(sp)
        pool_p.append(pp)
        conv_s.append(cs)
        delta_s.append(ss)
        pool_s.append(ps)
    return (yp, ys, jnp.stack(conv_p), jnp.stack(delta_p), jnp.stack(pool_p),
            jnp.stack(conv_s), jnp.stack(delta_s), jnp.stack(pool_s))
```

```python
import functools

import jax
import jax.numpy as jnp
from jax import lax
from jax.experimental import pallas as pl
from jax.experimental.pallas import tpu as pltpu
from jax.experimental.pallas import tpu_sc as plsc

F32 = jnp.float32
BF16 = jnp.bfloat16
I32 = jnp.int32

D_MODEL = 1024
DEPTH = 1
PAST_LEN = 16384
DN_HEADS = 8
DN_HEAD_DIM = 128
DN_WIDTH = DN_HEADS * DN_HEAD_DIM
QKV_WIDTH = 3 * DN_WIDTH
CONV_WIDTH = 4
DN_CHUNK = 64
POOL_WINDOWS = (2, 4, 8, 16)
POOL_GROUP_DIM = 128
POOL_WIDTH = len(POOL_WINDOWS) * POOL_GROUP_DIM
POOL_OUT_GROUP = D_MODEL // len(POOL_WINDOWS)
POOL_BUF = max(POOL_WINDOWS) - 1
PEER_HEADS = 8
PEER_NKEYS = 128
PEER_TOPK = 16
PEER_KEY_HALF = 128
PEER_HK = PEER_HEADS * PEER_TOPK
EPS = 1e-6

LANES = 128
SUBLANES = 8
CONV_PAD = SUBLANES
POOL_PAD = 16
VMEM_LIMIT = 56 * 1024 * 1024

NT_DIMS = (((1,), (1,)), ((), ()))
TN_DIMS = (((0,), (0,)), ((), ()))


def _dot(a, b):
    return jnp.dot(a.astype(BF16), b.astype(BF16), preferred_element_type=F32)


def _dot_nt(a, b):
    return lax.dot_general(a.astype(BF16), b.astype(BF16), NT_DIMS, preferred_element_type=F32)


def _split3(x):
    hi = x.astype(BF16)
    r1 = x - hi.astype(F32)
    mid = r1.astype(BF16)
    lo = (r1 - mid.astype(F32)).astype(BF16)
    return hi, mid, lo


def _silu(x):
    return x * jax.nn.sigmoid(x)


def _gelu(x):
    return 0.5 * x * (1.0 + lax.erf(x * (0.5 ** 0.5)))


def _softplus(x):
    return jnp.maximum(x, 0.0) + jnp.log(1.0 + jnp.exp(-jnp.abs(x)))


def _mod_rows(ref):
    m = ref[...]
    return m.reshape(m.shape[-2], m.shape[-1])


def _mod_spec(k, rows_per_batch, tm):
    if rows_per_batch >= tm:
        tiles = rows_per_batch // tm
        return pl.BlockSpec((1, 1, 1, D_MODEL), lambda i, *_: (k, i // tiles, 0, 0))
    return pl.BlockSpec((1, tm, D_MODEL), lambda i, *_: (k, i, 0))


def _const_spec(shape):
    nd = len(shape)
    return pl.BlockSpec(shape, lambda *_: (0,) * nd)


def _ada_body(c_ref, w_ref, b_ref, o_ref):
    o_ref[...] = _dot(_silu(c_ref[...]), w_ref[...]) + b_ref[...]


def _ada(c, w_ada, b_ada):
    n = c.shape[0]
    return pl.pallas_call(
        _ada_body,
        grid=(6,),
        in_specs=[pl.BlockSpec((n, D_MODEL), lambda j: (0, 0)),
                  pl.BlockSpec((D_MODEL, D_MODEL), lambda j: (0, j)),
                  pl.BlockSpec((1, D_MODEL), lambda j: (0, j))],
        out_specs=pl.BlockSpec((n, D_MODEL), lambda j: (0, j)),
        out_shape=jax.ShapeDtypeStruct((n, 6 * D_MODEL), F32),
        name="ada",
    )(c, w_ada, b_ada.reshape(1, -1))


_IN_BLOCKS = (("qkv", QKV_WIDTH), ("z", DN_WIDTH), ("pool", POOL_WIDTH),
              ("ga", D_MODEL), ("gb", D_MODEL), ("ba", LANES))
_IN_TOTAL = sum(w for _, w in _IN_BLOCKS)
_IN_COL_CHUNK = 512


def _inproj_body(x_ref, sc_ref, sh_ref, g_ref, w_ref, *out_refs):
    x = x_ref[...]
    y = x * lax.rsqrt(jnp.mean(x * x, axis=-1, keepdims=True) + EPS) * g_ref[...]
    h = (y * (1.0 + _mod_rows(sc_ref)) + _mod_rows(sh_ref)).astype(BF16)
    off = 0
    for (_, width), o_ref in zip(_IN_BLOCKS, out_refs):
        for c0 in range(0, width, _IN_COL_CHUNK):
            cw = min(_IN_COL_CHUNK, width - c0)
            o_ref[:, c0:c0 + cw] = jnp.dot(h, w_ref[:, off + c0:off + c0 + cw],
                                           preferred_element_type=F32)
        off += width


def _inproj(x2d, mod, rows_per_batch, norm_g, w_cat, tm):
    t = x2d.shape[0]
    row = lambda w: pl.BlockSpec((tm, w), lambda i: (i, 0))
    return pl.pallas_call(
        _inproj_body,
        grid=(t // tm,),
        in_specs=[row(D_MODEL), _mod_spec(1, rows_per_batch, tm), _mod_spec(0, rows_per_batch, tm),
                  _const_spec((1, D_MODEL)),
                  pl.BlockSpec((D_MODEL, _IN_TOTAL), lambda i: (0, 0), pipeline_mode=pl.Buffered(1))],
        out_specs=[row(w) for _, w in _IN_BLOCKS],
        out_shape=[jax.ShapeDtypeStruct((t, w), F32) for _, w in _IN_BLOCKS],
        compiler_params=pltpu.CompilerParams(vmem_limit_bytes=VMEM_LIMIT),
        name="inproj",
    )(x2d, mod, mod, norm_g.reshape(1, -1), w_cat)


def _mixer_body(C, Lv, start,
                qkv_ref, ba_ref, z_ref, pin_ref, ga_ref, gb_ref, cbuf_ref, s0_ref, pbuf_ref,
                convw_ref, alog_ref, dtb_ref, dng_ref, wpool_ref, pscale_ref,
                mixed_ref, nconv_ref, ns_ref, npool_ref,
                xp_scr, act_scr, s_scr, pp_scr, odn_scr):
    n = pl.program_id(1)
    last = pl.num_programs(1) - 1

    @pl.when(n == 0)
    def _load_state():
        xp_scr[0:CONV_PAD, :] = cbuf_ref[0]
        pp_scr[0:POOL_PAD, :] = pbuf_ref[0]
        s_scr[...] = s0_ref[0]

    xp_scr[CONV_PAD:CONV_PAD + C, :] = qkv_ref[0]
    for c0 in range(0, QKV_WIDTH, 512):
        cs = slice(c0, c0 + 512)
        y = xp_scr[CONV_PAD:CONV_PAD + C, cs] * convw_ref[CONV_WIDTH - 1:CONV_WIDTH, cs]
        for k in range(CONV_WIDTH - 1):
            r0 = CONV_PAD - (CONV_WIDTH - 1) + k
            y = y + xp_scr[r0:r0 + C, cs] * convw_ref[k:k + 1, cs]
        act_scr[:, cs] = _silu(y)

    ba = ba_ref[0]
    lane = lax.broadcasted_iota(I32, (C, LANES), 1)
    beta_all = jax.nn.sigmoid(ba)
    g_all = -jnp.exp(alog_ref[...]) * _softplus(ba + dtb_ref[...])
    if Lv < C:
        valid = lax.broadcasted_iota(I32, (C, LANES), 0) < Lv
        beta_all = jnp.where(valid, beta_all, 0.0)
        g_all = jnp.where(valid, g_all, 0.0)
    ii = lax.broadcasted_iota(I32, (C, C), 0)
    jj = lax.broadcasted_iota(I32, (C, C), 1)
    causal = ii >= jj
    strict = ii > jj
    tril = jnp.where(causal, 1.0, 0.0).astype(BF16)
    eye = jnp.where(ii == jj, 1.0, 0.0)
    gc_all = sum(jnp.dot(tril, part, preferred_element_type=F32) for part in _split3(g_all))
    if C < LANES:
        gc_sq = jnp.concatenate([gc_all, jnp.zeros((LANES - C, LANES), F32)], axis=0)
    else:
        gc_sq = gc_all
    gc_t = gc_sq.T

    H = range(DN_HEADS)
    hsl = [slice(h * DN_HEAD_DIM, (h + 1) * DN_HEAD_DIM) for h in H]
    beta = [jnp.sum(jnp.where(lane == h, beta_all, 0.0), axis=1, keepdims=True) for h in H]
    gcol = [jnp.sum(jnp.where(lane == DN_HEADS + h, gc_all, 0.0), axis=1, keepdims=True) for h in H]
    grow = [gc_t[DN_HEADS + h:DN_HEADS + h + 1, 0:C] for h in H]
    glast = [g[C - 1:C, :] for g in gcol]
    q = [act_scr[:, hsl[h]] for h in H]
    k = [act_scr[:, DN_WIDTH + h * DN_HEAD_DIM:DN_WIDTH + (h + 1) * DN_HEAD_DIM] for h in H]
    v = [act_scr[:, 2 * DN_WIDTH + h * DN_HEAD_DIM:2 * DN_WIDTH + (h + 1) * DN_HEAD_DIM] for h in H]
    q = [x * lax.rsqrt(jnp.sum(x * x, axis=-1, keepdims=True) + EPS) * (DN_HEAD_DIM ** -0.5) for x in q]
    k = [x * lax.rsqrt(jnp.sum(x * x, axis=-1, keepdims=True) + EPS) for x in k]
    kb = [k[h] * beta[h] for h in H]
    vb = [v[h] * beta[h] for h in H]
    decay = [jnp.where(causal, jnp.exp(jnp.where(causal, gcol[h] - grow[h], 0.0)), 0.0) for h in H]
    lower = [jnp.where(strict, _dot_nt(kb[h], k[h]) * decay[h], 0.0) for h in H]
    ainv = [eye - x for x in lower]
    pw = lower
    p = 1
    while 2 * p < C:
        pw = [_dot(x, x) for x in pw]
        ainv = [ainv[h] + _dot(ainv[h], pw[h]) for h in H]
        p *= 2
    sol = [_dot(ainv[h], jnp.concatenate([vb[h], kb[h] * jnp.exp(gcol[h])], axis=1)) for h in H]
    qk = [_dot_nt(q[h], k[h]) * decay[h] for h in H]
    k_tail = [k[h] * jnp.exp(glast[h] - gcol[h]) for h in H]
    S = [s_scr[h] for h in H]
    v_new = [sol[h][:, :DN_HEAD_DIM] - _dot(sol[h][:, DN_HEAD_DIM:], S[h]) for h in H]
    o = [_dot(q[h] * jnp.exp(gcol[h]), S[h]) + _dot(qk[h], v_new[h]) for h in H]
    for h in H:
        s_scr[h] = S[h] * jnp.exp(glast[h]) + lax.dot_general(
            k_tail[h].astype(BF16), v_new[h].astype(BF16), TN_DIMS, preferred_element_type=F32)
    for h in H:
        zf = z_ref[0, :, hsl[h]]
        odn_scr[:, hsl[h]] = (o[h] * lax.rsqrt(jnp.mean(o[h] * o[h], axis=-1, keepdims=True) + EPS)
                              * dng_ref[...] * _silu(zf))

    pp_scr[POOL_PAD:POOL_PAD + C, :] = pin_ref[0]
    pos = start + n * C + lax.broadcasted_iota(I32, (C, 1), 0)
    for gi, win in enumerate(POOL_WINDOWS):
        gs = slice(gi * POOL_GROUP_DIM, (gi + 1) * POOL_GROUP_DIM)
        xg = pp_scr[POOL_PAD:POOL_PAD + C, gs]
        ssum = xg
        for sft in range(1, win):
            ssum = ssum + pp_scr[POOL_PAD - sft:POOL_PAD - sft + C, gs]
        cnt = jnp.minimum(pos + 1, win).astype(F32)
        pooled = ssum / cnt - xg
        os_ = slice(gi * POOL_OUT_GROUP, (gi + 1) * POOL_OUT_GROUP)
        yp = _dot(pooled, wpool_ref[gi]) * pscale_ref[:, os_]
        mixed_ref[0, :, os_] = (jax.nn.sigmoid(ga_ref[0, :, os_]) * odn_scr[:, os_]
                                + jax.nn.sigmoid(gb_ref[0, :, os_]) * yp)

    @pl.when(n == last)
    def _store_state():
        nconv_ref[0] = xp_scr[Lv + CONV_PAD - (CONV_WIDTH - 1):Lv + CONV_PAD, :]
        npool_ref[0] = pp_scr[Lv + POOL_PAD - POOL_BUF:Lv + POOL_PAD, :]
        ns_ref[0] = s_scr[...]

    xp_scr[0:CONV_PAD, :] = xp_scr[C:C + CONV_PAD, :]
    pp_scr[0:POOL_PAD, :] = pp_scr[C:C + POOL_PAD, :]


def _mixer(proj, conv_buf, s0, pool_buf, start, seq_len, C,
           conv_w, a_log, dt_bias, dn_norm_g, w_pool, pool_scale):
    b, lp, _ = proj["qkv"].shape
    nchunks = lp // C
    lv = seq_len - (nchunks - 1) * C
    cbuf = jnp.pad(conv_buf, ((0, 0), (CONV_PAD - (CONV_WIDTH - 1), 0), (0, 0)))
    pbuf = jnp.pad(pool_buf, ((0, 0), (POOL_PAD - POOL_BUF, 0), (0, 0)))
    lane_pad = lambda a: jnp.pad(a.reshape(1, -1), ((0, 0), (DN_HEADS, LANES - 2 * DN_HEADS)))
    chunk = lambda w: pl.BlockSpec((1, C, w), lambda i, j: (i, j, 0))
    state = lambda *s: pl.BlockSpec((1,) + s, lambda i, j: (i,) + (0,) * len(s))
    return pl.pallas_call(
        functools.partial(_mixer_body, C, lv, start),
        grid=(b, nchunks),
        in_specs=[chunk(QKV_WIDTH), chunk(LANES), chunk(DN_WIDTH), chunk(POOL_WIDTH),
                  chunk(D_MODEL), chunk(D_MODEL),
                  state(CONV_PAD, QKV_WIDTH), state(DN_HEADS, DN_HEAD_DIM, DN_HEAD_DIM),
                  state(POOL_PAD, POOL_WIDTH),
                  _const_spec((CONV_WIDTH, QKV_WIDTH)), _const_spec((1, LANES)), _const_spec((1, LANES)),
                  _const_spec((1, DN_HEAD_DIM)),
                  _const_spec((len(POOL_WINDOWS), POOL_GROUP_DIM, POOL_OUT_GROUP)),
                  _const_spec((1, D_MODEL))],
        out_specs=[chunk(D_MODEL), state(CONV_WIDTH - 1, QKV_WIDTH),
                   state(DN_HEADS, DN_HEAD_DIM, DN_HEAD_DIM), state(POOL_BUF, POOL_WIDTH)],
        out_shape=[jax.ShapeDtypeStruct((b, lp, D_MODEL), F32),
                   jax.ShapeDtypeStruct((b, CONV_WIDTH - 1, QKV_WIDTH), F32),
                   jax.ShapeDtypeStruct((b, DN_HEADS, DN_HEAD_DIM, DN_HEAD_DIM), F32),
                   jax.ShapeDtypeStruct((b, POOL_BUF, POOL_WIDTH), F32)],
        scratch_shapes=[pltpu.VMEM((CONV_PAD + C + CONV_PAD, QKV_WIDTH), F32),
                        pltpu.VMEM((C, QKV_WIDTH), F32),
                        pltpu.VMEM((DN_HEADS, DN_HEAD_DIM, DN_HEAD_DIM), F32),
                        pltpu.VMEM((POOL_PAD + C + POOL_PAD, POOL_WIDTH), F32),
                        pltpu.VMEM((C, DN_WIDTH), F32)],
        compiler_params=pltpu.CompilerParams(dimension_semantics=("arbitrary", "arbitrary"),
                                             vmem_limit_bytes=VMEM_LIMIT),
        name="mixer",
    )(proj["qkv"], proj["ba"], proj["z"], proj["pool"], proj["ga"], proj["gb"], cbuf, s0, pbuf,
      conv_w, lane_pad(a_log), lane_pad(dt_bias), dn_norm_g.reshape(1, -1), w_pool,
      pool_scale.reshape(1, -1))


def _top16(s, ids, payload=None):
    big = float(2 ** 24)
    vals, sel, pays = [], [], []
    for _ in range(PEER_TOPK):
        m = jnp.max(s, axis=0, keepdims=True)
        am = jnp.min(jnp.where(s == m, ids, big), axis=0, keepdims=True)
        hit = ids == am
        if payload is not None:
            pays.append(jnp.max(jnp.where(hit, payload, -1.0), axis=0, keepdims=True))
        s = jnp.where(hit, -jnp.inf, s)
        vals.append(m)
        sel.append(am)
    out = (jnp.concatenate(vals, axis=0), jnp.concatenate(sel, axis=0))
    if payload is not None:
        out += (jnp.concatenate(pays, axis=0),)
    return out


_CAND_EDGE = 4


def _post_body(has_prev, has_fin, mixed_ref, x_ref, g1_ref, sc2_ref, sh2_ref, n2g_ref, wout_ref,
               wq_ref, keys_ref, *refs):
    refs = list(refs)
    prev_in = [refs.pop(0) for _ in range(2 if has_prev else 0)]
    fin_in = [refs.pop(0) for _ in range(4 if has_fin else 0)]
    x1_ref, h2_ref, idx_ref, gate_ref = refs[:4]
    extra_out = refs[4:]
    if has_prev:
        pre_ref, pgate_ref = prev_in
        extra_out.pop(0)[...] = _coef_words(pre_ref[...], pgate_ref[...])
    if has_fin:
        _final_body(*fin_in, extra_out.pop(0))
    tm = x_ref.shape[0]
    x1 = x_ref[...] + _mod_rows(g1_ref) * _dot(mixed_ref[...], wout_ref[...])
    x1_ref[...] = x1
    y = x1 * lax.rsqrt(jnp.mean(x1 * x1, axis=-1, keepdims=True) + EPS) * n2g_ref[...]
    h2 = y * (1.0 + _mod_rows(sc2_ref)) + _mod_rows(sh2_ref)
    h2_ref[...] = _pack_words(h2[:, :PACK_HALF], h2[:, PACK_HALF:])
    q = _dot(h2, wq_ref[...])

    K = PEER_TOPK
    key_id = lax.broadcasted_iota(I32, (PEER_NKEYS, 1), 0).astype(F32)
    r16 = lax.broadcasted_iota(I32, (K, 1), 0)
    cand_id = jnp.concatenate([(a * K + r16) for a in range(_CAND_EDGE)]
                              + [(r16 * K + b) for b in range(_CAND_EDGE)], axis=0).astype(F32)
    dup = r16 < _CAND_EDGE
    idx_rows, gate_rows = [], []
    for h in range(PEER_HEADS):
        half = []
        for p in range(2):
            c0 = (h * 2 + p) * PEER_KEY_HALF
            st = _dot_nt(keys_ref[h * 2 + p], q[:, c0:c0 + PEER_KEY_HALF])
            half.append(_top16(st, key_id))
        (s1, i1), (s2, i2) = half
        cand = jnp.concatenate(
            [s1[a:a + 1] + s2 for a in range(_CAND_EDGE)]
            + [jnp.where(dup, -jnp.inf, s1 + s2[b:b + 1]) for b in range(_CAND_EDGE)], axis=0)
        cidx = jnp.concatenate(
            [i1[a:a + 1] * PEER_NKEYS + i2 for a in range(_CAND_EDGE)]
            + [i1 * PEER_NKEYS + i2[b:b + 1] for b in range(_CAND_EDGE)], axis=0)
        best, _, eidx = _top16(cand, cand_id, cidx)
        e = jnp.exp(best - best[0:1])
        gate_rows.append(e / jnp.sum(e, axis=0, keepdims=True))
        idx_rows.append(eidx)
    idx_ref[...] = jnp.concatenate(idx_rows, axis=0).T.astype(I32)
    gate_ref[...] = jnp.concatenate(gate_rows, axis=0).T


def _post(mixed2d, x2d, mod, rows_per_batch, norm2_g, w_out, w_query, keys, tm, prev=None, fin=None):
    t = x2d.shape[0]
    steps = t // tm
    row = lambda w: pl.BlockSpec((tm, w), lambda i: (i, 0))
    in_specs = [row(D_MODEL), row(D_MODEL),
                _mod_spec(2, rows_per_batch, tm), _mod_spec(4, rows_per_batch, tm),
                _mod_spec(3, rows_per_batch, tm), _const_spec((1, D_MODEL)),
                _const_spec((D_MODEL, D_MODEL)), _const_spec((D_MODEL, 2 * PEER_HEADS * PEER_KEY_HALF)),
                _const_spec((2 * PEER_HEADS, PEER_NKEYS, PEER_KEY_HALF))]
    out_specs = [row(D_MODEL), row(PACK_HALF), row(PEER_HK), row(PEER_HK)]
    out_shape = [jax.ShapeDtypeStruct((t, D_MODEL), F32), jax.ShapeDtypeStruct((t, PACK_HALF), I32),
                 jax.ShapeDtypeStruct((t, PEER_HK), I32), jax.ShapeDtypeStruct((t, PEER_HK), F32)]
    args = [mixed2d, x2d, mod, mod, mod, norm2_g.reshape(1, -1), w_out, w_query, keys]
    if prev is not None:
        tp = prev[0].shape[0]
        prow = pl.BlockSpec((tp // steps, PEER_HK), lambda i: (i, 0))
        in_specs += [prow, prow]
        out_specs += [prow]
        out_shape += [jax.ShapeDtypeStruct((tp, PEER_HK), I32)]
        args += list(prev)
    if fin is not None:
        x1_f, peer_f, mod_f, rows_f, final_g = fin
        tf = x1_f.shape[0]
        frow = pl.BlockSpec((tf // steps, D_MODEL), lambda i: (i, 0))
        in_specs += [frow, frow, _mod_spec(5, rows_f, tf // steps), _const_spec((1, D_MODEL))]
        out_specs += [frow]
        out_shape += [jax.ShapeDtypeStruct((tf, D_MODEL), F32)]
        args += [x1_f, peer_f, mod_f, final_g.reshape(1, -1)]
    return pl.pallas_call(
        functools.partial(_post_body, prev is not None, fin is not None),
        grid=(steps,),
        in_specs=in_specs, out_specs=out_specs, out_shape=out_shape,
        compiler_params=pltpu.CompilerParams(vmem_limit_bytes=VMEM_LIMIT),
        name="post",
    )(*args)


SC_CORES = 2
SC_SUBCORES = 16
SC_LANES = 16
SC_WORKERS = SC_CORES * SC_SUBCORES
SC_TOKENS = 16
SC_SLOTS = 6
SC_JOB_HEADS = 2
SC_BF16_GROUP = 4
PACK_HALF = D_MODEL // 2
SC_CHUNKS = PACK_HALF // SC_LANES
PROMPT_PARTS = 8
SEQ_SPLITS = 1
FIN_LAG = 3
ROW_TILE = 256


def _bf16_bits(v):
    return lax.bitcast_convert_type(v.astype(BF16).astype(F32), jnp.uint32)


def _pack_words(lo, hi):
    return lax.bitcast_convert_type((_bf16_bits(lo) >> 16) | _bf16_bits(hi), I32)


def _pack_body(x_ref, o_ref):
    o_ref[...] = _pack_words(x_ref[:, :PACK_HALF], x_ref[:, PACK_HALF:])


def _pack_table(tbl, rows=512):
    e = tbl.shape[0]
    return pl.pallas_call(
        _pack_body, grid=(e // rows,),
        in_specs=[pl.BlockSpec((rows, D_MODEL), lambda i: (i, 0))],
        out_specs=pl.BlockSpec((rows, PACK_HALF), lambda i: (i, 0)),
        out_shape=jax.ShapeDtypeStruct((e, PACK_HALF), I32), name="pack_table")(tbl)


def _tree_sum(terms):
    terms = list(terms)
    while len(terms) > 1:
        terms = [a + b for a, b in zip(terms[0::2], terms[1::2])] + terms[len(terms) & ~1:]
    return terms[0]


def _unpack_pair(w):
    lo = plsc.bitcast(lax.shift_left(w, jnp.full(w.shape, 16, I32)), F32)
    hi = plsc.bitcast(w & jnp.full(w.shape, -65536, I32), F32)
    return lo, hi


def _sc_mesh():
    return plsc.VectorSubcoreMesh(core_axis_name="c", subcore_axis_name="s")


def _sc_worker():
    return lax.axis_index("s") * SC_CORES + lax.axis_index("c")


def _sc_jobs(table_hbm, idx_v, buf, sem, compute):
    per_tok = PEER_HEADS // SC_JOB_HEADS
    njobs = idx_v.shape[0] * per_tok
    nrows = SC_JOB_HEADS * PEER_TOPK

    def copy(j, slot):
        rows = idx_v.at[j // per_tok, pl.ds((j % per_tok) * nrows, nrows)]
        return pltpu.make_async_copy(table_hbm.at[rows], buf.at[slot], sem.at[slot])

    for s in range(SC_SLOTS):
        copy(s, s).start()

    def job(j, c):
        s = j % SC_SLOTS
        copy(j, s).wait()

        def head(i, cc):
            compute(j // per_tok, (j % per_tok) * SC_JOB_HEADS + i, s, i * PEER_TOPK)
            return cc
        lax.fori_loop(0, SC_JOB_HEADS, head, 0)

        @pl.when(j + SC_SLOTS < njobs)
        def _next():
            copy(j + SC_SLOTS, s).start()
        return c

    lax.fori_loop(0, njobs, job, 0)


def _peer_u_body(n_tok, idx_hbm, h2_hbm, u_hbm, pre_hbm, idx_v, h2_v, pre_v, ubuf, acc_v, sem):
    base = _sc_worker() * n_tok
    lane = lax.iota(I32, SC_LANES)

    def compute(tt, h, slot, r0):
        def chunk(cg, accs):
            cs = [pl.ds((cg * SC_BF16_GROUP + i) * SC_LANES, SC_LANES) for i in range(SC_BF16_GROUP)]
            xs = [plsc.bitcast(h2_v[tt, c], BF16) for c in cs]
            out = []
            for k, a in enumerate(accs):
                part = _tree_sum([plsc.bitcast(ubuf[slot, r0 + k, c], BF16) * x for c, x in zip(cs, xs)])
                lo, hi = _unpack_pair(plsc.bitcast(part, I32))
                out.append(a + (lo + hi))
            return tuple(out)
        zero = jnp.zeros((SC_LANES,), F32)
        accs = lax.fori_loop(0, SC_CHUNKS // SC_BF16_GROUP, chunk, (zero,) * PEER_TOPK)
        for k, a in enumerate(accs):
            acc_v[k, :] = a
        tot = zero
        for j in range(SC_LANES):
            tot = tot + plsc.load_gather(acc_v, [lane, (lane + j) & (SC_LANES - 1)])
        pre_v[tt, pl.ds(h * PEER_TOPK, PEER_TOPK)] = tot

    tb = idx_v.shape[0]

    def block(bi, c):
        t0 = base + bi * tb
        pltpu.sync_copy(idx_hbm.at[pl.ds(t0, tb)], idx_v)
        pltpu.sync_copy(h2_hbm.at[pl.ds(t0, tb)], h2_v)
        _sc_jobs(u_hbm, idx_v, ubuf, sem, compute)
        pltpu.sync_copy(pre_v, pre_hbm.at[pl.ds(t0, tb)])
        return c

    lax.fori_loop(0, n_tok // tb, block, 0)


def _peer_v_body(n_tok, idx_hbm, coef_hbm, v_hbm, out_hbm, idx_v, coef_v, out_v, vbuf, sem):
    base = _sc_worker() * n_tok
    zero = jnp.zeros((SC_LANES,), F32)

    def compute(tt, h, slot, r0):
        row = jnp.full((SC_LANES,), tt, I32)
        cb = [plsc.bitcast(plsc.load_gather(
                  coef_v, [row, jnp.full((SC_LANES,), h * PEER_TOPK + k, I32)]), BF16)
              for k in range(PEER_TOPK)]

        @plsc.parallel_loop(0, SC_CHUNKS, unroll=2)
        def _chunk(c):
            cs = pl.ds(c * SC_LANES, SC_LANES)
            prods = [plsc.bitcast(vbuf[slot, r0 + k, cs], BF16) * cb[k] for k in range(PEER_TOPK)]
            pairs = [_unpack_pair(plsc.bitcast(_tree_sum(prods[g:g + SC_BF16_GROUP]), I32))
                     for g in range(0, PEER_TOPK, SC_BF16_GROUP)]
            for half, off in ((0, 0), (1, PACK_HALF)):
                plsc.addupdate(out_v.at[tt, pl.ds(off + c * SC_LANES, SC_LANES)],
                               _tree_sum([p[half] for p in pairs]))

    tb = idx_v.shape[0]

    def block(bi, c):
        t0 = base + bi * tb
        pltpu.sync_copy(idx_hbm.at[pl.ds(t0, tb)], idx_v)
        pltpu.sync_copy(coef_hbm.at[pl.ds(t0, tb)], coef_v)

        def clear(i, cc):
            per_row = D_MODEL // SC_LANES
            out_v[i // per_row, pl.ds((i % per_row) * SC_LANES, SC_LANES)] = zero
            return cc
        lax.fori_loop(0, tb * (D_MODEL // SC_LANES), clear, 0)
        _sc_jobs(v_hbm, idx_v, vbuf, sem, compute)
        pltpu.sync_copy(out_v, out_hbm.at[pl.ds(t0, tb)])
        return c

    lax.fori_loop(0, n_tok // tb, block, 0)


def _peer_sc(body, idx, rows, table, out_width, name):
    t = idx.shape[0]
    assert t % SC_WORKERS == 0
    n_tok = t // SC_WORKERS
    tb = min(SC_TOKENS, n_tok)
    assert n_tok % tb == 0 and tb * PEER_HEADS // SC_JOB_HEADS >= SC_SLOTS
    return pl.kernel(
        functools.partial(body, n_tok),
        out_type=jax.ShapeDtypeStruct((t, out_width), F32),
        mesh=_sc_mesh(),
        scratch_types=[pltpu.VMEM((tb, PEER_HK), I32),
                       pltpu.VMEM((tb, rows.shape[1]), rows.dtype),
                       pltpu.VMEM((tb, out_width), F32),
                       pltpu.VMEM((SC_SLOTS, SC_JOB_HEADS * PEER_TOPK, PACK_HALF), I32)]
                      + ([pltpu.VMEM((PEER_TOPK, SC_LANES), F32)] if body is _peer_u_body else [])
                      + [pltpu.SemaphoreType.DMA((SC_SLOTS,))],
        compiler_params=pltpu.CompilerParams(needs_layout_passes=False),
        name=name,
    )(idx, rows, table)


def _coef_words(pre, gates):
    return _pack_words(*(gates * _gelu(pre),) * 2)


def _coef_body(pre_ref, gate_ref, coef_ref):
    coef_ref[...] = _coef_words(pre_ref[...], gate_ref[...])


def _coef(pre, gates, tm):
    t = pre.shape[0]
    row = pl.BlockSpec((tm, PEER_HK), lambda i: (i, 0))
    return pl.pallas_call(_coef_body, grid=(t // tm,), in_specs=[row, row], out_specs=row,
                          out_shape=jax.ShapeDtypeStruct((t, PEER_HK), I32), name="coef")(pre, gates)


def _final_body(x1_ref, peer_ref, g2_ref, fng_ref, y_ref):
    x2 = x1_ref[...] + _mod_rows(g2_ref) * peer_ref[...]
    y_ref[...] = x2 * lax.rsqrt(jnp.mean(x2 * x2, axis=-1, keepdims=True) + EPS) * fng_ref[...]


def _final(x1, peer_out, mod, rows_per_batch, final_g, tm):
    t = x1.shape[0]
    row = pl.BlockSpec((tm, D_MODEL), lambda i: (i, 0))
    return pl.pallas_call(
        _final_body, grid=(t // tm,),
        in_specs=[row, row, _mod_spec(5, rows_per_batch, tm), _const_spec((1, D_MODEL))],
        out_specs=row, out_shape=jax.ShapeDtypeStruct((t, D_MODEL), F32), name="final",
    )(x1, peer_out, mod, final_g.reshape(1, -1))


def _expert_gather_v(g, coef, expert_v):
    g["peer_out"] = _peer_sc(_peer_v_body, g["idx"], coef, expert_v, D_MODEL, "peer_v")


def _front(x, mod, conv_buf, s0, pool_buf, start, chunk, tm, wts, prev, fin):
    b, l, _ = x.shape
    t = b * l
    x2d = x.reshape(t, D_MODEL)
    if l >= tm:
        modx = mod.reshape(b, 6, 1, D_MODEL).transpose(1, 0, 2, 3)
    else:
        modx = jnp.repeat(mod.reshape(b, 6, D_MODEL), l, axis=0).transpose(1, 0, 2)
    outs = _inproj(x2d, modx, l, wts["norm1_g"], wts["w_cat"], tm)
    lp = -(-l // chunk) * chunk
    proj = {}
    for (name, w), a in zip(_IN_BLOCKS, outs):
        a = a.reshape(b, l, w)
        proj[name] = a if lp == l else jnp.pad(a, ((0, 0), (0, lp - l), (0, 0)))
    mixed, nconv, ns, npool = _mixer(proj, conv_buf, s0, pool_buf, start, l, chunk,
                                     wts["conv_w"], wts["a_log"], wts["dt_bias"], wts["dn_norm_g"],
                                     wts["w_pool"], wts["pool_scale"])
    mixed2d = mixed[:, :l].reshape(t, D_MODEL)
    res = _post(mixed2d, x2d, modx, l, wts["norm2_g"], wts["w_out"], wts["w_query"], wts["keys"], tm,
                prev=None if prev is None else (prev["pre"], prev["gates"]),
                fin=None if fin is None else (fin["x1"], fin["peer_out"], fin["mod"], fin["l"],
                                              wts["final_norm_g"]))
    x1, h2, idx, gates = res[:4]
    extra = list(res[4:])
    coef_prev = extra.pop(0) if prev is not None else None
    y_fin = extra.pop(0).reshape(fin["b"], fin["l"], D_MODEL) if fin is not None else None
    pre = _peer_sc(_peer_u_body, idx, h2, wts["expert_u"], PEER_HK, "peer_u")
    g = dict(x1=x1, idx=idx, gates=gates, pre=pre, mod=modx, b=b, l=l, tm=tm,
             states=(nconv, ns, npool))
    return g, coef_prev, y_fin


def kernel(x_prompt, x_sample, c_prompt, c_sample, state_conv, state_delta, state_pool, w_ada, b_ada, norm1_g, w_in, conv_w, a_log, dt_bias, dn_norm_g, w_pool, pool_scale, w_out, norm2_g, w_query, sub_keys, expert_u, expert_v, final_norm_g):
    bp = x_prompt.shape[0]
    bs = x_sample.shape[0]
    yp, ys = x_prompt, x_sample
    conv_p, delta_p, pool_p, conv_s, delta_s, pool_s = [], [], [], [], [], []
    zero_conv = jnp.zeros((bp, CONV_WIDTH - 1, QKV_WIDTH), F32)
    zero_delta = jnp.zeros((bp, DN_HEADS, DN_HEAD_DIM, DN_HEAD_DIM), F32)
    zero_pool = jnp.zeros((bp, POOL_BUF, POOL_WIDTH), F32)
    c_all = jnp.concatenate([c_prompt, c_sample], axis=0)
    for layer in range(DEPTH):
        wi = w_in[layer]
        o_b = QKV_WIDTH
        o_z = o_b + 2 * DN_HEADS
        w_ba = jnp.pad(wi[:, o_b:o_z], ((0, 0), (0, LANES - 2 * DN_HEADS)))
        w_cat = jnp.concatenate([wi[:, :o_b], wi[:, o_z:], w_ba], axis=1).astype(BF16)
        last = layer == DEPTH - 1
        wts = dict(
            norm1_g=norm1_g[layer], w_cat=w_cat, conv_w=conv_w[layer], a_log=a_log[layer],
            dt_bias=dt_bias[layer], dn_norm_g=dn_norm_g[layer], w_pool=w_pool[layer],
            pool_scale=pool_scale[layer], w_out=w_out[layer].astype(BF16), norm2_g=norm2_g[layer],
            w_query=w_query[layer].astype(BF16),
            keys=sub_keys[layer].reshape(2 * PEER_HEADS, PEER_NKEYS, PEER_KEY_HALF).astype(BF16),
            expert_u=_pack_table(expert_u[layer]), expert_v=_pack_table(expert_v[layer]),
            final_norm_g=final_norm_g if last else jnp.ones_like(final_norm_g))
        mod = _ada(c_all, w_ada[layer], b_ada[layer])
        assert last, "final norm is fused into the expert stage"
        step = bp // PROMPT_PARTS
        ls = x_prompt.shape[1] // SEQ_SPLITS
        zeros = (zero_conv[:step], zero_delta[:step], zero_pool[:step])
        jobs = [(yp[b0:b0 + step, s0:s0 + ls], mod[b0:b0 + step], zeros if s0 == 0 else None, s0, DN_CHUNK)
                for b0 in range(0, bp, step) for s0 in range(0, SEQ_SPLITS * ls, ls)]
        jobs.append((ys, mod[bp:], (state_conv[layer], state_delta[layer], state_pool[layer]),
                     PAST_LEN, SUBLANES))
        groups = []
        for j, (xg, mg, states, start, chunk) in enumerate(jobs):
            prev = groups[j - 1] if j >= 1 else None
            fin = groups[j - FIN_LAG] if j >= FIN_LAG else None
            if fin is not None and fin["x1"].shape[0] != xg.shape[0] * xg.shape[1]:
                fin = None
            if states is None:
                states = prev["states"]
            g, coef_prev, y_fin = _front(xg, mg, *states, start, chunk, ROW_TILE, wts, prev, fin)
            if prev is not None:
                _expert_gather_v(prev, coef_prev, wts["expert_v"])
            if fin is not None:
                fin["y"] = y_fin
            groups.append(g)
        _expert_gather_v(groups[-1], _coef(groups[-1]["pre"], groups[-1]["gates"], ROW_TILE),
                         wts["expert_v"])
        for g in groups:
            if "y" not in g:
                g["y"] = _final(g["x1"], g["peer_out"], g["mod"], g["l"], wts["final_norm_g"],
                                g["tm"]).reshape(g["b"], g["l"], D_MODEL)
        rows = [groups[i:i + SEQ_SPLITS] for i in range(0, len(groups) - 1, SEQ_SPLITS)]
        yp = jnp.concatenate([jnp.concatenate([g["y"] for g in row], axis=1) for row in rows], axis=0)
        cp, sp, pp = (jnp.concatenate(a, axis=0) for a in zip(*(row[-1]["states"] for row in rows)))
        ys = groups[-1]["y"]
        cs, ss, ps = groups[-1]["states"]
        conv_p.append(cp)
        delta_p.append(sp)
        pool_p.append(pp)
        conv_s.append(cs)
        delta_s.append(ss)
        pool_s.append(ps)
    return (yp, ys, jnp.stack(conv_p), jnp.stack(delta_p), jnp.stack(pool_p),
            jnp.stack(conv_s), jnp.stack(delta_s), jnp.stack(pool_s))
```

```python
import functools

import jax
import jax.numpy as jnp
from jax import lax
from jax.experimental import pallas as pl
from jax.experimental.pallas import tpu as pltpu
from jax.experimental.pallas import tpu_sc as plsc

F32 = jnp.float32
BF16 = jnp.bfloat16
I32 = jnp.int32

D_MODEL = 1024
DEPTH = 1
PAST_LEN = 16384
DN_HEADS = 8
DN_HEAD_DIM = 128
DN_WIDTH = DN_HEADS * DN_HEAD_DIM
QKV_WIDTH = 3 * DN_WIDTH
CONV_WIDTH = 4
DN_CHUNK = 64
POOL_WINDOWS = (2, 4, 8, 16)
POOL_GROUP_DIM = 128
POOL_WIDTH = len(POOL_WINDOWS) * POOL_GROUP_DIM
POOL_OUT_GROUP = D_MODEL // len(POOL_WINDOWS)
POOL_BUF = max(POOL_WINDOWS) - 1
PEER_HEADS = 8
PEER_NKEYS = 128
PEER_TOPK = 16
PEER_KEY_HALF = 128
PEER_HK = PEER_HEADS * PEER_TOPK
EPS = 1e-6

LANES = 128
SUBLANES = 8
CONV_PAD = SUBLANES
POOL_PAD = 16
VMEM_LIMIT = 56 * 1024 * 1024

NT_DIMS = (((1,), (1,)), ((), ()))
TN_DIMS = (((0,), (0,)), ((), ()))


def _dot(a, b):
    return jnp.dot(a.astype(BF16), b.astype(BF16), preferred_element_type=F32)


def _dot_nt(a, b):
    return lax.dot_general(a.astype(BF16), b.astype(BF16), NT_DIMS, preferred_element_type=F32)


def _split3(x):
    hi = x.astype(BF16)
    r1 = x - hi.astype(F32)
    mid = r1.astype(BF16)
    lo = (r1 - mid.astype(F32)).astype(BF16)
    return hi, mid, lo


def _silu(x):
    return x * jax.nn.sigmoid(x)


def _gelu(x):
    return 0.5 * x * (1.0 + lax.erf(x * (0.5 ** 0.5)))


def _softplus(x):
    return jnp.maximum(x, 0.0) + jnp.log(1.0 + jnp.exp(-jnp.abs(x)))


def _mod_rows(ref):
    m = ref[...]
    return m.reshape(m.shape[-2], m.shape[-1])


def _mod_spec(k, rows_per_batch, tm):
    if rows_per_batch >= tm:
        tiles = rows_per_batch // tm
        return pl.BlockSpec((1, 1, 1, D_MODEL), lambda i, *_: (k, i // tiles, 0, 0))
    return pl.BlockSpec((1, tm, D_MODEL), lambda i, *_: (k, i, 0))


def _const_spec(shape):
    nd = len(shape)
    return pl.BlockSpec(shape, lambda *_: (0,) * nd)


def _ada_body(c_ref, w_ref, b_ref, o_ref):
    o_ref[...] = _dot(_silu(c_ref[...]), w_ref[...]) + b_ref[...]


def _ada(c, w_ada, b_ada):
    n = c.shape[0]
    return pl.pallas_call(
        _ada_body,
        grid=(6,),
        in_specs=[pl.BlockSpec((n, D_MODEL), lambda j: (0, 0)),
                  pl.BlockSpec((D_MODEL, D_MODEL), lambda j: (0, j)),
                  pl.BlockSpec((1, D_MODEL), lambda j: (0, j))],
        out_specs=pl.BlockSpec((n, D_MODEL), lambda j: (0, j)),
        out_shape=jax.ShapeDtypeStruct((n, 6 * D_MODEL), F32),
        name="ada",
    )(c, w_ada, b_ada.reshape(1, -1))


_IN_BLOCKS = (("qkv", QKV_WIDTH), ("z", DN_WIDTH), ("pool", POOL_WIDTH),
              ("ga", D_MODEL), ("gb", D_MODEL), ("ba", LANES))
_IN_TOTAL = sum(w for _, w in _IN_BLOCKS)
_IN_COL_CHUNK = 512


def _inproj_body(x_ref, sc_ref, sh_ref, g_ref, w_ref, *out_refs):
    x = x_ref[...]
    y = x * lax.rsqrt(jnp.mean(x * x, axis=-1, keepdims=True) + EPS) * g_ref[...]
    h = (y * (1.0 + _mod_rows(sc_ref)) + _mod_rows(sh_ref)).astype(BF16)
    off = 0
    for (_, width), o_ref in zip(_IN_BLOCKS, out_refs):
        for c0 in range(0, width, _IN_COL_CHUNK):
            cw = min(_IN_COL_CHUNK, width - c0)
            o_ref[:, c0:c0 + cw] = jnp.dot(h, w_ref[:, off + c0:off + c0 + cw],
                                           preferred_element_type=F32)
        off += width


def _inproj(x2d, mod, rows_per_batch, norm_g, w_cat, tm):
    t = x2d.shape[0]
    row = lambda w: pl.BlockSpec((tm, w), lambda i: (i, 0))
    return pl.pallas_call(
        _inproj_body,
        grid=(t // tm,),
        in_specs=[row(D_MODEL), _mod_spec(1, rows_per_batch, tm), _mod_spec(0, rows_per_batch, tm),
                  _const_spec((1, D_MODEL)),
                  pl.BlockSpec((D_MODEL, _IN_TOTAL), lambda i: (0, 0), pipeline_mode=pl.Buffered(1))],
        out_specs=[row(w) for _, w in _IN_BLOCKS],
        out_shape=[jax.ShapeDtypeStruct((t, w), F32) for _, w in _IN_BLOCKS],
        compiler_params=pltpu.CompilerParams(vmem_limit_bytes=VMEM_LIMIT),
        name="inproj",
    )(x2d, mod, mod, norm_g.reshape(1, -1), w_cat)


def _mixer_body(C, Lv, start,
                qkv_ref, ba_ref, z_ref, pin_ref, ga_ref, gb_ref, cbuf_ref, s0_ref, pbuf_ref,
                convw_ref, alog_ref, dtb_ref, dng_ref, wpool_ref, pscale_ref,
                mixed_ref, nconv_ref, ns_ref, npool_ref,
                xp_scr, act_scr, s_scr, pp_scr, odn_scr):
    n = pl.program_id(1)
    last = pl.num_programs(1) - 1

    @pl.when(n == 0)
    def _load_state():
        xp_scr[0:CONV_PAD, :] = cbuf_ref[0]
        pp_scr[0:POOL_PAD, :] = pbuf_ref[0]
        s_scr[...] = s0_ref[0]

    xp_scr[CONV_PAD:CONV_PAD + C, :] = qkv_ref[0]
    for c0 in range(0, QKV_WIDTH, 512):
        cs = slice(c0, c0 + 512)
        y = xp_scr[CONV_PAD:CONV_PAD + C, cs] * convw_ref[CONV_WIDTH - 1:CONV_WIDTH, cs]
        for k in range(CONV_WIDTH - 1):
            r0 = CONV_PAD - (CONV_WIDTH - 1) + k
            y = y + xp_scr[r0:r0 + C, cs] * convw_ref[k:k + 1, cs]
        act_scr[:, cs] = _silu(y)

    ba = ba_ref[0]
    lane = lax.broadcasted_iota(I32, (C, LANES), 1)
    beta_all = jax.nn.sigmoid(ba)
    g_all = -jnp.exp(alog_ref[...]) * _softplus(ba + dtb_ref[...])
    if Lv < C:
        valid = lax.broadcasted_iota(I32, (C, LANES), 0) < Lv
        beta_all = jnp.where(valid, beta_all, 0.0)
        g_all = jnp.where(valid, g_all, 0.0)
    ii = lax.broadcasted_iota(I32, (C, C), 0)
    jj = lax.broadcasted_iota(I32, (C, C), 1)
    causal = ii >= jj
    strict = ii > jj
    tril = jnp.where(causal, 1.0, 0.0).astype(BF16)
    eye = jnp.where(ii == jj, 1.0, 0.0)
    gc_all = sum(jnp.dot(tril, part, preferred_element_type=F32) for part in _split3(g_all))
    if C < LANES:
        gc_sq = jnp.concatenate([gc_all, jnp.zeros((LANES - C, LANES), F32)], axis=0)
    else:
        gc_sq = gc_all
    gc_t = gc_sq.T

    H = range(DN_HEADS)
    hsl = [slice(h * DN_HEAD_DIM, (h + 1) * DN_HEAD_DIM) for h in H]
    beta = [jnp.sum(jnp.where(lane == h, beta_all, 0.0), axis=1, keepdims=True) for h in H]
    gcol = [jnp.sum(jnp.where(lane == DN_HEADS + h, gc_all, 0.0), axis=1, keepdims=True) for h in H]
    grow = [gc_t[DN_HEADS + h:DN_HEADS + h + 1, 0:C] for h in H]
    glast = [g[C - 1:C, :] for g in gcol]
    q = [act_scr[:, hsl[h]] for h in H]
    k = [act_scr[:, DN_WIDTH + h * DN_HEAD_DIM:DN_WIDTH + (h + 1) * DN_HEAD_DIM] for h in H]
    v = [act_scr[:, 2 * DN_WIDTH + h * DN_HEAD_DIM:2 * DN_WIDTH + (h + 1) * DN_HEAD_DIM] for h in H]
    q = [x * lax.rsqrt(jnp.sum(x * x, axis=-1, keepdims=True) + EPS) * (DN_HEAD_DIM ** -0.5) for x in q]
    k = [x * lax.rsqrt(jnp.sum(x * x, axis=-1, keepdims=True) + EPS) for x in k]
    kb = [k[h] * beta[h] for h in H]
    vb = [v[h] * beta[h] for h in H]
    decay = [jnp.where(causal, jnp.exp(jnp.where(causal, gcol[h] - grow[h], 0.0)), 0.0) for h in H]
    lower = [jnp.where(strict, _dot_nt(kb[h], k[h]) * decay[h], 0.0) for h in H]
    ainv = [eye - x for x in lower]
    pw = lower
    p = 1
    while 2 * p < C:
        pw = [_dot(x, x) for x in pw]
        ainv = [ainv[h] + _dot(ainv[h], pw[h]) for h in H]
        p *= 2
    sol = [_dot(ainv[h], jnp.concatenate([vb[h], kb[h] * jnp.exp(gcol[h])], axis=1)) for h in H]
    qk = [_dot_nt(q[h], k[h]) * decay[h] for h in H]
    k_tail = [k[h] * jnp.exp(glast[h] - gcol[h]) for h in H]
    S = [s_scr[h] for h in H]
    v_new = [sol[h][:, :DN_HEAD_DIM] - _dot(sol[h][:, DN_HEAD_DIM:], S[h]) for h in H]
    o = [_dot(q[h] * jnp.exp(gcol[h]), S[h]) + _dot(qk[h], v_new[h]) for h in H]
    for h in H:
        s_scr[h] = S[h] * jnp.exp(glast[h]) + lax.dot_general(
            k_tail[h].astype(BF16), v_new[h].astype(BF16), TN_DIMS, preferred_element_type=F32)
    for h in H:
        zf = z_ref[0, :, hsl[h]]
        odn_scr[:, hsl[h]] = (o[h] * lax.rsqrt(jnp.mean(o[h] * o[h], axis=-1, keepdims=True) + EPS)
                              * dng_ref[...] * _silu(zf))

    pp_scr[POOL_PAD:POOL_PAD + C, :] = pin_ref[0]
    pos = start + n * C + lax.broadcasted_iota(I32, (C, 1), 0)
    for gi, win in enumerate(POOL_WINDOWS):
        gs = slice(gi * POOL_GROUP_DIM, (gi + 1) * POOL_GROUP_DIM)
        xg = pp_scr[POOL_PAD:POOL_PAD + C, gs]
        ssum = xg
        for sft in range(1, win):
            ssum = ssum + pp_scr[POOL_PAD - sft:POOL_PAD - sft + C, gs]
        cnt = jnp.minimum(pos + 1, win).astype(F32)
        pooled = ssum / cnt - xg
        os_ = slice(gi * POOL_OUT_GROUP, (gi + 1) * POOL_OUT_GROUP)
        yp = _dot(pooled, wpool_ref[gi]) * pscale_ref[:, os_]
        mixed_ref[0, :, os_] = (jax.nn.sigmoid(ga_ref[0, :, os_]) * odn_scr[:, os_]
                                + jax.nn.sigmoid(gb_ref[0, :, os_]) * yp)

    @pl.when(n == last)
    def _store_state():
        nconv_ref[0] = xp_scr[Lv + CONV_PAD - (CONV_WIDTH - 1):Lv + CONV_PAD, :]
        npool_ref[0] = pp_scr[Lv + POOL_PAD - POOL_BUF:Lv + POOL_PAD, :]
        ns_ref[0] = s_scr[...]

    xp_scr[0:CONV_PAD, :] = xp_scr[C:C + CONV_PAD, :]
    pp_scr[0:POOL_PAD, :] = pp_scr[C:C + POOL_PAD, :]


def _mixer(proj, conv_buf, s0, pool_buf, start, seq_len, C,
           conv_w, a_log, dt_bias, dn_norm_g, w_pool, pool_scale):
    b, lp, _ = proj["qkv"].shape
    nchunks = lp // C
    lv = seq_len - (nchunks - 1) * C
    cbuf = jnp.pad(conv_buf, ((0, 0), (CONV_PAD - (CONV_WIDTH - 1), 0), (0, 0)))
    pbuf = jnp.pad(pool_buf, ((0, 0), (POOL_PAD - POOL_BUF, 0), (0, 0)))
    lane_pad = lambda a: jnp.pad(a.reshape(1, -1), ((0, 0), (DN_HEADS, LANES - 2 * DN_HEADS)))
    chunk = lambda w: pl.BlockSpec((1, C, w), lambda i, j: (i, j, 0))
    state = lambda *s: pl.BlockSpec((1,) + s, lambda i, j: (i,) + (0,) * len(s))
    return pl.pallas_call(
        functools.partial(_mixer_body, C, lv, start),
        grid=(b, nchunks),
        in_specs=[chunk(QKV_WIDTH), chunk(LANES), chunk(DN_WIDTH), chunk(POOL_WIDTH),
                  chunk(D_MODEL), chunk(D_MODEL),
                  state(CONV_PAD, QKV_WIDTH), state(DN_HEADS, DN_HEAD_DIM, DN_HEAD_DIM),
                  state(POOL_PAD, POOL_WIDTH),
                  _const_spec((CONV_WIDTH, QKV_WIDTH)), _const_spec((1, LANES)), _const_spec((1, LANES)),
                  _const_spec((1, DN_HEAD_DIM)),
                  _const_spec((len(POOL_WINDOWS), POOL_GROUP_DIM, POOL_OUT_GROUP)),
                  _const_spec((1, D_MODEL))],
        out_specs=[chunk(D_MODEL), state(CONV_WIDTH - 1, QKV_WIDTH),
                   state(DN_HEADS, DN_HEAD_DIM, DN_HEAD_DIM), state(POOL_BUF, POOL_WIDTH)],
        out_shape=[jax.ShapeDtypeStruct((b, lp, D_MODEL), F32),
                   jax.ShapeDtypeStruct((b, CONV_WIDTH - 1, QKV_WIDTH), F32),
                   jax.ShapeDtypeStruct((b, DN_HEADS, DN_HEAD_DIM, DN_HEAD_DIM), F32),
                   jax.ShapeDtypeStruct((b, POOL_BUF, POOL_WIDTH), F32)],
        scratch_shapes=[pltpu.VMEM((CONV_PAD + C + CONV_PAD, QKV_WIDTH), F32),
                        pltpu.VMEM((C, QKV_WIDTH), F32),
                        pltpu.VMEM((DN_HEADS, DN_HEAD_DIM, DN_HEAD_DIM), F32),
                        pltpu.VMEM((POOL_PAD + C + POOL_PAD, POOL_WIDTH), F32),
                        pltpu.VMEM((C, DN_WIDTH), F32)],
        compiler_params=pltpu.CompilerParams(dimension_semantics=("arbitrary", "arbitrary"),
                                             vmem_limit_bytes=VMEM_LIMIT),
        name="mixer",
    )(proj["qkv"], proj["ba"], proj["z"], proj["pool"], proj["ga"], proj["gb"], cbuf, s0, pbuf,
      conv_w, lane_pad(a_log), lane_pad(dt_bias), dn_norm_g.reshape(1, -1), w_pool,
      pool_scale.reshape(1, -1))


def _top16(s, ids, payload=None):
    big = float(2 ** 24)
    vals, sel, pays = [], [], []
    for _ in range(PEER_TOPK):
        m = jnp.max(s, axis=0, keepdims=True)
        am = jnp.min(jnp.where(s == m, ids, big), axis=0, keepdims=True)
        hit = ids == am
        if payload is not None:
            pays.append(jnp.max(jnp.where(hit, payload, -1.0), axis=0, keepdims=True))
        s = jnp.where(hit, -jnp.inf, s)
        vals.append(m)
        sel.append(am)
    out = (jnp.concatenate(vals, axis=0), jnp.concatenate(sel, axis=0))
    if payload is not None:
        out += (jnp.concatenate(pays, axis=0),)
    return out


_CAND_EDGE = 4


def _post_body(has_prev, has_fin, mixed_ref, x_ref, g1_ref, sc2_ref, sh2_ref, n2g_ref, wout_ref,
               wq_ref, keys_ref, *refs):
    refs = list(refs)
    prev_in = [refs.pop(0) for _ in range(2 if has_prev else 0)]
    fin_in = [refs.pop(0) for _ in range(4 if has_fin else 0)]
    x1_ref, h2_ref, idx_ref, gate_ref = refs[:4]
    extra_out = refs[4:]
    if has_prev:
        pre_ref, pgate_ref = prev_in
        extra_out.pop(0)[...] = _coef_words(pre_ref[...], pgate_ref[...])
    if has_fin:
        _final_body(*fin_in, extra_out.pop(0))
    tm = x_ref.shape[0]
    x1 = x_ref[...] + _mod_rows(g1_ref) * _dot(mixed_ref[...], wout_ref[...])
    x1_ref[...] = x1
    y = x1 * lax.rsqrt(jnp.mean(x1 * x1, axis=-1, keepdims=True) + EPS) * n2g_ref[...]
    h2 = y * (1.0 + _mod_rows(sc2_ref)) + _mod_rows(sh2_ref)
    h2_ref[...] = _pack_words(h2[:, :PACK_HALF], h2[:, PACK_HALF:])
    q = _dot(h2, wq_ref[...])

    K = PEER_TOPK
    key_id = lax.broadcasted_iota(I32, (PEER_NKEYS, 1), 0).astype(F32)
    r16 = lax.broadcasted_iota(I32, (K, 1), 0)
    cand_id = jnp.concatenate([(a * K + r16) for a in range(_CAND_EDGE)]
                              + [(r16 * K + b) for b in range(_CAND_EDGE)], axis=0).astype(F32)
    dup = r16 < _CAND_EDGE
    idx_rows, gate_rows = [], []
    for h in range(PEER_HEADS):
        half = []
        for p in range(2):
            c0 = (h * 2 + p) * PEER_KEY_HALF
            st = _dot_nt(keys_ref[h * 2 + p], q[:, c0:c0 + PEER_KEY_HALF])
            half.append(_top16(st, key_id))
        (s1, i1), (s2, i2) = half
        cand = jnp.concatenate(
            [s1[a:a + 1] + s2 for a in range(_CAND_EDGE)]
            + [jnp.where(dup, -jnp.inf, s1 + s2[b:b + 1]) for b in range(_CAND_EDGE)], axis=0)
        cidx = jnp.concatenate(
            [i1[a:a + 1] * PEER_NKEYS + i2 for a in range(_CAND_EDGE)]
            + [i1 * PEER_NKEYS + i2[b:b + 1] for b in range(_CAND_EDGE)], axis=0)
        best, _, eidx = _top16(cand, cand_id, cidx)
        e = jnp.exp(best - best[0:1])
        gate_rows.append(e / jnp.sum(e, axis=0, keepdims=True))
        idx_rows.append(eidx)
    idx_ref[...] = jnp.concatenate(idx_rows, axis=0).T.astype(I32)
    gate_ref[...] = jnp.concatenate(gate_rows, axis=0).T


def _post(mixed2d, x2d, mod, rows_per_batch, norm2_g, w_out, w_query, keys, tm, prev=None, fin=None):
    t = x2d.shape[0]
    steps = t // tm
    row = lambda w: pl.BlockSpec((tm, w), lambda i: (i, 0))
    in_specs = [row(D_MODEL), row(D_MODEL),
                _mod_spec(2, rows_per_batch, tm), _mod_spec(4, rows_per_batch, tm),
                _mod_spec(3, rows_per_batch, tm), _const_spec((1, D_MODEL)),
                _const_spec((D_MODEL, D_MODEL)), _const_spec((D_MODEL, 2 * PEER_HEADS * PEER_KEY_HALF)),
                _const_spec((2 * PEER_HEADS, PEER_NKEYS, PEER_KEY_HALF))]
    out_specs = [row(D_MODEL), row(PACK_HALF), row(PEER_HK), row(PEER_HK)]
    out_shape = [jax.ShapeDtypeStruct((t, D_MODEL), F32), jax.ShapeDtypeStruct((t, PACK_HALF), I32),
                 jax.ShapeDtypeStruct((t, PEER_HK), I32), jax.ShapeDtypeStruct((t, PEER_HK), F32)]
    args = [mixed2d, x2d, mod, mod, mod, norm2_g.reshape(1, -1), w_out, w_query, keys]
    if prev is not None:
        tp = prev[0].shape[0]
        prow = pl.BlockSpec((tp // steps, PEER_HK), lambda i: (i, 0))
        in_specs += [prow, prow]
        out_specs += [prow]
        out_shape += [jax.ShapeDtypeStruct((tp, PEER_HK), I32)]
        args += list(prev)
    if fin is not None:
        x1_f, peer_f, mod_f, rows_f, final_g = fin
        tf = x1_f.shape[0]
        frow = pl.BlockSpec((tf // steps, D_MODEL), lambda i: (i, 0))
        in_specs += [frow, frow, _mod_spec(5, rows_f, tf // steps), _const_spec((1, D_MODEL))]
        out_specs += [frow]
        out_shape += [jax.ShapeDtypeStruct((tf, D_MODEL), F32)]
        args += [x1_f, peer_f, mod_f, final_g.reshape(1, -1)]
    return pl.pallas_call(
        functools.partial(_post_body, prev is not None, fin is not None),
        grid=(steps,),
        in_specs=in_specs, out_specs=out_specs, out_shape=out_shape,
        compiler_params=pltpu.CompilerParams(vmem_limit_bytes=VMEM_LIMIT),
        name="post",
    )(*args)


SC_CORES = 2
SC_SUBCORES = 16
SC_LANES = 16
SC_WORKERS = SC_CORES * SC_SUBCORES
SC_TOKENS = 16
SC_SLOTS = 4
SC_JOB_HEADS = 2
SC_BF16_GROUP = 4
PACK_HALF = D_MODEL // 2
SC_CHUNKS = PACK_HALF // SC_LANES
PROMPT_PARTS = 8
SEQ_SPLITS = 1
FIN_LAG = 3
ROW_TILE = 256


def _bf16_bits(v):
    return lax.bitcast_convert_type(v.astype(BF16).astype(F32), jnp.uint32)


def _pack_words(lo, hi):
    return lax.bitcast_convert_type((_bf16_bits(lo) >> 16) | _bf16_bits(hi), I32)


def _pack_body(x_ref, o_ref):
    o_ref[...] = _pack_words(x_ref[:, :PACK_HALF], x_ref[:, PACK_HALF:])


def _pack_table(tbl, rows=512):
    e = tbl.shape[0]
    return pl.pallas_call(
        _pack_body, grid=(e // rows,),
        in_specs=[pl.BlockSpec((rows, D_MODEL), lambda i: (i, 0))],
        out_specs=pl.BlockSpec((rows, PACK_HALF), lambda i: (i, 0)),
        out_shape=jax.ShapeDtypeStruct((e, PACK_HALF), I32), name="pack_table")(tbl)


def _tree_sum(terms):
    terms = list(terms)
    while len(terms) > 1:
        terms = [a + b for a, b in zip(terms[0::2], terms[1::2])] + terms[len(terms) & ~1:]
    return terms[0]


def _unpack_pair(w):
    lo = plsc.bitcast(lax.shift_left(w, jnp.full(w.shape, 16, I32)), F32)
    hi = plsc.bitcast(w & jnp.full(w.shape, -65536, I32), F32)
    return lo, hi


def _sc_mesh():
    return plsc.VectorSubcoreMesh(core_axis_name="c", subcore_axis_name="s")


def _sc_worker():
    return lax.axis_index("s") * SC_CORES + lax.axis_index("c")


def _sc_jobs(table_hbm, idx_v, buf, sem, compute):
    per_tok = PEER_HEADS // SC_JOB_HEADS
    njobs = idx_v.shape[0] * per_tok
    nrows = SC_JOB_HEADS * PEER_TOPK

    def copy(j, slot):
        rows = idx_v.at[j // per_tok, pl.ds((j % per_tok) * nrows, nrows)]
        return pltpu.make_async_copy(table_hbm.at[rows], buf.at[slot], sem.at[slot])

    for s in range(SC_SLOTS):
        copy(s, s).start()

    def job(j, c):
        s = j % SC_SLOTS
        copy(j, s).wait()

        def head(i, cc):
            compute(j // per_tok, (j % per_tok) * SC_JOB_HEADS + i, s, i * PEER_TOPK)
            return cc
        lax.fori_loop(0, SC_JOB_HEADS, head, 0)

        @pl.when(j + SC_SLOTS < njobs)
        def _next():
            copy(j + SC_SLOTS, s).start()
        return c

    lax.fori_loop(0, njobs, job, 0)


def _peer_u_body(n_tok, idx_hbm, h2_hbm, u_hbm, pre_hbm, idx_v, h2_v, pre_v, ubuf, acc_v, sem):
    base = _sc_worker() * n_tok
    lane = lax.iota(I32, SC_LANES)

    def compute(tt, h, slot, r0):
        def chunk(cg, accs):
            cs = [pl.ds((cg * SC_BF16_GROUP + i) * SC_LANES, SC_LANES) for i in range(SC_BF16_GROUP)]
            xs = [plsc.bitcast(h2_v[tt, c], BF16) for c in cs]
            out = []
            for k, a in enumerate(accs):
                part = _tree_sum([plsc.bitcast(ubuf[slot, r0 + k, c], BF16) * x for c, x in zip(cs, xs)])
                lo, hi = _unpack_pair(plsc.bitcast(part, I32))
                out.append(a + (lo + hi))
            return tuple(out)
        zero = jnp.zeros((SC_LANES,), F32)
        accs = lax.fori_loop(0, SC_CHUNKS // SC_BF16_GROUP, chunk, (zero,) * PEER_TOPK)
        for k, a in enumerate(accs):
            acc_v[k, :] = a
        tot = zero
        for j in range(SC_LANES):
            tot = tot + plsc.load_gather(acc_v, [lane, (lane + j) & (SC_LANES - 1)])
        pre_v[tt, pl.ds(h * PEER_TOPK, PEER_TOPK)] = tot

    tb = idx_v.shape[0]

    def block(bi, c):
        t0 = base + bi * tb
        pltpu.sync_copy(idx_hbm.at[pl.ds(t0, tb)], idx_v)
        pltpu.sync_copy(h2_hbm.at[pl.ds(t0, tb)], h2_v)
        _sc_jobs(u_hbm, idx_v, ubuf, sem, compute)
        pltpu.sync_copy(pre_v, pre_hbm.at[pl.ds(t0, tb)])
        return c

    lax.fori_loop(0, n_tok // tb, block, 0)


def _peer_v_body(n_tok, idx_hbm, coef_hbm, v_hbm, out_hbm, idx_v, coef_v, out_v, vbuf, sem):
    base = _sc_worker() * n_tok
    zero = jnp.zeros((SC_LANES,), F32)

    def compute(tt, h, slot, r0):
        cvec = coef_v[tt, pl.ds(h * PEER_TOPK, PEER_TOPK)]
        cb = [plsc.bitcast(jnp.take_along_axis(cvec, jnp.full((SC_LANES,), k, I32), axis=0), BF16)
              for k in range(PEER_TOPK)]

        @plsc.parallel_loop(0, SC_CHUNKS, unroll=2)
        def _chunk(c):
            cs = pl.ds(c * SC_LANES, SC_LANES)
            prods = [plsc.bitcast(vbuf[slot, r0 + k, cs], BF16) * cb[k] for k in range(PEER_TOPK)]
            pairs = [_unpack_pair(plsc.bitcast(_tree_sum(prods[g:g + SC_BF16_GROUP]), I32))
                     for g in range(0, PEER_TOPK, SC_BF16_GROUP)]
            for half, off in ((0, 0), (1, PACK_HALF)):
                plsc.addupdate(out_v.at[tt, pl.ds(off + c * SC_LANES, SC_LANES)],
                               _tree_sum([p[half] for p in pairs]))

    tb = idx_v.shape[0]

    def block(bi, c):
        t0 = base + bi * tb
        pltpu.sync_copy(idx_hbm.at[pl.ds(t0, tb)], idx_v)
        pltpu.sync_copy(coef_hbm.at[pl.ds(t0, tb)], coef_v)

        def clear(i, cc):
            per_row = D_MODEL // SC_LANES
            out_v[i // per_row, pl.ds((i % per_row) * SC_LANES, SC_LANES)] = zero
            return cc
        lax.fori_loop(0, tb * (D_MODEL // SC_LANES), clear, 0)
        _sc_jobs(v_hbm, idx_v, vbuf, sem, compute)
        pltpu.sync_copy(out_v, out_hbm.at[pl.ds(t0, tb)])
        return c

    lax.fori_loop(0, n_tok // tb, block, 0)


def _peer_sc(body, idx, rows, table, out_width, name):
    t = idx.shape[0]
    assert t % SC_WORKERS == 0
    n_tok = t // SC_WORKERS
    tb = min(SC_TOKENS, n_tok)
    assert n_tok % tb == 0 and tb * PEER_HEADS // SC_JOB_HEADS >= SC_SLOTS
    return pl.kernel(
        functools.partial(body, n_tok),
        out_type=jax.ShapeDtypeStruct((t, out_width), F32),
        mesh=_sc_mesh(),
        scratch_types=[pltpu.VMEM((tb, PEER_HK), I32),
                       pltpu.VMEM((tb, rows.shape[1]), rows.dtype),
                       pltpu.VMEM((tb, out_width), F32),
                       pltpu.VMEM((SC_SLOTS, SC_JOB_HEADS * PEER_TOPK, PACK_HALF), I32)]
                      + ([pltpu.VMEM((PEER_TOPK, SC_LANES), F32)] if body is _peer_u_body else [])
                      + [pltpu.SemaphoreType.DMA((SC_SLOTS,))],
        compiler_params=pltpu.CompilerParams(needs_layout_passes=False),
        name=name,
    )(idx, rows, table)


def _coef_words(pre, gates):
    return _pack_words(*(gates * _gelu(pre),) * 2)


def _coef_body(pre_ref, gate_ref, coef_ref):
    coef_ref[...] = _coef_words(pre_ref[...], gate_ref[...])


def _coef(pre, gates, tm):
    t = pre.shape[0]
    row = pl.BlockSpec((tm, PEER_HK), lambda i: (i, 0))
    return pl.pallas_call(_coef_body, grid=(t // tm,), in_specs=[row, row], out_specs=row,
                          out_shape=jax.ShapeDtypeStruct((t, PEER_HK), I32), name="coef")(pre, gates)


def _final_body(x1_ref, peer_ref, g2_ref, fng_ref, y_ref):
    x2 = x1_ref[...] + _mod_rows(g2_ref) * peer_ref[...]
    y_ref[...] = x2 * lax.rsqrt(jnp.mean(x2 * x2, axis=-1, keepdims=True) + EPS) * fng_ref[...]


def _final(x1, peer_out, mod, rows_per_batch, final_g, tm):
    t = x1.shape[0]
    row = pl.BlockSpec((tm, D_MODEL), lambda i: (i, 0))
    return pl.pallas_call(
        _final_body, grid=(t // tm,),
        in_specs=[row, row, _mod_spec(5, rows_per_batch, tm), _const_spec((1, D_MODEL))],
        out_specs=row, out_shape=jax.ShapeDtypeStruct((t, D_MODEL), F32), name="final",
    )(x1, peer_out, mod, final_g.reshape(1, -1))


def _expert_gather_v(g, coef, expert_v):
    g["peer_out"] = _peer_sc(_peer_v_body, g["idx"], coef, expert_v, D_MODEL, "peer_v")


def _front(x, mod, conv_buf, s0, pool_buf, start, chunk, tm, wts, prev, fin):
    b, l, _ = x.shape
    t = b * l
    x2d = x.reshape(t, D_MODEL)
    if l >= tm:
        modx = mod.reshape(b, 6, 1, D_MODEL).transpose(1, 0, 2, 3)
    else:
        modx = jnp.repeat(mod.reshape(b, 6, D_MODEL), l, axis=0).transpose(1, 0, 2)
    outs = _inproj(x2d, modx, l, wts["norm1_g"], wts["w_cat"], tm)
    lp = -(-l // chunk) * chunk
    proj = {}
    for (name, w), a in zip(_IN_BLOCKS, outs):
        a = a.reshape(b, l, w)
        proj[name] = a if lp == l else jnp.pad(a, ((0, 0), (0, lp - l), (0, 0)))
    mixed, nconv, ns, npool = _mixer(proj, conv_buf, s0, pool_buf, start, l, chunk,
                                     wts["conv_w"], wts["a_log"], wts["dt_bias"], wts["dn_norm_g"],
                                     wts["w_pool"], wts["pool_scale"])
    mixed2d = mixed[:, :l].reshape(t, D_MODEL)
    res = _post(mixed2d, x2d, modx, l, wts["norm2_g"], wts["w_out"], wts["w_query"], wts["keys"], tm,
                prev=None if prev is None else (prev["pre"], prev["gates"]),
                fin=None if fin is None else (fin["x1"], fin["peer_out"], fin["mod"], fin["l"],
                                              wts["final_norm_g"]))
    x1, h2, idx, gates = res[:4]
    extra = list(res[4:])
    coef_prev = extra.pop(0) if prev is not None else None
    y_fin = extra.pop(0).reshape(fin["b"], fin["l"], D_MODEL) if fin is not None else None
    pre = _peer_sc(_peer_u_body, idx, h2, wts["expert_u"], PEER_HK, "peer_u")
    g = dict(x1=x1, idx=idx, gates=gates, pre=pre, mod=modx, b=b, l=l, tm=tm,
             states=(nconv, ns, npool))
    return g, coef_prev, y_fin


def kernel(x_prompt, x_sample, c_prompt, c_sample, state_conv, state_delta, state_pool, w_ada, b_ada, norm1_g, w_in, conv_w, a_log, dt_bias, dn_norm_g, w_pool, pool_scale, w_out, norm2_g, w_query, sub_keys, expert_u, expert_v, final_norm_g):
    bp = x_prompt.shape[0]
    bs = x_sample.shape[0]
    yp, ys = x_prompt, x_sample
    conv_p, delta_p, pool_p, conv_s, delta_s, pool_s = [], [], [], [], [], []
    zero_conv = jnp.zeros((bp, CONV_WIDTH - 1, QKV_WIDTH), F32)
    zero_delta = jnp.zeros((bp, DN_HEADS, DN_HEAD_DIM, DN_HEAD_DIM), F32)
    zero_pool = jnp.zeros((bp, POOL_BUF, POOL_WIDTH), F32)
    c_all = jnp.concatenate([c_prompt, c_sample], axis=0)
    for layer in range(DEPTH):
        wi = w_in[layer]
        o_b = QKV_WIDTH
        o_z = o_b + 2 * DN_HEADS
        w_ba = jnp.pad(wi[:, o_b:o_z], ((0, 0), (0, LANES - 2 * DN_HEADS)))
        w_cat = jnp.concatenate([wi[:, :o_b], wi[:, o_z:], w_ba], axis=1).astype(BF16)
        last = layer == DEPTH - 1
        wts = dict(
            norm1_g=norm1_g[layer], w_cat=w_cat, conv_w=conv_w[layer], a_log=a_log[layer],
            dt_bias=dt_bias[layer], dn_norm_g=dn_norm_g[layer], w_pool=w_pool[layer],
            pool_scale=pool_scale[layer], w_out=w_out[layer].astype(BF16), norm2_g=norm2_g[layer],
            w_query=w_query[layer].astype(BF16),
            keys=sub_keys[layer].reshape(2 * PEER_HEADS, PEER_NKEYS, PEER_KEY_HALF).astype(BF16),
            expert_u=_pack_table(expert_u[layer]), expert_v=_pack_table(expert_v[layer]),
            final_norm_g=final_norm_g if last else jnp.ones_like(final_norm_g))
        mod = _ada(c_all, w_ada[layer], b_ada[layer])
        assert last, "final norm is fused into the expert stage"
        step = bp // PROMPT_PARTS
        ls = x_prompt.shape[1] // SEQ_SPLITS
        zeros = (zero_conv[:step], zero_delta[:step], zero_pool[:step])
        jobs = [(yp[b0:b0 + step, s0:s0 + ls], mod[b0:b0 + step], zeros if s0 == 0 else None, s0, DN_CHUNK)
                for b0 in range(0, bp, step) for s0 in range(0, SEQ_SPLITS * ls, ls)]
        jobs.append((ys, mod[bp:], (state_conv[layer], state_delta[layer], state_pool[layer]),
                     PAST_LEN, SUBLANES))
        groups = []
        for j, (xg, mg, states, start, chunk) in enumerate(jobs):
            prev = groups[j - 1] if j >= 1 else None
            fin = groups[j - FIN_LAG] if j >= FIN_LAG else None
            if fin is not None and fin["x1"].shape[0] != xg.shape[0] * xg.shape[1]:
                fin = None
            if states is None:
                states = prev["states"]
            g, coef_prev, y_fin = _front(xg, mg, *states, start, chunk, ROW_TILE, wts, prev, fin)
            if prev is not None:
                _expert_gather_v(prev, coef_prev, wts["expert_v"])
            if fin is not None:
                fin["y"] = y_fin
            groups.append(g)
        _expert_gather_v(groups[-1], _coef(groups[-1]["pre"], groups[-1]["gates"], ROW_TILE),
                         wts["expert_v"])
        for g in groups:
            if "y" not in g:
                g["y"] = _final(g["x1"], g["peer_out"], g["mod"], g["l"], wts["final_norm_g"],
                                g["tm"]).reshape(g["b"], g["l"], D_MODEL)
        rows = [groups[i:i + SEQ_SPLITS] for i in range(0, len(groups) - 1, SEQ_SPLITS)]
        yp = jnp.concatenate([jnp.concatenate([g["y"] for g in row], axis=1) for row in rows], axis=0)
        cp, sp, pp = (jnp.concatenate(a, axis=0) for a in zip(*(row[-1]["states"] for row in rows)))
        ys = groups[-1]["y"]
        cs, ss, ps = groups[-1]["states"]
        conv_p.append(cp)
        delta_p.append(sp)
        pool_p.append(pp)
        conv_s.append(cs)
        delta_s.append(ss)
        pool_s.append(ps)
    return (yp, ys, jnp.stack(conv_p), jnp.stack(delta_p), jnp.stack(pool_p),
            jnp.stack(conv_s), jnp.stack(delta_s), jnp.stack(pool_s))
```

```python
import functools

import jax
import jax.numpy as jnp
from jax import lax
from jax.experimental import pallas as pl
from jax.experimental.pallas import tpu as pltpu
from jax.experimental.pallas import tpu_sc as plsc

F32 = jnp.float32
BF16 = jnp.bfloat16
I32 = jnp.int32

D_MODEL = 1024
DEPTH = 1
PAST_LEN = 16384
DN_HEADS = 8
DN_HEAD_DIM = 128
DN_WIDTH = DN_HEADS * DN_HEAD_DIM
QKV_WIDTH = 3 * DN_WIDTH
CONV_WIDTH = 4
DN_CHUNK = 64
POOL_WINDOWS = (2, 4, 8, 16)
POOL_GROUP_DIM = 128
POOL_WIDTH = len(POOL_WINDOWS) * POOL_GROUP_DIM
POOL_OUT_GROUP = D_MODEL // len(POOL_WINDOWS)
POOL_BUF = max(POOL_WINDOWS) - 1
PEER_HEADS = 8
PEER_NKEYS = 128
PEER_TOPK = 16
PEER_KEY_HALF = 128
PEER_HK = PEER_HEADS * PEER_TOPK
EPS = 1e-6

LANES = 128
SUBLANES = 8
CONV_PAD = SUBLANES
POOL_PAD = 16
VMEM_LIMIT = 56 * 1024 * 1024

NT_DIMS = (((1,), (1,)), ((), ()))
TN_DIMS = (((0,), (0,)), ((), ()))


def _dot(a, b):
    return jnp.dot(a.astype(BF16), b.astype(BF16), preferred_element_type=F32)


def _dot_nt(a, b):
    return lax.dot_general(a.astype(BF16), b.astype(BF16), NT_DIMS, preferred_element_type=F32)


def _split3(x):
    hi = x.astype(BF16)
    r1 = x - hi.astype(F32)
    mid = r1.astype(BF16)
    lo = (r1 - mid.astype(F32)).astype(BF16)
    return hi, mid, lo


def _silu(x):
    return x * jax.nn.sigmoid(x)


def _gelu(x):
    return 0.5 * x * (1.0 + lax.erf(x * (0.5 ** 0.5)))


def _softplus(x):
    return jnp.maximum(x, 0.0) + jnp.log(1.0 + jnp.exp(-jnp.abs(x)))


def _mod_rows(ref):
    m = ref[...]
    return m.reshape(m.shape[-2], m.shape[-1])


def _mod_spec(k, rows_per_batch, tm):
    if rows_per_batch >= tm:
        tiles = rows_per_batch // tm
        return pl.BlockSpec((1, 1, 1, D_MODEL), lambda i, *_: (k, i // tiles, 0, 0))
    return pl.BlockSpec((1, tm, D_MODEL), lambda i, *_: (k, i, 0))


def _const_spec(shape):
    nd = len(shape)
    return pl.BlockSpec(shape, lambda *_: (0,) * nd)


def _ada_body(c_ref, w_ref, b_ref, o_ref):
    o_ref[...] = _dot(_silu(c_ref[...]), w_ref[...]) + b_ref[...]


def _ada(c, w_ada, b_ada):
    n = c.shape[0]
    return pl.pallas_call(
        _ada_body,
        grid=(6,),
        in_specs=[pl.BlockSpec((n, D_MODEL), lambda j: (0, 0)),
                  pl.BlockSpec((D_MODEL, D_MODEL), lambda j: (0, j)),
                  pl.BlockSpec((1, D_MODEL), lambda j: (0, j))],
        out_specs=pl.BlockSpec((n, D_MODEL), lambda j: (0, j)),
        out_shape=jax.ShapeDtypeStruct((n, 6 * D_MODEL), F32),
        name="ada",
    )(c, w_ada, b_ada.reshape(1, -1))


_IN_BLOCKS = (("qkv", QKV_WIDTH), ("z", DN_WIDTH), ("pool", POOL_WIDTH),
              ("ga", D_MODEL), ("gb", D_MODEL), ("ba", LANES))
_IN_TOTAL = sum(w for _, w in _IN_BLOCKS)
_IN_COL_CHUNK = 512


def _inproj_body(x_ref, sc_ref, sh_ref, g_ref, w_ref, *out_refs):
    x = x_ref[...]
    y = x * lax.rsqrt(jnp.mean(x * x, axis=-1, keepdims=True) + EPS) * g_ref[...]
    h = (y * (1.0 + _mod_rows(sc_ref)) + _mod_rows(sh_ref)).astype(BF16)
    off = 0
    for (_, width), o_ref in zip(_IN_BLOCKS, out_refs):
        for c0 in range(0, width, _IN_COL_CHUNK):
            cw = min(_IN_COL_CHUNK, width - c0)
            o_ref[:, c0:c0 + cw] = jnp.dot(h, w_ref[:, off + c0:off + c0 + cw],
                                           preferred_element_type=F32)
        off += width


def _inproj(x2d, mod, rows_per_batch, norm_g, w_cat, tm):
    t = x2d.shape[0]
    row = lambda w: pl.BlockSpec((tm, w), lambda i: (i, 0))
    return pl.pallas_call(
        _inproj_body,
        grid=(t // tm,),
        in_specs=[row(D_MODEL), _mod_spec(1, rows_per_batch, tm), _mod_spec(0, rows_per_batch, tm),
                  _const_spec((1, D_MODEL)),
                  pl.BlockSpec((D_MODEL, _IN_TOTAL), lambda i: (0, 0), pipeline_mode=pl.Buffered(1))],
        out_specs=[row(w) for _, w in _IN_BLOCKS],
        out_shape=[jax.ShapeDtypeStruct((t, w), F32) for _, w in _IN_BLOCKS],
        compiler_params=pltpu.CompilerParams(vmem_limit_bytes=VMEM_LIMIT),
        name="inproj",
    )(x2d, mod, mod, norm_g.reshape(1, -1), w_cat)


def _mixer_body(C, Lv, start,
                qkv_ref, ba_ref, z_ref, pin_ref, ga_ref, gb_ref, cbuf_ref, s0_ref, pbuf_ref,
                convw_ref, alog_ref, dtb_ref, dng_ref, wpool_ref, pscale_ref,
                mixed_ref, nconv_ref, ns_ref, npool_ref,
                xp_scr, act_scr, s_scr, pp_scr, odn_scr):
    n = pl.program_id(1)
    last = pl.num_programs(1) - 1

    @pl.when(n == 0)
    def _load_state():
        xp_scr[0:CONV_PAD, :] = cbuf_ref[0]
        pp_scr[0:POOL_PAD, :] = pbuf_ref[0]
        s_scr[...] = s0_ref[0]

    xp_scr[CONV_PAD:CONV_PAD + C, :] = qkv_ref[0]
    for c0 in range(0, QKV_WIDTH, 512):
        cs = slice(c0, c0 + 512)
        y = xp_scr[CONV_PAD:CONV_PAD + C, cs] * convw_ref[CONV_WIDTH - 1:CONV_WIDTH, cs]
        for k in range(CONV_WIDTH - 1):
            r0 = CONV_PAD - (CONV_WIDTH - 1) + k
            y = y + xp_scr[r0:r0 + C, cs] * convw_ref[k:k + 1, cs]
        act_scr[:, cs] = _silu(y)

    ba = ba_ref[0]
    lane = lax.broadcasted_iota(I32, (C, LANES), 1)
    beta_all = jax.nn.sigmoid(ba)
    g_all = -jnp.exp(alog_ref[...]) * _softplus(ba + dtb_ref[...])
    if Lv < C:
        valid = lax.broadcasted_iota(I32, (C, LANES), 0) < Lv
        beta_all = jnp.where(valid, beta_all, 0.0)
        g_all = jnp.where(valid, g_all, 0.0)
    ii = lax.broadcasted_iota(I32, (C, C), 0)
    jj = lax.broadcasted_iota(I32, (C, C), 1)
    causal = ii >= jj
    strict = ii > jj
    tril = jnp.where(causal, 1.0, 0.0).astype(BF16)
    eye = jnp.where(ii == jj, 1.0, 0.0)
    gc_all = sum(jnp.dot(tril, part, preferred_element_type=F32) for part in _split3(g_all))
    if C < LANES:
        gc_sq = jnp.concatenate([gc_all, jnp.zeros((LANES - C, LANES), F32)], axis=0)
    else:
        gc_sq = gc_all
    gc_t = gc_sq.T

    H = range(DN_HEADS)
    hsl = [slice(h * DN_HEAD_DIM, (h + 1) * DN_HEAD_DIM) for h in H]
    beta = [jnp.sum(jnp.where(lane == h, beta_all, 0.0), axis=1, keepdims=True) for h in H]
    gcol = [jnp.sum(jnp.where(lane == DN_HEADS + h, gc_all, 0.0), axis=1, keepdims=True) for h in H]
    grow = [gc_t[DN_HEADS + h:DN_HEADS + h + 1, 0:C] for h in H]
    glast = [g[C - 1:C, :] for g in gcol]
    q = [act_scr[:, hsl[h]] for h in H]
    k = [act_scr[:, DN_WIDTH + h * DN_HEAD_DIM:DN_WIDTH + (h + 1) * DN_HEAD_DIM] for h in H]
    v = [act_scr[:, 2 * DN_WIDTH + h * DN_HEAD_DIM:2 * DN_WIDTH + (h + 1) * DN_HEAD_DIM] for h in H]
    q = [x * lax.rsqrt(jnp.sum(x * x, axis=-1, keepdims=True) + EPS) * (DN_HEAD_DIM ** -0.5) for x in q]
    k = [x * lax.rsqrt(jnp.sum(x * x, axis=-1, keepdims=True) + EPS) for x in k]
    kb = [k[h] * beta[h] for h in H]
    vb = [v[h] * beta[h] for h in H]
    decay = [jnp.where(causal, jnp.exp(jnp.where(causal, gcol[h] - grow[h], 0.0)), 0.0) for h in H]
    lower = [jnp.where(strict, _dot_nt(kb[h], k[h]) * decay[h], 0.0) for h in H]
    ainv = [eye - x for x in lower]
    pw = lower
    p = 1
    while 2 * p < C:
        pw = [_dot(x, x) for x in pw]
        ainv = [ainv[h] + _dot(ainv[h], pw[h]) for h in H]
        p *= 2
    sol = [_dot(ainv[h], jnp.concatenate([vb[h], kb[h] * jnp.exp(gcol[h])], axis=1)) for h in H]
    qk = [_dot_nt(q[h], k[h]) * decay[h] for h in H]
    k_tail = [k[h] * jnp.exp(glast[h] - gcol[h]) for h in H]
    S = [s_scr[h] for h in H]
    v_new = [sol[h][:, :DN_HEAD_DIM] - _dot(sol[h][:, DN_HEAD_DIM:], S[h]) for h in H]
    o = [_dot(q[h] * jnp.exp(gcol[h]), S[h]) + _dot(qk[h], v_new[h]) for h in H]
    for h in H:
        s_scr[h] = S[h] * jnp.exp(glast[h]) + lax.dot_general(
            k_tail[h].astype(BF16), v_new[h].astype(BF16), TN_DIMS, preferred_element_type=F32)
    for h in H:
        zf = z_ref[0, :, hsl[h]]
        odn_scr[:, hsl[h]] = (o[h] * lax.rsqrt(jnp.mean(o[h] * o[h], axis=-1, keepdims=True) + EPS)
                              * dng_ref[...] * _silu(zf))

    pp_scr[POOL_PAD:POOL_PAD + C, :] = pin_ref[0]
    pos = start + n * C + lax.broadcasted_iota(I32, (C, 1), 0)
    for gi, win in enumerate(POOL_WINDOWS):
        gs = slice(gi * POOL_GROUP_DIM, (gi + 1) * POOL_GROUP_DIM)
        xg = pp_scr[POOL_PAD:POOL_PAD + C, gs]
        ssum = xg
        for sft in range(1, win):
            ssum = ssum + pp_scr[POOL_PAD - sft:POOL_PAD - sft + C, gs]
        cnt = jnp.minimum(pos + 1, win).astype(F32)
        pooled = ssum / cnt - xg
        os_ = slice(gi * POOL_OUT_GROUP, (gi + 1) * POOL_OUT_GROUP)
        yp = _dot(pooled, wpool_ref[gi]) * pscale_ref[:, os_]
        mixed_ref[0, :, os_] = (jax.nn.sigmoid(ga_ref[0, :, os_]) * odn_scr[:, os_]
                                + jax.nn.sigmoid(gb_ref[0, :, os_]) * yp)

    @pl.when(n == last)
    def _store_state():
        nconv_ref[0] = xp_scr[Lv + CONV_PAD - (CONV_WIDTH - 1):Lv + CONV_PAD, :]
        npool_ref[0] = pp_scr[Lv + POOL_PAD - POOL_BUF:Lv + POOL_PAD, :]
        ns_ref[0] = s_scr[...]

    xp_scr[0:CONV_PAD, :] = xp_scr[C:C + CONV_PAD, :]
    pp_scr[0:POOL_PAD, :] = pp_scr[C:C + POOL_PAD, :]


def _mixer(proj, conv_buf, s0, pool_buf, start, seq_len, C,
           conv_w, a_log, dt_bias, dn_norm_g, w_pool, pool_scale):
    b, lp, _ = proj["qkv"].shape
    nchunks = lp // C
    lv = seq_len - (nchunks - 1) * C
    cbuf = jnp.pad(conv_buf, ((0, 0), (CONV_PAD - (CONV_WIDTH - 1), 0), (0, 0)))
    pbuf = jnp.pad(pool_buf, ((0, 0), (POOL_PAD - POOL_BUF, 0), (0, 0)))
    lane_pad = lambda a: jnp.pad(a.reshape(1, -1), ((0, 0), (DN_HEADS, LANES - 2 * DN_HEADS)))
    chunk = lambda w: pl.BlockSpec((1, C, w), lambda i, j: (i, j, 0))
    state = lambda *s: pl.BlockSpec((1,) + s, lambda i, j: (i,) + (0,) * len(s))
    return pl.pallas_call(
        functools.partial(_mixer_body, C, lv, start),
        grid=(b, nchunks),
        in_specs=[chunk(QKV_WIDTH), chunk(LANES), chunk(DN_WIDTH), chunk(POOL_WIDTH),
                  chunk(D_MODEL), chunk(D_MODEL),
                  state(CONV_PAD, QKV_WIDTH), state(DN_HEADS, DN_HEAD_DIM, DN_HEAD_DIM),
                  state(POOL_PAD, POOL_WIDTH),
                  _const_spec((CONV_WIDTH, QKV_WIDTH)), _const_spec((1, LANES)), _const_spec((1, LANES)),
                  _const_spec((1, DN_HEAD_DIM)),
                  _const_spec((len(POOL_WINDOWS), POOL_GROUP_DIM, POOL_OUT_GROUP)),
                  _const_spec((1, D_MODEL))],
        out_specs=[chunk(D_MODEL), state(CONV_WIDTH - 1, QKV_WIDTH),
                   state(DN_HEADS, DN_HEAD_DIM, DN_HEAD_DIM), state(POOL_BUF, POOL_WIDTH)],
        out_shape=[jax.ShapeDtypeStruct((b, lp, D_MODEL), F32),
                   jax.ShapeDtypeStruct((b, CONV_WIDTH - 1, QKV_WIDTH), F32),
                   jax.ShapeDtypeStruct((b, DN_HEADS, DN_HEAD_DIM, DN_HEAD_DIM), F32),
                   jax.ShapeDtypeStruct((b, POOL_BUF, POOL_WIDTH), F32)],
        scratch_shapes=[pltpu.VMEM((CONV_PAD + C + CONV_PAD, QKV_WIDTH), F32),
                        pltpu.VMEM((C, QKV_WIDTH), F32),
                        pltpu.VMEM((DN_HEADS, DN_HEAD_DIM, DN_HEAD_DIM), F32),
                        pltpu.VMEM((POOL_PAD + C + POOL_PAD, POOL_WIDTH), F32),
                        pltpu.VMEM((C, DN_WIDTH), F32)],
        compiler_params=pltpu.CompilerParams(dimension_semantics=("arbitrary", "arbitrary"),
                                             vmem_limit_bytes=VMEM_LIMIT),
        name="mixer",
    )(proj["qkv"], proj["ba"], proj["z"], proj["pool"], proj["ga"], proj["gb"], cbuf, s0, pbuf,
      conv_w, lane_pad(a_log), lane_pad(dt_bias), dn_norm_g.reshape(1, -1), w_pool,
      pool_scale.reshape(1, -1))


def _top16(s, ids, payload=None):
    big = float(2 ** 24)
    vals, sel, pays = [], [], []
    for _ in range(PEER_TOPK):
        m = jnp.max(s, axis=0, keepdims=True)
        am = jnp.min(jnp.where(s == m, ids, big), axis=0, keepdims=True)
        hit = ids == am
        if payload is not None:
            pays.append(jnp.max(jnp.where(hit, payload, -1.0), axis=0, keepdims=True))
        s = jnp.where(hit, -jnp.inf, s)
        vals.append(m)
        sel.append(am)
    out = (jnp.concatenate(vals, axis=0), jnp.concatenate(sel, axis=0))
    if payload is not None:
        out += (jnp.concatenate(pays, axis=0),)
    return out


_CAND_EDGE = 4


def _post_body(has_prev, has_fin, mixed_ref, x_ref, g1_ref, sc2_ref, sh2_ref, n2g_ref, wout_ref,
               wq_ref, keys_ref, *refs):
    refs = list(refs)
    prev_in = [refs.pop(0) for _ in range(2 if has_prev else 0)]
    fin_in = [refs.pop(0) for _ in range(4 if has_fin else 0)]
    x1_ref, h2_ref, idx_ref, gate_ref = refs[:4]
    extra_out = refs[4:]
    if has_prev:
        pre_ref, pgate_ref = prev_in
        extra_out.pop(0)[...] = _coef_words(pre_ref[...], pgate_ref[...])
    if has_fin:
        _final_body(*fin_in, extra_out.pop(0))
    tm = x_ref.shape[0]
    x1 = x_ref[...] + _mod_rows(g1_ref) * _dot(mixed_ref[...], wout_ref[...])
    x1_ref[...] = x1
    y = x1 * lax.rsqrt(jnp.mean(x1 * x1, axis=-1, keepdims=True) + EPS) * n2g_ref[...]
    h2 = y * (1.0 + _mod_rows(sc2_ref)) + _mod_rows(sh2_ref)
    h2_ref[...] = _pack_words(h2[:, :PACK_HALF], h2[:, PACK_HALF:])
    q = _dot(h2, wq_ref[...])

    K = PEER_TOPK
    key_id = lax.broadcasted_iota(I32, (PEER_NKEYS, 1), 0).astype(F32)
    r16 = lax.broadcasted_iota(I32, (K, 1), 0)
    cand_id = jnp.concatenate([(a * K + r16) for a in range(_CAND_EDGE)]
                              + [(r16 * K + b) for b in range(_CAND_EDGE)], axis=0).astype(F32)
    dup = r16 < _CAND_EDGE
    idx_rows, gate_rows = [], []
    for h in range(PEER_HEADS):
        half = []
        for p in range(2):
            c0 = (h * 2 + p) * PEER_KEY_HALF
            st = _dot_nt(keys_ref[h * 2 + p], q[:, c0:c0 + PEER_KEY_HALF])
            half.append(_top16(st, key_id))
        (s1, i1), (s2, i2) = half
        cand = jnp.concatenate(
            [s1[a:a + 1] + s2 for a in range(_CAND_EDGE)]
            + [jnp.where(dup, -jnp.inf, s1 + s2[b:b + 1]) for b in range(_CAND_EDGE)], axis=0)
        cidx = jnp.concatenate(
            [i1[a:a + 1] * PEER_NKEYS + i2 for a in range(_CAND_EDGE)]
            + [i1 * PEER_NKEYS + i2[b:b + 1] for b in range(_CAND_EDGE)], axis=0)
        best, _, eidx = _top16(cand, cand_id, cidx)
        e = jnp.exp(best - best[0:1])
        gate_rows.append(e / jnp.sum(e, axis=0, keepdims=True))
        idx_rows.append(eidx)
    idx_ref[...] = jnp.concatenate(idx_rows, axis=0).T.astype(I32)
    gate_ref[...] = jnp.concatenate(gate_rows, axis=0).T


def _post(mixed2d, x2d, mod, rows_per_batch, norm2_g, w_out, w_query, keys, tm, prev=None, fin=None):
    t = x2d.shape[0]
    steps = t // tm
    row = lambda w: pl.BlockSpec((tm, w), lambda i: (i, 0))
    in_specs = [row(D_MODEL), row(D_MODEL),
                _mod_spec(2, rows_per_batch, tm), _mod_spec(4, rows_per_batch, tm),
                _mod_spec(3, rows_per_batch, tm), _const_spec((1, D_MODEL)),
                _const_spec((D_MODEL, D_MODEL)), _const_spec((D_MODEL, 2 * PEER_HEADS * PEER_KEY_HALF)),
                _const_spec((2 * PEER_HEADS, PEER_NKEYS, PEER_KEY_HALF))]
    out_specs = [row(D_MODEL), row(PACK_HALF), row(PEER_HK), row(PEER_HK)]
    out_shape = [jax.ShapeDtypeStruct((t, D_MODEL), F32), jax.ShapeDtypeStruct((t, PACK_HALF), I32),
                 jax.ShapeDtypeStruct((t, PEER_HK), I32), jax.ShapeDtypeStruct((t, PEER_HK), F32)]
    args = [mixed2d, x2d, mod, mod, mod, norm2_g.reshape(1, -1), w_out, w_query, keys]
    if prev is not None:
        tp = prev[0].shape[0]
        prow = pl.BlockSpec((tp // steps, PEER_HK), lambda i: (i, 0))
        in_specs += [prow, prow]
        out_specs += [prow]
        out_shape += [jax.ShapeDtypeStruct((tp, PEER_HK), I32)]
        args += list(prev)
    if fin is not None:
        x1_f, peer_f, mod_f, rows_f, final_g = fin
        tf = x1_f.shape[0]
        frow = pl.BlockSpec((tf // steps, D_MODEL), lambda i: (i, 0))
        in_specs += [frow, frow, _mod_spec(5, rows_f, tf // steps), _const_spec((1, D_MODEL))]
        out_specs += [frow]
        out_shape += [jax.ShapeDtypeStruct((tf, D_MODEL), F32)]
        args += [x1_f, peer_f, mod_f, final_g.reshape(1, -1)]
    return pl.pallas_call(
        functools.partial(_post_body, prev is not None, fin is not None),
        grid=(steps,),
        in_specs=in_specs, out_specs=out_specs, out_shape=out_shape,
        compiler_params=pltpu.CompilerParams(vmem_limit_bytes=VMEM_LIMIT),
        name="post",
    )(*args)


SC_CORES = 2
SC_SUBCORES = 16
SC_LANES = 16
SC_WORKERS = SC_CORES * SC_SUBCORES
SC_TOKENS = 16
SC_SLOTS = 4
SC_JOB_HEADS = 2
SC_BF16_GROUP = 4
PACK_HALF = D_MODEL // 2
SC_CHUNKS = PACK_HALF // SC_LANES
PROMPT_PARTS = 8
EDGE_SPLITS = 2
FIN_LAG = 3
ROW_TILE = 256


def _bf16_bits(v):
    return lax.bitcast_convert_type(v.astype(BF16).astype(F32), jnp.uint32)


def _pack_words(lo, hi):
    return lax.bitcast_convert_type((_bf16_bits(lo) >> 16) | _bf16_bits(hi), I32)


def _pack_body(x_ref, o_ref):
    o_ref[...] = _pack_words(x_ref[:, :PACK_HALF], x_ref[:, PACK_HALF:])


def _pack_table(tbl, rows=512):
    e = tbl.shape[0]
    return pl.pallas_call(
        _pack_body, grid=(e // rows,),
        in_specs=[pl.BlockSpec((rows, D_MODEL), lambda i: (i, 0))],
        out_specs=pl.BlockSpec((rows, PACK_HALF), lambda i: (i, 0)),
        out_shape=jax.ShapeDtypeStruct((e, PACK_HALF), I32), name="pack_table")(tbl)


def _tree_sum(terms):
    terms = list(terms)
    while len(terms) > 1:
        terms = [a + b for a, b in zip(terms[0::2], terms[1::2])] + terms[len(terms) & ~1:]
    return terms[0]


def _unpack_pair(w):
    lo = plsc.bitcast(lax.shift_left(w, jnp.full(w.shape, 16, I32)), F32)
    hi = plsc.bitcast(w & jnp.full(w.shape, -65536, I32), F32)
    return lo, hi


def _sc_mesh():
    return plsc.VectorSubcoreMesh(core_axis_name="c", subcore_axis_name="s")


def _sc_worker():
    return lax.axis_index("s") * SC_CORES + lax.axis_index("c")


def _sc_jobs(table_hbm, idx_v, buf, sem, compute):
    per_tok = PEER_HEADS // SC_JOB_HEADS
    njobs = idx_v.shape[0] * per_tok
    nrows = SC_JOB_HEADS * PEER_TOPK

    def copy(j, slot):
        rows = idx_v.at[j // per_tok, pl.ds((j % per_tok) * nrows, nrows)]
        return pltpu.make_async_copy(table_hbm.at[rows], buf.at[slot], sem.at[slot])

    for s in range(SC_SLOTS):
        copy(s, s).start()

    def job(j, c):
        s = j % SC_SLOTS
        copy(j, s).wait()

        def head(i, cc):
            compute(j // per_tok, (j % per_tok) * SC_JOB_HEADS + i, s, i * PEER_TOPK)
            return cc
        lax.fori_loop(0, SC_JOB_HEADS, head, 0)

        @pl.when(j + SC_SLOTS < njobs)
        def _next():
            copy(j + SC_SLOTS, s).start()
        return c

    lax.fori_loop(0, njobs, job, 0)


def _peer_u_body(n_tok, idx_hbm, h2_hbm, u_hbm, pre_hbm, idx_v, h2_v, pre_v, ubuf, acc_v, sem):
    base = _sc_worker() * n_tok
    lane = lax.iota(I32, SC_LANES)

    def compute(tt, h, slot, r0):
        def chunk(cg, accs):
            cs = [pl.ds((cg * SC_BF16_GROUP + i) * SC_LANES, SC_LANES) for i in range(SC_BF16_GROUP)]
            xs = [plsc.bitcast(h2_v[tt, c], BF16) for c in cs]
            out = []
            for k, a in enumerate(accs):
                part = _tree_sum([plsc.bitcast(ubuf[slot, r0 + k, c], BF16) * x for c, x in zip(cs, xs)])
                lo, hi = _unpack_pair(plsc.bitcast(part, I32))
                out.append(a + (lo + hi))
            return tuple(out)
        zero = jnp.zeros((SC_LANES,), F32)
        accs = lax.fori_loop(0, SC_CHUNKS // SC_BF16_GROUP, chunk, (zero,) * PEER_TOPK)
        for k, a in enumerate(accs):
            acc_v[k, :] = a
        tot = zero
        for j in range(SC_LANES):
            tot = tot + plsc.load_gather(acc_v, [lane, (lane + j) & (SC_LANES - 1)])
        pre_v[tt, pl.ds(h * PEER_TOPK, PEER_TOPK)] = tot

    tb = idx_v.shape[0]

    def block(bi, c):
        t0 = base + bi * tb
        pltpu.sync_copy(idx_hbm.at[pl.ds(t0, tb)], idx_v)
        pltpu.sync_copy(h2_hbm.at[pl.ds(t0, tb)], h2_v)
        _sc_jobs(u_hbm, idx_v, ubuf, sem, compute)
        pltpu.sync_copy(pre_v, pre_hbm.at[pl.ds(t0, tb)])
        return c

    lax.fori_loop(0, n_tok // tb, block, 0)


def _peer_v_body(n_tok, idx_hbm, coef_hbm, v_hbm, out_hbm, idx_v, coef_v, out_v, vbuf, sem):
    base = _sc_worker() * n_tok
    zero = jnp.zeros((SC_LANES,), F32)

    def compute(tt, h, slot, r0):
        cvec = coef_v[tt, pl.ds(h * PEER_TOPK, PEER_TOPK)]
        cb = [plsc.bitcast(jnp.take_along_axis(cvec, jnp.full((SC_LANES,), k, I32), axis=0), BF16)
              for k in range(PEER_TOPK)]

        @plsc.parallel_loop(0, SC_CHUNKS, unroll=2)
        def _chunk(c):
            cs = pl.ds(c * SC_LANES, SC_LANES)
            prods = [plsc.bitcast(vbuf[slot, r0 + k, cs], BF16) * cb[k] for k in range(PEER_TOPK)]
            pairs = [_unpack_pair(plsc.bitcast(_tree_sum(prods[g:g + SC_BF16_GROUP]), I32))
                     for g in range(0, PEER_TOPK, SC_BF16_GROUP)]
            for half, off in ((0, 0), (1, PACK_HALF)):
                plsc.addupdate(out_v.at[tt, pl.ds(off + c * SC_LANES, SC_LANES)],
                               _tree_sum([p[half] for p in pairs]))

    tb = idx_v.shape[0]

    def block(bi, c):
        t0 = base + bi * tb
        pltpu.sync_copy(idx_hbm.at[pl.ds(t0, tb)], idx_v)
        pltpu.sync_copy(coef_hbm.at[pl.ds(t0, tb)], coef_v)

        def clear(i, cc):
            per_row = D_MODEL // SC_LANES
            out_v[i // per_row, pl.ds((i % per_row) * SC_LANES, SC_LANES)] = zero
            return cc
        lax.fori_loop(0, tb * (D_MODEL // SC_LANES), clear, 0)
        _sc_jobs(v_hbm, idx_v, vbuf, sem, compute)
        pltpu.sync_copy(out_v, out_hbm.at[pl.ds(t0, tb)])
        return c

    lax.fori_loop(0, n_tok // tb, block, 0)


def _peer_sc(body, idx, rows, table, out_width, name):
    t = idx.shape[0]
    assert t % SC_WORKERS == 0
    n_tok = t // SC_WORKERS
    tb = min(SC_TOKENS, n_tok)
    assert n_tok % tb == 0 and tb * PEER_HEADS // SC_JOB_HEADS >= SC_SLOTS
    return pl.kernel(
        functools.partial(body, n_tok),
        out_type=jax.ShapeDtypeStruct((t, out_width), F32),
        mesh=_sc_mesh(),
        scratch_types=[pltpu.VMEM((tb, PEER_HK), I32),
                       pltpu.VMEM((tb, rows.shape[1]), rows.dtype),
                       pltpu.VMEM((tb, out_width), F32),
                       pltpu.VMEM((SC_SLOTS, SC_JOB_HEADS * PEER_TOPK, PACK_HALF), I32)]
                      + ([pltpu.VMEM((PEER_TOPK, SC_LANES), F32)] if body is _peer_u_body else [])
                      + [pltpu.SemaphoreType.DMA((SC_SLOTS,))],
        compiler_params=pltpu.CompilerParams(needs_layout_passes=False),
        name=name,
    )(idx, rows, table)


def _coef_words(pre, gates):
    return _pack_words(*(gates * _gelu(pre),) * 2)


def _coef_body(pre_ref, gate_ref, coef_ref):
    coef_ref[...] = _coef_words(pre_ref[...], gate_ref[...])


def _coef(pre, gates, tm):
    t = pre.shape[0]
    row = pl.BlockSpec((tm, PEER_HK), lambda i: (i, 0))
    return pl.pallas_call(_coef_body, grid=(t // tm,), in_specs=[row, row], out_specs=row,
                          out_shape=jax.ShapeDtypeStruct((t, PEER_HK), I32), name="coef")(pre, gates)


def _final_body(x1_ref, peer_ref, g2_ref, fng_ref, y_ref):
    x2 = x1_ref[...] + _mod_rows(g2_ref) * peer_ref[...]
    y_ref[...] = x2 * lax.rsqrt(jnp.mean(x2 * x2, axis=-1, keepdims=True) + EPS) * fng_ref[...]


def _final(x1, peer_out, mod, rows_per_batch, final_g, tm):
    t = x1.shape[0]
    row = pl.BlockSpec((tm, D_MODEL), lambda i: (i, 0))
    return pl.pallas_call(
        _final_body, grid=(t // tm,),
        in_specs=[row, row, _mod_spec(5, rows_per_batch, tm), _const_spec((1, D_MODEL))],
        out_specs=row, out_shape=jax.ShapeDtypeStruct((t, D_MODEL), F32), name="final",
    )(x1, peer_out, mod, final_g.reshape(1, -1))


def _expert_gather_v(g, coef, expert_v):
    g["peer_out"] = _peer_sc(_peer_v_body, g["idx"], coef, expert_v, D_MODEL, "peer_v")


def _front(x, mod, conv_buf, s0, pool_buf, start, chunk, tm, wts, prev, fin):
    b, l, _ = x.shape
    t = b * l
    x2d = x.reshape(t, D_MODEL)
    if l >= tm:
        modx = mod.reshape(b, 6, 1, D_MODEL).transpose(1, 0, 2, 3)
    else:
        modx = jnp.repeat(mod.reshape(b, 6, D_MODEL), l, axis=0).transpose(1, 0, 2)
    outs = _inproj(x2d, modx, l, wts["norm1_g"], wts["w_cat"], tm)
    lp = -(-l // chunk) * chunk
    proj = {}
    for (name, w), a in zip(_IN_BLOCKS, outs):
        a = a.reshape(b, l, w)
        proj[name] = a if lp == l else jnp.pad(a, ((0, 0), (0, lp - l), (0, 0)))
    mixed, nconv, ns, npool = _mixer(proj, conv_buf, s0, pool_buf, start, l, chunk,
                                     wts["conv_w"], wts["a_log"], wts["dt_bias"], wts["dn_norm_g"],
                                     wts["w_pool"], wts["pool_scale"])
    mixed2d = mixed[:, :l].reshape(t, D_MODEL)
    res = _post(mixed2d, x2d, modx, l, wts["norm2_g"], wts["w_out"], wts["w_query"], wts["keys"], tm,
                prev=None if prev is None else (prev["pre"], prev["gates"]),
                fin=None if fin is None else (fin["x1"], fin["peer_out"], fin["mod"], fin["l"],
                                              wts["final_norm_g"]))
    x1, h2, idx, gates = res[:4]
    extra = list(res[4:])
    coef_prev = extra.pop(0) if prev is not None else None
    y_fin = extra.pop(0).reshape(fin["b"], fin["l"], D_MODEL) if fin is not None else None
    pre = _peer_sc(_peer_u_body, idx, h2, wts["expert_u"], PEER_HK, "peer_u")
    g = dict(x1=x1, idx=idx, gates=gates, pre=pre, mod=modx, b=b, l=l, tm=tm,
             states=(nconv, ns, npool))
    return g, coef_prev, y_fin


def kernel(x_prompt, x_sample, c_prompt, c_sample, state_conv, state_delta, state_pool, w_ada, b_ada, norm1_g, w_in, conv_w, a_log, dt_bias, dn_norm_g, w_pool, pool_scale, w_out, norm2_g, w_query, sub_keys, expert_u, expert_v, final_norm_g):
    bp = x_prompt.shape[0]
    bs = x_sample.shape[0]
    yp, ys = x_prompt, x_sample
    conv_p, delta_p, pool_p, conv_s, delta_s, pool_s = [], [], [], [], [], []
    zero_conv = jnp.zeros((bp, CONV_WIDTH - 1, QKV_WIDTH), F32)
    zero_delta = jnp.zeros((bp, DN_HEADS, DN_HEAD_DIM, DN_HEAD_DIM), F32)
    zero_pool = jnp.zeros((bp, POOL_BUF, POOL_WIDTH), F32)
    c_all = jnp.concatenate([c_prompt, c_sample], axis=0)
    for layer in range(DEPTH):
        wi = w_in[layer]
        o_b = QKV_WIDTH
        o_z = o_b + 2 * DN_HEADS
        w_ba = jnp.pad(wi[:, o_b:o_z], ((0, 0), (0, LANES - 2 * DN_HEADS)))
        w_cat = jnp.concatenate([wi[:, :o_b], wi[:, o_z:], w_ba], axis=1).astype(BF16)
        last = layer == DEPTH - 1
        wts = dict(
            norm1_g=norm1_g[layer], w_cat=w_cat, conv_w=conv_w[layer], a_log=a_log[layer],
            dt_bias=dt_bias[layer], dn_norm_g=dn_norm_g[layer], w_pool=w_pool[layer],
            pool_scale=pool_scale[layer], w_out=w_out[layer].astype(BF16), norm2_g=norm2_g[layer],
            w_query=w_query[layer].astype(BF16),
            keys=sub_keys[layer].reshape(2 * PEER_HEADS, PEER_NKEYS, PEER_KEY_HALF).astype(BF16),
            expert_u=_pack_table(expert_u[layer]), expert_v=_pack_table(expert_v[layer]),
            final_norm_g=final_norm_g if last else jnp.ones_like(final_norm_g))
        mod = _ada(c_all, w_ada[layer], b_ada[layer])
        assert last, "final norm is fused into the expert stage"
        step = bp // PROMPT_PARTS
        seq = x_prompt.shape[1]
        zeros = (zero_conv[:step], zero_delta[:step], zero_pool[:step])
        jobs, cuts = [], []
        for b0 in range(0, bp, step):
            n = EDGE_SPLITS if b0 in (0, bp - step) else 1
            cuts.append(n)
            for s0 in range(0, seq, seq // n):
                jobs.append((yp[b0:b0 + step, s0:s0 + seq // n], mod[b0:b0 + step],
                             zeros if s0 == 0 else None, s0, DN_CHUNK))
        jobs.append((ys, mod[bp:], (state_conv[layer], state_delta[layer], state_pool[layer]),
                     PAST_LEN, SUBLANES))
        groups = []
        for j, (xg, mg, states, start, chunk) in enumerate(jobs):
            prev = groups[j - 1] if j >= 1 else None
            fin = groups[j - FIN_LAG] if j >= FIN_LAG else None
            if fin is not None and fin["x1"].shape[0] % (xg.shape[0] * xg.shape[1] // ROW_TILE):
                fin = None
            if states is None:
                states = prev["states"]
            g, coef_prev, y_fin = _front(xg, mg, *states, start, chunk, ROW_TILE, wts, prev, fin)
            if prev is not None:
                _expert_gather_v(prev, coef_prev, wts["expert_v"])
            if fin is not None:
                fin["y"] = y_fin
            groups.append(g)
        _expert_gather_v(groups[-1], _coef(groups[-1]["pre"], groups[-1]["gates"], ROW_TILE),
                         wts["expert_v"])
        for g in groups:
            if "y" not in g:
                g["y"] = _final(g["x1"], g["peer_out"], g["mod"], g["l"], wts["final_norm_g"],
                                g["tm"]).reshape(g["b"], g["l"], D_MODEL)
        rows, at = [], 0
        for n in cuts:
            rows.append(groups[at:at + n])
            at += n
        yp =jnp.concatenate([jnp.concatenate([g["y"] for g in row], axis=1) for row in rows], axis=0)
        cp, sp, pp = (jnp.concatenate(a, axis=0) for a in zip(*(row[-1]["states"] for row in rows)))
        ys = groups[-1]["y"]
        cs, ss, ps = groups[-1]["states"]
        conv_p.append(cp)
        delta_p.append(sp)
        pool_p.append(pp)
        conv_s.append(cs)
        delta_s.append(ss)
        pool_s.append(ps)
    return (yp, ys, jnp.stack(conv_p), jnp.stack(delta_p), jnp.stack(pool_p),
            jnp.stack(conv_s), jnp.stack(delta_s), jnp.stack(pool_s))
```

```python
import functools

import jax
import jax.numpy as jnp
from jax import lax
from jax.experimental import pallas as pl
from jax.experimental.pallas import tpu as pltpu
from jax.experimental.pallas import tpu_sc as plsc

F32 = jnp.float32
BF16 = jnp.bfloat16
I32 = jnp.int32

D_MODEL = 1024
DEPTH = 1
PAST_LEN = 16384
DN_HEADS = 8
DN_HEAD_DIM = 128
DN_WIDTH = DN_HEADS * DN_HEAD_DIM
QKV_WIDTH = 3 * DN_WIDTH
CONV_WIDTH = 4
DN_CHUNK = 64
POOL_WINDOWS = (2, 4, 8, 16)
POOL_GROUP_DIM = 128
POOL_WIDTH = len(POOL_WINDOWS) * POOL_GROUP_DIM
POOL_OUT_GROUP = D_MODEL // len(POOL_WINDOWS)
POOL_BUF = max(POOL_WINDOWS) - 1
PEER_HEADS = 8
PEER_NKEYS = 128
PEER_TOPK = 16
PEER_KEY_HALF = 128
PEER_HK = PEER_HEADS * PEER_TOPK
EPS = 1e-6

LANES = 128
SUBLANES = 8
CONV_PAD = SUBLANES
POOL_PAD = 16
VMEM_LIMIT = 56 * 1024 * 1024

NT_DIMS = (((1,), (1,)), ((), ()))
TN_DIMS = (((0,), (0,)), ((), ()))


def _dot(a, b):
    return jnp.dot(a.astype(BF16), b.astype(BF16), preferred_element_type=F32)


def _dot_nt(a, b):
    return lax.dot_general(a.astype(BF16), b.astype(BF16), NT_DIMS, preferred_element_type=F32)


def _split3(x):
    hi = x.astype(BF16)
    r1 = x - hi.astype(F32)
    mid = r1.astype(BF16)
    lo = (r1 - mid.astype(F32)).astype(BF16)
    return hi, mid, lo


def _silu(x):
    return x * jax.nn.sigmoid(x)


def _gelu(x):
    return 0.5 * x * (1.0 + lax.erf(x * (0.5 ** 0.5)))


def _softplus(x):
    return jnp.maximum(x, 0.0) + jnp.log(1.0 + jnp.exp(-jnp.abs(x)))


def _mod_rows(ref):
    m = ref[...]
    return m.reshape(m.shape[-2], m.shape[-1])


def _mod_spec(k, rows_per_batch, tm):
    if rows_per_batch >= tm:
        tiles = rows_per_batch // tm
        return pl.BlockSpec((1, 1, 1, D_MODEL), lambda i, *_: (k, i // tiles, 0, 0))
    return pl.BlockSpec((1, tm, D_MODEL), lambda i, *_: (k, i, 0))


def _const_spec(shape):
    nd = len(shape)
    return pl.BlockSpec(shape, lambda *_: (0,) * nd)


def _ada_body(c_ref, w_ref, b_ref, o_ref):
    o_ref[...] = _dot(_silu(c_ref[...]), w_ref[...]) + b_ref[...]


def _ada(c, w_ada, b_ada):
    n = c.shape[0]
    return pl.pallas_call(
        _ada_body,
        grid=(6,),
        in_specs=[pl.BlockSpec((n, D_MODEL), lambda j: (0, 0)),
                  pl.BlockSpec((D_MODEL, D_MODEL), lambda j: (0, j)),
                  pl.BlockSpec((1, D_MODEL), lambda j: (0, j))],
        out_specs=pl.BlockSpec((n, D_MODEL), lambda j: (0, j)),
        out_shape=jax.ShapeDtypeStruct((n, 6 * D_MODEL), F32),
        name="ada",
    )(c, w_ada, b_ada.reshape(1, -1))


_IN_BLOCKS = (("qkv", QKV_WIDTH), ("z", DN_WIDTH), ("pool", POOL_WIDTH),
              ("ga", D_MODEL), ("gb", D_MODEL), ("ba", LANES))
_IN_TOTAL = sum(w for _, w in _IN_BLOCKS)
_IN_COL_CHUNK = 512


def _inproj_body(x_ref, sc_ref, sh_ref, g_ref, w_ref, *out_refs):
    x = x_ref[...]
    y = x * lax.rsqrt(jnp.mean(x * x, axis=-1, keepdims=True) + EPS) * g_ref[...]
    h = (y * (1.0 + _mod_rows(sc_ref)) + _mod_rows(sh_ref)).astype(BF16)
    off = 0
    for (_, width), o_ref in zip(_IN_BLOCKS, out_refs):
        for c0 in range(0, width, _IN_COL_CHUNK):
            cw = min(_IN_COL_CHUNK, width - c0)
            o_ref[:, c0:c0 + cw] = jnp.dot(h, w_ref[:, off + c0:off + c0 + cw],
                                           preferred_element_type=F32)
        off += width


def _inproj(x2d, mod, rows_per_batch, norm_g, w_cat, tm):
    t = x2d.shape[0]
    row = lambda w: pl.BlockSpec((tm, w), lambda i: (i, 0))
    return pl.pallas_call(
        _inproj_body,
        grid=(t // tm,),
        in_specs=[row(D_MODEL), _mod_spec(1, rows_per_batch, tm), _mod_spec(0, rows_per_batch, tm),
                  _const_spec((1, D_MODEL)),
                  pl.BlockSpec((D_MODEL, _IN_TOTAL), lambda i: (0, 0), pipeline_mode=pl.Buffered(1))],
        out_specs=[row(w) for _, w in _IN_BLOCKS],
        out_shape=[jax.ShapeDtypeStruct((t, w), F32) for _, w in _IN_BLOCKS],
        compiler_params=pltpu.CompilerParams(vmem_limit_bytes=VMEM_LIMIT),
        name="inproj",
    )(x2d, mod, mod, norm_g.reshape(1, -1), w_cat)


def _mixer_body(C, Lv, start,
                qkv_ref, ba_ref, z_ref, pin_ref, ga_ref, gb_ref, cbuf_ref, s0_ref, pbuf_ref,
                convw_ref, alog_ref, dtb_ref, dng_ref, wpool_ref, pscale_ref,
                mixed_ref, nconv_ref, ns_ref, npool_ref,
                xp_scr, act_scr, s_scr, pp_scr, odn_scr):
    n = pl.program_id(1)
    last = pl.num_programs(1) - 1

    @pl.when(n == 0)
    def _load_state():
        xp_scr[0:CONV_PAD, :] = cbuf_ref[0]
        pp_scr[0:POOL_PAD, :] = pbuf_ref[0]
        s_scr[...] = s0_ref[0]

    xp_scr[CONV_PAD:CONV_PAD + C, :] = qkv_ref[0]
    for c0 in range(0, QKV_WIDTH, 512):
        cs = slice(c0, c0 + 512)
        y = xp_scr[CONV_PAD:CONV_PAD + C, cs] * convw_ref[CONV_WIDTH - 1:CONV_WIDTH, cs]
        for k in range(CONV_WIDTH - 1):
            r0 = CONV_PAD - (CONV_WIDTH - 1) + k
            y = y + xp_scr[r0:r0 + C, cs] * convw_ref[k:k + 1, cs]
        act_scr[:, cs] = _silu(y)

    ba = ba_ref[0]
    lane = lax.broadcasted_iota(I32, (C, LANES), 1)
    beta_all = jax.nn.sigmoid(ba)
    g_all = -jnp.exp(alog_ref[...]) * _softplus(ba + dtb_ref[...])
    if Lv < C:
        valid = lax.broadcasted_iota(I32, (C, LANES), 0) < Lv
        beta_all = jnp.where(valid, beta_all, 0.0)
        g_all = jnp.where(valid, g_all, 0.0)
    ii = lax.broadcasted_iota(I32, (C, C), 0)
    jj = lax.broadcasted_iota(I32, (C, C), 1)
    causal = ii >= jj
    strict = ii > jj
    tril = jnp.where(causal, 1.0, 0.0).astype(BF16)
    eye = jnp.where(ii == jj, 1.0, 0.0)
    gc_all = sum(jnp.dot(tril, part, preferred_element_type=F32) for part in _split3(g_all))
    if C < LANES:
        gc_sq = jnp.concatenate([gc_all, jnp.zeros((LANES - C, LANES), F32)], axis=0)
    else:
        gc_sq = gc_all
    gc_t = gc_sq.T

    H = range(DN_HEADS)
    hsl = [slice(h * DN_HEAD_DIM, (h + 1) * DN_HEAD_DIM) for h in H]
    beta = [jnp.sum(jnp.where(lane == h, beta_all, 0.0), axis=1, keepdims=True) for h in H]
    gcol = [jnp.sum(jnp.where(lane == DN_HEADS + h, gc_all, 0.0), axis=1, keepdims=True) for h in H]
    grow = [gc_t[DN_HEADS + h:DN_HEADS + h + 1, 0:C] for h in H]
    glast = [g[C - 1:C, :] for g in gcol]
    q = [act_scr[:, hsl[h]] for h in H]
    k = [act_scr[:, DN_WIDTH + h * DN_HEAD_DIM:DN_WIDTH + (h + 1) * DN_HEAD_DIM] for h in H]
    v = [act_scr[:, 2 * DN_WIDTH + h * DN_HEAD_DIM:2 * DN_WIDTH + (h + 1) * DN_HEAD_DIM] for h in H]
    q = [x * lax.rsqrt(jnp.sum(x * x, axis=-1, keepdims=True) + EPS) * (DN_HEAD_DIM ** -0.5) for x in q]
    k = [x * lax.rsqrt(jnp.sum(x * x, axis=-1, keepdims=True) + EPS) for x in k]
    kb = [k[h] * beta[h] for h in H]
    vb = [v[h] * beta[h] for h in H]
    decay = [jnp.where(causal, jnp.exp(jnp.where(causal, gcol[h] - grow[h], 0.0)), 0.0) for h in H]
    lower = [jnp.where(strict, _dot_nt(kb[h], k[h]) * decay[h], 0.0) for h in H]
    ainv = [eye - x for x in lower]
    pw = lower
    p = 1
    while 2 * p < C:
        pw = [_dot(x, x) for x in pw]
        ainv = [ainv[h] + _dot(ainv[h], pw[h]) for h in H]
        p *= 2
    sol = [_dot(ainv[h], jnp.concatenate([vb[h], kb[h] * jnp.exp(gcol[h])], axis=1)) for h in H]
    qk = [_dot_nt(q[h], k[h]) * decay[h] for h in H]
    k_tail = [k[h] * jnp.exp(glast[h] - gcol[h]) for h in H]
    S = [s_scr[h] for h in H]
    v_new = [sol[h][:, :DN_HEAD_DIM] - _dot(sol[h][:, DN_HEAD_DIM:], S[h]) for h in H]
    o = [_dot(q[h] * jnp.exp(gcol[h]), S[h]) + _dot(qk[h], v_new[h]) for h in H]
    for h in H:
        s_scr[h] = S[h] * jnp.exp(glast[h]) + lax.dot_general(
            k_tail[h].astype(BF16), v_new[h].astype(BF16), TN_DIMS, preferred_element_type=F32)
    for h in H:
        zf = z_ref[0, :, hsl[h]]
        odn_scr[:, hsl[h]] = (o[h] * lax.rsqrt(jnp.mean(o[h] * o[h], axis=-1, keepdims=True) + EPS)
                              * dng_ref[...] * _silu(zf))

    pp_scr[POOL_PAD:POOL_PAD + C, :] = pin_ref[0]
    pos = start + n * C + lax.broadcasted_iota(I32, (C, 1), 0)
    for gi, win in enumerate(POOL_WINDOWS):
        gs = slice(gi * POOL_GROUP_DIM, (gi + 1) * POOL_GROUP_DIM)
        xg = pp_scr[POOL_PAD:POOL_PAD + C, gs]
        ssum = xg
        for sft in range(1, win):
            ssum = ssum + pp_scr[POOL_PAD - sft:POOL_PAD - sft + C, gs]
        cnt = jnp.minimum(pos + 1, win).astype(F32)
        pooled = ssum / cnt - xg
        os_ = slice(gi * POOL_OUT_GROUP, (gi + 1) * POOL_OUT_GROUP)
        yp = _dot(pooled, wpool_ref[gi]) * pscale_ref[:, os_]
        mixed_ref[0, :, os_] = (jax.nn.sigmoid(ga_ref[0, :, os_]) * odn_scr[:, os_]
                                + jax.nn.sigmoid(gb_ref[0, :, os_]) * yp)

    @pl.when(n == last)
    def _store_state():
        nconv_ref[0] = xp_scr[Lv + CONV_PAD - (CONV_WIDTH - 1):Lv + CONV_PAD, :]
        npool_ref[0] = pp_scr[Lv + POOL_PAD - POOL_BUF:Lv + POOL_PAD, :]
        ns_ref[0] = s_scr[...]

    xp_scr[0:CONV_PAD, :] = xp_scr[C:C + CONV_PAD, :]
    pp_scr[0:POOL_PAD, :] = pp_scr[C:C + POOL_PAD, :]


def _mixer(proj, conv_buf, s0, pool_buf, start, seq_len, C,
           conv_w, a_log, dt_bias, dn_norm_g, w_pool, pool_scale):
    b, lp, _ = proj["qkv"].shape
    nchunks = lp // C
    lv = seq_len - (nchunks - 1) * C
    cbuf = jnp.pad(conv_buf, ((0, 0), (CONV_PAD - (CONV_WIDTH - 1), 0), (0, 0)))
    pbuf = jnp.pad(pool_buf, ((0, 0), (POOL_PAD - POOL_BUF, 0), (0, 0)))
    lane_pad = lambda a: jnp.pad(a.reshape(1, -1), ((0, 0), (DN_HEADS, LANES - 2 * DN_HEADS)))
    chunk = lambda w: pl.BlockSpec((1, C, w), lambda i, j: (i, j, 0))
    state = lambda *s: pl.BlockSpec((1,) + s, lambda i, j: (i,) + (0,) * len(s))
    return pl.pallas_call(
        functools.partial(_mixer_body, C, lv, start),
        grid=(b, nchunks),
        in_specs=[chunk(QKV_WIDTH), chunk(LANES), chunk(DN_WIDTH), chunk(POOL_WIDTH),
                  chunk(D_MODEL), chunk(D_MODEL),
                  state(CONV_PAD, QKV_WIDTH), state(DN_HEADS, DN_HEAD_DIM, DN_HEAD_DIM),
                  state(POOL_PAD, POOL_WIDTH),
                  _const_spec((CONV_WIDTH, QKV_WIDTH)), _const_spec((1, LANES)), _const_spec((1, LANES)),
                  _const_spec((1, DN_HEAD_DIM)),
                  _const_spec((len(POOL_WINDOWS), POOL_GROUP_DIM, POOL_OUT_GROUP)),
                  _const_spec((1, D_MODEL))],
        out_specs=[chunk(D_MODEL), state(CONV_WIDTH - 1, QKV_WIDTH),
                   state(DN_HEADS, DN_HEAD_DIM, DN_HEAD_DIM), state(POOL_BUF, POOL_WIDTH)],
        out_shape=[jax.ShapeDtypeStruct((b, lp, D_MODEL), F32),
                   jax.ShapeDtypeStruct((b, CONV_WIDTH - 1, QKV_WIDTH), F32),
                   jax.ShapeDtypeStruct((b, DN_HEADS, DN_HEAD_DIM, DN_HEAD_DIM), F32),
                   jax.ShapeDtypeStruct((b, POOL_BUF, POOL_WIDTH), F32)],
        scratch_shapes=[pltpu.VMEM((CONV_PAD + C + CONV_PAD, QKV_WIDTH), F32),
                        pltpu.VMEM((C, QKV_WIDTH), F32),
                        pltpu.VMEM((DN_HEADS, DN_HEAD_DIM, DN_HEAD_DIM), F32),
                        pltpu.VMEM((POOL_PAD + C + POOL_PAD, POOL_WIDTH), F32),
                        pltpu.VMEM((C, DN_WIDTH), F32)],
        compiler_params=pltpu.CompilerParams(dimension_semantics=("arbitrary", "arbitrary"),
                                             vmem_limit_bytes=VMEM_LIMIT),
        name="mixer",
    )(proj["qkv"], proj["ba"], proj["z"], proj["pool"], proj["ga"], proj["gb"], cbuf, s0, pbuf,
      conv_w, lane_pad(a_log), lane_pad(dt_bias), dn_norm_g.reshape(1, -1), w_pool,
      pool_scale.reshape(1, -1))


def _top16(s, ids, payload=None):
    big = float(2 ** 24)
    vals, sel, pays = [], [], []
    for _ in range(PEER_TOPK):
        m = jnp.max(s, axis=0, keepdims=True)
        am = jnp.min(jnp.where(s == m, ids, big), axis=0, keepdims=True)
        hit = ids == am
        if payload is not None:
            pays.append(jnp.max(jnp.where(hit, payload, -1.0), axis=0, keepdims=True))
        s = jnp.where(hit, -jnp.inf, s)
        vals.append(m)
        sel.append(am)
    out = (jnp.concatenate(vals, axis=0), jnp.concatenate(sel, axis=0))
    if payload is not None:
        out += (jnp.concatenate(pays, axis=0),)
    return out


_CAND_EDGE = 4


def _post_body(has_prev, has_fin, mixed_ref, x_ref, g1_ref, sc2_ref, sh2_ref, n2g_ref, wout_ref,
               wq_ref, keys_ref, *refs):
    refs = list(refs)
    prev_in = [refs.pop(0) for _ in range(2 if has_prev else 0)]
    fin_in = [refs.pop(0) for _ in range(4 if has_fin else 0)]
    x1_ref, h2_ref, idx_ref, gate_ref = refs[:4]
    extra_out = refs[4:]
    if has_prev:
        pre_ref, pgate_ref = prev_in
        extra_out.pop(0)[...] = _coef_words(pre_ref[...], pgate_ref[...])
    if has_fin:
        _final_body(*fin_in, extra_out.pop(0))
    tm = x_ref.shape[0]
    x1 = x_ref[...] + _mod_rows(g1_ref) * _dot(mixed_ref[...], wout_ref[...])
    x1_ref[...] = x1
    y = x1 * lax.rsqrt(jnp.mean(x1 * x1, axis=-1, keepdims=True) + EPS) * n2g_ref[...]
    h2 = y * (1.0 + _mod_rows(sc2_ref)) + _mod_rows(sh2_ref)
    h2_ref[...] = _pack_words(h2[:, :PACK_HALF], h2[:, PACK_HALF:])
    q = _dot(h2, wq_ref[...])

    K = PEER_TOPK
    key_id = lax.broadcasted_iota(I32, (PEER_NKEYS, 1), 0).astype(F32)
    r16 = lax.broadcasted_iota(I32, (K, 1), 0)
    cand_id = jnp.concatenate([(a * K + r16) for a in range(_CAND_EDGE)]
                              + [(r16 * K + b) for b in range(_CAND_EDGE)], axis=0).astype(F32)
    dup = r16 < _CAND_EDGE
    idx_rows, gate_rows = [], []
    for h in range(PEER_HEADS):
        half = []
        for p in range(2):
            c0 = (h * 2 + p) * PEER_KEY_HALF
            st = _dot_nt(keys_ref[h * 2 + p], q[:, c0:c0 + PEER_KEY_HALF])
            half.append(_top16(st, key_id))
        (s1, i1), (s2, i2) = half
        cand = jnp.concatenate(
            [s1[a:a + 1] + s2 for a in range(_CAND_EDGE)]
            + [jnp.where(dup, -jnp.inf, s1 + s2[b:b + 1]) for b in range(_CAND_EDGE)], axis=0)
        cidx = jnp.concatenate(
            [i1[a:a + 1] * PEER_NKEYS + i2 for a in range(_CAND_EDGE)]
            + [i1 * PEER_NKEYS + i2[b:b + 1] for b in range(_CAND_EDGE)], axis=0)
        best, _, eidx = _top16(cand, cand_id, cidx)
        e = jnp.exp(best - best[0:1])
        gate_rows.append(e / jnp.sum(e, axis=0, keepdims=True))
        idx_rows.append(eidx)
    idx_ref[...] = jnp.concatenate(idx_rows, axis=0).T.astype(I32)
    gate_ref[...] = jnp.concatenate(gate_rows, axis=0).T


def _post(mixed2d, x2d, mod, rows_per_batch, norm2_g, w_out, w_query, keys, tm, prev=None, fin=None):
    t = x2d.shape[0]
    steps = t // tm
    row = lambda w: pl.BlockSpec((tm, w), lambda i: (i, 0))
    in_specs = [row(D_MODEL), row(D_MODEL),
                _mod_spec(2, rows_per_batch, tm), _mod_spec(4, rows_per_batch, tm),
                _mod_spec(3, rows_per_batch, tm), _const_spec((1, D_MODEL)),
                _const_spec((D_MODEL, D_MODEL)), _const_spec((D_MODEL, 2 * PEER_HEADS * PEER_KEY_HALF)),
                _const_spec((2 * PEER_HEADS, PEER_NKEYS, PEER_KEY_HALF))]
    out_specs = [row(D_MODEL), row(PACK_HALF), row(PEER_HK), row(PEER_HK)]
    out_shape = [jax.ShapeDtypeStruct((t, D_MODEL), F32), jax.ShapeDtypeStruct((t, PACK_HALF), I32),
                 jax.ShapeDtypeStruct((t, PEER_HK), I32), jax.ShapeDtypeStruct((t, PEER_HK), F32)]
    args = [mixed2d, x2d, mod, mod, mod, norm2_g.reshape(1, -1), w_out, w_query, keys]
    if prev is not None:
        tp = prev[0].shape[0]
        prow = pl.BlockSpec((tp // steps, PEER_HK), lambda i: (i, 0))
        in_specs += [prow, prow]
        out_specs += [prow]
        out_shape += [jax.ShapeDtypeStruct((tp, PEER_HK), I32)]
        args += list(prev)
    if fin is not None:
        x1_f, peer_f, mod_f, rows_f, final_g = fin
        tf = x1_f.shape[0]
        frow = pl.BlockSpec((tf // steps, D_MODEL), lambda i: (i, 0))
        in_specs += [frow, frow, _mod_spec(5, rows_f, tf // steps), _const_spec((1, D_MODEL))]
        out_specs += [frow]
        out_shape += [jax.ShapeDtypeStruct((tf, D_MODEL), F32)]
        args += [x1_f, peer_f, mod_f, final_g.reshape(1, -1)]
    return pl.pallas_call(
        functools.partial(_post_body, prev is not None, fin is not None),
        grid=(steps,),
        in_specs=in_specs, out_specs=out_specs, out_shape=out_shape,
        compiler_params=pltpu.CompilerParams(vmem_limit_bytes=VMEM_LIMIT),
        name="post",
    )(*args)


SC_CORES = 2
SC_SUBCORES = 16
SC_LANES = 16
SC_WORKERS = SC_CORES * SC_SUBCORES
SC_TOKENS = 16
SC_SLOTS = 4
SC_JOB_HEADS = 2
SC_BF16_GROUP = 4
PACK_HALF = D_MODEL // 2
SC_CHUNKS = PACK_HALF // SC_LANES
PROMPT_PARTS = 8
EDGE_SPLITS = 2
FIN_LAG = 3
ROW_TILE = 256


def _bf16_bits(v):
    return lax.bitcast_convert_type(v.astype(BF16).astype(F32), jnp.uint32)


def _pack_words(lo, hi):
    return lax.bitcast_convert_type((_bf16_bits(lo) >> 16) | _bf16_bits(hi), I32)


def _pack_body(x_ref, o_ref):
    o_ref[...] = _pack_words(x_ref[:, :PACK_HALF], x_ref[:, PACK_HALF:])


def _pack_table(tbl, rows=512):
    e = tbl.shape[0]
    return pl.pallas_call(
        _pack_body, grid=(e // rows,),
        in_specs=[pl.BlockSpec((rows, D_MODEL), lambda i: (i, 0))],
        out_specs=pl.BlockSpec((rows, PACK_HALF), lambda i: (i, 0)),
        out_shape=jax.ShapeDtypeStruct((e, PACK_HALF), I32), name="pack_table")(tbl)


def _tree_sum(terms):
    terms = list(terms)
    while len(terms) > 1:
        terms = [a + b for a, b in zip(terms[0::2], terms[1::2])] + terms[len(terms) & ~1:]
    return terms[0]


def _unpack_pair(w):
    lo = plsc.bitcast(lax.shift_left(w, jnp.full(w.shape, 16, I32)), F32)
    hi = plsc.bitcast(w & jnp.full(w.shape, -65536, I32), F32)
    return lo, hi


def _sc_mesh():
    return plsc.VectorSubcoreMesh(core_axis_name="c", subcore_axis_name="s")


def _sc_worker():
    return lax.axis_index("s") * SC_CORES + lax.axis_index("c")


def _sc_jobs(table_hbm, idx_v, buf, sem, compute):
    per_tok = PEER_HEADS // SC_JOB_HEADS
    njobs = idx_v.shape[0] * per_tok
    nrows = SC_JOB_HEADS * PEER_TOPK

    def copy(j, slot):
        rows = idx_v.at[j // per_tok, pl.ds((j % per_tok) * nrows, nrows)]
        return pltpu.make_async_copy(table_hbm.at[rows], buf.at[slot], sem.at[slot])

    for s in range(SC_SLOTS):
        copy(s, s).start()

    def job(j, c):
        s = j % SC_SLOTS
        copy(j, s).wait()

        def head(i, cc):
            compute(j // per_tok, (j % per_tok) * SC_JOB_HEADS + i, s, i * PEER_TOPK)
            return cc
        lax.fori_loop(0, SC_JOB_HEADS, head, 0)

        @pl.when(j + SC_SLOTS < njobs)
        def _next():
            copy(j + SC_SLOTS, s).start()
        return c

    lax.fori_loop(0, njobs, job, 0)


def _peer_u_body(n_tok, idx_hbm, h2_hbm, u_hbm, pre_hbm, idx_v, h2_v, pre_v, ubuf, acc_v, sem):
    base = _sc_worker() * n_tok
    lane = lax.iota(I32, SC_LANES)

    def compute(tt, h, slot, r0):
        def chunk(cg, accs):
            cs = [pl.ds((cg * SC_BF16_GROUP + i) * SC_LANES, SC_LANES) for i in range(SC_BF16_GROUP)]
            xs = [plsc.bitcast(h2_v[tt, c], BF16) for c in cs]
            out = []
            for k, a in enumerate(accs):
                part = _tree_sum([plsc.bitcast(ubuf[slot, r0 + k, c], BF16) * x for c, x in zip(cs, xs)])
                lo, hi = _unpack_pair(plsc.bitcast(part, I32))
                out.append(a + (lo + hi))
            return tuple(out)
        zero = jnp.zeros((SC_LANES,), F32)
        accs = lax.fori_loop(0, SC_CHUNKS // SC_BF16_GROUP, chunk, (zero,) * PEER_TOPK)
        for k, a in enumerate(accs):
            acc_v[k, :] = a
        tot = zero
        for j in range(SC_LANES):
            tot = tot + plsc.load_gather(acc_v, [lane, (lane + j) & (SC_LANES - 1)])
        pre_v[tt, pl.ds(h * PEER_TOPK, PEER_TOPK)] = tot

    tb = idx_v.shape[0]

    def block(bi, c):
        t0 = base + bi * tb
        pltpu.sync_copy(idx_hbm.at[pl.ds(t0, tb)], idx_v)
        pltpu.sync_copy(h2_hbm.at[pl.ds(t0, tb)], h2_v)
        _sc_jobs(u_hbm, idx_v, ubuf, sem, compute)
        pltpu.sync_copy(pre_v, pre_hbm.at[pl.ds(t0, tb)])
        return c

    lax.fori_loop(0, n_tok // tb, block, 0)


def _peer_v_body(n_tok, idx_hbm, coef_hbm, v_hbm, out_hbm, idx_v, coef_v, out_v, vbuf, sem):
    base = _sc_worker() * n_tok
    zero = jnp.zeros((SC_LANES,), F32)

    def compute(tt, h, slot, r0):
        cvec = coef_v[tt, pl.ds(h * PEER_TOPK, PEER_TOPK)]
        cb = [plsc.bitcast(jnp.take_along_axis(cvec, jnp.full((SC_LANES,), k, I32), axis=0), BF16)
              for k in range(PEER_TOPK)]

        @plsc.parallel_loop(0, SC_CHUNKS, unroll=2)
        def _chunk(c):
            cs = pl.ds(c * SC_LANES, SC_LANES)
            prods = [plsc.bitcast(vbuf[slot, r0 + k, cs], BF16) * cb[k] for k in range(PEER_TOPK)]
            pairs = [_unpack_pair(plsc.bitcast(_tree_sum(prods[g:g + SC_BF16_GROUP]), I32))
                     for g in range(0, PEER_TOPK, SC_BF16_GROUP)]
            for half, off in ((0, 0), (1, PACK_HALF)):
                plsc.addupdate(out_v.at[tt, pl.ds(off + c * SC_LANES, SC_LANES)],
                               _tree_sum([p[half] for p in pairs]))

    tb = idx_v.shape[0]

    def block(bi, c):
        t0 = base + bi * tb
        pltpu.sync_copy(idx_hbm.at[pl.ds(t0, tb)], idx_v)
        pltpu.sync_copy(coef_hbm.at[pl.ds(t0, tb)], coef_v)

        def clear(i, cc):
            per_row = D_MODEL // SC_LANES
            out_v[i // per_row, pl.ds((i % per_row) * SC_LANES, SC_LANES)] = zero
            return cc
        lax.fori_loop(0, tb * (D_MODEL // SC_LANES), clear, 0)
        _sc_jobs(v_hbm, idx_v, vbuf, sem, compute)
        pltpu.sync_copy(out_v, out_hbm.at[pl.ds(t0, tb)])
        return c

    lax.fori_loop(0, n_tok // tb, block, 0)


def _peer_sc(body, idx, rows, table, out_width, name):
    t = idx.shape[0]
    assert t % SC_WORKERS == 0
    n_tok = t // SC_WORKERS
    tb = min(SC_TOKENS, n_tok)
    assert n_tok % tb == 0 and tb * PEER_HEADS // SC_JOB_HEADS >= SC_SLOTS
    return pl.kernel(
        functools.partial(body, n_tok),
        out_type=jax.ShapeDtypeStruct((t, out_width), F32),
        mesh=_sc_mesh(),
        scratch_types=[pltpu.VMEM((tb, PEER_HK), I32),
                       pltpu.VMEM((tb, rows.shape[1]), rows.dtype),
                       pltpu.VMEM((tb, out_width), F32),
                       pltpu.VMEM((SC_SLOTS, SC_JOB_HEADS * PEER_TOPK, PACK_HALF), I32)]
                      + ([pltpu.VMEM((PEER_TOPK, SC_LANES), F32)] if body is _peer_u_body else [])
                      + [pltpu.SemaphoreType.DMA((SC_SLOTS,))],
        compiler_params=pltpu.CompilerParams(needs_layout_passes=False),
        name=name,
    )(idx, rows, table)


def _coef_words(pre, gates):
    return _pack_words(*(gates * _gelu(pre),) * 2)


def _coef_body(pre_ref, gate_ref, coef_ref):
    coef_ref[...] = _coef_words(pre_ref[...], gate_ref[...])


def _coef(pre, gates, tm):
    t = pre.shape[0]
    row = pl.BlockSpec((tm, PEER_HK), lambda i: (i, 0))
    return pl.pallas_call(_coef_body, grid=(t // tm,), in_specs=[row, row], out_specs=row,
                          out_shape=jax.ShapeDtypeStruct((t, PEER_HK), I32), name="coef")(pre, gates)


def _final_body(x1_ref, peer_ref, g2_ref, fng_ref, y_ref):
    x2 = x1_ref[...] + _mod_rows(g2_ref) * peer_ref[...]
    y_ref[...] = x2 * lax.rsqrt(jnp.mean(x2 * x2, axis=-1, keepdims=True) + EPS) * fng_ref[...]


def _final(x1, peer_out, mod, rows_per_batch, final_g, tm):
    t = x1.shape[0]
    row = pl.BlockSpec((tm, D_MODEL), lambda i: (i, 0))
    return pl.pallas_call(
        _final_body, grid=(t // tm,),
        in_specs=[row, row, _mod_spec(5, rows_per_batch, tm), _const_spec((1, D_MODEL))],
        out_specs=row, out_shape=jax.ShapeDtypeStruct((t, D_MODEL), F32), name="final",
    )(x1, peer_out, mod, final_g.reshape(1, -1))


FULL_SLICE_TOKENS = 2048
TC_V_TOKENS = 512
TC_V_BLOCK = 64
TC_V_UNROLL = 8
PACK_SUB = PACK_HALF // LANES


def _peer_v_tc_body(idx_ref, coef_ref, tbl_ref, out_ref):
    hi_mask = jnp.int32(-65536)

    def token(t, c):
        def experts(jb, accs):
            accs = list(accs)
            for jj in range(TC_V_UNROLL):
                j = jb * TC_V_UNROLL + jj
                w = tbl_ref[idx_ref[t, j]]
                cw = jnp.full(w.shape, coef_ref[t, j], I32)
                cf = lax.bitcast_convert_type(cw & hi_mask, F32)
                a = 2 * (jj % 2)
                accs[a] = accs[a] + cf * lax.bitcast_convert_type(w << 16, F32)
                accs[a + 1] = accs[a + 1] + cf * lax.bitcast_convert_type(w & hi_mask, F32)
            return tuple(accs)
        z = jnp.zeros((PACK_SUB, LANES), F32)
        a0, a1, a2, a3 = lax.fori_loop(0, PEER_HK // TC_V_UNROLL, experts, (z, z, z, z))
        out_ref[t] = jnp.concatenate([a0 + a2, a1 + a3], axis=0)
        return c

    lax.fori_loop(0, out_ref.shape[0], token, 0)


def _peer_v_tc(idx, coef, table3d):
    t = idx.shape[0]
    smem = pl.BlockSpec((TC_V_BLOCK, PEER_HK), lambda i: (i, 0), memory_space=pltpu.SMEM)
    return pl.pallas_call(
        _peer_v_tc_body, grid=(t // TC_V_BLOCK,),
        in_specs=[smem, smem, pl.BlockSpec(memory_space=pltpu.VMEM)],
        out_specs=pl.BlockSpec((TC_V_BLOCK, 2 * PACK_SUB, LANES), lambda i: (i, 0, 0)),
        out_shape=jax.ShapeDtypeStruct((t, 2 * PACK_SUB, LANES), F32),
        compiler_params=pltpu.CompilerParams(vmem_limit_bytes=VMEM_LIMIT),
        name="peer_v_tc",
    )(idx, coef, table3d).reshape(t, D_MODEL)


def _expert_gather_v(g, coef, expert_v, expert_v3d):
    t = g["idx"].shape[0]
    t_sc = t - TC_V_TOKENS if t == FULL_SLICE_TOKENS else t
    out = _peer_sc(_peer_v_body, g["idx"][:t_sc], coef[:t_sc], expert_v, D_MODEL, "peer_v")
    if t_sc < t:
        out = jnp.concatenate([out, _peer_v_tc(g["idx"][t_sc:], coef[t_sc:], expert_v3d)], axis=0)
    g["peer_out"] = out


def _front(x, mod, conv_buf, s0, pool_buf, start, chunk, tm, wts, prev, fin):
    b, l, _ = x.shape
    t = b * l
    x2d = x.reshape(t, D_MODEL)
    if l >= tm:
        modx = mod.reshape(b, 6, 1, D_MODEL).transpose(1, 0, 2, 3)
    else:
        modx = jnp.repeat(mod.reshape(b, 6, D_MODEL), l, axis=0).transpose(1, 0, 2)
    outs = _inproj(x2d, modx, l, wts["norm1_g"], wts["w_cat"], tm)
    lp = -(-l // chunk) * chunk
    proj = {}
    for (name, w), a in zip(_IN_BLOCKS, outs):
        a = a.reshape(b, l, w)
        proj[name] = a if lp == l else jnp.pad(a, ((0, 0), (0, lp - l), (0, 0)))
    mixed, nconv, ns, npool = _mixer(proj, conv_buf, s0, pool_buf, start, l, chunk,
                                     wts["conv_w"], wts["a_log"], wts["dt_bias"], wts["dn_norm_g"],
                                     wts["w_pool"], wts["pool_scale"])
    mixed2d = mixed[:, :l].reshape(t, D_MODEL)
    res = _post(mixed2d, x2d, modx, l, wts["norm2_g"], wts["w_out"], wts["w_query"], wts["keys"], tm,
                prev=None if prev is None else (prev["pre"], prev["gates"]),
                fin=None if fin is None else (fin["x1"], fin["peer_out"], fin["mod"], fin["l"],
                                              wts["final_norm_g"]))
    x1, h2, idx, gates = res[:4]
    extra = list(res[4:])
    coef_prev = extra.pop(0) if prev is not None else None
    y_fin = extra.pop(0).reshape(fin["b"], fin["l"], D_MODEL) if fin is not None else None
    pre = _peer_sc(_peer_u_body, idx, h2, wts["expert_u"], PEER_HK, "peer_u")
    g = dict(x1=x1, idx=idx, gates=gates, pre=pre, mod=modx, b=b, l=l, tm=tm,
             states=(nconv, ns, npool))
    return g, coef_prev, y_fin


def kernel(x_prompt, x_sample, c_prompt, c_sample, state_conv, state_delta, state_pool, w_ada, b_ada, norm1_g, w_in, conv_w, a_log, dt_bias, dn_norm_g, w_pool, pool_scale, w_out, norm2_g, w_query, sub_keys, expert_u, expert_v, final_norm_g):
    bp = x_prompt.shape[0]
    bs = x_sample.shape[0]
    yp, ys = x_prompt, x_sample
    conv_p, delta_p, pool_p, conv_s, delta_s, pool_s = [], [], [], [], [], []
    zero_conv = jnp.zeros((bp, CONV_WIDTH - 1, QKV_WIDTH), F32)
    zero_delta = jnp.zeros((bp, DN_HEADS, DN_HEAD_DIM, DN_HEAD_DIM), F32)
    zero_pool = jnp.zeros((bp, POOL_BUF, POOL_WIDTH), F32)
    c_all = jnp.concatenate([c_prompt, c_sample], axis=0)
    for layer in range(DEPTH):
        wi = w_in[layer]
        o_b = QKV_WIDTH
        o_z = o_b + 2 * DN_HEADS
        w_ba = jnp.pad(wi[:, o_b:o_z], ((0, 0), (0, LANES - 2 * DN_HEADS)))
        w_cat = jnp.concatenate([wi[:, :o_b], wi[:, o_z:], w_ba], axis=1).astype(BF16)
        last = layer == DEPTH - 1
        wts = dict(
            norm1_g=norm1_g[layer], w_cat=w_cat, conv_w=conv_w[layer], a_log=a_log[layer],
            dt_bias=dt_bias[layer], dn_norm_g=dn_norm_g[layer], w_pool=w_pool[layer],
            pool_scale=pool_scale[layer], w_out=w_out[layer].astype(BF16), norm2_g=norm2_g[layer],
            w_query=w_query[layer].astype(BF16),
            keys=sub_keys[layer].reshape(2 * PEER_HEADS, PEER_NKEYS, PEER_KEY_HALF).astype(BF16),
            expert_u=_pack_table(expert_u[layer]), expert_v=_pack_table(expert_v[layer]),
            final_norm_g=final_norm_g if last else jnp.ones_like(final_norm_g))
        wts["expert_v3d"] = wts["expert_v"].reshape(-1, PACK_SUB, LANES)
        mod = _ada(c_all, w_ada[layer], b_ada[layer])
        assert last, "final norm is fused into the expert stage"
        step = bp // PROMPT_PARTS
        seq = x_prompt.shape[1]
        zeros = (zero_conv[:step], zero_delta[:step], zero_pool[:step])
        jobs, cuts = [], []
        for b0 in range(0, bp, step):
            n = EDGE_SPLITS if b0 in (0, bp - step) else 1
            cuts.append(n)
            for s0 in range(0, seq, seq // n):
                jobs.append((yp[b0:b0 + step, s0:s0 + seq // n], mod[b0:b0 + step],
                             zeros if s0 == 0 else None, s0, DN_CHUNK))
        jobs.append((ys, mod[bp:], (state_conv[layer], state_delta[layer], state_pool[layer]),
                     PAST_LEN, SUBLANES))
        groups = []
        for j, (xg, mg, states, start, chunk) in enumerate(jobs):
            prev = groups[j - 1] if j >= 1 else None
            fin = groups[j - FIN_LAG] if j >= FIN_LAG else None
            if fin is not None and fin["x1"].shape[0] % (xg.shape[0] * xg.shape[1] // ROW_TILE):
                fin = None
            if states is None:
                states = prev["states"]
            g, coef_prev, y_fin = _front(xg, mg, *states, start, chunk, ROW_TILE, wts, prev, fin)
            if prev is not None:
                _expert_gather_v(prev, coef_prev, wts["expert_v"], wts["expert_v3d"])
            if fin is not None:
                fin["y"] = y_fin
            groups.append(g)
        _expert_gather_v(groups[-1], _coef(groups[-1]["pre"], groups[-1]["gates"], ROW_TILE),
                         wts["expert_v"], wts["expert_v3d"])
        for g in groups:
            if "y" not in g:
                g["y"] = _final(g["x1"], g["peer_out"], g["mod"], g["l"], wts["final_norm_g"],
                                g["tm"]).reshape(g["b"], g["l"], D_MODEL)
        rows, at = [], 0
        for n in cuts:
            rows.append(groups[at:at + n])
            at += n
        yp = jnp.concatenate([jnp.concatenate([g["y"] for g in row], axis=1) for row in rows], axis=0)
        cp, sp, pp = (jnp.concatenate(a, axis=0) for a in zip(*(row[-1]["states"] for row in rows)))
        ys = groups[-1]["y"]
        cs, ss, ps = groups[-1]["states"]
        conv_p.append(cp)
        delta_p.append(sp)
        pool_p.append(pp)
        conv_s.append(cs)
        delta_s.append(ss)
        pool_s.append(ps)
    return (yp, ys, jnp.stack(conv_p), jnp.stack(delta_p), jnp.stack(pool_p),
            jnp.stack(conv_s), jnp.stack(delta_s), jnp.stack(pool_s))
```

```python
import functools

import jax
import jax.numpy as jnp
from jax import lax
from jax.experimental import pallas as pl
from jax.experimental.pallas import tpu as pltpu
from jax.experimental.pallas import tpu_sc as plsc

F32 = jnp.float32
BF16 = jnp.bfloat16
I32 = jnp.int32

D_MODEL = 1024
DEPTH = 1
PAST_LEN = 16384
DN_HEADS = 8
DN_HEAD_DIM = 128
DN_WIDTH = DN_HEADS * DN_HEAD_DIM
QKV_WIDTH = 3 * DN_WIDTH
CONV_WIDTH = 4
DN_CHUNK = 64
POOL_WINDOWS = (2, 4, 8, 16)
POOL_GROUP_DIM = 128
POOL_WIDTH = len(POOL_WINDOWS) * POOL_GROUP_DIM
POOL_OUT_GROUP = D_MODEL // len(POOL_WINDOWS)
POOL_BUF = max(POOL_WINDOWS) - 1
PEER_HEADS = 8
PEER_NKEYS = 128
PEER_TOPK = 16
PEER_KEY_HALF = 128
PEER_HK = PEER_HEADS * PEER_TOPK
EPS = 1e-6

LANES = 128
SUBLANES = 8
CONV_PAD = SUBLANES
POOL_PAD = 16
VMEM_LIMIT = 56 * 1024 * 1024

NT_DIMS = (((1,), (1,)), ((), ()))
TN_DIMS = (((0,), (0,)), ((), ()))


def _dot(a, b):
    return jnp.dot(a.astype(BF16), b.astype(BF16), preferred_element_type=F32)


def _dot_nt(a, b):
    return lax.dot_general(a.astype(BF16), b.astype(BF16), NT_DIMS, preferred_element_type=F32)


def _split3(x):
    hi = x.astype(BF16)
    r1 = x - hi.astype(F32)
    mid = r1.astype(BF16)
    lo = (r1 - mid.astype(F32)).astype(BF16)
    return hi, mid, lo


def _silu(x):
    return x * jax.nn.sigmoid(x)


def _gelu(x):
    return 0.5 * x * (1.0 + lax.erf(x * (0.5 ** 0.5)))


def _softplus(x):
    return jnp.maximum(x, 0.0) + jnp.log(1.0 + jnp.exp(-jnp.abs(x)))


def _mod_rows(ref):
    m = ref[...]
    return m.reshape(m.shape[-2], m.shape[-1])


def _mod_spec(k, rows_per_batch, tm):
    if rows_per_batch >= tm:
        tiles = rows_per_batch // tm
        return pl.BlockSpec((1, 1, 1, D_MODEL), lambda i, *_: (k, i // tiles, 0, 0))
    return pl.BlockSpec((1, tm, D_MODEL), lambda i, *_: (k, i, 0))


def _const_spec(shape):
    nd = len(shape)
    return pl.BlockSpec(shape, lambda *_: (0,) * nd)


def _ada_body(c_ref, w_ref, b_ref, o_ref):
    o_ref[...] = _dot(_silu(c_ref[...]), w_ref[...]) + b_ref[...]


def _ada(c, w_ada, b_ada):
    n = c.shape[0]
    return pl.pallas_call(
        _ada_body,
        grid=(6,),
        in_specs=[pl.BlockSpec((n, D_MODEL), lambda j: (0, 0)),
                  pl.BlockSpec((D_MODEL, D_MODEL), lambda j: (0, j)),
                  pl.BlockSpec((1, D_MODEL), lambda j: (0, j))],
        out_specs=pl.BlockSpec((n, D_MODEL), lambda j: (0, j)),
        out_shape=jax.ShapeDtypeStruct((n, 6 * D_MODEL), F32),
        name="ada",
    )(c, w_ada, b_ada.reshape(1, -1))


_IN_BLOCKS = (("qkv", QKV_WIDTH), ("z", DN_WIDTH), ("pool", POOL_WIDTH),
              ("ga", D_MODEL), ("gb", D_MODEL), ("ba", LANES))
_IN_TOTAL = sum(w for _, w in _IN_BLOCKS)
_IN_COL_CHUNK = 512


def _inproj_body(x_ref, sc_ref, sh_ref, g_ref, w_ref, *out_refs):
    x = x_ref[...]
    y = x * lax.rsqrt(jnp.mean(x * x, axis=-1, keepdims=True) + EPS) * g_ref[...]
    h = (y * (1.0 + _mod_rows(sc_ref)) + _mod_rows(sh_ref)).astype(BF16)
    off = 0
    for (_, width), o_ref in zip(_IN_BLOCKS, out_refs):
        for c0 in range(0, width, _IN_COL_CHUNK):
            cw = min(_IN_COL_CHUNK, width - c0)
            o_ref[:, c0:c0 + cw] = jnp.dot(h, w_ref[:, off + c0:off + c0 + cw],
                                           preferred_element_type=F32)
        off += width


def _inproj(x2d, mod, rows_per_batch, norm_g, w_cat, tm):
    t = x2d.shape[0]
    row = lambda w: pl.BlockSpec((tm, w), lambda i: (i, 0))
    return pl.pallas_call(
        _inproj_body,
        grid=(t // tm,),
        in_specs=[row(D_MODEL), _mod_spec(1, rows_per_batch, tm), _mod_spec(0, rows_per_batch, tm),
                  _const_spec((1, D_MODEL)),
                  pl.BlockSpec((D_MODEL, _IN_TOTAL), lambda i: (0, 0), pipeline_mode=pl.Buffered(1))],
        out_specs=[row(w) for _, w in _IN_BLOCKS],
        out_shape=[jax.ShapeDtypeStruct((t, w), F32) for _, w in _IN_BLOCKS],
        compiler_params=pltpu.CompilerParams(vmem_limit_bytes=VMEM_LIMIT),
        name="inproj",
    )(x2d, mod, mod, norm_g.reshape(1, -1), w_cat)


def _mixer_body(C, Lv, start,
                qkv_ref, ba_ref, z_ref, pin_ref, ga_ref, gb_ref, cbuf_ref, s0_ref, pbuf_ref,
                convw_ref, alog_ref, dtb_ref, dng_ref, wpool_ref, pscale_ref,
                mixed_ref, nconv_ref, ns_ref, npool_ref,
                xp_scr, act_scr, s_scr, pp_scr, odn_scr):
    n = pl.program_id(1)
    last = pl.num_programs(1) - 1

    @pl.when(n == 0)
    def _load_state():
        xp_scr[0:CONV_PAD, :] = cbuf_ref[0]
        pp_scr[0:POOL_PAD, :] = pbuf_ref[0]
        s_scr[...] = s0_ref[0]

    xp_scr[CONV_PAD:CONV_PAD + C, :] = qkv_ref[0]
    for c0 in range(0, QKV_WIDTH, 512):
        cs = slice(c0, c0 + 512)
        y = xp_scr[CONV_PAD:CONV_PAD + C, cs] * convw_ref[CONV_WIDTH - 1:CONV_WIDTH, cs]
        for k in range(CONV_WIDTH - 1):
            r0 = CONV_PAD - (CONV_WIDTH - 1) + k
            y = y + xp_scr[r0:r0 + C, cs] * convw_ref[k:k + 1, cs]
        act_scr[:, cs] = _silu(y)

    ba = ba_ref[0]
    lane = lax.broadcasted_iota(I32, (C, LANES), 1)
    beta_all = jax.nn.sigmoid(ba)
    g_all = -jnp.exp(alog_ref[...]) * _softplus(ba + dtb_ref[...])
    if Lv < C:
        valid = lax.broadcasted_iota(I32, (C, LANES), 0) < Lv
        beta_all = jnp.where(valid, beta_all, 0.0)
        g_all = jnp.where(valid, g_all, 0.0)
    ii = lax.broadcasted_iota(I32, (C, C), 0)
    jj = lax.broadcasted_iota(I32, (C, C), 1)
    causal = ii >= jj
    strict = ii > jj
    tril = jnp.where(causal, 1.0, 0.0).astype(BF16)
    eye = jnp.where(ii == jj, 1.0, 0.0)
    gc_all = sum(jnp.dot(tril, part, preferred_element_type=F32) for part in _split3(g_all))
    if C < LANES:
        gc_sq = jnp.concatenate([gc_all, jnp.zeros((LANES - C, LANES), F32)], axis=0)
    else:
        gc_sq = gc_all
    gc_t = gc_sq.T

    H = range(DN_HEADS)
    hsl = [slice(h * DN_HEAD_DIM, (h + 1) * DN_HEAD_DIM) for h in H]
    beta = [jnp.sum(jnp.where(lane == h, beta_all, 0.0), axis=1, keepdims=True) for h in H]
    gcol = [jnp.sum(jnp.where(lane == DN_HEADS + h, gc_all, 0.0), axis=1, keepdims=True) for h in H]
    grow = [gc_t[DN_HEADS + h:DN_HEADS + h + 1, 0:C] for h in H]
    glast = [g[C - 1:C, :] for g in gcol]
    q = [act_scr[:, hsl[h]] for h in H]
    k = [act_scr[:, DN_WIDTH + h * DN_HEAD_DIM:DN_WIDTH + (h + 1) * DN_HEAD_DIM] for h in H]
    v = [act_scr[:, 2 * DN_WIDTH + h * DN_HEAD_DIM:2 * DN_WIDTH + (h + 1) * DN_HEAD_DIM] for h in H]
    q = [x * lax.rsqrt(jnp.sum(x * x, axis=-1, keepdims=True) + EPS) * (DN_HEAD_DIM ** -0.5) for x in q]
    k = [x * lax.rsqrt(jnp.sum(x * x, axis=-1, keepdims=True) + EPS) for x in k]
    kb = [k[h] * beta[h] for h in H]
    vb = [v[h] * beta[h] for h in H]
    decay = [jnp.where(causal, jnp.exp(jnp.where(causal, gcol[h] - grow[h], 0.0)), 0.0) for h in H]
    lower = [jnp.where(strict, _dot_nt(kb[h], k[h]) * decay[h], 0.0) for h in H]
    ainv = [eye - x for x in lower]
    pw = lower
    p = 1
    while 2 * p < C:
        pw = [_dot(x, x) for x in pw]
        ainv = [ainv[h] + _dot(ainv[h], pw[h]) for h in H]
        p *= 2
    sol = [_dot(ainv[h], jnp.concatenate([vb[h], kb[h] * jnp.exp(gcol[h])], axis=1)) for h in H]
    qk = [_dot_nt(q[h], k[h]) * decay[h] for h in H]
    k_tail = [k[h] * jnp.exp(glast[h] - gcol[h]) for h in H]
    S = [s_scr[h] for h in H]
    v_new = [sol[h][:, :DN_HEAD_DIM] - _dot(sol[h][:, DN_HEAD_DIM:], S[h]) for h in H]
    o = [_dot(q[h] * jnp.exp(gcol[h]), S[h]) + _dot(qk[h], v_new[h]) for h in H]
    for h in H:
        s_scr[h] = S[h] * jnp.exp(glast[h]) + lax.dot_general(
            k_tail[h].astype(BF16), v_new[h].astype(BF16), TN_DIMS, preferred_element_type=F32)
    for h in H:
        zf = z_ref[0, :, hsl[h]]
        odn_scr[:, hsl[h]] = (o[h] * lax.rsqrt(jnp.mean(o[h] * o[h], axis=-1, keepdims=True) + EPS)
                              * dng_ref[...] * _silu(zf))

    pp_scr[POOL_PAD:POOL_PAD + C, :] = pin_ref[0]
    pos = start + n * C + lax.broadcasted_iota(I32, (C, 1), 0)
    for gi, win in enumerate(POOL_WINDOWS):
        gs = slice(gi * POOL_GROUP_DIM, (gi + 1) * POOL_GROUP_DIM)
        xg = pp_scr[POOL_PAD:POOL_PAD + C, gs]
        ssum = xg
        for sft in range(1, win):
            ssum = ssum + pp_scr[POOL_PAD - sft:POOL_PAD - sft + C, gs]
        cnt = jnp.minimum(pos + 1, win).astype(F32)
        pooled = ssum / cnt - xg
        os_ = slice(gi * POOL_OUT_GROUP, (gi + 1) * POOL_OUT_GROUP)
        yp = _dot(pooled, wpool_ref[gi]) * pscale_ref[:, os_]
        mixed_ref[0, :, os_] = (jax.nn.sigmoid(ga_ref[0, :, os_]) * odn_scr[:, os_]
                                + jax.nn.sigmoid(gb_ref[0, :, os_]) * yp)

    @pl.when(n == last)
    def _store_state():
        nconv_ref[0] = xp_scr[Lv + CONV_PAD - (CONV_WIDTH - 1):Lv + CONV_PAD, :]
        npool_ref[0] = pp_scr[Lv + POOL_PAD - POOL_BUF:Lv + POOL_PAD, :]
        ns_ref[0] = s_scr[...]

    xp_scr[0:CONV_PAD, :] = xp_scr[C:C + CONV_PAD, :]
    pp_scr[0:POOL_PAD, :] = pp_scr[C:C + POOL_PAD, :]


def _mixer(proj, conv_buf, s0, pool_buf, start, seq_len, C,
           conv_w, a_log, dt_bias, dn_norm_g, w_pool, pool_scale):
    b, lp, _ = proj["qkv"].shape
    nchunks = lp // C
    lv = seq_len - (nchunks - 1) * C
    cbuf = jnp.pad(conv_buf, ((0, 0), (CONV_PAD - (CONV_WIDTH - 1), 0), (0, 0)))
    pbuf = jnp.pad(pool_buf, ((0, 0), (POOL_PAD - POOL_BUF, 0), (0, 0)))
    lane_pad = lambda a: jnp.pad(a.reshape(1, -1), ((0, 0), (DN_HEADS, LANES - 2 * DN_HEADS)))
    chunk = lambda w: pl.BlockSpec((1, C, w), lambda i, j: (i, j, 0))
    state = lambda *s: pl.BlockSpec((1,) + s, lambda i, j: (i,) + (0,) * len(s))
    return pl.pallas_call(
        functools.partial(_mixer_body, C, lv, start),
        grid=(b, nchunks),
        in_specs=[chunk(QKV_WIDTH), chunk(LANES), chunk(DN_WIDTH), chunk(POOL_WIDTH),
                  chunk(D_MODEL), chunk(D_MODEL),
                  state(CONV_PAD, QKV_WIDTH), state(DN_HEADS, DN_HEAD_DIM, DN_HEAD_DIM),
                  state(POOL_PAD, POOL_WIDTH),
                  _const_spec((CONV_WIDTH, QKV_WIDTH)), _const_spec((1, LANES)), _const_spec((1, LANES)),
                  _const_spec((1, DN_HEAD_DIM)),
                  _const_spec((len(POOL_WINDOWS), POOL_GROUP_DIM, POOL_OUT_GROUP)),
                  _const_spec((1, D_MODEL))],
        out_specs=[chunk(D_MODEL), state(CONV_WIDTH - 1, QKV_WIDTH),
                   state(DN_HEADS, DN_HEAD_DIM, DN_HEAD_DIM), state(POOL_BUF, POOL_WIDTH)],
        out_shape=[jax.ShapeDtypeStruct((b, lp, D_MODEL), F32),
                   jax.ShapeDtypeStruct((b, CONV_WIDTH - 1, QKV_WIDTH), F32),
                   jax.ShapeDtypeStruct((b, DN_HEADS, DN_HEAD_DIM, DN_HEAD_DIM), F32),
                   jax.ShapeDtypeStruct((b, POOL_BUF, POOL_WIDTH), F32)],
        scratch_shapes=[pltpu.VMEM((CONV_PAD + C + CONV_PAD, QKV_WIDTH), F32),
                        pltpu.VMEM((C, QKV_WIDTH), F32),
                        pltpu.VMEM((DN_HEADS, DN_HEAD_DIM, DN_HEAD_DIM), F32),
                        pltpu.VMEM((POOL_PAD + C + POOL_PAD, POOL_WIDTH), F32),
                        pltpu.VMEM((C, DN_WIDTH), F32)],
        compiler_params=pltpu.CompilerParams(dimension_semantics=("arbitrary", "arbitrary"),
                                             vmem_limit_bytes=VMEM_LIMIT),
        name="mixer",
    )(proj["qkv"], proj["ba"], proj["z"], proj["pool"], proj["ga"], proj["gb"], cbuf, s0, pbuf,
      conv_w, lane_pad(a_log), lane_pad(dt_bias), dn_norm_g.reshape(1, -1), w_pool,
      pool_scale.reshape(1, -1))


def _top16(s, ids, payload=None):
    big = float(2 ** 24)
    vals, sel, pays = [], [], []
    for _ in range(PEER_TOPK):
        m = jnp.max(s, axis=0, keepdims=True)
        am = jnp.min(jnp.where(s == m, ids, big), axis=0, keepdims=True)
        hit = ids == am
        if payload is not None:
            pays.append(jnp.max(jnp.where(hit, payload, -1.0), axis=0, keepdims=True))
        s = jnp.where(hit, -jnp.inf, s)
        vals.append(m)
        sel.append(am)
    out = (jnp.concatenate(vals, axis=0), jnp.concatenate(sel, axis=0))
    if payload is not None:
        out += (jnp.concatenate(pays, axis=0),)
    return out


_CAND_EDGE = 4


def _post_body(has_prev, has_fin, mixed_ref, x_ref, g1_ref, sc2_ref, sh2_ref, n2g_ref, wout_ref,
               wq_ref, keys_ref, *refs):
    refs = list(refs)
    prev_in = [refs.pop(0) for _ in range(2 if has_prev else 0)]
    fin_in = [refs.pop(0) for _ in range(4 if has_fin else 0)]
    x1_ref, h2_ref, idx_ref, gate_ref = refs[:4]
    extra_out = refs[4:]
    if has_prev:
        pre_ref, pgate_ref = prev_in
        extra_out.pop(0)[...] = _coef_words(pre_ref[...], pgate_ref[...])
    if has_fin:
        _final_body(*fin_in, extra_out.pop(0))
    tm = x_ref.shape[0]
    x1 = x_ref[...] + _mod_rows(g1_ref) * _dot(mixed_ref[...], wout_ref[...])
    x1_ref[...] = x1
    y = x1 * lax.rsqrt(jnp.mean(x1 * x1, axis=-1, keepdims=True) + EPS) * n2g_ref[...]
    h2 = y * (1.0 + _mod_rows(sc2_ref)) + _mod_rows(sh2_ref)
    h2_ref[...] = _pack_words(h2[:, :PACK_HALF], h2[:, PACK_HALF:])
    q = _dot(h2, wq_ref[...])

    K = PEER_TOPK
    key_id = lax.broadcasted_iota(I32, (PEER_NKEYS, 1), 0).astype(F32)
    r16 = lax.broadcasted_iota(I32, (K, 1), 0)
    cand_id = jnp.concatenate([(a * K + r16) for a in range(_CAND_EDGE)]
                              + [(r16 * K + b) for b in range(_CAND_EDGE)], axis=0).astype(F32)
    dup = r16 < _CAND_EDGE
    idx_rows, gate_rows = [], []
    for h in range(PEER_HEADS):
        half = []
        for p in range(2):
            c0 = (h * 2 + p) * PEER_KEY_HALF
            st = _dot_nt(keys_ref[h * 2 + p], q[:, c0:c0 + PEER_KEY_HALF])
            half.append(_top16(st, key_id))
        (s1, i1), (s2, i2) = half
        cand = jnp.concatenate(
            [s1[a:a + 1] + s2 for a in range(_CAND_EDGE)]
            + [jnp.where(dup, -jnp.inf, s1 + s2[b:b + 1]) for b in range(_CAND_EDGE)], axis=0)
        cidx = jnp.concatenate(
            [i1[a:a + 1] * PEER_NKEYS + i2 for a in range(_CAND_EDGE)]
            + [i1 * PEER_NKEYS + i2[b:b + 1] for b in range(_CAND_EDGE)], axis=0)
        best, _, eidx = _top16(cand, cand_id, cidx)
        e = jnp.exp(best - best[0:1])
        gate_rows.append(e / jnp.sum(e, axis=0, keepdims=True))
        idx_rows.append(eidx)
    idx_ref[...] = jnp.concatenate(idx_rows, axis=0).T.astype(I32)
    gate_ref[...] = jnp.concatenate(gate_rows, axis=0).T


def _post(mixed2d, x2d, mod, rows_per_batch, norm2_g, w_out, w_query, keys, tm, prev=None, fin=None):
    t = x2d.shape[0]
    steps = t // tm
    row = lambda w: pl.BlockSpec((tm, w), lambda i: (i, 0))
    in_specs = [row(D_MODEL), row(D_MODEL),
                _mod_spec(2, rows_per_batch, tm), _mod_spec(4, rows_per_batch, tm),
                _mod_spec(3, rows_per_batch, tm), _const_spec((1, D_MODEL)),
                _const_spec((D_MODEL, D_MODEL)), _const_spec((D_MODEL, 2 * PEER_HEADS * PEER_KEY_HALF)),
                _const_spec((2 * PEER_HEADS, PEER_NKEYS, PEER_KEY_HALF))]
    out_specs = [row(D_MODEL), row(PACK_HALF), row(PEER_HK), row(PEER_HK)]
    out_shape = [jax.ShapeDtypeStruct((t, D_MODEL), F32), jax.ShapeDtypeStruct((t, PACK_HALF), I32),
                 jax.ShapeDtypeStruct((t, PEER_HK), I32), jax.ShapeDtypeStruct((t, PEER_HK), F32)]
    args = [mixed2d, x2d, mod, mod, mod, norm2_g.reshape(1, -1), w_out, w_query, keys]
    if prev is not None:
        tp = prev[0].shape[0]
        prow = pl.BlockSpec((tp // steps, PEER_HK), lambda i: (i, 0))
        in_specs += [prow, prow]
        out_specs += [prow]
        out_shape += [jax.ShapeDtypeStruct((tp, PEER_HK), I32)]
        args += list(prev)
    if fin is not None:
        x1_f, peer_f, mod_f, rows_f, final_g = fin
        tf = x1_f.shape[0]
        frow = pl.BlockSpec((tf // steps, D_MODEL), lambda i: (i, 0))
        in_specs += [frow, frow, _mod_spec(5, rows_f, tf // steps), _const_spec((1, D_MODEL))]
        out_specs += [frow]
        out_shape += [jax.ShapeDtypeStruct((tf, D_MODEL), F32)]
        args += [x1_f, peer_f, mod_f, final_g.reshape(1, -1)]
    return pl.pallas_call(
        functools.partial(_post_body, prev is not None, fin is not None),
        grid=(steps,),
        in_specs=in_specs, out_specs=out_specs, out_shape=out_shape,
        compiler_params=pltpu.CompilerParams(vmem_limit_bytes=VMEM_LIMIT),
        name="post",
    )(*args)


SC_CORES = 2
SC_SUBCORES = 16
SC_LANES = 16
SC_WORKERS = SC_CORES * SC_SUBCORES
SC_TOKENS = 16
SC_SLOTS = 4
SC_JOB_HEADS = 2
SC_BF16_GROUP = 4
PACK_HALF = D_MODEL // 2
SC_CHUNKS = PACK_HALF // SC_LANES
PROMPT_PARTS = 8
EDGE_SPLITS = 4
FIN_LAG = 3
ROW_TILE = 256


def _bf16_bits(v):
    return lax.bitcast_convert_type(v.astype(BF16).astype(F32), jnp.uint32)


def _pack_words(lo, hi):
    return lax.bitcast_convert_type((_bf16_bits(lo) >> 16) | _bf16_bits(hi), I32)


def _pack_body(x_ref, o_ref):
    o_ref[...] = _pack_words(x_ref[:, :PACK_HALF], x_ref[:, PACK_HALF:])


def _pack_table(tbl, rows=512):
    e = tbl.shape[0]
    return pl.pallas_call(
        _pack_body, grid=(e // rows,),
        in_specs=[pl.BlockSpec((rows, D_MODEL), lambda i: (i, 0))],
        out_specs=pl.BlockSpec((rows, PACK_HALF), lambda i: (i, 0)),
        out_shape=jax.ShapeDtypeStruct((e, PACK_HALF), I32), name="pack_table")(tbl)


def _tree_sum(terms):
    terms = list(terms)
    while len(terms) > 1:
        terms = [a + b for a, b in zip(terms[0::2], terms[1::2])] + terms[len(terms) & ~1:]
    return terms[0]


def _unpack_pair(w):
    lo = plsc.bitcast(lax.shift_left(w, jnp.full(w.shape, 16, I32)), F32)
    hi = plsc.bitcast(w & jnp.full(w.shape, -65536, I32), F32)
    return lo, hi


def _sc_mesh():
    return plsc.VectorSubcoreMesh(core_axis_name="c", subcore_axis_name="s")


def _sc_worker():
    return lax.axis_index("s") * SC_CORES + lax.axis_index("c")


def _sc_jobs(table_hbm, idx_v, buf, sem, compute):
    per_tok = PEER_HEADS // SC_JOB_HEADS
    njobs = idx_v.shape[0] * per_tok
    nrows = SC_JOB_HEADS * PEER_TOPK

    def copy(j, slot):
        rows = idx_v.at[j // per_tok, pl.ds((j % per_tok) * nrows, nrows)]
        return pltpu.make_async_copy(table_hbm.at[rows], buf.at[slot], sem.at[slot])

    for s in range(SC_SLOTS):
        copy(s, s).start()

    def job(j, c):
        s = j % SC_SLOTS
        copy(j, s).wait()

        def head(i, cc):
            compute(j // per_tok, (j % per_tok) * SC_JOB_HEADS + i, s, i * PEER_TOPK)
            return cc
        lax.fori_loop(0, SC_JOB_HEADS, head, 0)

        @pl.when(j + SC_SLOTS < njobs)
        def _next():
            copy(j + SC_SLOTS, s).start()
        return c

    lax.fori_loop(0, njobs, job, 0)


def _peer_u_body(n_tok, idx_hbm, h2_hbm, u_hbm, pre_hbm, idx_v, h2_v, pre_v, ubuf, acc_v, sem):
    base = _sc_worker() * n_tok
    lane = lax.iota(I32, SC_LANES)

    def compute(tt, h, slot, r0):
        def chunk(cg, accs):
            cs = [pl.ds((cg * SC_BF16_GROUP + i) * SC_LANES, SC_LANES) for i in range(SC_BF16_GROUP)]
            xs = [plsc.bitcast(h2_v[tt, c], BF16) for c in cs]
            out = []
            for k, a in enumerate(accs):
                part = _tree_sum([plsc.bitcast(ubuf[slot, r0 + k, c], BF16) * x for c, x in zip(cs, xs)])
                lo, hi = _unpack_pair(plsc.bitcast(part, I32))
                out.append(a + (lo + hi))
            return tuple(out)
        zero = jnp.zeros((SC_LANES,), F32)
        accs = lax.fori_loop(0, SC_CHUNKS // SC_BF16_GROUP, chunk, (zero,) * PEER_TOPK)
        for k, a in enumerate(accs):
            acc_v[k, :] = a
        tot = zero
        for j in range(SC_LANES):
            tot = tot + plsc.load_gather(acc_v, [lane, (lane + j) & (SC_LANES - 1)])
        pre_v[tt, pl.ds(h * PEER_TOPK, PEER_TOPK)] = tot

    tb = idx_v.shape[0]

    def block(bi, c):
        t0 = base + bi * tb
        pltpu.sync_copy(idx_hbm.at[pl.ds(t0, tb)], idx_v)
        pltpu.sync_copy(h2_hbm.at[pl.ds(t0, tb)], h2_v)
        _sc_jobs(u_hbm, idx_v, ubuf, sem, compute)
        pltpu.sync_copy(pre_v, pre_hbm.at[pl.ds(t0, tb)])
        return c

    lax.fori_loop(0, n_tok // tb, block, 0)


def _peer_v_body(n_tok, idx_hbm, coef_hbm, v_hbm, out_hbm, idx_v, coef_v, out_v, vbuf, sem):
    base = _sc_worker() * n_tok
    zero = jnp.zeros((SC_LANES,), F32)

    def compute(tt, h, slot, r0):
        cvec = coef_v[tt, pl.ds(h * PEER_TOPK, PEER_TOPK)]
        cb = [plsc.bitcast(jnp.take_along_axis(cvec, jnp.full((SC_LANES,), k, I32), axis=0), BF16)
              for k in range(PEER_TOPK)]

        @plsc.parallel_loop(0, SC_CHUNKS, unroll=2)
        def _chunk(c):
            cs = pl.ds(c * SC_LANES, SC_LANES)
            prods = [plsc.bitcast(vbuf[slot, r0 + k, cs], BF16) * cb[k] for k in range(PEER_TOPK)]
            pairs = [_unpack_pair(plsc.bitcast(_tree_sum(prods[g:g + SC_BF16_GROUP]), I32))
                     for g in range(0, PEER_TOPK, SC_BF16_GROUP)]
            for half, off in ((0, 0), (1, PACK_HALF)):
                plsc.addupdate(out_v.at[tt, pl.ds(off + c * SC_LANES, SC_LANES)],
                               _tree_sum([p[half] for p in pairs]))

    tb = idx_v.shape[0]

    def block(bi, c):
        t0 = base + bi * tb
        pltpu.sync_copy(idx_hbm.at[pl.ds(t0, tb)], idx_v)
        pltpu.sync_copy(coef_hbm.at[pl.ds(t0, tb)], coef_v)

        def clear(i, cc):
            per_row = D_MODEL // SC_LANES
            out_v[i // per_row, pl.ds((i % per_row) * SC_LANES, SC_LANES)] = zero
            return cc
        lax.fori_loop(0, tb * (D_MODEL // SC_LANES), clear, 0)
        _sc_jobs(v_hbm, idx_v, vbuf, sem, compute)
        pltpu.sync_copy(out_v, out_hbm.at[pl.ds(t0, tb)])
        return c

    lax.fori_loop(0, n_tok // tb, block, 0)


def _peer_sc(body, idx, rows, table, out_width, name):
    t = idx.shape[0]
    assert t % SC_WORKERS == 0
    n_tok = t // SC_WORKERS
    tb = min(SC_TOKENS, n_tok)
    assert n_tok % tb == 0 and tb * PEER_HEADS // SC_JOB_HEADS >= SC_SLOTS
    return pl.kernel(
        functools.partial(body, n_tok),
        out_type=jax.ShapeDtypeStruct((t, out_width), F32),
        mesh=_sc_mesh(),
        scratch_types=[pltpu.VMEM((tb, PEER_HK), I32),
                       pltpu.VMEM((tb, rows.shape[1]), rows.dtype),
                       pltpu.VMEM((tb, out_width), F32),
                       pltpu.VMEM((SC_SLOTS, SC_JOB_HEADS * PEER_TOPK, PACK_HALF), I32)]
                      + ([pltpu.VMEM((PEER_TOPK, SC_LANES), F32)] if body is _peer_u_body else [])
                      + [pltpu.SemaphoreType.DMA((SC_SLOTS,))],
        compiler_params=pltpu.CompilerParams(needs_layout_passes=False),
        name=name,
    )(idx, rows, table)


def _coef_words(pre, gates):
    return _pack_words(*(gates * _gelu(pre),) * 2)


def _coef_body(pre_ref, gate_ref, coef_ref):
    coef_ref[...] = _coef_words(pre_ref[...], gate_ref[...])


def _coef(pre, gates, tm):
    t = pre.shape[0]
    row = pl.BlockSpec((tm, PEER_HK), lambda i: (i, 0))
    return pl.pallas_call(_coef_body, grid=(t // tm,), in_specs=[row, row], out_specs=row,
                          out_shape=jax.ShapeDtypeStruct((t, PEER_HK), I32), name="coef")(pre, gates)


def _final_body(x1_ref, peer_ref, g2_ref, fng_ref, y_ref):
    x2 = x1_ref[...] + _mod_rows(g2_ref) * peer_ref[...]
    y_ref[...] = x2 * lax.rsqrt(jnp.mean(x2 * x2, axis=-1, keepdims=True) + EPS) * fng_ref[...]


def _final(x1, peer_out, mod, rows_per_batch, final_g, tm):
    t = x1.shape[0]
    row = pl.BlockSpec((tm, D_MODEL), lambda i: (i, 0))
    return pl.pallas_call(
        _final_body, grid=(t // tm,),
        in_specs=[row, row, _mod_spec(5, rows_per_batch, tm), _const_spec((1, D_MODEL))],
        out_specs=row, out_shape=jax.ShapeDtypeStruct((t, D_MODEL), F32), name="final",
    )(x1, peer_out, mod, final_g.reshape(1, -1))


def _expert_gather_v(g, coef, expert_v):
    g["peer_out"] = _peer_sc(_peer_v_body, g["idx"], coef, expert_v, D_MODEL, "peer_v")


def _front(x, mod, conv_buf, s0, pool_buf, start, chunk, tm, wts, prev, fin):
    b, l, _ = x.shape
    t = b * l
    x2d = x.reshape(t, D_MODEL)
    if l >= tm:
        modx = mod.reshape(b, 6, 1, D_MODEL).transpose(1, 0, 2, 3)
    else:
        modx = jnp.repeat(mod.reshape(b, 6, D_MODEL), l, axis=0).transpose(1, 0, 2)
    outs = _inproj(x2d, modx, l, wts["norm1_g"], wts["w_cat"], tm)
    lp = -(-l // chunk) * chunk
    proj = {}
    for (name, w), a in zip(_IN_BLOCKS, outs):
        a = a.reshape(b, l, w)
        proj[name] = a if lp == l else jnp.pad(a, ((0, 0), (0, lp - l), (0, 0)))
    mixed, nconv, ns, npool = _mixer(proj, conv_buf, s0, pool_buf, start, l, chunk,
                                     wts["conv_w"], wts["a_log"], wts["dt_bias"], wts["dn_norm_g"],
                                     wts["w_pool"], wts["pool_scale"])
    mixed2d = mixed[:, :l].reshape(t, D_MODEL)
    res = _post(mixed2d, x2d, modx, l, wts["norm2_g"], wts["w_out"], wts["w_query"], wts["keys"], tm,
                prev=None if prev is None else (prev["pre"], prev["gates"]),
                fin=None if fin is None else (fin["x1"], fin["peer_out"], fin["mod"], fin["l"],
                                              wts["final_norm_g"]))
    x1, h2, idx, gates = res[:4]
    extra = list(res[4:])
    coef_prev = extra.pop(0) if prev is not None else None
    y_fin = extra.pop(0).reshape(fin["b"], fin["l"], D_MODEL) if fin is not None else None
    pre = _peer_sc(_peer_u_body, idx, h2, wts["expert_u"], PEER_HK, "peer_u")
    g = dict(x1=x1, idx=idx, gates=gates, pre=pre, mod=modx, b=b, l=l, tm=tm,
             states=(nconv, ns, npool))
    return g, coef_prev, y_fin


def kernel(x_prompt, x_sample, c_prompt, c_sample, state_conv, state_delta, state_pool, w_ada, b_ada, norm1_g, w_in, conv_w, a_log, dt_bias, dn_norm_g, w_pool, pool_scale, w_out, norm2_g, w_query, sub_keys, expert_u, expert_v, final_norm_g):
    bp = x_prompt.shape[0]
    bs = x_sample.shape[0]
    yp, ys = x_prompt, x_sample
    conv_p, delta_p, pool_p, conv_s, delta_s, pool_s = [], [], [], [], [], []
    zero_conv = jnp.zeros((bp, CONV_WIDTH - 1, QKV_WIDTH), F32)
    zero_delta = jnp.zeros((bp, DN_HEADS, DN_HEAD_DIM, DN_HEAD_DIM), F32)
    zero_pool = jnp.zeros((bp, POOL_BUF, POOL_WIDTH), F32)
    c_all = jnp.concatenate([c_prompt, c_sample], axis=0)
    for layer in range(DEPTH):
        wi = w_in[layer]
        o_b = QKV_WIDTH
        o_z = o_b + 2 * DN_HEADS
        w_ba = jnp.pad(wi[:, o_b:o_z], ((0, 0), (0, LANES - 2 * DN_HEADS)))
        w_cat = jnp.concatenate([wi[:, :o_b], wi[:, o_z:], w_ba], axis=1).astype(BF16)
        last = layer == DEPTH - 1
        wts = dict(
            norm1_g=norm1_g[layer], w_cat=w_cat, conv_w=conv_w[layer], a_log=a_log[layer],
            dt_bias=dt_bias[layer], dn_norm_g=dn_norm_g[layer], w_pool=w_pool[layer],
            pool_scale=pool_scale[layer], w_out=w_out[layer].astype(BF16), norm2_g=norm2_g[layer],
            w_query=w_query[layer].astype(BF16),
            keys=sub_keys[layer].reshape(2 * PEER_HEADS, PEER_NKEYS, PEER_KEY_HALF).astype(BF16),
            expert_u=_pack_table(expert_u[layer]), expert_v=_pack_table(expert_v[layer]),
            final_norm_g=final_norm_g if last else jnp.ones_like(final_norm_g))
        mod = _ada(c_all, w_ada[layer], b_ada[layer])
        assert last, "final norm is fused into the expert stage"
        step = bp // PROMPT_PARTS
        seq = x_prompt.shape[1]
        zeros = (zero_conv[:step], zero_delta[:step], zero_pool[:step])
        jobs, cuts = [], []
        for b0 in range(0, bp, step):
            n = EDGE_SPLITS if b0 in (0, bp - step) else 1
            cuts.append(n)
            for s0 in range(0, seq, seq // n):
                jobs.append((yp[b0:b0 + step, s0:s0 + seq // n], mod[b0:b0 + step],
                             zeros if s0 == 0 else None, s0, DN_CHUNK))
        jobs.append((ys, mod[bp:], (state_conv[layer], state_delta[layer], state_pool[layer]),
                     PAST_LEN, SUBLANES))
        groups = []
        for j, (xg, mg, states, start, chunk) in enumerate(jobs):
            prev = groups[j - 1] if j >= 1 else None
            fin = groups[j - FIN_LAG] if j >= FIN_LAG else None
            if fin is not None and fin["x1"].shape[0] % (xg.shape[0] * xg.shape[1] // ROW_TILE):
                fin = None
            if states is None:
                states = prev["states"]
            g, coef_prev, y_fin = _front(xg, mg, *states, start, chunk, ROW_TILE, wts, prev, fin)
            if prev is not None:
                _expert_gather_v(prev, coef_prev, wts["expert_v"])
            if fin is not None:
                fin["y"] = y_fin
            groups.append(g)
        _expert_gather_v(groups[-1], _coef(groups[-1]["pre"], groups[-1]["gates"], ROW_TILE),
                         wts["expert_v"])
        for g in groups:
            if "y" not in g:
                g["y"] = _final(g["x1"], g["peer_out"], g["mod"], g["l"], wts["final_norm_g"],
                                g["tm"]).reshape(g["b"], g["l"], D_MODEL)
        rows, at = [], 0
        for n in cuts:
            rows.append(groups[at:at + n])
            at += n
        yp = jnp.concatenate([jnp.concatenate([g["y"] for g in row], axis=1) for row in rows], axis=0)
        cp, sp, pp = (jnp.concatenate(a, axis=0) for a in zip(*(row[-1]["states"] for row in rows)))
        ys = groups[-1]["y"]
        cs, ss, ps = groups[-1]["states"]
        conv_p.append(cp)
        delta_p.append(sp)
        pool_p.append(pp)
        conv_s.append(cs)
        delta_s.append(ss)
        pool_s.append(ps)
    return (yp, ys, jnp.stack(conv_p), jnp.stack(delta_p), jnp.stack(pool_p),
            jnp.stack(conv_s), jnp.stack(delta_s), jnp.stack(pool_s))
```

```python
import functools

import jax
import jax.numpy as jnp
from jax import lax
from jax.experimental import pallas as pl
from jax.experimental.pallas import tpu as pltpu
from jax.experimental.pallas import tpu_sc as plsc

F32 = jnp.float32
BF16 = jnp.bfloat16
I32 = jnp.int32

D_MODEL = 1024
DEPTH = 1
PAST_LEN = 16384
DN_HEADS = 8
DN_HEAD_DIM = 128
DN_WIDTH = DN_HEADS * DN_HEAD_DIM
QKV_WIDTH = 3 * DN_WIDTH
CONV_WIDTH = 4
DN_CHUNK = 64
POOL_WINDOWS = (2, 4, 8, 16)
POOL_GROUP_DIM = 128
POOL_WIDTH = len(POOL_WINDOWS) * POOL_GROUP_DIM
POOL_OUT_GROUP = D_MODEL // len(POOL_WINDOWS)
POOL_BUF = max(POOL_WINDOWS) - 1
PEER_HEADS = 8
PEER_NKEYS = 128
PEER_TOPK = 16
PEER_KEY_HALF = 128
PEER_HK = PEER_HEADS * PEER_TOPK
EPS = 1e-6

LANES = 128
SUBLANES = 8
CONV_PAD = SUBLANES
POOL_PAD = 16
VMEM_LIMIT = 56 * 1024 * 1024

NT_DIMS = (((1,), (1,)), ((), ()))
TN_DIMS = (((0,), (0,)), ((), ()))


def _dot(a, b):
    return jnp.dot(a.astype(BF16), b.astype(BF16), preferred_element_type=F32)


def _dot_nt(a, b):
    return lax.dot_general(a.astype(BF16), b.astype(BF16), NT_DIMS, preferred_element_type=F32)


def _split3(x):
    hi = x.astype(BF16)
    r1 = x - hi.astype(F32)
    mid = r1.astype(BF16)
    lo = (r1 - mid.astype(F32)).astype(BF16)
    return hi, mid, lo


def _silu(x):
    return x * jax.nn.sigmoid(x)


def _gelu(x):
    return 0.5 * x * (1.0 + lax.erf(x * (0.5 ** 0.5)))


def _softplus(x):
    return jnp.maximum(x, 0.0) + jnp.log(1.0 + jnp.exp(-jnp.abs(x)))


def _mod_rows(ref):
    m = ref[...]
    return m.reshape(m.shape[-2], m.shape[-1])


def _mod_spec(k, rows_per_batch, tm):
    if rows_per_batch >= tm:
        tiles = rows_per_batch // tm
        return pl.BlockSpec((1, 1, 1, D_MODEL), lambda i, *_: (k, i // tiles, 0, 0))
    return pl.BlockSpec((1, tm, D_MODEL), lambda i, *_: (k, i, 0))


def _const_spec(shape):
    nd = len(shape)
    return pl.BlockSpec(shape, lambda *_: (0,) * nd)


def _ada_body(c_ref, w_ref, b_ref, o_ref):
    o_ref[...] = _dot(_silu(c_ref[...]), w_ref[...]) + b_ref[...]


def _ada(c, w_ada, b_ada):
    n = c.shape[0]
    return pl.pallas_call(
        _ada_body,
        grid=(6,),
        in_specs=[pl.BlockSpec((n, D_MODEL), lambda j: (0, 0)),
                  pl.BlockSpec((D_MODEL, D_MODEL), lambda j: (0, j)),
                  pl.BlockSpec((1, D_MODEL), lambda j: (0, j))],
        out_specs=pl.BlockSpec((n, D_MODEL), lambda j: (0, j)),
        out_shape=jax.ShapeDtypeStruct((n, 6 * D_MODEL), F32),
        name="ada",
    )(c, w_ada, b_ada.reshape(1, -1))


_IN_BLOCKS = (("qkv", QKV_WIDTH), ("z", DN_WIDTH), ("pool", POOL_WIDTH),
              ("ga", D_MODEL), ("gb", D_MODEL), ("ba", LANES))
_IN_TOTAL = sum(w for _, w in _IN_BLOCKS)
_IN_COL_CHUNK = 512


def _inproj_body(x_ref, sc_ref, sh_ref, g_ref, w_ref, *out_refs):
    x = x_ref[...]
    y = x * lax.rsqrt(jnp.mean(x * x, axis=-1, keepdims=True) + EPS) * g_ref[...]
    h = (y * (1.0 + _mod_rows(sc_ref)) + _mod_rows(sh_ref)).astype(BF16)
    off = 0
    for (_, width), o_ref in zip(_IN_BLOCKS, out_refs):
        for c0 in range(0, width, _IN_COL_CHUNK):
            cw = min(_IN_COL_CHUNK, width - c0)
            o_ref[:, c0:c0 + cw] = jnp.dot(h, w_ref[:, off + c0:off + c0 + cw],
                                           preferred_element_type=F32)
        off += width


def _inproj(x2d, mod, rows_per_batch, norm_g, w_cat, tm):
    t = x2d.shape[0]
    row = lambda w: pl.BlockSpec((tm, w), lambda i: (i, 0))
    return pl.pallas_call(
        _inproj_body,
        grid=(t // tm,),
        in_specs=[row(D_MODEL), _mod_spec(1, rows_per_batch, tm), _mod_spec(0, rows_per_batch, tm),
                  _const_spec((1, D_MODEL)),
                  pl.BlockSpec((D_MODEL, _IN_TOTAL), lambda i: (0, 0), pipeline_mode=pl.Buffered(1))],
        out_specs=[row(w) for _, w in _IN_BLOCKS],
        out_shape=[jax.ShapeDtypeStruct((t, w), F32) for _, w in _IN_BLOCKS],
        compiler_params=pltpu.CompilerParams(vmem_limit_bytes=VMEM_LIMIT),
        name="inproj",
    )(x2d, mod, mod, norm_g.reshape(1, -1), w_cat)


def _mixer_body(C, Lv, start,
                qkv_ref, ba_ref, z_ref, pin_ref, ga_ref, gb_ref, cbuf_ref, s0_ref, pbuf_ref,
                convw_ref, alog_ref, dtb_ref, dng_ref, wpool_ref, pscale_ref,
                mixed_ref, nconv_ref, ns_ref, npool_ref,
                xp_scr, act_scr, s_scr, pp_scr, odn_scr):
    n = pl.program_id(1)
    last = pl.num_programs(1) - 1

    @pl.when(n == 0)
    def _load_state():
        xp_scr[0:CONV_PAD, :] = cbuf_ref[0]
        pp_scr[0:POOL_PAD, :] = pbuf_ref[0]
        s_scr[...] = s0_ref[0]

    xp_scr[CONV_PAD:CONV_PAD + C, :] = qkv_ref[0]
    for c0 in range(0, QKV_WIDTH, 512):
        cs = slice(c0, c0 + 512)
        y = xp_scr[CONV_PAD:CONV_PAD + C, cs] * convw_ref[CONV_WIDTH - 1:CONV_WIDTH, cs]
        for k in range(CONV_WIDTH - 1):
            r0 = CONV_PAD - (CONV_WIDTH - 1) + k
            y = y + xp_scr[r0:r0 + C, cs] * convw_ref[k:k + 1, cs]
        act_scr[:, cs] = _silu(y)

    ba = ba_ref[0]
    lane = lax.broadcasted_iota(I32, (C, LANES), 1)
    beta_all = jax.nn.sigmoid(ba)
    g_all = -jnp.exp(alog_ref[...]) * _softplus(ba + dtb_ref[...])
    if Lv < C:
        valid = lax.broadcasted_iota(I32, (C, LANES), 0) < Lv
        beta_all = jnp.where(valid, beta_all, 0.0)
        g_all = jnp.where(valid, g_all, 0.0)
    ii = lax.broadcasted_iota(I32, (C, C), 0)
    jj = lax.broadcasted_iota(I32, (C, C), 1)
    causal = ii >= jj
    strict = ii > jj
    tril = jnp.where(causal, 1.0, 0.0).astype(BF16)
    eye = jnp.where(ii == jj, 1.0, 0.0)
    gc_all = sum(jnp.dot(tril, part, preferred_element_type=F32) for part in _split3(g_all))
    if C < LANES:
        gc_sq = jnp.concatenate([gc_all, jnp.zeros((LANES - C, LANES), F32)], axis=0)
    else:
        gc_sq = gc_all
    gc_t = gc_sq.T

    H = range(DN_HEADS)
    hsl = [slice(h * DN_HEAD_DIM, (h + 1) * DN_HEAD_DIM) for h in H]
    beta = [jnp.sum(jnp.where(lane == h, beta_all, 0.0), axis=1, keepdims=True) for h in H]
    gcol = [jnp.sum(jnp.where(lane == DN_HEADS + h, gc_all, 0.0), axis=1, keepdims=True) for h in H]
    grow = [gc_t[DN_HEADS + h:DN_HEADS + h + 1, 0:C] for h in H]
    glast = [g[C - 1:C, :] for g in gcol]
    q = [act_scr[:, hsl[h]] for h in H]
    k = [act_scr[:, DN_WIDTH + h * DN_HEAD_DIM:DN_WIDTH + (h + 1) * DN_HEAD_DIM] for h in H]
    v = [act_scr[:, 2 * DN_WIDTH + h * DN_HEAD_DIM:2 * DN_WIDTH + (h + 1) * DN_HEAD_DIM] for h in H]
    q = [x * lax.rsqrt(jnp.sum(x * x, axis=-1, keepdims=True) + EPS) * (DN_HEAD_DIM ** -0.5) for x in q]
    k = [x * lax.rsqrt(jnp.sum(x * x, axis=-1, keepdims=True) + EPS) for x in k]
    kb = [k[h] * beta[h] for h in H]
    vb = [v[h] * beta[h] for h in H]
    decay = [jnp.where(causal, jnp.exp(jnp.where(causal, gcol[h] - grow[h], 0.0)), 0.0) for h in H]
    lower = [jnp.where(strict, _dot_nt(kb[h], k[h]) * decay[h], 0.0) for h in H]
    ainv = [eye - x for x in lower]
    pw = lower
    p = 1
    while 2 * p < C:
        pw = [_dot(x, x) for x in pw]
        ainv = [ainv[h] + _dot(ainv[h], pw[h]) for h in H]
        p *= 2
    sol = [_dot(ainv[h], jnp.concatenate([vb[h], kb[h] * jnp.exp(gcol[h])], axis=1)) for h in H]
    qk = [_dot_nt(q[h], k[h]) * decay[h] for h in H]
    k_tail = [k[h] * jnp.exp(glast[h] - gcol[h]) for h in H]
    S = [s_scr[h] for h in H]
    v_new = [sol[h][:, :DN_HEAD_DIM] - _dot(sol[h][:, DN_HEAD_DIM:], S[h]) for h in H]
    o = [_dot(q[h] * jnp.exp(gcol[h]), S[h]) + _dot(qk[h], v_new[h]) for h in H]
    for h in H:
        s_scr[h] = S[h] * jnp.exp(glast[h]) + lax.dot_general(
            k_tail[h].astype(BF16), v_new[h].astype(BF16), TN_DIMS, preferred_element_type=F32)
    for h in H:
        zf = z_ref[0, :, hsl[h]]
        odn_scr[:, hsl[h]] = (o[h] * lax.rsqrt(jnp.mean(o[h] * o[h], axis=-1, keepdims=True) + EPS)
                              * dng_ref[...] * _silu(zf))

    pp_scr[POOL_PAD:POOL_PAD + C, :] = pin_ref[0]
    pos = start + n * C + lax.broadcasted_iota(I32, (C, 1), 0)
    for gi, win in enumerate(POOL_WINDOWS):
        gs = slice(gi * POOL_GROUP_DIM, (gi + 1) * POOL_GROUP_DIM)
        xg = pp_scr[POOL_PAD:POOL_PAD + C, gs]
        ssum = xg
        for sft in range(1, win):
            ssum = ssum + pp_scr[POOL_PAD - sft:POOL_PAD - sft + C, gs]
        cnt = jnp.minimum(pos + 1, win).astype(F32)
        pooled = ssum / cnt - xg
        os_ = slice(gi * POOL_OUT_GROUP, (gi + 1) * POOL_OUT_GROUP)
        yp = _dot(pooled, wpool_ref[gi]) * pscale_ref[:, os_]
        mixed_ref[0, :, os_] = (jax.nn.sigmoid(ga_ref[0, :, os_]) * odn_scr[:, os_]
                                + jax.nn.sigmoid(gb_ref[0, :, os_]) * yp)

    @pl.when(n == last)
    def _store_state():
        nconv_ref[0] = xp_scr[Lv + CONV_PAD - (CONV_WIDTH - 1):Lv + CONV_PAD, :]
        npool_ref[0] = pp_scr[Lv + POOL_PAD - POOL_BUF:Lv + POOL_PAD, :]
        ns_ref[0] = s_scr[...]

    xp_scr[0:CONV_PAD, :] = xp_scr[C:C + CONV_PAD, :]
    pp_scr[0:POOL_PAD, :] = pp_scr[C:C + POOL_PAD, :]


def _mixer(proj, conv_buf, s0, pool_buf, start, seq_len, C,
           conv_w, a_log, dt_bias, dn_norm_g, w_pool, pool_scale):
    b, lp, _ = proj["qkv"].shape
    nchunks = lp // C
    lv = seq_len - (nchunks - 1) * C
    cbuf = jnp.pad(conv_buf, ((0, 0), (CONV_PAD - (CONV_WIDTH - 1), 0), (0, 0)))
    pbuf = jnp.pad(pool_buf, ((0, 0), (POOL_PAD - POOL_BUF, 0), (0, 0)))
    lane_pad = lambda a: jnp.pad(a.reshape(1, -1), ((0, 0), (DN_HEADS, LANES - 2 * DN_HEADS)))
    chunk = lambda w: pl.BlockSpec((1, C, w), lambda i, j: (i, j, 0))
    state = lambda *s: pl.BlockSpec((1,) + s, lambda i, j: (i,) + (0,) * len(s))
    return pl.pallas_call(
        functools.partial(_mixer_body, C, lv, start),
        grid=(b, nchunks),
        in_specs=[chunk(QKV_WIDTH), chunk(LANES), chunk(DN_WIDTH), chunk(POOL_WIDTH),
                  chunk(D_MODEL), chunk(D_MODEL),
                  state(CONV_PAD, QKV_WIDTH), state(DN_HEADS, DN_HEAD_DIM, DN_HEAD_DIM),
                  state(POOL_PAD, POOL_WIDTH),
                  _const_spec((CONV_WIDTH, QKV_WIDTH)), _const_spec((1, LANES)), _const_spec((1, LANES)),
                  _const_spec((1, DN_HEAD_DIM)),
                  _const_spec((len(POOL_WINDOWS), POOL_GROUP_DIM, POOL_OUT_GROUP)),
                  _const_spec((1, D_MODEL))],
        out_specs=[chunk(D_MODEL), state(CONV_WIDTH - 1, QKV_WIDTH),
                   state(DN_HEADS, DN_HEAD_DIM, DN_HEAD_DIM), state(POOL_BUF, POOL_WIDTH)],
        out_shape=[jax.ShapeDtypeStruct((b, lp, D_MODEL), F32),
                   jax.ShapeDtypeStruct((b, CONV_WIDTH - 1, QKV_WIDTH), F32),
                   jax.ShapeDtypeStruct((b, DN_HEADS, DN_HEAD_DIM, DN_HEAD_DIM), F32),
                   jax.ShapeDtypeStruct((b, POOL_BUF, POOL_WIDTH), F32)],
        scratch_shapes=[pltpu.VMEM((CONV_PAD + C + CONV_PAD, QKV_WIDTH), F32),
                        pltpu.VMEM((C, QKV_WIDTH), F32),
                        pltpu.VMEM((DN_HEADS, DN_HEAD_DIM, DN_HEAD_DIM), F32),
                        pltpu.VMEM((POOL_PAD + C + POOL_PAD, POOL_WIDTH), F32),
                        pltpu.VMEM((C, DN_WIDTH), F32)],
        compiler_params=pltpu.CompilerParams(dimension_semantics=("arbitrary", "arbitrary"),
                                             vmem_limit_bytes=VMEM_LIMIT),
        name="mixer",
    )(proj["qkv"], proj["ba"], proj["z"], proj["pool"], proj["ga"], proj["gb"], cbuf, s0, pbuf,
      conv_w, lane_pad(a_log), lane_pad(dt_bias), dn_norm_g.reshape(1, -1), w_pool,
      pool_scale.reshape(1, -1))


def _top16(s, ids, payload=None):
    big = float(2 ** 24)
    vals, sel, pays = [], [], []
    for _ in range(PEER_TOPK):
        m = jnp.max(s, axis=0, keepdims=True)
        am = jnp.min(jnp.where(s == m, ids, big), axis=0, keepdims=True)
        hit = ids == am
        if payload is not None:
            pays.append(jnp.max(jnp.where(hit, payload, -1.0), axis=0, keepdims=True))
        s = jnp.where(hit, -jnp.inf, s)
        vals.append(m)
        sel.append(am)
    out = (jnp.concatenate(vals, axis=0), jnp.concatenate(sel, axis=0))
    if payload is not None:
        out += (jnp.concatenate(pays, axis=0),)
    return out


_CAND_EDGE = 4


def _post_body(has_prev, has_fin, mixed_ref, x_ref, g1_ref, sc2_ref, sh2_ref, n2g_ref, wout_ref,
               wq_ref, keys_ref, *refs):
    refs = list(refs)
    prev_in = [refs.pop(0) for _ in range(2 if has_prev else 0)]
    fin_in = [refs.pop(0) for _ in range(4 if has_fin else 0)]
    x1_ref, h2_ref, idx_ref, gate_ref = refs[:4]
    extra_out = refs[4:]
    if has_prev:
        pre_ref, pgate_ref = prev_in
        extra_out.pop(0)[...] = _coef_words(pre_ref[...], pgate_ref[...])
    if has_fin:
        _final_body(*fin_in, extra_out.pop(0))
    tm = x_ref.shape[0]
    x1 = x_ref[...] + _mod_rows(g1_ref) * _dot(mixed_ref[...], wout_ref[...])
    x1_ref[...] = x1
    y = x1 * lax.rsqrt(jnp.mean(x1 * x1, axis=-1, keepdims=True) + EPS) * n2g_ref[...]
    h2 = y * (1.0 + _mod_rows(sc2_ref)) + _mod_rows(sh2_ref)
    h2_ref[...] = _pack_words(h2[:, :PACK_HALF], h2[:, PACK_HALF:])
    q = _dot(h2, wq_ref[...])

    K = PEER_TOPK
    key_id = lax.broadcasted_iota(I32, (PEER_NKEYS, 1), 0).astype(F32)
    r16 = lax.broadcasted_iota(I32, (K, 1), 0)
    cand_id = jnp.concatenate([(a * K + r16) for a in range(_CAND_EDGE)]
                              + [(r16 * K + b) for b in range(_CAND_EDGE)], axis=0).astype(F32)
    dup = r16 < _CAND_EDGE
    idx_rows, gate_rows = [], []
    for h in range(PEER_HEADS):
        half = []
        for p in range(2):
            c0 = (h * 2 + p) * PEER_KEY_HALF
            st = _dot_nt(keys_ref[h * 2 + p], q[:, c0:c0 + PEER_KEY_HALF])
            half.append(_top16(st, key_id))
        (s1, i1), (s2, i2) = half
        cand = jnp.concatenate(
            [s1[a:a + 1] + s2 for a in range(_CAND_EDGE)]
            + [jnp.where(dup, -jnp.inf, s1 + s2[b:b + 1]) for b in range(_CAND_EDGE)], axis=0)
        cidx = jnp.concatenate(
            [i1[a:a + 1] * PEER_NKEYS + i2 for a in range(_CAND_EDGE)]
            + [i1 * PEER_NKEYS + i2[b:b + 1] for b in range(_CAND_EDGE)], axis=0)
        best, _, eidx = _top16(cand, cand_id, cidx)
        e = jnp.exp(best - best[0:1])
        gate_rows.append(e / jnp.sum(e, axis=0, keepdims=True))
        idx_rows.append(eidx)
    idx_ref[...] = jnp.concatenate(idx_rows, axis=0).T.astype(I32)
    gate_ref[...] = jnp.concatenate(gate_rows, axis=0).T


def _post(mixed2d, x2d, mod, rows_per_batch, norm2_g, w_out, w_query, keys, tm, prev=None, fin=None):
    t = x2d.shape[0]
    steps = t // tm
    row = lambda w: pl.BlockSpec((tm, w), lambda i: (i, 0))
    in_specs = [row(D_MODEL), row(D_MODEL),
                _mod_spec(2, rows_per_batch, tm), _mod_spec(4, rows_per_batch, tm),
                _mod_spec(3, rows_per_batch, tm), _const_spec((1, D_MODEL)),
                _const_spec((D_MODEL, D_MODEL)), _const_spec((D_MODEL, 2 * PEER_HEADS * PEER_KEY_HALF)),
                _const_spec((2 * PEER_HEADS, PEER_NKEYS, PEER_KEY_HALF))]
    out_specs = [row(D_MODEL), row(PACK_HALF), row(PEER_HK), row(PEER_HK)]
    out_shape = [jax.ShapeDtypeStruct((t, D_MODEL), F32), jax.ShapeDtypeStruct((t, PACK_HALF), I32),
                 jax.ShapeDtypeStruct((t, PEER_HK), I32), jax.ShapeDtypeStruct((t, PEER_HK), F32)]
    args = [mixed2d, x2d, mod, mod, mod, norm2_g.reshape(1, -1), w_out, w_query, keys]
    if prev is not None:
        tp = prev[0].shape[0]
        prow = pl.BlockSpec((tp // steps, PEER_HK), lambda i: (i, 0))
        in_specs += [prow, prow]
        out_specs += [prow]
        out_shape += [jax.ShapeDtypeStruct((tp, PEER_HK), I32)]
        args += list(prev)
    if fin is not None:
        x1_f, peer_f, mod_f, rows_f, final_g = fin
        tf = x1_f.shape[0]
        frow = pl.BlockSpec((tf // steps, D_MODEL), lambda i: (i, 0))
        in_specs += [frow, frow, _mod_spec(5, rows_f, tf // steps), _const_spec((1, D_MODEL))]
        out_specs += [frow]
        out_shape += [jax.ShapeDtypeStruct((tf, D_MODEL), F32)]
        args += [x1_f, peer_f, mod_f, final_g.reshape(1, -1)]
    return pl.pallas_call(
        functools.partial(_post_body, prev is not None, fin is not None),
        grid=(steps,),
        in_specs=in_specs, out_specs=out_specs, out_shape=out_shape,
        compiler_params=pltpu.CompilerParams(vmem_limit_bytes=VMEM_LIMIT),
        name="post",
    )(*args)


SC_CORES = 2
SC_SUBCORES = 16
SC_LANES = 16
SC_WORKERS = SC_CORES * SC_SUBCORES
SC_TOKENS = 16
SC_SLOTS = 4
SC_JOB_HEADS = 2
SC_BF16_GROUP = 4
PACK_HALF = D_MODEL // 2
SC_CHUNKS = PACK_HALF // SC_LANES
PROMPT_PARTS = 8
EDGE_SPLITS = 2
SAMPLE_SLOT = 4
FIN_LAG = 3
ROW_TILE = 256


def _bf16_bits(v):
    return lax.bitcast_convert_type(v.astype(BF16).astype(F32), jnp.uint32)


def _pack_words(lo, hi):
    return lax.bitcast_convert_type((_bf16_bits(lo) >> 16) | _bf16_bits(hi), I32)


def _pack_body(x_ref, o_ref):
    o_ref[...] = _pack_words(x_ref[:, :PACK_HALF], x_ref[:, PACK_HALF:])


def _pack_table(tbl, rows=512):
    e = tbl.shape[0]
    return pl.pallas_call(
        _pack_body, grid=(e // rows,),
        in_specs=[pl.BlockSpec((rows, D_MODEL), lambda i: (i, 0))],
        out_specs=pl.BlockSpec((rows, PACK_HALF), lambda i: (i, 0)),
        out_shape=jax.ShapeDtypeStruct((e, PACK_HALF), I32), name="pack_table")(tbl)


def _tree_sum(terms):
    terms = list(terms)
    while len(terms) > 1:
        terms = [a + b for a, b in zip(terms[0::2], terms[1::2])] + terms[len(terms) & ~1:]
    return terms[0]


def _unpack_pair(w):
    lo = plsc.bitcast(lax.shift_left(w, jnp.full(w.shape, 16, I32)), F32)
    hi = plsc.bitcast(w & jnp.full(w.shape, -65536, I32), F32)
    return lo, hi


def _sc_mesh():
    return plsc.VectorSubcoreMesh(core_axis_name="c", subcore_axis_name="s")


def _sc_worker():
    return lax.axis_index("s") * SC_CORES + lax.axis_index("c")


def _sc_jobs(table_hbm, idx_v, buf, sem, compute):
    per_tok = PEER_HEADS // SC_JOB_HEADS
    njobs = idx_v.shape[0] * per_tok
    nrows = SC_JOB_HEADS * PEER_TOPK

    def copy(j, slot):
        rows = idx_v.at[j // per_tok, pl.ds((j % per_tok) * nrows, nrows)]
        return pltpu.make_async_copy(table_hbm.at[rows], buf.at[slot], sem.at[slot])

    for s in range(SC_SLOTS):
        copy(s, s).start()

    def job(j, c):
        s = j % SC_SLOTS
        copy(j, s).wait()

        def head(i, cc):
            compute(j // per_tok, (j % per_tok) * SC_JOB_HEADS + i, s, i * PEER_TOPK)
            return cc
        lax.fori_loop(0, SC_JOB_HEADS, head, 0)

        @pl.when(j + SC_SLOTS < njobs)
        def _next():
            copy(j + SC_SLOTS, s).start()
        return c

    lax.fori_loop(0, njobs, job, 0)


def _peer_u_body(n_tok, idx_hbm, h2_hbm, u_hbm, pre_hbm, idx_v, h2_v, pre_v, ubuf, acc_v, sem):
    base = _sc_worker() * n_tok
    lane = lax.iota(I32, SC_LANES)

    def compute(tt, h, slot, r0):
        def chunk(cg, accs):
            cs = [pl.ds((cg * SC_BF16_GROUP + i) * SC_LANES, SC_LANES) for i in range(SC_BF16_GROUP)]
            xs = [plsc.bitcast(h2_v[tt, c], BF16) for c in cs]
            out = []
            for k, a in enumerate(accs):
                part = _tree_sum([plsc.bitcast(ubuf[slot, r0 + k, c], BF16) * x for c, x in zip(cs, xs)])
                lo, hi = _unpack_pair(plsc.bitcast(part, I32))
                out.append(a + (lo + hi))
            return tuple(out)
        zero = jnp.zeros((SC_LANES,), F32)
        accs = lax.fori_loop(0, SC_CHUNKS // SC_BF16_GROUP, chunk, (zero,) * PEER_TOPK)
        for k, a in enumerate(accs):
            acc_v[k, :] = a
        tot = zero
        for j in range(SC_LANES):
            tot = tot + plsc.load_gather(acc_v, [lane, (lane + j) & (SC_LANES - 1)])
        pre_v[tt, pl.ds(h * PEER_TOPK, PEER_TOPK)] = tot

    tb = idx_v.shape[0]

    def block(bi, c):
        t0 = base + bi * tb
        pltpu.sync_copy(idx_hbm.at[pl.ds(t0, tb)], idx_v)
        pltpu.sync_copy(h2_hbm.at[pl.ds(t0, tb)], h2_v)
        _sc_jobs(u_hbm, idx_v, ubuf, sem, compute)
        pltpu.sync_copy(pre_v, pre_hbm.at[pl.ds(t0, tb)])
        return c

    lax.fori_loop(0, n_tok // tb, block, 0)


def _peer_v_body(n_tok, idx_hbm, coef_hbm, v_hbm, out_hbm, idx_v, coef_v, out_v, vbuf, sem):
    base = _sc_worker() * n_tok
    zero = jnp.zeros((SC_LANES,), F32)

    def compute(tt, h, slot, r0):
        cvec = coef_v[tt, pl.ds(h * PEER_TOPK, PEER_TOPK)]
        cb = [plsc.bitcast(jnp.take_along_axis(cvec, jnp.full((SC_LANES,), k, I32), axis=0), BF16)
              for k in range(PEER_TOPK)]

        @plsc.parallel_loop(0, SC_CHUNKS, unroll=2)
        def _chunk(c):
            cs = pl.ds(c * SC_LANES, SC_LANES)
            prods = [plsc.bitcast(vbuf[slot, r0 + k, cs], BF16) * cb[k] for k in range(PEER_TOPK)]
            pairs = [_unpack_pair(plsc.bitcast(_tree_sum(prods[g:g + SC_BF16_GROUP]), I32))
                     for g in range(0, PEER_TOPK, SC_BF16_GROUP)]
            for half, off in ((0, 0), (1, PACK_HALF)):
                plsc.addupdate(out_v.at[tt, pl.ds(off + c * SC_LANES, SC_LANES)],
                               _tree_sum([p[half] for p in pairs]))

    tb = idx_v.shape[0]

    def block(bi, c):
        t0 = base + bi * tb
        pltpu.sync_copy(idx_hbm.at[pl.ds(t0, tb)], idx_v)
        pltpu.sync_copy(coef_hbm.at[pl.ds(t0, tb)], coef_v)

        def clear(i, cc):
            per_row = D_MODEL // SC_LANES
            out_v[i // per_row, pl.ds((i % per_row) * SC_LANES, SC_LANES)] = zero
            return cc
        lax.fori_loop(0, tb * (D_MODEL // SC_LANES), clear, 0)
        _sc_jobs(v_hbm, idx_v, vbuf, sem, compute)
        pltpu.sync_copy(out_v, out_hbm.at[pl.ds(t0, tb)])
        return c

    lax.fori_loop(0, n_tok // tb, block, 0)


def _peer_sc(body, idx, rows, table, out_width, name):
    t = idx.shape[0]
    assert t % SC_WORKERS == 0
    n_tok = t // SC_WORKERS
    tb = min(SC_TOKENS, n_tok)
    assert n_tok % tb == 0 and tb * PEER_HEADS // SC_JOB_HEADS >= SC_SLOTS
    return pl.kernel(
        functools.partial(body, n_tok),
        out_type=jax.ShapeDtypeStruct((t, out_width), F32),
        mesh=_sc_mesh(),
        scratch_types=[pltpu.VMEM((tb, PEER_HK), I32),
                       pltpu.VMEM((tb, rows.shape[1]), rows.dtype),
                       pltpu.VMEM((tb, out_width), F32),
                       pltpu.VMEM((SC_SLOTS, SC_JOB_HEADS * PEER_TOPK, PACK_HALF), I32)]
                      + ([pltpu.VMEM((PEER_TOPK, SC_LANES), F32)] if body is _peer_u_body else [])
                      + [pltpu.SemaphoreType.DMA((SC_SLOTS,))],
        compiler_params=pltpu.CompilerParams(needs_layout_passes=False),
        name=name,
    )(idx, rows, table)


def _coef_words(pre, gates):
    return _pack_words(*(gates * _gelu(pre),) * 2)


def _coef_body(pre_ref, gate_ref, coef_ref):
    coef_ref[...] = _coef_words(pre_ref[...], gate_ref[...])


def _coef(pre, gates, tm):
    t = pre.shape[0]
    row = pl.BlockSpec((tm, PEER_HK), lambda i: (i, 0))
    return pl.pallas_call(_coef_body, grid=(t // tm,), in_specs=[row, row], out_specs=row,
                          out_shape=jax.ShapeDtypeStruct((t, PEER_HK), I32), name="coef")(pre, gates)


def _final_body(x1_ref, peer_ref, g2_ref, fng_ref, y_ref):
    x2 = x1_ref[...] + _mod_rows(g2_ref) * peer_ref[...]
    y_ref[...] = x2 * lax.rsqrt(jnp.mean(x2 * x2, axis=-1, keepdims=True) + EPS) * fng_ref[...]


def _final(x1, peer_out, mod, rows_per_batch, final_g, tm):
    t = x1.shape[0]
    row = pl.BlockSpec((tm, D_MODEL), lambda i: (i, 0))
    return pl.pallas_call(
        _final_body, grid=(t // tm,),
        in_specs=[row, row, _mod_spec(5, rows_per_batch, tm), _const_spec((1, D_MODEL))],
        out_specs=row, out_shape=jax.ShapeDtypeStruct((t, D_MODEL), F32), name="final",
    )(x1, peer_out, mod, final_g.reshape(1, -1))


def _expert_gather_v(g, coef, expert_v):
    g["peer_out"] = _peer_sc(_peer_v_body, g["idx"], coef, expert_v, D_MODEL, "peer_v")


def _front(x, mod, conv_buf, s0, pool_buf, start, chunk, tm, wts, prev, fin):
    b, l, _ = x.shape
    t = b * l
    x2d = x.reshape(t, D_MODEL)
    if l >= tm:
        modx = mod.reshape(b, 6, 1, D_MODEL).transpose(1, 0, 2, 3)
    else:
        modx = jnp.repeat(mod.reshape(b, 6, D_MODEL), l, axis=0).transpose(1, 0, 2)
    outs = _inproj(x2d, modx, l, wts["norm1_g"], wts["w_cat"], tm)
    lp = -(-l // chunk) * chunk
    proj = {}
    for (name, w), a in zip(_IN_BLOCKS, outs):
        a = a.reshape(b, l, w)
        proj[name] = a if lp == l else jnp.pad(a, ((0, 0), (0, lp - l), (0, 0)))
    mixed, nconv, ns, npool = _mixer(proj, conv_buf, s0, pool_buf, start, l, chunk,
                                     wts["conv_w"], wts["a_log"], wts["dt_bias"], wts["dn_norm_g"],
                                     wts["w_pool"], wts["pool_scale"])
    mixed2d = mixed[:, :l].reshape(t, D_MODEL)
    res = _post(mixed2d, x2d, modx, l, wts["norm2_g"], wts["w_out"], wts["w_query"], wts["keys"], tm,
                prev=None if prev is None else (prev["pre"], prev["gates"]),
                fin=None if fin is None else (fin["x1"], fin["peer_out"], fin["mod"], fin["l"],
                                              wts["final_norm_g"]))
    x1, h2, idx, gates = res[:4]
    extra = list(res[4:])
    coef_prev = extra.pop(0) if prev is not None else None
    y_fin = extra.pop(0).reshape(fin["b"], fin["l"], D_MODEL) if fin is not None else None
    pre = _peer_sc(_peer_u_body, idx, h2, wts["expert_u"], PEER_HK, "peer_u")
    g = dict(x1=x1, idx=idx, gates=gates, pre=pre, mod=modx, b=b, l=l, tm=tm,
             states=(nconv, ns, npool))
    return g, coef_prev, y_fin


def kernel(x_prompt, x_sample, c_prompt, c_sample, state_conv, state_delta, state_pool, w_ada, b_ada, norm1_g, w_in, conv_w, a_log, dt_bias, dn_norm_g, w_pool, pool_scale, w_out, norm2_g, w_query, sub_keys, expert_u, expert_v, final_norm_g):
    bp = x_prompt.shape[0]
    bs = x_sample.shape[0]
    yp, ys = x_prompt, x_sample
    conv_p, delta_p, pool_p, conv_s, delta_s, pool_s = [], [], [], [], [], []
    zero_conv = jnp.zeros((bp, CONV_WIDTH - 1, QKV_WIDTH), F32)
    zero_delta = jnp.zeros((bp, DN_HEADS, DN_HEAD_DIM, DN_HEAD_DIM), F32)
    zero_pool = jnp.zeros((bp, POOL_BUF, POOL_WIDTH), F32)
    c_all = jnp.concatenate([c_prompt, c_sample], axis=0)
    for layer in range(DEPTH):
        wi = w_in[layer]
        o_b = QKV_WIDTH
        o_z = o_b + 2 * DN_HEADS
        w_ba = jnp.pad(wi[:, o_b:o_z], ((0, 0), (0, LANES - 2 * DN_HEADS)))
        w_cat = jnp.concatenate([wi[:, :o_b], wi[:, o_z:], w_ba], axis=1).astype(BF16)
        last = layer == DEPTH - 1
        wts = dict(
            norm1_g=norm1_g[layer], w_cat=w_cat, conv_w=conv_w[layer], a_log=a_log[layer],
            dt_bias=dt_bias[layer], dn_norm_g=dn_norm_g[layer], w_pool=w_pool[layer],
            pool_scale=pool_scale[layer], w_out=w_out[layer].astype(BF16), norm2_g=norm2_g[layer],
            w_query=w_query[layer].astype(BF16),
            keys=sub_keys[layer].reshape(2 * PEER_HEADS, PEER_NKEYS, PEER_KEY_HALF).astype(BF16),
            expert_u=_pack_table(expert_u[layer]), expert_v=_pack_table(expert_v[layer]),
            final_norm_g=final_norm_g if last else jnp.ones_like(final_norm_g))
        mod = _ada(c_all, w_ada[layer], b_ada[layer])
        assert last, "final norm is fused into the expert stage"
        step = bp // PROMPT_PARTS
        seq = x_prompt.shape[1]
        zeros = (zero_conv[:step], zero_delta[:step], zero_pool[:step])
        jobs, cuts = [], []
        for b0 in range(0, bp, step):
            n = EDGE_SPLITS if b0 in (0, bp - step) else 1
            cuts.append(n)
            for s0 in range(0, seq, seq // n):
                jobs.append((yp[b0:b0 + step, s0:s0 + seq // n], mod[b0:b0 + step],
                             zeros if s0 == 0 else None, s0, DN_CHUNK))
        sample_at = sum(cuts[:SAMPLE_SLOT])
        jobs.insert(sample_at, (ys, mod[bp:], (state_conv[layer], state_delta[layer], state_pool[layer]),
                                PAST_LEN, SUBLANES))
        groups = []
        for j, (xg, mg, states, start, chunk) in enumerate(jobs):
            prev = groups[j - 1] if j >= 1 else None
            fin = groups[j - FIN_LAG] if j >= FIN_LAG else None
            if fin is not None and fin["x1"].shape[0] % (xg.shape[0] * xg.shape[1] // ROW_TILE):
                fin = None
            if states is None:
                states = prev["states"]
            g, coef_prev, y_fin = _front(xg, mg, *states, start, chunk, ROW_TILE, wts, prev, fin)
            if prev is not None:
                _expert_gather_v(prev, coef_prev, wts["expert_v"])
            if fin is not None:
                fin["y"] = y_fin
            groups.append(g)
        _expert_gather_v(groups[-1], _coef(groups[-1]["pre"], groups[-1]["gates"], ROW_TILE),
                         wts["expert_v"])
        for g in groups:
            if "y" not in g:
                g["y"] = _final(g["x1"], g["peer_out"], g["mod"], g["l"], wts["final_norm_g"],
                                g["tm"]).reshape(g["b"], g["l"], D_MODEL)
        sample = groups.pop(sample_at)
        rows, at = [], 0
        for n in cuts:
            rows.append(groups[at:at + n])
            at += n
        yp = jnp.concatenate([jnp.concatenate([g["y"] for g in row], axis=1) for row in rows], axis=0)
        cp, sp, pp = (jnp.concatenate(a, axis=0) for a in zip(*(row[-1]["states"] for row in rows)))
        ys = sample["y"]
        cs, ss, ps = sample["states"]
        conv_p.append(cp)
        delta_p.append(sp)
        pool_p.append(pp)
        conv_s.append(cs)
        delta_s.append(ss)
        pool_s.append(ps)
    return (yp, ys, jnp.stack(conv_p), jnp.stack(delta_p), jnp.stack(pool_p),
            jnp.stack(conv_s), jnp.stack(delta_s), jnp.stack(pool_s))
```

```python
import functools

import jax
import jax.numpy as jnp
from jax import lax
from jax.experimental import pallas as pl
from jax.experimental.pallas import tpu as pltpu
from jax.experimental.pallas import tpu_sc as plsc

F32 = jnp.float32
BF16 = jnp.bfloat16
I32 = jnp.int32

D_MODEL = 1024
DEPTH = 1
PAST_LEN = 16384
DN_HEADS = 8
DN_HEAD_DIM = 128
DN_WIDTH = DN_HEADS * DN_HEAD_DIM
QKV_WIDTH = 3 * DN_WIDTH
CONV_WIDTH = 4
DN_CHUNK = 64
POOL_WINDOWS = (2, 4, 8, 16)
POOL_GROUP_DIM = 128
POOL_WIDTH = len(POOL_WINDOWS) * POOL_GROUP_DIM
POOL_OUT_GROUP = D_MODEL // len(POOL_WINDOWS)
POOL_BUF = max(POOL_WINDOWS) - 1
PEER_HEADS = 8
PEER_NKEYS = 128
PEER_TOPK = 16
PEER_KEY_HALF = 128
PEER_HK = PEER_HEADS * PEER_TOPK
EPS = 1e-6

LANES = 128
SUBLANES = 8
CONV_PAD = SUBLANES
POOL_PAD = 16
VMEM_LIMIT = 56 * 1024 * 1024

NT_DIMS = (((1,), (1,)), ((), ()))
TN_DIMS = (((0,), (0,)), ((), ()))


def _dot(a, b):
    return jnp.dot(a.astype(BF16), b.astype(BF16), preferred_element_type=F32)


def _dot_nt(a, b):
    return lax.dot_general(a.astype(BF16), b.astype(BF16), NT_DIMS, preferred_element_type=F32)


def _split3(x):
    hi = x.astype(BF16)
    r1 = x - hi.astype(F32)
    mid = r1.astype(BF16)
    lo = (r1 - mid.astype(F32)).astype(BF16)
    return hi, mid, lo


def _silu(x):
    return x * jax.nn.sigmoid(x)


def _gelu(x):
    return 0.5 * x * (1.0 + lax.erf(x * (0.5 ** 0.5)))


def _softplus(x):
    return jnp.maximum(x, 0.0) + jnp.log(1.0 + jnp.exp(-jnp.abs(x)))


def _mod_rows(ref):
    m = ref[...]
    return m.reshape(m.shape[-2], m.shape[-1])


def _mod_spec(k, rows_per_batch, tm):
    if rows_per_batch >= tm:
        tiles = rows_per_batch // tm
        return pl.BlockSpec((1, 1, 1, D_MODEL), lambda i, *_: (k, i // tiles, 0, 0))
    return pl.BlockSpec((1, tm, D_MODEL), lambda i, *_: (k, i, 0))


def _const_spec(shape):
    nd = len(shape)
    return pl.BlockSpec(shape, lambda *_: (0,) * nd)


def _ada_body(c_ref, w_ref, b_ref, o_ref):
    o_ref[...] = _dot(_silu(c_ref[...]), w_ref[...]) + b_ref[...]


def _ada(c, w_ada, b_ada):
    n = c.shape[0]
    return pl.pallas_call(
        _ada_body,
        grid=(6,),
        in_specs=[pl.BlockSpec((n, D_MODEL), lambda j: (0, 0)),
                  pl.BlockSpec((D_MODEL, D_MODEL), lambda j: (0, j)),
                  pl.BlockSpec((1, D_MODEL), lambda j: (0, j))],
        out_specs=pl.BlockSpec((n, D_MODEL), lambda j: (0, j)),
        out_shape=jax.ShapeDtypeStruct((n, 6 * D_MODEL), F32),
        name="ada",
    )(c, w_ada, b_ada.reshape(1, -1))


_IN_BLOCKS = (("qkv", QKV_WIDTH), ("z", DN_WIDTH), ("pool", POOL_WIDTH),
              ("ga", D_MODEL), ("gb", D_MODEL), ("ba", LANES))
_IN_TOTAL = sum(w for _, w in _IN_BLOCKS)
_IN_COL_CHUNK = 512


def _inproj_body(x_ref, sc_ref, sh_ref, g_ref, w_ref, *out_refs):
    x = x_ref[...]
    y = x * lax.rsqrt(jnp.mean(x * x, axis=-1, keepdims=True) + EPS) * g_ref[...]
    h = (y * (1.0 + _mod_rows(sc_ref)) + _mod_rows(sh_ref)).astype(BF16)
    off = 0
    for (_, width), o_ref in zip(_IN_BLOCKS, out_refs):
        for c0 in range(0, width, _IN_COL_CHUNK):
            cw = min(_IN_COL_CHUNK, width - c0)
            o_ref[:, c0:c0 + cw] = jnp.dot(h, w_ref[:, off + c0:off + c0 + cw],
                                           preferred_element_type=F32)
        off += width


def _inproj(x2d, mod, rows_per_batch, norm_g, w_cat, tm):
    t = x2d.shape[0]
    row = lambda w: pl.BlockSpec((tm, w), lambda i: (i, 0))
    return pl.pallas_call(
        _inproj_body,
        grid=(t // tm,),
        in_specs=[row(D_MODEL), _mod_spec(1, rows_per_batch, tm), _mod_spec(0, rows_per_batch, tm),
                  _const_spec((1, D_MODEL)),
                  pl.BlockSpec((D_MODEL, _IN_TOTAL), lambda i: (0, 0), pipeline_mode=pl.Buffered(1))],
        out_specs=[row(w) for _, w in _IN_BLOCKS],
        out_shape=[jax.ShapeDtypeStruct((t, w), F32) for _, w in _IN_BLOCKS],
        compiler_params=pltpu.CompilerParams(vmem_limit_bytes=VMEM_LIMIT),
        name="inproj",
    )(x2d, mod, mod, norm_g.reshape(1, -1), w_cat)


def _mixer_body(C, Lv, start,
                qkv_ref, ba_ref, z_ref, pin_ref, ga_ref, gb_ref, cbuf_ref, s0_ref, pbuf_ref,
                convw_ref, alog_ref, dtb_ref, dng_ref, wpool_ref, pscale_ref,
                mixed_ref, nconv_ref, ns_ref, npool_ref,
                xp_scr, act_scr, s_scr, pp_scr, odn_scr):
    n = pl.program_id(1)
    last = pl.num_programs(1) - 1

    @pl.when(n == 0)
    def _load_state():
        xp_scr[0:CONV_PAD, :] = cbuf_ref[0]
        pp_scr[0:POOL_PAD, :] = pbuf_ref[0]
        s_scr[...] = s0_ref[0]

    xp_scr[CONV_PAD:CONV_PAD + C, :] = qkv_ref[0]
    for c0 in range(0, QKV_WIDTH, 512):
        cs = slice(c0, c0 + 512)
        y = xp_scr[CONV_PAD:CONV_PAD + C, cs] * convw_ref[CONV_WIDTH - 1:CONV_WIDTH, cs]
        for k in range(CONV_WIDTH - 1):
            r0 = CONV_PAD - (CONV_WIDTH - 1) + k
            y = y + xp_scr[r0:r0 + C, cs] * convw_ref[k:k + 1, cs]
        act_scr[:, cs] = _silu(y)

    ba = ba_ref[0]
    lane = lax.broadcasted_iota(I32, (C, LANES), 1)
    beta_all = jax.nn.sigmoid(ba)
    g_all = -jnp.exp(alog_ref[...]) * _softplus(ba + dtb_ref[...])
    if Lv < C:
        valid = lax.broadcasted_iota(I32, (C, LANES), 0) < Lv
        beta_all = jnp.where(valid, beta_all, 0.0)
        g_all = jnp.where(valid, g_all, 0.0)
    ii = lax.broadcasted_iota(I32, (C, C), 0)
    jj = lax.broadcasted_iota(I32, (C, C), 1)
    causal = ii >= jj
    strict = ii > jj
    tril = jnp.where(causal, 1.0, 0.0).astype(BF16)
    eye = jnp.where(ii == jj, 1.0, 0.0)
    gc_all = sum(jnp.dot(tril, part, preferred_element_type=F32) for part in _split3(g_all))
    if C < LANES:
        gc_sq = jnp.concatenate([gc_all, jnp.zeros((LANES - C, LANES), F32)], axis=0)
    else:
        gc_sq = gc_all
    gc_t = gc_sq.T

    H = range(DN_HEADS)
    hsl = [slice(h * DN_HEAD_DIM, (h + 1) * DN_HEAD_DIM) for h in H]
    beta = [jnp.sum(jnp.where(lane == h, beta_all, 0.0), axis=1, keepdims=True) for h in H]
    gcol = [jnp.sum(jnp.where(lane == DN_HEADS + h, gc_all, 0.0), axis=1, keepdims=True) for h in H]
    grow = [gc_t[DN_HEADS + h:DN_HEADS + h + 1, 0:C] for h in H]
    glast = [g[C - 1:C, :] for g in gcol]
    q = [act_scr[:, hsl[h]] for h in H]
    k = [act_scr[:, DN_WIDTH + h * DN_HEAD_DIM:DN_WIDTH + (h + 1) * DN_HEAD_DIM] for h in H]
    v = [act_scr[:, 2 * DN_WIDTH + h * DN_HEAD_DIM:2 * DN_WIDTH + (h + 1) * DN_HEAD_DIM] for h in H]
    q = [x * lax.rsqrt(jnp.sum(x * x, axis=-1, keepdims=True) + EPS) * (DN_HEAD_DIM ** -0.5) for x in q]
    k = [x * lax.rsqrt(jnp.sum(x * x, axis=-1, keepdims=True) + EPS) for x in k]
    kb = [k[h] * beta[h] for h in H]
    vb = [v[h] * beta[h] for h in H]
    decay = [jnp.where(causal, jnp.exp(jnp.where(causal, gcol[h] - grow[h], 0.0)), 0.0) for h in H]
    lower = [jnp.where(strict, _dot_nt(kb[h], k[h]) * decay[h], 0.0) for h in H]
    ainv = [eye - x for x in lower]
    pw = lower
    p = 1
    while 2 * p < C:
        pw = [_dot(x, x) for x in pw]
        ainv = [ainv[h] + _dot(ainv[h], pw[h]) for h in H]
        p *= 2
    sol = [_dot(ainv[h], jnp.concatenate([vb[h], kb[h] * jnp.exp(gcol[h])], axis=1)) for h in H]
    qk = [_dot_nt(q[h], k[h]) * decay[h] for h in H]
    k_tail = [k[h] * jnp.exp(glast[h] - gcol[h]) for h in H]
    S = [s_scr[h] for h in H]
    v_new = [sol[h][:, :DN_HEAD_DIM] - _dot(sol[h][:, DN_HEAD_DIM:], S[h]) for h in H]
    o = [_dot(q[h] * jnp.exp(gcol[h]), S[h]) + _dot(qk[h], v_new[h]) for h in H]
    for h in H:
        s_scr[h] = S[h] * jnp.exp(glast[h]) + lax.dot_general(
            k_tail[h].astype(BF16), v_new[h].astype(BF16), TN_DIMS, preferred_element_type=F32)
    for h in H:
        zf = z_ref[0, :, hsl[h]]
        odn_scr[:, hsl[h]] = (o[h] * lax.rsqrt(jnp.mean(o[h] * o[h], axis=-1, keepdims=True) + EPS)
                              * dng_ref[...] * _silu(zf))

    pp_scr[POOL_PAD:POOL_PAD + C, :] = pin_ref[0]
    pos = start + n * C + lax.broadcasted_iota(I32, (C, 1), 0)
    for gi, win in enumerate(POOL_WINDOWS):
        gs = slice(gi * POOL_GROUP_DIM, (gi + 1) * POOL_GROUP_DIM)
        xg = pp_scr[POOL_PAD:POOL_PAD + C, gs]
        ssum = xg
        for sft in range(1, win):
            ssum = ssum + pp_scr[POOL_PAD - sft:POOL_PAD - sft + C, gs]
        cnt = jnp.minimum(pos + 1, win).astype(F32)
        pooled = ssum / cnt - xg
        os_ = slice(gi * POOL_OUT_GROUP, (gi + 1) * POOL_OUT_GROUP)
        yp = _dot(pooled, wpool_ref[gi]) * pscale_ref[:, os_]
        mixed_ref[0, :, os_] = (jax.nn.sigmoid(ga_ref[0, :, os_]) * odn_scr[:, os_]
                                + jax.nn.sigmoid(gb_ref[0, :, os_]) * yp)

    @pl.when(n == last)
    def _store_state():
        nconv_ref[0] = xp_scr[Lv + CONV_PAD - (CONV_WIDTH - 1):Lv + CONV_PAD, :]
        npool_ref[0] = pp_scr[Lv + POOL_PAD - POOL_BUF:Lv + POOL_PAD, :]
        ns_ref[0] = s_scr[...]

    xp_scr[0:CONV_PAD, :] = xp_scr[C:C + CONV_PAD, :]
    pp_scr[0:POOL_PAD, :] = pp_scr[C:C + POOL_PAD, :]


def _mixer(proj, conv_buf, s0, pool_buf, start, seq_len, C,
           conv_w, a_log, dt_bias, dn_norm_g, w_pool, pool_scale):
    b, lp, _ = proj["qkv"].shape
    nchunks = lp // C
    lv = seq_len - (nchunks - 1) * C
    cbuf = jnp.pad(conv_buf, ((0, 0), (CONV_PAD - (CONV_WIDTH - 1), 0), (0, 0)))
    pbuf = jnp.pad(pool_buf, ((0, 0), (POOL_PAD - POOL_BUF, 0), (0, 0)))
    lane_pad = lambda a: jnp.pad(a.reshape(1, -1), ((0, 0), (DN_HEADS, LANES - 2 * DN_HEADS)))
    chunk = lambda w: pl.BlockSpec((1, C, w), lambda i, j: (i, j, 0))
    state = lambda *s: pl.BlockSpec((1,) + s, lambda i, j: (i,) + (0,) * len(s))
    return pl.pallas_call(
        functools.partial(_mixer_body, C, lv, start),
        grid=(b, nchunks),
        in_specs=[chunk(QKV_WIDTH), chunk(LANES), chunk(DN_WIDTH), chunk(POOL_WIDTH),
                  chunk(D_MODEL), chunk(D_MODEL),
                  state(CONV_PAD, QKV_WIDTH), state(DN_HEADS, DN_HEAD_DIM, DN_HEAD_DIM),
                  state(POOL_PAD, POOL_WIDTH),
                  _const_spec((CONV_WIDTH, QKV_WIDTH)), _const_spec((1, LANES)), _const_spec((1, LANES)),
                  _const_spec((1, DN_HEAD_DIM)),
                  _const_spec((len(POOL_WINDOWS), POOL_GROUP_DIM, POOL_OUT_GROUP)),
                  _const_spec((1, D_MODEL))],
        out_specs=[chunk(D_MODEL), state(CONV_WIDTH - 1, QKV_WIDTH),
                   state(DN_HEADS, DN_HEAD_DIM, DN_HEAD_DIM), state(POOL_BUF, POOL_WIDTH)],
        out_shape=[jax.ShapeDtypeStruct((b, lp, D_MODEL), F32),
                   jax.ShapeDtypeStruct((b, CONV_WIDTH - 1, QKV_WIDTH), F32),
                   jax.ShapeDtypeStruct((b, DN_HEADS, DN_HEAD_DIM, DN_HEAD_DIM), F32),
                   jax.ShapeDtypeStruct((b, POOL_BUF, POOL_WIDTH), F32)],
        scratch_shapes=[pltpu.VMEM((CONV_PAD + C + CONV_PAD, QKV_WIDTH), F32),
                        pltpu.VMEM((C, QKV_WIDTH), F32),
                        pltpu.VMEM((DN_HEADS, DN_HEAD_DIM, DN_HEAD_DIM), F32),
                        pltpu.VMEM((POOL_PAD + C + POOL_PAD, POOL_WIDTH), F32),
                        pltpu.VMEM((C, DN_WIDTH), F32)],
        compiler_params=pltpu.CompilerParams(dimension_semantics=("arbitrary", "arbitrary"),
                                             vmem_limit_bytes=VMEM_LIMIT),
        name="mixer",
    )(proj["qkv"], proj["ba"], proj["z"], proj["pool"], proj["ga"], proj["gb"], cbuf, s0, pbuf,
      conv_w, lane_pad(a_log), lane_pad(dt_bias), dn_norm_g.reshape(1, -1), w_pool,
      pool_scale.reshape(1, -1))


def _top16(s, ids, payload=None):
    big = float(2 ** 24)
    vals, sel, pays = [], [], []
    for _ in range(PEER_TOPK):
        m = jnp.max(s, axis=0, keepdims=True)
        am = jnp.min(jnp.where(s == m, ids, big), axis=0, keepdims=True)
        hit = ids == am
        if payload is not None:
            pays.append(jnp.max(jnp.where(hit, payload, -1.0), axis=0, keepdims=True))
        s = jnp.where(hit, -jnp.inf, s)
        vals.append(m)
        sel.append(am)
    out = (jnp.concatenate(vals, axis=0), jnp.concatenate(sel, axis=0))
    if payload is not None:
        out += (jnp.concatenate(pays, axis=0),)
    return out


_CAND_EDGE = 4


def _post_body(has_prev, has_fin, mixed_ref, x_ref, g1_ref, sc2_ref, sh2_ref, n2g_ref, wout_ref,
               wq_ref, keys_ref, *refs):
    refs = list(refs)
    prev_in = [refs.pop(0) for _ in range(2 if has_prev else 0)]
    fin_in = [refs.pop(0) for _ in range(4 if has_fin else 0)]
    x1_ref, h2_ref, idx_ref, gate_ref = refs[:4]
    extra_out = refs[4:]
    if has_prev:
        pre_ref, pgate_ref = prev_in
        extra_out.pop(0)[...] = _coef_words(pre_ref[...], pgate_ref[...])
    if has_fin:
        _final_body(*fin_in, extra_out.pop(0))
    tm = x_ref.shape[0]
    x1 = x_ref[...] + _mod_rows(g1_ref) * _dot(mixed_ref[...], wout_ref[...])
    x1_ref[...] = x1
    y = x1 * lax.rsqrt(jnp.mean(x1 * x1, axis=-1, keepdims=True) + EPS) * n2g_ref[...]
    h2 = y * (1.0 + _mod_rows(sc2_ref)) + _mod_rows(sh2_ref)
    h2_ref[...] = _pack_words(h2[:, :PACK_HALF], h2[:, PACK_HALF:])
    q = _dot(h2, wq_ref[...])

    K = PEER_TOPK
    key_id = lax.broadcasted_iota(I32, (PEER_NKEYS, 1), 0).astype(F32)
    r16 = lax.broadcasted_iota(I32, (K, 1), 0)
    cand_id = jnp.concatenate([(a * K + r16) for a in range(_CAND_EDGE)]
                              + [(r16 * K + b) for b in range(_CAND_EDGE)], axis=0).astype(F32)
    dup = r16 < _CAND_EDGE
    idx_rows, gate_rows = [], []
    for h in range(PEER_HEADS):
        half = []
        for p in range(2):
            c0 = (h * 2 + p) * PEER_KEY_HALF
            st = _dot_nt(keys_ref[h * 2 + p], q[:, c0:c0 + PEER_KEY_HALF])
            half.append(_top16(st, key_id))
        (s1, i1), (s2, i2) = half
        cand = jnp.concatenate(
            [s1[a:a + 1] + s2 for a in range(_CAND_EDGE)]
            + [jnp.where(dup, -jnp.inf, s1 + s2[b:b + 1]) for b in range(_CAND_EDGE)], axis=0)
        cidx = jnp.concatenate(
            [i1[a:a + 1] * PEER_NKEYS + i2 for a in range(_CAND_EDGE)]
            + [i1 * PEER_NKEYS + i2[b:b + 1] for b in range(_CAND_EDGE)], axis=0)
        best, _, eidx = _top16(cand, cand_id, cidx)
        e = jnp.exp(best - best[0:1])
        gate_rows.append(e / jnp.sum(e, axis=0, keepdims=True))
        idx_rows.append(eidx)
    idx_ref[...] = jnp.concatenate(idx_rows, axis=0).T.astype(I32)
    gate_ref[...] = jnp.concatenate(gate_rows, axis=0).T


def _post(mixed2d, x2d, mod, rows_per_batch, norm2_g, w_out, w_query, keys, tm, prev=None, fin=None):
    t = x2d.shape[0]
    steps = t // tm
    row = lambda w: pl.BlockSpec((tm, w), lambda i: (i, 0))
    in_specs = [row(D_MODEL), row(D_MODEL),
                _mod_spec(2, rows_per_batch, tm), _mod_spec(4, rows_per_batch, tm),
                _mod_spec(3, rows_per_batch, tm), _const_spec((1, D_MODEL)),
                _const_spec((D_MODEL, D_MODEL)), _const_spec((D_MODEL, 2 * PEER_HEADS * PEER_KEY_HALF)),
                _const_spec((2 * PEER_HEADS, PEER_NKEYS, PEER_KEY_HALF))]
    out_specs = [row(D_MODEL), row(PACK_HALF), row(PEER_HK), row(PEER_HK)]
    out_shape = [jax.ShapeDtypeStruct((t, D_MODEL), F32), jax.ShapeDtypeStruct((t, PACK_HALF), I32),
                 jax.ShapeDtypeStruct((t, PEER_HK), I32), jax.ShapeDtypeStruct((t, PEER_HK), F32)]
    args = [mixed2d, x2d, mod, mod, mod, norm2_g.reshape(1, -1), w_out, w_query, keys]
    if prev is not None:
        tp = prev[0].shape[0]
        prow = pl.BlockSpec((tp // steps, PEER_HK), lambda i: (i, 0))
        in_specs += [prow, prow]
        out_specs += [prow]
        out_shape += [jax.ShapeDtypeStruct((tp, PEER_HK), I32)]
        args += list(prev)
    if fin is not None:
        x1_f, peer_f, mod_f, rows_f, final_g = fin
        tf = x1_f.shape[0]
        frow = pl.BlockSpec((tf // steps, D_MODEL), lambda i: (i, 0))
        in_specs += [frow, frow, _mod_spec(5, rows_f, tf // steps), _const_spec((1, D_MODEL))]
        out_specs += [frow]
        out_shape += [jax.ShapeDtypeStruct((tf, D_MODEL), F32)]
        args += [x1_f, peer_f, mod_f, final_g.reshape(1, -1)]
    return pl.pallas_call(
        functools.partial(_post_body, prev is not None, fin is not None),
        grid=(steps,),
        in_specs=in_specs, out_specs=out_specs, out_shape=out_shape,
        compiler_params=pltpu.CompilerParams(vmem_limit_bytes=VMEM_LIMIT),
        name="post",
    )(*args)


SC_CORES = 2
SC_SUBCORES = 16
SC_LANES = 16
SC_WORKERS = SC_CORES * SC_SUBCORES
SC_TOKENS = 32
SC_SLOTS = 4
SC_JOB_HEADS = 2
SC_BF16_GROUP = 4
PACK_HALF = D_MODEL // 2
SC_CHUNKS = PACK_HALF // SC_LANES
PROMPT_PARTS = 8
EDGE_SPLITS = 2
FIN_LAG = 3
ROW_TILE = 256


def _bf16_bits(v):
    return lax.bitcast_convert_type(v.astype(BF16).astype(F32), jnp.uint32)


def _pack_words(lo, hi):
    return lax.bitcast_convert_type((_bf16_bits(lo) >> 16) | _bf16_bits(hi), I32)


def _pack_body(x_ref, o_ref):
    o_ref[...] = _pack_words(x_ref[:, :PACK_HALF], x_ref[:, PACK_HALF:])


def _pack_table(tbl, rows=512):
    e = tbl.shape[0]
    return pl.pallas_call(
        _pack_body, grid=(e // rows,),
        in_specs=[pl.BlockSpec((rows, D_MODEL), lambda i: (i, 0))],
        out_specs=pl.BlockSpec((rows, PACK_HALF), lambda i: (i, 0)),
        out_shape=jax.ShapeDtypeStruct((e, PACK_HALF), I32), name="pack_table")(tbl)


def _tree_sum(terms):
    terms = list(terms)
    while len(terms) > 1:
        terms = [a + b for a, b in zip(terms[0::2], terms[1::2])] + terms[len(terms) & ~1:]
    return terms[0]


def _unpack_pair(w):
    lo = plsc.bitcast(lax.shift_left(w, jnp.full(w.shape, 16, I32)), F32)
    hi = plsc.bitcast(w & jnp.full(w.shape, -65536, I32), F32)
    return lo, hi


def _sc_mesh():
    return plsc.VectorSubcoreMesh(core_axis_name="c", subcore_axis_name="s")


def _sc_worker():
    return lax.axis_index("s") * SC_CORES + lax.axis_index("c")


def _sc_jobs(table_hbm, idx_v, buf, sem, compute):
    per_tok = PEER_HEADS // SC_JOB_HEADS
    njobs = idx_v.shape[0] * per_tok
    nrows = SC_JOB_HEADS * PEER_TOPK

    def copy(j, slot):
        rows = idx_v.at[j // per_tok, pl.ds((j % per_tok) * nrows, nrows)]
        return pltpu.make_async_copy(table_hbm.at[rows], buf.at[slot], sem.at[slot])

    for s in range(SC_SLOTS):
        copy(s, s).start()

    def job(j, c):
        s = j % SC_SLOTS
        copy(j, s).wait()

        def head(i, cc):
            compute(j // per_tok, (j % per_tok) * SC_JOB_HEADS + i, s, i * PEER_TOPK)
            return cc
        lax.fori_loop(0, SC_JOB_HEADS, head, 0)

        @pl.when(j + SC_SLOTS < njobs)
        def _next():
            copy(j + SC_SLOTS, s).start()
        return c

    lax.fori_loop(0, njobs, job, 0)


def _peer_u_body(n_tok, idx_hbm, h2_hbm, u_hbm, pre_hbm, idx_v, h2_v, pre_v, ubuf, acc_v, sem):
    base = _sc_worker() * n_tok
    lane = lax.iota(I32, SC_LANES)

    def compute(tt, h, slot, r0):
        def chunk(cg, accs):
            cs = [pl.ds((cg * SC_BF16_GROUP + i) * SC_LANES, SC_LANES) for i in range(SC_BF16_GROUP)]
            xs = [plsc.bitcast(h2_v[tt, c], BF16) for c in cs]
            out = []
            for k, a in enumerate(accs):
                part = _tree_sum([plsc.bitcast(ubuf[slot, r0 + k, c], BF16) * x for c, x in zip(cs, xs)])
                lo, hi = _unpack_pair(plsc.bitcast(part, I32))
                out.append(a + (lo + hi))
            return tuple(out)
        zero = jnp.zeros((SC_LANES,), F32)
        accs = lax.fori_loop(0, SC_CHUNKS // SC_BF16_GROUP, chunk, (zero,) * PEER_TOPK)
        for k, a in enumerate(accs):
            acc_v[k, :] = a
        tot = zero
        for j in range(SC_LANES):
            tot = tot + plsc.load_gather(acc_v, [lane, (lane + j) & (SC_LANES - 1)])
        pre_v[tt, pl.ds(h * PEER_TOPK, PEER_TOPK)] = tot

    tb = idx_v.shape[0]

    def block(bi, c):
        t0 = base + bi * tb
        pltpu.sync_copy(idx_hbm.at[pl.ds(t0, tb)], idx_v)
        pltpu.sync_copy(h2_hbm.at[pl.ds(t0, tb)], h2_v)
        _sc_jobs(u_hbm, idx_v, ubuf, sem, compute)
        pltpu.sync_copy(pre_v, pre_hbm.at[pl.ds(t0, tb)])
        return c

    lax.fori_loop(0, n_tok // tb, block, 0)


def _peer_v_body(n_tok, idx_hbm, coef_hbm, v_hbm, out_hbm, idx_v, coef_v, out_v, vbuf, sem):
    base = _sc_worker() * n_tok
    zero = jnp.zeros((SC_LANES,), F32)

    def compute(tt, h, slot, r0):
        cvec = coef_v[tt, pl.ds(h * PEER_TOPK, PEER_TOPK)]
        cb = [plsc.bitcast(jnp.take_along_axis(cvec, jnp.full((SC_LANES,), k, I32), axis=0), BF16)
              for k in range(PEER_TOPK)]

        @plsc.parallel_loop(0, SC_CHUNKS, unroll=2)
        def _chunk(c):
            cs = pl.ds(c * SC_LANES, SC_LANES)
            prods = [plsc.bitcast(vbuf[slot, r0 + k, cs], BF16) * cb[k] for k in range(PEER_TOPK)]
            pairs = [_unpack_pair(plsc.bitcast(_tree_sum(prods[g:g + SC_BF16_GROUP]), I32))
                     for g in range(0, PEER_TOPK, SC_BF16_GROUP)]
            for half, off in ((0, 0), (1, PACK_HALF)):
                plsc.addupdate(out_v.at[tt, pl.ds(off + c * SC_LANES, SC_LANES)],
                               _tree_sum([p[half] for p in pairs]))

    tb = idx_v.shape[0]

    def block(bi, c):
        t0 = base + bi * tb
        pltpu.sync_copy(idx_hbm.at[pl.ds(t0, tb)], idx_v)
        pltpu.sync_copy(coef_hbm.at[pl.ds(t0, tb)], coef_v)

        def clear(i, cc):
            per_row = D_MODEL // SC_LANES
            out_v[i // per_row, pl.ds((i % per_row) * SC_LANES, SC_LANES)] = zero
            return cc
        lax.fori_loop(0, tb * (D_MODEL // SC_LANES), clear, 0)
        _sc_jobs(v_hbm, idx_v, vbuf, sem, compute)
        pltpu.sync_copy(out_v, out_hbm.at[pl.ds(t0, tb)])
        return c

    lax.fori_loop(0, n_tok // tb, block, 0)


def _peer_sc(body, idx, rows, table, out_width, name):
    t = idx.shape[0]
    assert t % SC_WORKERS == 0
    n_tok = t // SC_WORKERS
    tb = min(SC_TOKENS, n_tok)
    assert n_tok % tb == 0 and tb * PEER_HEADS // SC_JOB_HEADS >= SC_SLOTS
    return pl.kernel(
        functools.partial(body, n_tok),
        out_type=jax.ShapeDtypeStruct((t, out_width), F32),
        mesh=_sc_mesh(),
        scratch_types=[pltpu.VMEM((tb, PEER_HK), I32),
                       pltpu.VMEM((tb, rows.shape[1]), rows.dtype),
                       pltpu.VMEM((tb, out_width), F32),
                       pltpu.VMEM((SC_SLOTS, SC_JOB_HEADS * PEER_TOPK, PACK_HALF), I32)]
                      + ([pltpu.VMEM((PEER_TOPK, SC_LANES), F32)] if body is _peer_u_body else [])
                      + [pltpu.SemaphoreType.DMA((SC_SLOTS,))],
        compiler_params=pltpu.CompilerParams(needs_layout_passes=False),
        name=name,
    )(idx, rows, table)


def _coef_words(pre, gates):
    return _pack_words(*(gates * _gelu(pre),) * 2)


def _coef_body(pre_ref, gate_ref, coef_ref):
    coef_ref[...] = _coef_words(pre_ref[...], gate_ref[...])


def _coef(pre, gates, tm):
    t = pre.shape[0]
    row = pl.BlockSpec((tm, PEER_HK), lambda i: (i, 0))
    return pl.pallas_call(_coef_body, grid=(t // tm,), in_specs=[row, row], out_specs=row,
                          out_shape=jax.ShapeDtypeStruct((t, PEER_HK), I32), name="coef")(pre, gates)


def _final_body(x1_ref, peer_ref, g2_ref, fng_ref, y_ref):
    x2 = x1_ref[...] + _mod_rows(g2_ref) * peer_ref[...]
    y_ref[...] = x2 * lax.rsqrt(jnp.mean(x2 * x2, axis=-1, keepdims=True) + EPS) * fng_ref[...]


def _final(x1, peer_out, mod, rows_per_batch, final_g, tm):
    t = x1.shape[0]
    row = pl.BlockSpec((tm, D_MODEL), lambda i: (i, 0))
    return pl.pallas_call(
        _final_body, grid=(t // tm,),
        in_specs=[row, row, _mod_spec(5, rows_per_batch, tm), _const_spec((1, D_MODEL))],
        out_specs=row, out_shape=jax.ShapeDtypeStruct((t, D_MODEL), F32), name="final",
    )(x1, peer_out, mod, final_g.reshape(1, -1))


def _expert_gather_v(g, coef, expert_v):
    g["peer_out"] = _peer_sc(_peer_v_body, g["idx"], coef, expert_v, D_MODEL, "peer_v")


def _front(x, mod, conv_buf, s0, pool_buf, start, chunk, tm, wts, prev, fin):
    b, l, _ = x.shape
    t = b * l
    x2d = x.reshape(t, D_MODEL)
    if l >= tm:
        modx = mod.reshape(b, 6, 1, D_MODEL).transpose(1, 0, 2, 3)
    else:
        modx = jnp.repeat(mod.reshape(b, 6, D_MODEL), l, axis=0).transpose(1, 0, 2)
    outs = _inproj(x2d, modx, l, wts["norm1_g"], wts["w_cat"], tm)
    lp = -(-l // chunk) * chunk
    proj = {}
    for (name, w), a in zip(_IN_BLOCKS, outs):
        a = a.reshape(b, l, w)
        proj[name] = a if lp == l else jnp.pad(a, ((0, 0), (0, lp - l), (0, 0)))
    mixed, nconv, ns, npool = _mixer(proj, conv_buf, s0, pool_buf, start, l, chunk,
                                     wts["conv_w"], wts["a_log"], wts["dt_bias"], wts["dn_norm_g"],
                                     wts["w_pool"], wts["pool_scale"])
    mixed2d = mixed[:, :l].reshape(t, D_MODEL)
    res = _post(mixed2d, x2d, modx, l, wts["norm2_g"], wts["w_out"], wts["w_query"], wts["keys"], tm,
                prev=None if prev is None else (prev["pre"], prev["gates"]),
                fin=None if fin is None else (fin["x1"], fin["peer_out"], fin["mod"], fin["l"],
                                              wts["final_norm_g"]))
    x1, h2, idx, gates = res[:4]
    extra = list(res[4:])
    coef_prev = extra.pop(0) if prev is not None else None
    y_fin = extra.pop(0).reshape(fin["b"], fin["l"], D_MODEL) if fin is not None else None
    pre = _peer_sc(_peer_u_body, idx, h2, wts["expert_u"], PEER_HK, "peer_u")
    g = dict(x1=x1, idx=idx, gates=gates, pre=pre, mod=modx, b=b, l=l, tm=tm,
             states=(nconv, ns, npool))
    return g, coef_prev, y_fin


def kernel(x_prompt, x_sample, c_prompt, c_sample, state_conv, state_delta, state_pool, w_ada, b_ada, norm1_g, w_in, conv_w, a_log, dt_bias, dn_norm_g, w_pool, pool_scale, w_out, norm2_g, w_query, sub_keys, expert_u, expert_v, final_norm_g):
    bp = x_prompt.shape[0]
    bs = x_sample.shape[0]
    yp, ys = x_prompt, x_sample
    conv_p, delta_p, pool_p, conv_s, delta_s, pool_s = [], [], [], [], [], []
    zero_conv = jnp.zeros((bp, CONV_WIDTH - 1, QKV_WIDTH), F32)
    zero_delta = jnp.zeros((bp, DN_HEADS, DN_HEAD_DIM, DN_HEAD_DIM), F32)
    zero_pool = jnp.zeros((bp, POOL_BUF, POOL_WIDTH), F32)
    c_all = jnp.concatenate([c_prompt, c_sample], axis=0)
    for layer in range(DEPTH):
        wi = w_in[layer]
        o_b = QKV_WIDTH
        o_z = o_b + 2 * DN_HEADS
        w_ba = jnp.pad(wi[:, o_b:o_z], ((0, 0), (0, LANES - 2 * DN_HEADS)))
        w_cat = jnp.concatenate([wi[:, :o_b], wi[:, o_z:], w_ba], axis=1).astype(BF16)
        last = layer == DEPTH - 1
        wts = dict(
            norm1_g=norm1_g[layer], w_cat=w_cat, conv_w=conv_w[layer], a_log=a_log[layer],
            dt_bias=dt_bias[layer], dn_norm_g=dn_norm_g[layer], w_pool=w_pool[layer],
            pool_scale=pool_scale[layer], w_out=w_out[layer].astype(BF16), norm2_g=norm2_g[layer],
            w_query=w_query[layer].astype(BF16),
            keys=sub_keys[layer].reshape(2 * PEER_HEADS, PEER_NKEYS, PEER_KEY_HALF).astype(BF16),
            expert_u=_pack_table(expert_u[layer]), expert_v=_pack_table(expert_v[layer]),
            final_norm_g=final_norm_g if last else jnp.ones_like(final_norm_g))
        mod = _ada(c_all, w_ada[layer], b_ada[layer])
        assert last, "final norm is fused into the expert stage"
        step = bp // PROMPT_PARTS
        seq = x_prompt.shape[1]
        zeros = (zero_conv[:step], zero_delta[:step], zero_pool[:step])
        jobs, cuts = [], []
        for b0 in range(0, bp, step):
            n = EDGE_SPLITS if b0 in (0, bp - step) else 1
            cuts.append(n)
            for s0 in range(0, seq, seq // n):
                jobs.append((yp[b0:b0 + step, s0:s0 + seq // n], mod[b0:b0 + step],
                             zeros if s0 == 0 else None, s0, DN_CHUNK))
        jobs.append((ys, mod[bp:], (state_conv[layer], state_delta[layer], state_pool[layer]),
                     PAST_LEN, SUBLANES))
        groups = []
        for j, (xg, mg, states, start, chunk) in enumerate(jobs):
            prev = groups[j - 1] if j >= 1 else None
            fin = groups[j - FIN_LAG] if j >= FIN_LAG else None
            if fin is not None and fin["x1"].shape[0] % (xg.shape[0] * xg.shape[1] // ROW_TILE):
                fin = None
            if states is None:
                states = prev["states"]
            g, coef_prev, y_fin = _front(xg, mg, *states, start, chunk, ROW_TILE, wts, prev, fin)
            if prev is not None:
                _expert_gather_v(prev, coef_prev, wts["expert_v"])
            if fin is not None:
                fin["y"] = y_fin
            groups.append(g)
        _expert_gather_v(groups[-1], _coef(groups[-1]["pre"], groups[-1]["gates"], ROW_TILE),
                         wts["expert_v"])
        for g in groups:
            if "y" not in g:
                g["y"] = _final(g["x1"], g["peer_out"], g["mod"], g["l"], wts["final_norm_g"],
                                g["tm"]).reshape(g["b"], g["l"], D_MODEL)
        rows, at = [], 0
        for n in cuts:
            rows.append(groups[at:at + n])
            at += n
        yp = jnp.concatenate([jnp.concatenate([g["y"] for g in row], axis=1) for row in rows], axis=0)
        cp, sp, pp = (jnp.concatenate(a, axis=0) for a in zip(*(row[-1]["states"] for row in rows)))
        ys = groups[-1]["y"]
        cs, ss, ps = groups[-1]["states"]
        conv_p.append(cp)
        delta_p.append(sp)
        pool_p.append(pp)
        conv_s.append(cs)
        delta_s.append(ss)
        pool_s.append(ps)
    return (yp, ys, jnp.stack(conv_p), jnp.stack(delta_p), jnp.stack(pool_p),
            jnp.stack(conv_s), jnp.stack(delta_s), jnp.stack(pool_s))
```

```python
import functools

import jax
import jax.numpy as jnp
from jax import lax
from jax.experimental import pallas as pl
from jax.experimental.pallas import tpu as pltpu
from jax.experimental.pallas import tpu_sc as plsc

F32 = jnp.float32
BF16 = jnp.bfloat16
I32 = jnp.int32

D_MODEL = 1024
DEPTH = 1
PAST_LEN = 16384
DN_HEADS = 8
DN_HEAD_DIM = 128
DN_WIDTH = DN_HEADS * DN_HEAD_DIM
QKV_WIDTH = 3 * DN_WIDTH
CONV_WIDTH = 4
DN_CHUNK = 64
POOL_WINDOWS = (2, 4, 8, 16)
POOL_GROUP_DIM = 128
POOL_WIDTH = len(POOL_WINDOWS) * POOL_GROUP_DIM
POOL_OUT_GROUP = D_MODEL // len(POOL_WINDOWS)
POOL_BUF = max(POOL_WINDOWS) - 1
PEER_HEADS = 8
PEER_NKEYS = 128
PEER_TOPK = 16
PEER_KEY_HALF = 128
PEER_HK = PEER_HEADS * PEER_TOPK
EPS = 1e-6

LANES = 128
SUBLANES = 8
CONV_PAD = SUBLANES
POOL_PAD = 16
VMEM_LIMIT = 56 * 1024 * 1024

NT_DIMS = (((1,), (1,)), ((), ()))
TN_DIMS = (((0,), (0,)), ((), ()))


def _dot(a, b):
    return jnp.dot(a.astype(BF16), b.astype(BF16), preferred_element_type=F32)


def _dot_nt(a, b):
    return lax.dot_general(a.astype(BF16), b.astype(BF16), NT_DIMS, preferred_element_type=F32)


def _split3(x):
    hi = x.astype(BF16)
    r1 = x - hi.astype(F32)
    mid = r1.astype(BF16)
    lo = (r1 - mid.astype(F32)).astype(BF16)
    return hi, mid, lo


def _silu(x):
    return x * jax.nn.sigmoid(x)


def _gelu(x):
    return 0.5 * x * (1.0 + lax.erf(x * (0.5 ** 0.5)))


def _softplus(x):
    return jnp.maximum(x, 0.0) + jnp.log(1.0 + jnp.exp(-jnp.abs(x)))


def _mod_rows(ref):
    m = ref[...]
    return m.reshape(m.shape[-2], m.shape[-1])


def _mod_spec(k, rows_per_batch, tm):
    if rows_per_batch >= tm:
        tiles = rows_per_batch // tm
        return pl.BlockSpec((1, 1, 1, D_MODEL), lambda i, *_: (k, i // tiles, 0, 0))
    return pl.BlockSpec((1, tm, D_MODEL), lambda i, *_: (k, i, 0))


def _const_spec(shape):
    nd = len(shape)
    return pl.BlockSpec(shape, lambda *_: (0,) * nd)


def _ada_body(c_ref, w_ref, b_ref, o_ref):
    o_ref[...] = _dot(_silu(c_ref[...]), w_ref[...]) + b_ref[...]


def _ada(c, w_ada, b_ada):
    n = c.shape[0]
    return pl.pallas_call(
        _ada_body,
        grid=(6,),
        in_specs=[pl.BlockSpec((n, D_MODEL), lambda j: (0, 0)),
                  pl.BlockSpec((D_MODEL, D_MODEL), lambda j: (0, j)),
                  pl.BlockSpec((1, D_MODEL), lambda j: (0, j))],
        out_specs=pl.BlockSpec((n, D_MODEL), lambda j: (0, j)),
        out_shape=jax.ShapeDtypeStruct((n, 6 * D_MODEL), F32),
        name="ada",
    )(c, w_ada, b_ada.reshape(1, -1))


_IN_BLOCKS = (("qkv", QKV_WIDTH), ("z", DN_WIDTH), ("pool", POOL_WIDTH),
              ("ga", D_MODEL), ("gb", D_MODEL), ("ba", LANES))
_IN_TOTAL = sum(w for _, w in _IN_BLOCKS)
_IN_COL_CHUNK = 512


def _inproj_body(x_ref, sc_ref, sh_ref, g_ref, w_ref, *out_refs):
    x = x_ref[...]
    y = x * lax.rsqrt(jnp.mean(x * x, axis=-1, keepdims=True) + EPS) * g_ref[...]
    h = (y * (1.0 + _mod_rows(sc_ref)) + _mod_rows(sh_ref)).astype(BF16)
    off = 0
    for (_, width), o_ref in zip(_IN_BLOCKS, out_refs):
        for c0 in range(0, width, _IN_COL_CHUNK):
            cw = min(_IN_COL_CHUNK, width - c0)
            o_ref[:, c0:c0 + cw] = jnp.dot(h, w_ref[:, off + c0:off + c0 + cw],
                                           preferred_element_type=F32)
        off += width


def _inproj(x2d, mod, rows_per_batch, norm_g, w_cat, tm):
    t = x2d.shape[0]
    row = lambda w: pl.BlockSpec((tm, w), lambda i: (i, 0))
    return pl.pallas_call(
        _inproj_body,
        grid=(t // tm,),
        in_specs=[row(D_MODEL), _mod_spec(1, rows_per_batch, tm), _mod_spec(0, rows_per_batch, tm),
                  _const_spec((1, D_MODEL)),
                  pl.BlockSpec((D_MODEL, _IN_TOTAL), lambda i: (0, 0), pipeline_mode=pl.Buffered(1))],
        out_specs=[row(w) for _, w in _IN_BLOCKS],
        out_shape=[jax.ShapeDtypeStruct((t, w), F32) for _, w in _IN_BLOCKS],
        compiler_params=pltpu.CompilerParams(vmem_limit_bytes=VMEM_LIMIT),
        name="inproj",
    )(x2d, mod, mod, norm_g.reshape(1, -1), w_cat)


def _mixer_body(C, Lv, start,
                qkv_ref, ba_ref, z_ref, pin_ref, ga_ref, gb_ref, cbuf_ref, s0_ref, pbuf_ref,
                convw_ref, alog_ref, dtb_ref, dng_ref, wpool_ref, pscale_ref,
                mixed_ref, nconv_ref, ns_ref, npool_ref,
                xp_scr, act_scr, s_scr, pp_scr, odn_scr):
    n = pl.program_id(1)
    last = pl.num_programs(1) - 1

    @pl.when(n == 0)
    def _load_state():
        xp_scr[0:CONV_PAD, :] = cbuf_ref[0]
        pp_scr[0:POOL_PAD, :] = pbuf_ref[0]
        s_scr[...] = s0_ref[0]

    xp_scr[CONV_PAD:CONV_PAD + C, :] = qkv_ref[0]
    for c0 in range(0, QKV_WIDTH, _IN_COL_CHUNK):
        cs = slice(c0, c0 + _IN_COL_CHUNK)
        y = xp_scr[CONV_PAD:CONV_PAD + C, cs] * convw_ref[CONV_WIDTH - 1:CONV_WIDTH, cs]
        for k in range(CONV_WIDTH - 1):
            r0 = CONV_PAD - (CONV_WIDTH - 1) + k
            y = y + xp_scr[r0:r0 + C, cs] * convw_ref[k:k + 1, cs]
        act_scr[:, cs] = _silu(y)

    ba = ba_ref[0]
    lane = lax.broadcasted_iota(I32, (C, LANES), 1)
    beta_all = jax.nn.sigmoid(ba)
    g_all = -jnp.exp(alog_ref[...]) * _softplus(ba + dtb_ref[...])
    if Lv < C:
        valid = lax.broadcasted_iota(I32, (C, LANES), 0) < Lv
        beta_all = jnp.where(valid, beta_all, 0.0)
        g_all = jnp.where(valid, g_all, 0.0)
    ii = lax.broadcasted_iota(I32, (C, C), 0)
    jj = lax.broadcasted_iota(I32, (C, C), 1)
    causal = ii >= jj
    strict = ii > jj
    tril = jnp.where(causal, 1.0, 0.0).astype(BF16)
    eye = jnp.where(ii == jj, 1.0, 0.0)
    gc_all = sum(jnp.dot(tril, part, preferred_element_type=F32) for part in _split3(g_all))
    if C < LANES:
        gc_sq = jnp.concatenate([gc_all, jnp.zeros((LANES - C, LANES), F32)], axis=0)
    else:
        gc_sq = gc_all
    gc_t = gc_sq.T

    H = range(DN_HEADS)
    hsl = [slice(h * DN_HEAD_DIM, (h + 1) * DN_HEAD_DIM) for h in H]
    beta = [jnp.sum(jnp.where(lane == h, beta_all, 0.0), axis=1, keepdims=True) for h in H]
    gcol = [jnp.sum(jnp.where(lane == DN_HEADS + h, gc_all, 0.0), axis=1, keepdims=True) for h in H]
    grow = [gc_t[DN_HEADS + h:DN_HEADS + h + 1, 0:C] for h in H]
    glast = [g[C - 1:C, :] for g in gcol]
    q = [act_scr[:, hsl[h]] for h in H]
    k = [act_scr[:, DN_WIDTH + h * DN_HEAD_DIM:DN_WIDTH + (h + 1) * DN_HEAD_DIM] for h in H]
    v = [act_scr[:, 2 * DN_WIDTH + h * DN_HEAD_DIM:2 * DN_WIDTH + (h + 1) * DN_HEAD_DIM] for h in H]
    q = [x * lax.rsqrt(jnp.sum(x * x, axis=-1, keepdims=True) + EPS) * (DN_HEAD_DIM ** -0.5) for x in q]
    k = [x * lax.rsqrt(jnp.sum(x * x, axis=-1, keepdims=True) + EPS) for x in k]
    kb = [k[h] * beta[h] for h in H]
    vb = [v[h] * beta[h] for h in H]
    decay = [jnp.where(causal, jnp.exp(jnp.where(causal, gcol[h] - grow[h], 0.0)), 0.0) for h in H]
    lower = [jnp.where(strict, _dot_nt(kb[h], k[h]) * decay[h], 0.0) for h in H]
    ainv = [eye - x for x in lower]
    pw = lower
    p = 1
    while 2 * p < C:
        pw = [_dot(x, x) for x in pw]
        ainv = [ainv[h] + _dot(ainv[h], pw[h]) for h in H]
        p *= 2
    sol = [_dot(ainv[h], jnp.concatenate([vb[h], kb[h] * jnp.exp(gcol[h])], axis=1)) for h in H]
    qk = [_dot_nt(q[h], k[h]) * decay[h] for h in H]
    k_tail = [k[h] * jnp.exp(glast[h] - gcol[h]) for h in H]
    S = [s_scr[h] for h in H]
    v_new = [sol[h][:, :DN_HEAD_DIM] - _dot(sol[h][:, DN_HEAD_DIM:], S[h]) for h in H]
    o = [_dot(q[h] * jnp.exp(gcol[h]), S[h]) + _dot(qk[h], v_new[h]) for h in H]
    for h in H:
        s_scr[h] = S[h] * jnp.exp(glast[h]) + lax.dot_general(
            k_tail[h].astype(BF16), v_new[h].astype(BF16), TN_DIMS, preferred_element_type=F32)
    for h in H:
        zf = z_ref[0, :, hsl[h]]
        odn_scr[:, hsl[h]] = (o[h] * lax.rsqrt(jnp.mean(o[h] * o[h], axis=-1, keepdims=True) + EPS)
                              * dng_ref[...] * _silu(zf))

    pp_scr[POOL_PAD:POOL_PAD + C, :] = pin_ref[0]
    pos = start + n * C + lax.broadcasted_iota(I32, (C, 1), 0)
    for gi, win in enumerate(POOL_WINDOWS):
        gs = slice(gi * POOL_GROUP_DIM, (gi + 1) * POOL_GROUP_DIM)
        xg = pp_scr[POOL_PAD:POOL_PAD + C, gs]
        ssum = xg
        for sft in range(1, win):
            ssum = ssum + pp_scr[POOL_PAD - sft:POOL_PAD - sft + C, gs]
        cnt = jnp.minimum(pos + 1, win).astype(F32)
        pooled = ssum / cnt - xg
        os_ = slice(gi * POOL_OUT_GROUP, (gi + 1) * POOL_OUT_GROUP)
        yp = _dot(pooled, wpool_ref[gi]) * pscale_ref[:, os_]
        mixed_ref[0, :, os_] = (jax.nn.sigmoid(ga_ref[0, :, os_]) * odn_scr[:, os_]
                                + jax.nn.sigmoid(gb_ref[0, :, os_]) * yp)

    @pl.when(n == last)
    def _store_state():
        nconv_ref[0] = xp_scr[Lv + CONV_PAD - (CONV_WIDTH - 1):Lv + CONV_PAD, :]
        npool_ref[0] = pp_scr[Lv + POOL_PAD - POOL_BUF:Lv + POOL_PAD, :]
        ns_ref[0] = s_scr[...]

    xp_scr[0:CONV_PAD, :] = xp_scr[C:C + CONV_PAD, :]
    pp_scr[0:POOL_PAD, :] = pp_scr[C:C + POOL_PAD, :]


def _mixer(proj, conv_buf, s0, pool_buf, start, seq_len, C,
           conv_w, a_log, dt_bias, dn_norm_g, w_pool, pool_scale):
    b, lp, _ = proj["qkv"].shape
    nchunks = lp // C
    lv = seq_len - (nchunks - 1) * C
    cbuf = jnp.pad(conv_buf, ((0, 0), (CONV_PAD - (CONV_WIDTH - 1), 0), (0, 0)))
    pbuf = jnp.pad(pool_buf, ((0, 0), (POOL_PAD - POOL_BUF, 0), (0, 0)))
    lane_pad = lambda a: jnp.pad(a.reshape(1, -1), ((0, 0), (DN_HEADS, LANES - 2 * DN_HEADS)))
    chunk = lambda w: pl.BlockSpec((1, C, w), lambda i, j: (i, j, 0))
    state = lambda *s: pl.BlockSpec((1,) + s, lambda i, j: (i,) + (0,) * len(s))
    return pl.pallas_call(
        functools.partial(_mixer_body, C, lv, start),
        grid=(b, nchunks),
        in_specs=[chunk(QKV_WIDTH), chunk(LANES), chunk(DN_WIDTH), chunk(POOL_WIDTH),
                  chunk(D_MODEL), chunk(D_MODEL),
                  state(CONV_PAD, QKV_WIDTH), state(DN_HEADS, DN_HEAD_DIM, DN_HEAD_DIM),
                  state(POOL_PAD, POOL_WIDTH),
                  _const_spec((CONV_WIDTH, QKV_WIDTH)), _const_spec((1, LANES)), _const_spec((1, LANES)),
                  _const_spec((1, DN_HEAD_DIM)),
                  _const_spec((len(POOL_WINDOWS), POOL_GROUP_DIM, POOL_OUT_GROUP)),
                  _const_spec((1, D_MODEL))],
        out_specs=[chunk(D_MODEL), state(CONV_WIDTH - 1, QKV_WIDTH),
                   state(DN_HEADS, DN_HEAD_DIM, DN_HEAD_DIM), state(POOL_BUF, POOL_WIDTH)],
        out_shape=[jax.ShapeDtypeStruct((b, lp, D_MODEL), F32),
                   jax.ShapeDtypeStruct((b, CONV_WIDTH - 1, QKV_WIDTH), F32),
                   jax.ShapeDtypeStruct((b, DN_HEADS, DN_HEAD_DIM, DN_HEAD_DIM), F32),
                   jax.ShapeDtypeStruct((b, POOL_BUF, POOL_WIDTH), F32)],
        scratch_shapes=[pltpu.VMEM((CONV_PAD + C + CONV_PAD, QKV_WIDTH), F32),
                        pltpu.VMEM((C, QKV_WIDTH), F32),
                        pltpu.VMEM((DN_HEADS, DN_HEAD_DIM, DN_HEAD_DIM), F32),
                        pltpu.VMEM((POOL_PAD + C + POOL_PAD, POOL_WIDTH), F32),
                        pltpu.VMEM((C, DN_WIDTH), F32)],
        compiler_params=pltpu.CompilerParams(dimension_semantics=("arbitrary", "arbitrary"),
                                             vmem_limit_bytes=VMEM_LIMIT),
        name="mixer",
    )(proj["qkv"], proj["ba"], proj["z"], proj["pool"], proj["ga"], proj["gb"], cbuf, s0, pbuf,
      conv_w, lane_pad(a_log), lane_pad(dt_bias), dn_norm_g.reshape(1, -1), w_pool,
      pool_scale.reshape(1, -1))


def _top16(s, ids, payload=None):
    big = float(2 ** 24)
    vals, sel, pays = [], [], []
    for _ in range(PEER_TOPK):
        m = jnp.max(s, axis=0, keepdims=True)
        am = jnp.min(jnp.where(s == m, ids, big), axis=0, keepdims=True)
        hit = ids == am
        if payload is not None:
            pays.append(jnp.max(jnp.where(hit, payload, -1.0), axis=0, keepdims=True))
        s = jnp.where(hit, -jnp.inf, s)
        vals.append(m)
        sel.append(am)
    out = (jnp.concatenate(vals, axis=0), jnp.concatenate(sel, axis=0))
    if payload is not None:
        out += (jnp.concatenate(pays, axis=0),)
    return out


_CAND_EDGE = 4


def _post_body(has_prev, has_fin, mixed_ref, x_ref, g1_ref, sc2_ref, sh2_ref, n2g_ref, wout_ref,
               wq_ref, keys_ref, *refs):
    refs = list(refs)
    prev_in = [refs.pop(0) for _ in range(2 if has_prev else 0)]
    fin_in = [refs.pop(0) for _ in range(4 if has_fin else 0)]
    x1_ref, h2_ref, idx_ref, gate_ref = refs[:4]
    extra_out = refs[4:]
    if has_prev:
        pre_ref, pgate_ref = prev_in
        extra_out.pop(0)[...] = _coef_words(pre_ref[...], pgate_ref[...])
    if has_fin:
        _final_body(*fin_in, extra_out.pop(0))
    tm = x_ref.shape[0]
    x1 = x_ref[...] + _mod_rows(g1_ref) * _dot(mixed_ref[...], wout_ref[...])
    x1_ref[...] = x1
    y = x1 * lax.rsqrt(jnp.mean(x1 * x1, axis=-1, keepdims=True) + EPS) * n2g_ref[...]
    h2 = y * (1.0 + _mod_rows(sc2_ref)) + _mod_rows(sh2_ref)
    h2_ref[...] = _pack_words(h2[:, :PACK_HALF], h2[:, PACK_HALF:])
    q = _dot(h2, wq_ref[...])

    K = PEER_TOPK
    key_id = lax.broadcasted_iota(I32, (PEER_NKEYS, 1), 0).astype(F32)
    r16 = lax.broadcasted_iota(I32, (K, 1), 0)
    cand_id = jnp.concatenate([(a * K + r16) for a in range(_CAND_EDGE)]
                              + [(r16 * K + b) for b in range(_CAND_EDGE)], axis=0).astype(F32)
    dup = r16 < _CAND_EDGE
    idx_rows, gate_rows = [], []
    for h in range(PEER_HEADS):
        half = []
        for p in range(2):
            c0 = (h * 2 + p) * PEER_KEY_HALF
            st = _dot_nt(keys_ref[h * 2 + p], q[:, c0:c0 + PEER_KEY_HALF])
            half.append(_top16(st, key_id))
        (s1, i1), (s2, i2) = half
        cand = jnp.concatenate(
            [s1[a:a + 1] + s2 for a in range(_CAND_EDGE)]
            + [jnp.where(dup, -jnp.inf, s1 + s2[b:b + 1]) for b in range(_CAND_EDGE)], axis=0)
        cidx = jnp.concatenate(
            [i1[a:a + 1] * PEER_NKEYS + i2 for a in range(_CAND_EDGE)]
            + [i1 * PEER_NKEYS + i2[b:b + 1] for b in range(_CAND_EDGE)], axis=0)
        best, _, eidx = _top16(cand, cand_id, cidx)
        e = jnp.exp(best - best[0:1])
        gate_rows.append(e / jnp.sum(e, axis=0, keepdims=True))
        idx_rows.append(eidx)
    idx_ref[...] = jnp.concatenate(idx_rows, axis=0).T.astype(I32)
    gate_ref[...] = jnp.concatenate(gate_rows, axis=0).T


def _post(mixed2d, x2d, mod, rows_per_batch, norm2_g, w_out, w_query, keys, tm, prev=None, fin=None):
    t = x2d.shape[0]
    steps = t // tm
    row = lambda w: pl.BlockSpec((tm, w), lambda i: (i, 0))
    in_specs = [row(D_MODEL), row(D_MODEL),
                _mod_spec(2, rows_per_batch, tm), _mod_spec(4, rows_per_batch, tm),
                _mod_spec(3, rows_per_batch, tm), _const_spec((1, D_MODEL)),
                _const_spec((D_MODEL, D_MODEL)), _const_spec((D_MODEL, 2 * PEER_HEADS * PEER_KEY_HALF)),
                _const_spec((2 * PEER_HEADS, PEER_NKEYS, PEER_KEY_HALF))]
    out_specs = [row(D_MODEL), row(PACK_HALF), row(PEER_HK), row(PEER_HK)]
    out_shape = [jax.ShapeDtypeStruct((t, D_MODEL), F32), jax.ShapeDtypeStruct((t, PACK_HALF), I32),
                 jax.ShapeDtypeStruct((t, PEER_HK), I32), jax.ShapeDtypeStruct((t, PEER_HK), F32)]
    args = [mixed2d, x2d, mod, mod, mod, norm2_g.reshape(1, -1), w_out, w_query, keys]
    if prev is not None:
        tp = prev[0].shape[0]
        prow = pl.BlockSpec((tp // steps, PEER_HK), lambda i: (i, 0))
        in_specs += [prow, prow]
        out_specs += [prow]
        out_shape += [jax.ShapeDtypeStruct((tp, PEER_HK), I32)]
        args += list(prev)
    if fin is not None:
        x1_f, peer_f, mod_f, rows_f, final_g = fin
        tf = x1_f.shape[0]
        frow = pl.BlockSpec((tf // steps, D_MODEL), lambda i: (i, 0))
        in_specs += [frow, frow, _mod_spec(5, rows_f, tf // steps), _const_spec((1, D_MODEL))]
        out_specs += [frow]
        out_shape += [jax.ShapeDtypeStruct((tf, D_MODEL), F32)]
        args += [x1_f, peer_f, mod_f, final_g.reshape(1, -1)]
    return pl.pallas_call(
        functools.partial(_post_body, prev is not None, fin is not None),
        grid=(steps,),
        in_specs=in_specs, out_specs=out_specs, out_shape=out_shape,
        compiler_params=pltpu.CompilerParams(vmem_limit_bytes=VMEM_LIMIT),
        name="post",
    )(*args)


SC_CORES = 2
SC_SUBCORES = 16
SC_LANES = 16
SC_WORKERS = SC_CORES * SC_SUBCORES
SC_TOKENS = 32
SC_SLOTS = 4
SC_JOB_HEADS = 2
SC_BF16_GROUP = 4
PACK_HALF = D_MODEL // 2
SC_CHUNKS = PACK_HALF // SC_LANES
PROMPT_PARTS = 8
EDGE_SPLITS = 2
COEF_LAG = 2
FIN_LAG = 3
ROW_TILE = 256


def _bf16_bits(v):
    return lax.bitcast_convert_type(v.astype(BF16).astype(F32), jnp.uint32)


def _pack_words(lo, hi):
    return lax.bitcast_convert_type((_bf16_bits(lo) >> 16) | _bf16_bits(hi), I32)


def _pack_body(x_ref, o_ref):
    o_ref[...] = _pack_words(x_ref[:, :PACK_HALF], x_ref[:, PACK_HALF:])


def _pack_table(tbl, rows=2 * ROW_TILE):
    e = tbl.shape[0]
    return pl.pallas_call(
        _pack_body, grid=(e // rows,),
        in_specs=[pl.BlockSpec((rows, D_MODEL), lambda i: (i, 0))],
        out_specs=pl.BlockSpec((rows, PACK_HALF), lambda i: (i, 0)),
        out_shape=jax.ShapeDtypeStruct((e, PACK_HALF), I32), name="pack_table")(tbl)


def _tree_sum(terms):
    terms = list(terms)
    while len(terms) > 1:
        terms = [a + b for a, b in zip(terms[0::2], terms[1::2])] + terms[len(terms) & ~1:]
    return terms[0]


def _unpack_pair(w):
    lo = plsc.bitcast(lax.shift_left(w, jnp.full(w.shape, 16, I32)), F32)
    hi = plsc.bitcast(w & jnp.full(w.shape, -65536, I32), F32)
    return lo, hi


def _sc_mesh():
    return plsc.VectorSubcoreMesh(core_axis_name="c", subcore_axis_name="s")


def _sc_worker():
    return lax.axis_index("s") * SC_CORES + lax.axis_index("c")


def _sc_jobs(table_hbm, idx_v, buf, sem, compute):
    per_tok = PEER_HEADS // SC_JOB_HEADS
    njobs = idx_v.shape[0] * per_tok
    nrows = SC_JOB_HEADS * PEER_TOPK

    def copy(j, slot):
        rows = idx_v.at[j // per_tok, pl.ds((j % per_tok) * nrows, nrows)]
        return pltpu.make_async_copy(table_hbm.at[rows], buf.at[slot], sem.at[slot])

    for s in range(SC_SLOTS):
        copy(s, s).start()

    def job(j, c):
        s = j % SC_SLOTS
        copy(j, s).wait()

        def head(i, cc):
            compute(j // per_tok, (j % per_tok) * SC_JOB_HEADS + i, s, i * PEER_TOPK)
            return cc
        lax.fori_loop(0, SC_JOB_HEADS, head, 0)

        @pl.when(j + SC_SLOTS < njobs)
        def _next():
            copy(j + SC_SLOTS, s).start()
        return c

    lax.fori_loop(0, njobs, job, 0)


def _peer_u_body(n_tok, idx_hbm, h2_hbm, u_hbm, pre_hbm, idx_v, h2_v, pre_v, ubuf, acc_v, sem):
    base = _sc_worker() * n_tok
    lane = lax.iota(I32, SC_LANES)

    def compute(tt, h, slot, r0):
        def chunk(cg, accs):
            cs = [pl.ds((cg * SC_BF16_GROUP + i) * SC_LANES, SC_LANES) for i in range(SC_BF16_GROUP)]
            xs = [plsc.bitcast(h2_v[tt, c], BF16) for c in cs]
            out = []
            for k, a in enumerate(accs):
                part = _tree_sum([plsc.bitcast(ubuf[slot, r0 + k, c], BF16) * x for c, x in zip(cs, xs)])
                lo, hi = _unpack_pair(plsc.bitcast(part, I32))
                out.append(a + (lo + hi))
            return tuple(out)
        zero = jnp.zeros((SC_LANES,), F32)
        accs = lax.fori_loop(0, SC_CHUNKS // SC_BF16_GROUP, chunk, (zero,) * PEER_TOPK)
        for k, a in enumerate(accs):
            acc_v[k, :] = a
        tot = zero
        for j in range(SC_LANES):
            tot = tot + plsc.load_gather(acc_v, [lane, (lane + j) & (SC_LANES - 1)])
        pre_v[tt, pl.ds(h * PEER_TOPK, PEER_TOPK)] = tot

    tb = idx_v.shape[0]

    def block(bi, c):
        t0 = base + bi * tb
        pltpu.sync_copy(idx_hbm.at[pl.ds(t0, tb)], idx_v)
        pltpu.sync_copy(h2_hbm.at[pl.ds(t0, tb)], h2_v)
        _sc_jobs(u_hbm, idx_v, ubuf, sem, compute)
        pltpu.sync_copy(pre_v, pre_hbm.at[pl.ds(t0, tb)])
        return c

    lax.fori_loop(0, n_tok // tb, block, 0)


def _peer_v_body(n_tok, idx_hbm, coef_hbm, v_hbm, out_hbm, idx_v, coef_v, out_v, vbuf, sem):
    base = _sc_worker() * n_tok
    zero = jnp.zeros((SC_LANES,), F32)

    def compute(tt, h, slot, r0):
        cvec = coef_v[tt, pl.ds(h * PEER_TOPK, PEER_TOPK)]
        cb = [plsc.bitcast(jnp.take_along_axis(cvec, jnp.full((SC_LANES,), k, I32), axis=0), BF16)
              for k in range(PEER_TOPK)]

        @plsc.parallel_loop(0, SC_CHUNKS, unroll=2)
        def _chunk(c):
            cs = pl.ds(c * SC_LANES, SC_LANES)
            prods = [plsc.bitcast(vbuf[slot, r0 + k, cs], BF16) * cb[k] for k in range(PEER_TOPK)]
            pairs = [_unpack_pair(plsc.bitcast(_tree_sum(prods[g:g + SC_BF16_GROUP]), I32))
                     for g in range(0, PEER_TOPK, SC_BF16_GROUP)]
            for half, off in ((0, 0), (1, PACK_HALF)):
                plsc.addupdate(out_v.at[tt, pl.ds(off + c * SC_LANES, SC_LANES)],
                               _tree_sum([p[half] for p in pairs]))

    tb = idx_v.shape[0]

    def block(bi, c):
        t0 = base + bi * tb
        pltpu.sync_copy(idx_hbm.at[pl.ds(t0, tb)], idx_v)
        pltpu.sync_copy(coef_hbm.at[pl.ds(t0, tb)], coef_v)

        def clear(i, cc):
            per_row = D_MODEL // SC_LANES
            out_v[i // per_row, pl.ds((i % per_row) * SC_LANES, SC_LANES)] = zero
            return cc
        lax.fori_loop(0, tb * (D_MODEL // SC_LANES), clear, 0)
        _sc_jobs(v_hbm, idx_v, vbuf, sem, compute)
        pltpu.sync_copy(out_v, out_hbm.at[pl.ds(t0, tb)])
        return c

    lax.fori_loop(0, n_tok // tb, block, 0)


def _peer_sc(body, idx, rows, table, out_width, name):
    t = idx.shape[0]
    assert t % SC_WORKERS == 0
    n_tok = t // SC_WORKERS
    tb = min(SC_TOKENS * (2 if body is _peer_u_body else 1), n_tok)
    assert n_tok % tb == 0 and tb * PEER_HEADS // SC_JOB_HEADS >= SC_SLOTS
    return pl.kernel(
        functools.partial(body, n_tok),
        out_type=jax.ShapeDtypeStruct((t, out_width), F32),
        mesh=_sc_mesh(),
        scratch_types=[pltpu.VMEM((tb, PEER_HK), I32),
                       pltpu.VMEM((tb, rows.shape[1]), rows.dtype),
                       pltpu.VMEM((tb, out_width), F32),
                       pltpu.VMEM((SC_SLOTS, SC_JOB_HEADS * PEER_TOPK, PACK_HALF), I32)]
                      + ([pltpu.VMEM((PEER_TOPK, SC_LANES), F32)] if body is _peer_u_body else [])
                      + [pltpu.SemaphoreType.DMA((SC_SLOTS,))],
        compiler_params=pltpu.CompilerParams(needs_layout_passes=False),
        name=name,
    )(idx, rows, table)


def _coef_words(pre, gates):
    return _pack_words(*(gates * _gelu(pre),) * 2)


def _coef_body(pre_ref, gate_ref, coef_ref):
    coef_ref[...] = _coef_words(pre_ref[...], gate_ref[...])


def _coef(pre, gates, tm):
    t = pre.shape[0]
    row = pl.BlockSpec((tm, PEER_HK), lambda i: (i, 0))
    return pl.pallas_call(_coef_body, grid=(t // tm,), in_specs=[row, row], out_specs=row,
                          out_shape=jax.ShapeDtypeStruct((t, PEER_HK), I32), name="coef")(pre, gates)


def _final_body(x1_ref, peer_ref, g2_ref, fng_ref, y_ref):
    x2 = x1_ref[...] + _mod_rows(g2_ref) * peer_ref[...]
    y_ref[...] = x2 * lax.rsqrt(jnp.mean(x2 * x2, axis=-1, keepdims=True) + EPS) * fng_ref[...]


def _final(x1, peer_out, mod, rows_per_batch, final_g, tm):
    t = x1.shape[0]
    row = pl.BlockSpec((tm, D_MODEL), lambda i: (i, 0))
    return pl.pallas_call(
        _final_body, grid=(t // tm,),
        in_specs=[row, row, _mod_spec(5, rows_per_batch, tm), _const_spec((1, D_MODEL))],
        out_specs=row, out_shape=jax.ShapeDtypeStruct((t, D_MODEL), F32), name="final",
    )(x1, peer_out, mod, final_g.reshape(1, -1))


def _expert_gather_v(g, coef, expert_v):
    g["peer_out"] = _peer_sc(_peer_v_body, g["idx"], coef, expert_v, D_MODEL, "peer_v")


def _front(x, mod, conv_buf, s0, pool_buf, start, chunk, tm, wts, prev, fin):
    b, l, _ = x.shape
    t = b * l
    x2d = x.reshape(t, D_MODEL)
    if l >= tm:
        modx = mod.reshape(b, 6, 1, D_MODEL).transpose(1, 0, 2, 3)
    else:
        modx = jnp.repeat(mod.reshape(b, 6, D_MODEL), l, axis=0).transpose(1, 0, 2)
    outs = _inproj(x2d, modx, l, wts["norm1_g"], wts["w_cat"], tm)
    lp = -(-l // chunk) * chunk
    proj = {}
    for (name, w), a in zip(_IN_BLOCKS, outs):
        a = a.reshape(b, l, w)
        proj[name] = a if lp == l else jnp.pad(a, ((0, 0), (0, lp - l), (0, 0)))
    mixed, nconv, ns, npool = _mixer(proj, conv_buf, s0, pool_buf, start, l, chunk,
                                     wts["conv_w"], wts["a_log"], wts["dt_bias"], wts["dn_norm_g"],
                                     wts["w_pool"], wts["pool_scale"])
    mixed2d = mixed[:, :l].reshape(t, D_MODEL)
    res = _post(mixed2d, x2d, modx, l, wts["norm2_g"], wts["w_out"], wts["w_query"], wts["keys"], tm,
                prev=None if prev is None else (prev["pre"], prev["gates"]),
                fin=None if fin is None else (fin["x1"], fin["peer_out"], fin["mod"], fin["l"],
                                              wts["final_norm_g"]))
    x1, h2, idx, gates = res[:4]
    extra = list(res[4:])
    coef_prev = extra.pop(0) if prev is not None else None
    y_fin = extra.pop(0).reshape(fin["b"], fin["l"], D_MODEL) if fin is not None else None
    pre = _peer_sc(_peer_u_body, idx, h2, wts["expert_u"], PEER_HK, "peer_u")
    g = dict(x1=x1, idx=idx, gates=gates, pre=pre, mod=modx, b=b, l=l, tm=tm,
             states=(nconv, ns, npool))
    return g, coef_prev, y_fin


def kernel(x_prompt, x_sample, c_prompt, c_sample, state_conv, state_delta, state_pool, w_ada, b_ada, norm1_g, w_in, conv_w, a_log, dt_bias, dn_norm_g, w_pool, pool_scale, w_out, norm2_g, w_query, sub_keys, expert_u, expert_v, final_norm_g):
    bp = x_prompt.shape[0]
    yp, ys = x_prompt, x_sample
    conv_p, delta_p, pool_p, conv_s, delta_s, pool_s = [], [], [], [], [], []
    zero_conv = jnp.zeros((bp, CONV_WIDTH - 1, QKV_WIDTH), F32)
    zero_delta = jnp.zeros((bp, DN_HEADS, DN_HEAD_DIM, DN_HEAD_DIM), F32)
    zero_pool = jnp.zeros((bp, POOL_BUF, POOL_WIDTH), F32)
    c_all = jnp.concatenate([c_prompt, c_sample], axis=0)
    for layer in range(DEPTH):
        wi = w_in[layer]
        o_b = QKV_WIDTH
        o_z = o_b + 2 * DN_HEADS
        w_ba = jnp.pad(wi[:, o_b:o_z], ((0, 0), (0, LANES - 2 * DN_HEADS)))
        w_cat = jnp.concatenate([wi[:, :o_b], wi[:, o_z:], w_ba], axis=1).astype(BF16)
        last = layer == DEPTH - 1
        wts = dict(
            norm1_g=norm1_g[layer], w_cat=w_cat, conv_w=conv_w[layer], a_log=a_log[layer],
            dt_bias=dt_bias[layer], dn_norm_g=dn_norm_g[layer], w_pool=w_pool[layer],
            pool_scale=pool_scale[layer], w_out=w_out[layer].astype(BF16), norm2_g=norm2_g[layer],
            w_query=w_query[layer].astype(BF16),
            keys=sub_keys[layer].reshape(2 * PEER_HEADS, PEER_NKEYS, PEER_KEY_HALF).astype(BF16),
            expert_u=_pack_table(expert_u[layer]), expert_v=_pack_table(expert_v[layer]),
            final_norm_g=final_norm_g if last else jnp.ones_like(final_norm_g))
        mod = _ada(c_all, w_ada[layer], b_ada[layer])
        assert last, "final norm is fused into the expert stage"
        step = bp // PROMPT_PARTS
        seq = x_prompt.shape[1]
        zeros = (zero_conv[:step], zero_delta[:step], zero_pool[:step])
        jobs, cuts = [], []
        for b0 in range(0, bp, step):
            n = EDGE_SPLITS if b0 in (0, bp - step) else 1
            cuts.append(n)
            for s0 in range(0, seq, seq // n):
                jobs.append((yp[b0:b0 + step, s0:s0 + seq // n], mod[b0:b0 + step],
                             zeros if s0 == 0 else None, s0, DN_CHUNK))
        jobs.append((ys, mod[bp:], (state_conv[layer], state_delta[layer], state_pool[layer]),
                     PAST_LEN, SUBLANES))
        groups = []
        for j, (xg, mg, states, start, chunk) in enumerate(jobs):
            prev = groups[j - COEF_LAG] if j >= COEF_LAG else None
            fin = groups[j - FIN_LAG] if j >= FIN_LAG else None
            if fin is not None and fin["x1"].shape[0] % (xg.shape[0] * xg.shape[1] // ROW_TILE):
                fin = None
            if states is None:
                states = groups[j - 1]["states"]
            g, coef_prev, y_fin = _front(xg, mg, *states, start, chunk, ROW_TILE, wts, prev, fin)
            if prev is not None:
                _expert_gather_v(prev, coef_prev, wts["expert_v"])
            if fin is not None:
                fin["y"] = y_fin
            groups.append(g)
        for g in groups[-COEF_LAG:]:
            _expert_gather_v(g, _coef(g["pre"], g["gates"], ROW_TILE), wts["expert_v"])
        for g in groups:
            if "y" not in g:
                g["y"] = _final(g["x1"], g["peer_out"], g["mod"], g["l"], wts["final_norm_g"],
                                g["tm"]).reshape(g["b"], g["l"], D_MODEL)
        rows, at = [], 0
        for n in cuts:
            rows.append(groups[at:at + n])
            at += n
        yp = jnp.concatenate([jnp.concatenate([g["y"] for g in row], axis=1) for row in rows], axis=0)
        cp, sp, pp = (jnp.concatenate(a, axis=0) for a in zip(*(row[-1]["states"] for row in rows)))
        ys = groups[-1]["y"]
        cs, ss, ps = groups[-1]["states"]
        conv_p.append(cp)
        delta_p.append(sp)
        pool_p.append(pp)
        conv_s.append(cs)
        delta_s.append(ss)
        pool_s.append(ps)
    return (yp, ys, jnp.stack(conv_p), jnp.stack(delta_p), jnp.stack(pool_p),
            jnp.stack(conv_s), jnp.stack(delta_s), jnp.stack(pool_s))
```

```python
import functools

import jax
import jax.numpy as jnp
from jax import lax
from jax.experimental import pallas as pl
from jax.experimental.pallas import tpu as pltpu
from jax.experimental.pallas import tpu_sc as plsc

F32 = jnp.float32
BF16 = jnp.bfloat16
I32 = jnp.int32

D_MODEL = 1024
DEPTH = 1
PAST_LEN = 16384
DN_HEADS = 8
DN_HEAD_DIM = 128
DN_WIDTH = DN_HEADS * DN_HEAD_DIM
QKV_WIDTH = 3 * DN_WIDTH
CONV_WIDTH = 4
DN_CHUNK = 64
POOL_WINDOWS = (2, 4, 8, 16)
POOL_GROUP_DIM = 128
POOL_WIDTH = len(POOL_WINDOWS) * POOL_GROUP_DIM
POOL_OUT_GROUP = D_MODEL // len(POOL_WINDOWS)
POOL_BUF = max(POOL_WINDOWS) - 1
PEER_HEADS = 8
PEER_NKEYS = 128
PEER_TOPK = 16
PEER_KEY_HALF = 128
PEER_HK = PEER_HEADS * PEER_TOPK
EPS = 1e-6

LANES = 128
SUBLANES = 8
CONV_PAD = SUBLANES
POOL_PAD = 16
VMEM_LIMIT = 56 * 1024 * 1024

NT_DIMS = (((1,), (1,)), ((), ()))
TN_DIMS = (((0,), (0,)), ((), ()))


def _dot(a, b):
    return jnp.dot(a.astype(BF16), b.astype(BF16), preferred_element_type=F32)


def _dot_nt(a, b):
    return lax.dot_general(a.astype(BF16), b.astype(BF16), NT_DIMS, preferred_element_type=F32)


def _split3(x):
    hi = x.astype(BF16)
    r1 = x - hi.astype(F32)
    mid = r1.astype(BF16)
    lo = (r1 - mid.astype(F32)).astype(BF16)
    return hi, mid, lo


def _silu(x):
    return x * jax.nn.sigmoid(x)


def _gelu(x):
    return 0.5 * x * (1.0 + lax.erf(x * (0.5 ** 0.5)))


def _softplus(x):
    return jnp.maximum(x, 0.0) + jnp.log(1.0 + jnp.exp(-jnp.abs(x)))


def _mod_rows(ref):
    m = ref[...]
    return m.reshape(m.shape[-2], m.shape[-1])


def _mod_spec(k, rows_per_batch, tm):
    if rows_per_batch >= tm:
        tiles = rows_per_batch // tm
        return pl.BlockSpec((1, 1, 1, D_MODEL), lambda i, *_: (k, i // tiles, 0, 0))
    return pl.BlockSpec((1, tm, D_MODEL), lambda i, *_: (k, i, 0))


def _const_spec(shape):
    nd = len(shape)
    return pl.BlockSpec(shape, lambda *_: (0,) * nd)


def _ada_body(c_ref, w_ref, b_ref, o_ref):
    o_ref[...] = _dot(_silu(c_ref[...]), w_ref[...]) + b_ref[...]


def _ada(c, w_ada, b_ada):
    n = c.shape[0]
    return pl.pallas_call(
        _ada_body,
        grid=(6,),
        in_specs=[pl.BlockSpec((n, D_MODEL), lambda j: (0, 0)),
                  pl.BlockSpec((D_MODEL, D_MODEL), lambda j: (0, j)),
                  pl.BlockSpec((1, D_MODEL), lambda j: (0, j))],
        out_specs=pl.BlockSpec((n, D_MODEL), lambda j: (0, j)),
        out_shape=jax.ShapeDtypeStruct((n, 6 * D_MODEL), F32),
        name="ada",
    )(c, w_ada, b_ada.reshape(1, -1))


_IN_BLOCKS = (("qkv", QKV_WIDTH), ("z", DN_WIDTH), ("pool", POOL_WIDTH),
              ("ga", D_MODEL), ("gb", D_MODEL), ("ba", LANES))
_IN_TOTAL = sum(w for _, w in _IN_BLOCKS)
_IN_F32 = ("ba",)
_IN_COL_CHUNK = 512


def _inproj_body(x_ref, sc_ref, sh_ref, g_ref, w_ref, *out_refs):
    x = x_ref[...]
    y = x * lax.rsqrt(jnp.mean(x * x, axis=-1, keepdims=True) + EPS) * g_ref[...]
    h = (y * (1.0 + _mod_rows(sc_ref)) + _mod_rows(sh_ref)).astype(BF16)
    off = 0
    for (_, width), o_ref in zip(_IN_BLOCKS, out_refs):
        for c0 in range(0, width, _IN_COL_CHUNK):
            cw = min(_IN_COL_CHUNK, width - c0)
            o_ref[:, c0:c0 + cw] = jnp.dot(h, w_ref[:, off + c0:off + c0 + cw],
                                           preferred_element_type=F32).astype(o_ref.dtype)
        off += width


def _inproj(x2d, mod, rows_per_batch, norm_g, w_cat, tm):
    t = x2d.shape[0]
    row = lambda w: pl.BlockSpec((tm, w), lambda i: (i, 0))
    return pl.pallas_call(
        _inproj_body,
        grid=(t // tm,),
        in_specs=[row(D_MODEL), _mod_spec(1, rows_per_batch, tm), _mod_spec(0, rows_per_batch, tm),
                  _const_spec((1, D_MODEL)),
                  pl.BlockSpec((D_MODEL, _IN_TOTAL), lambda i: (0, 0), pipeline_mode=pl.Buffered(1))],
        out_specs=[row(w) for _, w in _IN_BLOCKS],
        out_shape=[jax.ShapeDtypeStruct((t, w), F32 if name in _IN_F32 else BF16) for name, w in _IN_BLOCKS],
        compiler_params=pltpu.CompilerParams(vmem_limit_bytes=VMEM_LIMIT),
        name="inproj",
    )(x2d, mod, mod, norm_g.reshape(1, -1), w_cat)


def _mixer_body(C, Lv, start,
                qkv_ref, ba_ref, z_ref, pin_ref, ga_ref, gb_ref, cbuf_ref, s0_ref, pbuf_ref,
                convw_ref, alog_ref, dtb_ref, dng_ref, wpool_ref, pscale_ref,
                mixed_ref, nconv_ref, ns_ref, npool_ref,
                xp_scr, act_scr, s_scr, pp_scr, odn_scr):
    n = pl.program_id(1)
    last = pl.num_programs(1) - 1

    @pl.when(n == 0)
    def _load_state():
        xp_scr[0:CONV_PAD, :] = cbuf_ref[0]
        pp_scr[0:POOL_PAD, :] = pbuf_ref[0]
        s_scr[...] = s0_ref[0]

    xp_scr[CONV_PAD:CONV_PAD + C, :] = qkv_ref[0].astype(F32)
    for c0 in range(0, QKV_WIDTH, _IN_COL_CHUNK):
        cs = slice(c0, c0 + _IN_COL_CHUNK)
        y = xp_scr[CONV_PAD:CONV_PAD + C, cs] * convw_ref[CONV_WIDTH - 1:CONV_WIDTH, cs]
        for k in range(CONV_WIDTH - 1):
            r0 = CONV_PAD - (CONV_WIDTH - 1) + k
            y = y + xp_scr[r0:r0 + C, cs] * convw_ref[k:k + 1, cs]
        act_scr[:, cs] = _silu(y)

    ba = ba_ref[0]
    lane = lax.broadcasted_iota(I32, (C, LANES), 1)
    beta_all = jax.nn.sigmoid(ba)
    g_all = -jnp.exp(alog_ref[...]) * _softplus(ba + dtb_ref[...])
    if Lv < C:
        valid = lax.broadcasted_iota(I32, (C, LANES), 0) < Lv
        beta_all = jnp.where(valid, beta_all, 0.0)
        g_all = jnp.where(valid, g_all, 0.0)
    ii = lax.broadcasted_iota(I32, (C, C), 0)
    jj = lax.broadcasted_iota(I32, (C, C), 1)
    causal = ii >= jj
    strict = ii > jj
    tril = jnp.where(causal, 1.0, 0.0).astype(BF16)
    eye = jnp.where(ii == jj, 1.0, 0.0)
    gc_all = sum(jnp.dot(tril, part, preferred_element_type=F32) for part in _split3(g_all))
    if C < LANES:
        gc_sq = jnp.concatenate([gc_all, jnp.zeros((LANES - C, LANES), F32)], axis=0)
    else:
        gc_sq = gc_all
    gc_t = gc_sq.T

    H = range(DN_HEADS)
    hsl = [slice(h * DN_HEAD_DIM, (h + 1) * DN_HEAD_DIM) for h in H]
    beta = [jnp.sum(jnp.where(lane == h, beta_all, 0.0), axis=1, keepdims=True) for h in H]
    gcol = [jnp.sum(jnp.where(lane == DN_HEADS + h, gc_all, 0.0), axis=1, keepdims=True) for h in H]
    grow = [gc_t[DN_HEADS + h:DN_HEADS + h + 1, 0:C] for h in H]
    glast = [g[C - 1:C, :] for g in gcol]
    q = [act_scr[:, hsl[h]] for h in H]
    k = [act_scr[:, DN_WIDTH + h * DN_HEAD_DIM:DN_WIDTH + (h + 1) * DN_HEAD_DIM] for h in H]
    v = [act_scr[:, 2 * DN_WIDTH + h * DN_HEAD_DIM:2 * DN_WIDTH + (h + 1) * DN_HEAD_DIM] for h in H]
    q = [x * lax.rsqrt(jnp.sum(x * x, axis=-1, keepdims=True) + EPS) * (DN_HEAD_DIM ** -0.5) for x in q]
    k = [x * lax.rsqrt(jnp.sum(x * x, axis=-1, keepdims=True) + EPS) for x in k]
    kb = [k[h] * beta[h] for h in H]
    vb = [v[h] * beta[h] for h in H]
    decay = [jnp.where(causal, jnp.exp(jnp.where(causal, gcol[h] - grow[h], 0.0)), 0.0) for h in H]
    lower = [jnp.where(strict, _dot_nt(kb[h], k[h]) * decay[h], 0.0) for h in H]
    ainv = [eye - x for x in lower]
    pw = lower
    p = 1
    while 2 * p < C:
        pw = [_dot(x, x) for x in pw]
        ainv = [ainv[h] + _dot(ainv[h], pw[h]) for h in H]
        p *= 2
    sol = [_dot(ainv[h], jnp.concatenate([vb[h], kb[h] * jnp.exp(gcol[h])], axis=1)) for h in H]
    qk = [_dot_nt(q[h], k[h]) * decay[h] for h in H]
    k_tail = [k[h] * jnp.exp(glast[h] - gcol[h]) for h in H]
    S = [s_scr[h] for h in H]
    v_new = [sol[h][:, :DN_HEAD_DIM] - _dot(sol[h][:, DN_HEAD_DIM:], S[h]) for h in H]
    o = [_dot(q[h] * jnp.exp(gcol[h]), S[h]) + _dot(qk[h], v_new[h]) for h in H]
    for h in H:
        s_scr[h] = S[h] * jnp.exp(glast[h]) + lax.dot_general(
            k_tail[h].astype(BF16), v_new[h].astype(BF16), TN_DIMS, preferred_element_type=F32)
    for h in H:
        zf = z_ref[0, :, hsl[h]].astype(F32)
        odn_scr[:, hsl[h]] = (o[h] * lax.rsqrt(jnp.mean(o[h] * o[h], axis=-1, keepdims=True) + EPS)
                              * dng_ref[...] * _silu(zf))

    pp_scr[POOL_PAD:POOL_PAD + C, :] = pin_ref[0].astype(F32)
    pos = start + n * C + lax.broadcasted_iota(I32, (C, 1), 0)
    for gi, win in enumerate(POOL_WINDOWS):
        gs = slice(gi * POOL_GROUP_DIM, (gi + 1) * POOL_GROUP_DIM)
        xg = pp_scr[POOL_PAD:POOL_PAD + C, gs]
        ssum = xg
        for sft in range(1, win):
            ssum = ssum + pp_scr[POOL_PAD - sft:POOL_PAD - sft + C, gs]
        cnt = jnp.minimum(pos + 1, win).astype(F32)
        pooled = ssum / cnt - xg
        os_ = slice(gi * POOL_OUT_GROUP, (gi + 1) * POOL_OUT_GROUP)
        yp = _dot(pooled, wpool_ref[gi]) * pscale_ref[:, os_]
        mixed_ref[0, :, os_] = (jax.nn.sigmoid(ga_ref[0, :, os_].astype(F32)) * odn_scr[:, os_]
                                + jax.nn.sigmoid(gb_ref[0, :, os_].astype(F32)) * yp)

    @pl.when(n == last)
    def _store_state():
        nconv_ref[0] = xp_scr[Lv + CONV_PAD - (CONV_WIDTH - 1):Lv + CONV_PAD, :]
        npool_ref[0] = pp_scr[Lv + POOL_PAD - POOL_BUF:Lv + POOL_PAD, :]
        ns_ref[0] = s_scr[...]

    xp_scr[0:CONV_PAD, :] = xp_scr[C:C + CONV_PAD, :]
    pp_scr[0:POOL_PAD, :] = pp_scr[C:C + POOL_PAD, :]


def _mixer(proj, conv_buf, s0, pool_buf, start, seq_len, C,
           conv_w, a_log, dt_bias, dn_norm_g, w_pool, pool_scale):
    b, lp, _ = proj["qkv"].shape
    nchunks = lp // C
    lv = seq_len - (nchunks - 1) * C
    cbuf = jnp.pad(conv_buf, ((0, 0), (CONV_PAD - (CONV_WIDTH - 1), 0), (0, 0)))
    pbuf = jnp.pad(pool_buf, ((0, 0), (POOL_PAD - POOL_BUF, 0), (0, 0)))
    lane_pad = lambda a: jnp.pad(a.reshape(1, -1), ((0, 0), (DN_HEADS, LANES - 2 * DN_HEADS)))
    chunk = lambda w: pl.BlockSpec((1, C, w), lambda i, j: (i, j, 0))
    state = lambda *s: pl.BlockSpec((1,) + s, lambda i, j: (i,) + (0,) * len(s))
    return pl.pallas_call(
        functools.partial(_mixer_body, C, lv, start),
        grid=(b, nchunks),
        in_specs=[chunk(QKV_WIDTH), chunk(LANES), chunk(DN_WIDTH), chunk(POOL_WIDTH),
                  chunk(D_MODEL), chunk(D_MODEL),
                  state(CONV_PAD, QKV_WIDTH), state(DN_HEADS, DN_HEAD_DIM, DN_HEAD_DIM),
                  state(POOL_PAD, POOL_WIDTH),
                  _const_spec((CONV_WIDTH, QKV_WIDTH)), _const_spec((1, LANES)), _const_spec((1, LANES)),
                  _const_spec((1, DN_HEAD_DIM)),
                  _const_spec((len(POOL_WINDOWS), POOL_GROUP_DIM, POOL_OUT_GROUP)),
                  _const_spec((1, D_MODEL))],
        out_specs=[chunk(D_MODEL), state(CONV_WIDTH - 1, QKV_WIDTH),
                   state(DN_HEADS, DN_HEAD_DIM, DN_HEAD_DIM), state(POOL_BUF, POOL_WIDTH)],
        out_shape=[jax.ShapeDtypeStruct((b, lp, D_MODEL), F32),
                   jax.ShapeDtypeStruct((b, CONV_WIDTH - 1, QKV_WIDTH), F32),
                   jax.ShapeDtypeStruct((b, DN_HEADS, DN_HEAD_DIM, DN_HEAD_DIM), F32),
                   jax.ShapeDtypeStruct((b, POOL_BUF, POOL_WIDTH), F32)],
        scratch_shapes=[pltpu.VMEM((CONV_PAD + C + CONV_PAD, QKV_WIDTH), F32),
                        pltpu.VMEM((C, QKV_WIDTH), F32),
                        pltpu.VMEM((DN_HEADS, DN_HEAD_DIM, DN_HEAD_DIM), F32),
                        pltpu.VMEM((POOL_PAD + C + POOL_PAD, POOL_WIDTH), F32),
                        pltpu.VMEM((C, DN_WIDTH), F32)],
        compiler_params=pltpu.CompilerParams(dimension_semantics=("arbitrary", "arbitrary"),
                                             vmem_limit_bytes=VMEM_LIMIT),
        name="mixer",
    )(proj["qkv"], proj["ba"], proj["z"], proj["pool"], proj["ga"], proj["gb"], cbuf, s0, pbuf,
      conv_w, lane_pad(a_log), lane_pad(dt_bias), dn_norm_g.reshape(1, -1), w_pool,
      pool_scale.reshape(1, -1))


def _top16(s, ids, payload=None):
    big = float(2 ** 24)
    vals, sel, pays = [], [], []
    for _ in range(PEER_TOPK):
        m = jnp.max(s, axis=0, keepdims=True)
        am = jnp.min(jnp.where(s == m, ids, big), axis=0, keepdims=True)
        hit = ids == am
        if payload is not None:
            pays.append(jnp.max(jnp.where(hit, payload, -1.0), axis=0, keepdims=True))
        s = jnp.where(hit, -jnp.inf, s)
        vals.append(m)
        sel.append(am)
    out = (jnp.concatenate(vals, axis=0), jnp.concatenate(sel, axis=0))
    if payload is not None:
        out += (jnp.concatenate(pays, axis=0),)
    return out


_CAND_EDGE = 4


def _post_body(has_prev, has_fin, mixed_ref, x_ref, g1_ref, sc2_ref, sh2_ref, n2g_ref, wout_ref,
               wq_ref, keys_ref, *refs):
    refs = list(refs)
    prev_in = [refs.pop(0) for _ in range(2 if has_prev else 0)]
    fin_in = [refs.pop(0) for _ in range(4 if has_fin else 0)]
    x1_ref, h2_ref, idx_ref, gate_ref = refs[:4]
    extra_out = refs[4:]
    if has_prev:
        pre_ref, pgate_ref = prev_in
        extra_out.pop(0)[...] = _coef_words(pre_ref[...], pgate_ref[...])
    if has_fin:
        _final_body(*fin_in, extra_out.pop(0))
    tm = x_ref.shape[0]
    x1 = x_ref[...] + _mod_rows(g1_ref) * _dot(mixed_ref[...], wout_ref[...])
    x1_ref[...] = x1
    y = x1 * lax.rsqrt(jnp.mean(x1 * x1, axis=-1, keepdims=True) + EPS) * n2g_ref[...]
    h2 = y * (1.0 + _mod_rows(sc2_ref)) + _mod_rows(sh2_ref)
    h2_ref[...] = _pack_words(h2[:, :PACK_HALF], h2[:, PACK_HALF:])
    q = _dot(h2, wq_ref[...])

    K = PEER_TOPK
    key_id = lax.broadcasted_iota(I32, (PEER_NKEYS, 1), 0).astype(F32)
    r16 = lax.broadcasted_iota(I32, (K, 1), 0)
    cand_id = jnp.concatenate([(a * K + r16) for a in range(_CAND_EDGE)]
                              + [(r16 * K + b) for b in range(_CAND_EDGE)], axis=0).astype(F32)
    dup = r16 < _CAND_EDGE
    idx_rows, gate_rows = [], []
    for h in range(PEER_HEADS):
        half = []
        for p in range(2):
            c0 = (h * 2 + p) * PEER_KEY_HALF
            st = _dot_nt(keys_ref[h * 2 + p], q[:, c0:c0 + PEER_KEY_HALF])
            half.append(_top16(st, key_id))
        (s1, i1), (s2, i2) = half
        cand = jnp.concatenate(
            [s1[a:a + 1] + s2 for a in range(_CAND_EDGE)]
            + [jnp.where(dup, -jnp.inf, s1 + s2[b:b + 1]) for b in range(_CAND_EDGE)], axis=0)
        cidx = jnp.concatenate(
            [i1[a:a + 1] * PEER_NKEYS + i2 for a in range(_CAND_EDGE)]
            + [i1 * PEER_NKEYS + i2[b:b + 1] for b in range(_CAND_EDGE)], axis=0)
        best, _, eidx = _top16(cand, cand_id, cidx)
        e = jnp.exp(best - best[0:1])
        gate_rows.append(e / jnp.sum(e, axis=0, keepdims=True))
        idx_rows.append(eidx)
    idx_ref[...] = jnp.concatenate(idx_rows, axis=0).T.astype(I32)
    gate_ref[...] = jnp.concatenate(gate_rows, axis=0).T


def _post(mixed2d, x2d, mod, rows_per_batch, norm2_g, w_out, w_query, keys, tm, prev=None, fin=None):
    t = x2d.shape[0]
    steps = t // tm
    row = lambda w: pl.BlockSpec((tm, w), lambda i: (i, 0))
    in_specs = [row(D_MODEL), row(D_MODEL),
                _mod_spec(2, rows_per_batch, tm), _mod_spec(4, rows_per_batch, tm),
                _mod_spec(3, rows_per_batch, tm), _const_spec((1, D_MODEL)),
                _const_spec((D_MODEL, D_MODEL)), _const_spec((D_MODEL, 2 * PEER_HEADS * PEER_KEY_HALF)),
                _const_spec((2 * PEER_HEADS, PEER_NKEYS, PEER_KEY_HALF))]
    out_specs = [row(D_MODEL), row(PACK_HALF), row(PEER_HK), row(PEER_HK)]
    out_shape = [jax.ShapeDtypeStruct((t, D_MODEL), F32), jax.ShapeDtypeStruct((t, PACK_HALF), I32),
                 jax.ShapeDtypeStruct((t, PEER_HK), I32), jax.ShapeDtypeStruct((t, PEER_HK), F32)]
    args = [mixed2d, x2d, mod, mod, mod, norm2_g.reshape(1, -1), w_out, w_query, keys]
    if prev is not None:
        tp = prev[0].shape[0]
        prow = pl.BlockSpec((tp // steps, PEER_HK), lambda i: (i, 0))
        in_specs += [prow, prow]
        out_specs += [prow]
        out_shape += [jax.ShapeDtypeStruct((tp, PEER_HK), I32)]
        args += list(prev)
    if fin is not None:
        x1_f, peer_f, mod_f, rows_f, final_g = fin
        tf = x1_f.shape[0]
        frow = pl.BlockSpec((tf // steps, D_MODEL), lambda i: (i, 0))
        in_specs += [frow, frow, _mod_spec(5, rows_f, tf // steps), _const_spec((1, D_MODEL))]
        out_specs += [frow]
        out_shape += [jax.ShapeDtypeStruct((tf, D_MODEL), F32)]
        args += [x1_f, peer_f, mod_f, final_g.reshape(1, -1)]
    return pl.pallas_call(
        functools.partial(_post_body, prev is not None, fin is not None),
        grid=(steps,),
        in_specs=in_specs, out_specs=out_specs, out_shape=out_shape,
        compiler_params=pltpu.CompilerParams(vmem_limit_bytes=VMEM_LIMIT),
        name="post",
    )(*args)


SC_CORES = 2
SC_SUBCORES = 16
SC_LANES = 16
SC_WORKERS = SC_CORES * SC_SUBCORES
SC_TOKENS = 32
SC_SLOTS = 4
SC_JOB_HEADS = 2
SC_BF16_GROUP = 4
PACK_HALF = D_MODEL // 2
SC_CHUNKS = PACK_HALF // SC_LANES
PROMPT_PARTS = 8
EDGE_SPLITS = 2
COEF_LAG = 2
FIN_LAG = 3
ROW_TILE = 256


def _bf16_bits(v):
    return lax.bitcast_convert_type(v.astype(BF16).astype(F32), jnp.uint32)


def _pack_words(lo, hi):
    return lax.bitcast_convert_type((_bf16_bits(lo) >> 16) | _bf16_bits(hi), I32)


def _pack_body(x_ref, o_ref):
    o_ref[...] = _pack_words(x_ref[:, :PACK_HALF], x_ref[:, PACK_HALF:])


def _pack_table(tbl, rows=2 * ROW_TILE):
    e = tbl.shape[0]
    return pl.pallas_call(
        _pack_body, grid=(e // rows,),
        in_specs=[pl.BlockSpec((rows, D_MODEL), lambda i: (i, 0))],
        out_specs=pl.BlockSpec((rows, PACK_HALF), lambda i: (i, 0)),
        out_shape=jax.ShapeDtypeStruct((e, PACK_HALF), I32), name="pack_table")(tbl)


def _tree_sum(terms):
    terms = list(terms)
    while len(terms) > 1:
        terms = [a + b for a, b in zip(terms[0::2], terms[1::2])] + terms[len(terms) & ~1:]
    return terms[0]


def _unpack_pair(w):
    lo = plsc.bitcast(lax.shift_left(w, jnp.full(w.shape, 16, I32)), F32)
    hi = plsc.bitcast(w & jnp.full(w.shape, -65536, I32), F32)
    return lo, hi


def _sc_mesh():
    return plsc.VectorSubcoreMesh(core_axis_name="c", subcore_axis_name="s")


def _sc_worker():
    return lax.axis_index("s") * SC_CORES + lax.axis_index("c")


def _sc_jobs(table_hbm, idx_v, buf, sem, compute):
    per_tok = PEER_HEADS // SC_JOB_HEADS
    njobs = idx_v.shape[0] * per_tok
    nrows = SC_JOB_HEADS * PEER_TOPK

    def copy(j, slot):
        rows = idx_v.at[j // per_tok, pl.ds((j % per_tok) * nrows, nrows)]
        return pltpu.make_async_copy(table_hbm.at[rows], buf.at[slot], sem.at[slot])

    for s in range(SC_SLOTS):
        copy(s, s).start()

    def job(j, c):
        s = j % SC_SLOTS
        copy(j, s).wait()

        def head(i, cc):
            compute(j // per_tok, (j % per_tok) * SC_JOB_HEADS + i, s, i * PEER_TOPK)
            return cc
        lax.fori_loop(0, SC_JOB_HEADS, head, 0)

        @pl.when(j + SC_SLOTS < njobs)
        def _next():
            copy(j + SC_SLOTS, s).start()
        return c

    lax.fori_loop(0, njobs, job, 0)


def _peer_u_body(n_tok, idx_hbm, h2_hbm, u_hbm, pre_hbm, idx_v, h2_v, pre_v, ubuf, acc_v, sem):
    base = _sc_worker() * n_tok
    lane = lax.iota(I32, SC_LANES)

    def compute(tt, h, slot, r0):
        def chunk(cg, accs):
            cs = [pl.ds((cg * SC_BF16_GROUP + i) * SC_LANES, SC_LANES) for i in range(SC_BF16_GROUP)]
            xs = [plsc.bitcast(h2_v[tt, c], BF16) for c in cs]
            out = []
            for k, a in enumerate(accs):
                part = _tree_sum([plsc.bitcast(ubuf[slot, r0 + k, c], BF16) * x for c, x in zip(cs, xs)])
                lo, hi = _unpack_pair(plsc.bitcast(part, I32))
                out.append(a + (lo + hi))
            return tuple(out)
        zero = jnp.zeros((SC_LANES,), F32)
        accs = lax.fori_loop(0, SC_CHUNKS // SC_BF16_GROUP, chunk, (zero,) * PEER_TOPK)
        for k, a in enumerate(accs):
            acc_v[k, :] = a
        tot = zero
        for j in range(SC_LANES):
            tot = tot + plsc.load_gather(acc_v, [lane, (lane + j) & (SC_LANES - 1)])
        pre_v[tt, pl.ds(h * PEER_TOPK, PEER_TOPK)] = tot

    tb = idx_v.shape[0]

    def block(bi, c):
        t0 = base + bi * tb
        pltpu.sync_copy(idx_hbm.at[pl.ds(t0, tb)], idx_v)
        pltpu.sync_copy(h2_hbm.at[pl.ds(t0, tb)], h2_v)
        _sc_jobs(u_hbm, idx_v, ubuf, sem, compute)
        pltpu.sync_copy(pre_v, pre_hbm.at[pl.ds(t0, tb)])
        return c

    lax.fori_loop(0, n_tok // tb, block, 0)


def _peer_v_body(n_tok, idx_hbm, coef_hbm, v_hbm, out_hbm, idx_v, coef_v, out_v, vbuf, sem):
    base = _sc_worker() * n_tok
    zero = jnp.zeros((SC_LANES,), F32)

    def compute(tt, h, slot, r0):
        cvec = coef_v[tt, pl.ds(h * PEER_TOPK, PEER_TOPK)]
        cb = [plsc.bitcast(jnp.take_along_axis(cvec, jnp.full((SC_LANES,), k, I32), axis=0), BF16)
              for k in range(PEER_TOPK)]

        @plsc.parallel_loop(0, SC_CHUNKS, unroll=2)
        def _chunk(c):
            cs = pl.ds(c * SC_LANES, SC_LANES)
            prods = [plsc.bitcast(vbuf[slot, r0 + k, cs], BF16) * cb[k] for k in range(PEER_TOPK)]
            pairs = [_unpack_pair(plsc.bitcast(_tree_sum(prods[g:g + SC_BF16_GROUP]), I32))
                     for g in range(0, PEER_TOPK, SC_BF16_GROUP)]
            for half, off in ((0, 0), (1, PACK_HALF)):
                plsc.addupdate(out_v.at[tt, pl.ds(off + c * SC_LANES, SC_LANES)],
                               _tree_sum([p[half] for p in pairs]))

    tb = idx_v.shape[0]

    def block(bi, c):
        t0 = base + bi * tb
        pltpu.sync_copy(idx_hbm.at[pl.ds(t0, tb)], idx_v)
        pltpu.sync_copy(coef_hbm.at[pl.ds(t0, tb)], coef_v)

        def clear(i, cc):
            per_row = D_MODEL // SC_LANES
            out_v[i // per_row, pl.ds((i % per_row) * SC_LANES, SC_LANES)] = zero
            return cc
        lax.fori_loop(0, tb * (D_MODEL // SC_LANES), clear, 0)
        _sc_jobs(v_hbm, idx_v, vbuf, sem, compute)
        pltpu.sync_copy(out_v, out_hbm.at[pl.ds(t0, tb)])
        return c

    lax.fori_loop(0, n_tok // tb, block, 0)


def _peer_sc(body, idx, rows, table, out_width, name):
    t = idx.shape[0]
    assert t % SC_WORKERS == 0
    n_tok = t // SC_WORKERS
    tb = min(SC_TOKENS * (2 if body is _peer_u_body else 1), n_tok)
    assert n_tok % tb == 0 and tb * PEER_HEADS // SC_JOB_HEADS >= SC_SLOTS
    return pl.kernel(
        functools.partial(body, n_tok),
        out_type=jax.ShapeDtypeStruct((t, out_width), F32),
        mesh=_sc_mesh(),
        scratch_types=[pltpu.VMEM((tb, PEER_HK), I32),
                       pltpu.VMEM((tb, rows.shape[1]), rows.dtype),
                       pltpu.VMEM((tb, out_width), F32),
                       pltpu.VMEM((SC_SLOTS, SC_JOB_HEADS * PEER_TOPK, PACK_HALF), I32)]
                      + ([pltpu.VMEM((PEER_TOPK, SC_LANES), F32)] if body is _peer_u_body else [])
                      + [pltpu.SemaphoreType.DMA((SC_SLOTS,))],
        compiler_params=pltpu.CompilerParams(needs_layout_passes=False),
        name=name,
    )(idx, rows, table)


def _coef_words(pre, gates):
    return _pack_words(*(gates * _gelu(pre),) * 2)


def _coef_body(pre_ref, gate_ref, coef_ref):
    coef_ref[...] = _coef_words(pre_ref[...], gate_ref[...])


def _coef(pre, gates, tm):
    t = pre.shape[0]
    row = pl.BlockSpec((tm, PEER_HK), lambda i: (i, 0))
    return pl.pallas_call(_coef_body, grid=(t // tm,), in_specs=[row, row], out_specs=row,
                          out_shape=jax.ShapeDtypeStruct((t, PEER_HK), I32), name="coef")(pre, gates)


def _final_body(x1_ref, peer_ref, g2_ref, fng_ref, y_ref):
    x2 = x1_ref[...] + _mod_rows(g2_ref) * peer_ref[...]
    y_ref[...] = x2 * lax.rsqrt(jnp.mean(x2 * x2, axis=-1, keepdims=True) + EPS) * fng_ref[...]


def _final(x1, peer_out, mod, rows_per_batch, final_g, tm):
    t = x1.shape[0]
    row = pl.BlockSpec((tm, D_MODEL), lambda i: (i, 0))
    return pl.pallas_call(
        _final_body, grid=(t // tm,),
        in_specs=[row, row, _mod_spec(5, rows_per_batch, tm), _const_spec((1, D_MODEL))],
        out_specs=row, out_shape=jax.ShapeDtypeStruct((t, D_MODEL), F32), name="final",
    )(x1, peer_out, mod, final_g.reshape(1, -1))


def _expert_gather_v(g, coef, expert_v):
    g["peer_out"] = _peer_sc(_peer_v_body, g["idx"], coef, expert_v, D_MODEL, "peer_v")


def _front(x, mod, conv_buf, s0, pool_buf, start, chunk, tm, wts, prev, fin):
    b, l, _ = x.shape
    t = b * l
    x2d = x.reshape(t, D_MODEL)
    if l >= tm:
        modx = mod.reshape(b, 6, 1, D_MODEL).transpose(1, 0, 2, 3)
    else:
        modx = jnp.repeat(mod.reshape(b, 6, D_MODEL), l, axis=0).transpose(1, 0, 2)
    outs = _inproj(x2d, modx, l, wts["norm1_g"], wts["w_cat"], tm)
    lp = -(-l // chunk) * chunk
    proj = {}
    for (name, w), a in zip(_IN_BLOCKS, outs):
        a = a.reshape(b, l, w)
        proj[name] = a if lp == l else jnp.pad(a, ((0, 0), (0, lp - l), (0, 0)))
    mixed, nconv, ns, npool = _mixer(proj, conv_buf, s0, pool_buf, start, l, chunk,
                                     wts["conv_w"], wts["a_log"], wts["dt_bias"], wts["dn_norm_g"],
                                     wts["w_pool"], wts["pool_scale"])
    mixed2d = mixed[:, :l].reshape(t, D_MODEL)
    res = _post(mixed2d, x2d, modx, l, wts["norm2_g"], wts["w_out"], wts["w_query"], wts["keys"], tm,
                prev=None if prev is None else (prev["pre"], prev["gates"]),
                fin=None if fin is None else (fin["x1"], fin["peer_out"], fin["mod"], fin["l"],
                                              wts["final_norm_g"]))
    x1, h2, idx, gates = res[:4]
    extra = list(res[4:])
    coef_prev = extra.pop(0) if prev is not None else None
    y_fin = extra.pop(0).reshape(fin["b"], fin["l"], D_MODEL) if fin is not None else None
    pre = _peer_sc(_peer_u_body, idx, h2, wts["expert_u"], PEER_HK, "peer_u")
    g = dict(x1=x1, idx=idx, gates=gates, pre=pre, mod=modx, b=b, l=l, tm=tm,
             states=(nconv, ns, npool))
    return g, coef_prev, y_fin


def kernel(x_prompt, x_sample, c_prompt, c_sample, state_conv, state_delta, state_pool, w_ada, b_ada, norm1_g, w_in, conv_w, a_log, dt_bias, dn_norm_g, w_pool, pool_scale, w_out, norm2_g, w_query, sub_keys, expert_u, expert_v, final_norm_g):
    bp = x_prompt.shape[0]
    yp, ys = x_prompt, x_sample
    conv_p, delta_p, pool_p, conv_s, delta_s, pool_s = [], [], [], [], [], []
    zero_conv = jnp.zeros((bp, CONV_WIDTH - 1, QKV_WIDTH), F32)
    zero_delta = jnp.zeros((bp, DN_HEADS, DN_HEAD_DIM, DN_HEAD_DIM), F32)
    zero_pool = jnp.zeros((bp, POOL_BUF, POOL_WIDTH), F32)
    c_all = jnp.concatenate([c_prompt, c_sample], axis=0)
    for layer in range(DEPTH):
        wi = w_in[layer]
        o_b = QKV_WIDTH
        o_z = o_b + 2 * DN_HEADS
        w_ba = jnp.pad(wi[:, o_b:o_z], ((0, 0), (0, LANES - 2 * DN_HEADS)))
        w_cat = jnp.concatenate([wi[:, :o_b], wi[:, o_z:], w_ba], axis=1).astype(BF16)
        last = layer == DEPTH - 1
        wts = dict(
            norm1_g=norm1_g[layer], w_cat=w_cat, conv_w=conv_w[layer], a_log=a_log[layer],
            dt_bias=dt_bias[layer], dn_norm_g=dn_norm_g[layer], w_pool=w_pool[layer],
            pool_scale=pool_scale[layer], w_out=w_out[layer].astype(BF16), norm2_g=norm2_g[layer],
            w_query=w_query[layer].astype(BF16),
            keys=sub_keys[layer].reshape(2 * PEER_HEADS, PEER_NKEYS, PEER_KEY_HALF).astype(BF16),
            expert_u=_pack_table(expert_u[layer]), expert_v=_pack_table(expert_v[layer]),
            final_norm_g=final_norm_g if last else jnp.ones_like(final_norm_g))
        mod = _ada(c_all, w_ada[layer], b_ada[layer])
        assert last, "final norm is fused into the expert stage"
        step = bp // PROMPT_PARTS
        seq = x_prompt.shape[1]
        zeros = (zero_conv[:step], zero_delta[:step], zero_pool[:step])
        jobs, cuts = [], []
        for b0 in range(0, bp, step):
            n = EDGE_SPLITS if b0 in (0, bp - step) else 1
            cuts.append(n)
            for s0 in range(0, seq, seq // n):
                jobs.append((yp[b0:b0 + step, s0:s0 + seq // n], mod[b0:b0 + step],
                             zeros if s0 == 0 else None, s0, DN_CHUNK))
        jobs.append((ys, mod[bp:], (state_conv[layer], state_delta[layer], state_pool[layer]),
                     PAST_LEN, SUBLANES))
        groups = []
        for j, (xg, mg, states, start, chunk) in enumerate(jobs):
            prev = groups[j - COEF_LAG] if j >= COEF_LAG else None
            fin = groups[j - FIN_LAG] if j >= FIN_LAG else None
            if fin is not None and fin["x1"].shape[0] % (xg.shape[0] * xg.shape[1] // ROW_TILE):
                fin = None
            if states is None:
                states = groups[j - 1]["states"]
            g, coef_prev, y_fin = _front(xg, mg, *states, start, chunk, ROW_TILE, wts, prev, fin)
            if prev is not None:
                _expert_gather_v(prev, coef_prev, wts["expert_v"])
            if fin is not None:
                fin["y"] = y_fin
            groups.append(g)
        for g in groups[-COEF_LAG:]:
            _expert_gather_v(g, _coef(g["pre"], g["gates"], ROW_TILE), wts["expert_v"])
        for g in groups:
            if "y" not in g:
                g["y"] = _final(g["x1"], g["peer_out"], g["mod"], g["l"], wts["final_norm_g"],
                                g["tm"]).reshape(g["b"], g["l"], D_MODEL)
        rows, at = [], 0
        for n in cuts:
            rows.append(groups[at:at + n])
            at += n
        yp = jnp.concatenate([jnp.concatenate([g["y"] for g in row], axis=1) for row in rows], axis=0)
        cp, sp, pp = (jnp.concatenate(a, axis=0) for a in zip(*(row[-1]["states"] for row in rows)))
        ys = groups[-1]["y"]
        cs, ss, ps = groups[-1]["states"]
        conv_p.append(cp)
        delta_p.append(sp)
        pool_p.append(pp)
        conv_s.append(cs)
        delta_s.append(ss)
        pool_s.append(ps)
    return (yp, ys, jnp.stack(conv_p), jnp.stack(delta_p), jnp.stack(pool_p),
            jnp.stack(conv_s), jnp.stack(delta_s), jnp.stack(pool_s))
```

```python
import functools

import jax
import jax.numpy as jnp
from jax import lax
from jax.experimental import pallas as pl
from jax.experimental.pallas import tpu as pltpu
from jax.experimental.pallas import tpu_sc as plsc

F32 = jnp.float32
BF16 = jnp.bfloat16
I32 = jnp.int32

D_MODEL = 1024
DEPTH = 1
PAST_LEN = 16384
DN_HEADS = 8
DN_HEAD_DIM = 128
DN_WIDTH = DN_HEADS * DN_HEAD_DIM
QKV_WIDTH = 3 * DN_WIDTH
CONV_WIDTH = 4
DN_CHUNK = 64
POOL_WINDOWS = (2, 4, 8, 16)
POOL_GROUP_DIM = 128
POOL_WIDTH = len(POOL_WINDOWS) * POOL_GROUP_DIM
POOL_OUT_GROUP = D_MODEL // len(POOL_WINDOWS)
POOL_BUF = max(POOL_WINDOWS) - 1
PEER_HEADS = 8
PEER_NKEYS = 128
PEER_TOPK = 16
PEER_KEY_HALF = 128
PEER_HK = PEER_HEADS * PEER_TOPK
EPS = 1e-6

LANES = 128
SUBLANES = 8
CONV_PAD = SUBLANES
POOL_PAD = 16
VMEM_LIMIT = 56 * 1024 * 1024

NT_DIMS = (((1,), (1,)), ((), ()))
TN_DIMS = (((0,), (0,)), ((), ()))


def _dot(a, b):
    return jnp.dot(a.astype(BF16), b.astype(BF16), preferred_element_type=F32)


def _dot_nt(a, b):
    return lax.dot_general(a.astype(BF16), b.astype(BF16), NT_DIMS, preferred_element_type=F32)


def _split3(x):
    hi = x.astype(BF16)
    r1 = x - hi.astype(F32)
    mid = r1.astype(BF16)
    lo = (r1 - mid.astype(F32)).astype(BF16)
    return hi, mid, lo


def _silu(x):
    return x * jax.nn.sigmoid(x)


def _gelu(x):
    return 0.5 * x * (1.0 + lax.erf(x * (0.5 ** 0.5)))


def _softplus(x):
    return jnp.maximum(x, 0.0) + jnp.log(1.0 + jnp.exp(-jnp.abs(x)))


def _mod_rows(ref):
    m = ref[...]
    return m.reshape(m.shape[-2], m.shape[-1])


def _mod_spec(k, rows_per_batch, tm):
    if rows_per_batch >= tm:
        tiles = rows_per_batch // tm
        return pl.BlockSpec((1, 1, 1, D_MODEL), lambda i, *_: (k, i // tiles, 0, 0))
    return pl.BlockSpec((1, tm, D_MODEL), lambda i, *_: (k, i, 0))


def _const_spec(shape):
    nd = len(shape)
    return pl.BlockSpec(shape, lambda *_: (0,) * nd)


def _ada_body(c_ref, w_ref, b_ref, o_ref):
    o_ref[...] = _dot(_silu(c_ref[...]), w_ref[...]) + b_ref[...]


def _ada(c, w_ada, b_ada):
    n = c.shape[0]
    return pl.pallas_call(
        _ada_body,
        grid=(6,),
        in_specs=[pl.BlockSpec((n, D_MODEL), lambda j: (0, 0)),
                  pl.BlockSpec((D_MODEL, D_MODEL), lambda j: (0, j)),
                  pl.BlockSpec((1, D_MODEL), lambda j: (0, j))],
        out_specs=pl.BlockSpec((n, D_MODEL), lambda j: (0, j)),
        out_shape=jax.ShapeDtypeStruct((n, 6 * D_MODEL), F32),
        name="ada",
    )(c, w_ada, b_ada.reshape(1, -1))


_IN_BLOCKS = (("qkv", QKV_WIDTH), ("z", DN_WIDTH), ("pool", POOL_WIDTH),
              ("ga", D_MODEL), ("gb", D_MODEL), ("ba", LANES))
_IN_TOTAL = sum(w for _, w in _IN_BLOCKS)
_IN_F32 = ("ba",)
_IN_COL_CHUNK = 512


def _inproj_body(x_ref, sc_ref, sh_ref, g_ref, w_ref, *out_refs):
    x = x_ref[...]
    y = x * lax.rsqrt(jnp.mean(x * x, axis=-1, keepdims=True) + EPS) * g_ref[...]
    h = (y * (1.0 + _mod_rows(sc_ref)) + _mod_rows(sh_ref)).astype(BF16)
    off = 0
    for (_, width), o_ref in zip(_IN_BLOCKS, out_refs):
        for c0 in range(0, width, _IN_COL_CHUNK):
            cw = min(_IN_COL_CHUNK, width - c0)
            o_ref[:, c0:c0 + cw] = jnp.dot(h, w_ref[:, off + c0:off + c0 + cw],
                                           preferred_element_type=F32).astype(o_ref.dtype)
        off += width


def _inproj(x2d, mod, rows_per_batch, norm_g, w_cat, tm):
    t = x2d.shape[0]
    row = lambda w: pl.BlockSpec((tm, w), lambda i: (i, 0))
    return pl.pallas_call(
        _inproj_body,
        grid=(t // tm,),
        in_specs=[row(D_MODEL), _mod_spec(1, rows_per_batch, tm), _mod_spec(0, rows_per_batch, tm),
                  _const_spec((1, D_MODEL)),
                  pl.BlockSpec((D_MODEL, _IN_TOTAL), lambda i: (0, 0), pipeline_mode=pl.Buffered(1))],
        out_specs=[row(w) for _, w in _IN_BLOCKS],
        out_shape=[jax.ShapeDtypeStruct((t, w), F32 if name in _IN_F32 else BF16) for name, w in _IN_BLOCKS],
        compiler_params=pltpu.CompilerParams(vmem_limit_bytes=VMEM_LIMIT),
        name="inproj",
    )(x2d, mod, mod, norm_g.reshape(1, -1), w_cat)


def _mixer_body(C, Lv, start,
                qkv_ref, ba_ref, z_ref, pin_ref, ga_ref, gb_ref, cbuf_ref, s0_ref, pbuf_ref,
                convw_ref, alog_ref, dtb_ref, dng_ref, wpool_ref, pscale_ref,
                mixed_ref, nconv_ref, ns_ref, npool_ref,
                xp_scr, act_scr, s_scr, pp_scr, odn_scr):
    n = pl.program_id(1)
    last = pl.num_programs(1) - 1

    @pl.when(n == 0)
    def _load_state():
        xp_scr[0:CONV_PAD, :] = cbuf_ref[0]
        pp_scr[0:POOL_PAD, :] = pbuf_ref[0]
        s_scr[...] = s0_ref[0]

    xp_scr[CONV_PAD:CONV_PAD + C, :] = qkv_ref[0].astype(F32)
    for c0 in range(0, QKV_WIDTH, _IN_COL_CHUNK):
        cs = slice(c0, c0 + _IN_COL_CHUNK)
        y = xp_scr[CONV_PAD:CONV_PAD + C, cs] * convw_ref[CONV_WIDTH - 1:CONV_WIDTH, cs]
        for k in range(CONV_WIDTH - 1):
            r0 = CONV_PAD - (CONV_WIDTH - 1) + k
            y = y + xp_scr[r0:r0 + C, cs] * convw_ref[k:k + 1, cs]
        act_scr[:, cs] = _silu(y)

    ba = ba_ref[0]
    lane = lax.broadcasted_iota(I32, (C, LANES), 1)
    beta_all = jax.nn.sigmoid(ba)
    g_all = -jnp.exp(alog_ref[...]) * _softplus(ba + dtb_ref[...])
    if Lv < C:
        valid = lax.broadcasted_iota(I32, (C, LANES), 0) < Lv
        beta_all = jnp.where(valid, beta_all, 0.0)
        g_all = jnp.where(valid, g_all, 0.0)
    ii = lax.broadcasted_iota(I32, (C, C), 0)
    jj = lax.broadcasted_iota(I32, (C, C), 1)
    causal = ii >= jj
    strict = ii > jj
    tril = jnp.where(causal, 1.0, 0.0).astype(BF16)
    eye = jnp.where(ii == jj, 1.0, 0.0)
    gc_all = sum(jnp.dot(tril, part, preferred_element_type=F32) for part in _split3(g_all))
    if C < LANES:
        gc_sq = jnp.concatenate([gc_all, jnp.zeros((LANES - C, LANES), F32)], axis=0)
    else:
        gc_sq = gc_all
    gc_t = gc_sq.T

    H = range(DN_HEADS)
    hsl = [slice(h * DN_HEAD_DIM, (h + 1) * DN_HEAD_DIM) for h in H]
    beta = [jnp.sum(jnp.where(lane == h, beta_all, 0.0), axis=1, keepdims=True) for h in H]
    gcol = [jnp.sum(jnp.where(lane == DN_HEADS + h, gc_all, 0.0), axis=1, keepdims=True) for h in H]
    grow = [gc_t[DN_HEADS + h:DN_HEADS + h + 1, 0:C] for h in H]
    glast = [g[C - 1:C, :] for g in gcol]
    q = [act_scr[:, hsl[h]] for h in H]
    k = [act_scr[:, DN_WIDTH + h * DN_HEAD_DIM:DN_WIDTH + (h + 1) * DN_HEAD_DIM] for h in H]
    v = [act_scr[:, 2 * DN_WIDTH + h * DN_HEAD_DIM:2 * DN_WIDTH + (h + 1) * DN_HEAD_DIM] for h in H]
    q = [x * lax.rsqrt(jnp.sum(x * x, axis=-1, keepdims=True) + EPS) * (DN_HEAD_DIM ** -0.5) for x in q]
    k = [x * lax.rsqrt(jnp.sum(x * x, axis=-1, keepdims=True) + EPS) for x in k]
    kb = [k[h] * beta[h] for h in H]
    vb = [v[h] * beta[h] for h in H]
    decay = [jnp.where(causal, jnp.exp(jnp.where(causal, gcol[h] - grow[h], 0.0)), 0.0) for h in H]
    lower = [jnp.where(strict, _dot_nt(kb[h], k[h]) * decay[h], 0.0) for h in H]
    ainv = [eye - x for x in lower]
    pw = lower
    p = 1
    while 2 * p < C:
        pw = [_dot(x, x) for x in pw]
        ainv = [ainv[h] + _dot(ainv[h], pw[h]) for h in H]
        p *= 2
    sol = [_dot(ainv[h], jnp.concatenate([vb[h], kb[h] * jnp.exp(gcol[h])], axis=1)) for h in H]
    qk = [_dot_nt(q[h], k[h]) * decay[h] for h in H]
    k_tail = [k[h] * jnp.exp(glast[h] - gcol[h]) for h in H]
    S = [s_scr[h] for h in H]
    v_new = [sol[h][:, :DN_HEAD_DIM] - _dot(sol[h][:, DN_HEAD_DIM:], S[h]) for h in H]
    o = [_dot(q[h] * jnp.exp(gcol[h]), S[h]) + _dot(qk[h], v_new[h]) for h in H]
    for h in H:
        s_scr[h] = S[h] * jnp.exp(glast[h]) + lax.dot_general(
            k_tail[h].astype(BF16), v_new[h].astype(BF16), TN_DIMS, preferred_element_type=F32)
    for h in H:
        zf = z_ref[0, :, hsl[h]].astype(F32)
        odn_scr[:, hsl[h]] = (o[h] * lax.rsqrt(jnp.mean(o[h] * o[h], axis=-1, keepdims=True) + EPS)
                              * dng_ref[...] * _silu(zf))

    pp_scr[POOL_PAD:POOL_PAD + C, :] = pin_ref[0].astype(F32)
    pos = start + n * C + lax.broadcasted_iota(I32, (C, 1), 0)
    for gi, win in enumerate(POOL_WINDOWS):
        gs = slice(gi * POOL_GROUP_DIM, (gi + 1) * POOL_GROUP_DIM)
        xg = pp_scr[POOL_PAD:POOL_PAD + C, gs]
        ssum = xg
        for sft in range(1, win):
            ssum = ssum + pp_scr[POOL_PAD - sft:POOL_PAD - sft + C, gs]
        cnt = jnp.minimum(pos + 1, win).astype(F32)
        pooled = ssum / cnt - xg
        os_ = slice(gi * POOL_OUT_GROUP, (gi + 1) * POOL_OUT_GROUP)
        yp = _dot(pooled, wpool_ref[gi]) * pscale_ref[:, os_]
        mixed_ref[0, :, os_] = (jax.nn.sigmoid(ga_ref[0, :, os_].astype(F32)) * odn_scr[:, os_]
                                + jax.nn.sigmoid(gb_ref[0, :, os_].astype(F32)) * yp).astype(BF16)

    @pl.when(n == last)
    def _store_state():
        nconv_ref[0] = xp_scr[Lv + CONV_PAD - (CONV_WIDTH - 1):Lv + CONV_PAD, :]
        npool_ref[0] = pp_scr[Lv + POOL_PAD - POOL_BUF:Lv + POOL_PAD, :]
        ns_ref[0] = s_scr[...]

    xp_scr[0:CONV_PAD, :] = xp_scr[C:C + CONV_PAD, :]
    pp_scr[0:POOL_PAD, :] = pp_scr[C:C + POOL_PAD, :]


def _mixer(proj, conv_buf, s0, pool_buf, start, seq_len, C,
           conv_w, a_log, dt_bias, dn_norm_g, w_pool, pool_scale):
    b, lp, _ = proj["qkv"].shape
    nchunks = lp // C
    lv = seq_len - (nchunks - 1) * C
    cbuf = jnp.pad(conv_buf, ((0, 0), (CONV_PAD - (CONV_WIDTH - 1), 0), (0, 0)))
    pbuf = jnp.pad(pool_buf, ((0, 0), (POOL_PAD - POOL_BUF, 0), (0, 0)))
    lane_pad = lambda a: jnp.pad(a.reshape(1, -1), ((0, 0), (DN_HEADS, LANES - 2 * DN_HEADS)))
    chunk = lambda w: pl.BlockSpec((1, C, w), lambda i, j: (i, j, 0))
    state = lambda *s: pl.BlockSpec((1,) + s, lambda i, j: (i,) + (0,) * len(s))
    return pl.pallas_call(
        functools.partial(_mixer_body, C, lv, start),
        grid=(b, nchunks),
        in_specs=[chunk(QKV_WIDTH), chunk(LANES), chunk(DN_WIDTH), chunk(POOL_WIDTH),
                  chunk(D_MODEL), chunk(D_MODEL),
                  state(CONV_PAD, QKV_WIDTH), state(DN_HEADS, DN_HEAD_DIM, DN_HEAD_DIM),
                  state(POOL_PAD, POOL_WIDTH),
                  _const_spec((CONV_WIDTH, QKV_WIDTH)), _const_spec((1, LANES)), _const_spec((1, LANES)),
                  _const_spec((1, DN_HEAD_DIM)),
                  _const_spec((len(POOL_WINDOWS), POOL_GROUP_DIM, POOL_OUT_GROUP)),
                  _const_spec((1, D_MODEL))],
        out_specs=[chunk(D_MODEL), state(CONV_WIDTH - 1, QKV_WIDTH),
                   state(DN_HEADS, DN_HEAD_DIM, DN_HEAD_DIM), state(POOL_BUF, POOL_WIDTH)],
        out_shape=[jax.ShapeDtypeStruct((b, lp, D_MODEL), BF16),
                   jax.ShapeDtypeStruct((b, CONV_WIDTH - 1, QKV_WIDTH), F32),
                   jax.ShapeDtypeStruct((b, DN_HEADS, DN_HEAD_DIM, DN_HEAD_DIM), F32),
                   jax.ShapeDtypeStruct((b, POOL_BUF, POOL_WIDTH), F32)],
        scratch_shapes=[pltpu.VMEM((CONV_PAD + C + CONV_PAD, QKV_WIDTH), F32),
                        pltpu.VMEM((C, QKV_WIDTH), F32),
                        pltpu.VMEM((DN_HEADS, DN_HEAD_DIM, DN_HEAD_DIM), F32),
                        pltpu.VMEM((POOL_PAD + C + POOL_PAD, POOL_WIDTH), F32),
                        pltpu.VMEM((C, DN_WIDTH), F32)],
        compiler_params=pltpu.CompilerParams(dimension_semantics=("arbitrary", "arbitrary"),
                                             vmem_limit_bytes=VMEM_LIMIT),
        name="mixer",
    )(proj["qkv"], proj["ba"], proj["z"], proj["pool"], proj["ga"], proj["gb"], cbuf, s0, pbuf,
      conv_w, lane_pad(a_log), lane_pad(dt_bias), dn_norm_g.reshape(1, -1), w_pool,
      pool_scale.reshape(1, -1))


def _top16(s, ids, payload=None):
    big = float(2 ** 24)
    vals, sel, pays = [], [], []
    for _ in range(PEER_TOPK):
        m = jnp.max(s, axis=0, keepdims=True)
        am = jnp.min(jnp.where(s == m, ids, big), axis=0, keepdims=True)
        hit = ids == am
        if payload is not None:
            pays.append(jnp.max(jnp.where(hit, payload, -1.0), axis=0, keepdims=True))
        s = jnp.where(hit, -jnp.inf, s)
        vals.append(m)
        sel.append(am)
    out = (jnp.concatenate(vals, axis=0), jnp.concatenate(sel, axis=0))
    if payload is not None:
        out += (jnp.concatenate(pays, axis=0),)
    return out


_CAND_EDGE = 4


def _post_body(has_prev, has_fin, mixed_ref, x_ref, g1_ref, sc2_ref, sh2_ref, n2g_ref, wout_ref,
               wq_ref, keys_ref, *refs):
    refs = list(refs)
    prev_in = [refs.pop(0) for _ in range(2 if has_prev else 0)]
    fin_in = [refs.pop(0) for _ in range(4 if has_fin else 0)]
    x1_ref, h2_ref, idx_ref, gate_ref = refs[:4]
    extra_out = refs[4:]
    if has_prev:
        pre_ref, pgate_ref = prev_in
        extra_out.pop(0)[...] = _coef_words(pre_ref[...], pgate_ref[...])
    if has_fin:
        _final_body(*fin_in, extra_out.pop(0))
    tm = x_ref.shape[0]
    x1 = x_ref[...] + _mod_rows(g1_ref) * _dot(mixed_ref[...], wout_ref[...])
    x1_ref[...] = x1
    y = x1 * lax.rsqrt(jnp.mean(x1 * x1, axis=-1, keepdims=True) + EPS) * n2g_ref[...]
    h2 = y * (1.0 + _mod_rows(sc2_ref)) + _mod_rows(sh2_ref)
    h2_ref[...] = _pack_words(h2[:, :PACK_HALF], h2[:, PACK_HALF:])
    q = _dot(h2, wq_ref[...])

    K = PEER_TOPK
    key_id = lax.broadcasted_iota(I32, (PEER_NKEYS, 1), 0).astype(F32)
    r16 = lax.broadcasted_iota(I32, (K, 1), 0)
    cand_id = jnp.concatenate([(a * K + r16) for a in range(_CAND_EDGE)]
                              + [(r16 * K + b) for b in range(_CAND_EDGE)], axis=0).astype(F32)
    dup = r16 < _CAND_EDGE
    idx_rows, gate_rows = [], []
    for h in range(PEER_HEADS):
        half = []
        for p in range(2):
            c0 = (h * 2 + p) * PEER_KEY_HALF
            st = _dot_nt(keys_ref[h * 2 + p], q[:, c0:c0 + PEER_KEY_HALF])
            half.append(_top16(st, key_id))
        (s1, i1), (s2, i2) = half
        cand = jnp.concatenate(
            [s1[a:a + 1] + s2 for a in range(_CAND_EDGE)]
            + [jnp.where(dup, -jnp.inf, s1 + s2[b:b + 1]) for b in range(_CAND_EDGE)], axis=0)
        cidx = jnp.concatenate(
            [i1[a:a + 1] * PEER_NKEYS + i2 for a in range(_CAND_EDGE)]
            + [i1 * PEER_NKEYS + i2[b:b + 1] for b in range(_CAND_EDGE)], axis=0)
        best, _, eidx = _top16(cand, cand_id, cidx)
        e = jnp.exp(best - best[0:1])
        gate_rows.append(e / jnp.sum(e, axis=0, keepdims=True))
        idx_rows.append(eidx)
    idx_ref[...] = jnp.concatenate(idx_rows, axis=0).T.astype(I32)
    gate_ref[...] = jnp.concatenate(gate_rows, axis=0).T


def _post(mixed2d, x2d, mod, rows_per_batch, norm2_g, w_out, w_query, keys, tm, prev=None, fin=None):
    t = x2d.shape[0]
    steps = t // tm
    row = lambda w: pl.BlockSpec((tm, w), lambda i: (i, 0))
    in_specs = [row(D_MODEL), row(D_MODEL),
                _mod_spec(2, rows_per_batch, tm), _mod_spec(4, rows_per_batch, tm),
                _mod_spec(3, rows_per_batch, tm), _const_spec((1, D_MODEL)),
                _const_spec((D_MODEL, D_MODEL)), _const_spec((D_MODEL, 2 * PEER_HEADS * PEER_KEY_HALF)),
                _const_spec((2 * PEER_HEADS, PEER_NKEYS, PEER_KEY_HALF))]
    out_specs = [row(D_MODEL), row(PACK_HALF), row(PEER_HK), row(PEER_HK)]
    out_shape = [jax.ShapeDtypeStruct((t, D_MODEL), F32), jax.ShapeDtypeStruct((t, PACK_HALF), I32),
                 jax.ShapeDtypeStruct((t, PEER_HK), I32), jax.ShapeDtypeStruct((t, PEER_HK), F32)]
    args = [mixed2d, x2d, mod, mod, mod, norm2_g.reshape(1, -1), w_out, w_query, keys]
    if prev is not None:
        tp = prev[0].shape[0]
        prow = pl.BlockSpec((tp // steps, PEER_HK), lambda i: (i, 0))
        in_specs += [prow, prow]
        out_specs += [prow]
        out_shape += [jax.ShapeDtypeStruct((tp, PEER_HK), I32)]
        args += list(prev)
    if fin is not None:
        x1_f, peer_f, mod_f, rows_f, final_g = fin
        tf = x1_f.shape[0]
        frow = pl.BlockSpec((tf // steps, D_MODEL), lambda i: (i, 0))
        in_specs += [frow, frow, _mod_spec(5, rows_f, tf // steps), _const_spec((1, D_MODEL))]
        out_specs += [frow]
        out_shape += [jax.ShapeDtypeStruct((tf, D_MODEL), F32)]
        args += [x1_f, peer_f, mod_f, final_g.reshape(1, -1)]
    return pl.pallas_call(
        functools.partial(_post_body, prev is not None, fin is not None),
        grid=(steps,),
        in_specs=in_specs, out_specs=out_specs, out_shape=out_shape,
        compiler_params=pltpu.CompilerParams(vmem_limit_bytes=VMEM_LIMIT),
        name="post",
    )(*args)


SC_CORES = 2
SC_SUBCORES = 16
SC_LANES = 16
SC_WORKERS = SC_CORES * SC_SUBCORES
SC_TOKENS = 32
SC_SLOTS = 4
SC_JOB_HEADS = 2
SC_BF16_GROUP = 4
PACK_HALF = D_MODEL // 2
SC_CHUNKS = PACK_HALF // SC_LANES
PROMPT_PARTS = 8
EDGE_SPLITS = 2
COEF_LAG = 2
FIN_LAG = 3
ROW_TILE = 256


def _bf16_bits(v):
    return lax.bitcast_convert_type(v.astype(BF16).astype(F32), jnp.uint32)


def _pack_words(lo, hi):
    return lax.bitcast_convert_type((_bf16_bits(lo) >> 16) | _bf16_bits(hi), I32)


def _pack_body(x_ref, o_ref):
    o_ref[...] = _pack_words(x_ref[:, :PACK_HALF], x_ref[:, PACK_HALF:])


def _pack_table(tbl, rows=2 * ROW_TILE):
    e = tbl.shape[0]
    return pl.pallas_call(
        _pack_body, grid=(e // rows,),
        in_specs=[pl.BlockSpec((rows, D_MODEL), lambda i: (i, 0))],
        out_specs=pl.BlockSpec((rows, PACK_HALF), lambda i: (i, 0)),
        out_shape=jax.ShapeDtypeStruct((e, PACK_HALF), I32), name="pack_table")(tbl)


def _tree_sum(terms):
    terms = list(terms)
    while len(terms) > 1:
        terms = [a + b for a, b in zip(terms[0::2], terms[1::2])] + terms[len(terms) & ~1:]
    return terms[0]


def _unpack_pair(w):
    lo = plsc.bitcast(lax.shift_left(w, jnp.full(w.shape, 16, I32)), F32)
    hi = plsc.bitcast(w & jnp.full(w.shape, -65536, I32), F32)
    return lo, hi


def _sc_mesh():
    return plsc.VectorSubcoreMesh(core_axis_name="c", subcore_axis_name="s")


def _sc_worker():
    return lax.axis_index("s") * SC_CORES + lax.axis_index("c")


def _sc_jobs(table_hbm, idx_v, buf, sem, compute):
    per_tok = PEER_HEADS // SC_JOB_HEADS
    njobs = idx_v.shape[0] * per_tok
    nrows = SC_JOB_HEADS * PEER_TOPK

    def copy(j, slot):
        rows = idx_v.at[j // per_tok, pl.ds((j % per_tok) * nrows, nrows)]
        return pltpu.make_async_copy(table_hbm.at[rows], buf.at[slot], sem.at[slot])

    for s in range(SC_SLOTS):
        copy(s, s).start()

    def job(j, c):
        s = j % SC_SLOTS
        copy(j, s).wait()

        def head(i, cc):
            compute(j // per_tok, (j % per_tok) * SC_JOB_HEADS + i, s, i * PEER_TOPK)
            return cc
        lax.fori_loop(0, SC_JOB_HEADS, head, 0)

        @pl.when(j + SC_SLOTS < njobs)
        def _next():
            copy(j + SC_SLOTS, s).start()
        return c

    lax.fori_loop(0, njobs, job, 0)


def _peer_u_body(n_tok, idx_hbm, h2_hbm, u_hbm, pre_hbm, idx_v, h2_v, pre_v, ubuf, acc_v, sem):
    base = _sc_worker() * n_tok
    lane = lax.iota(I32, SC_LANES)

    def compute(tt, h, slot, r0):
        def chunk(cg, accs):
            cs = [pl.ds((cg * SC_BF16_GROUP + i) * SC_LANES, SC_LANES) for i in range(SC_BF16_GROUP)]
            xs = [plsc.bitcast(h2_v[tt, c], BF16) for c in cs]
            out = []
            for k, a in enumerate(accs):
                part = _tree_sum([plsc.bitcast(ubuf[slot, r0 + k, c], BF16) * x for c, x in zip(cs, xs)])
                lo, hi = _unpack_pair(plsc.bitcast(part, I32))
                out.append(a + (lo + hi))
            return tuple(out)
        zero = jnp.zeros((SC_LANES,), F32)
        accs = lax.fori_loop(0, SC_CHUNKS // SC_BF16_GROUP, chunk, (zero,) * PEER_TOPK)
        for k, a in enumerate(accs):
            acc_v[k, :] = a
        tot = zero
        for j in range(SC_LANES):
            tot = tot + plsc.load_gather(acc_v, [lane, (lane + j) & (SC_LANES - 1)])
        pre_v[tt, pl.ds(h * PEER_TOPK, PEER_TOPK)] = tot

    tb = idx_v.shape[0]

    def block(bi, c):
        t0 = base + bi * tb
        pltpu.sync_copy(idx_hbm.at[pl.ds(t0, tb)], idx_v)
        pltpu.sync_copy(h2_hbm.at[pl.ds(t0, tb)], h2_v)
        _sc_jobs(u_hbm, idx_v, ubuf, sem, compute)
        pltpu.sync_copy(pre_v, pre_hbm.at[pl.ds(t0, tb)])
        return c

    lax.fori_loop(0, n_tok // tb, block, 0)


def _peer_v_body(n_tok, idx_hbm, coef_hbm, v_hbm, out_hbm, idx_v, coef_v, out_v, vbuf, sem):
    base = _sc_worker() * n_tok
    zero = jnp.zeros((SC_LANES,), F32)

    def compute(tt, h, slot, r0):
        cvec = coef_v[tt, pl.ds(h * PEER_TOPK, PEER_TOPK)]
        cb = [plsc.bitcast(jnp.take_along_axis(cvec, jnp.full((SC_LANES,), k, I32), axis=0), BF16)
              for k in range(PEER_TOPK)]

        @plsc.parallel_loop(0, SC_CHUNKS, unroll=2)
        def _chunk(c):
            cs = pl.ds(c * SC_LANES, SC_LANES)
            prods = [plsc.bitcast(vbuf[slot, r0 + k, cs], BF16) * cb[k] for k in range(PEER_TOPK)]
            pairs = [_unpack_pair(plsc.bitcast(_tree_sum(prods[g:g + SC_BF16_GROUP]), I32))
                     for g in range(0, PEER_TOPK, SC_BF16_GROUP)]
            for half, off in ((0, 0), (1, PACK_HALF)):
                plsc.addupdate(out_v.at[tt, pl.ds(off + c * SC_LANES, SC_LANES)],
                               _tree_sum([p[half] for p in pairs]))

    tb = idx_v.shape[0]

    def block(bi, c):
        t0 = base + bi * tb
        pltpu.sync_copy(idx_hbm.at[pl.ds(t0, tb)], idx_v)
        pltpu.sync_copy(coef_hbm.at[pl.ds(t0, tb)], coef_v)

        def clear(i, cc):
            per_row = D_MODEL // SC_LANES
            out_v[i // per_row, pl.ds((i % per_row) * SC_LANES, SC_LANES)] = zero
            return cc
        lax.fori_loop(0, tb * (D_MODEL // SC_LANES), clear, 0)
        _sc_jobs(v_hbm, idx_v, vbuf, sem, compute)
        pltpu.sync_copy(out_v, out_hbm.at[pl.ds(t0, tb)])
        return c

    lax.fori_loop(0, n_tok // tb, block, 0)


def _peer_sc(body, idx, rows, table, out_width, name):
    t = idx.shape[0]
    assert t % SC_WORKERS == 0
    n_tok = t // SC_WORKERS
    tb = min(SC_TOKENS * (2 if body is _peer_u_body else 1), n_tok)
    assert n_tok % tb == 0 and tb * PEER_HEADS // SC_JOB_HEADS >= SC_SLOTS
    return pl.kernel(
        functools.partial(body, n_tok),
        out_type=jax.ShapeDtypeStruct((t, out_width), F32),
        mesh=_sc_mesh(),
        scratch_types=[pltpu.VMEM((tb, PEER_HK), I32),
                       pltpu.VMEM((tb, rows.shape[1]), rows.dtype),
                       pltpu.VMEM((tb, out_width), F32),
                       pltpu.VMEM((SC_SLOTS, SC_JOB_HEADS * PEER_TOPK, PACK_HALF), I32)]
                      + ([pltpu.VMEM((PEER_TOPK, SC_LANES), F32)] if body is _peer_u_body else [])
                      + [pltpu.SemaphoreType.DMA((SC_SLOTS,))],
        compiler_params=pltpu.CompilerParams(needs_layout_passes=False),
        name=name,
    )(idx, rows, table)


def _coef_words(pre, gates):
    return _pack_words(*(gates * _gelu(pre),) * 2)


def _coef_body(pre_ref, gate_ref, coef_ref):
    coef_ref[...] = _coef_words(pre_ref[...], gate_ref[...])


def _coef(pre, gates, tm):
    t = pre.shape[0]
    row = pl.BlockSpec((tm, PEER_HK), lambda i: (i, 0))
    return pl.pallas_call(_coef_body, grid=(t // tm,), in_specs=[row, row], out_specs=row,
                          out_shape=jax.ShapeDtypeStruct((t, PEER_HK), I32), name="coef")(pre, gates)


def _final_body(x1_ref, peer_ref, g2_ref, fng_ref, y_ref):
    x2 = x1_ref[...] + _mod_rows(g2_ref) * peer_ref[...]
    y_ref[...] = x2 * lax.rsqrt(jnp.mean(x2 * x2, axis=-1, keepdims=True) + EPS) * fng_ref[...]


def _final(x1, peer_out, mod, rows_per_batch, final_g, tm):
    t = x1.shape[0]
    row = pl.BlockSpec((tm, D_MODEL), lambda i: (i, 0))
    return pl.pallas_call(
        _final_body, grid=(t // tm,),
        in_specs=[row, row, _mod_spec(5, rows_per_batch, tm), _const_spec((1, D_MODEL))],
        out_specs=row, out_shape=jax.ShapeDtypeStruct((t, D_MODEL), F32), name="final",
    )(x1, peer_out, mod, final_g.reshape(1, -1))


def _expert_gather_v(g, coef, expert_v):
    g["peer_out"] = _peer_sc(_peer_v_body, g["idx"], coef, expert_v, D_MODEL, "peer_v")


def _front(x, mod, conv_buf, s0, pool_buf, start, chunk, tm, wts, prev, fin):
    b, l, _ = x.shape
    t = b * l
    x2d = x.reshape(t, D_MODEL)
    if l >= tm:
        modx = mod.reshape(b, 6, 1, D_MODEL).transpose(1, 0, 2, 3)
    else:
        modx = jnp.repeat(mod.reshape(b, 6, D_MODEL), l, axis=0).transpose(1, 0, 2)
    outs = _inproj(x2d, modx, l, wts["norm1_g"], wts["w_cat"], tm)
    lp = -(-l // chunk) * chunk
    proj = {}
    for (name, w), a in zip(_IN_BLOCKS, outs):
        a = a.reshape(b, l, w)
        proj[name] = a if lp == l else jnp.pad(a, ((0, 0), (0, lp - l), (0, 0)))
    mixed, nconv, ns, npool = _mixer(proj, conv_buf, s0, pool_buf, start, l, chunk,
                                     wts["conv_w"], wts["a_log"], wts["dt_bias"], wts["dn_norm_g"],
                                     wts["w_pool"], wts["pool_scale"])
    mixed2d = mixed[:, :l].reshape(t, D_MODEL)
    res = _post(mixed2d, x2d, modx, l, wts["norm2_g"], wts["w_out"], wts["w_query"], wts["keys"], tm,
                prev=None if prev is None else (prev["pre"], prev["gates"]),
                fin=None if fin is None else (fin["x1"], fin["peer_out"], fin["mod"], fin["l"],
                                              wts["final_norm_g"]))
    x1, h2, idx, gates = res[:4]
    extra = list(res[4:])
    coef_prev = extra.pop(0) if prev is not None else None
    y_fin = extra.pop(0).reshape(fin["b"], fin["l"], D_MODEL) if fin is not None else None
    pre = _peer_sc(_peer_u_body, idx, h2, wts["expert_u"], PEER_HK, "peer_u")
    g = dict(x1=x1, idx=idx, gates=gates, pre=pre, mod=modx, b=b, l=l, tm=tm,
             states=(nconv, ns, npool))
    return g, coef_prev, y_fin


def kernel(x_prompt, x_sample, c_prompt, c_sample, state_conv, state_delta, state_pool, w_ada, b_ada, norm1_g, w_in, conv_w, a_log, dt_bias, dn_norm_g, w_pool, pool_scale, w_out, norm2_g, w_query, sub_keys, expert_u, expert_v, final_norm_g):
    bp = x_prompt.shape[0]
    yp, ys = x_prompt, x_sample
    conv_p, delta_p, pool_p, conv_s, delta_s, pool_s = [], [], [], [], [], []
    zero_conv = jnp.zeros((bp, CONV_WIDTH - 1, QKV_WIDTH), F32)
    zero_delta = jnp.zeros((bp, DN_HEADS, DN_HEAD_DIM, DN_HEAD_DIM), F32)
    zero_pool = jnp.zeros((bp, POOL_BUF, POOL_WIDTH), F32)
    c_all = jnp.concatenate([c_prompt, c_sample], axis=0)
    for layer in range(DEPTH):
        wi = w_in[layer]
        o_b = QKV_WIDTH
        o_z = o_b + 2 * DN_HEADS
        w_ba = jnp.pad(wi[:, o_b:o_z], ((0, 0), (0, LANES - 2 * DN_HEADS)))
        w_cat = jnp.concatenate([wi[:, :o_b], wi[:, o_z:], w_ba], axis=1).astype(BF16)
        last = layer == DEPTH - 1
        wts = dict(
            norm1_g=norm1_g[layer], w_cat=w_cat, conv_w=conv_w[layer], a_log=a_log[layer],
            dt_bias=dt_bias[layer], dn_norm_g=dn_norm_g[layer], w_pool=w_pool[layer],
            pool_scale=pool_scale[layer], w_out=w_out[layer].astype(BF16), norm2_g=norm2_g[layer],
            w_query=w_query[layer].astype(BF16),
            keys=sub_keys[layer].reshape(2 * PEER_HEADS, PEER_NKEYS, PEER_KEY_HALF).astype(BF16),
            expert_u=_pack_table(expert_u[layer]), expert_v=_pack_table(expert_v[layer]),
            final_norm_g=final_norm_g if last else jnp.ones_like(final_norm_g))
        mod = _ada(c_all, w_ada[layer], b_ada[layer])
        assert last, "final norm is fused into the expert stage"
        step = bp // PROMPT_PARTS
        seq = x_prompt.shape[1]
        zeros = (zero_conv[:step], zero_delta[:step], zero_pool[:step])
        jobs, cuts = [], []
        for b0 in range(0, bp, step):
            n = EDGE_SPLITS if b0 in (0, bp - step) else 1
            cuts.append(n)
            for s0 in range(0, seq, seq // n):
                jobs.append((yp[b0:b0 + step, s0:s0 + seq // n], mod[b0:b0 + step],
                             zeros if s0 == 0 else None, s0, DN_CHUNK))
        jobs.append((ys, mod[bp:], (state_conv[layer], state_delta[layer], state_pool[layer]),
                     PAST_LEN, SUBLANES))
        groups = []
        for j, (xg, mg, states, start, chunk) in enumerate(jobs):
            prev = groups[j - COEF_LAG] if j >= COEF_LAG else None
            fin = groups[j - FIN_LAG] if j >= FIN_LAG else None
            if fin is not None and fin["x1"].shape[0] % (xg.shape[0] * xg.shape[1] // ROW_TILE):
                fin = None
            if states is None:
                states = groups[j - 1]["states"]
            g, coef_prev, y_fin = _front(xg, mg, *states, start, chunk, ROW_TILE, wts, prev, fin)
            if prev is not None:
                _expert_gather_v(prev, coef_prev, wts["expert_v"])
            if fin is not None:
                fin["y"] = y_fin
            groups.append(g)
        for g in groups[-COEF_LAG:]:
            _expert_gather_v(g, _coef(g["pre"], g["gates"], ROW_TILE), wts["expert_v"])
        for g in groups:
            if "y" not in g:
                g["y"] = _final(g["x1"], g["peer_out"], g["mod"], g["l"], wts["final_norm_g"],
                                g["tm"]).reshape(g["b"], g["l"], D_MODEL)
        rows, at = [], 0
        for n in cuts:
            rows.append(groups[at:at + n])
            at += n
        yp = jnp.concatenate([jnp.concatenate([g["y"] for g in row], axis=1) for row in rows], axis=0)
        cp, sp, pp = (jnp.concatenate(a, axis=0) for a in zip(*(row[-1]["states"] for row in rows)))
        ys = groups[-1]["y"]
        cs, ss, ps = groups[-1]["states"]
        conv_p.append(cp)
        delta_p.append(sp)
        pool_p.append(pp)
        conv_s.append(cs)
        delta_s.append(ss)
        pool_s.append(ps)
    return (yp, ys, jnp.stack(conv_p), jnp.stack(delta_p), jnp.stack(pool_p),
            jnp.stack(conv_s), jnp.stack(delta_s), jnp.stack(pool_s))
```

```python
import functools

import jax
import jax.numpy as jnp
from jax import lax
from jax.experimental import pallas as pl
from jax.experimental.pallas import tpu as pltpu
from jax.experimental.pallas import tpu_sc as plsc

F32 = jnp.float32
BF16 = jnp.bfloat16
I32 = jnp.int32

D_MODEL = 1024
DEPTH = 1
PAST_LEN = 16384
DN_HEADS = 8
DN_HEAD_DIM = 128
DN_WIDTH = DN_HEADS * DN_HEAD_DIM
QKV_WIDTH = 3 * DN_WIDTH
CONV_WIDTH = 4
DN_CHUNK = 64
POOL_WINDOWS = (2, 4, 8, 16)
POOL_GROUP_DIM = 128
POOL_WIDTH = len(POOL_WINDOWS) * POOL_GROUP_DIM
POOL_OUT_GROUP = D_MODEL // len(POOL_WINDOWS)
POOL_BUF = max(POOL_WINDOWS) - 1
PEER_HEADS = 8
PEER_NKEYS = 128
PEER_TOPK = 16
PEER_KEY_HALF = 128
PEER_HK = PEER_HEADS * PEER_TOPK
EPS = 1e-6

LANES = 128
SUBLANES = 8
CONV_PAD = SUBLANES
POOL_PAD = 16
VMEM_LIMIT = 56 * 1024 * 1024

NT_DIMS = (((1,), (1,)), ((), ()))
TN_DIMS = (((0,), (0,)), ((), ()))


def _dot(a, b):
    return jnp.dot(a.astype(BF16), b.astype(BF16), preferred_element_type=F32)


def _dot_nt(a, b):
    return lax.dot_general(a.astype(BF16), b.astype(BF16), NT_DIMS, preferred_element_type=F32)


def _split3(x):
    hi = x.astype(BF16)
    r1 = x - hi.astype(F32)
    mid = r1.astype(BF16)
    lo = (r1 - mid.astype(F32)).astype(BF16)
    return hi, mid, lo


def _silu(x):
    return x * jax.nn.sigmoid(x)


def _gelu(x):
    return 0.5 * x * (1.0 + lax.erf(x * (0.5 ** 0.5)))


def _softplus(x):
    return jnp.maximum(x, 0.0) + jnp.log(1.0 + jnp.exp(-jnp.abs(x)))


def _mod_rows(ref):
    m = ref[...]
    return m.reshape(m.shape[-2], m.shape[-1])


def _mod_spec(k, rows_per_batch, tm):
    if rows_per_batch >= tm:
        tiles = rows_per_batch // tm
        return pl.BlockSpec((1, 1, 1, D_MODEL), lambda i, *_: (k, i // tiles, 0, 0))
    return pl.BlockSpec((1, tm, D_MODEL), lambda i, *_: (k, i, 0))


def _const_spec(shape):
    nd = len(shape)
    return pl.BlockSpec(shape, lambda *_: (0,) * nd)


def _ada_body(c_ref, w_ref, b_ref, o_ref):
    o_ref[...] = _dot(_silu(c_ref[...]), w_ref[...]) + b_ref[...]


def _ada(c, w_ada, b_ada):
    n = c.shape[0]
    return pl.pallas_call(
        _ada_body,
        grid=(6,),
        in_specs=[pl.BlockSpec((n, D_MODEL), lambda j: (0, 0)),
                  pl.BlockSpec((D_MODEL, D_MODEL), lambda j: (0, j)),
                  pl.BlockSpec((1, D_MODEL), lambda j: (0, j))],
        out_specs=pl.BlockSpec((n, D_MODEL), lambda j: (0, j)),
        out_shape=jax.ShapeDtypeStruct((n, 6 * D_MODEL), F32),
        name="ada",
    )(c, w_ada, b_ada.reshape(1, -1))


_IN_BLOCKS = (("qkv", QKV_WIDTH), ("z", DN_WIDTH), ("pool", POOL_WIDTH),
              ("ga", D_MODEL), ("gb", D_MODEL), ("ba", LANES))
_IN_TOTAL = sum(w for _, w in _IN_BLOCKS)
_IN_F32 = ("ba",)
_IN_COL_CHUNK = 512


def _inproj_body(x_ref, sc_ref, sh_ref, g_ref, w_ref, *out_refs):
    x = x_ref[...]
    y = x * lax.rsqrt(jnp.mean(x * x, axis=-1, keepdims=True) + EPS) * g_ref[...]
    h = (y * (1.0 + _mod_rows(sc_ref)) + _mod_rows(sh_ref)).astype(BF16)
    off = 0
    for (_, width), o_ref in zip(_IN_BLOCKS, out_refs):
        for c0 in range(0, width, _IN_COL_CHUNK):
            cw = min(_IN_COL_CHUNK, width - c0)
            o_ref[:, c0:c0 + cw] = jnp.dot(h, w_ref[:, off + c0:off + c0 + cw],
                                           preferred_element_type=F32).astype(o_ref.dtype)
        off += width


def _inproj(x2d, mod, rows_per_batch, norm_g, w_cat, tm):
    t = x2d.shape[0]
    row = lambda w: pl.BlockSpec((tm, w), lambda i: (i, 0))
    return pl.pallas_call(
        _inproj_body,
        grid=(t // tm,),
        in_specs=[row(D_MODEL), _mod_spec(1, rows_per_batch, tm), _mod_spec(0, rows_per_batch, tm),
                  _const_spec((1, D_MODEL)),
                  pl.BlockSpec((D_MODEL, _IN_TOTAL), lambda i: (0, 0), pipeline_mode=pl.Buffered(1))],
        out_specs=[row(w) for _, w in _IN_BLOCKS],
        out_shape=[jax.ShapeDtypeStruct((t, w), F32 if name in _IN_F32 else BF16) for name, w in _IN_BLOCKS],
        compiler_params=pltpu.CompilerParams(vmem_limit_bytes=VMEM_LIMIT),
        name="inproj",
    )(x2d, mod, mod, norm_g.reshape(1, -1), w_cat)


def _mixer_body(C, Lv, start,
                qkv_ref, ba_ref, z_ref, pin_ref, ga_ref, gb_ref, cbuf_ref, s0_ref, pbuf_ref,
                convw_ref, alog_ref, dtb_ref, dng_ref, wpool_ref, pscale_ref,
                mixed_ref, nconv_ref, ns_ref, npool_ref,
                xp_scr, act_scr, s_scr, pp_scr, odn_scr):
    n = pl.program_id(1)
    last = pl.num_programs(1) - 1

    @pl.when(n == 0)
    def _load_state():
        xp_scr[0:CONV_PAD, :] = cbuf_ref[0]
        pp_scr[0:POOL_PAD, :] = pbuf_ref[0]
        s_scr[...] = s0_ref[0]

    xp_scr[CONV_PAD:CONV_PAD + C, :] = qkv_ref[0].astype(F32)
    for c0 in range(0, QKV_WIDTH, _IN_COL_CHUNK):
        cs = slice(c0, c0 + _IN_COL_CHUNK)
        y = xp_scr[CONV_PAD:CONV_PAD + C, cs] * convw_ref[CONV_WIDTH - 1:CONV_WIDTH, cs]
        for k in range(CONV_WIDTH - 1):
            r0 = CONV_PAD - (CONV_WIDTH - 1) + k
            y = y + xp_scr[r0:r0 + C, cs] * convw_ref[k:k + 1, cs]
        act_scr[:, cs] = _silu(y)

    ba = ba_ref[0]
    lane = lax.broadcasted_iota(I32, (C, LANES), 1)
    beta_all = jax.nn.sigmoid(ba)
    g_all = -jnp.exp(alog_ref[...]) * _softplus(ba + dtb_ref[...])
    if Lv < C:
        valid = lax.broadcasted_iota(I32, (C, LANES), 0) < Lv
        beta_all = jnp.where(valid, beta_all, 0.0)
        g_all = jnp.where(valid, g_all, 0.0)
    ii = lax.broadcasted_iota(I32, (C, C), 0)
    jj = lax.broadcasted_iota(I32, (C, C), 1)
    causal = ii >= jj
    strict = ii > jj
    tril = jnp.where(causal, 1.0, 0.0).astype(BF16)
    eye = jnp.where(ii == jj, 1.0, 0.0)
    gc_all = sum(jnp.dot(tril, part, preferred_element_type=F32) for part in _split3(g_all))
    if C < LANES:
        gc_sq = jnp.concatenate([gc_all, jnp.zeros((LANES - C, LANES), F32)], axis=0)
    else:
        gc_sq = gc_all
    gc_t = gc_sq.T

    H = range(DN_HEADS)
    hsl = [slice(h * DN_HEAD_DIM, (h + 1) * DN_HEAD_DIM) for h in H]
    beta = [jnp.sum(jnp.where(lane == h, beta_all, 0.0), axis=1, keepdims=True) for h in H]
    gcol = [jnp.sum(jnp.where(lane == DN_HEADS + h, gc_all, 0.0), axis=1, keepdims=True) for h in H]
    grow = [gc_t[DN_HEADS + h:DN_HEADS + h + 1, 0:C] for h in H]
    glast = [g[C - 1:C, :] for g in gcol]
    q = [act_scr[:, hsl[h]] for h in H]
    k = [act_scr[:, DN_WIDTH + h * DN_HEAD_DIM:DN_WIDTH + (h + 1) * DN_HEAD_DIM] for h in H]
    v = [act_scr[:, 2 * DN_WIDTH + h * DN_HEAD_DIM:2 * DN_WIDTH + (h + 1) * DN_HEAD_DIM] for h in H]
    q = [x * lax.rsqrt(jnp.sum(x * x, axis=-1, keepdims=True) + EPS) * (DN_HEAD_DIM ** -0.5) for x in q]
    k = [x * lax.rsqrt(jnp.sum(x * x, axis=-1, keepdims=True) + EPS) for x in k]
    kb = [k[h] * beta[h] for h in H]
    vb = [v[h] * beta[h] for h in H]
    decay = [jnp.where(causal, jnp.exp(jnp.where(causal, gcol[h] - grow[h], 0.0)), 0.0) for h in H]
    lower = [jnp.where(strict, _dot_nt(kb[h], k[h]) * decay[h], 0.0) for h in H]
    ainv = [eye - x for x in lower]
    pw = lower
    p = 1
    while 2 * p < C:
        pw = [_dot(x, x) for x in pw]
        ainv = [ainv[h] + _dot(ainv[h], pw[h]) for h in H]
        p *= 2
    sol = [_dot(ainv[h], jnp.concatenate([vb[h], kb[h] * jnp.exp(gcol[h])], axis=1)) for h in H]
    qk = [_dot_nt(q[h], k[h]) * decay[h] for h in H]
    k_tail = [k[h] * jnp.exp(glast[h] - gcol[h]) for h in H]
    S = [s_scr[h] for h in H]
    v_new = [sol[h][:, :DN_HEAD_DIM] - _dot(sol[h][:, DN_HEAD_DIM:], S[h]) for h in H]
    o = [_dot(q[h] * jnp.exp(gcol[h]), S[h]) + _dot(qk[h], v_new[h]) for h in H]
    for h in H:
        s_scr[h] = S[h] * jnp.exp(glast[h]) + lax.dot_general(
            k_tail[h].astype(BF16), v_new[h].astype(BF16), TN_DIMS, preferred_element_type=F32)
    for h in H:
        zf = z_ref[0, :, hsl[h]].astype(F32)
        odn_scr[:, hsl[h]] = (o[h] * lax.rsqrt(jnp.mean(o[h] * o[h], axis=-1, keepdims=True) + EPS)
                              * dng_ref[...] * _silu(zf))

    pp_scr[POOL_PAD:POOL_PAD + C, :] = pin_ref[0].astype(F32)
    pos = start + n * C + lax.broadcasted_iota(I32, (C, 1), 0)
    for gi, win in enumerate(POOL_WINDOWS):
        gs = slice(gi * POOL_GROUP_DIM, (gi + 1) * POOL_GROUP_DIM)
        xg = pp_scr[POOL_PAD:POOL_PAD + C, gs]
        ssum = xg
        for sft in range(1, win):
            ssum = ssum + pp_scr[POOL_PAD - sft:POOL_PAD - sft + C, gs]
        cnt = jnp.minimum(pos + 1, win).astype(F32)
        pooled = ssum / cnt - xg
        os_ = slice(gi * POOL_OUT_GROUP, (gi + 1) * POOL_OUT_GROUP)
        yp = _dot(pooled, wpool_ref[gi]) * pscale_ref[:, os_]
        mixed_ref[0, :, os_] = (jax.nn.sigmoid(ga_ref[0, :, os_].astype(F32)) * odn_scr[:, os_]
                                + jax.nn.sigmoid(gb_ref[0, :, os_].astype(F32)) * yp).astype(BF16)

    @pl.when(n == last)
    def _store_state():
        nconv_ref[0] = xp_scr[Lv + CONV_PAD - (CONV_WIDTH - 1):Lv + CONV_PAD, :]
        npool_ref[0] = pp_scr[Lv + POOL_PAD - POOL_BUF:Lv + POOL_PAD, :]
        ns_ref[0] = s_scr[...]

    xp_scr[0:CONV_PAD, :] = xp_scr[C:C + CONV_PAD, :]
    pp_scr[0:POOL_PAD, :] = pp_scr[C:C + POOL_PAD, :]


def _mixer(proj, conv_buf, s0, pool_buf, start, seq_len, C,
           conv_w, a_log, dt_bias, dn_norm_g, w_pool, pool_scale):
    b, lp, _ = proj["qkv"].shape
    nchunks = lp // C
    lv = seq_len - (nchunks - 1) * C
    cbuf = jnp.pad(conv_buf, ((0, 0), (CONV_PAD - (CONV_WIDTH - 1), 0), (0, 0)))
    pbuf = jnp.pad(pool_buf, ((0, 0), (POOL_PAD - POOL_BUF, 0), (0, 0)))
    lane_pad = lambda a: jnp.pad(a.reshape(1, -1), ((0, 0), (DN_HEADS, LANES - 2 * DN_HEADS)))
    chunk = lambda w: pl.BlockSpec((1, C, w), lambda i, j: (i, j, 0))
    state = lambda *s: pl.BlockSpec((1,) + s, lambda i, j: (i,) + (0,) * len(s))
    return pl.pallas_call(
        functools.partial(_mixer_body, C, lv, start),
        grid=(b, nchunks),
        in_specs=[chunk(QKV_WIDTH), chunk(LANES), chunk(DN_WIDTH), chunk(POOL_WIDTH),
                  chunk(D_MODEL), chunk(D_MODEL),
                  state(CONV_PAD, QKV_WIDTH), state(DN_HEADS, DN_HEAD_DIM, DN_HEAD_DIM),
                  state(POOL_PAD, POOL_WIDTH),
                  _const_spec((CONV_WIDTH, QKV_WIDTH)), _const_spec((1, LANES)), _const_spec((1, LANES)),
                  _const_spec((1, DN_HEAD_DIM)),
                  _const_spec((len(POOL_WINDOWS), POOL_GROUP_DIM, POOL_OUT_GROUP)),
                  _const_spec((1, D_MODEL))],
        out_specs=[chunk(D_MODEL), state(CONV_WIDTH - 1, QKV_WIDTH),
                   state(DN_HEADS, DN_HEAD_DIM, DN_HEAD_DIM), state(POOL_BUF, POOL_WIDTH)],
        out_shape=[jax.ShapeDtypeStruct((b, lp, D_MODEL), BF16),
                   jax.ShapeDtypeStruct((b, CONV_WIDTH - 1, QKV_WIDTH), F32),
                   jax.ShapeDtypeStruct((b, DN_HEADS, DN_HEAD_DIM, DN_HEAD_DIM), F32),
                   jax.ShapeDtypeStruct((b, POOL_BUF, POOL_WIDTH), F32)],
        scratch_shapes=[pltpu.VMEM((CONV_PAD + C + CONV_PAD, QKV_WIDTH), F32),
                        pltpu.VMEM((C, QKV_WIDTH), F32),
                        pltpu.VMEM((DN_HEADS, DN_HEAD_DIM, DN_HEAD_DIM), F32),
                        pltpu.VMEM((POOL_PAD + C + POOL_PAD, POOL_WIDTH), F32),
                        pltpu.VMEM((C, DN_WIDTH), F32)],
        compiler_params=pltpu.CompilerParams(dimension_semantics=("arbitrary", "arbitrary"),
                                             vmem_limit_bytes=VMEM_LIMIT),
        name="mixer",
    )(proj["qkv"], proj["ba"], proj["z"], proj["pool"], proj["ga"], proj["gb"], cbuf, s0, pbuf,
      conv_w, lane_pad(a_log), lane_pad(dt_bias), dn_norm_g.reshape(1, -1), w_pool,
      pool_scale.reshape(1, -1))


def _top16(s, ids, payload=None):
    big = float(2 ** 24)
    vals, sel, pays = [], [], []
    for _ in range(PEER_TOPK):
        m = jnp.max(s, axis=0, keepdims=True)
        am = jnp.min(jnp.where(s == m, ids, big), axis=0, keepdims=True)
        hit = ids == am
        if payload is not None:
            pays.append(jnp.max(jnp.where(hit, payload, -1.0), axis=0, keepdims=True))
        s = jnp.where(hit, -jnp.inf, s)
        vals.append(m)
        sel.append(am)
    out = (jnp.concatenate(vals, axis=0), jnp.concatenate(sel, axis=0))
    if payload is not None:
        out += (jnp.concatenate(pays, axis=0),)
    return out


_CAND_EDGE = 4


def _post_body(has_prev, has_fin, mixed_ref, x_ref, g1_ref, sc2_ref, sh2_ref, n2g_ref, wout_ref,
               wq_ref, keys_ref, *refs):
    refs = list(refs)
    prev_in = [refs.pop(0) for _ in range(2 if has_prev else 0)]
    fin_in = [refs.pop(0) for _ in range(4 if has_fin else 0)]
    x1_ref, h2_ref, idx_ref, gate_ref = refs[:4]
    extra_out = refs[4:]
    if has_prev:
        pre_ref, pgate_ref = prev_in
        extra_out.pop(0)[...] = _coef_words(pre_ref[...], pgate_ref[...])
    if has_fin:
        _final_body(*fin_in, extra_out.pop(0))
    tm = x_ref.shape[0]
    x1 = x_ref[...] + _mod_rows(g1_ref) * _dot(mixed_ref[...], wout_ref[...])
    x1_ref[...] = x1
    y = x1 * lax.rsqrt(jnp.mean(x1 * x1, axis=-1, keepdims=True) + EPS) * n2g_ref[...]
    h2 = y * (1.0 + _mod_rows(sc2_ref)) + _mod_rows(sh2_ref)
    h2_ref[...] = _pack_words(h2[:, :PACK_HALF], h2[:, PACK_HALF:])
    q = _dot(h2, wq_ref[...])

    K = PEER_TOPK
    key_id = lax.broadcasted_iota(I32, (PEER_NKEYS, 1), 0).astype(F32)
    r16 = lax.broadcasted_iota(I32, (K, 1), 0)
    cand_id = jnp.concatenate([(a * K + r16) for a in range(_CAND_EDGE)]
                              + [(r16 * K + b) for b in range(_CAND_EDGE)], axis=0).astype(F32)
    dup = r16 < _CAND_EDGE
    idx_rows, gate_rows = [], []
    for h in range(PEER_HEADS):
        half = []
        for p in range(2):
            c0 = (h * 2 + p) * PEER_KEY_HALF
            st = _dot_nt(keys_ref[h * 2 + p], q[:, c0:c0 + PEER_KEY_HALF])
            half.append(_top16(st, key_id))
        (s1, i1), (s2, i2) = half
        cand = jnp.concatenate(
            [s1[a:a + 1] + s2 for a in range(_CAND_EDGE)]
            + [jnp.where(dup, -jnp.inf, s1 + s2[b:b + 1]) for b in range(_CAND_EDGE)], axis=0)
        cidx = jnp.concatenate(
            [i1[a:a + 1] * PEER_NKEYS + i2 for a in range(_CAND_EDGE)]
            + [i1 * PEER_NKEYS + i2[b:b + 1] for b in range(_CAND_EDGE)], axis=0)
        best, _, eidx = _top16(cand, cand_id, cidx)
        e = jnp.exp(best - best[0:1])
        gate_rows.append(e / jnp.sum(e, axis=0, keepdims=True))
        idx_rows.append(eidx)
    idx_ref[...] = jnp.concatenate(idx_rows, axis=0).T.astype(I32)
    gate_ref[...] = jnp.concatenate(gate_rows, axis=0).T


def _post(mixed2d, x2d, mod, rows_per_batch, norm2_g, w_out, w_query, keys, tm, prev=None, fin=None):
    t = x2d.shape[0]
    steps = t // tm
    row = lambda w: pl.BlockSpec((tm, w), lambda i: (i, 0))
    in_specs = [row(D_MODEL), row(D_MODEL),
                _mod_spec(2, rows_per_batch, tm), _mod_spec(4, rows_per_batch, tm),
                _mod_spec(3, rows_per_batch, tm), _const_spec((1, D_MODEL)),
                _const_spec((D_MODEL, D_MODEL)), _const_spec((D_MODEL, 2 * PEER_HEADS * PEER_KEY_HALF)),
                _const_spec((2 * PEER_HEADS, PEER_NKEYS, PEER_KEY_HALF))]
    out_specs = [row(D_MODEL), row(PACK_HALF), row(PEER_HK), row(PEER_HK)]
    out_shape = [jax.ShapeDtypeStruct((t, D_MODEL), F32), jax.ShapeDtypeStruct((t, PACK_HALF), I32),
                 jax.ShapeDtypeStruct((t, PEER_HK), I32), jax.ShapeDtypeStruct((t, PEER_HK), F32)]
    args = [mixed2d, x2d, mod, mod, mod, norm2_g.reshape(1, -1), w_out, w_query, keys]
    if prev is not None:
        tp = prev[0].shape[0]
        prow = pl.BlockSpec((tp // steps, PEER_HK), lambda i: (i, 0))
        in_specs += [prow, prow]
        out_specs += [prow]
        out_shape += [jax.ShapeDtypeStruct((tp, PEER_HK), I32)]
        args += list(prev)
    if fin is not None:
        x1_f, peer_f, mod_f, rows_f, final_g = fin
        tf = x1_f.shape[0]
        frow = pl.BlockSpec((tf // steps, D_MODEL), lambda i: (i, 0))
        in_specs += [frow, frow, _mod_spec(5, rows_f, tf // steps), _const_spec((1, D_MODEL))]
        out_specs += [frow]
        out_shape += [jax.ShapeDtypeStruct((tf, D_MODEL), F32)]
        args += [x1_f, peer_f, mod_f, final_g.reshape(1, -1)]
    return pl.pallas_call(
        functools.partial(_post_body, prev is not None, fin is not None),
        grid=(steps,),
        in_specs=in_specs, out_specs=out_specs, out_shape=out_shape,
        compiler_params=pltpu.CompilerParams(vmem_limit_bytes=VMEM_LIMIT),
        name="post",
    )(*args)


SC_CORES = 2
SC_SUBCORES = 16
SC_LANES = 16
SC_WORKERS = SC_CORES * SC_SUBCORES
SC_TOKENS = 32
SC_SLOTS = 4
SC_JOB_HEADS = 2
SC_BF16_GROUP = 4
PACK_HALF = D_MODEL // 2
SC_CHUNKS = PACK_HALF // SC_LANES
PROMPT_PARTS = 8
EDGE_SPLITS = (2, 4)
COEF_LAG = 2
FIN_LAG = 3
ROW_TILE = 256


def _bf16_bits(v):
    return lax.bitcast_convert_type(v.astype(BF16).astype(F32), jnp.uint32)


def _pack_words(lo, hi):
    return lax.bitcast_convert_type((_bf16_bits(lo) >> 16) | _bf16_bits(hi), I32)


def _pack_body(x_ref, o_ref):
    o_ref[...] = _pack_words(x_ref[:, :PACK_HALF], x_ref[:, PACK_HALF:])


def _pack_table(tbl, rows=2 * ROW_TILE):
    e = tbl.shape[0]
    return pl.pallas_call(
        _pack_body, grid=(e // rows,),
        in_specs=[pl.BlockSpec((rows, D_MODEL), lambda i: (i, 0))],
        out_specs=pl.BlockSpec((rows, PACK_HALF), lambda i: (i, 0)),
        out_shape=jax.ShapeDtypeStruct((e, PACK_HALF), I32), name="pack_table")(tbl)


def _tree_sum(terms):
    terms = list(terms)
    while len(terms) > 1:
        terms = [a + b for a, b in zip(terms[0::2], terms[1::2])] + terms[len(terms) & ~1:]
    return terms[0]


def _unpack_pair(w):
    lo = plsc.bitcast(lax.shift_left(w, jnp.full(w.shape, 16, I32)), F32)
    hi = plsc.bitcast(w & jnp.full(w.shape, -65536, I32), F32)
    return lo, hi


def _sc_mesh():
    return plsc.VectorSubcoreMesh(core_axis_name="c", subcore_axis_name="s")


def _sc_worker():
    return lax.axis_index("s") * SC_CORES + lax.axis_index("c")


def _sc_jobs(table_hbm, idx_v, buf, sem, compute):
    per_tok = PEER_HEADS // SC_JOB_HEADS
    njobs = idx_v.shape[0] * per_tok
    nrows = SC_JOB_HEADS * PEER_TOPK

    def copy(j, slot):
        rows = idx_v.at[j // per_tok, pl.ds((j % per_tok) * nrows, nrows)]
        return pltpu.make_async_copy(table_hbm.at[rows], buf.at[slot], sem.at[slot])

    for s in range(SC_SLOTS):
        copy(s, s).start()

    def job(j, c):
        s = j % SC_SLOTS
        copy(j, s).wait()

        def head(i, cc):
            compute(j // per_tok, (j % per_tok) * SC_JOB_HEADS + i, s, i * PEER_TOPK)
            return cc
        lax.fori_loop(0, SC_JOB_HEADS, head, 0)

        @pl.when(j + SC_SLOTS < njobs)
        def _next():
            copy(j + SC_SLOTS, s).start()
        return c

    lax.fori_loop(0, njobs, job, 0)


def _peer_u_body(n_tok, idx_hbm, h2_hbm, u_hbm, pre_hbm, idx_v, h2_v, pre_v, ubuf, acc_v, sem):
    base = _sc_worker() * n_tok
    lane = lax.iota(I32, SC_LANES)

    def compute(tt, h, slot, r0):
        def chunk(cg, accs):
            cs = [pl.ds((cg * SC_BF16_GROUP + i) * SC_LANES, SC_LANES) for i in range(SC_BF16_GROUP)]
            xs = [plsc.bitcast(h2_v[tt, c], BF16) for c in cs]
            out = []
            for k, a in enumerate(accs):
                part = _tree_sum([plsc.bitcast(ubuf[slot, r0 + k, c], BF16) * x for c, x in zip(cs, xs)])
                lo, hi = _unpack_pair(plsc.bitcast(part, I32))
                out.append(a + (lo + hi))
            return tuple(out)
        zero = jnp.zeros((SC_LANES,), F32)
        accs = lax.fori_loop(0, SC_CHUNKS // SC_BF16_GROUP, chunk, (zero,) * PEER_TOPK)
        for k, a in enumerate(accs):
            acc_v[k, :] = a
        tot = zero
        for j in range(SC_LANES):
            tot = tot + plsc.load_gather(acc_v, [lane, (lane + j) & (SC_LANES - 1)])
        pre_v[tt, pl.ds(h * PEER_TOPK, PEER_TOPK)] = tot

    tb = idx_v.shape[0]

    def block(bi, c):
        t0 = base + bi * tb
        pltpu.sync_copy(idx_hbm.at[pl.ds(t0, tb)], idx_v)
        pltpu.sync_copy(h2_hbm.at[pl.ds(t0, tb)], h2_v)
        _sc_jobs(u_hbm, idx_v, ubuf, sem, compute)
        pltpu.sync_copy(pre_v, pre_hbm.at[pl.ds(t0, tb)])
        return c

    lax.fori_loop(0, n_tok // tb, block, 0)


def _peer_v_body(n_tok, idx_hbm, coef_hbm, v_hbm, out_hbm, idx_v, coef_v, out_v, vbuf, sem):
    base = _sc_worker() * n_tok
    zero = jnp.zeros((SC_LANES,), F32)

    def compute(tt, h, slot, r0):
        cvec = coef_v[tt, pl.ds(h * PEER_TOPK, PEER_TOPK)]
        cb = [plsc.bitcast(jnp.take_along_axis(cvec, jnp.full((SC_LANES,), k, I32), axis=0), BF16)
              for k in range(PEER_TOPK)]

        @plsc.parallel_loop(0, SC_CHUNKS, unroll=2)
        def _chunk(c):
            cs = pl.ds(c * SC_LANES, SC_LANES)
            prods = [plsc.bitcast(vbuf[slot, r0 + k, cs], BF16) * cb[k] for k in range(PEER_TOPK)]
            pairs = [_unpack_pair(plsc.bitcast(_tree_sum(prods[g:g + SC_BF16_GROUP]), I32))
                     for g in range(0, PEER_TOPK, SC_BF16_GROUP)]
            for half, off in ((0, 0), (1, PACK_HALF)):
                plsc.addupdate(out_v.at[tt, pl.ds(off + c * SC_LANES, SC_LANES)],
                               _tree_sum([p[half] for p in pairs]))

    tb = idx_v.shape[0]

    def block(bi, c):
        t0 = base + bi * tb
        pltpu.sync_copy(idx_hbm.at[pl.ds(t0, tb)], idx_v)
        pltpu.sync_copy(coef_hbm.at[pl.ds(t0, tb)], coef_v)

        def clear(i, cc):
            per_row = D_MODEL // SC_LANES
            out_v[i // per_row, pl.ds((i % per_row) * SC_LANES, SC_LANES)] = zero
            return cc
        lax.fori_loop(0, tb * (D_MODEL // SC_LANES), clear, 0)
        _sc_jobs(v_hbm, idx_v, vbuf, sem, compute)
        pltpu.sync_copy(out_v, out_hbm.at[pl.ds(t0, tb)])
        return c

    lax.fori_loop(0, n_tok // tb, block, 0)


def _peer_sc(body, idx, rows, table, out_width, name):
    t = idx.shape[0]
    assert t % SC_WORKERS == 0
    n_tok = t // SC_WORKERS
    tb = min(SC_TOKENS * (2 if body is _peer_u_body else 1), n_tok)
    assert n_tok % tb == 0 and tb * PEER_HEADS // SC_JOB_HEADS >= SC_SLOTS
    return pl.kernel(
        functools.partial(body, n_tok),
        out_type=jax.ShapeDtypeStruct((t, out_width), F32),
        mesh=_sc_mesh(),
        scratch_types=[pltpu.VMEM((tb, PEER_HK), I32),
                       pltpu.VMEM((tb, rows.shape[1]), rows.dtype),
                       pltpu.VMEM((tb, out_width), F32),
                       pltpu.VMEM((SC_SLOTS, SC_JOB_HEADS * PEER_TOPK, PACK_HALF), I32)]
                      + ([pltpu.VMEM((PEER_TOPK, SC_LANES), F32)] if body is _peer_u_body else [])
                      + [pltpu.SemaphoreType.DMA((SC_SLOTS,))],
        compiler_params=pltpu.CompilerParams(needs_layout_passes=False),
        name=name,
    )(idx, rows, table)


def _coef_words(pre, gates):
    return _pack_words(*(gates * _gelu(pre),) * 2)


def _coef_body(pre_ref, gate_ref, coef_ref):
    coef_ref[...] = _coef_words(pre_ref[...], gate_ref[...])


def _coef(pre, gates, tm):
    t = pre.shape[0]
    row = pl.BlockSpec((tm, PEER_HK), lambda i: (i, 0))
    return pl.pallas_call(_coef_body, grid=(t // tm,), in_specs=[row, row], out_specs=row,
                          out_shape=jax.ShapeDtypeStruct((t, PEER_HK), I32), name="coef")(pre, gates)


def _final_body(x1_ref, peer_ref, g2_ref, fng_ref, y_ref):
    x2 = x1_ref[...] + _mod_rows(g2_ref) * peer_ref[...]
    y_ref[...] = x2 * lax.rsqrt(jnp.mean(x2 * x2, axis=-1, keepdims=True) + EPS) * fng_ref[...]


def _final(x1, peer_out, mod, rows_per_batch, final_g, tm):
    t = x1.shape[0]
    row = pl.BlockSpec((tm, D_MODEL), lambda i: (i, 0))
    return pl.pallas_call(
        _final_body, grid=(t // tm,),
        in_specs=[row, row, _mod_spec(5, rows_per_batch, tm), _const_spec((1, D_MODEL))],
        out_specs=row, out_shape=jax.ShapeDtypeStruct((t, D_MODEL), F32), name="final",
    )(x1, peer_out, mod, final_g.reshape(1, -1))


def _expert_gather_v(g, coef, expert_v):
    g["peer_out"] = _peer_sc(_peer_v_body, g["idx"], coef, expert_v, D_MODEL, "peer_v")


def _front(x, mod, conv_buf, s0, pool_buf, start, chunk, tm, wts, prev, fin):
    b, l, _ = x.shape
    t = b * l
    x2d = x.reshape(t, D_MODEL)
    if l >= tm:
        modx = mod.reshape(b, 6, 1, D_MODEL).transpose(1, 0, 2, 3)
    else:
        modx = jnp.repeat(mod.reshape(b, 6, D_MODEL), l, axis=0).transpose(1, 0, 2)
    outs = _inproj(x2d, modx, l, wts["norm1_g"], wts["w_cat"], tm)
    lp = -(-l // chunk) * chunk
    proj = {}
    for (name, w), a in zip(_IN_BLOCKS, outs):
        a = a.reshape(b, l, w)
        proj[name] = a if lp == l else jnp.pad(a, ((0, 0), (0, lp - l), (0, 0)))
    mixed, nconv, ns, npool = _mixer(proj, conv_buf, s0, pool_buf, start, l, chunk,
                                     wts["conv_w"], wts["a_log"], wts["dt_bias"], wts["dn_norm_g"],
                                     wts["w_pool"], wts["pool_scale"])
    mixed2d = mixed[:, :l].reshape(t, D_MODEL)
    res = _post(mixed2d, x2d, modx, l, wts["norm2_g"], wts["w_out"], wts["w_query"], wts["keys"], tm,
                prev=None if prev is None else (prev["pre"], prev["gates"]),
                fin=None if fin is None else (fin["x1"], fin["peer_out"], fin["mod"], fin["l"],
                                              wts["final_norm_g"]))
    x1, h2, idx, gates = res[:4]
    extra = list(res[4:])
    coef_prev = extra.pop(0) if prev is not None else None
    y_fin = extra.pop(0).reshape(fin["b"], fin["l"], D_MODEL) if fin is not None else None
    pre = _peer_sc(_peer_u_body, idx, h2, wts["expert_u"], PEER_HK, "peer_u")
    g = dict(x1=x1, idx=idx, gates=gates, pre=pre, mod=modx, b=b, l=l, tm=tm,
             states=(nconv, ns, npool))
    return g, coef_prev, y_fin


def kernel(x_prompt, x_sample, c_prompt, c_sample, state_conv, state_delta, state_pool, w_ada, b_ada, norm1_g, w_in, conv_w, a_log, dt_bias, dn_norm_g, w_pool, pool_scale, w_out, norm2_g, w_query, sub_keys, expert_u, expert_v, final_norm_g):
    bp = x_prompt.shape[0]
    yp, ys = x_prompt, x_sample
    conv_p, delta_p, pool_p, conv_s, delta_s, pool_s = [], [], [], [], [], []
    zero_conv = jnp.zeros((bp, CONV_WIDTH - 1, QKV_WIDTH), F32)
    zero_delta = jnp.zeros((bp, DN_HEADS, DN_HEAD_DIM, DN_HEAD_DIM), F32)
    zero_pool = jnp.zeros((bp, POOL_BUF, POOL_WIDTH), F32)
    c_all = jnp.concatenate([c_prompt, c_sample], axis=0)
    for layer in range(DEPTH):
        wi = w_in[layer]
        o_b = QKV_WIDTH
        o_z = o_b + 2 * DN_HEADS
        w_ba = jnp.pad(wi[:, o_b:o_z], ((0, 0), (0, LANES - 2 * DN_HEADS)))
        w_cat = jnp.concatenate([wi[:, :o_b], wi[:, o_z:], w_ba], axis=1).astype(BF16)
        last = layer == DEPTH - 1
        wts = dict(
            norm1_g=norm1_g[layer], w_cat=w_cat, conv_w=conv_w[layer], a_log=a_log[layer],
            dt_bias=dt_bias[layer], dn_norm_g=dn_norm_g[layer], w_pool=w_pool[layer],
            pool_scale=pool_scale[layer], w_out=w_out[layer].astype(BF16), norm2_g=norm2_g[layer],
            w_query=w_query[layer].astype(BF16),
            keys=sub_keys[layer].reshape(2 * PEER_HEADS, PEER_NKEYS, PEER_KEY_HALF).astype(BF16),
            expert_u=_pack_table(expert_u[layer]), expert_v=_pack_table(expert_v[layer]),
            final_norm_g=final_norm_g if last else jnp.ones_like(final_norm_g))
        mod = _ada(c_all, w_ada[layer], b_ada[layer])
        assert last, "final norm is fused into the expert stage"
        step = bp // PROMPT_PARTS
        seq = x_prompt.shape[1]
        zeros = (zero_conv[:step], zero_delta[:step], zero_pool[:step])
        jobs, cuts = [], []
        for b0 in range(0, bp, step):
            n = EDGE_SPLITS[0] if b0 == 0 else EDGE_SPLITS[1] if b0 == bp - step else 1
            cuts.append(n)
            for s0 in range(0, seq, seq // n):
                jobs.append((yp[b0:b0 + step, s0:s0 + seq // n], mod[b0:b0 + step],
                             zeros if s0 == 0 else None, s0, DN_CHUNK))
        jobs.append((ys, mod[bp:], (state_conv[layer], state_delta[layer], state_pool[layer]),
                     PAST_LEN, SUBLANES))
        groups = []
        for j, (xg, mg, states, start, chunk) in enumerate(jobs):
            prev = groups[j - COEF_LAG] if j >= COEF_LAG else None
            fin = groups[j - FIN_LAG] if j >= FIN_LAG else None
            if fin is not None and fin["x1"].shape[0] % (xg.shape[0] * xg.shape[1] // ROW_TILE):
                fin = None
            if states is None:
                states = groups[j - 1]["states"]
            g, coef_prev, y_fin = _front(xg, mg, *states, start, chunk, ROW_TILE, wts, prev, fin)
            if prev is not None:
                _expert_gather_v(prev, coef_prev, wts["expert_v"])
            if fin is not None:
                fin["y"] = y_fin
            groups.append(g)
        for g in groups[-COEF_LAG:]:
            _expert_gather_v(g, _coef(g["pre"], g["gates"], ROW_TILE), wts["expert_v"])
        for g in groups:
            if "y" not in g:
                g["y"] = _final(g["x1"], g["peer_out"], g["mod"], g["l"], wts["final_norm_g"],
                                g["tm"]).reshape(g["b"], g["l"], D_MODEL)
        rows, at = [], 0
        for n in cuts:
            rows.append(groups[at:at + n])
            at += n
        yp = jnp.concatenate([jnp.concatenate([g["y"] for g in row], axis=1) for row in rows], axis=0)
        cp, sp, pp = (jnp.concatenate(a, axis=0) for a in zip(*(row[-1]["states"] for row in rows)))
        ys = groups[-1]["y"]
        cs, ss, ps = groups[-1]["states"]
        conv_p.append(cp)
        delta_p.append(sp)
        pool_p.append(pp)
        conv_s.append(cs)
        delta_s.append(ss)
        pool_s.append(ps)
    return (yp, ys, jnp.stack(conv_p), jnp.stack(delta_p), jnp.stack(pool_p),
            jnp.stack(conv_s), jnp.stack(delta_s), jnp.stack(pool_s))
```

```python
import functools

import jax
import jax.numpy as jnp
from jax import lax
from jax.experimental import pallas as pl
from jax.experimental.pallas import tpu as pltpu
from jax.experimental.pallas import tpu_sc as plsc

F32 = jnp.float32
BF16 = jnp.bfloat16
I32 = jnp.int32

D_MODEL = 1024
DEPTH = 1
PAST_LEN = 16384
DN_HEADS = 8
DN_HEAD_DIM = 128
DN_WIDTH = DN_HEADS * DN_HEAD_DIM
QKV_WIDTH = 3 * DN_WIDTH
CONV_WIDTH = 4
DN_CHUNK = 64
POOL_WINDOWS = (2, 4, 8, 16)
POOL_GROUP_DIM = 128
POOL_WIDTH = len(POOL_WINDOWS) * POOL_GROUP_DIM
POOL_OUT_GROUP = D_MODEL // len(POOL_WINDOWS)
POOL_BUF = max(POOL_WINDOWS) - 1
PEER_HEADS = 8
PEER_NKEYS = 128
PEER_TOPK = 16
PEER_KEY_HALF = 128
PEER_HK = PEER_HEADS * PEER_TOPK
EPS = 1e-6

LANES = 128
SUBLANES = 8
CONV_PAD = SUBLANES
POOL_PAD = 16
VMEM_LIMIT = 56 * 1024 * 1024

NT_DIMS = (((1,), (1,)), ((), ()))
TN_DIMS = (((0,), (0,)), ((), ()))


def _dot(a, b):
    return jnp.dot(a.astype(BF16), b.astype(BF16), preferred_element_type=F32)


def _dot_nt(a, b):
    return lax.dot_general(a.astype(BF16), b.astype(BF16), NT_DIMS, preferred_element_type=F32)


def _split3(x):
    hi = x.astype(BF16)
    r1 = x - hi.astype(F32)
    mid = r1.astype(BF16)
    lo = (r1 - mid.astype(F32)).astype(BF16)
    return hi, mid, lo


def _silu(x):
    return x * jax.nn.sigmoid(x)


def _gelu(x):
    return 0.5 * x * (1.0 + lax.erf(x * (0.5 ** 0.5)))


def _softplus(x):
    return jnp.maximum(x, 0.0) + jnp.log(1.0 + jnp.exp(-jnp.abs(x)))


def _mod_rows(ref):
    m = ref[...]
    return m.reshape(m.shape[-2], m.shape[-1])


def _mod_spec(k, rows_per_batch, tm):
    if rows_per_batch >= tm:
        tiles = rows_per_batch // tm
        return pl.BlockSpec((1, 1, 1, D_MODEL), lambda i, *_: (k, i // tiles, 0, 0))
    return pl.BlockSpec((1, tm, D_MODEL), lambda i, *_: (k, i, 0))


def _const_spec(shape):
    nd = len(shape)
    return pl.BlockSpec(shape, lambda *_: (0,) * nd)


def _ada_body(c_ref, w_ref, b_ref, o_ref):
    o_ref[...] = _dot(_silu(c_ref[...]), w_ref[...]) + b_ref[...]


def _ada(c, w_ada, b_ada):
    n = c.shape[0]
    return pl.pallas_call(
        _ada_body,
        grid=(6,),
        in_specs=[pl.BlockSpec((n, D_MODEL), lambda j: (0, 0)),
                  pl.BlockSpec((D_MODEL, D_MODEL), lambda j: (0, j)),
                  pl.BlockSpec((1, D_MODEL), lambda j: (0, j))],
        out_specs=pl.BlockSpec((n, D_MODEL), lambda j: (0, j)),
        out_shape=jax.ShapeDtypeStruct((n, 6 * D_MODEL), F32),
        name="ada",
    )(c, w_ada, b_ada.reshape(1, -1))


_IN_BLOCKS = (("qkv", QKV_WIDTH), ("z", DN_WIDTH), ("pool", POOL_WIDTH),
              ("ga", D_MODEL), ("gb", D_MODEL), ("ba", LANES))
_IN_TOTAL = sum(w for _, w in _IN_BLOCKS)
_IN_F32 = ("ba",)
_IN_COL_CHUNK = 512


def _inproj_body(n_after, x_ref, sc_ref, sh_ref, g_ref, w_ref, *refs):
    out_refs = refs[n_after:]
    x = x_ref[...]
    y = x * lax.rsqrt(jnp.mean(x * x, axis=-1, keepdims=True) + EPS) * g_ref[...]
    h = (y * (1.0 + _mod_rows(sc_ref)) + _mod_rows(sh_ref)).astype(BF16)
    off = 0
    for (_, width), o_ref in zip(_IN_BLOCKS, out_refs):
        for c0 in range(0, width, _IN_COL_CHUNK):
            cw = min(_IN_COL_CHUNK, width - c0)
            o_ref[:, c0:c0 + cw] = jnp.dot(h, w_ref[:, off + c0:off + c0 + cw],
                                           preferred_element_type=F32).astype(o_ref.dtype)
        off += width


def _inproj(x2d, mod, rows_per_batch, norm_g, w_cat, tm, after=()):
    t = x2d.shape[0]
    row = lambda w: pl.BlockSpec((tm, w), lambda i: (i, 0))
    return pl.pallas_call(
        functools.partial(_inproj_body, len(after)),
        grid=(t // tm,),
        in_specs=[row(D_MODEL), _mod_spec(1, rows_per_batch, tm), _mod_spec(0, rows_per_batch, tm),
                  _const_spec((1, D_MODEL)),
                  pl.BlockSpec((D_MODEL, _IN_TOTAL), lambda i: (0, 0), pipeline_mode=pl.Buffered(1))]
                 + [pl.BlockSpec((SUBLANES, a.shape[1]), lambda i: (0, 0)) for a in after],
        out_specs=[row(w) for _, w in _IN_BLOCKS],
        out_shape=[jax.ShapeDtypeStruct((t, w), F32 if name in _IN_F32 else BF16) for name, w in _IN_BLOCKS],
        compiler_params=pltpu.CompilerParams(vmem_limit_bytes=VMEM_LIMIT),
        name="inproj",
    )(x2d, mod, mod, norm_g.reshape(1, -1), w_cat, *after)


def _mixer_body(C, Lv, start,
                qkv_ref, ba_ref, z_ref, pin_ref, ga_ref, gb_ref, cbuf_ref, s0_ref, pbuf_ref,
                convw_ref, alog_ref, dtb_ref, dng_ref, wpool_ref, pscale_ref,
                mixed_ref, nconv_ref, ns_ref, npool_ref,
                xp_scr, act_scr, s_scr, pp_scr, odn_scr):
    n = pl.program_id(1)
    last = pl.num_programs(1) - 1

    @pl.when(n == 0)
    def _load_state():
        xp_scr[0:CONV_PAD, :] = cbuf_ref[0]
        pp_scr[0:POOL_PAD, :] = pbuf_ref[0]
        s_scr[...] = s0_ref[0]

    xp_scr[CONV_PAD:CONV_PAD + C, :] = qkv_ref[0].astype(F32)
    for c0 in range(0, QKV_WIDTH, _IN_COL_CHUNK):
        cs = slice(c0, c0 + _IN_COL_CHUNK)
        y = xp_scr[CONV_PAD:CONV_PAD + C, cs] * convw_ref[CONV_WIDTH - 1:CONV_WIDTH, cs]
        for k in range(CONV_WIDTH - 1):
            r0 = CONV_PAD - (CONV_WIDTH - 1) + k
            y = y + xp_scr[r0:r0 + C, cs] * convw_ref[k:k + 1, cs]
        act_scr[:, cs] = _silu(y)

    ba = ba_ref[0]
    lane = lax.broadcasted_iota(I32, (C, LANES), 1)
    beta_all = jax.nn.sigmoid(ba)
    g_all = -jnp.exp(alog_ref[...]) * _softplus(ba + dtb_ref[...])
    if Lv < C:
        valid = lax.broadcasted_iota(I32, (C, LANES), 0) < Lv
        beta_all = jnp.where(valid, beta_all, 0.0)
        g_all = jnp.where(valid, g_all, 0.0)
    ii = lax.broadcasted_iota(I32, (C, C), 0)
    jj = lax.broadcasted_iota(I32, (C, C), 1)
    causal = ii >= jj
    strict = ii > jj
    tril = jnp.where(causal, 1.0, 0.0).astype(BF16)
    eye = jnp.where(ii == jj, 1.0, 0.0)
    gc_all = sum(jnp.dot(tril, part, preferred_element_type=F32) for part in _split3(g_all))
    if C < LANES:
        gc_sq = jnp.concatenate([gc_all, jnp.zeros((LANES - C, LANES), F32)], axis=0)
    else:
        gc_sq = gc_all
    gc_t = gc_sq.T

    H = range(DN_HEADS)
    hsl = [slice(h * DN_HEAD_DIM, (h + 1) * DN_HEAD_DIM) for h in H]
    beta = [jnp.sum(jnp.where(lane == h, beta_all, 0.0), axis=1, keepdims=True) for h in H]
    gcol = [jnp.sum(jnp.where(lane == DN_HEADS + h, gc_all, 0.0), axis=1, keepdims=True) for h in H]
    grow = [gc_t[DN_HEADS + h:DN_HEADS + h + 1, 0:C] for h in H]
    glast = [g[C - 1:C, :] for g in gcol]
    q = [act_scr[:, hsl[h]] for h in H]
    k = [act_scr[:, DN_WIDTH + h * DN_HEAD_DIM:DN_WIDTH + (h + 1) * DN_HEAD_DIM] for h in H]
    v = [act_scr[:, 2 * DN_WIDTH + h * DN_HEAD_DIM:2 * DN_WIDTH + (h + 1) * DN_HEAD_DIM] for h in H]
    q = [x * lax.rsqrt(jnp.sum(x * x, axis=-1, keepdims=True) + EPS) * (DN_HEAD_DIM ** -0.5) for x in q]
    k = [x * lax.rsqrt(jnp.sum(x * x, axis=-1, keepdims=True) + EPS) for x in k]
    kb = [k[h] * beta[h] for h in H]
    vb = [v[h] * beta[h] for h in H]
    decay = [jnp.where(causal, jnp.exp(jnp.where(causal, gcol[h] - grow[h], 0.0)), 0.0) for h in H]
    lower = [jnp.where(strict, _dot_nt(kb[h], k[h]) * decay[h], 0.0) for h in H]
    ainv = [eye - x for x in lower]
    pw = lower
    p = 1
    while 2 * p < C:
        pw = [_dot(x, x) for x in pw]
        ainv = [ainv[h] + _dot(ainv[h], pw[h]) for h in H]
        p *= 2
    sol = [_dot(ainv[h], jnp.concatenate([vb[h], kb[h] * jnp.exp(gcol[h])], axis=1)) for h in H]
    qk = [_dot_nt(q[h], k[h]) * decay[h] for h in H]
    k_tail = [k[h] * jnp.exp(glast[h] - gcol[h]) for h in H]
    S = [s_scr[h] for h in H]
    v_new = [sol[h][:, :DN_HEAD_DIM] - _dot(sol[h][:, DN_HEAD_DIM:], S[h]) for h in H]
    o = [_dot(q[h] * jnp.exp(gcol[h]), S[h]) + _dot(qk[h], v_new[h]) for h in H]
    for h in H:
        s_scr[h] = S[h] * jnp.exp(glast[h]) + lax.dot_general(
            k_tail[h].astype(BF16), v_new[h].astype(BF16), TN_DIMS, preferred_element_type=F32)
    for h in H:
        zf = z_ref[0, :, hsl[h]].astype(F32)
        odn_scr[:, hsl[h]] = (o[h] * lax.rsqrt(jnp.mean(o[h] * o[h], axis=-1, keepdims=True) + EPS)
                              * dng_ref[...] * _silu(zf))

    pp_scr[POOL_PAD:POOL_PAD + C, :] = pin_ref[0].astype(F32)
    pos = start + n * C + lax.broadcasted_iota(I32, (C, 1), 0)
    for gi, win in enumerate(POOL_WINDOWS):
        gs = slice(gi * POOL_GROUP_DIM, (gi + 1) * POOL_GROUP_DIM)
        xg = pp_scr[POOL_PAD:POOL_PAD + C, gs]
        ssum = xg
        for sft in range(1, win):
            ssum = ssum + pp_scr[POOL_PAD - sft:POOL_PAD - sft + C, gs]
        cnt = jnp.minimum(pos + 1, win).astype(F32)
        pooled = ssum / cnt - xg
        os_ = slice(gi * POOL_OUT_GROUP, (gi + 1) * POOL_OUT_GROUP)
        yp = _dot(pooled, wpool_ref[gi]) * pscale_ref[:, os_]
        mixed_ref[0, :, os_] = (jax.nn.sigmoid(ga_ref[0, :, os_].astype(F32)) * odn_scr[:, os_]
                                + jax.nn.sigmoid(gb_ref[0, :, os_].astype(F32)) * yp).astype(BF16)

    @pl.when(n == last)
    def _store_state():
        nconv_ref[0] = xp_scr[Lv + CONV_PAD - (CONV_WIDTH - 1):Lv + CONV_PAD, :]
        npool_ref[0] = pp_scr[Lv + POOL_PAD - POOL_BUF:Lv + POOL_PAD, :]
        ns_ref[0] = s_scr[...]

    xp_scr[0:CONV_PAD, :] = xp_scr[C:C + CONV_PAD, :]
    pp_scr[0:POOL_PAD, :] = pp_scr[C:C + POOL_PAD, :]


def _mixer(proj, conv_buf, s0, pool_buf, start, seq_len, C,
           conv_w, a_log, dt_bias, dn_norm_g, w_pool, pool_scale):
    b, lp, _ = proj["qkv"].shape
    nchunks = lp // C
    lv = seq_len - (nchunks - 1) * C
    cbuf = jnp.pad(conv_buf, ((0, 0), (CONV_PAD - (CONV_WIDTH - 1), 0), (0, 0)))
    pbuf = jnp.pad(pool_buf, ((0, 0), (POOL_PAD - POOL_BUF, 0), (0, 0)))
    lane_pad = lambda a: jnp.pad(a.reshape(1, -1), ((0, 0), (DN_HEADS, LANES - 2 * DN_HEADS)))
    chunk = lambda w: pl.BlockSpec((1, C, w), lambda i, j: (i, j, 0))
    state = lambda *s: pl.BlockSpec((1,) + s, lambda i, j: (i,) + (0,) * len(s))
    return pl.pallas_call(
        functools.partial(_mixer_body, C, lv, start),
        grid=(b, nchunks),
        in_specs=[chunk(QKV_WIDTH), chunk(LANES), chunk(DN_WIDTH), chunk(POOL_WIDTH),
                  chunk(D_MODEL), chunk(D_MODEL),
                  state(CONV_PAD, QKV_WIDTH), state(DN_HEADS, DN_HEAD_DIM, DN_HEAD_DIM),
                  state(POOL_PAD, POOL_WIDTH),
                  _const_spec((CONV_WIDTH, QKV_WIDTH)), _const_spec((1, LANES)), _const_spec((1, LANES)),
                  _const_spec((1, DN_HEAD_DIM)),
                  _const_spec((len(POOL_WINDOWS), POOL_GROUP_DIM, POOL_OUT_GROUP)),
                  _const_spec((1, D_MODEL))],
        out_specs=[chunk(D_MODEL), state(CONV_WIDTH - 1, QKV_WIDTH),
                   state(DN_HEADS, DN_HEAD_DIM, DN_HEAD_DIM), state(POOL_BUF, POOL_WIDTH)],
        out_shape=[jax.ShapeDtypeStruct((b, lp, D_MODEL), BF16),
                   jax.ShapeDtypeStruct((b, CONV_WIDTH - 1, QKV_WIDTH), F32),
                   jax.ShapeDtypeStruct((b, DN_HEADS, DN_HEAD_DIM, DN_HEAD_DIM), F32),
                   jax.ShapeDtypeStruct((b, POOL_BUF, POOL_WIDTH), F32)],
        scratch_shapes=[pltpu.VMEM((CONV_PAD + C + CONV_PAD, QKV_WIDTH), F32),
                        pltpu.VMEM((C, QKV_WIDTH), F32),
                        pltpu.VMEM((DN_HEADS, DN_HEAD_DIM, DN_HEAD_DIM), F32),
                        pltpu.VMEM((POOL_PAD + C + POOL_PAD, POOL_WIDTH), F32),
                        pltpu.VMEM((C, DN_WIDTH), F32)],
        compiler_params=pltpu.CompilerParams(dimension_semantics=("arbitrary", "arbitrary"),
                                             vmem_limit_bytes=VMEM_LIMIT),
        name="mixer",
    )(proj["qkv"], proj["ba"], proj["z"], proj["pool"], proj["ga"], proj["gb"], cbuf, s0, pbuf,
      conv_w, lane_pad(a_log), lane_pad(dt_bias), dn_norm_g.reshape(1, -1), w_pool,
      pool_scale.reshape(1, -1))


def _top16(s, ids, payload=None):
    big = float(2 ** 24)
    vals, sel, pays = [], [], []
    for _ in range(PEER_TOPK):
        m = jnp.max(s, axis=0, keepdims=True)
        am = jnp.min(jnp.where(s == m, ids, big), axis=0, keepdims=True)
        hit = ids == am
        if payload is not None:
            pays.append(jnp.max(jnp.where(hit, payload, -1.0), axis=0, keepdims=True))
        s = jnp.where(hit, -jnp.inf, s)
        vals.append(m)
        sel.append(am)
    out = (jnp.concatenate(vals, axis=0), jnp.concatenate(sel, axis=0))
    if payload is not None:
        out += (jnp.concatenate(pays, axis=0),)
    return out


_CAND_EDGE = 4


def _post_body(has_prev, has_fin, mixed_ref, x_ref, g1_ref, sc2_ref, sh2_ref, n2g_ref, wout_ref,
               wq_ref, keys_ref, *refs):
    refs = list(refs)
    prev_in = [refs.pop(0) for _ in range(2 if has_prev else 0)]
    fin_in = [refs.pop(0) for _ in range(4 if has_fin else 0)]
    x1_ref, h2_ref, idx_ref, gate_ref = refs[:4]
    extra_out = refs[4:]
    if has_prev:
        pre_ref, pgate_ref = prev_in
        extra_out.pop(0)[...] = _coef_words(pre_ref[...], pgate_ref[...])
    if has_fin:
        _final_body(*fin_in, extra_out.pop(0))
    tm = x_ref.shape[0]
    x1 = x_ref[...] + _mod_rows(g1_ref) * _dot(mixed_ref[...], wout_ref[...])
    x1_ref[...] = x1
    y = x1 * lax.rsqrt(jnp.mean(x1 * x1, axis=-1, keepdims=True) + EPS) * n2g_ref[...]
    h2 = y * (1.0 + _mod_rows(sc2_ref)) + _mod_rows(sh2_ref)
    h2_ref[...] = _pack_words(h2[:, :PACK_HALF], h2[:, PACK_HALF:])
    q = _dot(h2, wq_ref[...])

    K = PEER_TOPK
    key_id = lax.broadcasted_iota(I32, (PEER_NKEYS, 1), 0).astype(F32)
    r16 = lax.broadcasted_iota(I32, (K, 1), 0)
    cand_id = jnp.concatenate([(a * K + r16) for a in range(_CAND_EDGE)]
                              + [(r16 * K + b) for b in range(_CAND_EDGE)], axis=0).astype(F32)
    dup = r16 < _CAND_EDGE
    idx_rows, gate_rows = [], []
    for h in range(PEER_HEADS):
        half = []
        for p in range(2):
            c0 = (h * 2 + p) * PEER_KEY_HALF
            st = _dot_nt(keys_ref[h * 2 + p], q[:, c0:c0 + PEER_KEY_HALF])
            half.append(_top16(st, key_id))
        (s1, i1), (s2, i2) = half
        cand = jnp.concatenate(
            [s1[a:a + 1] + s2 for a in range(_CAND_EDGE)]
            + [jnp.where(dup, -jnp.inf, s1 + s2[b:b + 1]) for b in range(_CAND_EDGE)], axis=0)
        cidx = jnp.concatenate(
            [i1[a:a + 1] * PEER_NKEYS + i2 for a in range(_CAND_EDGE)]
            + [i1 * PEER_NKEYS + i2[b:b + 1] for b in range(_CAND_EDGE)], axis=0)
        best, _, eidx = _top16(cand, cand_id, cidx)
        e = jnp.exp(best - best[0:1])
        gate_rows.append(e / jnp.sum(e, axis=0, keepdims=True))
        idx_rows.append(eidx)
    idx_ref[...] = jnp.concatenate(idx_rows, axis=0).T.astype(I32)
    gate_ref[...] = jnp.concatenate(gate_rows, axis=0).T


def _post(mixed2d, x2d, mod, rows_per_batch, norm2_g, w_out, w_query, keys, tm, prev=None, fin=None):
    t = x2d.shape[0]
    steps = t // tm
    row = lambda w: pl.BlockSpec((tm, w), lambda i: (i, 0))
    in_specs = [row(D_MODEL), row(D_MODEL),
                _mod_spec(2, rows_per_batch, tm), _mod_spec(4, rows_per_batch, tm),
                _mod_spec(3, rows_per_batch, tm), _const_spec((1, D_MODEL)),
                _const_spec((D_MODEL, D_MODEL)), _const_spec((D_MODEL, 2 * PEER_HEADS * PEER_KEY_HALF)),
                _const_spec((2 * PEER_HEADS, PEER_NKEYS, PEER_KEY_HALF))]
    out_specs = [row(D_MODEL), row(PACK_HALF), row(PEER_HK), row(PEER_HK)]
    out_shape = [jax.ShapeDtypeStruct((t, D_MODEL), F32), jax.ShapeDtypeStruct((t, PACK_HALF), I32),
                 jax.ShapeDtypeStruct((t, PEER_HK), I32), jax.ShapeDtypeStruct((t, PEER_HK), F32)]
    args = [mixed2d, x2d, mod, mod, mod, norm2_g.reshape(1, -1), w_out, w_query, keys]
    if prev is not None:
        tp = prev[0].shape[0]
        prow = pl.BlockSpec((tp // steps, PEER_HK), lambda i: (i, 0))
        in_specs += [prow, prow]
        out_specs += [prow]
        out_shape += [jax.ShapeDtypeStruct((tp, PEER_HK), I32)]
        args += list(prev)
    if fin is not None:
        x1_f, peer_f, mod_f, rows_f, final_g = fin
        tf = x1_f.shape[0]
        frow = pl.BlockSpec((tf // steps, D_MODEL), lambda i: (i, 0))
        in_specs += [frow, frow, _mod_spec(5, rows_f, tf // steps), _const_spec((1, D_MODEL))]
        out_specs += [frow]
        out_shape += [jax.ShapeDtypeStruct((tf, D_MODEL), F32)]
        args += [x1_f, peer_f, mod_f, final_g.reshape(1, -1)]
    return pl.pallas_call(
        functools.partial(_post_body, prev is not None, fin is not None),
        grid=(steps,),
        in_specs=in_specs, out_specs=out_specs, out_shape=out_shape,
        compiler_params=pltpu.CompilerParams(vmem_limit_bytes=VMEM_LIMIT),
        name="post",
    )(*args)


SC_CORES = 2
SC_SUBCORES = 16
SC_LANES = 16
SC_WORKERS = SC_CORES * SC_SUBCORES
SC_TOKENS = 32
SC_SLOTS = 4
SC_JOB_HEADS = 2
SC_BF16_GROUP = 4
PACK_HALF = D_MODEL // 2
SC_CHUNKS = PACK_HALF // SC_LANES
PROMPT_PARTS = 8
EDGE_SPLITS = (2, 2)
SAMPLE_SLOT = 5
COEF_LAG = 2
FIN_LAG = 3
ROW_TILE = 256


def _bf16_bits(v):
    return lax.bitcast_convert_type(v.astype(BF16).astype(F32), jnp.uint32)


def _pack_words(lo, hi):
    return lax.bitcast_convert_type((_bf16_bits(lo) >> 16) | _bf16_bits(hi), I32)


def _pack_body(x_ref, o_ref):
    o_ref[...] = _pack_words(x_ref[:, :PACK_HALF], x_ref[:, PACK_HALF:])


def _pack_table(tbl, rows=2 * ROW_TILE):
    e = tbl.shape[0]
    return pl.pallas_call(
        _pack_body, grid=(e // rows,),
        in_specs=[pl.BlockSpec((rows, D_MODEL), lambda i: (i, 0))],
        out_specs=pl.BlockSpec((rows, PACK_HALF), lambda i: (i, 0)),
        out_shape=jax.ShapeDtypeStruct((e, PACK_HALF), I32), name="pack_table")(tbl)


def _tree_sum(terms):
    terms = list(terms)
    while len(terms) > 1:
        terms = [a + b for a, b in zip(terms[0::2], terms[1::2])] + terms[len(terms) & ~1:]
    return terms[0]


def _unpack_pair(w):
    lo = plsc.bitcast(lax.shift_left(w, jnp.full(w.shape, 16, I32)), F32)
    hi = plsc.bitcast(w & jnp.full(w.shape, -65536, I32), F32)
    return lo, hi


def _sc_mesh():
    return plsc.VectorSubcoreMesh(core_axis_name="c", subcore_axis_name="s")


def _sc_worker():
    return lax.axis_index("s") * SC_CORES + lax.axis_index("c")


def _sc_jobs(table_hbm, idx_v, buf, sem, compute):
    per_tok = PEER_HEADS // SC_JOB_HEADS
    njobs = idx_v.shape[0] * per_tok
    nrows = SC_JOB_HEADS * PEER_TOPK

    def copy(j, slot):
        rows = idx_v.at[j // per_tok, pl.ds((j % per_tok) * nrows, nrows)]
        return pltpu.make_async_copy(table_hbm.at[rows], buf.at[slot], sem.at[slot])

    for s in range(SC_SLOTS):
        copy(s, s).start()

    def job(j, c):
        s = j % SC_SLOTS
        copy(j, s).wait()

        def head(i, cc):
            compute(j // per_tok, (j % per_tok) * SC_JOB_HEADS + i, s, i * PEER_TOPK)
            return cc
        lax.fori_loop(0, SC_JOB_HEADS, head, 0)

        @pl.when(j + SC_SLOTS < njobs)
        def _next():
            copy(j + SC_SLOTS, s).start()
        return c

    lax.fori_loop(0, njobs, job, 0)


def _peer_u_body(n_tok, idx_hbm, h2_hbm, u_hbm, pre_hbm, idx_v, h2_v, pre_v, ubuf, acc_v, sem):
    base = _sc_worker() * n_tok
    lane = lax.iota(I32, SC_LANES)

    def compute(tt, h, slot, r0):
        def chunk(cg, accs):
            cs = [pl.ds((cg * SC_BF16_GROUP + i) * SC_LANES, SC_LANES) for i in range(SC_BF16_GROUP)]
            xs = [plsc.bitcast(h2_v[tt, c], BF16) for c in cs]
            out = []
            for k, a in enumerate(accs):
                part = _tree_sum([plsc.bitcast(ubuf[slot, r0 + k, c], BF16) * x for c, x in zip(cs, xs)])
                lo, hi = _unpack_pair(plsc.bitcast(part, I32))
                out.append(a + (lo + hi))
            return tuple(out)
        zero = jnp.zeros((SC_LANES,), F32)
        accs = lax.fori_loop(0, SC_CHUNKS // SC_BF16_GROUP, chunk, (zero,) * PEER_TOPK)
        for k, a in enumerate(accs):
            acc_v[k, :] = a
        tot = zero
        for j in range(SC_LANES):
            tot = tot + plsc.load_gather(acc_v, [lane, (lane + j) & (SC_LANES - 1)])
        pre_v[tt, pl.ds(h * PEER_TOPK, PEER_TOPK)] = tot

    tb = idx_v.shape[0]

    def block(bi, c):
        t0 = base + bi * tb
        pltpu.sync_copy(idx_hbm.at[pl.ds(t0, tb)], idx_v)
        pltpu.sync_copy(h2_hbm.at[pl.ds(t0, tb)], h2_v)
        _sc_jobs(u_hbm, idx_v, ubuf, sem, compute)
        pltpu.sync_copy(pre_v, pre_hbm.at[pl.ds(t0, tb)])
        return c

    lax.fori_loop(0, n_tok // tb, block, 0)


def _peer_v_body(n_tok, idx_hbm, coef_hbm, v_hbm, out_hbm, idx_v, coef_v, out_v, vbuf, sem):
    base = _sc_worker() * n_tok
    zero = jnp.zeros((SC_LANES,), F32)

    def compute(tt, h, slot, r0):
        cvec = coef_v[tt, pl.ds(h * PEER_TOPK, PEER_TOPK)]
        cb = [plsc.bitcast(jnp.take_along_axis(cvec, jnp.full((SC_LANES,), k, I32), axis=0), BF16)
              for k in range(PEER_TOPK)]

        @plsc.parallel_loop(0, SC_CHUNKS, unroll=2)
        def _chunk(c):
            cs = pl.ds(c * SC_LANES, SC_LANES)
            prods = [plsc.bitcast(vbuf[slot, r0 + k, cs], BF16) * cb[k] for k in range(PEER_TOPK)]
            pairs = [_unpack_pair(plsc.bitcast(_tree_sum(prods[g:g + SC_BF16_GROUP]), I32))
                     for g in range(0, PEER_TOPK, SC_BF16_GROUP)]
            for half, off in ((0, 0), (1, PACK_HALF)):
                plsc.addupdate(out_v.at[tt, pl.ds(off + c * SC_LANES, SC_LANES)],
                               _tree_sum([p[half] for p in pairs]))

    tb = idx_v.shape[0]

    def block(bi, c):
        t0 = base + bi * tb
        pltpu.sync_copy(idx_hbm.at[pl.ds(t0, tb)], idx_v)
        pltpu.sync_copy(coef_hbm.at[pl.ds(t0, tb)], coef_v)

        def clear(i, cc):
            per_row = D_MODEL // SC_LANES
            out_v[i // per_row, pl.ds((i % per_row) * SC_LANES, SC_LANES)] = zero
            return cc
        lax.fori_loop(0, tb * (D_MODEL // SC_LANES), clear, 0)
        _sc_jobs(v_hbm, idx_v, vbuf, sem, compute)
        pltpu.sync_copy(out_v, out_hbm.at[pl.ds(t0, tb)])
        return c

    lax.fori_loop(0, n_tok // tb, block, 0)


def _peer_sc(body, idx, rows, table, out_width, name):
    t = idx.shape[0]
    assert t % SC_WORKERS == 0
    n_tok = t // SC_WORKERS
    tb = min(SC_TOKENS * (2 if body is _peer_u_body else 1), n_tok)
    assert n_tok % tb == 0 and tb * PEER_HEADS // SC_JOB_HEADS >= SC_SLOTS
    return pl.kernel(
        functools.partial(body, n_tok),
        out_type=jax.ShapeDtypeStruct((t, out_width), F32),
        mesh=_sc_mesh(),
        scratch_types=[pltpu.VMEM((tb, PEER_HK), I32),
                       pltpu.VMEM((tb, rows.shape[1]), rows.dtype),
                       pltpu.VMEM((tb, out_width), F32),
                       pltpu.VMEM((SC_SLOTS, SC_JOB_HEADS * PEER_TOPK, PACK_HALF), I32)]
                      + ([pltpu.VMEM((PEER_TOPK, SC_LANES), F32)] if body is _peer_u_body else [])
                      + [pltpu.SemaphoreType.DMA((SC_SLOTS,))],
        compiler_params=pltpu.CompilerParams(needs_layout_passes=False),
        name=name,
    )(idx, rows, table)


def _coef_words(pre, gates):
    return _pack_words(*(gates * _gelu(pre),) * 2)


def _coef_body(pre_ref, gate_ref, coef_ref):
    coef_ref[...] = _coef_words(pre_ref[...], gate_ref[...])


def _coef(pre, gates, tm):
    t = pre.shape[0]
    row = pl.BlockSpec((tm, PEER_HK), lambda i: (i, 0))
    return pl.pallas_call(_coef_body, grid=(t // tm,), in_specs=[row, row], out_specs=row,
                          out_shape=jax.ShapeDtypeStruct((t, PEER_HK), I32), name="coef")(pre, gates)


def _final_body(x1_ref, peer_ref, g2_ref, fng_ref, y_ref):
    x2 = x1_ref[...] + _mod_rows(g2_ref) * peer_ref[...]
    y_ref[...] = x2 * lax.rsqrt(jnp.mean(x2 * x2, axis=-1, keepdims=True) + EPS) * fng_ref[...]


def _final(x1, peer_out, mod, rows_per_batch, final_g, tm):
    t = x1.shape[0]
    row = pl.BlockSpec((tm, D_MODEL), lambda i: (i, 0))
    return pl.pallas_call(
        _final_body, grid=(t // tm,),
        in_specs=[row, row, _mod_spec(5, rows_per_batch, tm), _const_spec((1, D_MODEL))],
        out_specs=row, out_shape=jax.ShapeDtypeStruct((t, D_MODEL), F32), name="final",
    )(x1, peer_out, mod, final_g.reshape(1, -1))


def _expert_gather_v(g, coef, expert_v):
    g["peer_out"] = _peer_sc(_peer_v_body, g["idx"], coef, expert_v, D_MODEL, "peer_v")


def _front(x, mod, conv_buf, s0, pool_buf, start, chunk, tm, wts, prev, fin, after=()):
    b, l, _ = x.shape
    t = b * l
    x2d = x.reshape(t, D_MODEL)
    if l >= tm:
        modx = mod.reshape(b, 6, 1, D_MODEL).transpose(1, 0, 2, 3)
    else:
        modx = jnp.repeat(mod.reshape(b, 6, D_MODEL), l, axis=0).transpose(1, 0, 2)
    outs = _inproj(x2d, modx, l, wts["norm1_g"], wts["w_cat"], tm, after)
    lp = -(-l // chunk) * chunk
    proj = {}
    for (name, w), a in zip(_IN_BLOCKS, outs):
        a = a.reshape(b, l, w)
        proj[name] = a if lp == l else jnp.pad(a, ((0, 0), (0, lp - l), (0, 0)))
    mixed, nconv, ns, npool = _mixer(proj, conv_buf, s0, pool_buf, start, l, chunk,
                                     wts["conv_w"], wts["a_log"], wts["dt_bias"], wts["dn_norm_g"],
                                     wts["w_pool"], wts["pool_scale"])
    mixed2d = mixed[:, :l].reshape(t, D_MODEL)
    res = _post(mixed2d, x2d, modx, l, wts["norm2_g"], wts["w_out"], wts["w_query"], wts["keys"], tm,
                prev=None if prev is None else (prev["pre"], prev["gates"]),
                fin=None if fin is None else (fin["x1"], fin["peer_out"], fin["mod"], fin["l"],
                                              wts["final_norm_g"]))
    x1, h2, idx, gates = res[:4]
    extra = list(res[4:])
    coef_prev = extra.pop(0) if prev is not None else None
    y_fin = extra.pop(0).reshape(fin["b"], fin["l"], D_MODEL) if fin is not None else None
    pre = _peer_sc(_peer_u_body, idx, h2, wts["expert_u"], PEER_HK, "peer_u")
    g = dict(x1=x1, idx=idx, gates=gates, pre=pre, mod=modx, b=b, l=l, tm=tm,
             states=(nconv, ns, npool))
    return g, coef_prev, y_fin


def kernel(x_prompt, x_sample, c_prompt, c_sample, state_conv, state_delta, state_pool, w_ada, b_ada, norm1_g, w_in, conv_w, a_log, dt_bias, dn_norm_g, w_pool, pool_scale, w_out, norm2_g, w_query, sub_keys, expert_u, expert_v, final_norm_g):
    bp = x_prompt.shape[0]
    yp, ys = x_prompt, x_sample
    conv_p, delta_p, pool_p, conv_s, delta_s, pool_s = [], [], [], [], [], []
    zero_conv = jnp.zeros((bp, CONV_WIDTH - 1, QKV_WIDTH), F32)
    zero_delta = jnp.zeros((bp, DN_HEADS, DN_HEAD_DIM, DN_HEAD_DIM), F32)
    zero_pool = jnp.zeros((bp, POOL_BUF, POOL_WIDTH), F32)
    c_all = jnp.concatenate([c_prompt, c_sample], axis=0)
    for layer in range(DEPTH):
        wi = w_in[layer]
        o_b = QKV_WIDTH
        o_z = o_b + 2 * DN_HEADS
        w_ba = jnp.pad(wi[:, o_b:o_z], ((0, 0), (0, LANES - 2 * DN_HEADS)))
        w_cat = jnp.concatenate([wi[:, :o_b], wi[:, o_z:], w_ba], axis=1).astype(BF16)
        last = layer == DEPTH - 1
        wts = dict(
            norm1_g=norm1_g[layer], w_cat=w_cat, conv_w=conv_w[layer], a_log=a_log[layer],
            dt_bias=dt_bias[layer], dn_norm_g=dn_norm_g[layer], w_pool=w_pool[layer],
            pool_scale=pool_scale[layer], w_out=w_out[layer].astype(BF16), norm2_g=norm2_g[layer],
            w_query=w_query[layer].astype(BF16),
            keys=sub_keys[layer].reshape(2 * PEER_HEADS, PEER_NKEYS, PEER_KEY_HALF).astype(BF16),
            expert_u=_pack_table(expert_u[layer]), expert_v=_pack_table(expert_v[layer]),
            final_norm_g=final_norm_g if last else jnp.ones_like(final_norm_g))
        mod = _ada(c_all, w_ada[layer], b_ada[layer])
        assert last, "final norm is fused into the expert stage"
        step = bp // PROMPT_PARTS
        seq = x_prompt.shape[1]
        zeros = (zero_conv[:step], zero_delta[:step], zero_pool[:step])
        jobs, cuts = [], []
        for b0 in range(0, bp, step):
            n = EDGE_SPLITS[0] if b0 == 0 else EDGE_SPLITS[1] if b0 == bp - step else 1
            cuts.append(n)
            for s0 in range(0, seq, seq // n):
                jobs.append((yp[b0:b0 + step, s0:s0 + seq // n], mod[b0:b0 + step],
                             zeros if s0 == 0 else None, s0, DN_CHUNK))
        sample_job = (ys, mod[bp:], (state_conv[layer], state_delta[layer], state_pool[layer]),
                      PAST_LEN, SUBLANES)
        groups, sample, after = [], None, ()
        for j, (xg, mg, states, start, chunk) in enumerate(jobs):
            if j == SAMPLE_SLOT:
                sample, _, _ = _front(*sample_job[:2], *sample_job[2], *sample_job[3:], ROW_TILE, wts, None, None)
                after = (sample["idx"],)
            prev = groups[j - COEF_LAG] if j >= COEF_LAG else None
            fin = groups[j - FIN_LAG] if j >= FIN_LAG else None
            if fin is not None and fin["x1"].shape[0] % (xg.shape[0] * xg.shape[1] // ROW_TILE):
                fin = None
            if states is None:
                states = groups[j - 1]["states"]
            g, coef_prev, y_fin = _front(xg, mg, *states, start, chunk, ROW_TILE, wts, prev, fin, after)
            after = ()
            if prev is not None:
                _expert_gather_v(prev, coef_prev, wts["expert_v"])
            if fin is not None:
                fin["y"] = y_fin
            groups.append(g)
        for g in groups[-COEF_LAG:] + [sample]:
            _expert_gather_v(g, _coef(g["pre"], g["gates"], ROW_TILE), wts["expert_v"])
        for g in groups + [sample]:
            if "y" not in g:
                g["y"] = _final(g["x1"], g["peer_out"], g["mod"], g["l"], wts["final_norm_g"],
                                g["tm"]).reshape(g["b"], g["l"], D_MODEL)
        rows, at = [], 0
        for n in cuts:
            rows.append(groups[at:at + n])
            at += n
        yp = jnp.concatenate([jnp.concatenate([g["y"] for g in row], axis=1) for row in rows], axis=0)
        cp, sp, pp = (jnp.concatenate(a, axis=0) for a in zip(*(row[-1]["states"] for row in rows)))
        ys = sample["y"]
        cs, ss, ps = sample["states"]
        conv_p.append(cp)
        delta_p.append(sp)
        pool_p.append(pp)
        conv_s.append(cs)
        delta_s.append(ss)
        pool_s.append(ps)
    return (yp, ys, jnp.stack(conv_p), jnp.stack(delta_p), jnp.stack(pool_p),
            jnp.stack(conv_s), jnp.stack(delta_s), jnp.stack(pool_s))
```

```python
import functools

import jax
import jax.numpy as jnp
from jax import lax
from jax.experimental import pallas as pl
from jax.experimental.pallas import tpu as pltpu
from jax.experimental.pallas import tpu_sc as plsc

F32 = jnp.float32
BF16 = jnp.bfloat16
I32 = jnp.int32

D_MODEL = 1024
DEPTH = 1
PAST_LEN = 16384
DN_HEADS = 8
DN_HEAD_DIM = 128
DN_WIDTH = DN_HEADS * DN_HEAD_DIM
QKV_WIDTH = 3 * DN_WIDTH
CONV_WIDTH = 4
DN_CHUNK = 64
POOL_WINDOWS = (2, 4, 8, 16)
POOL_GROUP_DIM = 128
POOL_WIDTH = len(POOL_WINDOWS) * POOL_GROUP_DIM
POOL_OUT_GROUP = D_MODEL // len(POOL_WINDOWS)
POOL_BUF = max(POOL_WINDOWS) - 1
PEER_HEADS = 8
PEER_NKEYS = 128
PEER_TOPK = 16
PEER_KEY_HALF = 128
PEER_HK = PEER_HEADS * PEER_TOPK
EPS = 1e-6

LANES = 128
SUBLANES = 8
CONV_PAD = SUBLANES
POOL_PAD = 16
VMEM_LIMIT = 56 * 1024 * 1024

NT_DIMS = (((1,), (1,)), ((), ()))
TN_DIMS = (((0,), (0,)), ((), ()))


def _dot(a, b):
    return jnp.dot(a.astype(BF16), b.astype(BF16), preferred_element_type=F32)


def _dot_nt(a, b):
    return lax.dot_general(a.astype(BF16), b.astype(BF16), NT_DIMS, preferred_element_type=F32)


def _split3(x):
    hi = x.astype(BF16)
    r1 = x - hi.astype(F32)
    mid = r1.astype(BF16)
    lo = (r1 - mid.astype(F32)).astype(BF16)
    return hi, mid, lo


def _silu(x):
    return x * jax.nn.sigmoid(x)


def _gelu(x):
    return 0.5 * x * (1.0 + lax.erf(x * (0.5 ** 0.5)))


def _softplus(x):
    return jnp.maximum(x, 0.0) + jnp.log(1.0 + jnp.exp(-jnp.abs(x)))


def _mod_rows(ref):
    m = ref[...]
    return m.reshape(m.shape[-2], m.shape[-1])


def _mod_spec(k, rows_per_batch, tm):
    if rows_per_batch >= tm:
        tiles = rows_per_batch // tm
        return pl.BlockSpec((1, 1, 1, D_MODEL), lambda i, *_: (k, i // tiles, 0, 0))
    return pl.BlockSpec((1, tm, D_MODEL), lambda i, *_: (k, i, 0))


def _const_spec(shape):
    nd = len(shape)
    return pl.BlockSpec(shape, lambda *_: (0,) * nd)


def _ada_body(c_ref, w_ref, b_ref, o_ref):
    o_ref[...] = _dot(_silu(c_ref[...]), w_ref[...]) + b_ref[...]


def _ada(c, w_ada, b_ada):
    n = c.shape[0]
    return pl.pallas_call(
        _ada_body,
        grid=(6,),
        in_specs=[pl.BlockSpec((n, D_MODEL), lambda j: (0, 0)),
                  pl.BlockSpec((D_MODEL, D_MODEL), lambda j: (0, j)),
                  pl.BlockSpec((1, D_MODEL), lambda j: (0, j))],
        out_specs=pl.BlockSpec((n, D_MODEL), lambda j: (0, j)),
        out_shape=jax.ShapeDtypeStruct((n, 6 * D_MODEL), F32),
        name="ada",
    )(c, w_ada, b_ada.reshape(1, -1))


_IN_BLOCKS = (("qkv", QKV_WIDTH), ("z", DN_WIDTH), ("pool", POOL_WIDTH),
              ("ga", D_MODEL), ("gb", D_MODEL), ("ba", LANES))
_IN_TOTAL = sum(w for _, w in _IN_BLOCKS)
_IN_F32 = ("ba",)
_IN_COL_CHUNK = 512


def _inproj_body(x_ref, sc_ref, sh_ref, g_ref, w_ref, *out_refs):
    x = x_ref[...]
    y = x * lax.rsqrt(jnp.mean(x * x, axis=-1, keepdims=True) + EPS) * g_ref[...]
    h = (y * (1.0 + _mod_rows(sc_ref)) + _mod_rows(sh_ref)).astype(BF16)
    off = 0
    for (_, width), o_ref in zip(_IN_BLOCKS, out_refs):
        for c0 in range(0, width, _IN_COL_CHUNK):
            cw = min(_IN_COL_CHUNK, width - c0)
            o_ref[:, c0:c0 + cw] = jnp.dot(h, w_ref[:, off + c0:off + c0 + cw],
                                           preferred_element_type=F32).astype(o_ref.dtype)
        off += width


def _inproj(x2d, mod, rows_per_batch, norm_g, w_cat, tm):
    t = x2d.shape[0]
    row = lambda w: pl.BlockSpec((tm, w), lambda i: (i, 0))
    return pl.pallas_call(
        _inproj_body,
        grid=(t // tm,),
        in_specs=[row(D_MODEL), _mod_spec(1, rows_per_batch, tm), _mod_spec(0, rows_per_batch, tm),
                  _const_spec((1, D_MODEL)),
                  pl.BlockSpec((D_MODEL, _IN_TOTAL), lambda i: (0, 0), pipeline_mode=pl.Buffered(1))],
        out_specs=[row(w) for _, w in _IN_BLOCKS],
        out_shape=[jax.ShapeDtypeStruct((t, w), F32 if name in _IN_F32 else BF16) for name, w in _IN_BLOCKS],
        compiler_params=pltpu.CompilerParams(vmem_limit_bytes=VMEM_LIMIT),
        name="inproj",
    )(x2d, mod, mod, norm_g.reshape(1, -1), w_cat)


def _mixer_body(C, Lv, start,
                qkv_ref, ba_ref, z_ref, pin_ref, ga_ref, gb_ref, cbuf_ref, s0_ref, pbuf_ref,
                convw_ref, alog_ref, dtb_ref, dng_ref, wpool_ref, pscale_ref,
                mixed_ref, nconv_ref, ns_ref, npool_ref,
                xp_scr, act_scr, s_scr, pp_scr, odn_scr):
    n = pl.program_id(1)
    last = pl.num_programs(1) - 1

    @pl.when(n == 0)
    def _load_state():
        xp_scr[0:CONV_PAD, :] = cbuf_ref[0]
        pp_scr[0:POOL_PAD, :] = pbuf_ref[0]
        s_scr[...] = s0_ref[0]

    xp_scr[CONV_PAD:CONV_PAD + C, :] = qkv_ref[0].astype(F32)
    for c0 in range(0, QKV_WIDTH, _IN_COL_CHUNK):
        cs = slice(c0, c0 + _IN_COL_CHUNK)
        y = xp_scr[CONV_PAD:CONV_PAD + C, cs] * convw_ref[CONV_WIDTH - 1:CONV_WIDTH, cs]
        for k in range(CONV_WIDTH - 1):
            r0 = CONV_PAD - (CONV_WIDTH - 1) + k
            y = y + xp_scr[r0:r0 + C, cs] * convw_ref[k:k + 1, cs]
        act_scr[:, cs] = _silu(y)

    ba = ba_ref[0]
    lane = lax.broadcasted_iota(I32, (C, LANES), 1)
    beta_all = jax.nn.sigmoid(ba)
    g_all = -jnp.exp(alog_ref[...]) * _softplus(ba + dtb_ref[...])
    if Lv < C:
        valid = lax.broadcasted_iota(I32, (C, LANES), 0) < Lv
        beta_all = jnp.where(valid, beta_all, 0.0)
        g_all = jnp.where(valid, g_all, 0.0)
    ii = lax.broadcasted_iota(I32, (C, C), 0)
    jj = lax.broadcasted_iota(I32, (C, C), 1)
    causal = ii >= jj
    strict = ii > jj
    tril = jnp.where(causal, 1.0, 0.0).astype(BF16)
    eye = jnp.where(ii == jj, 1.0, 0.0)
    gc_all = sum(jnp.dot(tril, part, preferred_element_type=F32) for part in _split3(g_all))
    if C < LANES:
        gc_sq = jnp.concatenate([gc_all, jnp.zeros((LANES - C, LANES), F32)], axis=0)
    else:
        gc_sq = gc_all
    gc_t = gc_sq.T

    H = range(DN_HEADS)
    hsl = [slice(h * DN_HEAD_DIM, (h + 1) * DN_HEAD_DIM) for h in H]
    beta = [jnp.sum(jnp.where(lane == h, beta_all, 0.0), axis=1, keepdims=True) for h in H]
    gcol = [jnp.sum(jnp.where(lane == DN_HEADS + h, gc_all, 0.0), axis=1, keepdims=True) for h in H]
    grow = [gc_t[DN_HEADS + h:DN_HEADS + h + 1, 0:C] for h in H]
    glast = [g[C - 1:C, :] for g in gcol]
    q = [act_scr[:, hsl[h]] for h in H]
    k = [act_scr[:, DN_WIDTH + h * DN_HEAD_DIM:DN_WIDTH + (h + 1) * DN_HEAD_DIM] for h in H]
    v = [act_scr[:, 2 * DN_WIDTH + h * DN_HEAD_DIM:2 * DN_WIDTH + (h + 1) * DN_HEAD_DIM] for h in H]
    q = [x * lax.rsqrt(jnp.sum(x * x, axis=-1, keepdims=True) + EPS) * (DN_HEAD_DIM ** -0.5) for x in q]
    k = [x * lax.rsqrt(jnp.sum(x * x, axis=-1, keepdims=True) + EPS) for x in k]
    kb = [k[h] * beta[h] for h in H]
    vb = [v[h] * beta[h] for h in H]
    decay = [jnp.where(causal, jnp.exp(jnp.where(causal, gcol[h] - grow[h], 0.0)), 0.0) for h in H]
    lower = [jnp.where(strict, _dot_nt(kb[h], k[h]) * decay[h], 0.0) for h in H]
    ainv = [eye - x for x in lower]
    pw = lower
    p = 1
    while 2 * p < C:
        pw = [_dot(x, x) for x in pw]
        ainv = [ainv[h] + _dot(ainv[h], pw[h]) for h in H]
        p *= 2
    sol = [_dot(ainv[h], jnp.concatenate([vb[h], kb[h] * jnp.exp(gcol[h])], axis=1)) for h in H]
    qk = [_dot_nt(q[h], k[h]) * decay[h] for h in H]
    k_tail = [k[h] * jnp.exp(glast[h] - gcol[h]) for h in H]
    S = [s_scr[h] for h in H]
    v_new = [sol[h][:, :DN_HEAD_DIM] - _dot(sol[h][:, DN_HEAD_DIM:], S[h]) for h in H]
    o = [_dot(q[h] * jnp.exp(gcol[h]), S[h]) + _dot(qk[h], v_new[h]) for h in H]
    for h in H:
        s_scr[h] = S[h] * jnp.exp(glast[h]) + lax.dot_general(
            k_tail[h].astype(BF16), v_new[h].astype(BF16), TN_DIMS, preferred_element_type=F32)
    for h in H:
        zf = z_ref[0, :, hsl[h]].astype(F32)
        odn_scr[:, hsl[h]] = (o[h] * lax.rsqrt(jnp.mean(o[h] * o[h], axis=-1, keepdims=True) + EPS)
                              * dng_ref[...] * _silu(zf))

    pp_scr[POOL_PAD:POOL_PAD + C, :] = pin_ref[0].astype(F32)
    pos = start + n * C + lax.broadcasted_iota(I32, (C, 1), 0)
    for gi, win in enumerate(POOL_WINDOWS):
        gs = slice(gi * POOL_GROUP_DIM, (gi + 1) * POOL_GROUP_DIM)
        xg = pp_scr[POOL_PAD:POOL_PAD + C, gs]
        ssum = xg
        for sft in range(1, win):
            ssum = ssum + pp_scr[POOL_PAD - sft:POOL_PAD - sft + C, gs]
        cnt = jnp.minimum(pos + 1, win).astype(F32)
        pooled = ssum / cnt - xg
        os_ = slice(gi * POOL_OUT_GROUP, (gi + 1) * POOL_OUT_GROUP)
        yp = _dot(pooled, wpool_ref[gi]) * pscale_ref[:, os_]
        mixed_ref[0, :, os_] = (jax.nn.sigmoid(ga_ref[0, :, os_].astype(F32)) * odn_scr[:, os_]
                                + jax.nn.sigmoid(gb_ref[0, :, os_].astype(F32)) * yp).astype(BF16)

    @pl.when(n == last)
    def _store_state():
        nconv_ref[0] = xp_scr[Lv + CONV_PAD - (CONV_WIDTH - 1):Lv + CONV_PAD, :]
        npool_ref[0] = pp_scr[Lv + POOL_PAD - POOL_BUF:Lv + POOL_PAD, :]
        ns_ref[0] = s_scr[...]

    xp_scr[0:CONV_PAD, :] = xp_scr[C:C + CONV_PAD, :]
    pp_scr[0:POOL_PAD, :] = pp_scr[C:C + POOL_PAD, :]


def _mixer(proj, conv_buf, s0, pool_buf, start, seq_len, C,
           conv_w, a_log, dt_bias, dn_norm_g, w_pool, pool_scale):
    b, lp, _ = proj["qkv"].shape
    nchunks = lp // C
    lv = seq_len - (nchunks - 1) * C
    cbuf = jnp.pad(conv_buf, ((0, 0), (CONV_PAD - (CONV_WIDTH - 1), 0), (0, 0)))
    pbuf = jnp.pad(pool_buf, ((0, 0), (POOL_PAD - POOL_BUF, 0), (0, 0)))
    lane_pad = lambda a: jnp.pad(a.reshape(1, -1), ((0, 0), (DN_HEADS, LANES - 2 * DN_HEADS)))
    chunk = lambda w: pl.BlockSpec((1, C, w), lambda i, j: (i, j, 0))
    state = lambda *s: pl.BlockSpec((1,) + s, lambda i, j: (i,) + (0,) * len(s))
    return pl.pallas_call(
        functools.partial(_mixer_body, C, lv, start),
        grid=(b, nchunks),
        in_specs=[chunk(QKV_WIDTH), chunk(LANES), chunk(DN_WIDTH), chunk(POOL_WIDTH),
                  chunk(D_MODEL), chunk(D_MODEL),
                  state(CONV_PAD, QKV_WIDTH), state(DN_HEADS, DN_HEAD_DIM, DN_HEAD_DIM),
                  state(POOL_PAD, POOL_WIDTH),
                  _const_spec((CONV_WIDTH, QKV_WIDTH)), _const_spec((1, LANES)), _const_spec((1, LANES)),
                  _const_spec((1, DN_HEAD_DIM)),
                  _const_spec((len(POOL_WINDOWS), POOL_GROUP_DIM, POOL_OUT_GROUP)),
                  _const_spec((1, D_MODEL))],
        out_specs=[chunk(D_MODEL), state(CONV_WIDTH - 1, QKV_WIDTH),
                   state(DN_HEADS, DN_HEAD_DIM, DN_HEAD_DIM), state(POOL_BUF, POOL_WIDTH)],
        out_shape=[jax.ShapeDtypeStruct((b, lp, D_MODEL), BF16),
                   jax.ShapeDtypeStruct((b, CONV_WIDTH - 1, QKV_WIDTH), F32),
                   jax.ShapeDtypeStruct((b, DN_HEADS, DN_HEAD_DIM, DN_HEAD_DIM), F32),
                   jax.ShapeDtypeStruct((b, POOL_BUF, POOL_WIDTH), F32)],
        scratch_shapes=[pltpu.VMEM((CONV_PAD + C + CONV_PAD, QKV_WIDTH), F32),
                        pltpu.VMEM((C, QKV_WIDTH), F32),
                        pltpu.VMEM((DN_HEADS, DN_HEAD_DIM, DN_HEAD_DIM), F32),
                        pltpu.VMEM((POOL_PAD + C + POOL_PAD, POOL_WIDTH), F32),
                        pltpu.VMEM((C, DN_WIDTH), F32)],
        compiler_params=pltpu.CompilerParams(dimension_semantics=("arbitrary", "arbitrary"),
                                             vmem_limit_bytes=VMEM_LIMIT),
        name="mixer",
    )(proj["qkv"], proj["ba"], proj["z"], proj["pool"], proj["ga"], proj["gb"], cbuf, s0, pbuf,
      conv_w, lane_pad(a_log), lane_pad(dt_bias), dn_norm_g.reshape(1, -1), w_pool,
      pool_scale.reshape(1, -1))


def _top16(s, ids, payload=None):
    big = float(2 ** 24)
    vals, sel, pays = [], [], []
    for _ in range(PEER_TOPK):
        m = jnp.max(s, axis=0, keepdims=True)
        am = jnp.min(jnp.where(s == m, ids, big), axis=0, keepdims=True)
        hit = ids == am
        if payload is not None:
            pays.append(jnp.max(jnp.where(hit, payload, -1.0), axis=0, keepdims=True))
        s = jnp.where(hit, -jnp.inf, s)
        vals.append(m)
        sel.append(am)
    out = (jnp.concatenate(vals, axis=0), jnp.concatenate(sel, axis=0))
    if payload is not None:
        out += (jnp.concatenate(pays, axis=0),)
    return out


_CAND_EDGE = 4


def _post_body(has_prev, has_fin, mixed_ref, x_ref, g1_ref, sc2_ref, sh2_ref, n2g_ref, wout_ref,
               wq_ref, keys_ref, *refs):
    refs = list(refs)
    prev_in = [refs.pop(0) for _ in range(2 if has_prev else 0)]
    fin_in = [refs.pop(0) for _ in range(4 if has_fin else 0)]
    x1_ref, h2_ref, idx_ref, gate_ref = refs[:4]
    extra_out = refs[4:]
    if has_prev:
        pre_ref, pgate_ref = prev_in
        extra_out.pop(0)[...] = _coef_words(pre_ref[...], pgate_ref[...])
    if has_fin:
        _final_body(*fin_in, extra_out.pop(0))
    tm = x_ref.shape[0]
    x1 = x_ref[...] + _mod_rows(g1_ref) * _dot(mixed_ref[...], wout_ref[...])
    x1_ref[...] = x1
    y = x1 * lax.rsqrt(jnp.mean(x1 * x1, axis=-1, keepdims=True) + EPS) * n2g_ref[...]
    h2 = y * (1.0 + _mod_rows(sc2_ref)) + _mod_rows(sh2_ref)
    h2_ref[...] = _pack_words(h2[:, :PACK_HALF], h2[:, PACK_HALF:])
    q = _dot(h2, wq_ref[...])

    K = PEER_TOPK
    key_id = lax.broadcasted_iota(I32, (PEER_NKEYS, 1), 0).astype(F32)
    r16 = lax.broadcasted_iota(I32, (K, 1), 0)
    cand_id = jnp.concatenate([(a * K + r16) for a in range(_CAND_EDGE)]
                              + [(r16 * K + b) for b in range(_CAND_EDGE)], axis=0).astype(F32)
    dup = r16 < _CAND_EDGE
    idx_rows, gate_rows = [], []
    for h in range(PEER_HEADS):
        half = []
        for p in range(2):
            c0 = (h * 2 + p) * PEER_KEY_HALF
            st = _dot_nt(keys_ref[h * 2 + p], q[:, c0:c0 + PEER_KEY_HALF])
            half.append(_top16(st, key_id))
        (s1, i1), (s2, i2) = half
        cand = jnp.concatenate(
            [s1[a:a + 1] + s2 for a in range(_CAND_EDGE)]
            + [jnp.where(dup, -jnp.inf, s1 + s2[b:b + 1]) for b in range(_CAND_EDGE)], axis=0)
        cidx = jnp.concatenate(
            [i1[a:a + 1] * PEER_NKEYS + i2 for a in range(_CAND_EDGE)]
            + [i1 * PEER_NKEYS + i2[b:b + 1] for b in range(_CAND_EDGE)], axis=0)
        best, _, eidx = _top16(cand, cand_id, cidx)
        e = jnp.exp(best - best[0:1])
        gate_rows.append(e / jnp.sum(e, axis=0, keepdims=True))
        idx_rows.append(eidx)
    idx_ref[...] = jnp.concatenate(idx_rows, axis=0).T.astype(I32)
    gate_ref[...] = jnp.concatenate(gate_rows, axis=0).T


def _post(mixed2d, x2d, mod, rows_per_batch, norm2_g, w_out, w_query, keys, tm, prev=None, fin=None):
    t = x2d.shape[0]
    steps = t // tm
    row = lambda w: pl.BlockSpec((tm, w), lambda i: (i, 0))
    in_specs = [row(D_MODEL), row(D_MODEL),
                _mod_spec(2, rows_per_batch, tm), _mod_spec(4, rows_per_batch, tm),
                _mod_spec(3, rows_per_batch, tm), _const_spec((1, D_MODEL)),
                _const_spec((D_MODEL, D_MODEL)), _const_spec((D_MODEL, 2 * PEER_HEADS * PEER_KEY_HALF)),
                _const_spec((2 * PEER_HEADS, PEER_NKEYS, PEER_KEY_HALF))]
    out_specs = [row(D_MODEL), row(PACK_HALF), row(PEER_HK), row(PEER_HK)]
    out_shape = [jax.ShapeDtypeStruct((t, D_MODEL), F32), jax.ShapeDtypeStruct((t, PACK_HALF), I32),
                 jax.ShapeDtypeStruct((t, PEER_HK), I32), jax.ShapeDtypeStruct((t, PEER_HK), F32)]
    args = [mixed2d, x2d, mod, mod, mod, norm2_g.reshape(1, -1), w_out, w_query, keys]
    if prev is not None:
        tp = prev[0].shape[0]
        prow = pl.BlockSpec((tp // steps, PEER_HK), lambda i: (i, 0))
        in_specs += [prow, prow]
        out_specs += [prow]
        out_shape += [jax.ShapeDtypeStruct((tp, PEER_HK), I32)]
        args += list(prev)
    if fin is not None:
        x1_f, peer_f, mod_f, rows_f, final_g = fin
        tf = x1_f.shape[0]
        frow = pl.BlockSpec((tf // steps, D_MODEL), lambda i: (i, 0))
        in_specs += [frow, frow, _mod_spec(5, rows_f, tf // steps), _const_spec((1, D_MODEL))]
        out_specs += [frow]
        out_shape += [jax.ShapeDtypeStruct((tf, D_MODEL), F32)]
        args += [x1_f, peer_f, mod_f, final_g.reshape(1, -1)]
    return pl.pallas_call(
        functools.partial(_post_body, prev is not None, fin is not None),
        grid=(steps,),
        in_specs=in_specs, out_specs=out_specs, out_shape=out_shape,
        compiler_params=pltpu.CompilerParams(vmem_limit_bytes=VMEM_LIMIT),
        name="post",
    )(*args)


SC_CORES = 2
SC_SUBCORES = 16
SC_LANES = 16
SC_WORKERS = SC_CORES * SC_SUBCORES
SC_TOKENS = 32
SC_SLOTS = 4
SC_JOB_HEADS = 2
SC_BF16_GROUP = 4
PACK_HALF = D_MODEL // 2
SC_CHUNKS = PACK_HALF // SC_LANES
PROMPT_PARTS = 4
EDGE_SPLITS = 4
COEF_LAG = 2
FIN_LAG = 3
ROW_TILE = 256


def _bf16_bits(v):
    return lax.bitcast_convert_type(v.astype(BF16).astype(F32), jnp.uint32)


def _pack_words(lo, hi):
    return lax.bitcast_convert_type((_bf16_bits(lo) >> 16) | _bf16_bits(hi), I32)


def _pack_body(x_ref, o_ref):
    o_ref[...] = _pack_words(x_ref[:, :PACK_HALF], x_ref[:, PACK_HALF:])


def _pack_table(tbl, rows=2 * ROW_TILE):
    e = tbl.shape[0]
    return pl.pallas_call(
        _pack_body, grid=(e // rows,),
        in_specs=[pl.BlockSpec((rows, D_MODEL), lambda i: (i, 0))],
        out_specs=pl.BlockSpec((rows, PACK_HALF), lambda i: (i, 0)),
        out_shape=jax.ShapeDtypeStruct((e, PACK_HALF), I32), name="pack_table")(tbl)


def _tree_sum(terms):
    terms = list(terms)
    while len(terms) > 1:
        terms = [a + b for a, b in zip(terms[0::2], terms[1::2])] + terms[len(terms) & ~1:]
    return terms[0]


def _unpack_pair(w):
    lo = plsc.bitcast(lax.shift_left(w, jnp.full(w.shape, 16, I32)), F32)
    hi = plsc.bitcast(w & jnp.full(w.shape, -65536, I32), F32)
    return lo, hi


def _sc_mesh():
    return plsc.VectorSubcoreMesh(core_axis_name="c", subcore_axis_name="s")


def _sc_worker():
    return lax.axis_index("s") * SC_CORES + lax.axis_index("c")


def _sc_jobs(table_hbm, idx_v, buf, sem, compute):
    per_tok = PEER_HEADS // SC_JOB_HEADS
    njobs = idx_v.shape[0] * per_tok
    nrows = SC_JOB_HEADS * PEER_TOPK

    def copy(j, slot):
        rows = idx_v.at[j // per_tok, pl.ds((j % per_tok) * nrows, nrows)]
        return pltpu.make_async_copy(table_hbm.at[rows], buf.at[slot], sem.at[slot])

    for s in range(SC_SLOTS):
        copy(s, s).start()

    def job(j, c):
        s = j % SC_SLOTS
        copy(j, s).wait()

        def head(i, cc):
            compute(j // per_tok, (j % per_tok) * SC_JOB_HEADS + i, s, i * PEER_TOPK)
            return cc
        lax.fori_loop(0, SC_JOB_HEADS, head, 0)

        @pl.when(j + SC_SLOTS < njobs)
        def _next():
            copy(j + SC_SLOTS, s).start()
        return c

    lax.fori_loop(0, njobs, job, 0)


def _peer_u_body(n_tok, idx_hbm, h2_hbm, u_hbm, pre_hbm, idx_v, h2_v, pre_v, ubuf, acc_v, sem):
    base = _sc_worker() * n_tok
    lane = lax.iota(I32, SC_LANES)

    def compute(tt, h, slot, r0):
        def chunk(cg, accs):
            cs = [pl.ds((cg * SC_BF16_GROUP + i) * SC_LANES, SC_LANES) for i in range(SC_BF16_GROUP)]
            xs = [plsc.bitcast(h2_v[tt, c], BF16) for c in cs]
            out = []
            for k, a in enumerate(accs):
                part = _tree_sum([plsc.bitcast(ubuf[slot, r0 + k, c], BF16) * x for c, x in zip(cs, xs)])
                lo, hi = _unpack_pair(plsc.bitcast(part, I32))
                out.append(a + (lo + hi))
            return tuple(out)
        zero = jnp.zeros((SC_LANES,), F32)
        accs = lax.fori_loop(0, SC_CHUNKS // SC_BF16_GROUP, chunk, (zero,) * PEER_TOPK)
        for k, a in enumerate(accs):
            acc_v[k, :] = a
        tot = zero
        for j in range(SC_LANES):
            tot = tot + plsc.load_gather(acc_v, [lane, (lane + j) & (SC_LANES - 1)])
        pre_v[tt, pl.ds(h * PEER_TOPK, PEER_TOPK)] = tot

    tb = idx_v.shape[0]

    def block(bi, c):
        t0 = base + bi * tb
        pltpu.sync_copy(idx_hbm.at[pl.ds(t0, tb)], idx_v)
        pltpu.sync_copy(h2_hbm.at[pl.ds(t0, tb)], h2_v)
        _sc_jobs(u_hbm, idx_v, ubuf, sem, compute)
        pltpu.sync_copy(pre_v, pre_hbm.at[pl.ds(t0, tb)])
        return c

    lax.fori_loop(0, n_tok // tb, block, 0)


def _peer_v_body(n_tok, idx_hbm, coef_hbm, v_hbm, out_hbm, idx_v, coef_v, out_v, vbuf, sem):
    base = _sc_worker() * n_tok
    zero = jnp.zeros((SC_LANES,), F32)

    def compute(tt, h, slot, r0):
        cvec = coef_v[tt, pl.ds(h * PEER_TOPK, PEER_TOPK)]
        cb = [plsc.bitcast(jnp.take_along_axis(cvec, jnp.full((SC_LANES,), k, I32), axis=0), BF16)
              for k in range(PEER_TOPK)]

        @plsc.parallel_loop(0, SC_CHUNKS, unroll=2)
        def _chunk(c):
            cs = pl.ds(c * SC_LANES, SC_LANES)
            prods = [plsc.bitcast(vbuf[slot, r0 + k, cs], BF16) * cb[k] for k in range(PEER_TOPK)]
            pairs = [_unpack_pair(plsc.bitcast(_tree_sum(prods[g:g + SC_BF16_GROUP]), I32))
                     for g in range(0, PEER_TOPK, SC_BF16_GROUP)]
            for half, off in ((0, 0), (1, PACK_HALF)):
                plsc.addupdate(out_v.at[tt, pl.ds(off + c * SC_LANES, SC_LANES)],
                               _tree_sum([p[half] for p in pairs]))

    tb = idx_v.shape[0]

    def block(bi, c):
        t0 = base + bi * tb
        pltpu.sync_copy(idx_hbm.at[pl.ds(t0, tb)], idx_v)
        pltpu.sync_copy(coef_hbm.at[pl.ds(t0, tb)], coef_v)

        def clear(i, cc):
            per_row = D_MODEL // SC_LANES
            out_v[i // per_row, pl.ds((i % per_row) * SC_LANES, SC_LANES)] = zero
            return cc
        lax.fori_loop(0, tb * (D_MODEL // SC_LANES), clear, 0)
        _sc_jobs(v_hbm, idx_v, vbuf, sem, compute)
        pltpu.sync_copy(out_v, out_hbm.at[pl.ds(t0, tb)])
        return c

    lax.fori_loop(0, n_tok // tb, block, 0)


def _peer_sc(body, idx, rows, table, out_width, name):
    t = idx.shape[0]
    assert t % SC_WORKERS == 0
    n_tok = t // SC_WORKERS
    tb = min(SC_TOKENS * (2 if body is _peer_u_body else 1), n_tok)
    assert n_tok % tb == 0 and tb * PEER_HEADS // SC_JOB_HEADS >= SC_SLOTS
    return pl.kernel(
        functools.partial(body, n_tok),
        out_type=jax.ShapeDtypeStruct((t, out_width), F32),
        mesh=_sc_mesh(),
        scratch_types=[pltpu.VMEM((tb, PEER_HK), I32),
                       pltpu.VMEM((tb, rows.shape[1]), rows.dtype),
                       pltpu.VMEM((tb, out_width), F32),
                       pltpu.VMEM((SC_SLOTS, SC_JOB_HEADS * PEER_TOPK, PACK_HALF), I32)]
                      + ([pltpu.VMEM((PEER_TOPK, SC_LANES), F32)] if body is _peer_u_body else [])
                      + [pltpu.SemaphoreType.DMA((SC_SLOTS,))],
        compiler_params=pltpu.CompilerParams(needs_layout_passes=False),
        name=name,
    )(idx, rows, table)


def _coef_words(pre, gates):
    return _pack_words(*(gates * _gelu(pre),) * 2)


def _coef_body(pre_ref, gate_ref, coef_ref):
    coef_ref[...] = _coef_words(pre_ref[...], gate_ref[...])


def _coef(pre, gates, tm):
    t = pre.shape[0]
    row = pl.BlockSpec((tm, PEER_HK), lambda i: (i, 0))
    return pl.pallas_call(_coef_body, grid=(t // tm,), in_specs=[row, row], out_specs=row,
                          out_shape=jax.ShapeDtypeStruct((t, PEER_HK), I32), name="coef")(pre, gates)


def _final_body(x1_ref, peer_ref, g2_ref, fng_ref, y_ref):
    x2 = x1_ref[...] + _mod_rows(g2_ref) * peer_ref[...]
    y_ref[...] = x2 * lax.rsqrt(jnp.mean(x2 * x2, axis=-1, keepdims=True) + EPS) * fng_ref[...]


def _final(x1, peer_out, mod, rows_per_batch, final_g, tm):
    t = x1.shape[0]
    row = pl.BlockSpec((tm, D_MODEL), lambda i: (i, 0))
    return pl.pallas_call(
        _final_body, grid=(t // tm,),
        in_specs=[row, row, _mod_spec(5, rows_per_batch, tm), _const_spec((1, D_MODEL))],
        out_specs=row, out_shape=jax.ShapeDtypeStruct((t, D_MODEL), F32), name="final",
    )(x1, peer_out, mod, final_g.reshape(1, -1))


def _expert_gather_v(g, coef, expert_v):
    g["peer_out"] = _peer_sc(_peer_v_body, g["idx"], coef, expert_v, D_MODEL, "peer_v")


def _front(x, mod, conv_buf, s0, pool_buf, start, chunk, tm, wts, prev, fin):
    b, l, _ = x.shape
    t = b * l
    x2d = x.reshape(t, D_MODEL)
    if l >= tm:
        modx = mod.reshape(b, 6, 1, D_MODEL).transpose(1, 0, 2, 3)
    else:
        modx = jnp.repeat(mod.reshape(b, 6, D_MODEL), l, axis=0).transpose(1, 0, 2)
    outs = _inproj(x2d, modx, l, wts["norm1_g"], wts["w_cat"], tm)
    lp = -(-l // chunk) * chunk
    proj = {}
    for (name, w), a in zip(_IN_BLOCKS, outs):
        a = a.reshape(b, l, w)
        proj[name] = a if lp == l else jnp.pad(a, ((0, 0), (0, lp - l), (0, 0)))
    mixed, nconv, ns, npool = _mixer(proj, conv_buf, s0, pool_buf, start, l, chunk,
                                     wts["conv_w"], wts["a_log"], wts["dt_bias"], wts["dn_norm_g"],
                                     wts["w_pool"], wts["pool_scale"])
    mixed2d = mixed[:, :l].reshape(t, D_MODEL)
    res = _post(mixed2d, x2d, modx, l, wts["norm2_g"], wts["w_out"], wts["w_query"], wts["keys"], tm,
                prev=None if prev is None else (prev["pre"], prev["gates"]),
                fin=None if fin is None else (fin["x1"], fin["peer_out"], fin["mod"], fin["l"],
                                              wts["final_norm_g"]))
    x1, h2, idx, gates = res[:4]
    extra = list(res[4:])
    coef_prev = extra.pop(0) if prev is not None else None
    y_fin = extra.pop(0).reshape(fin["b"], fin["l"], D_MODEL) if fin is not None else None
    pre = _peer_sc(_peer_u_body, idx, h2, wts["expert_u"], PEER_HK, "peer_u")
    g = dict(x1=x1, idx=idx, gates=gates, pre=pre, mod=modx, b=b, l=l, tm=tm,
             states=(nconv, ns, npool))
    return g, coef_prev, y_fin


def kernel(x_prompt, x_sample, c_prompt, c_sample, state_conv, state_delta, state_pool, w_ada, b_ada, norm1_g, w_in, conv_w, a_log, dt_bias, dn_norm_g, w_pool, pool_scale, w_out, norm2_g, w_query, sub_keys, expert_u, expert_v, final_norm_g):
    bp = x_prompt.shape[0]
    yp, ys = x_prompt, x_sample
    conv_p, delta_p, pool_p, conv_s, delta_s, pool_s = [], [], [], [], [], []
    zero_conv = jnp.zeros((bp, CONV_WIDTH - 1, QKV_WIDTH), F32)
    zero_delta = jnp.zeros((bp, DN_HEADS, DN_HEAD_DIM, DN_HEAD_DIM), F32)
    zero_pool = jnp.zeros((bp, POOL_BUF, POOL_WIDTH), F32)
    c_all = jnp.concatenate([c_prompt, c_sample], axis=0)
    for layer in range(DEPTH):
        wi = w_in[layer]
        o_b = QKV_WIDTH
        o_z = o_b + 2 * DN_HEADS
        w_ba = jnp.pad(wi[:, o_b:o_z], ((0, 0), (0, LANES - 2 * DN_HEADS)))
        w_cat = jnp.concatenate([wi[:, :o_b], wi[:, o_z:], w_ba], axis=1).astype(BF16)
        last = layer == DEPTH - 1
        wts = dict(
            norm1_g=norm1_g[layer], w_cat=w_cat, conv_w=conv_w[layer], a_log=a_log[layer],
            dt_bias=dt_bias[layer], dn_norm_g=dn_norm_g[layer], w_pool=w_pool[layer],
            pool_scale=pool_scale[layer], w_out=w_out[layer].astype(BF16), norm2_g=norm2_g[layer],
            w_query=w_query[layer].astype(BF16),
            keys=sub_keys[layer].reshape(2 * PEER_HEADS, PEER_NKEYS, PEER_KEY_HALF).astype(BF16),
            expert_u=_pack_table(expert_u[layer]), expert_v=_pack_table(expert_v[layer]),
            final_norm_g=final_norm_g if last else jnp.ones_like(final_norm_g))
        mod = _ada(c_all, w_ada[layer], b_ada[layer])
        assert last, "final norm is fused into the expert stage"
        step = bp // PROMPT_PARTS
        seq = x_prompt.shape[1]
        zeros = (zero_conv[:step], zero_delta[:step], zero_pool[:step])
        jobs, cuts = [], []
        for b0 in range(0, bp, step):
            n = EDGE_SPLITS if b0 in (0, bp - step) else 1
            cuts.append(n)
            for s0 in range(0, seq, seq // n):
                jobs.append((yp[b0:b0 + step, s0:s0 + seq // n], mod[b0:b0 + step],
                             zeros if s0 == 0 else None, s0, DN_CHUNK))
        jobs.append((ys, mod[bp:], (state_conv[layer], state_delta[layer], state_pool[layer]),
                     PAST_LEN, SUBLANES))
        groups = []
        for j, (xg, mg, states, start, chunk) in enumerate(jobs):
            prev = groups[j - COEF_LAG] if j >= COEF_LAG else None
            fin = groups[j - FIN_LAG] if j >= FIN_LAG else None
            if fin is not None and fin["x1"].shape[0] % (xg.shape[0] * xg.shape[1] // ROW_TILE):
                fin = None
            if states is None:
                states = groups[j - 1]["states"]
            g, coef_prev, y_fin = _front(xg, mg, *states, start, chunk, ROW_TILE, wts, prev, fin)
            if prev is not None:
                _expert_gather_v(prev, coef_prev, wts["expert_v"])
            if fin is not None:
                fin["y"] = y_fin
            groups.append(g)
        for g in groups[-COEF_LAG:]:
            _expert_gather_v(g, _coef(g["pre"], g["gates"], ROW_TILE), wts["expert_v"])
        for g in groups:
            if "y" not in g:
                g["y"] = _final(g["x1"], g["peer_out"], g["mod"], g["l"], wts["final_norm_g"],
                                g["tm"]).reshape(g["b"], g["l"], D_MODEL)
        rows, at = [], 0
        for n in cuts:
            rows.append(groups[at:at + n])
            at += n
        yp = jnp.concatenate([jnp.concatenate([g["y"] for g in row], axis=1) for row in rows], axis=0)
        cp, sp, pp = (jnp.concatenate(a, axis=0) for a in zip(*(row[-1]["states"] for row in rows)))
        ys = groups[-1]["y"]
        cs, ss, ps = groups[-1]["states"]
        conv_p.append(cp)
        delta_p.append(sp)
        pool_p.append(pp)
        conv_s.append(cs)
        delta_s.append(ss)
        pool_s.append(ps)
    return (yp, ys, jnp.stack(conv_p), jnp.stack(delta_p), jnp.stack(pool_p),
            jnp.stack(conv_s), jnp.stack(delta_s), jnp.stack(pool_s))
```

```python
import functools

import jax
import jax.numpy as jnp
from jax import lax
from jax.experimental import pallas as pl
from jax.experimental.pallas import tpu as pltpu
from jax.experimental.pallas import tpu_sc as plsc

F32 = jnp.float32
BF16 = jnp.bfloat16
I32 = jnp.int32

D_MODEL = 1024
DEPTH = 1
PAST_LEN = 16384
DN_HEADS = 8
DN_HEAD_DIM = 128
DN_WIDTH = DN_HEADS * DN_HEAD_DIM
QKV_WIDTH = 3 * DN_WIDTH
CONV_WIDTH = 4
DN_CHUNK = 64
POOL_WINDOWS = (2, 4, 8, 16)
POOL_GROUP_DIM = 128
POOL_WIDTH = len(POOL_WINDOWS) * POOL_GROUP_DIM
POOL_OUT_GROUP = D_MODEL // len(POOL_WINDOWS)
POOL_BUF = max(POOL_WINDOWS) - 1
PEER_HEADS = 8
PEER_NKEYS = 128
PEER_TOPK = 16
PEER_KEY_HALF = 128
PEER_HK = PEER_HEADS * PEER_TOPK
EPS = 1e-6

LANES = 128
SUBLANES = 8
CONV_PAD = SUBLANES
POOL_PAD = 16
VMEM_LIMIT = 56 * 1024 * 1024

NT_DIMS = (((1,), (1,)), ((), ()))
TN_DIMS = (((0,), (0,)), ((), ()))


def _dot(a, b):
    return jnp.dot(a.astype(BF16), b.astype(BF16), preferred_element_type=F32)


def _dot_nt(a, b):
    return lax.dot_general(a.astype(BF16), b.astype(BF16), NT_DIMS, preferred_element_type=F32)


def _split3(x):
    hi = x.astype(BF16)
    r1 = x - hi.astype(F32)
    mid = r1.astype(BF16)
    lo = (r1 - mid.astype(F32)).astype(BF16)
    return hi, mid, lo


def _silu(x):
    return x * jax.nn.sigmoid(x)


def _gelu(x):
    return 0.5 * x * (1.0 + lax.erf(x * (0.5 ** 0.5)))


def _softplus(x):
    return jnp.maximum(x, 0.0) + jnp.log(1.0 + jnp.exp(-jnp.abs(x)))


def _mod_rows(ref):
    m = ref[...]
    return m.reshape(m.shape[-2], m.shape[-1])


def _mod_spec(k, rows_per_batch, tm):
    if rows_per_batch >= tm:
        tiles = rows_per_batch // tm
        return pl.BlockSpec((1, 1, 1, D_MODEL), lambda i, *_: (k, i // tiles, 0, 0))
    return pl.BlockSpec((1, tm, D_MODEL), lambda i, *_: (k, i, 0))


def _const_spec(shape):
    nd = len(shape)
    return pl.BlockSpec(shape, lambda *_: (0,) * nd)


def _ada_body(c_ref, w_ref, b_ref, o_ref):
    o_ref[...] = _dot(_silu(c_ref[...]), w_ref[...]) + b_ref[...]


def _ada(c, w_ada, b_ada):
    n = c.shape[0]
    return pl.pallas_call(
        _ada_body,
        grid=(6,),
        in_specs=[pl.BlockSpec((n, D_MODEL), lambda j: (0, 0)),
                  pl.BlockSpec((D_MODEL, D_MODEL), lambda j: (0, j)),
                  pl.BlockSpec((1, D_MODEL), lambda j: (0, j))],
        out_specs=pl.BlockSpec((n, D_MODEL), lambda j: (0, j)),
        out_shape=jax.ShapeDtypeStruct((n, 6 * D_MODEL), F32),
        name="ada",
    )(c, w_ada, b_ada.reshape(1, -1))


_IN_BLOCKS = (("qkv", QKV_WIDTH), ("z", DN_WIDTH), ("pool", POOL_WIDTH),
              ("ga", D_MODEL), ("gb", D_MODEL), ("ba", LANES))
_IN_TOTAL = sum(w for _, w in _IN_BLOCKS)
_IN_F32 = ("ba",)
_IN_COL_CHUNK = 512


def _inproj_body(x_ref, sc_ref, sh_ref, g_ref, w_ref, *out_refs):
    x = x_ref[...]
    y = x * lax.rsqrt(jnp.mean(x * x, axis=-1, keepdims=True) + EPS) * g_ref[...]
    h = (y * (1.0 + _mod_rows(sc_ref)) + _mod_rows(sh_ref)).astype(BF16)
    off = 0
    for (_, width), o_ref in zip(_IN_BLOCKS, out_refs):
        for c0 in range(0, width, _IN_COL_CHUNK):
            cw = min(_IN_COL_CHUNK, width - c0)
            o_ref[:, c0:c0 + cw] = jnp.dot(h, w_ref[:, off + c0:off + c0 + cw],
                                           preferred_element_type=F32).astype(o_ref.dtype)
        off += width


def _inproj(x2d, mod, rows_per_batch, norm_g, w_cat, tm):
    t = x2d.shape[0]
    row = lambda w: pl.BlockSpec((tm, w), lambda i: (i, 0))
    return pl.pallas_call(
        _inproj_body,
        grid=(t // tm,),
        in_specs=[row(D_MODEL), _mod_spec(1, rows_per_batch, tm), _mod_spec(0, rows_per_batch, tm),
                  _const_spec((1, D_MODEL)),
                  pl.BlockSpec((D_MODEL, _IN_TOTAL), lambda i: (0, 0), pipeline_mode=pl.Buffered(1))],
        out_specs=[row(w) for _, w in _IN_BLOCKS],
        out_shape=[jax.ShapeDtypeStruct((t, w), F32 if name in _IN_F32 else BF16) for name, w in _IN_BLOCKS],
        compiler_params=pltpu.CompilerParams(vmem_limit_bytes=VMEM_LIMIT),
        name="inproj",
    )(x2d, mod, mod, norm_g.reshape(1, -1), w_cat)


def _mixer_body(C, Lv, start,
                qkv_ref, ba_ref, z_ref, pin_ref, ga_ref, gb_ref, cbuf_ref, s0_ref, pbuf_ref,
                convw_ref, alog_ref, dtb_ref, dng_ref, wpool_ref, pscale_ref,
                mixed_ref, nconv_ref, ns_ref, npool_ref,
                xp_scr, act_scr, s_scr, pp_scr, odn_scr):
    n = pl.program_id(1)
    last = pl.num_programs(1) - 1

    @pl.when(n == 0)
    def _load_state():
        xp_scr[0:CONV_PAD, :] = cbuf_ref[0]
        pp_scr[0:POOL_PAD, :] = pbuf_ref[0]
        s_scr[...] = s0_ref[0]

    xp_scr[CONV_PAD:CONV_PAD + C, :] = qkv_ref[0].astype(F32)
    for c0 in range(0, QKV_WIDTH, _IN_COL_CHUNK):
        cs = slice(c0, c0 + _IN_COL_CHUNK)
        y = xp_scr[CONV_PAD:CONV_PAD + C, cs] * convw_ref[CONV_WIDTH - 1:CONV_WIDTH, cs]
        for k in range(CONV_WIDTH - 1):
            r0 = CONV_PAD - (CONV_WIDTH - 1) + k
            y = y + xp_scr[r0:r0 + C, cs] * convw_ref[k:k + 1, cs]
        act_scr[:, cs] = _silu(y)

    ba = ba_ref[0]
    lane = lax.broadcasted_iota(I32, (C, LANES), 1)
    beta_all = jax.nn.sigmoid(ba)
    g_all = -jnp.exp(alog_ref[...]) * _softplus(ba + dtb_ref[...])
    if Lv < C:
        valid = lax.broadcasted_iota(I32, (C, LANES), 0) < Lv
        beta_all = jnp.where(valid, beta_all, 0.0)
        g_all = jnp.where(valid, g_all, 0.0)
    ii = lax.broadcasted_iota(I32, (C, C), 0)
    jj = lax.broadcasted_iota(I32, (C, C), 1)
    causal = ii >= jj
    strict = ii > jj
    tril = jnp.where(causal, 1.0, 0.0).astype(BF16)
    eye = jnp.where(ii == jj, 1.0, 0.0)
    gc_all = sum(jnp.dot(tril, part, preferred_element_type=F32) for part in _split3(g_all))
    if C < LANES:
        gc_sq = jnp.concatenate([gc_all, jnp.zeros((LANES - C, LANES), F32)], axis=0)
    else:
        gc_sq = gc_all
    gc_t = gc_sq.T

    H = range(DN_HEADS)
    hsl = [slice(h * DN_HEAD_DIM, (h + 1) * DN_HEAD_DIM) for h in H]
    beta = [jnp.sum(jnp.where(lane == h, beta_all, 0.0), axis=1, keepdims=True) for h in H]
    gcol = [jnp.sum(jnp.where(lane == DN_HEADS + h, gc_all, 0.0), axis=1, keepdims=True) for h in H]
    grow = [gc_t[DN_HEADS + h:DN_HEADS + h + 1, 0:C] for h in H]
    glast = [g[C - 1:C, :] for g in gcol]
    q = [act_scr[:, hsl[h]] for h in H]
    k = [act_scr[:, DN_WIDTH + h * DN_HEAD_DIM:DN_WIDTH + (h + 1) * DN_HEAD_DIM] for h in H]
    v = [act_scr[:, 2 * DN_WIDTH + h * DN_HEAD_DIM:2 * DN_WIDTH + (h + 1) * DN_HEAD_DIM] for h in H]
    q = [x * lax.rsqrt(jnp.sum(x * x, axis=-1, keepdims=True) + EPS) * (DN_HEAD_DIM ** -0.5) for x in q]
    k = [x * lax.rsqrt(jnp.sum(x * x, axis=-1, keepdims=True) + EPS) for x in k]
    kb = [k[h] * beta[h] for h in H]
    vb = [v[h] * beta[h] for h in H]
    decay = [jnp.where(causal, jnp.exp(jnp.where(causal, gcol[h] - grow[h], 0.0)), 0.0) for h in H]
    lower = [jnp.where(strict, _dot_nt(kb[h], k[h]) * decay[h], 0.0) for h in H]
    ainv = [eye - x for x in lower]
    pw = lower
    p = 1
    while 2 * p < C:
        pw = [_dot(x, x) for x in pw]
        ainv = [ainv[h] + _dot(ainv[h], pw[h]) for h in H]
        p *= 2
    sol = [_dot(ainv[h], jnp.concatenate([vb[h], kb[h] * jnp.exp(gcol[h])], axis=1)) for h in H]
    qk = [_dot_nt(q[h], k[h]) * decay[h] for h in H]
    k_tail = [k[h] * jnp.exp(glast[h] - gcol[h]) for h in H]
    S = [s_scr[h] for h in H]
    v_new = [sol[h][:, :DN_HEAD_DIM] - _dot(sol[h][:, DN_HEAD_DIM:], S[h]) for h in H]
    o = [_dot(q[h] * jnp.exp(gcol[h]), S[h]) + _dot(qk[h], v_new[h]) for h in H]
    for h in H:
        s_scr[h] = S[h] * jnp.exp(glast[h]) + lax.dot_general(
            k_tail[h].astype(BF16), v_new[h].astype(BF16), TN_DIMS, preferred_element_type=F32)
    for h in H:
        zf = z_ref[0, :, hsl[h]].astype(F32)
        odn_scr[:, hsl[h]] = (o[h] * lax.rsqrt(jnp.mean(o[h] * o[h], axis=-1, keepdims=True) + EPS)
                              * dng_ref[...] * _silu(zf))

    pp_scr[POOL_PAD:POOL_PAD + C, :] = pin_ref[0].astype(F32)
    pos = start + n * C + lax.broadcasted_iota(I32, (C, 1), 0)
    for gi, win in enumerate(POOL_WINDOWS):
        gs = slice(gi * POOL_GROUP_DIM, (gi + 1) * POOL_GROUP_DIM)
        xg = pp_scr[POOL_PAD:POOL_PAD + C, gs]
        ssum = xg
        for sft in range(1, win):
            ssum = ssum + pp_scr[POOL_PAD - sft:POOL_PAD - sft + C, gs]
        cnt = jnp.minimum(pos + 1, win).astype(F32)
        pooled = ssum / cnt - xg
        os_ = slice(gi * POOL_OUT_GROUP, (gi + 1) * POOL_OUT_GROUP)
        yp = _dot(pooled, wpool_ref[gi]) * pscale_ref[:, os_]
        mixed_ref[0, :, os_] = (jax.nn.sigmoid(ga_ref[0, :, os_].astype(F32)) * odn_scr[:, os_]
                                + jax.nn.sigmoid(gb_ref[0, :, os_].astype(F32)) * yp).astype(BF16)

    @pl.when(n == last)
    def _store_state():
        nconv_ref[0] = xp_scr[Lv + CONV_PAD - (CONV_WIDTH - 1):Lv + CONV_PAD, :]
        npool_ref[0] = pp_scr[Lv + POOL_PAD - POOL_BUF:Lv + POOL_PAD, :]
        ns_ref[0] = s_scr[...]

    xp_scr[0:CONV_PAD, :] = xp_scr[C:C + CONV_PAD, :]
    pp_scr[0:POOL_PAD, :] = pp_scr[C:C + POOL_PAD, :]


def _mixer(proj, conv_buf, s0, pool_buf, start, seq_len, C,
           conv_w, a_log, dt_bias, dn_norm_g, w_pool, pool_scale):
    b, lp, _ = proj["qkv"].shape
    nchunks = lp // C
    lv = seq_len - (nchunks - 1) * C
    cbuf = jnp.pad(conv_buf, ((0, 0), (CONV_PAD - (CONV_WIDTH - 1), 0), (0, 0)))
    pbuf = jnp.pad(pool_buf, ((0, 0), (POOL_PAD - POOL_BUF, 0), (0, 0)))
    lane_pad = lambda a: jnp.pad(a.reshape(1, -1), ((0, 0), (DN_HEADS, LANES - 2 * DN_HEADS)))
    chunk = lambda w: pl.BlockSpec((1, C, w), lambda i, j: (i, j, 0))
    state = lambda *s: pl.BlockSpec((1,) + s, lambda i, j: (i,) + (0,) * len(s))
    return pl.pallas_call(
        functools.partial(_mixer_body, C, lv, start),
        grid=(b, nchunks),
        in_specs=[chunk(QKV_WIDTH), chunk(LANES), chunk(DN_WIDTH), chunk(POOL_WIDTH),
                  chunk(D_MODEL), chunk(D_MODEL),
                  state(CONV_PAD, QKV_WIDTH), state(DN_HEADS, DN_HEAD_DIM, DN_HEAD_DIM),
                  state(POOL_PAD, POOL_WIDTH),
                  _const_spec((CONV_WIDTH, QKV_WIDTH)), _const_spec((1, LANES)), _const_spec((1, LANES)),
                  _const_spec((1, DN_HEAD_DIM)),
                  _const_spec((len(POOL_WINDOWS), POOL_GROUP_DIM, POOL_OUT_GROUP)),
                  _const_spec((1, D_MODEL))],
        out_specs=[chunk(D_MODEL), state(CONV_WIDTH - 1, QKV_WIDTH),
                   state(DN_HEADS, DN_HEAD_DIM, DN_HEAD_DIM), state(POOL_BUF, POOL_WIDTH)],
        out_shape=[jax.ShapeDtypeStruct((b, lp, D_MODEL), BF16),
                   jax.ShapeDtypeStruct((b, CONV_WIDTH - 1, QKV_WIDTH), F32),
                   jax.ShapeDtypeStruct((b, DN_HEADS, DN_HEAD_DIM, DN_HEAD_DIM), F32),
                   jax.ShapeDtypeStruct((b, POOL_BUF, POOL_WIDTH), F32)],
        scratch_shapes=[pltpu.VMEM((CONV_PAD + C + CONV_PAD, QKV_WIDTH), F32),
                        pltpu.VMEM((C, QKV_WIDTH), F32),
                        pltpu.VMEM((DN_HEADS, DN_HEAD_DIM, DN_HEAD_DIM), F32),
                        pltpu.VMEM((POOL_PAD + C + POOL_PAD, POOL_WIDTH), F32),
                        pltpu.VMEM((C, DN_WIDTH), F32)],
        compiler_params=pltpu.CompilerParams(dimension_semantics=("arbitrary", "arbitrary"),
                                             vmem_limit_bytes=VMEM_LIMIT),
        name="mixer",
    )(proj["qkv"], proj["ba"], proj["z"], proj["pool"], proj["ga"], proj["gb"], cbuf, s0, pbuf,
      conv_w, lane_pad(a_log), lane_pad(dt_bias), dn_norm_g.reshape(1, -1), w_pool,
      pool_scale.reshape(1, -1))


def _top16(s, ids, payload=None):
    big = float(2 ** 24)
    vals, sel, pays = [], [], []
    for _ in range(PEER_TOPK):
        m = jnp.max(s, axis=0, keepdims=True)
        am = jnp.min(jnp.where(s == m, ids, big), axis=0, keepdims=True)
        hit = ids == am
        if payload is not None:
            pays.append(jnp.max(jnp.where(hit, payload, -1.0), axis=0, keepdims=True))
        s = jnp.where(hit, -jnp.inf, s)
        vals.append(m)
        sel.append(am)
    out = (jnp.concatenate(vals, axis=0), jnp.concatenate(sel, axis=0))
    if payload is not None:
        out += (jnp.concatenate(pays, axis=0),)
    return out


_CAND_EDGE = 4


def _post_body(has_prev, has_fin, mixed_ref, x_ref, g1_ref, sc2_ref, sh2_ref, n2g_ref, wout_ref,
               wq_ref, keys_ref, *refs):
    refs = list(refs)
    prev_in = [refs.pop(0) for _ in range(2 if has_prev else 0)]
    fin_in = [refs.pop(0) for _ in range(4 if has_fin else 0)]
    x1_ref, h2_ref, idx_ref, gate_ref = refs[:4]
    extra_out = refs[4:]
    if has_prev:
        pre_ref, pgate_ref = prev_in
        extra_out.pop(0)[...] = _coef_words(pre_ref[...], pgate_ref[...])
    if has_fin:
        _final_body(*fin_in, extra_out.pop(0))
    tm = x_ref.shape[0]
    x1 = x_ref[...] + _mod_rows(g1_ref) * _dot(mixed_ref[...], wout_ref[...])
    x1_ref[...] = x1
    y = x1 * lax.rsqrt(jnp.mean(x1 * x1, axis=-1, keepdims=True) + EPS) * n2g_ref[...]
    h2 = y * (1.0 + _mod_rows(sc2_ref)) + _mod_rows(sh2_ref)
    h2_ref[...] = _pack_words(h2[:, :PACK_HALF], h2[:, PACK_HALF:])
    q = _dot(h2, wq_ref[...])

    K = PEER_TOPK
    key_id = lax.broadcasted_iota(I32, (PEER_NKEYS, 1), 0).astype(F32)
    r16 = lax.broadcasted_iota(I32, (K, 1), 0)
    cand_id = jnp.concatenate([(a * K + r16) for a in range(_CAND_EDGE)]
                              + [(r16 * K + b) for b in range(_CAND_EDGE)], axis=0).astype(F32)
    dup = r16 < _CAND_EDGE
    idx_rows, gate_rows = [], []
    for h in range(PEER_HEADS):
        half = []
        for p in range(2):
            c0 = (h * 2 + p) * PEER_KEY_HALF
            st = _dot_nt(keys_ref[h * 2 + p], q[:, c0:c0 + PEER_KEY_HALF])
            half.append(_top16(st, key_id))
        (s1, i1), (s2, i2) = half
        cand = jnp.concatenate(
            [s1[a:a + 1] + s2 for a in range(_CAND_EDGE)]
            + [jnp.where(dup, -jnp.inf, s1 + s2[b:b + 1]) for b in range(_CAND_EDGE)], axis=0)
        cidx = jnp.concatenate(
            [i1[a:a + 1] * PEER_NKEYS + i2 for a in range(_CAND_EDGE)]
            + [i1 * PEER_NKEYS + i2[b:b + 1] for b in range(_CAND_EDGE)], axis=0)
        best, _, eidx = _top16(cand, cand_id, cidx)
        e = jnp.exp(best - best[0:1])
        gate_rows.append(e / jnp.sum(e, axis=0, keepdims=True))
        idx_rows.append(eidx)
    idx_ref[...] = jnp.concatenate(idx_rows, axis=0).T.astype(I32)
    gate_ref[...] = jnp.concatenate(gate_rows, axis=0).T


def _post(mixed2d, x2d, mod, rows_per_batch, norm2_g, w_out, w_query, keys, tm, prev=None, fin=None):
    t = x2d.shape[0]
    steps = t // tm
    row = lambda w: pl.BlockSpec((tm, w), lambda i: (i, 0))
    in_specs = [row(D_MODEL), row(D_MODEL),
                _mod_spec(2, rows_per_batch, tm), _mod_spec(4, rows_per_batch, tm),
                _mod_spec(3, rows_per_batch, tm), _const_spec((1, D_MODEL)),
                _const_spec((D_MODEL, D_MODEL)), _const_spec((D_MODEL, 2 * PEER_HEADS * PEER_KEY_HALF)),
                _const_spec((2 * PEER_HEADS, PEER_NKEYS, PEER_KEY_HALF))]
    out_specs = [row(D_MODEL), row(PACK_HALF), row(PEER_HK), row(PEER_HK)]
    out_shape = [jax.ShapeDtypeStruct((t, D_MODEL), F32), jax.ShapeDtypeStruct((t, PACK_HALF), I32),
                 jax.ShapeDtypeStruct((t, PEER_HK), I32), jax.ShapeDtypeStruct((t, PEER_HK), F32)]
    args = [mixed2d, x2d, mod, mod, mod, norm2_g.reshape(1, -1), w_out, w_query, keys]
    if prev is not None:
        tp = prev[0].shape[0]
        prow = pl.BlockSpec((tp // steps, PEER_HK), lambda i: (i, 0))
        in_specs += [prow, prow]
        out_specs += [prow]
        out_shape += [jax.ShapeDtypeStruct((tp, PEER_HK), I32)]
        args += list(prev)
    if fin is not None:
        x1_f, peer_f, mod_f, rows_f, final_g = fin
        tf = x1_f.shape[0]
        frow = pl.BlockSpec((tf // steps, D_MODEL), lambda i: (i, 0))
        in_specs += [frow, frow, _mod_spec(5, rows_f, tf // steps), _const_spec((1, D_MODEL))]
        out_specs += [frow]
        out_shape += [jax.ShapeDtypeStruct((tf, D_MODEL), F32)]
        args += [x1_f, peer_f, mod_f, final_g.reshape(1, -1)]
    return pl.pallas_call(
        functools.partial(_post_body, prev is not None, fin is not None),
        grid=(steps,),
        in_specs=in_specs, out_specs=out_specs, out_shape=out_shape,
        compiler_params=pltpu.CompilerParams(vmem_limit_bytes=VMEM_LIMIT),
        name="post",
    )(*args)


SC_CORES = 2
SC_SUBCORES = 16
SC_LANES = 16
SC_WORKERS = SC_CORES * SC_SUBCORES
SC_TOKENS = 32
SC_SLOTS = 4
SC_JOB_HEADS = 2
SC_BF16_GROUP = 4
PACK_HALF = D_MODEL // 2
SC_CHUNKS = PACK_HALF // SC_LANES
PROMPT_PARTS = 8
EDGE_SPLITS = 2
RAMP_PARTS = 2
COEF_LAG = 2
FIN_LAG = 3
ROW_TILE = 256


def _bf16_bits(v):
    return lax.bitcast_convert_type(v.astype(BF16).astype(F32), jnp.uint32)


def _pack_words(lo, hi):
    return lax.bitcast_convert_type((_bf16_bits(lo) >> 16) | _bf16_bits(hi), I32)


def _pack_body(x_ref, o_ref):
    o_ref[...] = _pack_words(x_ref[:, :PACK_HALF], x_ref[:, PACK_HALF:])


def _pack_table(tbl, rows=2 * ROW_TILE):
    e = tbl.shape[0]
    return pl.pallas_call(
        _pack_body, grid=(e // rows,),
        in_specs=[pl.BlockSpec((rows, D_MODEL), lambda i: (i, 0))],
        out_specs=pl.BlockSpec((rows, PACK_HALF), lambda i: (i, 0)),
        out_shape=jax.ShapeDtypeStruct((e, PACK_HALF), I32), name="pack_table")(tbl)


def _tree_sum(terms):
    terms = list(terms)
    while len(terms) > 1:
        terms = [a + b for a, b in zip(terms[0::2], terms[1::2])] + terms[len(terms) & ~1:]
    return terms[0]


def _unpack_pair(w):
    lo = plsc.bitcast(lax.shift_left(w, jnp.full(w.shape, 16, I32)), F32)
    hi = plsc.bitcast(w & jnp.full(w.shape, -65536, I32), F32)
    return lo, hi


def _sc_mesh():
    return plsc.VectorSubcoreMesh(core_axis_name="c", subcore_axis_name="s")


def _sc_worker():
    return lax.axis_index("s") * SC_CORES + lax.axis_index("c")


def _sc_jobs(table_hbm, idx_v, buf, sem, compute):
    per_tok = PEER_HEADS // SC_JOB_HEADS
    njobs = idx_v.shape[0] * per_tok
    nrows = SC_JOB_HEADS * PEER_TOPK

    def copy(j, slot):
        rows = idx_v.at[j // per_tok, pl.ds((j % per_tok) * nrows, nrows)]
        return pltpu.make_async_copy(table_hbm.at[rows], buf.at[slot], sem.at[slot])

    for s in range(SC_SLOTS):
        copy(s, s).start()

    def job(j, c):
        s = j % SC_SLOTS
        copy(j, s).wait()

        def head(i, cc):
            compute(j // per_tok, (j % per_tok) * SC_JOB_HEADS + i, s, i * PEER_TOPK)
            return cc
        lax.fori_loop(0, SC_JOB_HEADS, head, 0)

        @pl.when(j + SC_SLOTS < njobs)
        def _next():
            copy(j + SC_SLOTS, s).start()
        return c

    lax.fori_loop(0, njobs, job, 0)


def _peer_u_body(n_tok, idx_hbm, h2_hbm, u_hbm, pre_hbm, idx_v, h2_v, pre_v, ubuf, acc_v, sem):
    base = _sc_worker() * n_tok
    lane = lax.iota(I32, SC_LANES)

    def compute(tt, h, slot, r0):
        def chunk(cg, accs):
            cs = [pl.ds((cg * SC_BF16_GROUP + i) * SC_LANES, SC_LANES) for i in range(SC_BF16_GROUP)]
            xs = [plsc.bitcast(h2_v[tt, c], BF16) for c in cs]
            out = []
            for k, a in enumerate(accs):
                part = _tree_sum([plsc.bitcast(ubuf[slot, r0 + k, c], BF16) * x for c, x in zip(cs, xs)])
                lo, hi = _unpack_pair(plsc.bitcast(part, I32))
                out.append(a + (lo + hi))
            return tuple(out)
        zero = jnp.zeros((SC_LANES,), F32)
        accs = lax.fori_loop(0, SC_CHUNKS // SC_BF16_GROUP, chunk, (zero,) * PEER_TOPK)
        for k, a in enumerate(accs):
            acc_v[k, :] = a
        tot = zero
        for j in range(SC_LANES):
            tot = tot + plsc.load_gather(acc_v, [lane, (lane + j) & (SC_LANES - 1)])
        pre_v[tt, pl.ds(h * PEER_TOPK, PEER_TOPK)] = tot

    tb = idx_v.shape[0]

    def block(bi, c):
        t0 = base + bi * tb
        pltpu.sync_copy(idx_hbm.at[pl.ds(t0, tb)], idx_v)
        pltpu.sync_copy(h2_hbm.at[pl.ds(t0, tb)], h2_v)
        _sc_jobs(u_hbm, idx_v, ubuf, sem, compute)
        pltpu.sync_copy(pre_v, pre_hbm.at[pl.ds(t0, tb)])
        return c

    lax.fori_loop(0, n_tok // tb, block, 0)


def _peer_v_body(n_tok, idx_hbm, coef_hbm, v_hbm, out_hbm, idx_v, coef_v, out_v, vbuf, sem):
    base = _sc_worker() * n_tok
    zero = jnp.zeros((SC_LANES,), F32)

    def compute(tt, h, slot, r0):
        cvec = coef_v[tt, pl.ds(h * PEER_TOPK, PEER_TOPK)]
        cb = [plsc.bitcast(jnp.take_along_axis(cvec, jnp.full((SC_LANES,), k, I32), axis=0), BF16)
              for k in range(PEER_TOPK)]

        @plsc.parallel_loop(0, SC_CHUNKS, unroll=2)
        def _chunk(c):
            cs = pl.ds(c * SC_LANES, SC_LANES)
            prods = [plsc.bitcast(vbuf[slot, r0 + k, cs], BF16) * cb[k] for k in range(PEER_TOPK)]
            pairs = [_unpack_pair(plsc.bitcast(_tree_sum(prods[g:g + SC_BF16_GROUP]), I32))
                     for g in range(0, PEER_TOPK, SC_BF16_GROUP)]
            for half, off in ((0, 0), (1, PACK_HALF)):
                plsc.addupdate(out_v.at[tt, pl.ds(off + c * SC_LANES, SC_LANES)],
                               _tree_sum([p[half] for p in pairs]))

    tb = idx_v.shape[0]

    def block(bi, c):
        t0 = base + bi * tb
        pltpu.sync_copy(idx_hbm.at[pl.ds(t0, tb)], idx_v)
        pltpu.sync_copy(coef_hbm.at[pl.ds(t0, tb)], coef_v)

        def clear(i, cc):
            per_row = D_MODEL // SC_LANES
            out_v[i // per_row, pl.ds((i % per_row) * SC_LANES, SC_LANES)] = zero
            return cc
        lax.fori_loop(0, tb * (D_MODEL // SC_LANES), clear, 0)
        _sc_jobs(v_hbm, idx_v, vbuf, sem, compute)
        pltpu.sync_copy(out_v, out_hbm.at[pl.ds(t0, tb)])
        return c

    lax.fori_loop(0, n_tok // tb, block, 0)


def _peer_sc(body, idx, rows, table, out_width, name):
    t = idx.shape[0]
    assert t % SC_WORKERS == 0
    n_tok = t // SC_WORKERS
    tb = min(SC_TOKENS * (2 if body is _peer_u_body else 1), n_tok)
    assert n_tok % tb == 0 and tb * PEER_HEADS // SC_JOB_HEADS >= SC_SLOTS
    return pl.kernel(
        functools.partial(body, n_tok),
        out_type=jax.ShapeDtypeStruct((t, out_width), F32),
        mesh=_sc_mesh(),
        scratch_types=[pltpu.VMEM((tb, PEER_HK), I32),
                       pltpu.VMEM((tb, rows.shape[1]), rows.dtype),
                       pltpu.VMEM((tb, out_width), F32),
                       pltpu.VMEM((SC_SLOTS, SC_JOB_HEADS * PEER_TOPK, PACK_HALF), I32)]
                      + ([pltpu.VMEM((PEER_TOPK, SC_LANES), F32)] if body is _peer_u_body else [])
                      + [pltpu.SemaphoreType.DMA((SC_SLOTS,))],
        compiler_params=pltpu.CompilerParams(needs_layout_passes=False),
        name=name,
    )(idx, rows, table)


def _coef_words(pre, gates):
    return _pack_words(*(gates * _gelu(pre),) * 2)


def _coef_body(pre_ref, gate_ref, coef_ref):
    coef_ref[...] = _coef_words(pre_ref[...], gate_ref[...])


def _coef(pre, gates, tm):
    t = pre.shape[0]
    row = pl.BlockSpec((tm, PEER_HK), lambda i: (i, 0))
    return pl.pallas_call(_coef_body, grid=(t // tm,), in_specs=[row, row], out_specs=row,
                          out_shape=jax.ShapeDtypeStruct((t, PEER_HK), I32), name="coef")(pre, gates)


def _final_body(x1_ref, peer_ref, g2_ref, fng_ref, y_ref):
    x2 = x1_ref[...] + _mod_rows(g2_ref) * peer_ref[...]
    y_ref[...] = x2 * lax.rsqrt(jnp.mean(x2 * x2, axis=-1, keepdims=True) + EPS) * fng_ref[...]


def _final(x1, peer_out, mod, rows_per_batch, final_g, tm):
    t = x1.shape[0]
    row = pl.BlockSpec((tm, D_MODEL), lambda i: (i, 0))
    return pl.pallas_call(
        _final_body, grid=(t // tm,),
        in_specs=[row, row, _mod_spec(5, rows_per_batch, tm), _const_spec((1, D_MODEL))],
        out_specs=row, out_shape=jax.ShapeDtypeStruct((t, D_MODEL), F32), name="final",
    )(x1, peer_out, mod, final_g.reshape(1, -1))


def _expert_gather_v(g, coef, expert_v):
    g["peer_out"] = _peer_sc(_peer_v_body, g["idx"], coef, expert_v, D_MODEL, "peer_v")


def _front(x, mod, conv_buf, s0, pool_buf, start, chunk, tm, wts, prev, fin):
    b, l, _ = x.shape
    t = b * l
    x2d = x.reshape(t, D_MODEL)
    if l >= tm:
        modx = mod.reshape(b, 6, 1, D_MODEL).transpose(1, 0, 2, 3)
    else:
        modx = jnp.repeat(mod.reshape(b, 6, D_MODEL), l, axis=0).transpose(1, 0, 2)
    outs = _inproj(x2d, modx, l, wts["norm1_g"], wts["w_cat"], tm)
    lp = -(-l // chunk) * chunk
    proj = {}
    for (name, w), a in zip(_IN_BLOCKS, outs):
        a = a.reshape(b, l, w)
        proj[name] = a if lp == l else jnp.pad(a, ((0, 0), (0, lp - l), (0, 0)))
    mixed, nconv, ns, npool = _mixer(proj, conv_buf, s0, pool_buf, start, l, chunk,
                                     wts["conv_w"], wts["a_log"], wts["dt_bias"], wts["dn_norm_g"],
                                     wts["w_pool"], wts["pool_scale"])
    mixed2d = mixed[:, :l].reshape(t, D_MODEL)
    res = _post(mixed2d, x2d, modx, l, wts["norm2_g"], wts["w_out"], wts["w_query"], wts["keys"], tm,
                prev=None if prev is None else (prev["pre"], prev["gates"]),
                fin=None if fin is None else (fin["x1"], fin["peer_out"], fin["mod"], fin["l"],
                                              wts["final_norm_g"]))
    x1, h2, idx, gates = res[:4]
    extra = list(res[4:])
    coef_prev = extra.pop(0) if prev is not None else None
    y_fin = extra.pop(0).reshape(fin["b"], fin["l"], D_MODEL) if fin is not None else None
    pre = _peer_sc(_peer_u_body, idx, h2, wts["expert_u"], PEER_HK, "peer_u")
    g = dict(x1=x1, idx=idx, gates=gates, pre=pre, mod=modx, b=b, l=l, tm=tm,
             states=(nconv, ns, npool))
    return g, coef_prev, y_fin


def kernel(x_prompt, x_sample, c_prompt, c_sample, state_conv, state_delta, state_pool, w_ada, b_ada, norm1_g, w_in, conv_w, a_log, dt_bias, dn_norm_g, w_pool, pool_scale, w_out, norm2_g, w_query, sub_keys, expert_u, expert_v, final_norm_g):
    bp = x_prompt.shape[0]
    yp, ys = x_prompt, x_sample
    conv_p, delta_p, pool_p, conv_s, delta_s, pool_s = [], [], [], [], [], []
    zero_conv = jnp.zeros((bp, CONV_WIDTH - 1, QKV_WIDTH), F32)
    zero_delta = jnp.zeros((bp, DN_HEADS, DN_HEAD_DIM, DN_HEAD_DIM), F32)
    zero_pool = jnp.zeros((bp, POOL_BUF, POOL_WIDTH), F32)
    c_all = jnp.concatenate([c_prompt, c_sample], axis=0)
    for layer in range(DEPTH):
        wi = w_in[layer]
        o_b = QKV_WIDTH
        o_z = o_b + 2 * DN_HEADS
        w_ba = jnp.pad(wi[:, o_b:o_z], ((0, 0), (0, LANES - 2 * DN_HEADS)))
        w_cat = jnp.concatenate([wi[:, :o_b], wi[:, o_z:], w_ba], axis=1).astype(BF16)
        last = layer == DEPTH - 1
        wts = dict(
            norm1_g=norm1_g[layer], w_cat=w_cat, conv_w=conv_w[layer], a_log=a_log[layer],
            dt_bias=dt_bias[layer], dn_norm_g=dn_norm_g[layer], w_pool=w_pool[layer],
            pool_scale=pool_scale[layer], w_out=w_out[layer].astype(BF16), norm2_g=norm2_g[layer],
            w_query=w_query[layer].astype(BF16),
            keys=sub_keys[layer].reshape(2 * PEER_HEADS, PEER_NKEYS, PEER_KEY_HALF).astype(BF16),
            expert_u=_pack_table(expert_u[layer]), expert_v=_pack_table(expert_v[layer]),
            final_norm_g=final_norm_g if last else jnp.ones_like(final_norm_g))
        mod = _ada(c_all, w_ada[layer], b_ada[layer])
        assert last, "final norm is fused into the expert stage"
        step = bp // PROMPT_PARTS
        seq = x_prompt.shape[1]
        zeros = (zero_conv[:step], zero_delta[:step], zero_pool[:step])
        jobs, cuts = [], []
        for b0 in range(0, bp, step):
            n = EDGE_SPLITS if b0 in (0, bp - step) else 1
            cuts.append(n)
            for s0 in range(0, seq, seq // n):
                jobs.append((yp[b0:b0 + step, s0:s0 + seq // n], mod[b0:b0 + step],
                             zeros if s0 == 0 else None, s0, DN_CHUNK))
        jobs.append((ys, mod[bp:], (state_conv[layer], state_delta[layer], state_pool[layer]),
                     PAST_LEN, SUBLANES))
        groups = []
        for j, (xg, mg, states, start, chunk) in enumerate(jobs):
            pi = j - (1 if j <= RAMP_PARTS else COEF_LAG)
            prev = groups[pi] if pi >= 0 and "peer_out" not in groups[pi] else None
            fin = groups[j - FIN_LAG] if j >= FIN_LAG and "peer_out" in groups[j - FIN_LAG] else None
            if fin is not None and fin["x1"].shape[0] % (xg.shape[0] * xg.shape[1] // ROW_TILE):
                fin = None
            if states is None:
                states = groups[j - 1]["states"]
            g, coef_prev, y_fin = _front(xg, mg, *states, start, chunk, ROW_TILE, wts, prev, fin)
            if prev is not None:
                _expert_gather_v(prev, coef_prev, wts["expert_v"])
            if fin is not None:
                fin["y"] = y_fin
            groups.append(g)
        for g in groups:
            if "peer_out" not in g:
                _expert_gather_v(g, _coef(g["pre"], g["gates"], ROW_TILE), wts["expert_v"])
        for g in groups:
            if "y" not in g:
                g["y"] = _final(g["x1"], g["peer_out"], g["mod"], g["l"], wts["final_norm_g"],
                                g["tm"]).reshape(g["b"], g["l"], D_MODEL)
        rows, at = [], 0
        for n in cuts:
            rows.append(groups[at:at + n])
            at += n
        yp = jnp.concatenate([jnp.concatenate([g["y"] for g in row], axis=1) for row in rows], axis=0)
        cp, sp, pp = (jnp.concatenate(a, axis=0) for a in zip(*(row[-1]["states"] for row in rows)))
        ys = groups[-1]["y"]
        cs, ss, ps = groups[-1]["states"]
        conv_p.append(cp)
        delta_p.append(sp)
        pool_p.append(pp)
        conv_s.append(cs)
        delta_s.append(ss)
        pool_s.append(ps)
    return (yp, ys, jnp.stack(conv_p), jnp.stack(delta_p), jnp.stack(pool_p),
            jnp.stack(conv_s), jnp.stack(delta_s), jnp.stack(pool_s))
```

```python
import functools

import jax
import jax.numpy as jnp
from jax import lax
from jax.experimental import pallas as pl
from jax.experimental.pallas import tpu as pltpu
from jax.experimental.pallas import tpu_sc as plsc

F32 = jnp.float32
BF16 = jnp.bfloat16
I32 = jnp.int32

D_MODEL = 1024
DEPTH = 1
PAST_LEN = 16384
DN_HEADS = 8
DN_HEAD_DIM = 128
DN_WIDTH = DN_HEADS * DN_HEAD_DIM
QKV_WIDTH = 3 * DN_WIDTH
CONV_WIDTH = 4
DN_CHUNK = 64
POOL_WINDOWS = (2, 4, 8, 16)
POOL_GROUP_DIM = 128
POOL_WIDTH = len(POOL_WINDOWS) * POOL_GROUP_DIM
POOL_OUT_GROUP = D_MODEL // len(POOL_WINDOWS)
POOL_BUF = max(POOL_WINDOWS) - 1
PEER_HEADS = 8
PEER_NKEYS = 128
PEER_TOPK = 16
PEER_KEY_HALF = 128
PEER_HK = PEER_HEADS * PEER_TOPK
EPS = 1e-6

LANES = 128
SUBLANES = 8
CONV_PAD = SUBLANES
POOL_PAD = 16
VMEM_LIMIT = 56 * 1024 * 1024

NT_DIMS = (((1,), (1,)), ((), ()))
TN_DIMS = (((0,), (0,)), ((), ()))


def _dot(a, b):
    return jnp.dot(a.astype(BF16), b.astype(BF16), preferred_element_type=F32)


def _dot_nt(a, b):
    return lax.dot_general(a.astype(BF16), b.astype(BF16), NT_DIMS, preferred_element_type=F32)


def _split3(x):
    hi = x.astype(BF16)
    r1 = x - hi.astype(F32)
    mid = r1.astype(BF16)
    lo = (r1 - mid.astype(F32)).astype(BF16)
    return hi, mid, lo


def _silu(x):
    return x * jax.nn.sigmoid(x)


def _gelu(x):
    return 0.5 * x * (1.0 + lax.erf(x * (0.5 ** 0.5)))


def _softplus(x):
    return jnp.maximum(x, 0.0) + jnp.log(1.0 + jnp.exp(-jnp.abs(x)))


def _mod_rows(ref):
    m = ref[...]
    return m.reshape(m.shape[-2], m.shape[-1])


def _mod_spec(k, rows_per_batch, tm):
    if rows_per_batch >= tm:
        tiles = rows_per_batch // tm
        return pl.BlockSpec((1, 1, 1, D_MODEL), lambda i, *_: (k, i // tiles, 0, 0))
    return pl.BlockSpec((1, tm, D_MODEL), lambda i, *_: (k, i, 0))


def _const_spec(shape):
    nd = len(shape)
    return pl.BlockSpec(shape, lambda *_: (0,) * nd)


def _ada_body(c_ref, w_ref, b_ref, o_ref):
    o_ref[...] = _dot(_silu(c_ref[...]), w_ref[...]) + b_ref[...]


def _ada(c, w_ada, b_ada):
    n = c.shape[0]
    return pl.pallas_call(
        _ada_body,
        grid=(6,),
        in_specs=[pl.BlockSpec((n, D_MODEL), lambda j: (0, 0)),
                  pl.BlockSpec((D_MODEL, D_MODEL), lambda j: (0, j)),
                  pl.BlockSpec((1, D_MODEL), lambda j: (0, j))],
        out_specs=pl.BlockSpec((n, D_MODEL), lambda j: (0, j)),
        out_shape=jax.ShapeDtypeStruct((n, 6 * D_MODEL), F32),
        name="ada",
    )(c, w_ada, b_ada.reshape(1, -1))


_IN_BLOCKS = (("qkv", QKV_WIDTH), ("z", DN_WIDTH), ("pool", POOL_WIDTH),
              ("ga", D_MODEL), ("gb", D_MODEL), ("ba", LANES))
_IN_TOTAL = sum(w for _, w in _IN_BLOCKS)
_IN_F32 = ("ba",)
_IN_COL_CHUNK = 512


def _inproj_body(x_ref, sc_ref, sh_ref, g_ref, w_ref, *out_refs):
    x = x_ref[...]
    y = x * lax.rsqrt(jnp.mean(x * x, axis=-1, keepdims=True) + EPS) * g_ref[...]
    h = (y * (1.0 + _mod_rows(sc_ref)) + _mod_rows(sh_ref)).astype(BF16)
    off = 0
    for (_, width), o_ref in zip(_IN_BLOCKS, out_refs):
        for c0 in range(0, width, _IN_COL_CHUNK):
            cw = min(_IN_COL_CHUNK, width - c0)
            o_ref[:, c0:c0 + cw] = jnp.dot(h, w_ref[:, off + c0:off + c0 + cw],
                                           preferred_element_type=F32).astype(o_ref.dtype)
        off += width


def _inproj(x2d, row0, t, mod, rows_per_batch, norm_g, w_cat, tm):
    row = lambda w: pl.BlockSpec((tm, w), lambda i: (i, 0))
    return pl.pallas_call(
        _inproj_body,
        grid=(t // tm,),
        in_specs=[pl.BlockSpec((tm, D_MODEL), lambda i: (i + row0 // tm, 0)),
                  _mod_spec(1, rows_per_batch, tm), _mod_spec(0, rows_per_batch, tm),
                  _const_spec((1, D_MODEL)),
                  pl.BlockSpec((D_MODEL, _IN_TOTAL), lambda i: (0, 0), pipeline_mode=pl.Buffered(1))],
        out_specs=[row(w) for _, w in _IN_BLOCKS],
        out_shape=[jax.ShapeDtypeStruct((t, w), F32 if name in _IN_F32 else BF16) for name, w in _IN_BLOCKS],
        compiler_params=pltpu.CompilerParams(vmem_limit_bytes=VMEM_LIMIT),
        name="inproj",
    )(x2d, mod, mod, norm_g.reshape(1, -1), w_cat)


def _mixer_body(C, Lv, start,
                qkv_ref, ba_ref, z_ref, pin_ref, ga_ref, gb_ref, cbuf_ref, s0_ref, pbuf_ref,
                convw_ref, alog_ref, dtb_ref, dng_ref, wpool_ref, pscale_ref,
                mixed_ref, nconv_ref, ns_ref, npool_ref,
                xp_scr, act_scr, s_scr, pp_scr, odn_scr):
    n = pl.program_id(1)
    last = pl.num_programs(1) - 1

    @pl.when(n == 0)
    def _load_state():
        xp_scr[0:CONV_PAD, :] = cbuf_ref[0]
        pp_scr[0:POOL_PAD, :] = pbuf_ref[0]
        s_scr[...] = s0_ref[0]

    xp_scr[CONV_PAD:CONV_PAD + C, :] = qkv_ref[0].astype(F32)
    for c0 in range(0, QKV_WIDTH, _IN_COL_CHUNK):
        cs = slice(c0, c0 + _IN_COL_CHUNK)
        y = xp_scr[CONV_PAD:CONV_PAD + C, cs] * convw_ref[CONV_WIDTH - 1:CONV_WIDTH, cs]
        for k in range(CONV_WIDTH - 1):
            r0 = CONV_PAD - (CONV_WIDTH - 1) + k
            y = y + xp_scr[r0:r0 + C, cs] * convw_ref[k:k + 1, cs]
        act_scr[:, cs] = _silu(y)

    ba = ba_ref[0]
    lane = lax.broadcasted_iota(I32, (C, LANES), 1)
    beta_all = jax.nn.sigmoid(ba)
    g_all = -jnp.exp(alog_ref[...]) * _softplus(ba + dtb_ref[...])
    if Lv < C:
        valid = lax.broadcasted_iota(I32, (C, LANES), 0) < Lv
        beta_all = jnp.where(valid, beta_all, 0.0)
        g_all = jnp.where(valid, g_all, 0.0)
    ii = lax.broadcasted_iota(I32, (C, C), 0)
    jj = lax.broadcasted_iota(I32, (C, C), 1)
    causal = ii >= jj
    strict = ii > jj
    tril = jnp.where(causal, 1.0, 0.0).astype(BF16)
    eye = jnp.where(ii == jj, 1.0, 0.0)
    gc_all = sum(jnp.dot(tril, part, preferred_element_type=F32) for part in _split3(g_all))
    if C < LANES:
        gc_sq = jnp.concatenate([gc_all, jnp.zeros((LANES - C, LANES), F32)], axis=0)
    else:
        gc_sq = gc_all
    gc_t = gc_sq.T

    H = range(DN_HEADS)
    hsl = [slice(h * DN_HEAD_DIM, (h + 1) * DN_HEAD_DIM) for h in H]
    beta = [jnp.sum(jnp.where(lane == h, beta_all, 0.0), axis=1, keepdims=True) for h in H]
    gcol = [jnp.sum(jnp.where(lane == DN_HEADS + h, gc_all, 0.0), axis=1, keepdims=True) for h in H]
    grow = [gc_t[DN_HEADS + h:DN_HEADS + h + 1, 0:C] for h in H]
    glast = [g[C - 1:C, :] for g in gcol]
    q = [act_scr[:, hsl[h]] for h in H]
    k = [act_scr[:, DN_WIDTH + h * DN_HEAD_DIM:DN_WIDTH + (h + 1) * DN_HEAD_DIM] for h in H]
    v = [act_scr[:, 2 * DN_WIDTH + h * DN_HEAD_DIM:2 * DN_WIDTH + (h + 1) * DN_HEAD_DIM] for h in H]
    q = [x * lax.rsqrt(jnp.sum(x * x, axis=-1, keepdims=True) + EPS) * (DN_HEAD_DIM ** -0.5) for x in q]
    k = [x * lax.rsqrt(jnp.sum(x * x, axis=-1, keepdims=True) + EPS) for x in k]
    kb = [k[h] * beta[h] for h in H]
    vb = [v[h] * beta[h] for h in H]
    decay = [jnp.where(causal, jnp.exp(jnp.where(causal, gcol[h] - grow[h], 0.0)), 0.0) for h in H]
    lower = [jnp.where(strict, _dot_nt(kb[h], k[h]) * decay[h], 0.0) for h in H]
    ainv = [eye - x for x in lower]
    pw = lower
    p = 1
    while 2 * p < C:
        pw = [_dot(x, x) for x in pw]
        ainv = [ainv[h] + _dot(ainv[h], pw[h]) for h in H]
        p *= 2
    sol = [_dot(ainv[h], jnp.concatenate([vb[h], kb[h] * jnp.exp(gcol[h])], axis=1)) for h in H]
    qk = [_dot_nt(q[h], k[h]) * decay[h] for h in H]
    k_tail = [k[h] * jnp.exp(glast[h] - gcol[h]) for h in H]
    S = [s_scr[h] for h in H]
    v_new = [sol[h][:, :DN_HEAD_DIM] - _dot(sol[h][:, DN_HEAD_DIM:], S[h]) for h in H]
    o = [_dot(q[h] * jnp.exp(gcol[h]), S[h]) + _dot(qk[h], v_new[h]) for h in H]
    for h in H:
        s_scr[h] = S[h] * jnp.exp(glast[h]) + lax.dot_general(
            k_tail[h].astype(BF16), v_new[h].astype(BF16), TN_DIMS, preferred_element_type=F32)
    for h in H:
        zf = z_ref[0, :, hsl[h]].astype(F32)
        odn_scr[:, hsl[h]] = (o[h] * lax.rsqrt(jnp.mean(o[h] * o[h], axis=-1, keepdims=True) + EPS)
                              * dng_ref[...] * _silu(zf))

    pp_scr[POOL_PAD:POOL_PAD + C, :] = pin_ref[0].astype(F32)
    pos = start + n * C + lax.broadcasted_iota(I32, (C, 1), 0)
    for gi, win in enumerate(POOL_WINDOWS):
        gs = slice(gi * POOL_GROUP_DIM, (gi + 1) * POOL_GROUP_DIM)
        xg = pp_scr[POOL_PAD:POOL_PAD + C, gs]
        ssum = xg
        for sft in range(1, win):
            ssum = ssum + pp_scr[POOL_PAD - sft:POOL_PAD - sft + C, gs]
        cnt = jnp.minimum(pos + 1, win).astype(F32)
        pooled = ssum / cnt - xg
        os_ = slice(gi * POOL_OUT_GROUP, (gi + 1) * POOL_OUT_GROUP)
        yp = _dot(pooled, wpool_ref[gi]) * pscale_ref[:, os_]
        mixed_ref[0, :, os_] = (jax.nn.sigmoid(ga_ref[0, :, os_].astype(F32)) * odn_scr[:, os_]
                                + jax.nn.sigmoid(gb_ref[0, :, os_].astype(F32)) * yp).astype(BF16)

    @pl.when(n == last)
    def _store_state():
        nconv_ref[0] = xp_scr[Lv + CONV_PAD - (CONV_WIDTH - 1):Lv + CONV_PAD, :]
        npool_ref[0] = pp_scr[Lv + POOL_PAD - POOL_BUF:Lv + POOL_PAD, :]
        ns_ref[0] = s_scr[...]

    xp_scr[0:CONV_PAD, :] = xp_scr[C:C + CONV_PAD, :]
    pp_scr[0:POOL_PAD, :] = pp_scr[C:C + POOL_PAD, :]


def _mixer(proj, conv_buf, s0, pool_buf, start, seq_len, C,
           conv_w, a_log, dt_bias, dn_norm_g, w_pool, pool_scale):
    b, lp, _ = proj["qkv"].shape
    nchunks = lp // C
    lv = seq_len - (nchunks - 1) * C
    cbuf = jnp.pad(conv_buf, ((0, 0), (CONV_PAD - (CONV_WIDTH - 1), 0), (0, 0)))
    pbuf = jnp.pad(pool_buf, ((0, 0), (POOL_PAD - POOL_BUF, 0), (0, 0)))
    lane_pad = lambda a: jnp.pad(a.reshape(1, -1), ((0, 0), (DN_HEADS, LANES - 2 * DN_HEADS)))
    chunk = lambda w: pl.BlockSpec((1, C, w), lambda i, j: (i, j, 0))
    state = lambda *s: pl.BlockSpec((1,) + s, lambda i, j: (i,) + (0,) * len(s))
    return pl.pallas_call(
        functools.partial(_mixer_body, C, lv, start),
        grid=(b, nchunks),
        in_specs=[chunk(QKV_WIDTH), chunk(LANES), chunk(DN_WIDTH), chunk(POOL_WIDTH),
                  chunk(D_MODEL), chunk(D_MODEL),
                  state(CONV_PAD, QKV_WIDTH), state(DN_HEADS, DN_HEAD_DIM, DN_HEAD_DIM),
                  state(POOL_PAD, POOL_WIDTH),
                  _const_spec((CONV_WIDTH, QKV_WIDTH)), _const_spec((1, LANES)), _const_spec((1, LANES)),
                  _const_spec((1, DN_HEAD_DIM)),
                  _const_spec((len(POOL_WINDOWS), POOL_GROUP_DIM, POOL_OUT_GROUP)),
                  _const_spec((1, D_MODEL))],
        out_specs=[chunk(D_MODEL), state(CONV_WIDTH - 1, QKV_WIDTH),
                   state(DN_HEADS, DN_HEAD_DIM, DN_HEAD_DIM), state(POOL_BUF, POOL_WIDTH)],
        out_shape=[jax.ShapeDtypeStruct((b, lp, D_MODEL), BF16),
                   jax.ShapeDtypeStruct((b, CONV_WIDTH - 1, QKV_WIDTH), F32),
                   jax.ShapeDtypeStruct((b, DN_HEADS, DN_HEAD_DIM, DN_HEAD_DIM), F32),
                   jax.ShapeDtypeStruct((b, POOL_BUF, POOL_WIDTH), F32)],
        scratch_shapes=[pltpu.VMEM((CONV_PAD + C + CONV_PAD, QKV_WIDTH), F32),
                        pltpu.VMEM((C, QKV_WIDTH), F32),
                        pltpu.VMEM((DN_HEADS, DN_HEAD_DIM, DN_HEAD_DIM), F32),
                        pltpu.VMEM((POOL_PAD + C + POOL_PAD, POOL_WIDTH), F32),
                        pltpu.VMEM((C, DN_WIDTH), F32)],
        compiler_params=pltpu.CompilerParams(dimension_semantics=("arbitrary", "arbitrary"),
                                             vmem_limit_bytes=VMEM_LIMIT),
        name="mixer",
    )(proj["qkv"], proj["ba"], proj["z"], proj["pool"], proj["ga"], proj["gb"], cbuf, s0, pbuf,
      conv_w, lane_pad(a_log), lane_pad(dt_bias), dn_norm_g.reshape(1, -1), w_pool,
      pool_scale.reshape(1, -1))


def _top16(s, ids, payload=None):
    big = float(2 ** 24)
    vals, sel, pays = [], [], []
    for _ in range(PEER_TOPK):
        m = jnp.max(s, axis=0, keepdims=True)
        am = jnp.min(jnp.where(s == m, ids, big), axis=0, keepdims=True)
        hit = ids == am
        if payload is not None:
            pays.append(jnp.max(jnp.where(hit, payload, -1.0), axis=0, keepdims=True))
        s = jnp.where(hit, -jnp.inf, s)
        vals.append(m)
        sel.append(am)
    out = (jnp.concatenate(vals, axis=0), jnp.concatenate(sel, axis=0))
    if payload is not None:
        out += (jnp.concatenate(pays, axis=0),)
    return out


_CAND_EDGE = 4


def _post_body(has_prev, has_fin, mixed_ref, x_ref, g1_ref, sc2_ref, sh2_ref, n2g_ref, wout_ref,
               wq_ref, keys_ref, *refs):
    refs = list(refs)
    prev_in = [refs.pop(0) for _ in range(2 if has_prev else 0)]
    fin_in = [refs.pop(0) for _ in range(4 if has_fin else 0)]
    x1_ref, h2_ref, idx_ref, gate_ref = refs[:4]
    extra_out = refs[4:]
    if has_prev:
        pre_ref, pgate_ref = prev_in
        extra_out.pop(0)[...] = _coef_words(pre_ref[...], pgate_ref[...])
    if has_fin:
        _final_body(*fin_in, extra_out.pop(0))
    tm = x_ref.shape[0]
    x1 = x_ref[...] + _mod_rows(g1_ref) * _dot(mixed_ref[...], wout_ref[...])
    x1_ref[...] = x1
    y = x1 * lax.rsqrt(jnp.mean(x1 * x1, axis=-1, keepdims=True) + EPS) * n2g_ref[...]
    h2 = y * (1.0 + _mod_rows(sc2_ref)) + _mod_rows(sh2_ref)
    h2_ref[...] = _pack_words(h2[:, :PACK_HALF], h2[:, PACK_HALF:])
    q = _dot(h2, wq_ref[...])

    K = PEER_TOPK
    key_id = lax.broadcasted_iota(I32, (PEER_NKEYS, 1), 0).astype(F32)
    r16 = lax.broadcasted_iota(I32, (K, 1), 0)
    cand_id = jnp.concatenate([(a * K + r16) for a in range(_CAND_EDGE)]
                              + [(r16 * K + b) for b in range(_CAND_EDGE)], axis=0).astype(F32)
    dup = r16 < _CAND_EDGE
    idx_rows, gate_rows = [], []
    for h in range(PEER_HEADS):
        half = []
        for p in range(2):
            c0 = (h * 2 + p) * PEER_KEY_HALF
            st = _dot_nt(keys_ref[h * 2 + p], q[:, c0:c0 + PEER_KEY_HALF])
            half.append(_top16(st, key_id))
        (s1, i1), (s2, i2) = half
        cand = jnp.concatenate(
            [s1[a:a + 1] + s2 for a in range(_CAND_EDGE)]
            + [jnp.where(dup, -jnp.inf, s1 + s2[b:b + 1]) for b in range(_CAND_EDGE)], axis=0)
        cidx = jnp.concatenate(
            [i1[a:a + 1] * PEER_NKEYS + i2 for a in range(_CAND_EDGE)]
            + [i1 * PEER_NKEYS + i2[b:b + 1] for b in range(_CAND_EDGE)], axis=0)
        best, _, eidx = _top16(cand, cand_id, cidx)
        e = jnp.exp(best - best[0:1])
        gate_rows.append(e / jnp.sum(e, axis=0, keepdims=True))
        idx_rows.append(eidx)
    idx_ref[...] = jnp.concatenate(idx_rows, axis=0).T.astype(I32)
    gate_ref[...] = jnp.concatenate(gate_rows, axis=0).T


def _post(mixed2d, x2d, row0, mod, rows_per_batch, norm2_g, w_out, w_query, keys, tm, prev=None, fin=None):
    t = mixed2d.shape[0]
    steps = t // tm
    row = lambda w: pl.BlockSpec((tm, w), lambda i: (i, 0))
    in_specs = [row(D_MODEL), pl.BlockSpec((tm, D_MODEL), lambda i: (i + row0 // tm, 0)),
                _mod_spec(2, rows_per_batch, tm), _mod_spec(4, rows_per_batch, tm),
                _mod_spec(3, rows_per_batch, tm), _const_spec((1, D_MODEL)),
                _const_spec((D_MODEL, D_MODEL)), _const_spec((D_MODEL, 2 * PEER_HEADS * PEER_KEY_HALF)),
                _const_spec((2 * PEER_HEADS, PEER_NKEYS, PEER_KEY_HALF))]
    out_specs = [row(D_MODEL), row(PACK_HALF), row(PEER_HK), row(PEER_HK)]
    out_shape = [jax.ShapeDtypeStruct((t, D_MODEL), F32), jax.ShapeDtypeStruct((t, PACK_HALF), I32),
                 jax.ShapeDtypeStruct((t, PEER_HK), I32), jax.ShapeDtypeStruct((t, PEER_HK), F32)]
    args = [mixed2d, x2d, mod, mod, mod, norm2_g.reshape(1, -1), w_out, w_query, keys]
    if prev is not None:
        tp = prev[0].shape[0]
        prow = pl.BlockSpec((tp // steps, PEER_HK), lambda i: (i, 0))
        in_specs += [prow, prow]
        out_specs += [prow]
        out_shape += [jax.ShapeDtypeStruct((tp, PEER_HK), I32)]
        args += list(prev)
    if fin is not None:
        x1_f, peer_f, mod_f, rows_f, final_g = fin
        tf = x1_f.shape[0]
        frow = pl.BlockSpec((tf // steps, D_MODEL), lambda i: (i, 0))
        in_specs += [frow, frow, _mod_spec(5, rows_f, tf // steps), _const_spec((1, D_MODEL))]
        out_specs += [frow]
        out_shape += [jax.ShapeDtypeStruct((tf, D_MODEL), F32)]
        args += [x1_f, peer_f, mod_f, final_g.reshape(1, -1)]
    return pl.pallas_call(
        functools.partial(_post_body, prev is not None, fin is not None),
        grid=(steps,),
        in_specs=in_specs, out_specs=out_specs, out_shape=out_shape,
        compiler_params=pltpu.CompilerParams(vmem_limit_bytes=VMEM_LIMIT),
        name="post",
    )(*args)


SC_CORES = 2
SC_SUBCORES = 16
SC_LANES = 16
SC_WORKERS = SC_CORES * SC_SUBCORES
SC_TOKENS = 32
SC_SLOTS = 4
SC_JOB_HEADS = 2
SC_BF16_GROUP = 4
PACK_HALF = D_MODEL // 2
SC_CHUNKS = PACK_HALF // SC_LANES
PROMPT_PARTS = 8
EDGE_SPLITS = 2
RAMP_PARTS = 2
COEF_LAG = 2
FIN_LAG = 3
ROW_TILE = 256


def _bf16_bits(v):
    return lax.bitcast_convert_type(v.astype(BF16).astype(F32), jnp.uint32)


def _pack_words(lo, hi):
    return lax.bitcast_convert_type((_bf16_bits(lo) >> 16) | _bf16_bits(hi), I32)


def _pack_body(x_ref, o_ref):
    o_ref[...] = _pack_words(x_ref[:, :PACK_HALF], x_ref[:, PACK_HALF:])


def _pack_table(tbl, rows=2 * ROW_TILE):
    e = tbl.shape[0]
    return pl.pallas_call(
        _pack_body, grid=(e // rows,),
        in_specs=[pl.BlockSpec((rows, D_MODEL), lambda i: (i, 0))],
        out_specs=pl.BlockSpec((rows, PACK_HALF), lambda i: (i, 0)),
        out_shape=jax.ShapeDtypeStruct((e, PACK_HALF), I32), name="pack_table")(tbl)


def _tree_sum(terms):
    terms = list(terms)
    while len(terms) > 1:
        terms = [a + b for a, b in zip(terms[0::2], terms[1::2])] + terms[len(terms) & ~1:]
    return terms[0]


def _unpack_pair(w):
    lo = plsc.bitcast(lax.shift_left(w, jnp.full(w.shape, 16, I32)), F32)
    hi = plsc.bitcast(w & jnp.full(w.shape, -65536, I32), F32)
    return lo, hi


def _sc_mesh():
    return plsc.VectorSubcoreMesh(core_axis_name="c", subcore_axis_name="s")


def _sc_worker():
    return lax.axis_index("s") * SC_CORES + lax.axis_index("c")


def _sc_jobs(table_hbm, idx_v, buf, sem, compute):
    per_tok = PEER_HEADS // SC_JOB_HEADS
    njobs = idx_v.shape[0] * per_tok
    nrows = SC_JOB_HEADS * PEER_TOPK

    def copy(j, slot):
        rows = idx_v.at[j // per_tok, pl.ds((j % per_tok) * nrows, nrows)]
        return pltpu.make_async_copy(table_hbm.at[rows], buf.at[slot], sem.at[slot])

    for s in range(SC_SLOTS):
        copy(s, s).start()

    def job(j, c):
        s = j % SC_SLOTS
        copy(j, s).wait()

        def head(i, cc):
            compute(j // per_tok, (j % per_tok) * SC_JOB_HEADS + i, s, i * PEER_TOPK)
            return cc
        lax.fori_loop(0, SC_JOB_HEADS, head, 0)

        @pl.when(j + SC_SLOTS < njobs)
        def _next():
            copy(j + SC_SLOTS, s).start()
        return c

    lax.fori_loop(0, njobs, job, 0)


def _peer_u_body(n_tok, idx_hbm, h2_hbm, u_hbm, pre_hbm, idx_v, h2_v, pre_v, ubuf, acc_v, sem):
    base = _sc_worker() * n_tok
    lane = lax.iota(I32, SC_LANES)

    def compute(tt, h, slot, r0):
        def chunk(cg, accs):
            cs = [pl.ds((cg * SC_BF16_GROUP + i) * SC_LANES, SC_LANES) for i in range(SC_BF16_GROUP)]
            xs = [plsc.bitcast(h2_v[tt, c], BF16) for c in cs]
            out = []
            for k, a in enumerate(accs):
                part = _tree_sum([plsc.bitcast(ubuf[slot, r0 + k, c], BF16) * x for c, x in zip(cs, xs)])
                lo, hi = _unpack_pair(plsc.bitcast(part, I32))
                out.append(a + (lo + hi))
            return tuple(out)
        zero = jnp.zeros((SC_LANES,), F32)
        accs = lax.fori_loop(0, SC_CHUNKS // SC_BF16_GROUP, chunk, (zero,) * PEER_TOPK)
        for k, a in enumerate(accs):
            acc_v[k, :] = a
        tot = zero
        for j in range(SC_LANES):
            tot = tot + plsc.load_gather(acc_v, [lane, (lane + j) & (SC_LANES - 1)])
        pre_v[tt, pl.ds(h * PEER_TOPK, PEER_TOPK)] = tot

    tb = idx_v.shape[0]

    def block(bi, c):
        t0 = base + bi * tb
        pltpu.sync_copy(idx_hbm.at[pl.ds(t0, tb)], idx_v)
        pltpu.sync_copy(h2_hbm.at[pl.ds(t0, tb)], h2_v)
        _sc_jobs(u_hbm, idx_v, ubuf, sem, compute)
        pltpu.sync_copy(pre_v, pre_hbm.at[pl.ds(t0, tb)])
        return c

    lax.fori_loop(0, n_tok // tb, block, 0)


def _peer_v_body(n_tok, idx_hbm, coef_hbm, v_hbm, out_hbm, idx_v, coef_v, out_v, vbuf, sem):
    base = _sc_worker() * n_tok
    zero = jnp.zeros((SC_LANES,), F32)

    def compute(tt, h, slot, r0):
        cvec = coef_v[tt, pl.ds(h * PEER_TOPK, PEER_TOPK)]
        cb = [plsc.bitcast(jnp.take_along_axis(cvec, jnp.full((SC_LANES,), k, I32), axis=0), BF16)
              for k in range(PEER_TOPK)]

        @plsc.parallel_loop(0, SC_CHUNKS, unroll=2)
        def _chunk(c):
            cs = pl.ds(c * SC_LANES, SC_LANES)
            prods = [plsc.bitcast(vbuf[slot, r0 + k, cs], BF16) * cb[k] for k in range(PEER_TOPK)]
            pairs = [_unpack_pair(plsc.bitcast(_tree_sum(prods[g:g + SC_BF16_GROUP]), I32))
                     for g in range(0, PEER_TOPK, SC_BF16_GROUP)]
            for half, off in ((0, 0), (1, PACK_HALF)):
                plsc.addupdate(out_v.at[tt, pl.ds(off + c * SC_LANES, SC_LANES)],
                               _tree_sum([p[half] for p in pairs]))

    tb = idx_v.shape[0]

    def block(bi, c):
        t0 = base + bi * tb
        pltpu.sync_copy(idx_hbm.at[pl.ds(t0, tb)], idx_v)
        pltpu.sync_copy(coef_hbm.at[pl.ds(t0, tb)], coef_v)

        def clear(i, cc):
            per_row = D_MODEL // SC_LANES
            out_v[i // per_row, pl.ds((i % per_row) * SC_LANES, SC_LANES)] = zero
            return cc
        lax.fori_loop(0, tb * (D_MODEL // SC_LANES), clear, 0)
        _sc_jobs(v_hbm, idx_v, vbuf, sem, compute)
        pltpu.sync_copy(out_v, out_hbm.at[pl.ds(t0, tb)])
        return c

    lax.fori_loop(0, n_tok // tb, block, 0)


def _peer_sc(body, idx, rows, table, out_width, name):
    t = idx.shape[0]
    assert t % SC_WORKERS == 0
    n_tok = t // SC_WORKERS
    tb = min(SC_TOKENS * (2 if body is _peer_u_body else 1), n_tok)
    assert n_tok % tb == 0 and tb * PEER_HEADS // SC_JOB_HEADS >= SC_SLOTS
    return pl.kernel(
        functools.partial(body, n_tok),
        out_type=jax.ShapeDtypeStruct((t, out_width), F32),
        mesh=_sc_mesh(),
        scratch_types=[pltpu.VMEM((tb, PEER_HK), I32),
                       pltpu.VMEM((tb, rows.shape[1]), rows.dtype),
                       pltpu.VMEM((tb, out_width), F32),
                       pltpu.VMEM((SC_SLOTS, SC_JOB_HEADS * PEER_TOPK, PACK_HALF), I32)]
                      + ([pltpu.VMEM((PEER_TOPK, SC_LANES), F32)] if body is _peer_u_body else [])
                      + [pltpu.SemaphoreType.DMA((SC_SLOTS,))],
        compiler_params=pltpu.CompilerParams(needs_layout_passes=False),
        name=name,
    )(idx, rows, table)


def _coef_words(pre, gates):
    return _pack_words(*(gates * _gelu(pre),) * 2)


def _coef_body(pre_ref, gate_ref, coef_ref):
    coef_ref[...] = _coef_words(pre_ref[...], gate_ref[...])


def _coef(pre, gates, tm):
    t = pre.shape[0]
    row = pl.BlockSpec((tm, PEER_HK), lambda i: (i, 0))
    return pl.pallas_call(_coef_body, grid=(t // tm,), in_specs=[row, row], out_specs=row,
                          out_shape=jax.ShapeDtypeStruct((t, PEER_HK), I32), name="coef")(pre, gates)


def _final_body(x1_ref, peer_ref, g2_ref, fng_ref, y_ref):
    x2 = x1_ref[...] + _mod_rows(g2_ref) * peer_ref[...]
    y_ref[...] = x2 * lax.rsqrt(jnp.mean(x2 * x2, axis=-1, keepdims=True) + EPS) * fng_ref[...]


def _final(x1, peer_out, mod, rows_per_batch, final_g, tm):
    t = x1.shape[0]
    row = pl.BlockSpec((tm, D_MODEL), lambda i: (i, 0))
    return pl.pallas_call(
        _final_body, grid=(t // tm,),
        in_specs=[row, row, _mod_spec(5, rows_per_batch, tm), _const_spec((1, D_MODEL))],
        out_specs=row, out_shape=jax.ShapeDtypeStruct((t, D_MODEL), F32), name="final",
    )(x1, peer_out, mod, final_g.reshape(1, -1))


def _expert_gather_v(g, coef, expert_v):
    g["peer_out"] = _peer_sc(_peer_v_body, g["idx"], coef, expert_v, D_MODEL, "peer_v")


def _front(x, mod, conv_buf, s0, pool_buf, start, chunk, tm, wts, prev, fin):
    x2d, row0, b, l = x
    t = b * l
    assert row0 % tm == 0
    if l >= tm:
        modx = mod.reshape(b, 6, 1, D_MODEL).transpose(1, 0, 2, 3)
    else:
        modx = jnp.repeat(mod.reshape(b, 6, D_MODEL), l, axis=0).transpose(1, 0, 2)
    outs = _inproj(x2d, row0, t, modx, l, wts["norm1_g"], wts["w_cat"], tm)
    lp = -(-l // chunk) * chunk
    proj = {}
    for (name, w), a in zip(_IN_BLOCKS, outs):
        a = a.reshape(b, l, w)
        proj[name] = a if lp == l else jnp.pad(a, ((0, 0), (0, lp - l), (0, 0)))
    mixed, nconv, ns, npool = _mixer(proj, conv_buf, s0, pool_buf, start, l, chunk,
                                     wts["conv_w"], wts["a_log"], wts["dt_bias"], wts["dn_norm_g"],
                                     wts["w_pool"], wts["pool_scale"])
    mixed2d = mixed[:, :l].reshape(t, D_MODEL)
    res = _post(mixed2d, x2d, row0, modx, l, wts["norm2_g"], wts["w_out"], wts["w_query"], wts["keys"], tm,
                prev=None if prev is None else (prev["pre"], prev["gates"]),
                fin=None if fin is None else (fin["x1"], fin["peer_out"], fin["mod"], fin["l"],
                                              wts["final_norm_g"]))
    x1, h2, idx, gates = res[:4]
    extra = list(res[4:])
    coef_prev = extra.pop(0) if prev is not None else None
    y_fin = extra.pop(0).reshape(fin["b"], fin["l"], D_MODEL) if fin is not None else None
    pre = _peer_sc(_peer_u_body, idx, h2, wts["expert_u"], PEER_HK, "peer_u")
    g = dict(x1=x1, idx=idx, gates=gates, pre=pre, mod=modx, b=b, l=l, tm=tm,
             states=(nconv, ns, npool))
    return g, coef_prev, y_fin


def kernel(x_prompt, x_sample, c_prompt, c_sample, state_conv, state_delta, state_pool, w_ada, b_ada, norm1_g, w_in, conv_w, a_log, dt_bias, dn_norm_g, w_pool, pool_scale, w_out, norm2_g, w_query, sub_keys, expert_u, expert_v, final_norm_g):
    bp = x_prompt.shape[0]
    yp, ys = x_prompt, x_sample
    conv_p, delta_p, pool_p, conv_s, delta_s, pool_s = [], [], [], [], [], []
    zero_conv = jnp.zeros((bp, CONV_WIDTH - 1, QKV_WIDTH), F32)
    zero_delta = jnp.zeros((bp, DN_HEADS, DN_HEAD_DIM, DN_HEAD_DIM), F32)
    zero_pool = jnp.zeros((bp, POOL_BUF, POOL_WIDTH), F32)
    c_all = jnp.concatenate([c_prompt, c_sample], axis=0)
    for layer in range(DEPTH):
        wi = w_in[layer]
        o_b = QKV_WIDTH
        o_z = o_b + 2 * DN_HEADS
        w_ba = jnp.pad(wi[:, o_b:o_z], ((0, 0), (0, LANES - 2 * DN_HEADS)))
        w_cat = jnp.concatenate([wi[:, :o_b], wi[:, o_z:], w_ba], axis=1).astype(BF16)
        last = layer == DEPTH - 1
        wts = dict(
            norm1_g=norm1_g[layer], w_cat=w_cat, conv_w=conv_w[layer], a_log=a_log[layer],
            dt_bias=dt_bias[layer], dn_norm_g=dn_norm_g[layer], w_pool=w_pool[layer],
            pool_scale=pool_scale[layer], w_out=w_out[layer].astype(BF16), norm2_g=norm2_g[layer],
            w_query=w_query[layer].astype(BF16),
            keys=sub_keys[layer].reshape(2 * PEER_HEADS, PEER_NKEYS, PEER_KEY_HALF).astype(BF16),
            expert_u=_pack_table(expert_u[layer]), expert_v=_pack_table(expert_v[layer]),
            final_norm_g=final_norm_g if last else jnp.ones_like(final_norm_g))
        mod = _ada(c_all, w_ada[layer], b_ada[layer])
        assert last, "final norm is fused into the expert stage"
        step = bp // PROMPT_PARTS
        seq = x_prompt.shape[1]
        assert step == 1
        xp2d = yp.reshape(bp * seq, D_MODEL)
        zeros = (zero_conv[:step], zero_delta[:step], zero_pool[:step])
        jobs, cuts = [], []
        for b0 in range(0, bp, step):
            n = EDGE_SPLITS if b0 in (0, bp - step) else 1
            cuts.append(n)
            for s0 in range(0, seq, seq // n):
                jobs.append(((xp2d, b0 * seq + s0, step, seq // n), mod[b0:b0 + step],
                             zeros if s0 == 0 else None, s0, DN_CHUNK))
        jobs.append(((ys.reshape(-1, D_MODEL), 0) + ys.shape[:2], mod[bp:], (state_conv[layer], state_delta[layer], state_pool[layer]),
                     PAST_LEN, SUBLANES))
        groups = []
        for j, (xg, mg, states, start, chunk) in enumerate(jobs):
            pi = j - (1 if j <= RAMP_PARTS else COEF_LAG)
            prev = groups[pi] if pi >= 0 and "peer_out" not in groups[pi] else None
            fin = groups[j - FIN_LAG] if j >= FIN_LAG and "peer_out" in groups[j - FIN_LAG] else None
            if fin is not None and fin["x1"].shape[0] % (xg[2] * xg[3] // ROW_TILE):
                fin = None
            if states is None:
                states = groups[j - 1]["states"]
            g, coef_prev, y_fin = _front(xg, mg, *states, start, chunk, ROW_TILE, wts, prev, fin)
            if prev is not None:
                _expert_gather_v(prev, coef_prev, wts["expert_v"])
            if fin is not None:
                fin["y"] = y_fin
            groups.append(g)
        for g in groups:
            if "peer_out" not in g:
                _expert_gather_v(g, _coef(g["pre"], g["gates"], ROW_TILE), wts["expert_v"])
        for g in groups:
            if "y" not in g:
                g["y"] = _final(g["x1"], g["peer_out"], g["mod"], g["l"], wts["final_norm_g"],
                                g["tm"]).reshape(g["b"], g["l"], D_MODEL)
        rows, at = [], 0
        for n in cuts:
            rows.append(groups[at:at + n])
            at += n
        yp = jnp.concatenate([jnp.concatenate([g["y"] for g in row], axis=1) for row in rows], axis=0)
        cp, sp, pp = (jnp.concatenate(a, axis=0) for a in zip(*(row[-1]["states"] for row in rows)))
        ys = groups[-1]["y"]
        cs, ss, ps = groups[-1]["states"]
        conv_p.append(cp)
        delta_p.append(sp)
        pool_p.append(pp)
        conv_s.append(cs)
        delta_s.append(ss)
        pool_s.append(ps)
    return (yp, ys, jnp.stack(conv_p), jnp.stack(delta_p), jnp.stack(pool_p),
            jnp.stack(conv_s), jnp.stack(delta_s), jnp.stack(pool_s))
```

```python
import functools

import jax
import jax.numpy as jnp
from jax import lax
from jax.experimental import pallas as pl
from jax.experimental.pallas import tpu as pltpu
from jax.experimental.pallas import tpu_sc as plsc

F32 = jnp.float32
BF16 = jnp.bfloat16
I32 = jnp.int32

D_MODEL = 1024
DEPTH = 1
PAST_LEN = 16384
DN_HEADS = 8
DN_HEAD_DIM = 128
DN_WIDTH = DN_HEADS * DN_HEAD_DIM
QKV_WIDTH = 3 * DN_WIDTH
CONV_WIDTH = 4
DN_CHUNK = 64
POOL_WINDOWS = (2, 4, 8, 16)
POOL_GROUP_DIM = 128
POOL_WIDTH = len(POOL_WINDOWS) * POOL_GROUP_DIM
POOL_OUT_GROUP = D_MODEL // len(POOL_WINDOWS)
POOL_BUF = max(POOL_WINDOWS) - 1
PEER_HEADS = 8
PEER_NKEYS = 128
PEER_TOPK = 16
PEER_KEY_HALF = 128
PEER_HK = PEER_HEADS * PEER_TOPK
EPS = 1e-6

LANES = 128
SUBLANES = 8
CONV_PAD = SUBLANES
POOL_PAD = 16
VMEM_LIMIT = 56 * 1024 * 1024

NT_DIMS = (((1,), (1,)), ((), ()))
TN_DIMS = (((0,), (0,)), ((), ()))


def _dot(a, b):
    return jnp.dot(a.astype(BF16), b.astype(BF16), preferred_element_type=F32)


def _dot_nt(a, b):
    return lax.dot_general(a.astype(BF16), b.astype(BF16), NT_DIMS, preferred_element_type=F32)


def _split3(x):
    hi = x.astype(BF16)
    r1 = x - hi.astype(F32)
    mid = r1.astype(BF16)
    lo = (r1 - mid.astype(F32)).astype(BF16)
    return hi, mid, lo


def _silu(x):
    return x * jax.nn.sigmoid(x)


def _gelu(x):
    return 0.5 * x * (1.0 + lax.erf(x * (0.5 ** 0.5)))


def _softplus(x):
    return jnp.maximum(x, 0.0) + jnp.log(1.0 + jnp.exp(-jnp.abs(x)))


def _mod_rows(ref):
    m = ref[...]
    return m.reshape(m.shape[-2], m.shape[-1])


def _mod_spec(k, rows_per_batch, tm):
    if rows_per_batch >= tm:
        tiles = rows_per_batch // tm
        return pl.BlockSpec((1, 1, 1, D_MODEL), lambda i, *_: (k, i // tiles, 0, 0))
    return pl.BlockSpec((1, tm, D_MODEL), lambda i, *_: (k, i, 0))


def _const_spec(shape):
    nd = len(shape)
    return pl.BlockSpec(shape, lambda *_: (0,) * nd)


def _ada_body(c_ref, w_ref, b_ref, o_ref):
    o_ref[...] = _dot(_silu(c_ref[...]), w_ref[...]) + b_ref[...]


def _ada(c, w_ada, b_ada):
    n = c.shape[0]
    return pl.pallas_call(
        _ada_body,
        grid=(6,),
        in_specs=[pl.BlockSpec((n, D_MODEL), lambda j: (0, 0)),
                  pl.BlockSpec((D_MODEL, D_MODEL), lambda j: (0, j)),
                  pl.BlockSpec((1, D_MODEL), lambda j: (0, j))],
        out_specs=pl.BlockSpec((n, D_MODEL), lambda j: (0, j)),
        out_shape=jax.ShapeDtypeStruct((n, 6 * D_MODEL), F32),
        name="ada",
    )(c, w_ada, b_ada.reshape(1, -1))


_IN_BLOCKS = (("qkv", QKV_WIDTH), ("z", DN_WIDTH), ("pool", POOL_WIDTH),
              ("ga", D_MODEL), ("gb", D_MODEL), ("ba", LANES))
_IN_TOTAL = sum(w for _, w in _IN_BLOCKS)
_IN_F32 = ("ba",)
_IN_COL_CHUNK = 512


def _inproj_body(x_ref, sc_ref, sh_ref, g_ref, w_ref, *out_refs):
    x = x_ref[...]
    y = x * lax.rsqrt(jnp.mean(x * x, axis=-1, keepdims=True) + EPS) * g_ref[...]
    h = (y * (1.0 + _mod_rows(sc_ref)) + _mod_rows(sh_ref)).astype(BF16)
    off = 0
    for (_, width), o_ref in zip(_IN_BLOCKS, out_refs):
        for c0 in range(0, width, _IN_COL_CHUNK):
            cw = min(_IN_COL_CHUNK, width - c0)
            o_ref[:, c0:c0 + cw] = jnp.dot(h, w_ref[:, off + c0:off + c0 + cw],
                                           preferred_element_type=F32).astype(o_ref.dtype)
        off += width


def _inproj(x2d, row0, t, mod, rows_per_batch, norm_g, w_cat, tm):
    row = lambda w: pl.BlockSpec((tm, w), lambda i: (i, 0))
    return pl.pallas_call(
        _inproj_body,
        grid=(t // tm,),
        in_specs=[pl.BlockSpec((tm, D_MODEL), lambda i: (i + row0 // tm, 0)),
                  _mod_spec(1, rows_per_batch, tm), _mod_spec(0, rows_per_batch, tm),
                  _const_spec((1, D_MODEL)),
                  pl.BlockSpec((D_MODEL, _IN_TOTAL), lambda i: (0, 0), pipeline_mode=pl.Buffered(1))],
        out_specs=[row(w) for _, w in _IN_BLOCKS],
        out_shape=[jax.ShapeDtypeStruct((t, w), F32 if name in _IN_F32 else BF16) for name, w in _IN_BLOCKS],
        compiler_params=pltpu.CompilerParams(vmem_limit_bytes=VMEM_LIMIT),
        name="inproj",
    )(x2d, mod, mod, norm_g.reshape(1, -1), w_cat)


def _mixer_body(C, Lv, start,
                qkv_ref, ba_ref, z_ref, pin_ref, ga_ref, gb_ref, cbuf_ref, s0_ref, pbuf_ref,
                convw_ref, alog_ref, dtb_ref, dng_ref, wpool_ref, pscale_ref,
                mixed_ref, nconv_ref, ns_ref, npool_ref,
                xp_scr, act_scr, s_scr, pp_scr, odn_scr):
    n = pl.program_id(1)
    last = pl.num_programs(1) - 1

    @pl.when(n == 0)
    def _load_state():
        xp_scr[0:CONV_PAD, :] = cbuf_ref[0]
        pp_scr[0:POOL_PAD, :] = pbuf_ref[0]
        s_scr[...] = s0_ref[0]

    xp_scr[CONV_PAD:CONV_PAD + C, :] = qkv_ref[0].astype(F32)
    for c0 in range(0, QKV_WIDTH, _IN_COL_CHUNK):
        cs = slice(c0, c0 + _IN_COL_CHUNK)
        y = xp_scr[CONV_PAD:CONV_PAD + C, cs] * convw_ref[CONV_WIDTH - 1:CONV_WIDTH, cs]
        for k in range(CONV_WIDTH - 1):
            r0 = CONV_PAD - (CONV_WIDTH - 1) + k
            y = y + xp_scr[r0:r0 + C, cs] * convw_ref[k:k + 1, cs]
        act_scr[:, cs] = _silu(y)

    ba = ba_ref[0]
    lane = lax.broadcasted_iota(I32, (C, LANES), 1)
    beta_all = jax.nn.sigmoid(ba)
    g_all = -jnp.exp(alog_ref[...]) * _softplus(ba + dtb_ref[...])
    if Lv < C:
        valid = lax.broadcasted_iota(I32, (C, LANES), 0) < Lv
        beta_all = jnp.where(valid, beta_all, 0.0)
        g_all = jnp.where(valid, g_all, 0.0)
    ii = lax.broadcasted_iota(I32, (C, C), 0)
    jj = lax.broadcasted_iota(I32, (C, C), 1)
    causal = ii >= jj
    strict = ii > jj
    tril = jnp.where(causal, 1.0, 0.0).astype(BF16)
    eye = jnp.where(ii == jj, 1.0, 0.0)
    gc_all = sum(jnp.dot(tril, part, preferred_element_type=F32) for part in _split3(g_all))
    if C < LANES:
        gc_sq = jnp.concatenate([gc_all, jnp.zeros((LANES - C, LANES), F32)], axis=0)
    else:
        gc_sq = gc_all
    gc_t = gc_sq.T

    H = range(DN_HEADS)
    hsl = [slice(h * DN_HEAD_DIM, (h + 1) * DN_HEAD_DIM) for h in H]
    beta = [jnp.sum(jnp.where(lane == h, beta_all, 0.0), axis=1, keepdims=True) for h in H]
    gcol = [jnp.sum(jnp.where(lane == DN_HEADS + h, gc_all, 0.0), axis=1, keepdims=True) for h in H]
    grow = [gc_t[DN_HEADS + h:DN_HEADS + h + 1, 0:C] for h in H]
    glast = [g[C - 1:C, :] for g in gcol]
    q = [act_scr[:, hsl[h]] for h in H]
    k = [act_scr[:, DN_WIDTH + h * DN_HEAD_DIM:DN_WIDTH + (h + 1) * DN_HEAD_DIM] for h in H]
    v = [act_scr[:, 2 * DN_WIDTH + h * DN_HEAD_DIM:2 * DN_WIDTH + (h + 1) * DN_HEAD_DIM] for h in H]
    q = [x * lax.rsqrt(jnp.sum(x * x, axis=-1, keepdims=True) + EPS) * (DN_HEAD_DIM ** -0.5) for x in q]
    k = [x * lax.rsqrt(jnp.sum(x * x, axis=-1, keepdims=True) + EPS) for x in k]
    kb = [k[h] * beta[h] for h in H]
    vb = [v[h] * beta[h] for h in H]
    decay = [jnp.where(causal, jnp.exp(jnp.where(causal, gcol[h] - grow[h], 0.0)), 0.0) for h in H]
    lower = [jnp.where(strict, _dot_nt(kb[h], k[h]) * decay[h], 0.0) for h in H]
    ainv = [eye - x for x in lower]
    pw = lower
    p = 1
    while 2 * p < C:
        pw = [_dot(x, x) for x in pw]
        ainv = [ainv[h] + _dot(ainv[h], pw[h]) for h in H]
        p *= 2
    sol = [_dot(ainv[h], jnp.concatenate([vb[h], kb[h] * jnp.exp(gcol[h])], axis=1)) for h in H]
    qk = [_dot_nt(q[h], k[h]) * decay[h] for h in H]
    k_tail = [k[h] * jnp.exp(glast[h] - gcol[h]) for h in H]
    S = [s_scr[h] for h in H]
    v_new = [sol[h][:, :DN_HEAD_DIM] - _dot(sol[h][:, DN_HEAD_DIM:], S[h]) for h in H]
    o = [_dot(q[h] * jnp.exp(gcol[h]), S[h]) + _dot(qk[h], v_new[h]) for h in H]
    for h in H:
        s_scr[h] = S[h] * jnp.exp(glast[h]) + lax.dot_general(
            k_tail[h].astype(BF16), v_new[h].astype(BF16), TN_DIMS, preferred_element_type=F32)
    for h in H:
        zf = z_ref[0, :, hsl[h]].astype(F32)
        odn_scr[:, hsl[h]] = (o[h] * lax.rsqrt(jnp.mean(o[h] * o[h], axis=-1, keepdims=True) + EPS)
                              * dng_ref[...] * _silu(zf))

    pp_scr[POOL_PAD:POOL_PAD + C, :] = pin_ref[0].astype(F32)
    pos = start + n * C + lax.broadcasted_iota(I32, (C, 1), 0)
    for gi, win in enumerate(POOL_WINDOWS):
        gs = slice(gi * POOL_GROUP_DIM, (gi + 1) * POOL_GROUP_DIM)
        xg = pp_scr[POOL_PAD:POOL_PAD + C, gs]
        ssum = xg
        for sft in range(1, win):
            ssum = ssum + pp_scr[POOL_PAD - sft:POOL_PAD - sft + C, gs]
        cnt = jnp.minimum(pos + 1, win).astype(F32)
        pooled = ssum / cnt - xg
        os_ = slice(gi * POOL_OUT_GROUP, (gi + 1) * POOL_OUT_GROUP)
        yp = _dot(pooled, wpool_ref[gi]) * pscale_ref[:, os_]
        mixed_ref[0, :, os_] = (jax.nn.sigmoid(ga_ref[0, :, os_].astype(F32)) * odn_scr[:, os_]
                                + jax.nn.sigmoid(gb_ref[0, :, os_].astype(F32)) * yp).astype(BF16)

    @pl.when(n == last)
    def _store_state():
        nconv_ref[0] = xp_scr[Lv + CONV_PAD - (CONV_WIDTH - 1):Lv + CONV_PAD, :]
        npool_ref[0] = pp_scr[Lv + POOL_PAD - POOL_BUF:Lv + POOL_PAD, :]
        ns_ref[0] = s_scr[...]

    xp_scr[0:CONV_PAD, :] = xp_scr[C:C + CONV_PAD, :]
    pp_scr[0:POOL_PAD, :] = pp_scr[C:C + POOL_PAD, :]


def _mixer(proj, conv_buf, s0, pool_buf, start, seq_len, C,
           conv_w, a_log, dt_bias, dn_norm_g, w_pool, pool_scale):
    b, lp, _ = proj["qkv"].shape
    nchunks = lp // C
    lv = seq_len - (nchunks - 1) * C
    cbuf = jnp.pad(conv_buf, ((0, 0), (CONV_PAD - (CONV_WIDTH - 1), 0), (0, 0)))
    pbuf = jnp.pad(pool_buf, ((0, 0), (POOL_PAD - POOL_BUF, 0), (0, 0)))
    lane_pad = lambda a: jnp.pad(a.reshape(1, -1), ((0, 0), (DN_HEADS, LANES - 2 * DN_HEADS)))
    chunk = lambda w: pl.BlockSpec((1, C, w), lambda i, j: (i, j, 0))
    state = lambda *s: pl.BlockSpec((1,) + s, lambda i, j: (i,) + (0,) * len(s))
    return pl.pallas_call(
        functools.partial(_mixer_body, C, lv, start),
        grid=(b, nchunks),
        in_specs=[chunk(QKV_WIDTH), chunk(LANES), chunk(DN_WIDTH), chunk(POOL_WIDTH),
                  chunk(D_MODEL), chunk(D_MODEL),
                  state(CONV_PAD, QKV_WIDTH), state(DN_HEADS, DN_HEAD_DIM, DN_HEAD_DIM),
                  state(POOL_PAD, POOL_WIDTH),
                  _const_spec((CONV_WIDTH, QKV_WIDTH)), _const_spec((1, LANES)), _const_spec((1, LANES)),
                  _const_spec((1, DN_HEAD_DIM)),
                  _const_spec((len(POOL_WINDOWS), POOL_GROUP_DIM, POOL_OUT_GROUP)),
                  _const_spec((1, D_MODEL))],
        out_specs=[chunk(D_MODEL), state(CONV_WIDTH - 1, QKV_WIDTH),
                   state(DN_HEADS, DN_HEAD_DIM, DN_HEAD_DIM), state(POOL_BUF, POOL_WIDTH)],
        out_shape=[jax.ShapeDtypeStruct((b, lp, D_MODEL), BF16),
                   jax.ShapeDtypeStruct((b, CONV_WIDTH - 1, QKV_WIDTH), F32),
                   jax.ShapeDtypeStruct((b, DN_HEADS, DN_HEAD_DIM, DN_HEAD_DIM), F32),
                   jax.ShapeDtypeStruct((b, POOL_BUF, POOL_WIDTH), F32)],
        scratch_shapes=[pltpu.VMEM((CONV_PAD + C + CONV_PAD, QKV_WIDTH), F32),
                        pltpu.VMEM((C, QKV_WIDTH), F32),
                        pltpu.VMEM((DN_HEADS, DN_HEAD_DIM, DN_HEAD_DIM), F32),
                        pltpu.VMEM((POOL_PAD + C + POOL_PAD, POOL_WIDTH), F32),
                        pltpu.VMEM((C, DN_WIDTH), F32)],
        compiler_params=pltpu.CompilerParams(dimension_semantics=("arbitrary", "arbitrary"),
                                             vmem_limit_bytes=VMEM_LIMIT),
        name="mixer",
    )(proj["qkv"], proj["ba"], proj["z"], proj["pool"], proj["ga"], proj["gb"], cbuf, s0, pbuf,
      conv_w, lane_pad(a_log), lane_pad(dt_bias), dn_norm_g.reshape(1, -1), w_pool,
      pool_scale.reshape(1, -1))


def _top16(s, ids, payload=None):
    big = float(2 ** 24)
    vals, sel, pays = [], [], []
    for _ in range(PEER_TOPK):
        m = jnp.max(s, axis=0, keepdims=True)
        am = jnp.min(jnp.where(s == m, ids, big), axis=0, keepdims=True)
        hit = ids == am
        if payload is not None:
            pays.append(jnp.max(jnp.where(hit, payload, -1.0), axis=0, keepdims=True))
        s = jnp.where(hit, -jnp.inf, s)
        vals.append(m)
        sel.append(am)
    out = (jnp.concatenate(vals, axis=0), jnp.concatenate(sel, axis=0))
    if payload is not None:
        out += (jnp.concatenate(pays, axis=0),)
    return out


_CAND_EDGE = 4


def _post_body(has_prev, has_fin, mixed_ref, x_ref, g1_ref, sc2_ref, sh2_ref, n2g_ref, wout_ref,
               wq_ref, keys_ref, *refs):
    refs = list(refs)
    prev_in = [refs.pop(0) for _ in range(2 if has_prev else 0)]
    fin_in = [refs.pop(0) for _ in range(4 if has_fin else 0)]
    x1_ref, h2_ref, idx_ref, gate_ref = refs[:4]
    extra_out = refs[4:]
    if has_prev:
        pre_ref, pgate_ref = prev_in
        extra_out.pop(0)[...] = _coef_words(pre_ref[...], pgate_ref[...])
    if has_fin:
        _final_body(*fin_in, extra_out.pop(0))
    tm = x_ref.shape[0]
    x1 = x_ref[...] + _mod_rows(g1_ref) * _dot(mixed_ref[...], wout_ref[...])
    x1_ref[...] = x1
    y = x1 * lax.rsqrt(jnp.mean(x1 * x1, axis=-1, keepdims=True) + EPS) * n2g_ref[...]
    h2 = y * (1.0 + _mod_rows(sc2_ref)) + _mod_rows(sh2_ref)
    h2_ref[...] = _pack_words(h2[:, :PACK_HALF], h2[:, PACK_HALF:])
    q = _dot(h2, wq_ref[...])

    K = PEER_TOPK
    key_id = lax.broadcasted_iota(I32, (PEER_NKEYS, 1), 0).astype(F32)
    r16 = lax.broadcasted_iota(I32, (K, 1), 0)
    cand_id = jnp.concatenate([(a * K + r16) for a in range(_CAND_EDGE)]
                              + [(r16 * K + b) for b in range(_CAND_EDGE)], axis=0).astype(F32)
    dup = r16 < _CAND_EDGE
    idx_rows, gate_rows = [], []
    for h in range(PEER_HEADS):
        half = []
        for p in range(2):
            c0 = (h * 2 + p) * PEER_KEY_HALF
            st = _dot_nt(keys_ref[h * 2 + p], q[:, c0:c0 + PEER_KEY_HALF])
            half.append(_top16(st, key_id))
        (s1, i1), (s2, i2) = half
        cand = jnp.concatenate(
            [s1[a:a + 1] + s2 for a in range(_CAND_EDGE)]
            + [jnp.where(dup, -jnp.inf, s1 + s2[b:b + 1]) for b in range(_CAND_EDGE)], axis=0)
        cidx = jnp.concatenate(
            [i1[a:a + 1] * PEER_NKEYS + i2 for a in range(_CAND_EDGE)]
            + [i1 * PEER_NKEYS + i2[b:b + 1] for b in range(_CAND_EDGE)], axis=0)
        best, _, eidx = _top16(cand, cand_id, cidx)
        e = jnp.exp(best - best[0:1])
        gate_rows.append(e / jnp.sum(e, axis=0, keepdims=True))
        idx_rows.append(eidx)
    idx_ref[...] = jnp.concatenate(idx_rows, axis=0).T.astype(I32)
    gate_ref[...] = jnp.concatenate(gate_rows, axis=0).T


def _post(mixed2d, x2d, row0, mod, rows_per_batch, norm2_g, w_out, w_query, keys, tm, prev=None, fin=None):
    t = mixed2d.shape[0]
    steps = t // tm
    row = lambda w: pl.BlockSpec((tm, w), lambda i: (i, 0))
    in_specs = [row(D_MODEL), pl.BlockSpec((tm, D_MODEL), lambda i: (i + row0 // tm, 0)),
                _mod_spec(2, rows_per_batch, tm), _mod_spec(4, rows_per_batch, tm),
                _mod_spec(3, rows_per_batch, tm), _const_spec((1, D_MODEL)),
                _const_spec((D_MODEL, D_MODEL)), _const_spec((D_MODEL, 2 * PEER_HEADS * PEER_KEY_HALF)),
                _const_spec((2 * PEER_HEADS, PEER_NKEYS, PEER_KEY_HALF))]
    out_specs = [row(D_MODEL), row(PACK_HALF), row(PEER_HK), row(PEER_HK)]
    out_shape = [jax.ShapeDtypeStruct((t, D_MODEL), F32), jax.ShapeDtypeStruct((t, PACK_HALF), I32),
                 jax.ShapeDtypeStruct((t, PEER_HK), I32), jax.ShapeDtypeStruct((t, PEER_HK), F32)]
    args = [mixed2d, x2d, mod, mod, mod, norm2_g.reshape(1, -1), w_out, w_query, keys]
    if prev is not None:
        tp = prev[0].shape[0]
        prow = pl.BlockSpec((tp // steps, PEER_HK), lambda i: (i, 0))
        in_specs += [prow, prow]
        out_specs += [prow]
        out_shape += [jax.ShapeDtypeStruct((tp, PEER_HK), I32)]
        args += list(prev)
    if fin is not None:
        x1_f, peer_f, mod_f, rows_f, final_g = fin
        tf = x1_f.shape[0]
        frow = pl.BlockSpec((tf // steps, D_MODEL), lambda i: (i, 0))
        in_specs += [frow, frow, _mod_spec(5, rows_f, tf // steps), _const_spec((1, D_MODEL))]
        out_specs += [frow]
        out_shape += [jax.ShapeDtypeStruct((tf, D_MODEL), F32)]
        args += [x1_f, peer_f, mod_f, final_g.reshape(1, -1)]
    return pl.pallas_call(
        functools.partial(_post_body, prev is not None, fin is not None),
        grid=(steps,),
        in_specs=in_specs, out_specs=out_specs, out_shape=out_shape,
        compiler_params=pltpu.CompilerParams(vmem_limit_bytes=VMEM_LIMIT),
        name="post",
    )(*args)


SC_CORES = 2
SC_SUBCORES = 16
SC_LANES = 16
SC_WORKERS = SC_CORES * SC_SUBCORES
SC_TOKENS = 32
SC_SLOTS = 2
SC_JOB_HEADS = 4
SC_BF16_GROUP = 4
PACK_HALF = D_MODEL // 2
SC_CHUNKS = PACK_HALF // SC_LANES
PROMPT_PARTS = 8
EDGE_SPLITS = 2
RAMP_PARTS = 2
COEF_LAG = 2
FIN_LAG = 3
ROW_TILE = 256


def _bf16_bits(v):
    return lax.bitcast_convert_type(v.astype(BF16).astype(F32), jnp.uint32)


def _pack_words(lo, hi):
    return lax.bitcast_convert_type((_bf16_bits(lo) >> 16) | _bf16_bits(hi), I32)


def _pack_body(x_ref, o_ref):
    o_ref[...] = _pack_words(x_ref[:, :PACK_HALF], x_ref[:, PACK_HALF:])


def _pack_table(tbl, rows=2 * ROW_TILE):
    e = tbl.shape[0]
    return pl.pallas_call(
        _pack_body, grid=(e // rows,),
        in_specs=[pl.BlockSpec((rows, D_MODEL), lambda i: (i, 0))],
        out_specs=pl.BlockSpec((rows, PACK_HALF), lambda i: (i, 0)),
        out_shape=jax.ShapeDtypeStruct((e, PACK_HALF), I32), name="pack_table")(tbl)


def _tree_sum(terms):
    terms = list(terms)
    while len(terms) > 1:
        terms = [a + b for a, b in zip(terms[0::2], terms[1::2])] + terms[len(terms) & ~1:]
    return terms[0]


def _unpack_pair(w):
    lo = plsc.bitcast(lax.shift_left(w, jnp.full(w.shape, 16, I32)), F32)
    hi = plsc.bitcast(w & jnp.full(w.shape, -65536, I32), F32)
    return lo, hi


def _sc_mesh():
    return plsc.VectorSubcoreMesh(core_axis_name="c", subcore_axis_name="s")


def _sc_worker():
    return lax.axis_index("s") * SC_CORES + lax.axis_index("c")


def _sc_jobs(table_hbm, idx_v, buf, sem, compute):
    per_tok = PEER_HEADS // SC_JOB_HEADS
    njobs = idx_v.shape[0] * per_tok
    nrows = SC_JOB_HEADS * PEER_TOPK

    def copy(j, slot):
        rows = idx_v.at[j // per_tok, pl.ds((j % per_tok) * nrows, nrows)]
        return pltpu.make_async_copy(table_hbm.at[rows], buf.at[slot], sem.at[slot])

    for s in range(SC_SLOTS):
        copy(s, s).start()

    def job(j, c):
        s = j % SC_SLOTS
        copy(j, s).wait()

        def head(i, cc):
            compute(j // per_tok, (j % per_tok) * SC_JOB_HEADS + i, s, i * PEER_TOPK)
            return cc
        lax.fori_loop(0, SC_JOB_HEADS, head, 0)

        @pl.when(j + SC_SLOTS < njobs)
        def _next():
            copy(j + SC_SLOTS, s).start()
        return c

    lax.fori_loop(0, njobs, job, 0)


def _peer_u_body(n_tok, idx_hbm, h2_hbm, u_hbm, pre_hbm, idx_v, h2_v, pre_v, ubuf, acc_v, sem):
    base = _sc_worker() * n_tok
    lane = lax.iota(I32, SC_LANES)

    def compute(tt, h, slot, r0):
        def chunk(cg, accs):
            cs = [pl.ds((cg * SC_BF16_GROUP + i) * SC_LANES, SC_LANES) for i in range(SC_BF16_GROUP)]
            xs = [plsc.bitcast(h2_v[tt, c], BF16) for c in cs]
            out = []
            for k, a in enumerate(accs):
                part = _tree_sum([plsc.bitcast(ubuf[slot, r0 + k, c], BF16) * x for c, x in zip(cs, xs)])
                lo, hi = _unpack_pair(plsc.bitcast(part, I32))
                out.append(a + (lo + hi))
            return tuple(out)
        zero = jnp.zeros((SC_LANES,), F32)
        accs = lax.fori_loop(0, SC_CHUNKS // SC_BF16_GROUP, chunk, (zero,) * PEER_TOPK)
        for k, a in enumerate(accs):
            acc_v[k, :] = a
        tot = zero
        for j in range(SC_LANES):
            tot = tot + plsc.load_gather(acc_v, [lane, (lane + j) & (SC_LANES - 1)])
        pre_v[tt, pl.ds(h * PEER_TOPK, PEER_TOPK)] = tot

    tb = idx_v.shape[0]

    def block(bi, c):
        t0 = base + bi * tb
        pltpu.sync_copy(idx_hbm.at[pl.ds(t0, tb)], idx_v)
        pltpu.sync_copy(h2_hbm.at[pl.ds(t0, tb)], h2_v)
        _sc_jobs(u_hbm, idx_v, ubuf, sem, compute)
        pltpu.sync_copy(pre_v, pre_hbm.at[pl.ds(t0, tb)])
        return c

    lax.fori_loop(0, n_tok // tb, block, 0)


def _peer_v_body(n_tok, idx_hbm, coef_hbm, v_hbm, out_hbm, idx_v, coef_v, out_v, vbuf, sem):
    base = _sc_worker() * n_tok
    zero = jnp.zeros((SC_LANES,), F32)

    def compute(tt, h, slot, r0):
        cvec = coef_v[tt, pl.ds(h * PEER_TOPK, PEER_TOPK)]
        cb = [plsc.bitcast(jnp.take_along_axis(cvec, jnp.full((SC_LANES,), k, I32), axis=0), BF16)
              for k in range(PEER_TOPK)]

        @plsc.parallel_loop(0, SC_CHUNKS, unroll=2)
        def _chunk(c):
            cs = pl.ds(c * SC_LANES, SC_LANES)
            prods = [plsc.bitcast(vbuf[slot, r0 + k, cs], BF16) * cb[k] for k in range(PEER_TOPK)]
            pairs = [_unpack_pair(plsc.bitcast(_tree_sum(prods[g:g + SC_BF16_GROUP]), I32))
                     for g in range(0, PEER_TOPK, SC_BF16_GROUP)]
            for half, off in ((0, 0), (1, PACK_HALF)):
                plsc.addupdate(out_v.at[tt, pl.ds(off + c * SC_LANES, SC_LANES)],
                               _tree_sum([p[half] for p in pairs]))

    tb = idx_v.shape[0]

    def block(bi, c):
        t0 = base + bi * tb
        pltpu.sync_copy(idx_hbm.at[pl.ds(t0, tb)], idx_v)
        pltpu.sync_copy(coef_hbm.at[pl.ds(t0, tb)], coef_v)

        def clear(i, cc):
            per_row = D_MODEL // SC_LANES
            out_v[i // per_row, pl.ds((i % per_row) * SC_LANES, SC_LANES)] = zero
            return cc
        lax.fori_loop(0, tb * (D_MODEL // SC_LANES), clear, 0)
        _sc_jobs(v_hbm, idx_v, vbuf, sem, compute)
        pltpu.sync_copy(out_v, out_hbm.at[pl.ds(t0, tb)])
        return c

    lax.fori_loop(0, n_tok // tb, block, 0)


def _peer_sc(body, idx, rows, table, out_width, name):
    t = idx.shape[0]
    assert t % SC_WORKERS == 0
    n_tok = t // SC_WORKERS
    tb = min(SC_TOKENS * (2 if body is _peer_u_body else 1), n_tok)
    assert n_tok % tb == 0 and tb * PEER_HEADS // SC_JOB_HEADS >= SC_SLOTS
    return pl.kernel(
        functools.partial(body, n_tok),
        out_type=jax.ShapeDtypeStruct((t, out_width), F32),
        mesh=_sc_mesh(),
        scratch_types=[pltpu.VMEM((tb, PEER_HK), I32),
                       pltpu.VMEM((tb, rows.shape[1]), rows.dtype),
                       pltpu.VMEM((tb, out_width), F32),
                       pltpu.VMEM((SC_SLOTS, SC_JOB_HEADS * PEER_TOPK, PACK_HALF), I32)]
                      + ([pltpu.VMEM((PEER_TOPK, SC_LANES), F32)] if body is _peer_u_body else [])
                      + [pltpu.SemaphoreType.DMA((SC_SLOTS,))],
        compiler_params=pltpu.CompilerParams(needs_layout_passes=False),
        name=name,
    )(idx, rows, table)


def _coef_words(pre, gates):
    return _pack_words(*(gates * _gelu(pre),) * 2)


def _coef_body(pre_ref, gate_ref, coef_ref):
    coef_ref[...] = _coef_words(pre_ref[...], gate_ref[...])


def _coef(pre, gates, tm):
    t = pre.shape[0]
    row = pl.BlockSpec((tm, PEER_HK), lambda i: (i, 0))
    return pl.pallas_call(_coef_body, grid=(t // tm,), in_specs=[row, row], out_specs=row,
                          out_shape=jax.ShapeDtypeStruct((t, PEER_HK), I32), name="coef")(pre, gates)


def _final_body(x1_ref, peer_ref, g2_ref, fng_ref, y_ref):
    x2 = x1_ref[...] + _mod_rows(g2_ref) * peer_ref[...]
    y_ref[...] = x2 * lax.rsqrt(jnp.mean(x2 * x2, axis=-1, keepdims=True) + EPS) * fng_ref[...]


def _final(x1, peer_out, mod, rows_per_batch, final_g, tm):
    t = x1.shape[0]
    row = pl.BlockSpec((tm, D_MODEL), lambda i: (i, 0))
    return pl.pallas_call(
        _final_body, grid=(t // tm,),
        in_specs=[row, row, _mod_spec(5, rows_per_batch, tm), _const_spec((1, D_MODEL))],
        out_specs=row, out_shape=jax.ShapeDtypeStruct((t, D_MODEL), F32), name="final",
    )(x1, peer_out, mod, final_g.reshape(1, -1))


def _expert_gather_v(g, coef, expert_v):
    g["peer_out"] = _peer_sc(_peer_v_body, g["idx"], coef, expert_v, D_MODEL, "peer_v")


def _front(x, mod, conv_buf, s0, pool_buf, start, chunk, tm, wts, prev, fin):
    x2d, row0, b, l = x
    t = b * l
    assert row0 % tm == 0
    if l >= tm:
        modx = mod.reshape(b, 6, 1, D_MODEL).transpose(1, 0, 2, 3)
    else:
        modx = jnp.repeat(mod.reshape(b, 6, D_MODEL), l, axis=0).transpose(1, 0, 2)
    outs = _inproj(x2d, row0, t, modx, l, wts["norm1_g"], wts["w_cat"], tm)
    lp = -(-l // chunk) * chunk
    proj = {}
    for (name, w), a in zip(_IN_BLOCKS, outs):
        a = a.reshape(b, l, w)
        proj[name] = a if lp == l else jnp.pad(a, ((0, 0), (0, lp - l), (0, 0)))
    mixed, nconv, ns, npool = _mixer(proj, conv_buf, s0, pool_buf, start, l, chunk,
                                     wts["conv_w"], wts["a_log"], wts["dt_bias"], wts["dn_norm_g"],
                                     wts["w_pool"], wts["pool_scale"])
    mixed2d = mixed[:, :l].reshape(t, D_MODEL)
    res = _post(mixed2d, x2d, row0, modx, l, wts["norm2_g"], wts["w_out"], wts["w_query"], wts["keys"], tm,
                prev=None if prev is None else (prev["pre"], prev["gates"]),
                fin=None if fin is None else (fin["x1"], fin["peer_out"], fin["mod"], fin["l"],
                                              wts["final_norm_g"]))
    x1, h2, idx, gates = res[:4]
    extra = list(res[4:])
    coef_prev = extra.pop(0) if prev is not None else None
    y_fin = extra.pop(0).reshape(fin["b"], fin["l"], D_MODEL) if fin is not None else None
    pre = _peer_sc(_peer_u_body, idx, h2, wts["expert_u"], PEER_HK, "peer_u")
    g = dict(x1=x1, idx=idx, gates=gates, pre=pre, mod=modx, b=b, l=l, tm=tm,
             states=(nconv, ns, npool))
    return g, coef_prev, y_fin


def kernel(x_prompt, x_sample, c_prompt, c_sample, state_conv, state_delta, state_pool, w_ada, b_ada, norm1_g, w_in, conv_w, a_log, dt_bias, dn_norm_g, w_pool, pool_scale, w_out, norm2_g, w_query, sub_keys, expert_u, expert_v, final_norm_g):
    bp = x_prompt.shape[0]
    yp, ys = x_prompt, x_sample
    conv_p, delta_p, pool_p, conv_s, delta_s, pool_s = [], [], [], [], [], []
    zero_conv = jnp.zeros((bp, CONV_WIDTH - 1, QKV_WIDTH), F32)
    zero_delta = jnp.zeros((bp, DN_HEADS, DN_HEAD_DIM, DN_HEAD_DIM), F32)
    zero_pool = jnp.zeros((bp, POOL_BUF, POOL_WIDTH), F32)
    c_all = jnp.concatenate([c_prompt, c_sample], axis=0)
    for layer in range(DEPTH):
        wi = w_in[layer]
        o_b = QKV_WIDTH
        o_z = o_b + 2 * DN_HEADS
        w_ba = jnp.pad(wi[:, o_b:o_z], ((0, 0), (0, LANES - 2 * DN_HEADS)))
        w_cat = jnp.concatenate([wi[:, :o_b], wi[:, o_z:], w_ba], axis=1).astype(BF16)
        last = layer == DEPTH - 1
        wts = dict(
            norm1_g=norm1_g[layer], w_cat=w_cat, conv_w=conv_w[layer], a_log=a_log[layer],
            dt_bias=dt_bias[layer], dn_norm_g=dn_norm_g[layer], w_pool=w_pool[layer],
            pool_scale=pool_scale[layer], w_out=w_out[layer].astype(BF16), norm2_g=norm2_g[layer],
            w_query=w_query[layer].astype(BF16),
            keys=sub_keys[layer].reshape(2 * PEER_HEADS, PEER_NKEYS, PEER_KEY_HALF).astype(BF16),
            expert_u=_pack_table(expert_u[layer]), expert_v=_pack_table(expert_v[layer]),
            final_norm_g=final_norm_g if last else jnp.ones_like(final_norm_g))
        mod = _ada(c_all, w_ada[layer], b_ada[layer])
        assert last, "final norm is fused into the expert stage"
        step = bp // PROMPT_PARTS
        seq = x_prompt.shape[1]
        assert step == 1
        xp2d = yp.reshape(bp * seq, D_MODEL)
        zeros = (zero_conv[:step], zero_delta[:step], zero_pool[:step])
        jobs, cuts = [], []
        for b0 in range(0, bp, step):
            n = EDGE_SPLITS if b0 in (0, bp - step) else 1
            cuts.append(n)
            for s0 in range(0, seq, seq // n):
                jobs.append(((xp2d, b0 * seq + s0, step, seq // n), mod[b0:b0 + step],
                             zeros if s0 == 0 else None, s0, DN_CHUNK))
        jobs.append(((ys.reshape(-1, D_MODEL), 0) + ys.shape[:2], mod[bp:], (state_conv[layer], state_delta[layer], state_pool[layer]),
                     PAST_LEN, SUBLANES))
        groups = []
        for j, (xg, mg, states, start, chunk) in enumerate(jobs):
            pi = j - (1 if j <= RAMP_PARTS else COEF_LAG)
            prev = groups[pi] if pi >= 0 and "peer_out" not in groups[pi] else None
            fin = groups[j - FIN_LAG] if j >= FIN_LAG and "peer_out" in groups[j - FIN_LAG] else None
            if fin is not None and fin["x1"].shape[0] % (xg[2] * xg[3] // ROW_TILE):
                fin = None
            if states is None:
                states = groups[j - 1]["states"]
            g, coef_prev, y_fin = _front(xg, mg, *states, start, chunk, ROW_TILE, wts, prev, fin)
            if prev is not None:
                _expert_gather_v(prev, coef_prev, wts["expert_v"])
            if fin is not None:
                fin["y"] = y_fin
            groups.append(g)
        for g in groups:
            if "peer_out" not in g:
                _expert_gather_v(g, _coef(g["pre"], g["gates"], ROW_TILE), wts["expert_v"])
        for g in groups:
            if "y" not in g:
                g["y"] = _final(g["x1"], g["peer_out"], g["mod"], g["l"], wts["final_norm_g"],
                                g["tm"]).reshape(g["b"], g["l"], D_MODEL)
        rows, at = [], 0
        for n in cuts:
            rows.append(groups[at:at + n])
            at += n
        yp = jnp.concatenate([jnp.concatenate([g["y"] for g in row], axis=1) for row in rows], axis=0)
        cp, sp, pp = (jnp.concatenate(a, axis=0) for a in zip(*(row[-1]["states"] for row in rows)))
        ys = groups[-1]["y"]
        cs, ss, ps = groups[-1]["states"]
        conv_p.append(cp)
        delta_p.append(sp)
        pool_p.append(pp)
        conv_s.append(cs)
        delta_s.append(ss)
        pool_s.append(ps)
    return (yp, ys, jnp.stack(conv_p), jnp.stack(delta_p), jnp.stack(pool_p),
            jnp.stack(conv_s), jnp.stack(delta_s), jnp.stack(pool_s))
```

```python
import functools

import jax
import jax.numpy as jnp
from jax import lax
from jax.experimental import pallas as pl
from jax.experimental.pallas import tpu as pltpu
from jax.experimental.pallas import tpu_sc as plsc

F32 = jnp.float32
BF16 = jnp.bfloat16
I32 = jnp.int32

D_MODEL = 1024
DEPTH = 1
PAST_LEN = 16384
DN_HEADS = 8
DN_HEAD_DIM = 128
DN_WIDTH = DN_HEADS * DN_HEAD_DIM
QKV_WIDTH = 3 * DN_WIDTH
CONV_WIDTH = 4
DN_CHUNK = 64
POOL_WINDOWS = (2, 4, 8, 16)
POOL_GROUP_DIM = 128
POOL_WIDTH = len(POOL_WINDOWS) * POOL_GROUP_DIM
POOL_OUT_GROUP = D_MODEL // len(POOL_WINDOWS)
POOL_BUF = max(POOL_WINDOWS) - 1
PEER_HEADS = 8
PEER_NKEYS = 128
PEER_TOPK = 16
PEER_KEY_HALF = 128
PEER_HK = PEER_HEADS * PEER_TOPK
EPS = 1e-6

LANES = 128
SUBLANES = 8
CONV_PAD = SUBLANES
POOL_PAD = 16
VMEM_LIMIT = 56 * 1024 * 1024

NT_DIMS = (((1,), (1,)), ((), ()))
TN_DIMS = (((0,), (0,)), ((), ()))


def _dot(a, b):
    return jnp.dot(a.astype(BF16), b.astype(BF16), preferred_element_type=F32)


def _dot_nt(a, b):
    return lax.dot_general(a.astype(BF16), b.astype(BF16), NT_DIMS, preferred_element_type=F32)


def _split3(x):
    hi = x.astype(BF16)
    r1 = x - hi.astype(F32)
    mid = r1.astype(BF16)
    lo = (r1 - mid.astype(F32)).astype(BF16)
    return hi, mid, lo


def _silu(x):
    return x * jax.nn.sigmoid(x)


def _gelu(x):
    return 0.5 * x * (1.0 + lax.erf(x * (0.5 ** 0.5)))


def _softplus(x):
    return jnp.maximum(x, 0.0) + jnp.log(1.0 + jnp.exp(-jnp.abs(x)))


def _mod_rows(ref):
    m = ref[...]
    return m.reshape(m.shape[-2], m.shape[-1])


def _mod_spec(k, rows_per_batch, tm):
    if rows_per_batch >= tm:
        tiles = rows_per_batch // tm
        return pl.BlockSpec((1, 1, 1, D_MODEL), lambda i, *_: (k, i // tiles, 0, 0))
    return pl.BlockSpec((1, tm, D_MODEL), lambda i, *_: (k, i, 0))


def _const_spec(shape):
    nd = len(shape)
    return pl.BlockSpec(shape, lambda *_: (0,) * nd)


def _ada_body(c_ref, w_ref, b_ref, o_ref):
    o_ref[...] = _dot(_silu(c_ref[...]), w_ref[...]) + b_ref[...]


def _ada(c, w_ada, b_ada):
    n = c.shape[0]
    return pl.pallas_call(
        _ada_body,
        grid=(6,),
        in_specs=[pl.BlockSpec((n, D_MODEL), lambda j: (0, 0)),
                  pl.BlockSpec((D_MODEL, D_MODEL), lambda j: (0, j)),
                  pl.BlockSpec((1, D_MODEL), lambda j: (0, j))],
        out_specs=pl.BlockSpec((n, D_MODEL), lambda j: (0, j)),
        out_shape=jax.ShapeDtypeStruct((n, 6 * D_MODEL), F32),
        name="ada",
    )(c, w_ada, b_ada.reshape(1, -1))


_IN_BLOCKS = (("qkv", QKV_WIDTH), ("z", DN_WIDTH), ("pool", POOL_WIDTH),
              ("ga", D_MODEL), ("gb", D_MODEL), ("ba", LANES))
_IN_TOTAL = sum(w for _, w in _IN_BLOCKS)
_IN_F32 = ("ba",)
_IN_COL_CHUNK = 512


def _inproj_body(x_ref, sc_ref, sh_ref, g_ref, w_ref, *out_refs):
    x = x_ref[...]
    y = x * lax.rsqrt(jnp.mean(x * x, axis=-1, keepdims=True) + EPS) * g_ref[...]
    h = (y * (1.0 + _mod_rows(sc_ref)) + _mod_rows(sh_ref)).astype(BF16)
    off = 0
    for (_, width), o_ref in zip(_IN_BLOCKS, out_refs):
        for c0 in range(0, width, _IN_COL_CHUNK):
            cw = min(_IN_COL_CHUNK, width - c0)
            o_ref[:, c0:c0 + cw] = jnp.dot(h, w_ref[:, off + c0:off + c0 + cw],
                                           preferred_element_type=F32).astype(o_ref.dtype)
        off += width


def _inproj(x2d, row0, t, mod, rows_per_batch, norm_g, w_cat, tm):
    row = lambda w: pl.BlockSpec((tm, w), lambda i: (i, 0))
    return pl.pallas_call(
        _inproj_body,
        grid=(t // tm,),
        in_specs=[pl.BlockSpec((tm, D_MODEL), lambda i: (i + row0 // tm, 0)),
                  _mod_spec(1, rows_per_batch, tm), _mod_spec(0, rows_per_batch, tm),
                  _const_spec((1, D_MODEL)),
                  pl.BlockSpec((D_MODEL, _IN_TOTAL), lambda i: (0, 0), pipeline_mode=pl.Buffered(1))],
        out_specs=[row(w) for _, w in _IN_BLOCKS],
        out_shape=[jax.ShapeDtypeStruct((t, w), F32 if name in _IN_F32 else BF16) for name, w in _IN_BLOCKS],
        compiler_params=pltpu.CompilerParams(vmem_limit_bytes=VMEM_LIMIT),
        name="inproj",
    )(x2d, mod, mod, norm_g.reshape(1, -1), w_cat)


def _mixer_body(C, Lv, start,
                qkv_ref, ba_ref, z_ref, pin_ref, ga_ref, gb_ref, cbuf_ref, s0_ref, pbuf_ref,
                convw_ref, alog_ref, dtb_ref, dng_ref, wpool_ref, pscale_ref,
                mixed_ref, nconv_ref, ns_ref, npool_ref,
                xp_scr, act_scr, s_scr, pp_scr, odn_scr):
    n = pl.program_id(1)
    last = pl.num_programs(1) - 1

    @pl.when(n == 0)
    def _load_state():
        xp_scr[0:CONV_PAD, :] = cbuf_ref[0]
        pp_scr[0:POOL_PAD, :] = pbuf_ref[0]
        s_scr[...] = s0_ref[0]

    xp_scr[CONV_PAD:CONV_PAD + C, :] = qkv_ref[0].astype(F32)
    for c0 in range(0, QKV_WIDTH, _IN_COL_CHUNK):
        cs = slice(c0, c0 + _IN_COL_CHUNK)
        y = xp_scr[CONV_PAD:CONV_PAD + C, cs] * convw_ref[CONV_WIDTH - 1:CONV_WIDTH, cs]
        for k in range(CONV_WIDTH - 1):
            r0 = CONV_PAD - (CONV_WIDTH - 1) + k
            y = y + xp_scr[r0:r0 + C, cs] * convw_ref[k:k + 1, cs]
        act_scr[:, cs] = _silu(y)

    ba = ba_ref[0]
    lane = lax.broadcasted_iota(I32, (C, LANES), 1)
    beta_all = jax.nn.sigmoid(ba)
    g_all = -jnp.exp(alog_ref[...]) * _softplus(ba + dtb_ref[...])
    if Lv < C:
        valid = lax.broadcasted_iota(I32, (C, LANES), 0) < Lv
        beta_all = jnp.where(valid, beta_all, 0.0)
        g_all = jnp.where(valid, g_all, 0.0)
    ii = lax.broadcasted_iota(I32, (C, C), 0)
    jj = lax.broadcasted_iota(I32, (C, C), 1)
    causal = ii >= jj
    strict = ii > jj
    tril = jnp.where(causal, 1.0, 0.0).astype(BF16)
    eye = jnp.where(ii == jj, 1.0, 0.0)
    gc_all = sum(jnp.dot(tril, part, preferred_element_type=F32) for part in _split3(g_all))
    if C < LANES:
        gc_sq = jnp.concatenate([gc_all, jnp.zeros((LANES - C, LANES), F32)], axis=0)
    else:
        gc_sq = gc_all
    gc_t = gc_sq.T

    H = range(DN_HEADS)
    hsl = [slice(h * DN_HEAD_DIM, (h + 1) * DN_HEAD_DIM) for h in H]
    beta = [jnp.sum(jnp.where(lane == h, beta_all, 0.0), axis=1, keepdims=True) for h in H]
    gcol = [jnp.sum(jnp.where(lane == DN_HEADS + h, gc_all, 0.0), axis=1, keepdims=True) for h in H]
    grow = [gc_t[DN_HEADS + h:DN_HEADS + h + 1, 0:C] for h in H]
    glast = [g[C - 1:C, :] for g in gcol]
    q = [act_scr[:, hsl[h]] for h in H]
    k = [act_scr[:, DN_WIDTH + h * DN_HEAD_DIM:DN_WIDTH + (h + 1) * DN_HEAD_DIM] for h in H]
    v = [act_scr[:, 2 * DN_WIDTH + h * DN_HEAD_DIM:2 * DN_WIDTH + (h + 1) * DN_HEAD_DIM] for h in H]
    q = [x * lax.rsqrt(jnp.sum(x * x, axis=-1, keepdims=True) + EPS) * (DN_HEAD_DIM ** -0.5) for x in q]
    k = [x * lax.rsqrt(jnp.sum(x * x, axis=-1, keepdims=True) + EPS) for x in k]
    kb = [k[h] * beta[h] for h in H]
    vb = [v[h] * beta[h] for h in H]
    decay = [jnp.where(causal, jnp.exp(jnp.where(causal, gcol[h] - grow[h], 0.0)), 0.0) for h in H]
    lower = [jnp.where(strict, _dot_nt(kb[h], k[h]) * decay[h], 0.0) for h in H]
    ainv = [eye - x for x in lower]
    pw = lower
    p = 1
    while 2 * p < C:
        pw = [_dot(x, x) for x in pw]
        ainv = [ainv[h] + _dot(ainv[h], pw[h]) for h in H]
        p *= 2
    sol = [_dot(ainv[h], jnp.concatenate([vb[h], kb[h] * jnp.exp(gcol[h])], axis=1)) for h in H]
    qk = [_dot_nt(q[h], k[h]) * decay[h] for h in H]
    k_tail = [k[h] * jnp.exp(glast[h] - gcol[h]) for h in H]
    S = [s_scr[h] for h in H]
    v_new = [sol[h][:, :DN_HEAD_DIM] - _dot(sol[h][:, DN_HEAD_DIM:], S[h]) for h in H]
    o = [_dot(q[h] * jnp.exp(gcol[h]), S[h]) + _dot(qk[h], v_new[h]) for h in H]
    for h in H:
        s_scr[h] = S[h] * jnp.exp(glast[h]) + lax.dot_general(
            k_tail[h].astype(BF16), v_new[h].astype(BF16), TN_DIMS, preferred_element_type=F32)
    for h in H:
        zf = z_ref[0, :, hsl[h]].astype(F32)
        odn_scr[:, hsl[h]] = (o[h] * lax.rsqrt(jnp.mean(o[h] * o[h], axis=-1, keepdims=True) + EPS)
                              * dng_ref[...] * _silu(zf))

    pp_scr[POOL_PAD:POOL_PAD + C, :] = pin_ref[0].astype(F32)
    pos = start + n * C + lax.broadcasted_iota(I32, (C, 1), 0)
    for gi, win in enumerate(POOL_WINDOWS):
        gs = slice(gi * POOL_GROUP_DIM, (gi + 1) * POOL_GROUP_DIM)
        xg = pp_scr[POOL_PAD:POOL_PAD + C, gs]
        ssum = xg
        for sft in range(1, win):
            ssum = ssum + pp_scr[POOL_PAD - sft:POOL_PAD - sft + C, gs]
        cnt = jnp.minimum(pos + 1, win).astype(F32)
        pooled = ssum / cnt - xg
        os_ = slice(gi * POOL_OUT_GROUP, (gi + 1) * POOL_OUT_GROUP)
        yp = _dot(pooled, wpool_ref[gi]) * pscale_ref[:, os_]
        mixed_ref[0, :, os_] = (jax.nn.sigmoid(ga_ref[0, :, os_].astype(F32)) * odn_scr[:, os_]
                                + jax.nn.sigmoid(gb_ref[0, :, os_].astype(F32)) * yp).astype(BF16)

    @pl.when(n == last)
    def _store_state():
        nconv_ref[0] = xp_scr[Lv + CONV_PAD - (CONV_WIDTH - 1):Lv + CONV_PAD, :]
        npool_ref[0] = pp_scr[Lv + POOL_PAD - POOL_BUF:Lv + POOL_PAD, :]
        ns_ref[0] = s_scr[...]

    xp_scr[0:CONV_PAD, :] = xp_scr[C:C + CONV_PAD, :]
    pp_scr[0:POOL_PAD, :] = pp_scr[C:C + POOL_PAD, :]


def _mixer(proj, conv_buf, s0, pool_buf, start, seq_len, C,
           conv_w, a_log, dt_bias, dn_norm_g, w_pool, pool_scale):
    b, lp, _ = proj["qkv"].shape
    nchunks = lp // C
    lv = seq_len - (nchunks - 1) * C
    cbuf = jnp.pad(conv_buf, ((0, 0), (CONV_PAD - (CONV_WIDTH - 1), 0), (0, 0)))
    pbuf = jnp.pad(pool_buf, ((0, 0), (POOL_PAD - POOL_BUF, 0), (0, 0)))
    lane_pad = lambda a: jnp.pad(a.reshape(1, -1), ((0, 0), (DN_HEADS, LANES - 2 * DN_HEADS)))
    chunk = lambda w: pl.BlockSpec((1, C, w), lambda i, j: (i, j, 0))
    state = lambda *s: pl.BlockSpec((1,) + s, lambda i, j: (i,) + (0,) * len(s))
    return pl.pallas_call(
        functools.partial(_mixer_body, C, lv, start),
        grid=(b, nchunks),
        in_specs=[chunk(QKV_WIDTH), chunk(LANES), chunk(DN_WIDTH), chunk(POOL_WIDTH),
                  chunk(D_MODEL), chunk(D_MODEL),
                  state(CONV_PAD, QKV_WIDTH), state(DN_HEADS, DN_HEAD_DIM, DN_HEAD_DIM),
                  state(POOL_PAD, POOL_WIDTH),
                  _const_spec((CONV_WIDTH, QKV_WIDTH)), _const_spec((1, LANES)), _const_spec((1, LANES)),
                  _const_spec((1, DN_HEAD_DIM)),
                  _const_spec((len(POOL_WINDOWS), POOL_GROUP_DIM, POOL_OUT_GROUP)),
                  _const_spec((1, D_MODEL))],
        out_specs=[chunk(D_MODEL), state(CONV_WIDTH - 1, QKV_WIDTH),
                   state(DN_HEADS, DN_HEAD_DIM, DN_HEAD_DIM), state(POOL_BUF, POOL_WIDTH)],
        out_shape=[jax.ShapeDtypeStruct((b, lp, D_MODEL), BF16),
                   jax.ShapeDtypeStruct((b, CONV_WIDTH - 1, QKV_WIDTH), F32),
                   jax.ShapeDtypeStruct((b, DN_HEADS, DN_HEAD_DIM, DN_HEAD_DIM), F32),
                   jax.ShapeDtypeStruct((b, POOL_BUF, POOL_WIDTH), F32)],
        scratch_shapes=[pltpu.VMEM((CONV_PAD + C + CONV_PAD, QKV_WIDTH), F32),
                        pltpu.VMEM((C, QKV_WIDTH), F32),
                        pltpu.VMEM((DN_HEADS, DN_HEAD_DIM, DN_HEAD_DIM), F32),
                        pltpu.VMEM((POOL_PAD + C + POOL_PAD, POOL_WIDTH), F32),
                        pltpu.VMEM((C, DN_WIDTH), F32)],
        compiler_params=pltpu.CompilerParams(dimension_semantics=("arbitrary", "arbitrary"),
                                             vmem_limit_bytes=VMEM_LIMIT),
        name="mixer",
    )(proj["qkv"], proj["ba"], proj["z"], proj["pool"], proj["ga"], proj["gb"], cbuf, s0, pbuf,
      conv_w, lane_pad(a_log), lane_pad(dt_bias), dn_norm_g.reshape(1, -1), w_pool,
      pool_scale.reshape(1, -1))


def _top16(s, ids, payload=None):
    big = float(2 ** 24)
    vals, sel, pays = [], [], []
    for _ in range(PEER_TOPK):
        m = jnp.max(s, axis=0, keepdims=True)
        am = jnp.min(jnp.where(s == m, ids, big), axis=0, keepdims=True)
        hit = ids == am
        if payload is not None:
            pays.append(jnp.max(jnp.where(hit, payload, -1.0), axis=0, keepdims=True))
        s = jnp.where(hit, -jnp.inf, s)
        vals.append(m)
        sel.append(am)
    out = (jnp.concatenate(vals, axis=0), jnp.concatenate(sel, axis=0))
    if payload is not None:
        out += (jnp.concatenate(pays, axis=0),)
    return out


_CAND_EDGE = 4


def _post_body(has_prev, has_fin, mixed_ref, x_ref, g1_ref, sc2_ref, sh2_ref, n2g_ref, wout_ref,
               wq_ref, keys_ref, *refs):
    refs = list(refs)
    prev_in = [refs.pop(0) for _ in range(2 if has_prev else 0)]
    fin_in = [refs.pop(0) for _ in range(4 if has_fin else 0)]
    x1_ref, h2_ref, idx_ref, gate_ref = refs[:4]
    extra_out = refs[4:]
    if has_prev:
        pre_ref, pgate_ref = prev_in
        extra_out.pop(0)[...] = _coef_words(pre_ref[...], pgate_ref[...])
    if has_fin:
        _final_body(*fin_in, extra_out.pop(0))
    tm = x_ref.shape[0]
    x1 = x_ref[...] + _mod_rows(g1_ref) * _dot(mixed_ref[...], wout_ref[...])
    x1_ref[...] = x1
    y = x1 * lax.rsqrt(jnp.mean(x1 * x1, axis=-1, keepdims=True) + EPS) * n2g_ref[...]
    h2 = y * (1.0 + _mod_rows(sc2_ref)) + _mod_rows(sh2_ref)
    h2_ref[...] = _pack_words(h2[:, :PACK_HALF], h2[:, PACK_HALF:])
    q = _dot(h2, wq_ref[...])

    K = PEER_TOPK
    key_id = lax.broadcasted_iota(I32, (PEER_NKEYS, 1), 0).astype(F32)
    r16 = lax.broadcasted_iota(I32, (K, 1), 0)
    cand_id = jnp.concatenate([(a * K + r16) for a in range(_CAND_EDGE)]
                              + [(r16 * K + b) for b in range(_CAND_EDGE)], axis=0).astype(F32)
    dup = r16 < _CAND_EDGE
    idx_rows, gate_rows = [], []
    for h in range(PEER_HEADS):
        half = []
        for p in range(2):
            c0 = (h * 2 + p) * PEER_KEY_HALF
            st = _dot_nt(keys_ref[h * 2 + p], q[:, c0:c0 + PEER_KEY_HALF])
            half.append(_top16(st, key_id))
        (s1, i1), (s2, i2) = half
        cand = jnp.concatenate(
            [s1[a:a + 1] + s2 for a in range(_CAND_EDGE)]
            + [jnp.where(dup, -jnp.inf, s1 + s2[b:b + 1]) for b in range(_CAND_EDGE)], axis=0)
        cidx = jnp.concatenate(
            [i1[a:a + 1] * PEER_NKEYS + i2 for a in range(_CAND_EDGE)]
            + [i1 * PEER_NKEYS + i2[b:b + 1] for b in range(_CAND_EDGE)], axis=0)
        best, _, eidx = _top16(cand, cand_id, cidx)
        e = jnp.exp(best - best[0:1])
        gate_rows.append(e / jnp.sum(e, axis=0, keepdims=True))
        idx_rows.append(eidx)
    idx_ref[...] = jnp.concatenate(idx_rows, axis=0).T.astype(I32)
    gate_ref[...] = jnp.concatenate(gate_rows, axis=0).T


def _post(mixed2d, x2d, row0, mod, rows_per_batch, norm2_g, w_out, w_query, keys, tm, prev=None, fin=None):
    t = mixed2d.shape[0]
    steps = t // tm
    row = lambda w: pl.BlockSpec((tm, w), lambda i: (i, 0))
    in_specs = [row(D_MODEL), pl.BlockSpec((tm, D_MODEL), lambda i: (i + row0 // tm, 0)),
                _mod_spec(2, rows_per_batch, tm), _mod_spec(4, rows_per_batch, tm),
                _mod_spec(3, rows_per_batch, tm), _const_spec((1, D_MODEL)),
                _const_spec((D_MODEL, D_MODEL)), _const_spec((D_MODEL, 2 * PEER_HEADS * PEER_KEY_HALF)),
                _const_spec((2 * PEER_HEADS, PEER_NKEYS, PEER_KEY_HALF))]
    out_specs = [row(D_MODEL), row(PACK_HALF), row(PEER_HK), row(PEER_HK)]
    out_shape = [jax.ShapeDtypeStruct((t, D_MODEL), F32), jax.ShapeDtypeStruct((t, PACK_HALF), I32),
                 jax.ShapeDtypeStruct((t, PEER_HK), I32), jax.ShapeDtypeStruct((t, PEER_HK), F32)]
    args = [mixed2d, x2d, mod, mod, mod, norm2_g.reshape(1, -1), w_out, w_query, keys]
    if prev is not None:
        tp = prev[0].shape[0]
        prow = pl.BlockSpec((tp // steps, PEER_HK), lambda i: (i, 0))
        in_specs += [prow, prow]
        out_specs += [prow]
        out_shape += [jax.ShapeDtypeStruct((tp, PEER_HK), I32)]
        args += list(prev)
    if fin is not None:
        x1_f, peer_f, mod_f, rows_f, final_g = fin
        tf = x1_f.shape[0]
        frow = pl.BlockSpec((tf // steps, D_MODEL), lambda i: (i, 0))
        in_specs += [frow, frow, _mod_spec(5, rows_f, tf // steps), _const_spec((1, D_MODEL))]
        out_specs += [frow]
        out_shape += [jax.ShapeDtypeStruct((tf, D_MODEL), F32)]
        args += [x1_f, peer_f, mod_f, final_g.reshape(1, -1)]
    return pl.pallas_call(
        functools.partial(_post_body, prev is not None, fin is not None),
        grid=(steps,),
        in_specs=in_specs, out_specs=out_specs, out_shape=out_shape,
        compiler_params=pltpu.CompilerParams(vmem_limit_bytes=VMEM_LIMIT),
        name="post",
    )(*args)


SC_CORES = 2
SC_SUBCORES = 16
SC_LANES = 16
SC_WORKERS = SC_CORES * SC_SUBCORES
SC_TOKENS = 32
SC_SLOTS = 4
SC_JOB_HEADS = 2
SC_BF16_GROUP = 4
PACK_HALF = D_MODEL // 2
SC_CHUNKS = PACK_HALF // SC_LANES
PROMPT_PARTS = 8
EDGE_SPLITS = 2
RAMP_PARTS = 3
COEF_LAG = 2
FIN_LAG = 3
ROW_TILE = 256


def _bf16_bits(v):
    return lax.bitcast_convert_type(v.astype(BF16).astype(F32), jnp.uint32)


def _pack_words(lo, hi):
    return lax.bitcast_convert_type((_bf16_bits(lo) >> 16) | _bf16_bits(hi), I32)


def _pack_body(x_ref, o_ref):
    o_ref[...] = _pack_words(x_ref[:, :PACK_HALF], x_ref[:, PACK_HALF:])


def _pack_table(tbl, rows=2 * ROW_TILE):
    e = tbl.shape[0]
    return pl.pallas_call(
        _pack_body, grid=(e // rows,),
        in_specs=[pl.BlockSpec((rows, D_MODEL), lambda i: (i, 0))],
        out_specs=pl.BlockSpec((rows, PACK_HALF), lambda i: (i, 0)),
        out_shape=jax.ShapeDtypeStruct((e, PACK_HALF), I32), name="pack_table")(tbl)


def _tree_sum(terms):
    terms = list(terms)
    while len(terms) > 1:
        terms = [a + b for a, b in zip(terms[0::2], terms[1::2])] + terms[len(terms) & ~1:]
    return terms[0]


def _unpack_pair(w):
    lo = plsc.bitcast(lax.shift_left(w, jnp.full(w.shape, 16, I32)), F32)
    hi = plsc.bitcast(w & jnp.full(w.shape, -65536, I32), F32)
    return lo, hi


def _sc_mesh():
    return plsc.VectorSubcoreMesh(core_axis_name="c", subcore_axis_name="s")


def _sc_worker():
    return lax.axis_index("s") * SC_CORES + lax.axis_index("c")


def _sc_jobs(table_hbm, idx_v, buf, sem, compute):
    per_tok = PEER_HEADS // SC_JOB_HEADS
    njobs = idx_v.shape[0] * per_tok
    nrows = SC_JOB_HEADS * PEER_TOPK

    def copy(j, slot):
        rows = idx_v.at[j // per_tok, pl.ds((j % per_tok) * nrows, nrows)]
        return pltpu.make_async_copy(table_hbm.at[rows], buf.at[slot], sem.at[slot])

    for s in range(SC_SLOTS):
        copy(s, s).start()

    def job(j, c):
        s = j % SC_SLOTS
        copy(j, s).wait()

        def head(i, cc):
            compute(j // per_tok, (j % per_tok) * SC_JOB_HEADS + i, s, i * PEER_TOPK)
            return cc
        lax.fori_loop(0, SC_JOB_HEADS, head, 0)

        @pl.when(j + SC_SLOTS < njobs)
        def _next():
            copy(j + SC_SLOTS, s).start()
        return c

    lax.fori_loop(0, njobs, job, 0)


def _peer_u_body(n_tok, idx_hbm, h2_hbm, u_hbm, pre_hbm, idx_v, h2_v, pre_v, ubuf, acc_v, sem):
    base = _sc_worker() * n_tok
    lane = lax.iota(I32, SC_LANES)

    def compute(tt, h, slot, r0):
        def chunk(cg, accs):
            cs = [pl.ds((cg * SC_BF16_GROUP + i) * SC_LANES, SC_LANES) for i in range(SC_BF16_GROUP)]
            xs = [plsc.bitcast(h2_v[tt, c], BF16) for c in cs]
            out = []
            for k, a in enumerate(accs):
                part = _tree_sum([plsc.bitcast(ubuf[slot, r0 + k, c], BF16) * x for c, x in zip(cs, xs)])
                lo, hi = _unpack_pair(plsc.bitcast(part, I32))
                out.append(a + (lo + hi))
            return tuple(out)
        zero = jnp.zeros((SC_LANES,), F32)
        accs = lax.fori_loop(0, SC_CHUNKS // SC_BF16_GROUP, chunk, (zero,) * PEER_TOPK)
        for k, a in enumerate(accs):
            acc_v[k, :] = a
        tot = zero
        for j in range(SC_LANES):
            tot = tot + plsc.load_gather(acc_v, [lane, (lane + j) & (SC_LANES - 1)])
        pre_v[tt, pl.ds(h * PEER_TOPK, PEER_TOPK)] = tot

    tb = idx_v.shape[0]

    def block(bi, c):
        t0 = base + bi * tb
        pltpu.sync_copy(idx_hbm.at[pl.ds(t0, tb)], idx_v)
        pltpu.sync_copy(h2_hbm.at[pl.ds(t0, tb)], h2_v)
        _sc_jobs(u_hbm, idx_v, ubuf, sem, compute)
        pltpu.sync_copy(pre_v, pre_hbm.at[pl.ds(t0, tb)])
        return c

    lax.fori_loop(0, n_tok // tb, block, 0)


def _peer_v_body(n_tok, idx_hbm, coef_hbm, v_hbm, out_hbm, idx_v, coef_v, out_v, vbuf, sem):
    base = _sc_worker() * n_tok
    zero = jnp.zeros((SC_LANES,), F32)

    def compute(tt, h, slot, r0):
        cvec = coef_v[tt, pl.ds(h * PEER_TOPK, PEER_TOPK)]
        cb = [plsc.bitcast(jnp.take_along_axis(cvec, jnp.full((SC_LANES,), k, I32), axis=0), BF16)
              for k in range(PEER_TOPK)]

        @plsc.parallel_loop(0, SC_CHUNKS, unroll=2)
        def _chunk(c):
            cs = pl.ds(c * SC_LANES, SC_LANES)
            prods = [plsc.bitcast(vbuf[slot, r0 + k, cs], BF16) * cb[k] for k in range(PEER_TOPK)]
            pairs = [_unpack_pair(plsc.bitcast(_tree_sum(prods[g:g + SC_BF16_GROUP]), I32))
                     for g in range(0, PEER_TOPK, SC_BF16_GROUP)]
            for half, off in ((0, 0), (1, PACK_HALF)):
                plsc.addupdate(out_v.at[tt, pl.ds(off + c * SC_LANES, SC_LANES)],
                               _tree_sum([p[half] for p in pairs]))

    tb = idx_v.shape[0]

    def block(bi, c):
        t0 = base + bi * tb
        pltpu.sync_copy(idx_hbm.at[pl.ds(t0, tb)], idx_v)
        pltpu.sync_copy(coef_hbm.at[pl.ds(t0, tb)], coef_v)

        def clear(i, cc):
            per_row = D_MODEL // SC_LANES
            out_v[i // per_row, pl.ds((i % per_row) * SC_LANES, SC_LANES)] = zero
            return cc
        lax.fori_loop(0, tb * (D_MODEL // SC_LANES), clear, 0)
        _sc_jobs(v_hbm, idx_v, vbuf, sem, compute)
        pltpu.sync_copy(out_v, out_hbm.at[pl.ds(t0, tb)])
        return c

    lax.fori_loop(0, n_tok // tb, block, 0)


def _peer_sc(body, idx, rows, table, out_width, name):
    t = idx.shape[0]
    assert t % SC_WORKERS == 0
    n_tok = t // SC_WORKERS
    tb = min(SC_TOKENS * (2 if body is _peer_u_body else 1), n_tok)
    assert n_tok % tb == 0 and tb * PEER_HEADS // SC_JOB_HEADS >= SC_SLOTS
    return pl.kernel(
        functools.partial(body, n_tok),
        out_type=jax.ShapeDtypeStruct((t, out_width), F32),
        mesh=_sc_mesh(),
        scratch_types=[pltpu.VMEM((tb, PEER_HK), I32),
                       pltpu.VMEM((tb, rows.shape[1]), rows.dtype),
                       pltpu.VMEM((tb, out_width), F32),
                       pltpu.VMEM((SC_SLOTS, SC_JOB_HEADS * PEER_TOPK, PACK_HALF), I32)]
                      + ([pltpu.VMEM((PEER_TOPK, SC_LANES), F32)] if body is _peer_u_body else [])
                      + [pltpu.SemaphoreType.DMA((SC_SLOTS,))],
        compiler_params=pltpu.CompilerParams(needs_layout_passes=False),
        name=name,
    )(idx, rows, table)


def _coef_words(pre, gates):
    return _pack_words(*(gates * _gelu(pre),) * 2)


def _coef_body(pre_ref, gate_ref, coef_ref):
    coef_ref[...] = _coef_words(pre_ref[...], gate_ref[...])


def _coef(pre, gates, tm):
    t = pre.shape[0]
    row = pl.BlockSpec((tm, PEER_HK), lambda i: (i, 0))
    return pl.pallas_call(_coef_body, grid=(t // tm,), in_specs=[row, row], out_specs=row,
                          out_shape=jax.ShapeDtypeStruct((t, PEER_HK), I32), name="coef")(pre, gates)


def _final_body(x1_ref, peer_ref, g2_ref, fng_ref, y_ref):
    x2 = x1_ref[...] + _mod_rows(g2_ref) * peer_ref[...]
    y_ref[...] = x2 * lax.rsqrt(jnp.mean(x2 * x2, axis=-1, keepdims=True) + EPS) * fng_ref[...]


def _final(x1, peer_out, mod, rows_per_batch, final_g, tm):
    t = x1.shape[0]
    row = pl.BlockSpec((tm, D_MODEL), lambda i: (i, 0))
    return pl.pallas_call(
        _final_body, grid=(t // tm,),
        in_specs=[row, row, _mod_spec(5, rows_per_batch, tm), _const_spec((1, D_MODEL))],
        out_specs=row, out_shape=jax.ShapeDtypeStruct((t, D_MODEL), F32), name="final",
    )(x1, peer_out, mod, final_g.reshape(1, -1))


def _expert_gather_v(g, coef, expert_v):
    g["peer_out"] = _peer_sc(_peer_v_body, g["idx"], coef, expert_v, D_MODEL, "peer_v")


def _front(x, mod, conv_buf, s0, pool_buf, start, chunk, tm, wts, prev, fin):
    x2d, row0, b, l = x
    t = b * l
    assert row0 % tm == 0
    if l >= tm:
        modx = mod.reshape(b, 6, 1, D_MODEL).transpose(1, 0, 2, 3)
    else:
        modx = jnp.repeat(mod.reshape(b, 6, D_MODEL), l, axis=0).transpose(1, 0, 2)
    outs = _inproj(x2d, row0, t, modx, l, wts["norm1_g"], wts["w_cat"], tm)
    lp = -(-l // chunk) * chunk
    proj = {}
    for (name, w), a in zip(_IN_BLOCKS, outs):
        a = a.reshape(b, l, w)
        proj[name] = a if lp == l else jnp.pad(a, ((0, 0), (0, lp - l), (0, 0)))
    mixed, nconv, ns, npool = _mixer(proj, conv_buf, s0, pool_buf, start, l, chunk,
                                     wts["conv_w"], wts["a_log"], wts["dt_bias"], wts["dn_norm_g"],
                                     wts["w_pool"], wts["pool_scale"])
    mixed2d = mixed[:, :l].reshape(t, D_MODEL)
    res = _post(mixed2d, x2d, row0, modx, l, wts["norm2_g"], wts["w_out"], wts["w_query"], wts["keys"], tm,
                prev=None if prev is None else (prev["pre"], prev["gates"]),
                fin=None if fin is None else (fin["x1"], fin["peer_out"], fin["mod"], fin["l"],
                                              wts["final_norm_g"]))
    x1, h2, idx, gates = res[:4]
    extra = list(res[4:])
    coef_prev = extra.pop(0) if prev is not None else None
    y_fin = extra.pop(0).reshape(fin["b"], fin["l"], D_MODEL) if fin is not None else None
    pre = _peer_sc(_peer_u_body, idx, h2, wts["expert_u"], PEER_HK, "peer_u")
    g = dict(x1=x1, idx=idx, gates=gates, pre=pre, mod=modx, b=b, l=l, tm=tm,
             states=(nconv, ns, npool))
    return g, coef_prev, y_fin


def kernel(x_prompt, x_sample, c_prompt, c_sample, state_conv, state_delta, state_pool, w_ada, b_ada, norm1_g, w_in, conv_w, a_log, dt_bias, dn_norm_g, w_pool, pool_scale, w_out, norm2_g, w_query, sub_keys, expert_u, expert_v, final_norm_g):
    bp = x_prompt.shape[0]
    yp, ys = x_prompt, x_sample
    conv_p, delta_p, pool_p, conv_s, delta_s, pool_s = [], [], [], [], [], []
    zero_conv = jnp.zeros((bp, CONV_WIDTH - 1, QKV_WIDTH), F32)
    zero_delta = jnp.zeros((bp, DN_HEADS, DN_HEAD_DIM, DN_HEAD_DIM), F32)
    zero_pool = jnp.zeros((bp, POOL_BUF, POOL_WIDTH), F32)
    c_all = jnp.concatenate([c_prompt, c_sample], axis=0)
    for layer in range(DEPTH):
        wi = w_in[layer]
        o_b = QKV_WIDTH
        o_z = o_b + 2 * DN_HEADS
        w_ba = jnp.pad(wi[:, o_b:o_z], ((0, 0), (0, LANES - 2 * DN_HEADS)))
        w_cat = jnp.concatenate([wi[:, :o_b], wi[:, o_z:], w_ba], axis=1).astype(BF16)
        last = layer == DEPTH - 1
        wts = dict(
            norm1_g=norm1_g[layer], w_cat=w_cat, conv_w=conv_w[layer], a_log=a_log[layer],
            dt_bias=dt_bias[layer], dn_norm_g=dn_norm_g[layer], w_pool=w_pool[layer],
            pool_scale=pool_scale[layer], w_out=w_out[layer].astype(BF16), norm2_g=norm2_g[layer],
            w_query=w_query[layer].astype(BF16),
            keys=sub_keys[layer].reshape(2 * PEER_HEADS, PEER_NKEYS, PEER_KEY_HALF).astype(BF16),
            expert_u=_pack_table(expert_u[layer]), expert_v=_pack_table(expert_v[layer]),
            final_norm_g=final_norm_g if last else jnp.ones_like(final_norm_g))
        mod = _ada(c_all, w_ada[layer], b_ada[layer])
        assert last, "final norm is fused into the expert stage"
        step = bp // PROMPT_PARTS
        seq = x_prompt.shape[1]
        assert step == 1
        xp2d = yp.reshape(bp * seq, D_MODEL)
        zeros = (zero_conv[:step], zero_delta[:step], zero_pool[:step])
        jobs, cuts = [], []
        for b0 in range(0, bp, step):
            n = EDGE_SPLITS if b0 in (0, bp - step) else 1
            cuts.append(n)
            for s0 in range(0, seq, seq // n):
                jobs.append(((xp2d, b0 * seq + s0, step, seq // n), mod[b0:b0 + step],
                             zeros if s0 == 0 else None, s0, DN_CHUNK))
        jobs.append(((ys.reshape(-1, D_MODEL), 0) + ys.shape[:2], mod[bp:], (state_conv[layer], state_delta[layer], state_pool[layer]),
                     PAST_LEN, SUBLANES))
        groups = []
        for j, (xg, mg, states, start, chunk) in enumerate(jobs):
            pi = j - (1 if j <= RAMP_PARTS else COEF_LAG)
            prev = groups[pi] if pi >= 0 and "peer_out" not in groups[pi] else None
            fin = groups[j - FIN_LAG] if j >= FIN_LAG and "peer_out" in groups[j - FIN_LAG] else None
            if fin is not None and fin["x1"].shape[0] % (xg[2] * xg[3] // ROW_TILE):
                fin = None
            if states is None:
                states = groups[j - 1]["states"]
            g, coef_prev, y_fin = _front(xg, mg, *states, start, chunk, ROW_TILE, wts, prev, fin)
            if prev is not None:
                _expert_gather_v(prev, coef_prev, wts["expert_v"])
            if fin is not None:
                fin["y"] = y_fin
            groups.append(g)
        for g in groups:
            if "peer_out" not in g:
                _expert_gather_v(g, _coef(g["pre"], g["gates"], ROW_TILE), wts["expert_v"])
        for g in groups:
            if "y" not in g:
                g["y"] = _final(g["x1"], g["peer_out"], g["mod"], g["l"], wts["final_norm_g"],
                                g["tm"]).reshape(g["b"], g["l"], D_MODEL)
        rows, at = [], 0
        for n in cuts:
            rows.append(groups[at:at + n])
            at += n
        yp = jnp.concatenate([jnp.concatenate([g["y"] for g in row], axis=1) for row in rows], axis=0)
        cp, sp, pp = (jnp.concatenate(a, axis=0) for a in zip(*(row[-1]["states"] for row in rows)))
        ys = groups[-1]["y"]
        cs, ss, ps = groups[-1]["states"]
        conv_p.append(cp)
        delta_p.append(sp)
        pool_p.append(pp)
        conv_s.append(cs)
        delta_s.append(ss)
        pool_s.append(ps)
    return (yp, ys, jnp.stack(conv_p), jnp.stack(delta_p), jnp.stack(pool_p),
            jnp.stack(conv_s), jnp.stack(delta_s), jnp.stack(pool_s))
```

```python
import functools

import jax
import jax.numpy as jnp
from jax import lax
from jax.experimental import pallas as pl
from jax.experimental.pallas import tpu as pltpu
from jax.experimental.pallas import tpu_sc as plsc

F32 = jnp.float32
BF16 = jnp.bfloat16
I32 = jnp.int32

D_MODEL = 1024
DEPTH = 1
PAST_LEN = 16384
DN_HEADS = 8
DN_HEAD_DIM = 128
DN_WIDTH = DN_HEADS * DN_HEAD_DIM
QKV_WIDTH = 3 * DN_WIDTH
CONV_WIDTH = 4
DN_CHUNK = 64
POOL_WINDOWS = (2, 4, 8, 16)
POOL_GROUP_DIM = 128
POOL_WIDTH = len(POOL_WINDOWS) * POOL_GROUP_DIM
POOL_OUT_GROUP = D_MODEL // len(POOL_WINDOWS)
POOL_BUF = max(POOL_WINDOWS) - 1
PEER_HEADS = 8
PEER_NKEYS = 128
PEER_TOPK = 16
PEER_KEY_HALF = 128
PEER_HK = PEER_HEADS * PEER_TOPK
EPS = 1e-6

LANES = 128
SUBLANES = 8
CONV_PAD = SUBLANES
POOL_PAD = 16
VMEM_LIMIT = 56 * 1024 * 1024

NT_DIMS = (((1,), (1,)), ((), ()))
TN_DIMS = (((0,), (0,)), ((), ()))


def _dot(a, b):
    return jnp.dot(a.astype(BF16), b.astype(BF16), preferred_element_type=F32)


def _dot_nt(a, b):
    return lax.dot_general(a.astype(BF16), b.astype(BF16), NT_DIMS, preferred_element_type=F32)


def _split3(x):
    hi = x.astype(BF16)
    r1 = x - hi.astype(F32)
    mid = r1.astype(BF16)
    lo = (r1 - mid.astype(F32)).astype(BF16)
    return hi, mid, lo


def _silu(x):
    return x * jax.nn.sigmoid(x)


def _gelu(x):
    return 0.5 * x * (1.0 + lax.erf(x * (0.5 ** 0.5)))


def _softplus(x):
    return jnp.maximum(x, 0.0) + jnp.log(1.0 + jnp.exp(-jnp.abs(x)))


def _mod_rows(ref):
    m = ref[...]
    return m.reshape(m.shape[-2], m.shape[-1])


def _mod_spec(k, rows_per_batch, tm):
    if rows_per_batch >= tm:
        tiles = rows_per_batch // tm
        return pl.BlockSpec((1, 1, 1, D_MODEL), lambda i, *_: (k, i // tiles, 0, 0))
    return pl.BlockSpec((1, tm, D_MODEL), lambda i, *_: (k, i, 0))


def _const_spec(shape):
    nd = len(shape)
    return pl.BlockSpec(shape, lambda *_: (0,) * nd)


def _ada_body(c_ref, w_ref, b_ref, o_ref):
    o_ref[...] = _dot(_silu(c_ref[...]), w_ref[...]) + b_ref[...]


def _ada(c, w_ada, b_ada):
    n = c.shape[0]
    return pl.pallas_call(
        _ada_body,
        grid=(6,),
        in_specs=[pl.BlockSpec((n, D_MODEL), lambda j: (0, 0)),
                  pl.BlockSpec((D_MODEL, D_MODEL), lambda j: (0, j)),
                  pl.BlockSpec((1, D_MODEL), lambda j: (0, j))],
        out_specs=pl.BlockSpec((n, D_MODEL), lambda j: (0, j)),
        out_shape=jax.ShapeDtypeStruct((n, 6 * D_MODEL), F32),
        name="ada",
    )(c, w_ada, b_ada.reshape(1, -1))


_IN_BLOCKS = (("qkv", QKV_WIDTH), ("z", DN_WIDTH), ("pool", POOL_WIDTH),
              ("ga", D_MODEL), ("gb", D_MODEL), ("ba", LANES))
_IN_TOTAL = sum(w for _, w in _IN_BLOCKS)
_IN_F32 = ("ba",)
_IN_COL_CHUNK = 512


def _inproj_body(x_ref, sc_ref, sh_ref, g_ref, w_ref, *out_refs):
    x = x_ref[...]
    y = x * lax.rsqrt(jnp.mean(x * x, axis=-1, keepdims=True) + EPS) * g_ref[...]
    h = (y * (1.0 + _mod_rows(sc_ref)) + _mod_rows(sh_ref)).astype(BF16)
    off = 0
    for (_, width), o_ref in zip(_IN_BLOCKS, out_refs):
        for c0 in range(0, width, _IN_COL_CHUNK):
            cw = min(_IN_COL_CHUNK, width - c0)
            o_ref[:, c0:c0 + cw] = jnp.dot(h, w_ref[:, off + c0:off + c0 + cw],
                                           preferred_element_type=F32).astype(o_ref.dtype)
        off += width


def _inproj(x2d, row0, t, mod, rows_per_batch, norm_g, w_cat, tm):
    row = lambda w: pl.BlockSpec((tm, w), lambda i: (i, 0))
    return pl.pallas_call(
        _inproj_body,
        grid=(t // tm,),
        in_specs=[pl.BlockSpec((tm, D_MODEL), lambda i: (i + row0 // tm, 0)),
                  _mod_spec(1, rows_per_batch, tm), _mod_spec(0, rows_per_batch, tm),
                  _const_spec((1, D_MODEL)),
                  pl.BlockSpec((D_MODEL, _IN_TOTAL), lambda i: (0, 0), pipeline_mode=pl.Buffered(1))],
        out_specs=[row(w) for _, w in _IN_BLOCKS],
        out_shape=[jax.ShapeDtypeStruct((t, w), F32 if name in _IN_F32 else BF16) for name, w in _IN_BLOCKS],
        compiler_params=pltpu.CompilerParams(vmem_limit_bytes=VMEM_LIMIT),
        name="inproj",
    )(x2d, mod, mod, norm_g.reshape(1, -1), w_cat)


def _mixer_body(C, Lv, start,
                qkv_ref, ba_ref, z_ref, pin_ref, ga_ref, gb_ref, cbuf_ref, s0_ref, pbuf_ref,
                convw_ref, alog_ref, dtb_ref, dng_ref, wpool_ref, pscale_ref,
                mixed_ref, nconv_ref, ns_ref, npool_ref,
                xp_scr, act_scr, s_scr, pp_scr, odn_scr):
    n = pl.program_id(1)
    last = pl.num_programs(1) - 1

    @pl.when(n == 0)
    def _load_state():
        xp_scr[0:CONV_PAD, :] = cbuf_ref[0]
        pp_scr[0:POOL_PAD, :] = pbuf_ref[0]
        s_scr[...] = s0_ref[0]

    xp_scr[CONV_PAD:CONV_PAD + C, :] = qkv_ref[0].astype(F32)
    for c0 in range(0, QKV_WIDTH, _IN_COL_CHUNK):
        cs = slice(c0, c0 + _IN_COL_CHUNK)
        y = xp_scr[CONV_PAD:CONV_PAD + C, cs] * convw_ref[CONV_WIDTH - 1:CONV_WIDTH, cs]
        for k in range(CONV_WIDTH - 1):
            r0 = CONV_PAD - (CONV_WIDTH - 1) + k
            y = y + xp_scr[r0:r0 + C, cs] * convw_ref[k:k + 1, cs]
        act_scr[:, cs] = _silu(y)

    ba = ba_ref[0]
    lane = lax.broadcasted_iota(I32, (C, LANES), 1)
    beta_all = jax.nn.sigmoid(ba)
    g_all = -jnp.exp(alog_ref[...]) * _softplus(ba + dtb_ref[...])
    if Lv < C:
        valid = lax.broadcasted_iota(I32, (C, LANES), 0) < Lv
        beta_all = jnp.where(valid, beta_all, 0.0)
        g_all = jnp.where(valid, g_all, 0.0)
    ii = lax.broadcasted_iota(I32, (C, C), 0)
    jj = lax.broadcasted_iota(I32, (C, C), 1)
    causal = ii >= jj
    strict = ii > jj
    tril = jnp.where(causal, 1.0, 0.0).astype(BF16)
    eye = jnp.where(ii == jj, 1.0, 0.0)
    gc_all = sum(jnp.dot(tril, part, preferred_element_type=F32) for part in _split3(g_all))
    if C < LANES:
        gc_sq = jnp.concatenate([gc_all, jnp.zeros((LANES - C, LANES), F32)], axis=0)
    else:
        gc_sq = gc_all
    gc_t = gc_sq.T

    H = range(DN_HEADS)
    hsl = [slice(h * DN_HEAD_DIM, (h + 1) * DN_HEAD_DIM) for h in H]
    beta = [jnp.sum(jnp.where(lane == h, beta_all, 0.0), axis=1, keepdims=True) for h in H]
    gcol = [jnp.sum(jnp.where(lane == DN_HEADS + h, gc_all, 0.0), axis=1, keepdims=True) for h in H]
    grow = [gc_t[DN_HEADS + h:DN_HEADS + h + 1, 0:C] for h in H]
    glast = [g[C - 1:C, :] for g in gcol]
    q = [act_scr[:, hsl[h]] for h in H]
    k = [act_scr[:, DN_WIDTH + h * DN_HEAD_DIM:DN_WIDTH + (h + 1) * DN_HEAD_DIM] for h in H]
    v = [act_scr[:, 2 * DN_WIDTH + h * DN_HEAD_DIM:2 * DN_WIDTH + (h + 1) * DN_HEAD_DIM] for h in H]
    q = [x * lax.rsqrt(jnp.sum(x * x, axis=-1, keepdims=True) + EPS) * (DN_HEAD_DIM ** -0.5) for x in q]
    k = [x * lax.rsqrt(jnp.sum(x * x, axis=-1, keepdims=True) + EPS) for x in k]
    kb = [k[h] * beta[h] for h in H]
    vb = [v[h] * beta[h] for h in H]
    decay = [jnp.where(causal, jnp.exp(jnp.where(causal, gcol[h] - grow[h], 0.0)), 0.0) for h in H]
    lower = [jnp.where(strict, _dot_nt(kb[h], k[h]) * decay[h], 0.0) for h in H]
    ainv = [eye - x for x in lower]
    pw = lower
    p = 1
    while 2 * p < C:
        pw = [_dot(x, x) for x in pw]
        ainv = [ainv[h] + _dot(ainv[h], pw[h]) for h in H]
        p *= 2
    sol = [_dot(ainv[h], jnp.concatenate([vb[h], kb[h] * jnp.exp(gcol[h])], axis=1)) for h in H]
    qk = [_dot_nt(q[h], k[h]) * decay[h] for h in H]
    k_tail = [k[h] * jnp.exp(glast[h] - gcol[h]) for h in H]
    S = [s_scr[h] for h in H]
    v_new = [sol[h][:, :DN_HEAD_DIM] - _dot(sol[h][:, DN_HEAD_DIM:], S[h]) for h in H]
    o = [_dot(q[h] * jnp.exp(gcol[h]), S[h]) + _dot(qk[h], v_new[h]) for h in H]
    for h in H:
        s_scr[h] = S[h] * jnp.exp(glast[h]) + lax.dot_general(
            k_tail[h].astype(BF16), v_new[h].astype(BF16), TN_DIMS, preferred_element_type=F32)
    for h in H:
        zf = z_ref[0, :, hsl[h]].astype(F32)
        odn_scr[:, hsl[h]] = (o[h] * lax.rsqrt(jnp.mean(o[h] * o[h], axis=-1, keepdims=True) + EPS)
                              * dng_ref[...] * _silu(zf))

    pp_scr[POOL_PAD:POOL_PAD + C, :] = pin_ref[0].astype(F32)
    pos = start + n * C + lax.broadcasted_iota(I32, (C, 1), 0)
    for gi, win in enumerate(POOL_WINDOWS):
        gs = slice(gi * POOL_GROUP_DIM, (gi + 1) * POOL_GROUP_DIM)
        xg = pp_scr[POOL_PAD:POOL_PAD + C, gs]
        ssum = xg
        for sft in range(1, win):
            ssum = ssum + pp_scr[POOL_PAD - sft:POOL_PAD - sft + C, gs]
        cnt = jnp.minimum(pos + 1, win).astype(F32)
        pooled = ssum / cnt - xg
        os_ = slice(gi * POOL_OUT_GROUP, (gi + 1) * POOL_OUT_GROUP)
        yp = _dot(pooled, wpool_ref[gi]) * pscale_ref[:, os_]
        mixed_ref[0, :, os_] = (jax.nn.sigmoid(ga_ref[0, :, os_].astype(F32)) * odn_scr[:, os_]
                                + jax.nn.sigmoid(gb_ref[0, :, os_].astype(F32)) * yp).astype(BF16)

    @pl.when(n == last)
    def _store_state():
        nconv_ref[0] = xp_scr[Lv + CONV_PAD - (CONV_WIDTH - 1):Lv + CONV_PAD, :]
        npool_ref[0] = pp_scr[Lv + POOL_PAD - POOL_BUF:Lv + POOL_PAD, :]
        ns_ref[0] = s_scr[...]

    xp_scr[0:CONV_PAD, :] = xp_scr[C:C + CONV_PAD, :]
    pp_scr[0:POOL_PAD, :] = pp_scr[C:C + POOL_PAD, :]


def _mixer(proj, conv_buf, s0, pool_buf, start, seq_len, C,
           conv_w, a_log, dt_bias, dn_norm_g, w_pool, pool_scale):
    b, lp, _ = proj["qkv"].shape
    nchunks = lp // C
    lv = seq_len - (nchunks - 1) * C
    cbuf = jnp.pad(conv_buf, ((0, 0), (CONV_PAD - (CONV_WIDTH - 1), 0), (0, 0)))
    pbuf = jnp.pad(pool_buf, ((0, 0), (POOL_PAD - POOL_BUF, 0), (0, 0)))
    lane_pad = lambda a: jnp.pad(a.reshape(1, -1), ((0, 0), (DN_HEADS, LANES - 2 * DN_HEADS)))
    chunk = lambda w: pl.BlockSpec((1, C, w), lambda i, j: (i, j, 0))
    state = lambda *s: pl.BlockSpec((1,) + s, lambda i, j: (i,) + (0,) * len(s))
    return pl.pallas_call(
        functools.partial(_mixer_body, C, lv, start),
        grid=(b, nchunks),
        in_specs=[chunk(QKV_WIDTH), chunk(LANES), chunk(DN_WIDTH), chunk(POOL_WIDTH),
                  chunk(D_MODEL), chunk(D_MODEL),
                  state(CONV_PAD, QKV_WIDTH), state(DN_HEADS, DN_HEAD_DIM, DN_HEAD_DIM),
                  state(POOL_PAD, POOL_WIDTH),
                  _const_spec((CONV_WIDTH, QKV_WIDTH)), _const_spec((1, LANES)), _const_spec((1, LANES)),
                  _const_spec((1, DN_HEAD_DIM)),
                  _const_spec((len(POOL_WINDOWS), POOL_GROUP_DIM, POOL_OUT_GROUP)),
                  _const_spec((1, D_MODEL))],
        out_specs=[chunk(D_MODEL), state(CONV_WIDTH - 1, QKV_WIDTH),
                   state(DN_HEADS, DN_HEAD_DIM, DN_HEAD_DIM), state(POOL_BUF, POOL_WIDTH)],
        out_shape=[jax.ShapeDtypeStruct((b, lp, D_MODEL), BF16),
                   jax.ShapeDtypeStruct((b, CONV_WIDTH - 1, QKV_WIDTH), F32),
                   jax.ShapeDtypeStruct((b, DN_HEADS, DN_HEAD_DIM, DN_HEAD_DIM), F32),
                   jax.ShapeDtypeStruct((b, POOL_BUF, POOL_WIDTH), F32)],
        scratch_shapes=[pltpu.VMEM((CONV_PAD + C + CONV_PAD, QKV_WIDTH), F32),
                        pltpu.VMEM((C, QKV_WIDTH), F32),
                        pltpu.VMEM((DN_HEADS, DN_HEAD_DIM, DN_HEAD_DIM), F32),
                        pltpu.VMEM((POOL_PAD + C + POOL_PAD, POOL_WIDTH), F32),
                        pltpu.VMEM((C, DN_WIDTH), F32)],
        compiler_params=pltpu.CompilerParams(dimension_semantics=("arbitrary", "arbitrary"),
                                             vmem_limit_bytes=VMEM_LIMIT),
        name="mixer",
    )(proj["qkv"], proj["ba"], proj["z"], proj["pool"], proj["ga"], proj["gb"], cbuf, s0, pbuf,
      conv_w, lane_pad(a_log), lane_pad(dt_bias), dn_norm_g.reshape(1, -1), w_pool,
      pool_scale.reshape(1, -1))


def _top16(s, ids, payload=None):
    big = float(2 ** 24)
    vals, sel, pays = [], [], []
    for _ in range(PEER_TOPK):
        m = jnp.max(s, axis=0, keepdims=True)
        am = jnp.min(jnp.where(s == m, ids, big), axis=0, keepdims=True)
        hit = ids == am
        if payload is not None:
            pays.append(jnp.max(jnp.where(hit, payload, -1.0), axis=0, keepdims=True))
        s = jnp.where(hit, -jnp.inf, s)
        vals.append(m)
        sel.append(am)
    out = (jnp.concatenate(vals, axis=0), jnp.concatenate(sel, axis=0))
    if payload is not None:
        out += (jnp.concatenate(pays, axis=0),)
    return out


_CAND_EDGE = 4


def _post_body(has_prev, has_fin, mixed_ref, x_ref, g1_ref, sc2_ref, sh2_ref, n2g_ref, wout_ref,
               wq_ref, keys_ref, *refs):
    refs = list(refs)
    prev_in = [refs.pop(0) for _ in range(2 if has_prev else 0)]
    fin_in = [refs.pop(0) for _ in range(4 if has_fin else 0)]
    x1_ref, h2_ref, idx_ref, gate_ref = refs[:4]
    extra_out = refs[4:]
    if has_prev:
        pre_ref, pgate_ref = prev_in
        extra_out.pop(0)[...] = _coef_words(pre_ref[...], pgate_ref[...])
    if has_fin:
        _final_body(*fin_in, extra_out.pop(0))
    tm = x_ref.shape[0]
    x1 = x_ref[...] + _mod_rows(g1_ref) * _dot(mixed_ref[...], wout_ref[...])
    x1_ref[...] = x1
    y = x1 * lax.rsqrt(jnp.mean(x1 * x1, axis=-1, keepdims=True) + EPS) * n2g_ref[...]
    h2 = y * (1.0 + _mod_rows(sc2_ref)) + _mod_rows(sh2_ref)
    h2_ref[...] = _pack_words(h2[:, :PACK_HALF], h2[:, PACK_HALF:])
    q = _dot(h2, wq_ref[...])

    K = PEER_TOPK
    key_id = lax.broadcasted_iota(I32, (PEER_NKEYS, 1), 0).astype(F32)
    r16 = lax.broadcasted_iota(I32, (K, 1), 0)
    cand_id = jnp.concatenate([(a * K + r16) for a in range(_CAND_EDGE)]
                              + [(r16 * K + b) for b in range(_CAND_EDGE)], axis=0).astype(F32)
    dup = r16 < _CAND_EDGE
    idx_rows, gate_rows = [], []
    for h in range(PEER_HEADS):
        half = []
        for p in range(2):
            c0 = (h * 2 + p) * PEER_KEY_HALF
            st = _dot_nt(keys_ref[h * 2 + p], q[:, c0:c0 + PEER_KEY_HALF])
            half.append(_top16(st, key_id))
        (s1, i1), (s2, i2) = half
        cand = jnp.concatenate(
            [s1[a:a + 1] + s2 for a in range(_CAND_EDGE)]
            + [jnp.where(dup, -jnp.inf, s1 + s2[b:b + 1]) for b in range(_CAND_EDGE)], axis=0)
        cidx = jnp.concatenate(
            [i1[a:a + 1] * PEER_NKEYS + i2 for a in range(_CAND_EDGE)]
            + [i1 * PEER_NKEYS + i2[b:b + 1] for b in range(_CAND_EDGE)], axis=0)
        best, _, eidx = _top16(cand, cand_id, cidx)
        e = jnp.exp(best - best[0:1])
        gate_rows.append(e / jnp.sum(e, axis=0, keepdims=True))
        idx_rows.append(eidx)
    idx_ref[...] = jnp.concatenate(idx_rows, axis=0).T.astype(I32)
    gate_ref[...] = jnp.concatenate(gate_rows, axis=0).T


def _post(mixed2d, x2d, row0, mod, rows_per_batch, norm2_g, w_out, w_query, keys, tm, prev=None, fin=None):
    t = mixed2d.shape[0]
    steps = t // tm
    row = lambda w: pl.BlockSpec((tm, w), lambda i: (i, 0))
    in_specs = [row(D_MODEL), pl.BlockSpec((tm, D_MODEL), lambda i: (i + row0 // tm, 0)),
                _mod_spec(2, rows_per_batch, tm), _mod_spec(4, rows_per_batch, tm),
                _mod_spec(3, rows_per_batch, tm), _const_spec((1, D_MODEL)),
                _const_spec((D_MODEL, D_MODEL)), _const_spec((D_MODEL, 2 * PEER_HEADS * PEER_KEY_HALF)),
                _const_spec((2 * PEER_HEADS, PEER_NKEYS, PEER_KEY_HALF))]
    out_specs = [row(D_MODEL), row(PACK_HALF), row(PEER_HK), row(PEER_HK)]
    out_shape = [jax.ShapeDtypeStruct((t, D_MODEL), F32), jax.ShapeDtypeStruct((t, PACK_HALF), I32),
                 jax.ShapeDtypeStruct((t, PEER_HK), I32), jax.ShapeDtypeStruct((t, PEER_HK), F32)]
    args = [mixed2d, x2d, mod, mod, mod, norm2_g.reshape(1, -1), w_out, w_query, keys]
    if prev is not None:
        tp = prev[0].shape[0]
        prow = pl.BlockSpec((tp // steps, PEER_HK), lambda i: (i, 0))
        in_specs += [prow, prow]
        out_specs += [prow]
        out_shape += [jax.ShapeDtypeStruct((tp, PEER_HK), I32)]
        args += list(prev)
    if fin is not None:
        x1_f, peer_f, mod_f, rows_f, final_g = fin
        tf = x1_f.shape[0]
        frow = pl.BlockSpec((tf // steps, D_MODEL), lambda i: (i, 0))
        in_specs += [frow, frow, _mod_spec(5, rows_f, tf // steps), _const_spec((1, D_MODEL))]
        out_specs += [frow]
        out_shape += [jax.ShapeDtypeStruct((tf, D_MODEL), F32)]
        args += [x1_f, peer_f, mod_f, final_g.reshape(1, -1)]
    return pl.pallas_call(
        functools.partial(_post_body, prev is not None, fin is not None),
        grid=(steps,),
        in_specs=in_specs, out_specs=out_specs, out_shape=out_shape,
        compiler_params=pltpu.CompilerParams(vmem_limit_bytes=VMEM_LIMIT),
        name="post",
    )(*args)


SC_CORES = 2
SC_SUBCORES = 16
SC_LANES = 16
SC_WORKERS = SC_CORES * SC_SUBCORES
SC_TOKENS = 32
SC_SLOTS = 4
SC_JOB_HEADS = 2
SC_BF16_GROUP = 4
PACK_HALF = D_MODEL // 2
SC_CHUNKS = PACK_HALF // SC_LANES
PROMPT_PARTS = 8
EDGE_SPLITS = (4, 2)
RAMP_PARTS = 2
COEF_LAG = 2
FIN_LAG = 3
ROW_TILE = 256


def _bf16_bits(v):
    return lax.bitcast_convert_type(v.astype(BF16).astype(F32), jnp.uint32)


def _pack_words(lo, hi):
    return lax.bitcast_convert_type((_bf16_bits(lo) >> 16) | _bf16_bits(hi), I32)


def _pack_body(x_ref, o_ref):
    o_ref[...] = _pack_words(x_ref[:, :PACK_HALF], x_ref[:, PACK_HALF:])


def _pack_table(tbl, rows=2 * ROW_TILE):
    e = tbl.shape[0]
    return pl.pallas_call(
        _pack_body, grid=(e // rows,),
        in_specs=[pl.BlockSpec((rows, D_MODEL), lambda i: (i, 0))],
        out_specs=pl.BlockSpec((rows, PACK_HALF), lambda i: (i, 0)),
        out_shape=jax.ShapeDtypeStruct((e, PACK_HALF), I32), name="pack_table")(tbl)


def _tree_sum(terms):
    terms = list(terms)
    while len(terms) > 1:
        terms = [a + b for a, b in zip(terms[0::2], terms[1::2])] + terms[len(terms) & ~1:]
    return terms[0]


def _unpack_pair(w):
    lo = plsc.bitcast(lax.shift_left(w, jnp.full(w.shape, 16, I32)), F32)
    hi = plsc.bitcast(w & jnp.full(w.shape, -65536, I32), F32)
    return lo, hi


def _sc_mesh():
    return plsc.VectorSubcoreMesh(core_axis_name="c", subcore_axis_name="s")


def _sc_worker():
    return lax.axis_index("s") * SC_CORES + lax.axis_index("c")


def _sc_jobs(table_hbm, idx_v, buf, sem, compute):
    per_tok = PEER_HEADS // SC_JOB_HEADS
    njobs = idx_v.shape[0] * per_tok
    nrows = SC_JOB_HEADS * PEER_TOPK

    def copy(j, slot):
        rows = idx_v.at[j // per_tok, pl.ds((j % per_tok) * nrows, nrows)]
        return pltpu.make_async_copy(table_hbm.at[rows], buf.at[slot], sem.at[slot])

    for s in range(SC_SLOTS):
        copy(s, s).start()

    def job(j, c):
        s = j % SC_SLOTS
        copy(j, s).wait()

        def head(i, cc):
            compute(j // per_tok, (j % per_tok) * SC_JOB_HEADS + i, s, i * PEER_TOPK)
            return cc
        lax.fori_loop(0, SC_JOB_HEADS, head, 0)

        @pl.when(j + SC_SLOTS < njobs)
        def _next():
            copy(j + SC_SLOTS, s).start()
        return c

    lax.fori_loop(0, njobs, job, 0)


def _peer_u_body(n_tok, idx_hbm, h2_hbm, u_hbm, pre_hbm, idx_v, h2_v, pre_v, ubuf, acc_v, sem):
    base = _sc_worker() * n_tok
    lane = lax.iota(I32, SC_LANES)

    def compute(tt, h, slot, r0):
        def chunk(cg, accs):
            cs = [pl.ds((cg * SC_BF16_GROUP + i) * SC_LANES, SC_LANES) for i in range(SC_BF16_GROUP)]
            xs = [plsc.bitcast(h2_v[tt, c], BF16) for c in cs]
            out = []
            for k, a in enumerate(accs):
                part = _tree_sum([plsc.bitcast(ubuf[slot, r0 + k, c], BF16) * x for c, x in zip(cs, xs)])
                lo, hi = _unpack_pair(plsc.bitcast(part, I32))
                out.append(a + (lo + hi))
            return tuple(out)
        zero = jnp.zeros((SC_LANES,), F32)
        accs = lax.fori_loop(0, SC_CHUNKS // SC_BF16_GROUP, chunk, (zero,) * PEER_TOPK)
        for k, a in enumerate(accs):
            acc_v[k, :] = a
        tot = zero
        for j in range(SC_LANES):
            tot = tot + plsc.load_gather(acc_v, [lane, (lane + j) & (SC_LANES - 1)])
        pre_v[tt, pl.ds(h * PEER_TOPK, PEER_TOPK)] = tot

    tb = idx_v.shape[0]

    def block(bi, c):
        t0 = base + bi * tb
        pltpu.sync_copy(idx_hbm.at[pl.ds(t0, tb)], idx_v)
        pltpu.sync_copy(h2_hbm.at[pl.ds(t0, tb)], h2_v)
        _sc_jobs(u_hbm, idx_v, ubuf, sem, compute)
        pltpu.sync_copy(pre_v, pre_hbm.at[pl.ds(t0, tb)])
        return c

    lax.fori_loop(0, n_tok // tb, block, 0)


def _peer_v_body(n_tok, idx_hbm, coef_hbm, v_hbm, out_hbm, idx_v, coef_v, out_v, vbuf, sem):
    base = _sc_worker() * n_tok
    zero = jnp.zeros((SC_LANES,), F32)

    def compute(tt, h, slot, r0):
        cvec = coef_v[tt, pl.ds(h * PEER_TOPK, PEER_TOPK)]
        cb = [plsc.bitcast(jnp.take_along_axis(cvec, jnp.full((SC_LANES,), k, I32), axis=0), BF16)
              for k in range(PEER_TOPK)]

        @plsc.parallel_loop(0, SC_CHUNKS, unroll=2)
        def _chunk(c):
            cs = pl.ds(c * SC_LANES, SC_LANES)
            prods = [plsc.bitcast(vbuf[slot, r0 + k, cs], BF16) * cb[k] for k in range(PEER_TOPK)]
            pairs = [_unpack_pair(plsc.bitcast(_tree_sum(prods[g:g + SC_BF16_GROUP]), I32))
                     for g in range(0, PEER_TOPK, SC_BF16_GROUP)]
            for half, off in ((0, 0), (1, PACK_HALF)):
                plsc.addupdate(out_v.at[tt, pl.ds(off + c * SC_LANES, SC_LANES)],
                               _tree_sum([p[half] for p in pairs]))

    tb = idx_v.shape[0]

    def block(bi, c):
        t0 = base + bi * tb
        pltpu.sync_copy(idx_hbm.at[pl.ds(t0, tb)], idx_v)
        pltpu.sync_copy(coef_hbm.at[pl.ds(t0, tb)], coef_v)

        def clear(i, cc):
            per_row = D_MODEL // SC_LANES
            out_v[i // per_row, pl.ds((i % per_row) * SC_LANES, SC_LANES)] = zero
            return cc
        lax.fori_loop(0, tb * (D_MODEL // SC_LANES), clear, 0)
        _sc_jobs(v_hbm, idx_v, vbuf, sem, compute)
        pltpu.sync_copy(out_v, out_hbm.at[pl.ds(t0, tb)])
        return c

    lax.fori_loop(0, n_tok // tb, block, 0)


def _peer_sc(body, idx, rows, table, out_width, name):
    t = idx.shape[0]
    assert t % SC_WORKERS == 0
    n_tok = t // SC_WORKERS
    tb = min(SC_TOKENS * (2 if body is _peer_u_body else 1), n_tok)
    assert n_tok % tb == 0 and tb * PEER_HEADS // SC_JOB_HEADS >= SC_SLOTS
    return pl.kernel(
        functools.partial(body, n_tok),
        out_type=jax.ShapeDtypeStruct((t, out_width), F32),
        mesh=_sc_mesh(),
        scratch_types=[pltpu.VMEM((tb, PEER_HK), I32),
                       pltpu.VMEM((tb, rows.shape[1]), rows.dtype),
                       pltpu.VMEM((tb, out_width), F32),
                       pltpu.VMEM((SC_SLOTS, SC_JOB_HEADS * PEER_TOPK, PACK_HALF), I32)]
                      + ([pltpu.VMEM((PEER_TOPK, SC_LANES), F32)] if body is _peer_u_body else [])
                      + [pltpu.SemaphoreType.DMA((SC_SLOTS,))],
        compiler_params=pltpu.CompilerParams(needs_layout_passes=False),
        name=name,
    )(idx, rows, table)


def _coef_words(pre, gates):
    return _pack_words(*(gates * _gelu(pre),) * 2)


def _coef_body(pre_ref, gate_ref, coef_ref):
    coef_ref[...] = _coef_words(pre_ref[...], gate_ref[...])


def _coef(pre, gates, tm):
    t = pre.shape[0]
    row = pl.BlockSpec((tm, PEER_HK), lambda i: (i, 0))
    return pl.pallas_call(_coef_body, grid=(t // tm,), in_specs=[row, row], out_specs=row,
                          out_shape=jax.ShapeDtypeStruct((t, PEER_HK), I32), name="coef")(pre, gates)


def _final_body(x1_ref, peer_ref, g2_ref, fng_ref, y_ref):
    x2 = x1_ref[...] + _mod_rows(g2_ref) * peer_ref[...]
    y_ref[...] = x2 * lax.rsqrt(jnp.mean(x2 * x2, axis=-1, keepdims=True) + EPS) * fng_ref[...]


def _final(x1, peer_out, mod, rows_per_batch, final_g, tm):
    t = x1.shape[0]
    row = pl.BlockSpec((tm, D_MODEL), lambda i: (i, 0))
    return pl.pallas_call(
        _final_body, grid=(t // tm,),
        in_specs=[row, row, _mod_spec(5, rows_per_batch, tm), _const_spec((1, D_MODEL))],
        out_specs=row, out_shape=jax.ShapeDtypeStruct((t, D_MODEL), F32), name="final",
    )(x1, peer_out, mod, final_g.reshape(1, -1))


def _expert_gather_v(g, coef, expert_v):
    g["peer_out"] = _peer_sc(_peer_v_body, g["idx"], coef, expert_v, D_MODEL, "peer_v")


def _front(x, mod, conv_buf, s0, pool_buf, start, chunk, tm, wts, prev, fin):
    x2d, row0, b, l = x
    t = b * l
    assert row0 % tm == 0
    if l >= tm:
        modx = mod.reshape(b, 6, 1, D_MODEL).transpose(1, 0, 2, 3)
    else:
        modx = jnp.repeat(mod.reshape(b, 6, D_MODEL), l, axis=0).transpose(1, 0, 2)
    outs = _inproj(x2d, row0, t, modx, l, wts["norm1_g"], wts["w_cat"], tm)
    lp = -(-l // chunk) * chunk
    proj = {}
    for (name, w), a in zip(_IN_BLOCKS, outs):
        a = a.reshape(b, l, w)
        proj[name] = a if lp == l else jnp.pad(a, ((0, 0), (0, lp - l), (0, 0)))
    mixed, nconv, ns, npool = _mixer(proj, conv_buf, s0, pool_buf, start, l, chunk,
                                     wts["conv_w"], wts["a_log"], wts["dt_bias"], wts["dn_norm_g"],
                                     wts["w_pool"], wts["pool_scale"])
    mixed2d = mixed[:, :l].reshape(t, D_MODEL)
    res = _post(mixed2d, x2d, row0, modx, l, wts["norm2_g"], wts["w_out"], wts["w_query"], wts["keys"], tm,
                prev=None if prev is None else (prev["pre"], prev["gates"]),
                fin=None if fin is None else (fin["x1"], fin["peer_out"], fin["mod"], fin["l"],
                                              wts["final_norm_g"]))
    x1, h2, idx, gates = res[:4]
    extra = list(res[4:])
    coef_prev = extra.pop(0) if prev is not None else None
    y_fin = extra.pop(0).reshape(fin["b"], fin["l"], D_MODEL) if fin is not None else None
    pre = _peer_sc(_peer_u_body, idx, h2, wts["expert_u"], PEER_HK, "peer_u")
    g = dict(x1=x1, idx=idx, gates=gates, pre=pre, mod=modx, b=b, l=l, tm=tm,
             states=(nconv, ns, npool))
    return g, coef_prev, y_fin


def kernel(x_prompt, x_sample, c_prompt, c_sample, state_conv, state_delta, state_pool, w_ada, b_ada, norm1_g, w_in, conv_w, a_log, dt_bias, dn_norm_g, w_pool, pool_scale, w_out, norm2_g, w_query, sub_keys, expert_u, expert_v, final_norm_g):
    bp = x_prompt.shape[0]
    yp, ys = x_prompt, x_sample
    conv_p, delta_p, pool_p, conv_s, delta_s, pool_s = [], [], [], [], [], []
    zero_conv = jnp.zeros((bp, CONV_WIDTH - 1, QKV_WIDTH), F32)
    zero_delta = jnp.zeros((bp, DN_HEADS, DN_HEAD_DIM, DN_HEAD_DIM), F32)
    zero_pool = jnp.zeros((bp, POOL_BUF, POOL_WIDTH), F32)
    c_all = jnp.concatenate([c_prompt, c_sample], axis=0)
    for layer in range(DEPTH):
        wi = w_in[layer]
        o_b = QKV_WIDTH
        o_z = o_b + 2 * DN_HEADS
        w_ba = jnp.pad(wi[:, o_b:o_z], ((0, 0), (0, LANES - 2 * DN_HEADS)))
        w_cat = jnp.concatenate([wi[:, :o_b], wi[:, o_z:], w_ba], axis=1).astype(BF16)
        last = layer == DEPTH - 1
        wts = dict(
            norm1_g=norm1_g[layer], w_cat=w_cat, conv_w=conv_w[layer], a_log=a_log[layer],
            dt_bias=dt_bias[layer], dn_norm_g=dn_norm_g[layer], w_pool=w_pool[layer],
            pool_scale=pool_scale[layer], w_out=w_out[layer].astype(BF16), norm2_g=norm2_g[layer],
            w_query=w_query[layer].astype(BF16),
            keys=sub_keys[layer].reshape(2 * PEER_HEADS, PEER_NKEYS, PEER_KEY_HALF).astype(BF16),
            expert_u=_pack_table(expert_u[layer]), expert_v=_pack_table(expert_v[layer]),
            final_norm_g=final_norm_g if last else jnp.ones_like(final_norm_g))
        mod = _ada(c_all, w_ada[layer], b_ada[layer])
        assert last, "final norm is fused into the expert stage"
        step = bp // PROMPT_PARTS
        seq = x_prompt.shape[1]
        assert step == 1
        xp2d = yp.reshape(bp * seq, D_MODEL)
        zeros = (zero_conv[:step], zero_delta[:step], zero_pool[:step])
        jobs, cuts = [], []
        for b0 in range(0, bp, step):
            n = EDGE_SPLITS[0] if b0 == 0 else EDGE_SPLITS[1] if b0 == bp - step else 1
            cuts.append(n)
            for s0 in range(0, seq, seq // n):
                jobs.append(((xp2d, b0 * seq + s0, step, seq // n), mod[b0:b0 + step],
                             zeros if s0 == 0 else None, s0, DN_CHUNK))
        jobs.append(((ys.reshape(-1, D_MODEL), 0) + ys.shape[:2], mod[bp:], (state_conv[layer], state_delta[layer], state_pool[layer]),
                     PAST_LEN, SUBLANES))
        groups = []
        for j, (xg, mg, states, start, chunk) in enumerate(jobs):
            pi = j - (1 if j <= RAMP_PARTS else COEF_LAG)
            prev = groups[pi] if pi >= 0 and "peer_out" not in groups[pi] else None
            fin = groups[j - FIN_LAG] if j >= FIN_LAG and "peer_out" in groups[j - FIN_LAG] else None
            if fin is not None and fin["x1"].shape[0] % (xg[2] * xg[3] // ROW_TILE):
                fin = None
            if states is None:
                states = groups[j - 1]["states"]
            g, coef_prev, y_fin = _front(xg, mg, *states, start, chunk, ROW_TILE, wts, prev, fin)
            if prev is not None:
                _expert_gather_v(prev, coef_prev, wts["expert_v"])
            if fin is not None:
                fin["y"] = y_fin
            groups.append(g)
        for g in groups:
            if "peer_out" not in g:
                _expert_gather_v(g, _coef(g["pre"], g["gates"], ROW_TILE), wts["expert_v"])
        for g in groups:
            if "y" not in g:
                g["y"] = _final(g["x1"], g["peer_out"], g["mod"], g["l"], wts["final_norm_g"],
                                g["tm"]).reshape(g["b"], g["l"], D_MODEL)
        rows, at = [], 0
        for n in cuts:
            rows.append(groups[at:at + n])
            at += n
        yp = jnp.concatenate([jnp.concatenate([g["y"] for g in row], axis=1) for row in rows], axis=0)
        cp, sp, pp = (jnp.concatenate(a, axis=0) for a in zip(*(row[-1]["states"] for row in rows)))
        ys = groups[-1]["y"]
        cs, ss, ps = groups[-1]["states"]
        conv_p.append(cp)
        delta_p.append(sp)
        pool_p.append(pp)
        conv_s.append(cs)
        delta_s.append(ss)
        pool_s.append(ps)
    return (yp, ys, jnp.stack(conv_p), jnp.stack(delta_p), jnp.stack(pool_p),
            jnp.stack(conv_s), jnp.stack(delta_s), jnp.stack(pool_s))
```

```python
import functools

import jax
import jax.numpy as jnp
from jax import lax
from jax.experimental import pallas as pl
from jax.experimental.pallas import tpu as pltpu
from jax.experimental.pallas import tpu_sc as plsc

F32 = jnp.float32
BF16 = jnp.bfloat16
I32 = jnp.int32

D_MODEL = 1024
DEPTH = 1
PAST_LEN = 16384
DN_HEADS = 8
DN_HEAD_DIM = 128
DN_WIDTH = DN_HEADS * DN_HEAD_DIM
QKV_WIDTH = 3 * DN_WIDTH
CONV_WIDTH = 4
DN_CHUNK = 64
POOL_WINDOWS = (2, 4, 8, 16)
POOL_GROUP_DIM = 128
POOL_WIDTH = len(POOL_WINDOWS) * POOL_GROUP_DIM
POOL_OUT_GROUP = D_MODEL // len(POOL_WINDOWS)
POOL_BUF = max(POOL_WINDOWS) - 1
PEER_HEADS = 8
PEER_NKEYS = 128
PEER_TOPK = 16
PEER_KEY_HALF = 128
PEER_HK = PEER_HEADS * PEER_TOPK
EPS = 1e-6

LANES = 128
SUBLANES = 8
CONV_PAD = SUBLANES
POOL_PAD = 16
VMEM_LIMIT = 56 * 1024 * 1024

NT_DIMS = (((1,), (1,)), ((), ()))
TN_DIMS = (((0,), (0,)), ((), ()))


def _dot(a, b):
    return jnp.dot(a.astype(BF16), b.astype(BF16), preferred_element_type=F32)


def _dot_nt(a, b):
    return lax.dot_general(a.astype(BF16), b.astype(BF16), NT_DIMS, preferred_element_type=F32)


def _split3(x):
    hi = x.astype(BF16)
    r1 = x - hi.astype(F32)
    mid = r1.astype(BF16)
    lo = (r1 - mid.astype(F32)).astype(BF16)
    return hi, mid, lo


def _silu(x):
    return x * jax.nn.sigmoid(x)


def _gelu(x):
    return 0.5 * x * (1.0 + lax.erf(x * (0.5 ** 0.5)))


def _softplus(x):
    return jnp.maximum(x, 0.0) + jnp.log(1.0 + jnp.exp(-jnp.abs(x)))


def _mod_rows(ref):
    m = ref[...]
    return m.reshape(m.shape[-2], m.shape[-1])


def _mod_spec(k, rows_per_batch, tm):
    if rows_per_batch >= tm:
        tiles = rows_per_batch // tm
        return pl.BlockSpec((1, 1, 1, D_MODEL), lambda i, *_: (k, i // tiles, 0, 0))
    return pl.BlockSpec((1, tm, D_MODEL), lambda i, *_: (k, i, 0))


def _const_spec(shape):
    nd = len(shape)
    return pl.BlockSpec(shape, lambda *_: (0,) * nd)


def _ada_body(c_ref, w_ref, b_ref, o_ref):
    o_ref[...] = _dot(_silu(c_ref[...]), w_ref[...]) + b_ref[...]


def _ada(c, w_ada, b_ada):
    n = c.shape[0]
    return pl.pallas_call(
        _ada_body,
        grid=(6,),
        in_specs=[pl.BlockSpec((n, D_MODEL), lambda j: (0, 0)),
                  pl.BlockSpec((D_MODEL, D_MODEL), lambda j: (0, j)),
                  pl.BlockSpec((1, D_MODEL), lambda j: (0, j))],
        out_specs=pl.BlockSpec((n, D_MODEL), lambda j: (0, j)),
        out_shape=jax.ShapeDtypeStruct((n, 6 * D_MODEL), F32),
        name="ada",
    )(c, w_ada, b_ada.reshape(1, -1))


_IN_BLOCKS = (("qkv", QKV_WIDTH), ("z", DN_WIDTH), ("pool", POOL_WIDTH),
              ("ga", D_MODEL), ("gb", D_MODEL), ("ba", LANES))
_IN_TOTAL = sum(w for _, w in _IN_BLOCKS)
_IN_F32 = ("ba",)
_IN_COL_CHUNK = 512


def _inproj_body(x_ref, sc_ref, sh_ref, g_ref, w_ref, *out_refs):
    x = x_ref[...]
    y = x * lax.rsqrt(jnp.mean(x * x, axis=-1, keepdims=True) + EPS) * g_ref[...]
    h = (y * (1.0 + _mod_rows(sc_ref)) + _mod_rows(sh_ref)).astype(BF16)
    off = 0
    for (_, width), o_ref in zip(_IN_BLOCKS, out_refs):
        for c0 in range(0, width, _IN_COL_CHUNK):
            cw = min(_IN_COL_CHUNK, width - c0)
            o_ref[:, c0:c0 + cw] = jnp.dot(h, w_ref[:, off + c0:off + c0 + cw],
                                           preferred_element_type=F32).astype(o_ref.dtype)
        off += width


def _inproj(x2d, row0, t, mod, rows_per_batch, norm_g, w_cat, tm):
    row = lambda w: pl.BlockSpec((tm, w), lambda i: (i, 0))
    return pl.pallas_call(
        _inproj_body,
        grid=(t // tm,),
        in_specs=[pl.BlockSpec((tm, D_MODEL), lambda i: (i + row0 // tm, 0)),
                  _mod_spec(1, rows_per_batch, tm), _mod_spec(0, rows_per_batch, tm),
                  _const_spec((1, D_MODEL)),
                  pl.BlockSpec((D_MODEL, _IN_TOTAL), lambda i: (0, 0), pipeline_mode=pl.Buffered(1))],
        out_specs=[row(w) for _, w in _IN_BLOCKS],
        out_shape=[jax.ShapeDtypeStruct((t, w), F32 if name in _IN_F32 else BF16) for name, w in _IN_BLOCKS],
        compiler_params=pltpu.CompilerParams(vmem_limit_bytes=VMEM_LIMIT),
        name="inproj",
    )(x2d, mod, mod, norm_g.reshape(1, -1), w_cat)


def _mixer_body(C, Lv, start,
                qkv_ref, ba_ref, z_ref, pin_ref, ga_ref, gb_ref, cbuf_ref, s0_ref, pbuf_ref,
                convw_ref, alog_ref, dtb_ref, dng_ref, wpool_ref, pscale_ref,
                mixed_ref, nconv_ref, ns_ref, npool_ref,
                xp_scr, act_scr, s_scr, pp_scr, odn_scr):
    n = pl.program_id(1)
    last = pl.num_programs(1) - 1

    @pl.when(n == 0)
    def _load_state():
        xp_scr[0:CONV_PAD, :] = cbuf_ref[0]
        pp_scr[0:POOL_PAD, :] = pbuf_ref[0]
        s_scr[...] = s0_ref[0]

    xp_scr[CONV_PAD:CONV_PAD + C, :] = qkv_ref[0].astype(F32)
    for c0 in range(0, QKV_WIDTH, _IN_COL_CHUNK):
        cs = slice(c0, c0 + _IN_COL_CHUNK)
        y = xp_scr[CONV_PAD:CONV_PAD + C, cs] * convw_ref[CONV_WIDTH - 1:CONV_WIDTH, cs]
        for k in range(CONV_WIDTH - 1):
            r0 = CONV_PAD - (CONV_WIDTH - 1) + k
            y = y + xp_scr[r0:r0 + C, cs] * convw_ref[k:k + 1, cs]
        act_scr[:, cs] = _silu(y)

    ba = ba_ref[0]
    lane = lax.broadcasted_iota(I32, (C, LANES), 1)
    beta_all = jax.nn.sigmoid(ba)
    g_all = -jnp.exp(alog_ref[...]) * _softplus(ba + dtb_ref[...])
    if Lv < C:
        valid = lax.broadcasted_iota(I32, (C, LANES), 0) < Lv
        beta_all = jnp.where(valid, beta_all, 0.0)
        g_all = jnp.where(valid, g_all, 0.0)
    ii = lax.broadcasted_iota(I32, (C, C), 0)
    jj = lax.broadcasted_iota(I32, (C, C), 1)
    causal = ii >= jj
    strict = ii > jj
    tril = jnp.where(causal, 1.0, 0.0).astype(BF16)
    eye = jnp.where(ii == jj, 1.0, 0.0)
    gc_all = sum(jnp.dot(tril, part, preferred_element_type=F32) for part in _split3(g_all))
    if C < LANES:
        gc_sq = jnp.concatenate([gc_all, jnp.zeros((LANES - C, LANES), F32)], axis=0)
    else:
        gc_sq = gc_all
    gc_t = gc_sq.T

    H = range(DN_HEADS)
    hsl = [slice(h * DN_HEAD_DIM, (h + 1) * DN_HEAD_DIM) for h in H]
    beta = [jnp.sum(jnp.where(lane == h, beta_all, 0.0), axis=1, keepdims=True) for h in H]
    gcol = [jnp.sum(jnp.where(lane == DN_HEADS + h, gc_all, 0.0), axis=1, keepdims=True) for h in H]
    grow = [gc_t[DN_HEADS + h:DN_HEADS + h + 1, 0:C] for h in H]
    glast = [g[C - 1:C, :] for g in gcol]
    q = [act_scr[:, hsl[h]] for h in H]
    k = [act_scr[:, DN_WIDTH + h * DN_HEAD_DIM:DN_WIDTH + (h + 1) * DN_HEAD_DIM] for h in H]
    v = [act_scr[:, 2 * DN_WIDTH + h * DN_HEAD_DIM:2 * DN_WIDTH + (h + 1) * DN_HEAD_DIM] for h in H]
    q = [x * lax.rsqrt(jnp.sum(x * x, axis=-1, keepdims=True) + EPS) * (DN_HEAD_DIM ** -0.5) for x in q]
    k = [x * lax.rsqrt(jnp.sum(x * x, axis=-1, keepdims=True) + EPS) for x in k]
    kb = [k[h] * beta[h] for h in H]
    vb = [v[h] * beta[h] for h in H]
    decay = [jnp.where(causal, jnp.exp(jnp.where(causal, gcol[h] - grow[h], 0.0)), 0.0) for h in H]
    lower = [jnp.where(strict, _dot_nt(kb[h], k[h]) * decay[h], 0.0) for h in H]
    ainv = [eye - x for x in lower]
    pw = lower
    p = 1
    while 2 * p < C:
        pw = [_dot(x, x) for x in pw]
        ainv = [ainv[h] + _dot(ainv[h], pw[h]) for h in H]
        p *= 2
    sol = [_dot(ainv[h], jnp.concatenate([vb[h], kb[h] * jnp.exp(gcol[h])], axis=1)) for h in H]
    qk = [_dot_nt(q[h], k[h]) * decay[h] for h in H]
    k_tail = [k[h] * jnp.exp(glast[h] - gcol[h]) for h in H]
    S = [s_scr[h] for h in H]
    v_new = [sol[h][:, :DN_HEAD_DIM] - _dot(sol[h][:, DN_HEAD_DIM:], S[h]) for h in H]
    o = [_dot(q[h] * jnp.exp(gcol[h]), S[h]) + _dot(qk[h], v_new[h]) for h in H]
    for h in H:
        s_scr[h] = S[h] * jnp.exp(glast[h]) + lax.dot_general(
            k_tail[h].astype(BF16), v_new[h].astype(BF16), TN_DIMS, preferred_element_type=F32)
    for h in H:
        zf = z_ref[0, :, hsl[h]].astype(F32)
        odn_scr[:, hsl[h]] = (o[h] * lax.rsqrt(jnp.mean(o[h] * o[h], axis=-1, keepdims=True) + EPS)
                              * dng_ref[...] * _silu(zf))

    pp_scr[POOL_PAD:POOL_PAD + C, :] = pin_ref[0].astype(F32)
    pos = start + n * C + lax.broadcasted_iota(I32, (C, 1), 0)
    for gi, win in enumerate(POOL_WINDOWS):
        gs = slice(gi * POOL_GROUP_DIM, (gi + 1) * POOL_GROUP_DIM)
        xg = pp_scr[POOL_PAD:POOL_PAD + C, gs]
        ssum = xg
        for sft in range(1, win):
            ssum = ssum + pp_scr[POOL_PAD - sft:POOL_PAD - sft + C, gs]
        cnt = jnp.minimum(pos + 1, win).astype(F32)
        pooled = ssum / cnt - xg
        os_ = slice(gi * POOL_OUT_GROUP, (gi + 1) * POOL_OUT_GROUP)
        yp = _dot(pooled, wpool_ref[gi]) * pscale_ref[:, os_]
        mixed_ref[0, :, os_] = (jax.nn.sigmoid(ga_ref[0, :, os_].astype(F32)) * odn_scr[:, os_]
                                + jax.nn.sigmoid(gb_ref[0, :, os_].astype(F32)) * yp).astype(BF16)

    @pl.when(n == last)
    def _store_state():
        nconv_ref[0] = xp_scr[Lv + CONV_PAD - (CONV_WIDTH - 1):Lv + CONV_PAD, :]
        npool_ref[0] = pp_scr[Lv + POOL_PAD - POOL_BUF:Lv + POOL_PAD, :]
        ns_ref[0] = s_scr[...]

    xp_scr[0:CONV_PAD, :] = xp_scr[C:C + CONV_PAD, :]
    pp_scr[0:POOL_PAD, :] = pp_scr[C:C + POOL_PAD, :]


def _mixer(proj, conv_buf, s0, pool_buf, start, seq_len, C,
           conv_w, a_log, dt_bias, dn_norm_g, w_pool, pool_scale):
    b, lp, _ = proj["qkv"].shape
    nchunks = lp // C
    lv = seq_len - (nchunks - 1) * C
    cbuf = jnp.pad(conv_buf, ((0, 0), (CONV_PAD - (CONV_WIDTH - 1), 0), (0, 0)))
    pbuf = jnp.pad(pool_buf, ((0, 0), (POOL_PAD - POOL_BUF, 0), (0, 0)))
    lane_pad = lambda a: jnp.pad(a.reshape(1, -1), ((0, 0), (DN_HEADS, LANES - 2 * DN_HEADS)))
    chunk = lambda w: pl.BlockSpec((1, C, w), lambda i, j: (i, j, 0))
    state = lambda *s: pl.BlockSpec((1,) + s, lambda i, j: (i,) + (0,) * len(s))
    return pl.pallas_call(
        functools.partial(_mixer_body, C, lv, start),
        grid=(b, nchunks),
        in_specs=[chunk(QKV_WIDTH), chunk(LANES), chunk(DN_WIDTH), chunk(POOL_WIDTH),
                  chunk(D_MODEL), chunk(D_MODEL),
                  state(CONV_PAD, QKV_WIDTH), state(DN_HEADS, DN_HEAD_DIM, DN_HEAD_DIM),
                  state(POOL_PAD, POOL_WIDTH),
                  _const_spec((CONV_WIDTH, QKV_WIDTH)), _const_spec((1, LANES)), _const_spec((1, LANES)),
                  _const_spec((1, DN_HEAD_DIM)),
                  _const_spec((len(POOL_WINDOWS), POOL_GROUP_DIM, POOL_OUT_GROUP)),
                  _const_spec((1, D_MODEL))],
        out_specs=[chunk(D_MODEL), state(CONV_WIDTH - 1, QKV_WIDTH),
                   state(DN_HEADS, DN_HEAD_DIM, DN_HEAD_DIM), state(POOL_BUF, POOL_WIDTH)],
        out_shape=[jax.ShapeDtypeStruct((b, lp, D_MODEL), BF16),
                   jax.ShapeDtypeStruct((b, CONV_WIDTH - 1, QKV_WIDTH), F32),
                   jax.ShapeDtypeStruct((b, DN_HEADS, DN_HEAD_DIM, DN_HEAD_DIM), F32),
                   jax.ShapeDtypeStruct((b, POOL_BUF, POOL_WIDTH), F32)],
        scratch_shapes=[pltpu.VMEM((CONV_PAD + C + CONV_PAD, QKV_WIDTH), F32),
                        pltpu.VMEM((C, QKV_WIDTH), F32),
                        pltpu.VMEM((DN_HEADS, DN_HEAD_DIM, DN_HEAD_DIM), F32),
                        pltpu.VMEM((POOL_PAD + C + POOL_PAD, POOL_WIDTH), F32),
                        pltpu.VMEM((C, DN_WIDTH), F32)],
        compiler_params=pltpu.CompilerParams(dimension_semantics=("arbitrary", "arbitrary"),
                                             vmem_limit_bytes=VMEM_LIMIT),
        name="mixer",
    )(proj["qkv"], proj["ba"], proj["z"], proj["pool"], proj["ga"], proj["gb"], cbuf, s0, pbuf,
      conv_w, lane_pad(a_log), lane_pad(dt_bias), dn_norm_g.reshape(1, -1), w_pool,
      pool_scale.reshape(1, -1))


def _top16(s, ids, payload=None):
    big = float(2 ** 24)
    vals, sel, pays = [], [], []
    for _ in range(PEER_TOPK):
        m = jnp.max(s, axis=0, keepdims=True)
        am = jnp.min(jnp.where(s == m, ids, big), axis=0, keepdims=True)
        hit = ids == am
        if payload is not None:
            pays.append(jnp.max(jnp.where(hit, payload, -1.0), axis=0, keepdims=True))
        s = jnp.where(hit, -jnp.inf, s)
        vals.append(m)
        sel.append(am)
    out = (jnp.concatenate(vals, axis=0), jnp.concatenate(sel, axis=0))
    if payload is not None:
        out += (jnp.concatenate(pays, axis=0),)
    return out


_CAND_EDGE = 4


def _post_body(has_prev, has_fin, mixed_ref, x_ref, g1_ref, sc2_ref, sh2_ref, n2g_ref, wout_ref,
               wq_ref, keys_ref, *refs):
    refs = list(refs)
    prev_in = [refs.pop(0) for _ in range(2 if has_prev else 0)]
    fin_in = [refs.pop(0) for _ in range(4 if has_fin else 0)]
    x1_ref, h2_ref, idx_ref, gate_ref = refs[:4]
    extra_out = refs[4:]
    if has_prev:
        pre_ref, pgate_ref = prev_in
        extra_out.pop(0)[...] = _coef_words(pre_ref[...], pgate_ref[...])
    if has_fin:
        _final_body(*fin_in, extra_out.pop(0))
    tm = x_ref.shape[0]
    x1 = x_ref[...] + _mod_rows(g1_ref) * _dot(mixed_ref[...], wout_ref[...])
    x1_ref[...] = x1
    y = x1 * lax.rsqrt(jnp.mean(x1 * x1, axis=-1, keepdims=True) + EPS) * n2g_ref[...]
    h2 = y * (1.0 + _mod_rows(sc2_ref)) + _mod_rows(sh2_ref)
    h2_ref[...] = _pack_words(h2[:, :PACK_HALF], h2[:, PACK_HALF:])
    q = _dot(h2, wq_ref[...])

    K = PEER_TOPK
    key_id = lax.broadcasted_iota(I32, (PEER_NKEYS, 1), 0).astype(F32)
    r16 = lax.broadcasted_iota(I32, (K, 1), 0)
    cand_id = jnp.concatenate([(a * K + r16) for a in range(_CAND_EDGE)]
                              + [(r16 * K + b) for b in range(_CAND_EDGE)], axis=0).astype(F32)
    dup = r16 < _CAND_EDGE
    idx_rows, gate_rows = [], []
    for h in range(PEER_HEADS):
        half = []
        for p in range(2):
            c0 = (h * 2 + p) * PEER_KEY_HALF
            st = _dot_nt(keys_ref[h * 2 + p], q[:, c0:c0 + PEER_KEY_HALF])
            half.append(_top16(st, key_id))
        (s1, i1), (s2, i2) = half
        cand = jnp.concatenate(
            [s1[a:a + 1] + s2 for a in range(_CAND_EDGE)]
            + [jnp.where(dup, -jnp.inf, s1 + s2[b:b + 1]) for b in range(_CAND_EDGE)], axis=0)
        cidx = jnp.concatenate(
            [i1[a:a + 1] * PEER_NKEYS + i2 for a in range(_CAND_EDGE)]
            + [i1 * PEER_NKEYS + i2[b:b + 1] for b in range(_CAND_EDGE)], axis=0)
        best, _, eidx = _top16(cand, cand_id, cidx)
        e = jnp.exp(best - best[0:1])
        gate_rows.append(e / jnp.sum(e, axis=0, keepdims=True))
        idx_rows.append(eidx)
    idx_ref[...] = jnp.concatenate(idx_rows, axis=0).T.astype(I32)
    gate_ref[...] = jnp.concatenate(gate_rows, axis=0).T


def _post(mixed2d, x2d, row0, mod, rows_per_batch, norm2_g, w_out, w_query, keys, tm, prev=None, fin=None):
    t = mixed2d.shape[0]
    steps = t // tm
    row = lambda w: pl.BlockSpec((tm, w), lambda i: (i, 0))
    in_specs = [row(D_MODEL), pl.BlockSpec((tm, D_MODEL), lambda i: (i + row0 // tm, 0)),
                _mod_spec(2, rows_per_batch, tm), _mod_spec(4, rows_per_batch, tm),
                _mod_spec(3, rows_per_batch, tm), _const_spec((1, D_MODEL)),
                _const_spec((D_MODEL, D_MODEL)), _const_spec((D_MODEL, 2 * PEER_HEADS * PEER_KEY_HALF)),
                _const_spec((2 * PEER_HEADS, PEER_NKEYS, PEER_KEY_HALF))]
    out_specs = [row(D_MODEL), row(PACK_HALF), row(PEER_HK), row(PEER_HK)]
    out_shape = [jax.ShapeDtypeStruct((t, D_MODEL), F32), jax.ShapeDtypeStruct((t, PACK_HALF), I32),
                 jax.ShapeDtypeStruct((t, PEER_HK), I32), jax.ShapeDtypeStruct((t, PEER_HK), F32)]
    args = [mixed2d, x2d, mod, mod, mod, norm2_g.reshape(1, -1), w_out, w_query, keys]
    if prev is not None:
        tp = prev[0].shape[0]
        prow = pl.BlockSpec((tp // steps, PEER_HK), lambda i: (i, 0))
        in_specs += [prow, prow]
        out_specs += [prow]
        out_shape += [jax.ShapeDtypeStruct((tp, PEER_HK), I32)]
        args += list(prev)
    if fin is not None:
        x1_f, peer_f, mod_f, rows_f, final_g = fin
        tf = x1_f.shape[0]
        frow = pl.BlockSpec((tf // steps, D_MODEL), lambda i: (i, 0))
        in_specs += [frow, frow, _mod_spec(5, rows_f, tf // steps), _const_spec((1, D_MODEL))]
        out_specs += [frow]
        out_shape += [jax.ShapeDtypeStruct((tf, D_MODEL), F32)]
        args += [x1_f, peer_f, mod_f, final_g.reshape(1, -1)]
    return pl.pallas_call(
        functools.partial(_post_body, prev is not None, fin is not None),
        grid=(steps,),
        in_specs=in_specs, out_specs=out_specs, out_shape=out_shape,
        compiler_params=pltpu.CompilerParams(vmem_limit_bytes=VMEM_LIMIT),
        name="post",
    )(*args)


SC_CORES = 2
SC_SUBCORES = 16
SC_LANES = 16
SC_WORKERS = SC_CORES * SC_SUBCORES
SC_TOKENS = 32
SC_SLOTS = 4
SC_JOB_HEADS = 2
SC_BF16_GROUP = 4
PACK_HALF = D_MODEL // 2
SC_CHUNKS = PACK_HALF // SC_LANES
PROMPT_PARTS = 8
EDGE_SPLITS = 2
RAMP_PARTS = 2
COEF_LAG = 2
FIN_LAG = 3
ROW_TILE = 256


def _bf16_bits(v):
    return lax.bitcast_convert_type(v.astype(BF16).astype(F32), jnp.uint32)


def _pack_words(lo, hi):
    return lax.bitcast_convert_type((_bf16_bits(lo) >> 16) | _bf16_bits(hi), I32)


def _pack_body(x_ref, o_ref):
    o_ref[...] = _pack_words(x_ref[:, :PACK_HALF], x_ref[:, PACK_HALF:])


def _pack_table(tbl, rows=2 * ROW_TILE):
    e = tbl.shape[0]
    return pl.pallas_call(
        _pack_body, grid=(e // rows,),
        in_specs=[pl.BlockSpec((rows, D_MODEL), lambda i: (i, 0))],
        out_specs=pl.BlockSpec((rows, PACK_HALF), lambda i: (i, 0)),
        out_shape=jax.ShapeDtypeStruct((e, PACK_HALF), I32), name="pack_table")(tbl)


def _tree_sum(terms):
    terms = list(terms)
    while len(terms) > 1:
        terms = [a + b for a, b in zip(terms[0::2], terms[1::2])] + terms[len(terms) & ~1:]
    return terms[0]


def _unpack_pair(w):
    lo = plsc.bitcast(lax.shift_left(w, jnp.full(w.shape, 16, I32)), F32)
    hi = plsc.bitcast(w & jnp.full(w.shape, -65536, I32), F32)
    return lo, hi


def _sc_mesh():
    return plsc.VectorSubcoreMesh(core_axis_name="c", subcore_axis_name="s")


def _sc_worker():
    return lax.axis_index("s") * SC_CORES + lax.axis_index("c")


def _sc_jobs(table_hbm, idx_v, buf, sem, compute):
    per_tok = PEER_HEADS // SC_JOB_HEADS
    njobs = idx_v.shape[0] * per_tok
    nrows = SC_JOB_HEADS * PEER_TOPK

    def copy(j, slot):
        rows = idx_v.at[j // per_tok, pl.ds((j % per_tok) * nrows, nrows)]
        return pltpu.make_async_copy(table_hbm.at[rows], buf.at[slot], sem.at[slot])

    for s in range(SC_SLOTS):
        copy(s, s).start()

    def job(j, c):
        s = j % SC_SLOTS
        copy(j, s).wait()

        def head(i, cc):
            compute(j // per_tok, (j % per_tok) * SC_JOB_HEADS + i, s, i * PEER_TOPK)
            return cc
        lax.fori_loop(0, SC_JOB_HEADS, head, 0)

        @pl.when(j + SC_SLOTS < njobs)
        def _next():
            copy(j + SC_SLOTS, s).start()
        return c

    lax.fori_loop(0, njobs, job, 0)


def _peer_u_body(n_tok, idx_hbm, h2_hbm, u_hbm, pre_hbm, idx_v, h2_v, pre_v, ubuf, acc_v, sem):
    base = _sc_worker() * n_tok
    lane = lax.iota(I32, SC_LANES)

    def compute(tt, h, slot, r0):
        def chunk(cg, accs):
            cs = [pl.ds((cg * SC_BF16_GROUP + i) * SC_LANES, SC_LANES) for i in range(SC_BF16_GROUP)]
            xs = [plsc.bitcast(h2_v[tt, c], BF16) for c in cs]
            out = []
            for k, a in enumerate(accs):
                part = _tree_sum([plsc.bitcast(ubuf[slot, r0 + k, c], BF16) * x for c, x in zip(cs, xs)])
                lo, hi = _unpack_pair(plsc.bitcast(part, I32))
                out.append(a + (lo + hi))
            return tuple(out)
        zero = jnp.zeros((SC_LANES,), F32)
        accs = lax.fori_loop(0, SC_CHUNKS // SC_BF16_GROUP, chunk, (zero,) * PEER_TOPK)
        for k, a in enumerate(accs):
            acc_v[k, :] = a
        tot = zero
        for j in range(SC_LANES):
            tot = tot + plsc.load_gather(acc_v, [lane, (lane + j) & (SC_LANES - 1)])
        pre_v[tt, pl.ds(h * PEER_TOPK, PEER_TOPK)] = tot

    tb = idx_v.shape[0]

    def block(bi, c):
        t0 = base + bi * tb
        pltpu.sync_copy(idx_hbm.at[pl.ds(t0, tb)], idx_v)
        pltpu.sync_copy(h2_hbm.at[pl.ds(t0, tb)], h2_v)
        _sc_jobs(u_hbm, idx_v, ubuf, sem, compute)
        pltpu.sync_copy(pre_v, pre_hbm.at[pl.ds(t0, tb)])
        return c

    lax.fori_loop(0, n_tok // tb, block, 0)


def _peer_v_body(n_tok, idx_hbm, coef_hbm, v_hbm, out_hbm, idx_v, coef_v, out_v, vbuf, sem):
    base = _sc_worker() * n_tok
    zero = jnp.zeros((SC_LANES,), F32)

    def compute(tt, h, slot, r0):
        cvec = coef_v[tt, pl.ds(h * PEER_TOPK, PEER_TOPK)]
        cb = [plsc.bitcast(jnp.take_along_axis(cvec, jnp.full((SC_LANES,), k, I32), axis=0), BF16)
              for k in range(PEER_TOPK)]

        @plsc.parallel_loop(0, SC_CHUNKS, unroll=2)
        def _chunk(c):
            cs = pl.ds(c * SC_LANES, SC_LANES)
            prods = [plsc.bitcast(vbuf[slot, r0 + k, cs], BF16) * cb[k] for k in range(PEER_TOPK)]
            pairs = [_unpack_pair(plsc.bitcast(_tree_sum(prods[g:g + SC_BF16_GROUP]), I32))
                     for g in range(0, PEER_TOPK, SC_BF16_GROUP)]
            for half, off in ((0, 0), (1, PACK_HALF)):
                plsc.addupdate(out_v.at[tt, pl.ds(off + c * SC_LANES, SC_LANES)],
                               _tree_sum([p[half] for p in pairs]))

    tb = idx_v.shape[0]

    def block(bi, c):
        t0 = base + bi * tb
        pltpu.sync_copy(idx_hbm.at[pl.ds(t0, tb)], idx_v)
        pltpu.sync_copy(coef_hbm.at[pl.ds(t0, tb)], coef_v)

        def clear(i, cc):
            per_row = D_MODEL // SC_LANES
            out_v[i // per_row, pl.ds((i % per_row) * SC_LANES, SC_LANES)] = zero
            return cc
        lax.fori_loop(0, tb * (D_MODEL // SC_LANES), clear, 0)
        _sc_jobs(v_hbm, idx_v, vbuf, sem, compute)
        pltpu.sync_copy(out_v, out_hbm.at[pl.ds(t0, tb)])
        return c

    lax.fori_loop(0, n_tok // tb, block, 0)


def _peer_sc(body, idx, rows, table, out_width, name):
    t = idx.shape[0]
    assert t % SC_WORKERS == 0
    n_tok = t // SC_WORKERS
    tb = min(SC_TOKENS * (2 if body is _peer_u_body else 1), n_tok)
    assert n_tok % tb == 0 and tb * PEER_HEADS // SC_JOB_HEADS >= SC_SLOTS
    return pl.kernel(
        functools.partial(body, n_tok),
        out_type=jax.ShapeDtypeStruct((t, out_width), F32),
        mesh=_sc_mesh(),
        scratch_types=[pltpu.VMEM((tb, PEER_HK), I32),
                       pltpu.VMEM((tb, rows.shape[1]), rows.dtype),
                       pltpu.VMEM((tb, out_width), F32),
                       pltpu.VMEM((SC_SLOTS, SC_JOB_HEADS * PEER_TOPK, PACK_HALF), I32)]
                      + ([pltpu.VMEM((PEER_TOPK, SC_LANES), F32)] if body is _peer_u_body else [])
                      + [pltpu.SemaphoreType.DMA((SC_SLOTS,))],
        compiler_params=pltpu.CompilerParams(needs_layout_passes=False),
        name=name,
    )(idx, rows, table)


def _sc_block_tokens(t, is_u):
    n_tok = t // SC_WORKERS
    tb = min(SC_TOKENS * (2 if is_u else 1), n_tok)
    assert t % SC_WORKERS == 0 and n_tok % tb == 0 and tb * PEER_HEADS // SC_JOB_HEADS >= SC_SLOTS
    return n_tok, tb


def _peer_vu_body(nv, tbv, nu, tbu, idxv_hbm, coef_hbm, v_hbm, idxu_hbm, h2_hbm, u_hbm,
                  out_hbm, pre_hbm, buf, acc_v, sem):
    pl.run_scoped(
        lambda a, b, c: _peer_v_body(nv, idxv_hbm, coef_hbm, v_hbm, out_hbm, a, b, c, buf, sem),
        pltpu.VMEM((tbv, PEER_HK), I32), pltpu.VMEM((tbv, PEER_HK), I32),
        pltpu.VMEM((tbv, D_MODEL), F32))
    pl.run_scoped(
        lambda a, b, c: _peer_u_body(nu, idxu_hbm, h2_hbm, u_hbm, pre_hbm, a, b, c, buf, acc_v, sem),
        pltpu.VMEM((tbu, PEER_HK), I32), pltpu.VMEM((tbu, PACK_HALF), I32),
        pltpu.VMEM((tbu, PEER_HK), F32))


def _peer_vu(gv, coef, gu, expert_v, expert_u):
    tv, tu = gv["idx"].shape[0], gu["idx"].shape[0]
    nv, tbv = _sc_block_tokens(tv, False)
    nu, tbu = _sc_block_tokens(tu, True)
    gv["peer_out"], gu["pre"] = pl.kernel(
        functools.partial(_peer_vu_body, nv, tbv, nu, tbu),
        out_type=(jax.ShapeDtypeStruct((tv, D_MODEL), F32), jax.ShapeDtypeStruct((tu, PEER_HK), F32)),
        mesh=_sc_mesh(),
        scratch_types=[pltpu.VMEM((SC_SLOTS, SC_JOB_HEADS * PEER_TOPK, PACK_HALF), I32),
                       pltpu.VMEM((PEER_TOPK, SC_LANES), F32),
                       pltpu.SemaphoreType.DMA((SC_SLOTS,))],
        compiler_params=pltpu.CompilerParams(needs_layout_passes=False),
        name="peer_vu",
    )(gv["idx"], coef, expert_v, gu["idx"], gu["h2"], expert_u)


def _coef_words(pre, gates):
    return _pack_words(*(gates * _gelu(pre),) * 2)


def _coef_body(pre_ref, gate_ref, coef_ref):
    coef_ref[...] = _coef_words(pre_ref[...], gate_ref[...])


def _coef(pre, gates, tm):
    t = pre.shape[0]
    row = pl.BlockSpec((tm, PEER_HK), lambda i: (i, 0))
    return pl.pallas_call(_coef_body, grid=(t // tm,), in_specs=[row, row], out_specs=row,
                          out_shape=jax.ShapeDtypeStruct((t, PEER_HK), I32), name="coef")(pre, gates)


def _final_body(x1_ref, peer_ref, g2_ref, fng_ref, y_ref):
    x2 = x1_ref[...] + _mod_rows(g2_ref) * peer_ref[...]
    y_ref[...] = x2 * lax.rsqrt(jnp.mean(x2 * x2, axis=-1, keepdims=True) + EPS) * fng_ref[...]


def _final(x1, peer_out, mod, rows_per_batch, final_g, tm):
    t = x1.shape[0]
    row = pl.BlockSpec((tm, D_MODEL), lambda i: (i, 0))
    return pl.pallas_call(
        _final_body, grid=(t // tm,),
        in_specs=[row, row, _mod_spec(5, rows_per_batch, tm), _const_spec((1, D_MODEL))],
        out_specs=row, out_shape=jax.ShapeDtypeStruct((t, D_MODEL), F32), name="final",
    )(x1, peer_out, mod, final_g.reshape(1, -1))


def _expert_gather_v(g, coef, expert_v):
    g["peer_out"] = _peer_sc(_peer_v_body, g["idx"], coef, expert_v, D_MODEL, "peer_v")


def _front(x, mod, conv_buf, s0, pool_buf, start, chunk, tm, wts, prev, fin):
    x2d, row0, b, l = x
    t = b * l
    assert row0 % tm == 0
    if l >= tm:
        modx = mod.reshape(b, 6, 1, D_MODEL).transpose(1, 0, 2, 3)
    else:
        modx = jnp.repeat(mod.reshape(b, 6, D_MODEL), l, axis=0).transpose(1, 0, 2)
    outs = _inproj(x2d, row0, t, modx, l, wts["norm1_g"], wts["w_cat"], tm)
    lp = -(-l // chunk) * chunk
    proj = {}
    for (name, w), a in zip(_IN_BLOCKS, outs):
        a = a.reshape(b, l, w)
        proj[name] = a if lp == l else jnp.pad(a, ((0, 0), (0, lp - l), (0, 0)))
    mixed, nconv, ns, npool = _mixer(proj, conv_buf, s0, pool_buf, start, l, chunk,
                                     wts["conv_w"], wts["a_log"], wts["dt_bias"], wts["dn_norm_g"],
                                     wts["w_pool"], wts["pool_scale"])
    mixed2d = mixed[:, :l].reshape(t, D_MODEL)
    res = _post(mixed2d, x2d, row0, modx, l, wts["norm2_g"], wts["w_out"], wts["w_query"], wts["keys"], tm,
                prev=None if prev is None else (prev["pre"], prev["gates"]),
                fin=None if fin is None else (fin["x1"], fin["peer_out"], fin["mod"], fin["l"],
                                              wts["final_norm_g"]))
    x1, h2, idx, gates = res[:4]
    extra = list(res[4:])
    coef_prev = extra.pop(0) if prev is not None else None
    y_fin = extra.pop(0).reshape(fin["b"], fin["l"], D_MODEL) if fin is not None else None
    g = dict(x1=x1, idx=idx, gates=gates, h2=h2, mod=modx, b=b, l=l, tm=tm,
             states=(nconv, ns, npool))
    return g, coef_prev, y_fin


def kernel(x_prompt, x_sample, c_prompt, c_sample, state_conv, state_delta, state_pool, w_ada, b_ada, norm1_g, w_in, conv_w, a_log, dt_bias, dn_norm_g, w_pool, pool_scale, w_out, norm2_g, w_query, sub_keys, expert_u, expert_v, final_norm_g):
    bp = x_prompt.shape[0]
    yp, ys = x_prompt, x_sample
    conv_p, delta_p, pool_p, conv_s, delta_s, pool_s = [], [], [], [], [], []
    zero_conv = jnp.zeros((bp, CONV_WIDTH - 1, QKV_WIDTH), F32)
    zero_delta = jnp.zeros((bp, DN_HEADS, DN_HEAD_DIM, DN_HEAD_DIM), F32)
    zero_pool = jnp.zeros((bp, POOL_BUF, POOL_WIDTH), F32)
    c_all = jnp.concatenate([c_prompt, c_sample], axis=0)
    for layer in range(DEPTH):
        wi = w_in[layer]
        o_b = QKV_WIDTH
        o_z = o_b + 2 * DN_HEADS
        w_ba = jnp.pad(wi[:, o_b:o_z], ((0, 0), (0, LANES - 2 * DN_HEADS)))
        w_cat = jnp.concatenate([wi[:, :o_b], wi[:, o_z:], w_ba], axis=1).astype(BF16)
        last = layer == DEPTH - 1
        wts = dict(
            norm1_g=norm1_g[layer], w_cat=w_cat, conv_w=conv_w[layer], a_log=a_log[layer],
            dt_bias=dt_bias[layer], dn_norm_g=dn_norm_g[layer], w_pool=w_pool[layer],
            pool_scale=pool_scale[layer], w_out=w_out[layer].astype(BF16), norm2_g=norm2_g[layer],
            w_query=w_query[layer].astype(BF16),
            keys=sub_keys[layer].reshape(2 * PEER_HEADS, PEER_NKEYS, PEER_KEY_HALF).astype(BF16),
            expert_u=_pack_table(expert_u[layer]), expert_v=_pack_table(expert_v[layer]),
            final_norm_g=final_norm_g if last else jnp.ones_like(final_norm_g))
        mod = _ada(c_all, w_ada[layer], b_ada[layer])
        assert last, "final norm is fused into the expert stage"
        step = bp // PROMPT_PARTS
        seq = x_prompt.shape[1]
        assert step == 1
        xp2d = yp.reshape(bp * seq, D_MODEL)
        zeros = (zero_conv[:step], zero_delta[:step], zero_pool[:step])
        jobs, cuts = [], []
        for b0 in range(0, bp, step):
            n = EDGE_SPLITS if b0 in (0, bp - step) else 1
            cuts.append(n)
            for s0 in range(0, seq, seq // n):
                jobs.append(((xp2d, b0 * seq + s0, step, seq // n), mod[b0:b0 + step],
                             zeros if s0 == 0 else None, s0, DN_CHUNK))
        jobs.append(((ys.reshape(-1, D_MODEL), 0) + ys.shape[:2], mod[bp:], (state_conv[layer], state_delta[layer], state_pool[layer]),
                     PAST_LEN, SUBLANES))
        groups = []
        for j, (xg, mg, states, start, chunk) in enumerate(jobs):
            pi = j - (1 if j <= RAMP_PARTS else COEF_LAG)
            prev = groups[pi] if pi >= 0 and "peer_out" not in groups[pi] else None
            fin = groups[j - FIN_LAG] if j >= FIN_LAG and "peer_out" in groups[j - FIN_LAG] else None
            if fin is not None and fin["x1"].shape[0] % (xg[2] * xg[3] // ROW_TILE):
                fin = None
            if states is None:
                states = groups[j - 1]["states"]
            g, coef_prev, y_fin = _front(xg, mg, *states, start, chunk, ROW_TILE, wts, prev, fin)
            if prev is not None:
                _peer_vu(prev, coef_prev, g, wts["expert_v"], wts["expert_u"])
            else:
                g["pre"] = _peer_sc(_peer_u_body, g["idx"], g["h2"], wts["expert_u"], PEER_HK, "peer_u")
            if fin is not None:
                fin["y"] = y_fin
            groups.append(g)
        for g in groups:
            if "peer_out" not in g:
                _expert_gather_v(g, _coef(g["pre"], g["gates"], ROW_TILE), wts["expert_v"])
        for g in groups:
            if "y" not in g:
                g["y"] = _final(g["x1"], g["peer_out"], g["mod"], g["l"], wts["final_norm_g"],
                                g["tm"]).reshape(g["b"], g["l"], D_MODEL)
        rows, at = [], 0
        for n in cuts:
            rows.append(groups[at:at + n])
            at += n
        yp = jnp.concatenate([jnp.concatenate([g["y"] for g in row], axis=1) for row in rows], axis=0)
        cp, sp, pp = (jnp.concatenate(a, axis=0) for a in zip(*(row[-1]["states"] for row in rows)))
        ys = groups[-1]["y"]
        cs, ss, ps = groups[-1]["states"]
        conv_p.append(cp)
        delta_p.append(sp)
        pool_p.append(pp)
        conv_s.append(cs)
        delta_s.append(ss)
        pool_s.append(ps)
    return (yp, ys, jnp.stack(conv_p), jnp.stack(delta_p), jnp.stack(pool_p),
            jnp.stack(conv_s), jnp.stack(delta_s), jnp.stack(pool_s))
```

```python
import functools

import jax
import jax.numpy as jnp
from jax import lax
from jax.experimental import pallas as pl
from jax.experimental.pallas import tpu as pltpu
from jax.experimental.pallas import tpu_sc as plsc

F32 = jnp.float32
BF16 = jnp.bfloat16
I32 = jnp.int32

D_MODEL = 1024
DEPTH = 1
PAST_LEN = 16384
DN_HEADS = 8
DN_HEAD_DIM = 128
DN_WIDTH = DN_HEADS * DN_HEAD_DIM
QKV_WIDTH = 3 * DN_WIDTH
CONV_WIDTH = 4
DN_CHUNK = 64
POOL_WINDOWS = (2, 4, 8, 16)
POOL_GROUP_DIM = 128
POOL_WIDTH = len(POOL_WINDOWS) * POOL_GROUP_DIM
POOL_OUT_GROUP = D_MODEL // len(POOL_WINDOWS)
POOL_BUF = max(POOL_WINDOWS) - 1
PEER_HEADS = 8
PEER_NKEYS = 128
PEER_TOPK = 16
PEER_KEY_HALF = 128
PEER_HK = PEER_HEADS * PEER_TOPK
EPS = 1e-6

LANES = 128
SUBLANES = 8
CONV_PAD = SUBLANES
POOL_PAD = 16
VMEM_LIMIT = 56 * 1024 * 1024

NT_DIMS = (((1,), (1,)), ((), ()))
TN_DIMS = (((0,), (0,)), ((), ()))


def _dot(a, b):
    return jnp.dot(a.astype(BF16), b.astype(BF16), preferred_element_type=F32)


def _dot_nt(a, b):
    return lax.dot_general(a.astype(BF16), b.astype(BF16), NT_DIMS, preferred_element_type=F32)


def _split3(x):
    hi = x.astype(BF16)
    r1 = x - hi.astype(F32)
    mid = r1.astype(BF16)
    lo = (r1 - mid.astype(F32)).astype(BF16)
    return hi, mid, lo


def _silu(x):
    return x * jax.nn.sigmoid(x)


def _gelu(x):
    return 0.5 * x * (1.0 + lax.erf(x * (0.5 ** 0.5)))


def _softplus(x):
    return jnp.maximum(x, 0.0) + jnp.log(1.0 + jnp.exp(-jnp.abs(x)))


def _mod_rows(ref):
    m = ref[...]
    return m.reshape(m.shape[-2], m.shape[-1])


def _mod_spec(k, rows_per_batch, tm):
    if rows_per_batch >= tm:
        tiles = rows_per_batch // tm
        return pl.BlockSpec((1, 1, 1, D_MODEL), lambda i, *_: (k, i // tiles, 0, 0))
    return pl.BlockSpec((1, tm, D_MODEL), lambda i, *_: (k, i, 0))


def _const_spec(shape):
    nd = len(shape)
    return pl.BlockSpec(shape, lambda *_: (0,) * nd)


def _ada_body(c_ref, w_ref, b_ref, o_ref):
    o_ref[...] = _dot(_silu(c_ref[...]), w_ref[...]) + b_ref[...]


def _ada(c, w_ada, b_ada):
    n = c.shape[0]
    return pl.pallas_call(
        _ada_body,
        grid=(6,),
        in_specs=[pl.BlockSpec((n, D_MODEL), lambda j: (0, 0)),
                  pl.BlockSpec((D_MODEL, D_MODEL), lambda j: (0, j)),
                  pl.BlockSpec((1, D_MODEL), lambda j: (0, j))],
        out_specs=pl.BlockSpec((n, D_MODEL), lambda j: (0, j)),
        out_shape=jax.ShapeDtypeStruct((n, 6 * D_MODEL), F32),
        name="ada",
    )(c, w_ada, b_ada.reshape(1, -1))


_IN_BLOCKS = (("qkv", QKV_WIDTH), ("z", DN_WIDTH), ("pool", POOL_WIDTH),
              ("ga", D_MODEL), ("gb", D_MODEL), ("ba", LANES))
_IN_TOTAL = sum(w for _, w in _IN_BLOCKS)
_IN_F32 = ("ba",)
_IN_COL_CHUNK = 512


def _inproj_body(x_ref, sc_ref, sh_ref, g_ref, w_ref, *out_refs):
    x = x_ref[...]
    y = x * lax.rsqrt(jnp.mean(x * x, axis=-1, keepdims=True) + EPS) * g_ref[...]
    h = (y * (1.0 + _mod_rows(sc_ref)) + _mod_rows(sh_ref)).astype(BF16)
    off = 0
    for (_, width), o_ref in zip(_IN_BLOCKS, out_refs):
        for c0 in range(0, width, _IN_COL_CHUNK):
            cw = min(_IN_COL_CHUNK, width - c0)
            o_ref[:, c0:c0 + cw] = jnp.dot(h, w_ref[:, off + c0:off + c0 + cw],
                                           preferred_element_type=F32).astype(o_ref.dtype)
        off += width


def _inproj(x2d, row0, t, mod, rows_per_batch, norm_g, w_cat, tm):
    row = lambda w: pl.BlockSpec((tm, w), lambda i: (i, 0))
    return pl.pallas_call(
        _inproj_body,
        grid=(t // tm,),
        in_specs=[pl.BlockSpec((tm, D_MODEL), lambda i: (i + row0 // tm, 0)),
                  _mod_spec(1, rows_per_batch, tm), _mod_spec(0, rows_per_batch, tm),
                  _const_spec((1, D_MODEL)),
                  pl.BlockSpec((D_MODEL, _IN_TOTAL), lambda i: (0, 0), pipeline_mode=pl.Buffered(1))],
        out_specs=[row(w) for _, w in _IN_BLOCKS],
        out_shape=[jax.ShapeDtypeStruct((t, w), F32 if name in _IN_F32 else BF16) for name, w in _IN_BLOCKS],
        compiler_params=pltpu.CompilerParams(vmem_limit_bytes=VMEM_LIMIT),
        name="inproj",
    )(x2d, mod, mod, norm_g.reshape(1, -1), w_cat)


def _mixer_body(C, Lv, start,
                qkv_ref, ba_ref, z_ref, pin_ref, ga_ref, gb_ref, cbuf_ref, s0_ref, pbuf_ref,
                convw_ref, alog_ref, dtb_ref, dng_ref, wpool_ref, pscale_ref,
                mixed_ref, nconv_ref, ns_ref, npool_ref,
                xp_scr, act_scr, s_scr, pp_scr, odn_scr):
    n = pl.program_id(1)
    last = pl.num_programs(1) - 1

    @pl.when(n == 0)
    def _load_state():
        xp_scr[0:CONV_PAD, :] = cbuf_ref[0]
        pp_scr[0:POOL_PAD, :] = pbuf_ref[0]
        s_scr[...] = s0_ref[0]

    xp_scr[CONV_PAD:CONV_PAD + C, :] = qkv_ref[0].astype(F32)
    for c0 in range(0, QKV_WIDTH, _IN_COL_CHUNK):
        cs = slice(c0, c0 + _IN_COL_CHUNK)
        y = xp_scr[CONV_PAD:CONV_PAD + C, cs] * convw_ref[CONV_WIDTH - 1:CONV_WIDTH, cs]
        for k in range(CONV_WIDTH - 1):
            r0 = CONV_PAD - (CONV_WIDTH - 1) + k
            y = y + xp_scr[r0:r0 + C, cs] * convw_ref[k:k + 1, cs]
        act_scr[:, cs] = _silu(y)

    ba = ba_ref[0]
    lane = lax.broadcasted_iota(I32, (C, LANES), 1)
    beta_all = jax.nn.sigmoid(ba)
    g_all = -jnp.exp(alog_ref[...]) * _softplus(ba + dtb_ref[...])
    if Lv < C:
        valid = lax.broadcasted_iota(I32, (C, LANES), 0) < Lv
        beta_all = jnp.where(valid, beta_all, 0.0)
        g_all = jnp.where(valid, g_all, 0.0)
    ii = lax.broadcasted_iota(I32, (C, C), 0)
    jj = lax.broadcasted_iota(I32, (C, C), 1)
    causal = ii >= jj
    strict = ii > jj
    tril = jnp.where(causal, 1.0, 0.0).astype(BF16)
    eye = jnp.where(ii == jj, 1.0, 0.0)
    gc_all = sum(jnp.dot(tril, part, preferred_element_type=F32) for part in _split3(g_all))
    if C < LANES:
        gc_sq = jnp.concatenate([gc_all, jnp.zeros((LANES - C, LANES), F32)], axis=0)
    else:
        gc_sq = gc_all
    gc_t = gc_sq.T

    H = range(DN_HEADS)
    hsl = [slice(h * DN_HEAD_DIM, (h + 1) * DN_HEAD_DIM) for h in H]
    beta = [jnp.sum(jnp.where(lane == h, beta_all, 0.0), axis=1, keepdims=True) for h in H]
    gcol = [jnp.sum(jnp.where(lane == DN_HEADS + h, gc_all, 0.0), axis=1, keepdims=True) for h in H]
    grow = [gc_t[DN_HEADS + h:DN_HEADS + h + 1, 0:C] for h in H]
    glast = [g[C - 1:C, :] for g in gcol]
    q = [act_scr[:, hsl[h]] for h in H]
    k = [act_scr[:, DN_WIDTH + h * DN_HEAD_DIM:DN_WIDTH + (h + 1) * DN_HEAD_DIM] for h in H]
    v = [act_scr[:, 2 * DN_WIDTH + h * DN_HEAD_DIM:2 * DN_WIDTH + (h + 1) * DN_HEAD_DIM] for h in H]
    q = [x * lax.rsqrt(jnp.sum(x * x, axis=-1, keepdims=True) + EPS) * (DN_HEAD_DIM ** -0.5) for x in q]
    k = [x * lax.rsqrt(jnp.sum(x * x, axis=-1, keepdims=True) + EPS) for x in k]
    kb = [k[h] * beta[h] for h in H]
    vb = [v[h] * beta[h] for h in H]
    decay = [jnp.where(causal, jnp.exp(jnp.where(causal, gcol[h] - grow[h], 0.0)), 0.0) for h in H]
    lower = [jnp.where(strict, _dot_nt(kb[h], k[h]) * decay[h], 0.0) for h in H]
    ainv = [eye - x for x in lower]
    pw = lower
    p = 1
    while 2 * p < C:
        pw = [_dot(x, x) for x in pw]
        ainv = [ainv[h] + _dot(ainv[h], pw[h]) for h in H]
        p *= 2
    sol = [_dot(ainv[h], jnp.concatenate([vb[h], kb[h] * jnp.exp(gcol[h])], axis=1)) for h in H]
    qk = [_dot_nt(q[h], k[h]) * decay[h] for h in H]
    k_tail = [k[h] * jnp.exp(glast[h] - gcol[h]) for h in H]
    S = [s_scr[h] for h in H]
    v_new = [sol[h][:, :DN_HEAD_DIM] - _dot(sol[h][:, DN_HEAD_DIM:], S[h]) for h in H]
    o = [_dot(q[h] * jnp.exp(gcol[h]), S[h]) + _dot(qk[h], v_new[h]) for h in H]
    for h in H:
        s_scr[h] = S[h] * jnp.exp(glast[h]) + lax.dot_general(
            k_tail[h].astype(BF16), v_new[h].astype(BF16), TN_DIMS, preferred_element_type=F32)
    for h in H:
        zf = z_ref[0, :, hsl[h]].astype(F32)
        odn_scr[:, hsl[h]] = (o[h] * lax.rsqrt(jnp.mean(o[h] * o[h], axis=-1, keepdims=True) + EPS)
                              * dng_ref[...] * _silu(zf))

    pp_scr[POOL_PAD:POOL_PAD + C, :] = pin_ref[0].astype(F32)
    pos = start + n * C + lax.broadcasted_iota(I32, (C, 1), 0)
    for gi, win in enumerate(POOL_WINDOWS):
        gs = slice(gi * POOL_GROUP_DIM, (gi + 1) * POOL_GROUP_DIM)
        xg = pp_scr[POOL_PAD:POOL_PAD + C, gs]
        ssum = xg
        for sft in range(1, win):
            ssum = ssum + pp_scr[POOL_PAD - sft:POOL_PAD - sft + C, gs]
        cnt = jnp.minimum(pos + 1, win).astype(F32)
        pooled = ssum / cnt - xg
        os_ = slice(gi * POOL_OUT_GROUP, (gi + 1) * POOL_OUT_GROUP)
        yp = _dot(pooled, wpool_ref[gi]) * pscale_ref[:, os_]
        mixed_ref[0, :, os_] = (jax.nn.sigmoid(ga_ref[0, :, os_].astype(F32)) * odn_scr[:, os_]
                                + jax.nn.sigmoid(gb_ref[0, :, os_].astype(F32)) * yp).astype(BF16)

    @pl.when(n == last)
    def _store_state():
        nconv_ref[0] = xp_scr[Lv + CONV_PAD - (CONV_WIDTH - 1):Lv + CONV_PAD, :]
        npool_ref[0] = pp_scr[Lv + POOL_PAD - POOL_BUF:Lv + POOL_PAD, :]
        ns_ref[0] = s_scr[...]

    xp_scr[0:CONV_PAD, :] = xp_scr[C:C + CONV_PAD, :]
    pp_scr[0:POOL_PAD, :] = pp_scr[C:C + POOL_PAD, :]


def _mixer(proj, conv_buf, s0, pool_buf, start, seq_len, C,
           conv_w, a_log, dt_bias, dn_norm_g, w_pool, pool_scale):
    b, lp, _ = proj["qkv"].shape
    nchunks = lp // C
    lv = seq_len - (nchunks - 1) * C
    cbuf = jnp.pad(conv_buf, ((0, 0), (CONV_PAD - (CONV_WIDTH - 1), 0), (0, 0)))
    pbuf = jnp.pad(pool_buf, ((0, 0), (POOL_PAD - POOL_BUF, 0), (0, 0)))
    lane_pad = lambda a: jnp.pad(a.reshape(1, -1), ((0, 0), (DN_HEADS, LANES - 2 * DN_HEADS)))
    chunk = lambda w: pl.BlockSpec((1, C, w), lambda i, j: (i, j, 0))
    state = lambda *s: pl.BlockSpec((1,) + s, lambda i, j: (i,) + (0,) * len(s))
    return pl.pallas_call(
        functools.partial(_mixer_body, C, lv, start),
        grid=(b, nchunks),
        in_specs=[chunk(QKV_WIDTH), chunk(LANES), chunk(DN_WIDTH), chunk(POOL_WIDTH),
                  chunk(D_MODEL), chunk(D_MODEL),
                  state(CONV_PAD, QKV_WIDTH), state(DN_HEADS, DN_HEAD_DIM, DN_HEAD_DIM),
                  state(POOL_PAD, POOL_WIDTH),
                  _const_spec((CONV_WIDTH, QKV_WIDTH)), _const_spec((1, LANES)), _const_spec((1, LANES)),
                  _const_spec((1, DN_HEAD_DIM)),
                  _const_spec((len(POOL_WINDOWS), POOL_GROUP_DIM, POOL_OUT_GROUP)),
                  _const_spec((1, D_MODEL))],
        out_specs=[chunk(D_MODEL), state(CONV_WIDTH - 1, QKV_WIDTH),
                   state(DN_HEADS, DN_HEAD_DIM, DN_HEAD_DIM), state(POOL_BUF, POOL_WIDTH)],
        out_shape=[jax.ShapeDtypeStruct((b, lp, D_MODEL), BF16),
                   jax.ShapeDtypeStruct((b, CONV_WIDTH - 1, QKV_WIDTH), F32),
                   jax.ShapeDtypeStruct((b, DN_HEADS, DN_HEAD_DIM, DN_HEAD_DIM), F32),
                   jax.ShapeDtypeStruct((b, POOL_BUF, POOL_WIDTH), F32)],
        scratch_shapes=[pltpu.VMEM((CONV_PAD + C + CONV_PAD, QKV_WIDTH), F32),
                        pltpu.VMEM((C, QKV_WIDTH), F32),
                        pltpu.VMEM((DN_HEADS, DN_HEAD_DIM, DN_HEAD_DIM), F32),
                        pltpu.VMEM((POOL_PAD + C + POOL_PAD, POOL_WIDTH), F32),
                        pltpu.VMEM((C, DN_WIDTH), F32)],
        compiler_params=pltpu.CompilerParams(dimension_semantics=("arbitrary", "arbitrary"),
                                             vmem_limit_bytes=VMEM_LIMIT),
        name="mixer",
    )(proj["qkv"], proj["ba"], proj["z"], proj["pool"], proj["ga"], proj["gb"], cbuf, s0, pbuf,
      conv_w, lane_pad(a_log), lane_pad(dt_bias), dn_norm_g.reshape(1, -1), w_pool,
      pool_scale.reshape(1, -1))


def _top16(s, ids, payload=None):
    big = float(2 ** 24)
    vals, sel, pays = [], [], []
    for _ in range(PEER_TOPK):
        m = jnp.max(s, axis=0, keepdims=True)
        am = jnp.min(jnp.where(s == m, ids, big), axis=0, keepdims=True)
        hit = ids == am
        if payload is not None:
            pays.append(jnp.max(jnp.where(hit, payload, -1.0), axis=0, keepdims=True))
        s = jnp.where(hit, -jnp.inf, s)
        vals.append(m)
        sel.append(am)
    out = (jnp.concatenate(vals, axis=0), jnp.concatenate(sel, axis=0))
    if payload is not None:
        out += (jnp.concatenate(pays, axis=0),)
    return out


_CAND_EDGE = 4


def _post_body(has_prev, has_fin, mixed_ref, x_ref, g1_ref, sc2_ref, sh2_ref, n2g_ref, wout_ref,
               wq_ref, keys_ref, *refs):
    refs = list(refs)
    prev_in = [refs.pop(0) for _ in range(2 if has_prev else 0)]
    fin_in = [refs.pop(0) for _ in range(4 if has_fin else 0)]
    x1_ref, h2_ref, idx_ref, gate_ref = refs[:4]
    extra_out = refs[4:]
    if has_prev:
        pre_ref, pgate_ref = prev_in
        extra_out.pop(0)[...] = _coef_words(pre_ref[...], pgate_ref[...])
    if has_fin:
        _final_body(*fin_in, extra_out.pop(0))
    tm = x_ref.shape[0]
    x1 = x_ref[...] + _mod_rows(g1_ref) * _dot(mixed_ref[...], wout_ref[...])
    x1_ref[...] = x1
    y = x1 * lax.rsqrt(jnp.mean(x1 * x1, axis=-1, keepdims=True) + EPS) * n2g_ref[...]
    h2 = y * (1.0 + _mod_rows(sc2_ref)) + _mod_rows(sh2_ref)
    h2_ref[...] = _pack_words(h2[:, :PACK_HALF], h2[:, PACK_HALF:])
    q = _dot(h2, wq_ref[...])

    K = PEER_TOPK
    key_id = lax.broadcasted_iota(I32, (PEER_NKEYS, 1), 0).astype(F32)
    r16 = lax.broadcasted_iota(I32, (K, 1), 0)
    cand_id = jnp.concatenate([(a * K + r16) for a in range(_CAND_EDGE)]
                              + [(r16 * K + b) for b in range(_CAND_EDGE)], axis=0).astype(F32)
    dup = r16 < _CAND_EDGE
    idx_rows, gate_rows = [], []
    for h in range(PEER_HEADS):
        half = []
        for p in range(2):
            c0 = (h * 2 + p) * PEER_KEY_HALF
            st = _dot_nt(keys_ref[h * 2 + p], q[:, c0:c0 + PEER_KEY_HALF])
            half.append(_top16(st, key_id))
        (s1, i1), (s2, i2) = half
        cand = jnp.concatenate(
            [s1[a:a + 1] + s2 for a in range(_CAND_EDGE)]
            + [jnp.where(dup, -jnp.inf, s1 + s2[b:b + 1]) for b in range(_CAND_EDGE)], axis=0)
        cidx = jnp.concatenate(
            [i1[a:a + 1] * PEER_NKEYS + i2 for a in range(_CAND_EDGE)]
            + [i1 * PEER_NKEYS + i2[b:b + 1] for b in range(_CAND_EDGE)], axis=0)
        best, _, eidx = _top16(cand, cand_id, cidx)
        e = jnp.exp(best - best[0:1])
        gate_rows.append(e / jnp.sum(e, axis=0, keepdims=True))
        idx_rows.append(eidx)
    idx_ref[...] = jnp.concatenate(idx_rows, axis=0).T.astype(I32)
    gate_ref[...] = jnp.concatenate(gate_rows, axis=0).T


def _post(mixed2d, x2d, row0, mod, rows_per_batch, norm2_g, w_out, w_query, keys, tm, prev=None, fin=None):
    t = mixed2d.shape[0]
    steps = t // tm
    row = lambda w: pl.BlockSpec((tm, w), lambda i: (i, 0))
    in_specs = [row(D_MODEL), pl.BlockSpec((tm, D_MODEL), lambda i: (i + row0 // tm, 0)),
                _mod_spec(2, rows_per_batch, tm), _mod_spec(4, rows_per_batch, tm),
                _mod_spec(3, rows_per_batch, tm), _const_spec((1, D_MODEL)),
                _const_spec((D_MODEL, D_MODEL)), _const_spec((D_MODEL, 2 * PEER_HEADS * PEER_KEY_HALF)),
                _const_spec((2 * PEER_HEADS, PEER_NKEYS, PEER_KEY_HALF))]
    out_specs = [row(D_MODEL), row(PACK_HALF), row(PEER_HK), row(PEER_HK)]
    out_shape = [jax.ShapeDtypeStruct((t, D_MODEL), F32), jax.ShapeDtypeStruct((t, PACK_HALF), I32),
                 jax.ShapeDtypeStruct((t, PEER_HK), I32), jax.ShapeDtypeStruct((t, PEER_HK), F32)]
    args = [mixed2d, x2d, mod, mod, mod, norm2_g.reshape(1, -1), w_out, w_query, keys]
    if prev is not None:
        tp = prev[0].shape[0]
        prow = pl.BlockSpec((tp // steps, PEER_HK), lambda i: (i, 0))
        in_specs += [prow, prow]
        out_specs += [prow]
        out_shape += [jax.ShapeDtypeStruct((tp, PEER_HK), I32)]
        args += list(prev)
    if fin is not None:
        x1_f, peer_f, mod_f, rows_f, final_g = fin
        tf = x1_f.shape[0]
        frow = pl.BlockSpec((tf // steps, D_MODEL), lambda i: (i, 0))
        in_specs += [frow, frow, _mod_spec(5, rows_f, tf // steps), _const_spec((1, D_MODEL))]
        out_specs += [frow]
        out_shape += [jax.ShapeDtypeStruct((tf, D_MODEL), F32)]
        args += [x1_f, peer_f, mod_f, final_g.reshape(1, -1)]
    return pl.pallas_call(
        functools.partial(_post_body, prev is not None, fin is not None),
        grid=(steps,),
        in_specs=in_specs, out_specs=out_specs, out_shape=out_shape,
        compiler_params=pltpu.CompilerParams(vmem_limit_bytes=VMEM_LIMIT),
        name="post",
    )(*args)


SC_CORES = 2
SC_SUBCORES = 16
SC_LANES = 16
SC_WORKERS = SC_CORES * SC_SUBCORES
SC_TOKENS = 32
SC_SLOTS = 4
SC_JOB_HEADS = 2
SC_BF16_GROUP = 4
PACK_HALF = D_MODEL // 2
SC_CHUNKS = PACK_HALF // SC_LANES
PROMPT_PARTS = 8
EDGE_SPLITS = 2
RAMP_PARTS = 2
COEF_LAG = 2
FIN_LAG = 4
ROW_TILE = 256


def _bf16_bits(v):
    return lax.bitcast_convert_type(v.astype(BF16).astype(F32), jnp.uint32)


def _pack_words(lo, hi):
    return lax.bitcast_convert_type((_bf16_bits(lo) >> 16) | _bf16_bits(hi), I32)


def _pack_body(x_ref, o_ref):
    o_ref[...] = _pack_words(x_ref[:, :PACK_HALF], x_ref[:, PACK_HALF:])


def _pack_table(tbl, rows=2 * ROW_TILE):
    e = tbl.shape[0]
    return pl.pallas_call(
        _pack_body, grid=(e // rows,),
        in_specs=[pl.BlockSpec((rows, D_MODEL), lambda i: (i, 0))],
        out_specs=pl.BlockSpec((rows, PACK_HALF), lambda i: (i, 0)),
        out_shape=jax.ShapeDtypeStruct((e, PACK_HALF), I32), name="pack_table")(tbl)


def _tree_sum(terms):
    terms = list(terms)
    while len(terms) > 1:
        terms = [a + b for a, b in zip(terms[0::2], terms[1::2])] + terms[len(terms) & ~1:]
    return terms[0]


def _unpack_pair(w):
    lo = plsc.bitcast(lax.shift_left(w, jnp.full(w.shape, 16, I32)), F32)
    hi = plsc.bitcast(w & jnp.full(w.shape, -65536, I32), F32)
    return lo, hi


def _sc_mesh():
    return plsc.VectorSubcoreMesh(core_axis_name="c", subcore_axis_name="s")


def _sc_worker():
    return lax.axis_index("s") * SC_CORES + lax.axis_index("c")


def _sc_jobs(table_hbm, idx_v, buf, sem, compute):
    per_tok = PEER_HEADS // SC_JOB_HEADS
    njobs = idx_v.shape[0] * per_tok
    nrows = SC_JOB_HEADS * PEER_TOPK

    def copy(j, slot):
        rows = idx_v.at[j // per_tok, pl.ds((j % per_tok) * nrows, nrows)]
        return pltpu.make_async_copy(table_hbm.at[rows], buf.at[slot], sem.at[slot])

    for s in range(SC_SLOTS):
        copy(s, s).start()

    def job(j, c):
        s = j % SC_SLOTS
        copy(j, s).wait()

        def head(i, cc):
            compute(j // per_tok, (j % per_tok) * SC_JOB_HEADS + i, s, i * PEER_TOPK)
            return cc
        lax.fori_loop(0, SC_JOB_HEADS, head, 0)

        @pl.when(j + SC_SLOTS < njobs)
        def _next():
            copy(j + SC_SLOTS, s).start()
        return c

    lax.fori_loop(0, njobs, job, 0)


def _peer_u_body(n_tok, idx_hbm, h2_hbm, u_hbm, pre_hbm, idx_v, h2_v, pre_v, ubuf, acc_v, sem):
    base = _sc_worker() * n_tok
    lane = lax.iota(I32, SC_LANES)

    def compute(tt, h, slot, r0):
        def chunk(cg, accs):
            cs = [pl.ds((cg * SC_BF16_GROUP + i) * SC_LANES, SC_LANES) for i in range(SC_BF16_GROUP)]
            xs = [plsc.bitcast(h2_v[tt, c], BF16) for c in cs]
            out = []
            for k, a in enumerate(accs):
                part = _tree_sum([plsc.bitcast(ubuf[slot, r0 + k, c], BF16) * x for c, x in zip(cs, xs)])
                lo, hi = _unpack_pair(plsc.bitcast(part, I32))
                out.append(a + (lo + hi))
            return tuple(out)
        zero = jnp.zeros((SC_LANES,), F32)
        accs = lax.fori_loop(0, SC_CHUNKS // SC_BF16_GROUP, chunk, (zero,) * PEER_TOPK)
        for k, a in enumerate(accs):
            acc_v[k, :] = a
        tot = zero
        for j in range(SC_LANES):
            tot = tot + plsc.load_gather(acc_v, [lane, (lane + j) & (SC_LANES - 1)])
        pre_v[tt, pl.ds(h * PEER_TOPK, PEER_TOPK)] = tot

    tb = idx_v.shape[0]

    def block(bi, c):
        t0 = base + bi * tb
        pltpu.sync_copy(idx_hbm.at[pl.ds(t0, tb)], idx_v)
        pltpu.sync_copy(h2_hbm.at[pl.ds(t0, tb)], h2_v)
        _sc_jobs(u_hbm, idx_v, ubuf, sem, compute)
        pltpu.sync_copy(pre_v, pre_hbm.at[pl.ds(t0, tb)])
        return c

    lax.fori_loop(0, n_tok // tb, block, 0)


def _peer_v_body(n_tok, idx_hbm, coef_hbm, v_hbm, out_hbm, idx_v, coef_v, out_v, vbuf, sem):
    base = _sc_worker() * n_tok
    zero = jnp.zeros((SC_LANES,), F32)

    def compute(tt, h, slot, r0):
        cvec = coef_v[tt, pl.ds(h * PEER_TOPK, PEER_TOPK)]
        cb = [plsc.bitcast(jnp.take_along_axis(cvec, jnp.full((SC_LANES,), k, I32), axis=0), BF16)
              for k in range(PEER_TOPK)]

        @plsc.parallel_loop(0, SC_CHUNKS, unroll=2)
        def _chunk(c):
            cs = pl.ds(c * SC_LANES, SC_LANES)
            prods = [plsc.bitcast(vbuf[slot, r0 + k, cs], BF16) * cb[k] for k in range(PEER_TOPK)]
            pairs = [_unpack_pair(plsc.bitcast(_tree_sum(prods[g:g + SC_BF16_GROUP]), I32))
                     for g in range(0, PEER_TOPK, SC_BF16_GROUP)]
            for half, off in ((0, 0), (1, PACK_HALF)):
                plsc.addupdate(out_v.at[tt, pl.ds(off + c * SC_LANES, SC_LANES)],
                               _tree_sum([p[half] for p in pairs]))

    tb = idx_v.shape[0]

    def block(bi, c):
        t0 = base + bi * tb
        pltpu.sync_copy(idx_hbm.at[pl.ds(t0, tb)], idx_v)
        pltpu.sync_copy(coef_hbm.at[pl.ds(t0, tb)], coef_v)

        def clear(i, cc):
            per_row = D_MODEL // SC_LANES
            out_v[i // per_row, pl.ds((i % per_row) * SC_LANES, SC_LANES)] = zero
            return cc
        lax.fori_loop(0, tb * (D_MODEL // SC_LANES), clear, 0)
        _sc_jobs(v_hbm, idx_v, vbuf, sem, compute)
        pltpu.sync_copy(out_v, out_hbm.at[pl.ds(t0, tb)])
        return c

    lax.fori_loop(0, n_tok // tb, block, 0)


def _peer_sc(body, idx, rows, table, out_width, name):
    t = idx.shape[0]
    assert t % SC_WORKERS == 0
    n_tok = t // SC_WORKERS
    tb = min(SC_TOKENS * (2 if body is _peer_u_body else 1), n_tok)
    assert n_tok % tb == 0 and tb * PEER_HEADS // SC_JOB_HEADS >= SC_SLOTS
    return pl.kernel(
        functools.partial(body, n_tok),
        out_type=jax.ShapeDtypeStruct((t, out_width), F32),
        mesh=_sc_mesh(),
        scratch_types=[pltpu.VMEM((tb, PEER_HK), I32),
                       pltpu.VMEM((tb, rows.shape[1]), rows.dtype),
                       pltpu.VMEM((tb, out_width), F32),
                       pltpu.VMEM((SC_SLOTS, SC_JOB_HEADS * PEER_TOPK, PACK_HALF), I32)]
                      + ([pltpu.VMEM((PEER_TOPK, SC_LANES), F32)] if body is _peer_u_body else [])
                      + [pltpu.SemaphoreType.DMA((SC_SLOTS,))],
        compiler_params=pltpu.CompilerParams(needs_layout_passes=False),
        name=name,
    )(idx, rows, table)


def _coef_words(pre, gates):
    return _pack_words(*(gates * _gelu(pre),) * 2)


def _coef_body(pre_ref, gate_ref, coef_ref):
    coef_ref[...] = _coef_words(pre_ref[...], gate_ref[...])


def _coef(pre, gates, tm):
    t = pre.shape[0]
    row = pl.BlockSpec((tm, PEER_HK), lambda i: (i, 0))
    return pl.pallas_call(_coef_body, grid=(t // tm,), in_specs=[row, row], out_specs=row,
                          out_shape=jax.ShapeDtypeStruct((t, PEER_HK), I32), name="coef")(pre, gates)


def _final_body(x1_ref, peer_ref, g2_ref, fng_ref, y_ref):
    x2 = x1_ref[...] + _mod_rows(g2_ref) * peer_ref[...]
    y_ref[...] = x2 * lax.rsqrt(jnp.mean(x2 * x2, axis=-1, keepdims=True) + EPS) * fng_ref[...]


def _final(x1, peer_out, mod, rows_per_batch, final_g, tm):
    t = x1.shape[0]
    row = pl.BlockSpec((tm, D_MODEL), lambda i: (i, 0))
    return pl.pallas_call(
        _final_body, grid=(t // tm,),
        in_specs=[row, row, _mod_spec(5, rows_per_batch, tm), _const_spec((1, D_MODEL))],
        out_specs=row, out_shape=jax.ShapeDtypeStruct((t, D_MODEL), F32), name="final",
    )(x1, peer_out, mod, final_g.reshape(1, -1))


def _expert_gather_v(g, coef, expert_v):
    g["peer_out"] = _peer_sc(_peer_v_body, g["idx"], coef, expert_v, D_MODEL, "peer_v")


def _front(x, mod, conv_buf, s0, pool_buf, start, chunk, tm, wts, prev, fin):
    x2d, row0, b, l = x
    t = b * l
    assert row0 % tm == 0
    if l >= tm:
        modx = mod.reshape(b, 6, 1, D_MODEL).transpose(1, 0, 2, 3)
    else:
        modx = jnp.repeat(mod.reshape(b, 6, D_MODEL), l, axis=0).transpose(1, 0, 2)
    outs = _inproj(x2d, row0, t, modx, l, wts["norm1_g"], wts["w_cat"], tm)
    lp = -(-l // chunk) * chunk
    proj = {}
    for (name, w), a in zip(_IN_BLOCKS, outs):
        a = a.reshape(b, l, w)
        proj[name] = a if lp == l else jnp.pad(a, ((0, 0), (0, lp - l), (0, 0)))
    mixed, nconv, ns, npool = _mixer(proj, conv_buf, s0, pool_buf, start, l, chunk,
                                     wts["conv_w"], wts["a_log"], wts["dt_bias"], wts["dn_norm_g"],
                                     wts["w_pool"], wts["pool_scale"])
    mixed2d = mixed[:, :l].reshape(t, D_MODEL)
    res = _post(mixed2d, x2d, row0, modx, l, wts["norm2_g"], wts["w_out"], wts["w_query"], wts["keys"], tm,
                prev=None if prev is None else (prev["pre"], prev["gates"]),
                fin=None if fin is None else (fin["x1"], fin["peer_out"], fin["mod"], fin["l"],
                                              wts["final_norm_g"]))
    x1, h2, idx, gates = res[:4]
    extra = list(res[4:])
    coef_prev = extra.pop(0) if prev is not None else None
    y_fin = extra.pop(0).reshape(fin["b"], fin["l"], D_MODEL) if fin is not None else None
    pre = _peer_sc(_peer_u_body, idx, h2, wts["expert_u"], PEER_HK, "peer_u")
    g = dict(x1=x1, idx=idx, gates=gates, pre=pre, mod=modx, b=b, l=l, tm=tm,
             states=(nconv, ns, npool))
    return g, coef_prev, y_fin


def kernel(x_prompt, x_sample, c_prompt, c_sample, state_conv, state_delta, state_pool, w_ada, b_ada, norm1_g, w_in, conv_w, a_log, dt_bias, dn_norm_g, w_pool, pool_scale, w_out, norm2_g, w_query, sub_keys, expert_u, expert_v, final_norm_g):
    bp = x_prompt.shape[0]
    yp, ys = x_prompt, x_sample
    conv_p, delta_p, pool_p, conv_s, delta_s, pool_s = [], [], [], [], [], []
    zero_conv = jnp.zeros((bp, CONV_WIDTH - 1, QKV_WIDTH), F32)
    zero_delta = jnp.zeros((bp, DN_HEADS, DN_HEAD_DIM, DN_HEAD_DIM), F32)
    zero_pool = jnp.zeros((bp, POOL_BUF, POOL_WIDTH), F32)
    c_all = jnp.concatenate([c_prompt, c_sample], axis=0)
    for layer in range(DEPTH):
        wi = w_in[layer]
        o_b = QKV_WIDTH
        o_z = o_b + 2 * DN_HEADS
        w_ba = jnp.pad(wi[:, o_b:o_z], ((0, 0), (0, LANES - 2 * DN_HEADS)))
        w_cat = jnp.concatenate([wi[:, :o_b], wi[:, o_z:], w_ba], axis=1).astype(BF16)
        last = layer == DEPTH - 1
        wts = dict(
            norm1_g=norm1_g[layer], w_cat=w_cat, conv_w=conv_w[layer], a_log=a_log[layer],
            dt_bias=dt_bias[layer], dn_norm_g=dn_norm_g[layer], w_pool=w_pool[layer],
            pool_scale=pool_scale[layer], w_out=w_out[layer].astype(BF16), norm2_g=norm2_g[layer],
            w_query=w_query[layer].astype(BF16),
            keys=sub_keys[layer].reshape(2 * PEER_HEADS, PEER_NKEYS, PEER_KEY_HALF).astype(BF16),
            expert_u=_pack_table(expert_u[layer]), expert_v=_pack_table(expert_v[layer]),
            final_norm_g=final_norm_g if last else jnp.ones_like(final_norm_g))
        mod = _ada(c_all, w_ada[layer], b_ada[layer])
        assert last, "final norm is fused into the expert stage"
        step = bp // PROMPT_PARTS
        seq = x_prompt.shape[1]
        assert step == 1
        xp2d = yp.reshape(bp * seq, D_MODEL)
        zeros = (zero_conv[:step], zero_delta[:step], zero_pool[:step])
        jobs, cuts = [], []
        for b0 in range(0, bp, step):
            n = EDGE_SPLITS if b0 in (0, bp - step) else 1
            cuts.append(n)
            for s0 in range(0, seq, seq // n):
                jobs.append(((xp2d, b0 * seq + s0, step, seq // n), mod[b0:b0 + step],
                             zeros if s0 == 0 else None, s0, DN_CHUNK))
        jobs.append(((ys.reshape(-1, D_MODEL), 0) + ys.shape[:2], mod[bp:], (state_conv[layer], state_delta[layer], state_pool[layer]),
                     PAST_LEN, SUBLANES))
        groups = []
        for j, (xg, mg, states, start, chunk) in enumerate(jobs):
            pi = j - (1 if j <= RAMP_PARTS else COEF_LAG)
            prev = groups[pi] if pi >= 0 and "peer_out" not in groups[pi] else None
            fin = groups[j - FIN_LAG] if j >= FIN_LAG and "peer_out" in groups[j - FIN_LAG] else None
            if fin is not None and fin["x1"].shape[0] % (xg[2] * xg[3] // ROW_TILE):
                fin = None
            if states is None:
                states = groups[j - 1]["states"]
            g, coef_prev, y_fin = _front(xg, mg, *states, start, chunk, ROW_TILE, wts, prev, fin)
            if prev is not None:
                _expert_gather_v(prev, coef_prev, wts["expert_v"])
            if fin is not None:
                fin["y"] = y_fin
            groups.append(g)
        for g in groups:
            if "peer_out" not in g:
                _expert_gather_v(g, _coef(g["pre"], g["gates"], ROW_TILE), wts["expert_v"])
        for g in groups:
            if "y" not in g:
                g["y"] = _final(g["x1"], g["peer_out"], g["mod"], g["l"], wts["final_norm_g"],
                                g["tm"]).reshape(g["b"], g["l"], D_MODEL)
        rows, at = [], 0
        for n in cuts:
            rows.append(groups[at:at + n])
            at += n
        yp = jnp.concatenate([jnp.concatenate([g["y"] for g in row], axis=1) for row in rows], axis=0)
        cp, sp, pp = (jnp.concatenate(a, axis=0) for a in zip(*(row[-1]["states"] for row in rows)))
        ys = groups[-1]["y"]
        cs, ss, ps = groups[-1]["states"]
        conv_p.append(cp)
        delta_p.append(sp)
        pool_p.append(pp)
        conv_s.append(cs)
        delta_s.append(ss)
        pool_s.append(ps)
    return (yp, ys, jnp.stack(conv_p), jnp.stack(delta_p), jnp.stack(pool_p),
            jnp.stack(conv_s), jnp.stack(delta_s), jnp.stack(pool_s))
```
